```python
import jax, jax.numpy as jnp
from jax import lax
import numpy as np

D_MODEL = 1024
BATCH = 8
SEQ = 4096
DEPTH = 1

CHUNK = 64
HEAD_DIM = 64
FOX_HEADS = 8
FOX_WIDTH = FOX_HEADS * HEAD_DIM
SGU_GROUPS = 8
SGU_GROUP_DIM = 64
SGU_WIDTH = SGU_GROUPS * SGU_GROUP_DIM
SGU_LEN = 128
Q_BLOCK = 128
N_BRANCH = 2
D_FF = 4 * D_MODEL
IN_COLS = 3 * FOX_WIDTH + FOX_HEADS + 2 * SGU_WIDTH + N_BRANCH * D_MODEL
EPS = 1e-6

kernel_name = "hybrid_fox_gmlp_gated_block"


def rmsnorm(x, g):
    xf = x.astype(jnp.float32)
    y = xf * lax.rsqrt(jnp.mean(xf * xf, axis=-1, keepdims=True) + EPS)
    return (y * g.astype(jnp.float32)).astype(x.dtype)


def layernorm(x, g, b):
    xf = x.astype(jnp.float32)
    mu = jnp.mean(xf, axis=-1, keepdims=True)
    xc = xf - mu
    y = xc * lax.rsqrt(jnp.mean(xc * xc, axis=-1, keepdims=True) + EPS)
    return (y * g.astype(jnp.float32) + b.astype(jnp.float32)).astype(x.dtype)


def forgetting_attention(q, k, v, logf):
    b, s, h, dh = q.shape
    c = jnp.cumsum(logf, axis=1)
    c_bh = jnp.transpose(c, (0, 2, 1))
    scale = dh ** -0.5
    outs = []
    for i in range(s // Q_BLOCK):
        s0 = i * Q_BLOCK
        e = s0 + Q_BLOCK
        logits = jnp.einsum('bqhd,bkhd->bhqk', q[:, s0:e], k[:, :e]).astype(jnp.float32) * scale
        logits = logits + c_bh[:, :, s0:e, None] - c_bh[:, :, None, :e]
        qpos = s0 + jnp.arange(Q_BLOCK)
        mask = qpos[:, None] >= jnp.arange(e)[None, :]
        logits = jnp.where(mask[None, None], logits, -jnp.inf)
        p = jax.nn.softmax(logits, axis=-1)
        outs.append(jnp.einsum('bhqk,bkhd->bqhd', p.astype(v.dtype), v[:, :e]))
    return jnp.concatenate(outs, axis=1)


def spatial_gating(u, v, ln_g, ln_b, w_s, b_s):
    b, s, _ = v.shape
    v = layernorm(v, ln_g, ln_b)
    vc = v.reshape(b, s // SGU_LEN, SGU_LEN, SGU_GROUPS, SGU_GROUP_DIM)
    chunk_id = jnp.arange(SGU_LEN) // CHUNK
    mask = chunk_id[None, :] <= chunk_id[:, None]
    w = jnp.where(mask[None], w_s, 0)
    mixed = jnp.einsum('gij,bnjgc->bnigc', w, vc) + jnp.transpose(b_s)[None, None, :, :, None]
    return u * mixed.reshape(b, s, SGU_WIDTH)


def _fwd_setup_inputs(seed: int = 0) -> dict:
    key = jax.random.key(seed)
    ks = jax.random.split(key, 16)
    f32 = jnp.float32
    nrm = lambda k, shape, fan_in: jax.random.normal(k, shape, f32) * (fan_in ** -0.5)
    return {
        "x": jax.random.normal(ks[0], (BATCH, SEQ, D_MODEL), f32),
        "norm1_g": 1.0 + 0.02 * jax.random.normal(ks[1], (DEPTH, D_MODEL), f32),
        "w_in": nrm(ks[2], (DEPTH, D_MODEL, IN_COLS), D_MODEL),
        "b_f": jax.random.uniform(ks[3], (DEPTH, FOX_HEADS), f32, 1.0, 3.0),
        "ln_v_g": 1.0 + 0.02 * jax.random.normal(ks[4], (DEPTH, SGU_WIDTH), f32),
        "ln_v_b": 0.02 * jax.random.normal(ks[5], (DEPTH, SGU_WIDTH), f32),
        "w_sgu": nrm(ks[6], (DEPTH, SGU_GROUPS, SGU_LEN, SGU_LEN), SGU_LEN),
        "b_sgu": 1.0 + 0.1 * jax.random.normal(ks[7], (DEPTH, SGU_GROUPS, SGU_LEN), f32),
        "w_a": nrm(ks[8], (DEPTH, FOX_WIDTH, D_MODEL), FOX_WIDTH),
        "w_b": nrm(ks[9], (DEPTH, SGU_WIDTH, D_MODEL), SGU_WIDTH),
        "w_o": nrm(ks[10], (DEPTH, D_MODEL, D_MODEL), D_MODEL),
        "norm2_g": 1.0 + 0.02 * jax.random.normal(ks[11], (DEPTH, D_MODEL), f32),
        "w_up": nrm(ks[12], (DEPTH, D_MODEL, D_FF), D_MODEL),
        "w_down": nrm(ks[13], (DEPTH, D_FF, D_MODEL), D_FF),
        "normf_g": 1.0 + 0.02 * jax.random.normal(ks[14], (D_MODEL,), f32),
    }


def _fwd_reference(x, norm1_g, w_in, b_f, ln_v_g, ln_v_b, w_sgu, b_sgu, w_a, w_b, w_o,
              norm2_g, w_up, w_down, normf_g):
    bsz, seq, _ = x.shape
    splits = np.cumsum([FOX_WIDTH, FOX_WIDTH, FOX_WIDTH, FOX_HEADS, SGU_WIDTH, SGU_WIDTH])
    for l in range(DEPTH):
        h = rmsnorm(x, norm1_g[l])
        p = h @ w_in[l]
        q, k, v, f_logit, u, sv, gates = jnp.split(p, splits, axis=-1)
        heads = (bsz, seq, FOX_HEADS, HEAD_DIM)
        logf = jax.nn.log_sigmoid((f_logit + b_f[l]).astype(jnp.float32))
        y_a = forgetting_attention(q.reshape(heads), k.reshape(heads), v.reshape(heads), logf)
        y_a = y_a.reshape(bsz, seq, FOX_WIDTH) @ w_a[l]
        y_b = spatial_gating(jax.nn.gelu(u), jax.nn.gelu(sv), ln_v_g[l], ln_v_b[l],
                             w_sgu[l], b_sgu[l]) @ w_b[l]
        g_a, g_b = jnp.split(gates, N_BRANCH, axis=-1)
        merged = jax.nn.sigmoid(g_a) * y_a + jax.nn.sigmoid(g_b) * y_b
        x = x + merged @ w_o[l]
        h2 = rmsnorm(x, norm2_g[l])
        x = x + jnp.square(jax.nn.relu(h2 @ w_up[l])) @ w_down[l]
    return rmsnorm(x, normf_g)


import jax as _jax
import jax.numpy as _jnp

TWIN_FORMAT = 'train_step'
FWD_PARAMS = ['x', 'norm1_g', 'w_in', 'b_f', 'ln_v_g', 'ln_v_b', 'w_sgu', 'b_sgu', 'w_a', 'w_b', 'w_o', 'norm2_g', 'w_up', 'w_down', 'normf_g']
TWIN_WEIGHTS = ['norm1_g', 'w_in', 'b_f', 'ln_v_g', 'ln_v_b', 'w_sgu', 'b_sgu', 'w_a', 'w_b', 'w_o', 'norm2_g', 'w_up', 'w_down', 'normf_g']
TWIN_DIFF_INPUT = 'x'
TWIN_INPUTS = ['x', 'norm1_g', 'w_in', 'b_f', 'ln_v_g', 'ln_v_b', 'w_sgu', 'b_sgu', 'w_a', 'w_b', 'w_o', 'norm2_g', 'w_up', 'w_down', 'normf_g', 'loss_target', 'm_norm1_g', 'm_w_in', 'm_b_f', 'm_ln_v_g', 'm_ln_v_b', 'm_w_sgu', 'm_b_sgu', 'm_w_a', 'm_w_b', 'm_w_o', 'm_norm2_g', 'm_w_up', 'm_w_down', 'm_normf_g', 'v_norm1_g', 'v_w_in', 'v_b_f', 'v_ln_v_g', 'v_ln_v_b', 'v_w_sgu', 'v_b_sgu', 'v_w_a', 'v_w_b', 'v_w_o', 'v_norm2_g', 'v_w_up', 'v_w_down', 'v_normf_g']
TWIN_OUTPUTS = ['loss', 'grad_x', 'grad_norm1_g', 'grad_w_in', 'grad_b_f', 'grad_ln_v_g', 'grad_ln_v_b', 'grad_w_sgu', 'grad_b_sgu', 'grad_w_a', 'grad_w_b', 'grad_w_o', 'grad_norm2_g', 'grad_w_up', 'grad_w_down', 'grad_normf_g', 'delta_norm1_g', 'delta_w_in', 'delta_b_f', 'delta_ln_v_g', 'delta_ln_v_b', 'delta_w_sgu', 'delta_b_sgu', 'delta_w_a', 'delta_w_b', 'delta_w_o', 'delta_norm2_g', 'delta_w_up', 'delta_w_down', 'delta_normf_g', 'new_m_norm1_g', 'new_m_w_in', 'new_m_b_f', 'new_m_ln_v_g', 'new_m_ln_v_b', 'new_m_w_sgu', 'new_m_b_sgu', 'new_m_w_a', 'new_m_w_b', 'new_m_w_o', 'new_m_norm2_g', 'new_m_w_up', 'new_m_w_down', 'new_m_normf_g', 'new_v_norm1_g', 'new_v_w_in', 'new_v_b_f', 'new_v_ln_v_g', 'new_v_ln_v_b', 'new_v_w_sgu', 'new_v_b_sgu', 'new_v_w_a', 'new_v_w_b', 'new_v_w_o', 'new_v_norm2_g', 'new_v_w_up', 'new_v_w_down', 'new_v_normf_g']
TWIN_LEAF_KINDS = {'loss': 'loss', 'grad_x': 'grad_x', 'grad_norm1_g': 'grad_w', 'grad_w_in': 'grad_w', 'grad_b_f': 'grad_w', 'grad_ln_v_g': 'grad_w', 'grad_ln_v_b': 'grad_w', 'grad_w_sgu': 'grad_w', 'grad_b_sgu': 'grad_w', 'grad_w_a': 'grad_w', 'grad_w_b': 'grad_w', 'grad_w_o': 'grad_w', 'grad_norm2_g': 'grad_w', 'grad_w_up': 'grad_w', 'grad_w_down': 'grad_w', 'grad_normf_g': 'grad_w', 'delta_norm1_g': 'delta_w', 'delta_w_in': 'delta_w', 'delta_b_f': 'delta_w', 'delta_ln_v_g': 'delta_w', 'delta_ln_v_b': 'delta_w', 'delta_w_sgu': 'delta_w', 'delta_b_sgu': 'delta_w', 'delta_w_a': 'delta_w', 'delta_w_b': 'delta_w', 'delta_w_o': 'delta_w', 'delta_norm2_g': 'delta_w', 'delta_w_up': 'delta_w', 'delta_w_down': 'delta_w', 'delta_normf_g': 'delta_w', 'new_m_norm1_g': 'new_m', 'new_m_w_in': 'new_m', 'new_m_b_f': 'new_m', 'new_m_ln_v_g': 'new_m', 'new_m_ln_v_b': 'new_m', 'new_m_w_sgu': 'new_m', 'new_m_b_sgu': 'new_m', 'new_m_w_a': 'new_m', 'new_m_w_b': 'new_m', 'new_m_w_o': 'new_m', 'new_m_norm2_g': 'new_m', 'new_m_w_up': 'new_m', 'new_m_w_down': 'new_m', 'new_m_normf_g': 'new_m', 'new_v_norm1_g': 'new_v', 'new_v_w_in': 'new_v', 'new_v_b_f': 'new_v', 'new_v_ln_v_g': 'new_v', 'new_v_ln_v_b': 'new_v', 'new_v_w_sgu': 'new_v', 'new_v_b_sgu': 'new_v', 'new_v_w_a': 'new_v', 'new_v_w_b': 'new_v', 'new_v_w_o': 'new_v', 'new_v_norm2_g': 'new_v', 'new_v_w_up': 'new_v', 'new_v_w_down': 'new_v', 'new_v_normf_g': 'new_v'}


def _forward(args):
    return _fwd_reference(*[args[k] for k in FWD_PARAMS])


def _output_shape():
    def fwd():
        inp = _fwd_setup_inputs(0)
        return _fwd_reference(*[inp[k] for k in FWD_PARAMS])
    out = _jax.eval_shape(fwd)
    return out.shape, out.dtype

N_MICROBATCH = 1
ADAM_LR = 0.001
ADAM_B1 = 0.9
ADAM_B2 = 0.999
ADAM_EPS = 1e-08
ADAM_WD = 0.01
ADAM_STEP = 10
PER_EXAMPLE_BATCH_AXIS = {'x': 0, 'loss_target': 0}
SHARED_INPUTS = []
_WEIGHT_DTYPES = {'norm1_g': _jnp.float32, 'w_in': _jnp.float32, 'b_f': _jnp.float32, 'ln_v_g': _jnp.float32, 'ln_v_b': _jnp.float32, 'w_sgu': _jnp.float32, 'b_sgu': _jnp.float32, 'w_a': _jnp.float32, 'w_b': _jnp.float32, 'w_o': _jnp.float32, 'norm2_g': _jnp.float32, 'w_up': _jnp.float32, 'w_down': _jnp.float32, 'normf_g': _jnp.float32}
MOMENT_SCALE = {'norm1_g': 1.215250e-01, 'w_in': 5.756975e-02, 'b_f': 2.495642e-01, 'ln_v_g': 8.289670e-02, 'ln_v_b': 7.128101e-02, 'w_sgu': 5.230613e-02, 'b_sgu': 6.308500e-02, 'w_a': 3.935055e-02, 'w_b': 8.464360e-02, 'w_o': 9.250382e-02, 'norm2_g': 1.551525e-01, 'w_up': 7.622641e-02, 'w_down': 1.557566e-01, 'normf_g': 3.229380e+01}


def _to_microbatches(a, axis):
    t = _jnp.moveaxis(a, axis, 0)
    t = t.reshape((N_MICROBATCH, t.shape[0] // N_MICROBATCH) + t.shape[1:])
    return _jnp.moveaxis(t, 1, axis + 1)


def setup_inputs(seed: int = 0) -> dict:
    inp = _fwd_setup_inputs(seed)
    key = _jax.random.fold_in(_jax.random.key(seed), 7919)
    shape, _ = _output_shape()
    out = dict(inp)
    out["loss_target"] = _jax.random.normal(_jax.random.fold_in(key, 0), shape, _jnp.float32)
    for i, name in enumerate(TWIN_WEIGHTS):
        w = inp[name].astype(_jnp.float32)
        if MOMENT_SCALE is None:
            s = _jnp.sqrt(_jnp.mean(_jnp.square(w)) + 1e-30)
        else:
            s = MOMENT_SCALE[name]
        km, kv = _jax.random.split(_jax.random.fold_in(key, i + 1))
        out[name] = w
        out["m_" + name] = s * _jax.random.normal(km, w.shape, _jnp.float32)
        out["v_" + name] = (s * s) * _jax.random.uniform(kv, w.shape, _jnp.float32, 0.5, 1.5)
    if N_MICROBATCH > 1:
        for name, axis in PER_EXAMPLE_BATCH_AXIS.items():
            out[name] = _to_microbatches(out[name], axis)
    return {'x': out['x'], 'norm1_g': out['norm1_g'], 'w_in': out['w_in'], 'b_f': out['b_f'], 'ln_v_g': out['ln_v_g'], 'ln_v_b': out['ln_v_b'], 'w_sgu': out['w_sgu'], 'b_sgu': out['b_sgu'], 'w_a': out['w_a'], 'w_b': out['w_b'], 'w_o': out['w_o'], 'norm2_g': out['norm2_g'], 'w_up': out['w_up'], 'w_down': out['w_down'], 'normf_g': out['normf_g'], 'loss_target': out['loss_target'], 'm_norm1_g': out['m_norm1_g'], 'm_w_in': out['m_w_in'], 'm_b_f': out['m_b_f'], 'm_ln_v_g': out['m_ln_v_g'], 'm_ln_v_b': out['m_ln_v_b'], 'm_w_sgu': out['m_w_sgu'], 'm_b_sgu': out['m_b_sgu'], 'm_w_a': out['m_w_a'], 'm_w_b': out['m_w_b'], 'm_w_o': out['m_w_o'], 'm_norm2_g': out['m_norm2_g'], 'm_w_up': out['m_w_up'], 'm_w_down': out['m_w_down'], 'm_normf_g': out['m_normf_g'], 'v_norm1_g': out['v_norm1_g'], 'v_w_in': out['v_w_in'], 'v_b_f': out['v_b_f'], 'v_ln_v_g': out['v_ln_v_g'], 'v_ln_v_b': out['v_ln_v_b'], 'v_w_sgu': out['v_w_sgu'], 'v_b_sgu': out['v_b_sgu'], 'v_w_a': out['v_w_a'], 'v_w_b': out['v_w_b'], 'v_w_o': out['v_w_o'], 'v_norm2_g': out['v_norm2_g'], 'v_w_up': out['v_w_up'], 'v_w_down': out['v_w_down'], 'v_normf_g': out['v_normf_g']}


def _loss(weights, diff, rest, loss_target):
    with _jax.named_scope("forward"):
        args = {**rest, TWIN_DIFF_INPUT: diff, **{k: w.astype(_WEIGHT_DTYPES[k]) for k, w in weights.items()}}
        y = _forward(args)
    with _jax.named_scope("loss_head"):
        err = _jnp.square(y.astype(_jnp.float32) - loss_target)
        return 0.5 * _jnp.sum(_jnp.mean(err, axis=-1)) if err.ndim else 0.5 * err


def _adamw(w, g, m, v):
    m = ADAM_B1 * m + (1.0 - ADAM_B1) * g
    v = ADAM_B2 * v + (1.0 - ADAM_B2) * _jnp.square(g)
    m_hat = m / (1.0 - ADAM_B1 ** ADAM_STEP)
    v_hat = v / (1.0 - ADAM_B2 ** ADAM_STEP)
    delta = -ADAM_LR * (m_hat / (_jnp.sqrt(v_hat) + ADAM_EPS) + ADAM_WD * w)
    return delta, m, v


def reference(x, norm1_g, w_in, b_f, ln_v_g, ln_v_b, w_sgu, b_sgu, w_a, w_b, w_o, norm2_g, w_up, w_down, normf_g, loss_target, m_norm1_g, m_w_in, m_b_f, m_ln_v_g, m_ln_v_b, m_w_sgu, m_b_sgu, m_w_a, m_w_b, m_w_o, m_norm2_g, m_w_up, m_w_down, m_normf_g, v_norm1_g, v_w_in, v_b_f, v_ln_v_g, v_ln_v_b, v_w_sgu, v_b_sgu, v_w_a, v_w_b, v_w_o, v_norm2_g, v_w_up, v_w_down, v_normf_g):
    given = dict(x=x, norm1_g=norm1_g, w_in=w_in, b_f=b_f, ln_v_g=ln_v_g, ln_v_b=ln_v_b, w_sgu=w_sgu, b_sgu=b_sgu, w_a=w_a, w_b=w_b, w_o=w_o, norm2_g=norm2_g, w_up=w_up, w_down=w_down, normf_g=normf_g, loss_target=loss_target, m_norm1_g=m_norm1_g, m_w_in=m_w_in, m_b_f=m_b_f, m_ln_v_g=m_ln_v_g, m_ln_v_b=m_ln_v_b, m_w_sgu=m_w_sgu, m_b_sgu=m_b_sgu, m_w_a=m_w_a, m_w_b=m_w_b, m_w_o=m_w_o, m_norm2_g=m_norm2_g, m_w_up=m_w_up, m_w_down=m_w_down, m_normf_g=m_normf_g, v_norm1_g=v_norm1_g, v_w_in=v_w_in, v_b_f=v_b_f, v_ln_v_g=v_ln_v_g, v_ln_v_b=v_ln_v_b, v_w_sgu=v_w_sgu, v_b_sgu=v_b_sgu, v_w_a=v_w_a, v_w_b=v_w_b, v_w_o=v_w_o, v_norm2_g=v_norm2_g, v_w_up=v_w_up, v_w_down=v_w_down, v_normf_g=v_normf_g)
    weights = {n: given[n] for n in TWIN_WEIGHTS}
    shared = {n: given[n] for n in SHARED_INPUTS}
    per_example = {n: given[n] for n in ['x']}
    grad_fn = _jax.value_and_grad(_loss, argnums=(0, 1))

    def one_microbatch(ex, loss_target):
        ex = dict(ex)
        diff = ex.pop(TWIN_DIFF_INPUT)
        return grad_fn(weights, diff, {**shared, **ex}, loss_target)

    if N_MICROBATCH == 1:
        loss, (grad_w, grad_x) = one_microbatch(per_example, given["loss_target"])
    else:
        def body(carry, xs):
            loss_sum, grad_sum = carry
            l_k, (gw_k, gx_k) = one_microbatch(xs[0], xs[1])
            with _jax.named_scope("update"):
                return (loss_sum + l_k, _jax.tree.map(_jnp.add, grad_sum, gw_k)), gx_k

        init = (_jnp.zeros((), _jnp.float32), _jax.tree.map(_jnp.zeros_like, weights))
        (loss, grad_w), grad_x = _jax.lax.scan(body, init, (per_example, given["loss_target"]))
    with _jax.named_scope("update"):
        delta_w, new_m, new_v = {}, {}, {}
        for n in TWIN_WEIGHTS:
            delta_w[n], new_m[n], new_v[n] = _adamw(weights[n], grad_w[n], given["m_" + n], given["v_" + n])
    return (loss, grad_x, *[grad_w[n] for n in TWIN_WEIGHTS], *[delta_w[n] for n in TWIN_WEIGHTS],
            *[new_m[n] for n in TWIN_WEIGHTS], *[new_v[n] for n in TWIN_WEIGHTS])
```

```python
import math

import jax
import jax.numpy as jnp
from jax import lax
from jax.experimental import pallas as pl
from jax.experimental.pallas import tpu as pltpu

F32 = jnp.float32
BF16 = jnp.bfloat16

N_DEV = 8
D_MODEL = 1024
N_HEADS = 8
HEAD_DIM = 64
FOX_W = N_HEADS * HEAD_DIM
SGU_G = 8
SGU_W = 512
SGU_LEN = 128
CHUNK = 64
D_FF = 4 * D_MODEL
IN_COLS = 3 * FOX_W + N_HEADS + 2 * SGU_W + 2 * D_MODEL
IN_SHARD = IN_COLS // N_DEV
LANES = 128
QKV_W = 3 * FOX_W
U_OFF, SV_OFF, GA_OFF, GB_OFF, F_OFF = 0, 512, 1024, 2048, 3072
REST_W = F_OFF + LANES
EPS = 1e-6
NEG = -1e30

ADAM_LR = 0.001
ADAM_B1 = 0.9
ADAM_B2 = 0.999
ADAM_EPS = 1e-08
ADAM_WD = 0.01
ADAM_STEP = 10

VMEM_LIMIT = 56 * 1024 * 1024
MESH = pl.DeviceIdType.MESH


def _params(sem=None):
    return pltpu.CompilerParams(dimension_semantics=sem, vmem_limit_bytes=VMEM_LIMIT)


def _dot(a, b):
    return jnp.dot(a, b, preferred_element_type=F32)


def _dot_nt(a, b):
    return lax.dot_general(a, b, (((1,), (1,)), ((), ())), preferred_element_type=F32)


def _dot_tn(a, b):
    return lax.dot_general(a, b, (((0,), (0,)), ((), ())), preferred_element_type=F32)


def _dot_f32(a, b):
    return jnp.dot(a, b, preferred_element_type=F32, precision=lax.Precision.HIGHEST)


def _sigmoid(x):
    return 1.0 / (1.0 + jnp.exp(-x))


def _log_sigmoid(z):
    return jnp.minimum(z, 0.0) - jnp.log(1.0 + jnp.exp(-jnp.abs(z)))


_GELU_K = math.sqrt(2.0 / math.pi)
_GELU_C = 0.044715


def _gelu(x):
    t = jnp.tanh(_GELU_K * (x + _GELU_C * (x * x * x)))
    return 0.5 * x * (1.0 + t)


def _gelu_grad(x):
    x2 = x * x
    t = jnp.tanh(_GELU_K * (x + _GELU_C * (x2 * x)))
    return 0.5 * (1.0 + t) + 0.5 * x * (1.0 - t * t) * (_GELU_K * (1.0 + 3.0 * _GELU_C * x2))


def _rms_bwd(xh, r, g, dy):
    gy = dy * g
    return r * (gy - xh * jnp.mean(xh * gy, axis=-1, keepdims=True))


def _lane_lt64(shape):
    return lax.broadcasted_iota(jnp.int32, shape, len(shape) - 1) < HEAD_DIM


def _all_gather(name, shard):
    def body(x_ref, out_ref, send_sems, recv_sems, local_sem):
        x, y, c = lax.axis_index("x"), lax.axis_index("y"), lax.axis_index("c")
        me, sibling = (x, y, c), (x, y, 1 - c)
        chips = [(1 - x, y), (x, 1 - y), (1 - x, 1 - y)]

        def rows(px, py, pc):
            return out_ref.at[4 * px + 2 * py + pc]

        def copy(k, block, to, src=None):
            return pltpu.make_async_remote_copy(
                src_ref=rows(*block) if src is None else src,
                dst_ref=rows(*block),
                send_sem=send_sems.at[k],
                recv_sem=recv_sems.at[k],
                device_id=to,
                device_id_type=MESH,
            )

        mine = pltpu.make_async_copy(x_ref, rows(*me), local_sem)
        mine.start()
        first = [copy(0, me, sibling, src=x_ref)]
        first += [copy(1 + j, me, (*chip, c), src=x_ref) for j, chip in enumerate(chips)]
        for cp in first:
            cp.start()
        passed = [copy(4 + j, (*chip, c), sibling) for j, chip in enumerate(chips)]
        for j, chip in enumerate(chips):
            copy(1 + j, (*chip, c), me).wait_recv()
            passed[j].start()
        copy(0, sibling, me).wait_recv()
        for j, chip in enumerate(chips):
            copy(4 + j, (*chip, 1 - c), me).wait_recv()
        for cp in first + passed:
            cp.wait_send()
        mine.wait()

    return pl.pallas_call(
        body,
        name=name,
        out_shape=jax.ShapeDtypeStruct((N_DEV,) + shard.shape, shard.dtype),
        in_specs=[pl.BlockSpec(memory_space=pl.ANY)],
        out_specs=pl.BlockSpec(memory_space=pl.ANY),
        scratch_shapes=[pltpu.SemaphoreType.DMA((7,)), pltpu.SemaphoreType.DMA((7,)), pltpu.SemaphoreType.DMA],
    )(shard)


def _pair_exchange(name, grads):
    n = len(grads)

    def body(*refs):
        g_refs, out_refs, send_sems, recv_sems = refs[:n], refs[n:2 * n], refs[2 * n], refs[2 * n + 1]
        x, y, c = lax.axis_index("x"), lax.axis_index("y"), lax.axis_index("c")
        copies = [
            pltpu.make_async_remote_copy(
                src_ref=g_refs[k].at[:, 1 - c],
                dst_ref=out_refs[k],
                send_sem=send_sems.at[k],
                recv_sem=recv_sems.at[k],
                device_id=(x, y, 1 - c),
                device_id_type=MESH,
            )
            for k in range(n)
        ]
        for cp in copies:
            cp.start()
        for cp in copies:
            cp.wait_recv()
        for cp in copies:
            cp.wait_send()

    return pl.pallas_call(
        body,
        name=name,
        out_shape=[jax.ShapeDtypeStruct((4,) + g.shape[2:], g.dtype) for g in grads],
        in_specs=[pl.BlockSpec(memory_space=pl.ANY)] * n,
        out_specs=[pl.BlockSpec(memory_space=pl.ANY)] * n,
        scratch_shapes=[pltpu.SemaphoreType.DMA((n,)), pltpu.SemaphoreType.DMA((n,))],
    )(*grads)


def _chip_exchange(name, parts):
    n = len(parts)

    def body(*refs):
        p_refs, out_refs, send_sems, recv_sems = refs[:n], refs[n:2 * n], refs[2 * n], refs[2 * n + 1]
        x, y, c = lax.axis_index("x"), lax.axis_index("y"), lax.axis_index("c")
        chips = [(1 - x, y), (x, 1 - y), (1 - x, 1 - y)]
        copies = []
        for k in range(n):
            for j, (px, py) in enumerate(chips):
                copies.append(pltpu.make_async_remote_copy(
                    src_ref=p_refs[k].at[2 * px + py],
                    dst_ref=out_refs[k].at[j],
                    send_sem=send_sems.at[3 * k + j],
                    recv_sem=recv_sems.at[3 * k + j],
                    device_id=(px, py, c),
                    device_id_type=MESH,
                ))
        for cp in copies:
            cp.start()
        for cp in copies:
            cp.wait_recv()
        for cp in copies:
            cp.wait_send()

    return pl.pallas_call(
        body,
        name=name,
        out_shape=[jax.ShapeDtypeStruct((3,) + p.shape[1:], p.dtype) for p in parts],
        in_specs=[pl.BlockSpec(memory_space=pl.ANY)] * n,
        out_specs=[pl.BlockSpec(memory_space=pl.ANY)] * n,
        scratch_shapes=[pltpu.SemaphoreType.DMA((3 * n,)), pltpu.SemaphoreType.DMA((3 * n,))],
    )(*parts)


def _mm(name, pairs, extras, outs, epi, *, m, tm, n, tn, arbitrary=False):
    nj = n // tn
    a_arrays, a_specs, b_arrays, b_specs, b_index = [], [], [], [], []
    for a, b, nt, cols in pairs:
        a_arrays.append(a)
        a_specs.append(pl.BlockSpec((tm, a.shape[1]), lambda i, j: (i, 0)))
        known = [k for k, other in enumerate(b_arrays) if other is b]
        if known:
            b_index.append(known[0])
            continue
        b_index.append(len(b_arrays))
        b_arrays.append(b)
        if cols is not None:
            assert nj == 1
            b_specs.append(pl.BlockSpec(b.shape, lambda i, j: (0, 0)))
        elif nt:
            b_specs.append(pl.BlockSpec((tn, b.shape[1]), lambda i, j: (j, 0)))
        else:
            b_specs.append(pl.BlockSpec((b.shape[0], tn), lambda i, j: (0, j)))
    arrays = a_arrays + b_arrays + [arr for arr, _ in extras]
    in_specs = a_specs + b_specs + [spec for _, spec in extras]
    n_a, n_b, n_extras = len(a_arrays), len(b_arrays), len(extras)

    def body(*refs):
        a_refs = refs[:n_a]
        b_refs = refs[n_a:n_a + n_b]
        ex = refs[n_a + n_b:n_a + n_b + n_extras]
        out = refs[n_a + n_b + n_extras:]
        accs = []
        for p, (_, _, nt, cols) in enumerate(pairs):
            av = a_refs[p][...]
            if av.dtype != BF16:
                av = av.astype(BF16)
            b_ref = b_refs[b_index[p]]
            bv = b_ref[...] if cols is None else b_ref[:, cols[0]:cols[1]]
            accs.append(_dot_nt(av, bv) if nt else _dot(av, bv))
        epi(accs, ex, out)

    sem = ("arbitrary", "arbitrary") if arbitrary else ("parallel", "parallel")
    return pl.pallas_call(
        body,
        name=name,
        grid=(m // tm, nj),
        in_specs=in_specs,
        out_specs=[spec for _, spec in outs],
        out_shape=[shape for shape, _ in outs],
        compiler_params=_params(sem),
    )(*arrays)


def _tile(tm, tn, off=0):
    return pl.BlockSpec((tm, tn), lambda i, j: (i, j + off))


def _row(tm, w, blk=0):
    return pl.BlockSpec((tm, w), lambda i, j: (i, blk))


def _whole(shape):
    zeros = (0,) * len(shape)
    return pl.BlockSpec(shape, lambda i, j: zeros)


def _sds(shape, dtype):
    return jax.ShapeDtypeStruct(shape, dtype)


def _mm_tn(name, a, g, *, tk, tn, ts, blocked=False):
    s_len, ka = a.shape
    n = g.shape[1]

    def body(a_ref, g_ref, o_ref):
        part = _dot_tn(a_ref[...].astype(BF16), g_ref[...].astype(BF16))

        @pl.when(pl.program_id(2) == 0)
        def _():
            o_ref[...] = part

        @pl.when(pl.program_id(2) > 0)
        def _():
            o_ref[...] += part

    if blocked:
        out_shape = _sds((n // tn, ka, tn), F32)
        out_spec = pl.BlockSpec((None, tk, tn), lambda i, j, s: (j, i, 0))
    else:
        out_shape = _sds((ka, n), F32)
        out_spec = pl.BlockSpec((tk, tn), lambda i, j, s: (i, j))
    return pl.pallas_call(
        body,
        name=name,
        grid=(ka // tk, n // tn, s_len // ts),
        in_specs=[pl.BlockSpec((ts, tk), lambda i, j, s: (s, i)), pl.BlockSpec((ts, tn), lambda i, j, s: (s, j))],
        out_specs=out_spec,
        out_shape=out_shape,
        compiler_params=_params(("parallel", "parallel", "arbitrary")),
    )(a, g)


def _rms_fwd(name, x, g, tm):
    s_len, d = x.shape

    def body(x_ref, g_ref, h_ref, r_ref):
        xv = x_ref[...]
        r = lax.rsqrt(jnp.mean(xv * xv, axis=-1, keepdims=True) + EPS)
        h_ref[...] = (xv * r * g_ref[...]).astype(BF16)
        r_ref[...] = r

    return pl.pallas_call(
        body,
        name=name,
        grid=(s_len // tm,),
        in_specs=[pl.BlockSpec((tm, d), lambda i: (i, 0)), pl.BlockSpec((1, d), lambda i: (0, 0))],
        out_specs=[pl.BlockSpec((tm, d), lambda i: (i, 0)), pl.BlockSpec((tm, 1), lambda i: (i, 0))],
        out_shape=[_sds((s_len, d), BF16), _sds((s_len, 1), F32)],
        compiler_params=_params(("parallel",)),
    )(x, g)


def _forget_cumsum(prest, b_f_pad, tc):
    s_len = prest.shape[0]

    def body(f_ref, b_ref, c_ref, carry):
        @pl.when(pl.program_id(0) == 0)
        def _():
            carry[...] = jnp.zeros_like(carry)

        logf = _log_sigmoid(f_ref[...] + b_ref[...])
        row = lax.broadcasted_iota(jnp.int32, (tc, tc), 0)
        col = lax.broadcasted_iota(jnp.int32, (tc, tc), 1)
        tri = (row >= col).astype(F32)
        c = _dot_f32(tri, logf) + carry[...]
        c_ref[...] = c
        carry[...] = c[tc - 1:tc, :]

    return pl.pallas_call(
        body,
        name="forget_cumsum",
        grid=(s_len // tc,),
        in_specs=[pl.BlockSpec((tc, LANES), lambda i: (i, F_OFF // LANES)), pl.BlockSpec((1, LANES), lambda i: (0, 0))],
        out_specs=pl.BlockSpec((tc, LANES), lambda i: (i, 0)),
        out_shape=_sds((s_len, LANES), F32),
        scratch_shapes=[pltpu.VMEM((1, LANES), F32)],
        compiler_params=_params(("arbitrary",)),
    )(prest, b_f_pad)


def _attn_fwd(qkv, c_col, c_row, tq, tk):
    s_len = qkv.shape[0]
    nq, nk = s_len // tq, s_len // tk
    ratio = tq // tk

    def last_k(i):
        return (i + 1) * ratio - 1

    def body(q_ref, k_ref, v_ref, cq_ref, ck_ref, o_ref, lse_ref, acc, m_sc, l_sc):
        i, j = pl.program_id(0), pl.program_id(1)

        @pl.when(j == 0)
        def _():
            acc[...] = jnp.zeros_like(acc)
            m_sc[...] = jnp.full_like(m_sc, NEG)
            l_sc[...] = jnp.zeros_like(l_sc)

        @pl.when(j <= last_k(i))
        def _():
            rows = i * tq + lax.broadcasted_iota(jnp.int32, (tq, tk), 0)
            cols = j * tk + lax.broadcasted_iota(jnp.int32, (tq, tk), 1)
            causal = rows >= cols
            lt64 = _lane_lt64((tq, LANES))
            for p in range(N_HEADS // 2):
                lanes = slice(p * LANES, (p + 1) * LANES)
                q_pair = q_ref[:, lanes] * jnp.asarray(HEAD_DIM ** -0.5, BF16)
                k_pair = k_ref[:, lanes]
                v_pair = v_ref[:, lanes]
                alphas, pvs = [], []
                for half in range(2):
                    h = 2 * p + half
                    sel = lt64 if half == 0 else jnp.logical_not(lt64)
                    q_h = jnp.where(sel, q_pair, jnp.zeros_like(q_pair))
                    s = _dot_nt(q_h, k_pair) + (cq_ref[:, h:h + 1] - ck_ref[h:h + 1, :])
                    s = jnp.where(causal, s, NEG)
                    m_old = m_sc[h]
                    m_new = jnp.maximum(m_old, jnp.max(s, axis=1, keepdims=True))
                    alpha = jnp.exp(m_old - m_new)
                    pr = jnp.exp(s - m_new)
                    l_sc[h] = alpha * l_sc[h] + jnp.sum(pr, axis=1, keepdims=True)
                    m_sc[h] = m_new
                    alphas.append(alpha)
                    pvs.append(_dot(pr.astype(BF16), v_pair))
                alpha_pair = jnp.where(lt64, alphas[0], alphas[1])
                pv_pair = jnp.where(lt64, pvs[0], pvs[1])
                acc[:, lanes] = acc[:, lanes] * alpha_pair + pv_pair

        @pl.when(j == last_k(i))
        def _():
            lt64 = _lane_lt64((tq, LANES))
            lane = lax.broadcasted_iota(jnp.int32, (tq, LANES), 1)
            lse = jnp.zeros((tq, LANES), F32)
            for p in range(N_HEADS // 2):
                lanes = slice(p * LANES, (p + 1) * LANES)
                l_pair = jnp.where(lt64, l_sc[2 * p], l_sc[2 * p + 1])
                o_ref[:, lanes] = acc[:, lanes] / l_pair
                for h in (2 * p, 2 * p + 1):
                    lse = jnp.where(lane == h, m_sc[h] + jnp.log(l_sc[h]), lse)
            lse_ref[...] = lse

    def kv_map(col):
        return lambda i, j: (jnp.minimum(j, last_k(i)), col)

    return pl.pallas_call(
        body,
        name="attn_fwd",
        grid=(nq, nk),
        in_specs=[
            pl.BlockSpec((tq, FOX_W), lambda i, j: (i, 0)),
            pl.BlockSpec((tk, FOX_W), kv_map(1)),
            pl.BlockSpec((tk, FOX_W), kv_map(2)),
            pl.BlockSpec((tq, LANES), lambda i, j: (i, 0)),
            pl.BlockSpec((N_HEADS, tk), lambda i, j: (0, jnp.minimum(j, last_k(i)))),
        ],
        out_specs=[pl.BlockSpec((tq, FOX_W), lambda i, j: (i, 0)), pl.BlockSpec((tq, LANES), lambda i, j: (i, 0))],
        out_shape=[_sds((s_len, FOX_W), F32), _sds((s_len, LANES), F32)],
        scratch_shapes=[pltpu.VMEM((tq, FOX_W), F32), pltpu.VMEM((N_HEADS, tq, 1), F32), pltpu.VMEM((N_HEADS, tq, 1), F32)],
        compiler_params=_params(("parallel", "arbitrary")),
    )(qkv, qkv, qkv, c_col, c_row)


def _sgu_mix(vn, w_stack, lt64):
    outs = []
    for p in range(SGU_G // 2):
        r = _dot(w_stack[p], vn[:, p * LANES:(p + 1) * LANES])
        outs.append(jnp.where(lt64, r[:SGU_LEN], r[SGU_LEN:]))
    return jnp.concatenate(outs, axis=1)


def _sgu_norm(sv, ln_g, ln_b):
    svg = _gelu(sv)
    xc = svg - jnp.mean(svg, axis=-1, keepdims=True)
    rstd = lax.rsqrt(jnp.mean(xc * xc, axis=-1, keepdims=True) + EPS)
    xhat = xc * rstd
    return xhat, rstd, xhat * ln_g + ln_b


def _sgu_fwd(prest, ln_g, ln_b, w_stack, b_pair, tm):
    s_len = prest.shape[0]

    def body(u_ref, sv_ref, g_ref, b_ref, w_ref, bp_ref, sg_ref):
        lt64 = _lane_lt64((SGU_LEN, LANES))
        _, _, vn = _sgu_norm(sv_ref[...], g_ref[...], b_ref[...])
        vn = vn.astype(BF16)
        w_stack_v = [w_ref[p] for p in range(SGU_G // 2)]
        for w in range(tm // SGU_LEN):
            win = slice(w * SGU_LEN, (w + 1) * SGU_LEN)
            mixed = _sgu_mix(vn[win], w_stack_v, lt64) + bp_ref[...]
            sg_ref[win, :] = (_gelu(u_ref[win, :]) * mixed).astype(BF16)

    return pl.pallas_call(
        body,
        name="sgu_fwd",
        grid=(s_len // tm,),
        in_specs=[
            pl.BlockSpec((tm, SGU_W), lambda i: (i, U_OFF // SGU_W)),
            pl.BlockSpec((tm, SGU_W), lambda i: (i, SV_OFF // SGU_W)),
            pl.BlockSpec((1, SGU_W), lambda i: (0, 0)),
            pl.BlockSpec((1, SGU_W), lambda i: (0, 0)),
            pl.BlockSpec((SGU_G // 2, 2 * SGU_LEN, SGU_LEN), lambda i: (0, 0, 0)),
            pl.BlockSpec((SGU_LEN, SGU_W), lambda i: (0, 0)),
        ],
        out_specs=pl.BlockSpec((tm, SGU_W), lambda i: (i, 0)),
        out_shape=_sds((s_len, SGU_W), BF16),
        compiler_params=_params(("parallel",)),
    )(prest, prest, ln_g, ln_b, w_stack, b_pair)


def _sgu_bwd(prest, dsg, ln_g, ln_b, w_stack, wt_stack, b_pair, tm):
    s_len = prest.shape[0]
    n_pair = SGU_G // 2

    def body(u_ref, sv_ref, dsg_ref, g_ref, b_ref, w_ref, wt_ref, bp_ref,
             du_ref, dsv_ref, dw_ref, db_ref, dg_ref, dbeta_ref, dvn_sc):
        @pl.when(pl.program_id(0) == 0)
        def _():
            dw_ref[...] = jnp.zeros_like(dw_ref)
            db_ref[...] = jnp.zeros_like(db_ref)
            dg_ref[...] = jnp.zeros_like(dg_ref)
            dbeta_ref[...] = jnp.zeros_like(dbeta_ref)

        lt64 = _lane_lt64((SGU_LEN, LANES))
        sv = sv_ref[...]
        xhat, rstd, vn32 = _sgu_norm(sv, g_ref[...], b_ref[...])
        vn = vn32.astype(BF16)
        w_stack_v = [w_ref[p] for p in range(n_pair)]
        db = jnp.zeros((SGU_LEN, SGU_W), F32)
        for w in range(tm // SGU_LEN):
            win = slice(w * SGU_LEN, (w + 1) * SGU_LEN)
            u = u_ref[win, :]
            dsg_w = dsg_ref[win, :]
            mixed = _sgu_mix(vn[win], w_stack_v, lt64) + bp_ref[...]
            du_ref[win, :] = (dsg_w * mixed * _gelu_grad(u)).astype(BF16)
            dmixed = dsg_w * _gelu(u)
            db = db + dmixed
            dm16 = dmixed.astype(BF16)
            for p in range(n_pair):
                lanes = slice(p * LANES, (p + 1) * LANES)
                dmp = dm16[:, lanes]
                r = _dot(wt_ref[p], dmp)
                dvn_sc[win, lanes] = jnp.where(lt64, r[:SGU_LEN], r[SGU_LEN:])
                zero = jnp.zeros_like(dmp)
                dm_ab = jnp.concatenate([jnp.where(lt64, dmp, zero), jnp.where(lt64, zero, dmp)], axis=0)
                dw_ref[p] += _dot_nt(dm_ab, vn[win, lanes])
        db_ref[...] += db
        dvn = dvn_sc[...]
        dg_ref[...] += jnp.sum(dvn * xhat, axis=0, keepdims=True)
        dbeta_ref[...] += jnp.sum(dvn, axis=0, keepdims=True)
        dxh = dvn * g_ref[...]
        dsvg = rstd * (dxh - jnp.mean(dxh, axis=-1, keepdims=True) - xhat * jnp.mean(dxh * xhat, axis=-1, keepdims=True))
        dsv_ref[...] = (dsvg * _gelu_grad(sv)).astype(BF16)

    const2 = lambda i: (0, 0)
    const3 = lambda i: (0, 0, 0)
    return pl.pallas_call(
        body,
        name="sgu_bwd",
        grid=(s_len // tm,),
        in_specs=[
            pl.BlockSpec((tm, SGU_W), lambda i: (i, U_OFF // SGU_W)),
            pl.BlockSpec((tm, SGU_W), lambda i: (i, SV_OFF // SGU_W)),
            pl.BlockSpec((tm, SGU_W), lambda i: (i, 0)),
            pl.BlockSpec((1, SGU_W), const2),
            pl.BlockSpec((1, SGU_W), const2),
            pl.BlockSpec((n_pair, 2 * SGU_LEN, SGU_LEN), const3),
            pl.BlockSpec((n_pair, 2 * SGU_LEN, SGU_LEN), const3),
            pl.BlockSpec((SGU_LEN, SGU_W), const2),
        ],
        out_specs=[
            pl.BlockSpec((tm, SGU_W), lambda i: (i, 0)),
            pl.BlockSpec((tm, SGU_W), lambda i: (i, 0)),
            pl.BlockSpec((n_pair, 2 * SGU_LEN, SGU_LEN), const3),
            pl.BlockSpec((SGU_LEN, SGU_W), const2),
            pl.BlockSpec((1, SGU_W), const2),
            pl.BlockSpec((1, SGU_W), const2),
        ],
        out_shape=[
            _sds((s_len, SGU_W), BF16), _sds((s_len, SGU_W), BF16), _sds((n_pair, 2 * SGU_LEN, SGU_LEN), F32),
            _sds((SGU_LEN, SGU_W), F32), _sds((1, SGU_W), F32), _sds((1, SGU_W), F32),
        ],
        scratch_shapes=[pltpu.VMEM((tm, SGU_W), F32)],
        compiler_params=_params(("arbitrary",)),
    )(prest, prest, dsg, ln_g, ln_b, w_stack, wt_stack, b_pair)


def _attn_dq(qkv, do, c_col, c_row, lse, delta, tq, tk):
    s_len = qkv.shape[0]
    nq, nk = s_len // tq, s_len // tk
    ratio = tq // tk
    scale = HEAD_DIM ** -0.5

    def last_k(i):
        return (i + 1) * ratio - 1

    def body(q_ref, k_ref, v_ref, do_ref, cq_ref, ck_ref, lse_ref, dl_ref, dq_ref, dc_ref, acc, dc_acc):
        i, j = pl.program_id(0), pl.program_id(1)

        @pl.when(j == 0)
        def _():
            acc[...] = jnp.zeros_like(acc)
            dc_acc[...] = jnp.zeros_like(dc_acc)

        @pl.when(j <= last_k(i))
        def _():
            rows = i * tq + lax.broadcasted_iota(jnp.int32, (tq, tk), 0)
            cols = j * tk + lax.broadcasted_iota(jnp.int32, (tq, tk), 1)
            causal = rows >= cols
            lt64 = _lane_lt64((tq, LANES))
            lane = lax.broadcasted_iota(jnp.int32, (tq, LANES), 1)
            dc = jnp.zeros((tq, LANES), F32)
            for p in range(N_HEADS // 2):
                lanes = slice(p * LANES, (p + 1) * LANES)
                q_pair = q_ref[:, lanes] * jnp.asarray(scale, BF16)
                k_pair = k_ref[:, lanes]
                v_pair = v_ref[:, lanes]
                do_pair = do_ref[:, lanes].astype(BF16)
                zero = jnp.zeros_like(q_pair)
                dqs = []
                for half in range(2):
                    h = 2 * p + half
                    sel = lt64 if half == 0 else jnp.logical_not(lt64)
                    s = _dot_nt(jnp.where(sel, q_pair, zero), k_pair) + (cq_ref[:, h:h + 1] - ck_ref[h:h + 1, :])
                    s = jnp.where(causal, s, NEG)
                    pr = jnp.exp(s - lse_ref[:, h:h + 1])
                    dp = _dot_nt(jnp.where(sel, do_pair, zero), v_pair)
                    ds = pr * (dp - dl_ref[:, h:h + 1])
                    dqs.append(_dot(ds.astype(BF16), k_pair))
                    dc = jnp.where(lane == h, jnp.sum(ds, axis=1, keepdims=True), dc)
                acc[:, lanes] += jnp.where(lt64, dqs[0], dqs[1])
            dc_acc[...] += dc

        @pl.when(j == last_k(i))
        def _():
            dq_ref[...] = (acc[...] * scale).astype(BF16)
            dc_ref[...] = dc_acc[...]

    def kv_map(col):
        return lambda i, j: (jnp.minimum(j, last_k(i)), col)

    q_row = lambda i, j: (i, 0)
    return pl.pallas_call(
        body,
        name="attn_dq",
        grid=(nq, nk),
        in_specs=[
            pl.BlockSpec((tq, FOX_W), q_row),
            pl.BlockSpec((tk, FOX_W), kv_map(1)),
            pl.BlockSpec((tk, FOX_W), kv_map(2)),
            pl.BlockSpec((tq, FOX_W), q_row),
            pl.BlockSpec((tq, LANES), q_row),
            pl.BlockSpec((N_HEADS, tk), lambda i, j: (0, jnp.minimum(j, last_k(i)))),
            pl.BlockSpec((tq, LANES), q_row),
            pl.BlockSpec((tq, LANES), q_row),
        ],
        out_specs=[pl.BlockSpec((tq, FOX_W), q_row), pl.BlockSpec((tq, LANES), q_row)],
        out_shape=[_sds((s_len, FOX_W), BF16), _sds((s_len, LANES), F32)],
        scratch_shapes=[pltpu.VMEM((tq, FOX_W), F32), pltpu.VMEM((tq, LANES), F32)],
        compiler_params=_params(("parallel", "arbitrary")),
    )(qkv, qkv, qkv, do, c_col, c_row, lse, delta)


def _attn_dkv(qkv, do, c_col, c_row, lse_row, delta_row, tq, tk):
    s_len = qkv.shape[0]
    nq, nk = s_len // tq, s_len // tk
    ratio = tq // tk
    scale = HEAD_DIM ** -0.5

    def first_q(j):
        return j // ratio

    def body(q_ref, k_ref, v_ref, do_ref, cq_ref, ck_ref, lse_ref, dl_ref, dk_ref, dv_ref, dc_ref, dk_acc, dv_acc, dc_acc):
        j, i = pl.program_id(0), pl.program_id(1)

        @pl.when(i == 0)
        def _():
            dk_acc[...] = jnp.zeros_like(dk_acc)
            dv_acc[...] = jnp.zeros_like(dv_acc)
            dc_acc[...] = jnp.zeros_like(dc_acc)

        @pl.when(i >= first_q(j))
        def _():
            krow = j * tk + lax.broadcasted_iota(jnp.int32, (tk, tq), 0)
            qcol = i * tq + lax.broadcasted_iota(jnp.int32, (tk, tq), 1)
            causal = qcol >= krow
            lt64 = _lane_lt64((tk, LANES))
            lane = lax.broadcasted_iota(jnp.int32, (tk, LANES), 1)
            dc = jnp.zeros((tk, LANES), F32)
            for p in range(N_HEADS // 2):
                lanes = slice(p * LANES, (p + 1) * LANES)
                q_pair = q_ref[:, lanes] * jnp.asarray(scale, BF16)
                k_pair = k_ref[:, lanes]
                v_pair = v_ref[:, lanes]
                do_pair = do_ref[:, lanes].astype(BF16)
                zero = jnp.zeros_like(k_pair)
                dks, dvs = [], []
                for half in range(2):
                    h = 2 * p + half
                    sel = lt64 if half == 0 else jnp.logical_not(lt64)
                    st = _dot_nt(jnp.where(sel, k_pair, zero), q_pair) + (cq_ref[h:h + 1, :] - ck_ref[:, h:h + 1])
                    st = jnp.where(causal, st, NEG)
                    pt = jnp.exp(st - lse_ref[h:h + 1, :])
                    dvs.append(_dot(pt.astype(BF16), do_pair))
                    dpt = _dot_nt(jnp.where(sel, v_pair, zero), do_pair)
                    dst = pt * (dpt - dl_ref[h:h + 1, :])
                    dks.append(_dot(dst.astype(BF16), q_pair))
                    dc = jnp.where(lane == h, -jnp.sum(dst, axis=1, keepdims=True), dc)
                dk_acc[:, lanes] += jnp.where(lt64, dks[0], dks[1])
                dv_acc[:, lanes] += jnp.where(lt64, dvs[0], dvs[1])
            dc_acc[...] += dc

        @pl.when(i == nq - 1)
        def _():
            dk_ref[...] = dk_acc[...].astype(BF16)
            dv_ref[...] = dv_acc[...].astype(BF16)
            dc_ref[...] = dc_acc[...]

    q_map = lambda j, i: (jnp.maximum(i, first_q(j)), 0)
    k_row = lambda j, i: (j, 0)
    return pl.pallas_call(
        body,
        name="attn_dkv",
        grid=(nk, nq),
        in_specs=[
            pl.BlockSpec((tq, FOX_W), q_map),
            pl.BlockSpec((tk, FOX_W), lambda j, i: (j, 1)),
            pl.BlockSpec((tk, FOX_W), lambda j, i: (j, 2)),
            pl.BlockSpec((tq, FOX_W), q_map),
            pl.BlockSpec((N_HEADS, tq), lambda j, i: (0, jnp.maximum(i, first_q(j)))),
            pl.BlockSpec((tk, LANES), k_row),
            pl.BlockSpec((N_HEADS, tq), lambda j, i: (0, jnp.maximum(i, first_q(j)))),
            pl.BlockSpec((N_HEADS, tq), lambda j, i: (0, jnp.maximum(i, first_q(j)))),
        ],
        out_specs=[pl.BlockSpec((tk, FOX_W), k_row), pl.BlockSpec((tk, FOX_W), k_row), pl.BlockSpec((tk, LANES), k_row)],
        out_shape=[_sds((s_len, FOX_W), BF16), _sds((s_len, FOX_W), BF16), _sds((s_len, LANES), F32)],
        scratch_shapes=[pltpu.VMEM((tk, FOX_W), F32), pltpu.VMEM((tk, FOX_W), F32), pltpu.VMEM((tk, LANES), F32)],
        compiler_params=_params(("parallel", "arbitrary")),
    )(qkv, qkv, qkv, do, c_row, c_col, lse_row, delta_row)


def _forget_bwd(dc_rows, dc_cols, prest, b_f_pad, tc):
    s_len = dc_rows.shape[0]
    nb = s_len // tc

    def body(dcr_ref, dc_ref, f_ref, b_ref, df_ref, db_ref, carry):
        @pl.when(pl.program_id(0) == 0)
        def _():
            carry[...] = jnp.zeros_like(carry)
            db_ref[...] = jnp.zeros_like(db_ref)

        row = lax.broadcasted_iota(jnp.int32, (tc, tc), 0)
        col = lax.broadcasted_iota(jnp.int32, (tc, tc), 1)
        tri = (row <= col).astype(F32)
        dlogf = _dot_f32(tri, dcr_ref[...] + dc_ref[...]) + carry[...]
        carry[...] = dlogf[0:1, :]
        z = f_ref[...] + b_ref[...]
        lane = lax.broadcasted_iota(jnp.int32, (tc, LANES), 1)
        dz = jnp.where(lane < N_HEADS, dlogf * _sigmoid(-z), 0.0)
        df_ref[...] = dz.astype(BF16)
        db_ref[...] += jnp.sum(dz, axis=0, keepdims=True)

    rev = lambda i: (nb - 1 - i, 0)
    return pl.pallas_call(
        body,
        name="forget_bwd",
        grid=(nb,),
        in_specs=[
            pl.BlockSpec((tc, LANES), rev),
            pl.BlockSpec((tc, LANES), rev),
            pl.BlockSpec((tc, LANES), lambda i: (nb - 1 - i, F_OFF // LANES)),
            pl.BlockSpec((1, LANES), lambda i: (0, 0)),
        ],
        out_specs=[pl.BlockSpec((tc, LANES), rev), pl.BlockSpec((1, LANES), lambda i: (0, 0))],
        out_shape=[_sds((s_len, LANES), BF16), _sds((1, LANES), F32)],
        scratch_shapes=[pltpu.VMEM((1, LANES), F32)],
        compiler_params=_params(("arbitrary",)),
    )(dc_rows, dc_cols, prest, b_f_pad)


def _pair_sum(name, g4, recv, idx, tr):
    _, _, r, c = g4.shape

    def body(idx_ref, g_ref, r_ref, p16_ref, own_ref):
        k = pl.program_id(1)
        s = g_ref[...] + r_ref[...]
        p16_ref[...] = s.astype(BF16)

        @pl.when(k == idx_ref[1])
        def _():
            own_ref[...] = s

    return pl.pallas_call(
        body,
        name=name,
        grid_spec=pltpu.PrefetchScalarGridSpec(
            num_scalar_prefetch=1,
            grid=(r // tr, 4),
            in_specs=[
                pl.BlockSpec((None, None, tr, c), lambda i, k, idx: (k, idx[0], i, 0)),
                pl.BlockSpec((None, tr, c), lambda i, k, idx: (k, i, 0)),
            ],
            out_specs=[
                pl.BlockSpec((None, tr, c), lambda i, k, idx: (k, i, 0)),
                pl.BlockSpec((tr, c), lambda i, k, idx: (i, 0)),
            ],
        ),
        out_shape=[_sds((4, r, c), BF16), _sds((r, c), F32)],
        compiler_params=_params(("parallel", "arbitrary")),
    )(idx, g4, recv)


def _adamw_math(w, g, m, v):
    m2 = ADAM_B1 * m + (1.0 - ADAM_B1) * g
    v2 = ADAM_B2 * v + (1.0 - ADAM_B2) * (g * g)
    m_hat = m2 / (1.0 - ADAM_B1 ** ADAM_STEP)
    v_hat = v2 / (1.0 - ADAM_B2 ** ADAM_STEP)
    delta = -ADAM_LR * (m_hat / (jnp.sqrt(v_hat) + ADAM_EPS) + ADAM_WD * w)
    return delta, m2, v2


def _adamw_shard(name, own, recv, w, m, v, tr):
    r, c = own.shape

    def body(own_ref, recv_ref, w_ref, m_ref, v_ref, g_ref, d_ref, m2_ref, v2_ref):
        g = own_ref[...]
        for k in range(3):
            g = g + recv_ref[k].astype(F32)
        delta, m2, v2 = _adamw_math(w_ref[...], g, m_ref[...], v_ref[...])
        g_ref[...] = g
        d_ref[...] = delta
        m2_ref[...] = m2
        v2_ref[...] = v2

    spec = pl.BlockSpec((tr, c), lambda i: (i, 0))
    return pl.pallas_call(
        body,
        name=name,
        grid=(r // tr,),
        in_specs=[spec, pl.BlockSpec((3, tr, c), lambda i: (0, i, 0)), spec, spec, spec],
        out_specs=[spec] * 4,
        out_shape=[_sds((r, c), F32)] * 4,
        compiler_params=_params(("parallel",)),
    )(own, recv, w, m, v)


def _adamw_small(gathered, w, m, v):
    _, r, _ = gathered.shape

    def body(ga_ref, w_ref, m_ref, v_ref, g_ref, d_ref, m2_ref, v2_ref):
        g = ga_ref[0]
        for k in range(1, N_DEV):
            g = g + ga_ref[k]
        delta, m2, v2 = _adamw_math(w_ref[...], g, m_ref[...], v_ref[...])
        g_ref[...] = g
        d_ref[...] = delta
        m2_ref[...] = m2
        v2_ref[...] = v2

    spec = pl.BlockSpec((r, LANES), lambda i: (0, 0))
    return pl.pallas_call(
        body,
        name="adamw_small",
        grid=(1,),
        in_specs=[pl.BlockSpec((N_DEV, r, LANES), lambda i: (0, 0, 0)), spec, spec, spec],
        out_specs=[spec] * 4,
        out_shape=[_sds((r, LANES), F32)] * 4,
        compiler_params=_params(("arbitrary",)),
    )(gathered, w, m, v)


_SMALL = (("w_sgu", (1, SGU_G, SGU_LEN, SGU_LEN)), ("b_sgu", (1, SGU_G, SGU_LEN)), ("norm1_g", (1, D_MODEL)),
          ("norm2_g", (1, D_MODEL)), ("normf_g", (D_MODEL,)), ("ln_v_g", (1, SGU_W)), ("ln_v_b", (1, SGU_W)),
          ("b_f", (1, N_HEADS)))


def _pack_small(values):
    rows = []
    for val in values:
        flat = val.reshape(-1).astype(F32)
        pad = (-flat.shape[0]) % LANES
        rows.append(jnp.pad(flat, (0, pad)).reshape(-1, LANES))
    packed = jnp.concatenate(rows, axis=0)
    return jnp.pad(packed, ((0, (-packed.shape[0]) % 8), (0, 0)))


def _unpack_small(packed):
    out, row = [], 0
    for _, shape in _SMALL:
        size = math.prod(shape)
        n_rows = -(-size // LANES)
        out.append(packed[row:row + n_rows].reshape(-1)[:size].reshape(shape))
        row += n_rows
    return out


def kernel(x, norm1_g, w_in, b_f, ln_v_g, ln_v_b, w_sgu, b_sgu, w_a, w_b, w_o, norm2_g, w_up, w_down, normf_g, loss_target, m_norm1_g, m_w_in, m_b_f, m_ln_v_g, m_ln_v_b, m_w_sgu, m_b_sgu, m_w_a, m_w_b, m_w_o, m_norm2_g, m_w_up, m_w_down, m_normf_g, v_norm1_g, v_w_in, v_b_f, v_ln_v_g, v_ln_v_b, v_w_sgu, v_b_sgu, v_w_a, v_w_b, v_w_o, v_norm2_g, v_w_up, v_w_down, v_normf_g):
    xs = x[0]
    target = loss_target[0]
    s_len, d = xs.shape
    tm = min(512, s_len)
    tr = min(256, s_len)
    ta = min(256, s_len)

    big = (w_in[0], w_a[0], w_b[0], w_o[0], w_up[0], w_down[0])
    flat = jnp.concatenate([w.reshape(-1).astype(BF16) for w in big]).reshape(-1, D_MODEL)
    gathered = _all_gather("gather_weights", flat)
    row = 0
    full = []
    for w, col_sharded in zip(big, (True, True, True, False, True, False)):
        n_rows = w.size // D_MODEL
        blk = gathered[:, row:row + n_rows].reshape((N_DEV,) + w.shape)
        row += n_rows
        if col_sharded:
            full.append(jnp.transpose(blk, (1, 0, 2)).reshape(w.shape[0], N_DEV * w.shape[1]))
        else:
            full.append(blk.reshape(N_DEV * w.shape[0], w.shape[1]))
    w_in_f, w_a_f, w_b_f, w_o_f, w_up_f, w_down_f = full
    w_qkv = w_in_f[:, :QKV_W]
    f_lo = QKV_W
    u_lo = f_lo + N_HEADS
    w_rest = jnp.concatenate([w_in_f[:, u_lo:], jnp.pad(w_in_f[:, f_lo:u_lo], ((0, 0), (0, LANES - N_HEADS)))], axis=1)

    chunk_id = jnp.arange(SGU_LEN) // CHUNK
    sgu_mask = chunk_id[None, :] <= chunk_id[:, None]
    w_masked = jnp.where(sgu_mask[None], w_sgu[0], 0.0)
    w_stack = w_masked.reshape(SGU_G // 2, 2 * SGU_LEN, SGU_LEN).astype(BF16)
    wt_stack = jnp.transpose(w_masked, (0, 2, 1)).reshape(SGU_G // 2, 2 * SGU_LEN, SGU_LEN).astype(BF16)
    b_pair = jnp.transpose(jnp.repeat(b_sgu[0], SGU_W // SGU_G, axis=0))
    b_f_pad = jnp.pad(b_f, ((0, 0), (0, LANES - N_HEADS)))
    head_sel = (jnp.arange(FOX_W)[:, None] // HEAD_DIM == jnp.arange(LANES)[None, :]).astype(F32)

    h, r1 = _rms_fwd("rms1", xs, norm1_g, tm)

    def store(dtype):
        def epi(accs, ex, out):
            out[0][...] = accs[0].astype(dtype)
        return epi

    (qkv,) = _mm("proj_qkv", [(h, w_qkv, False, None)], [], [(_sds((s_len, QKV_W), BF16), _tile(tm, 512))],
                 store(BF16), m=s_len, tm=tm, n=QKV_W, tn=512)
    (prest,) = _mm("proj_rest", [(h, w_rest, False, None)], [], [(_sds((s_len, REST_W), F32), _tile(tm, 640))],
                   store(F32), m=s_len, tm=tm, n=REST_W, tn=640)

    c_col = _forget_cumsum(prest, b_f_pad, ta)
    c_row = jnp.transpose(c_col[:, :N_HEADS])
    o, lse = _attn_fwd(qkv, c_col, c_row, ta, ta)
    sg = _sgu_fwd(prest, ln_v_g, ln_v_b, w_stack, b_pair, tm)

    def merge_epi(accs, ex, out):
        ya, yb = accs
        sa, sb = _sigmoid(ex[0][...]), _sigmoid(ex[1][...])
        out[0][...] = (sa * ya + sb * yb).astype(BF16)
        out[1][...] = ya
        out[2][...] = yb

    merged, ya, yb = _mm(
        "merge", [(o, w_a_f, False, None), (sg, w_b_f, False, None)],
        [(prest, _tile(tm, 512, GA_OFF // 512)), (prest, _tile(tm, 512, GB_OFF // 512))],
        [(_sds((s_len, d), BF16), _tile(tm, 512)), (_sds((s_len, d), F32), _tile(tm, 512)), (_sds((s_len, d), F32), _tile(tm, 512))],
        merge_epi, m=s_len, tm=tm, n=d, tn=512)

    def resid_epi(accs, ex, out):
        out[0][...] = ex[0][...] + accs[0]

    (x1,) = _mm("out_proj", [(merged, w_o_f, False, None)], [(xs, _tile(tm, 512))],
                [(_sds((s_len, d), F32), _tile(tm, 512))], resid_epi, m=s_len, tm=tm, n=d, tn=512)

    h2, r2 = _rms_fwd("rms2", x1, norm2_g, tm)

    def up_epi(accs, ex, out):
        a = accs[0]
        out[0][...] = a
        out[1][...] = jnp.square(jnp.maximum(a, 0.0)).astype(BF16)

    a_up, act = _mm("mlp_up", [(h2, w_up_f, False, None)], [],
                    [(_sds((s_len, D_FF), F32), _tile(tm, 512)), (_sds((s_len, D_FF), BF16), _tile(tm, 512))],
                    up_epi, m=s_len, tm=tm, n=D_FF, tn=512)

    def first_step():
        return jnp.logical_and(pl.program_id(0) == 0, pl.program_id(1) == 0)

    def accumulate(ref, val):
        @pl.when(first_step())
        def _():
            ref[...] = val

        @pl.when(jnp.logical_not(first_step()))
        def _():
            ref[...] += val

    def final_epi(accs, ex, out):
        x1_ref, t_ref, g_ref = ex
        x2 = x1_ref[...] + accs[0]
        rf = lax.rsqrt(jnp.mean(x2 * x2, axis=-1, keepdims=True) + EPS)
        xh = x2 * rf
        gf = g_ref[...]
        err = xh * gf - t_ref[...]
        dy = err * (1.0 / d)
        out[0][...] = _rms_bwd(xh, rf, gf, dy)
        accumulate(out[1], jnp.sum(dy * xh, axis=0, keepdims=True))
        part = 0.5 * jnp.sum(jnp.sum(err * err, axis=-1, keepdims=True) * (1.0 / d), axis=0, keepdims=True)
        accumulate(out[2], jnp.broadcast_to(part, (1, LANES)))

    gf2 = normf_g.reshape(1, d)
    dx2, g_normf, loss_part = _mm(
        "mlp_down_loss", [(act, w_down_f, False, None)],
        [(x1, _row(tr, d)), (target, _row(tr, d)), (gf2, _whole((1, d)))],
        [(_sds((s_len, d), F32), _row(tr, d)), (_sds((1, d), F32), _whole((1, d))), (_sds((1, LANES), F32), _whole((1, LANES)))],
        final_epi, m=s_len, tm=tr, n=d, tn=d, arbitrary=True)

    def dact_epi(accs, ex, out):
        out[0][...] = (accs[0] * (2.0 * jnp.maximum(ex[0][...], 0.0))).astype(BF16)

    (da,) = _mm("mlp_down_bwd", [(dx2, w_down_f, True, None)], [(a_up, _tile(tm, 512))],
                [(_sds((s_len, D_FF), BF16), _tile(tm, 512))], dact_epi, m=s_len, tm=tm, n=D_FF, tn=512)
    ts = min(512, s_len)
    g_down = _mm_tn("grad_w_down", act, dx2, tk=1024, tn=d, ts=ts)
    g_up = _mm_tn("grad_w_up", h2, da, tk=d, tn=D_FF // N_DEV, ts=ts, blocked=True)

    def dh2_epi(accs, ex, out):
        x1_ref, r_ref, g_ref, dx2_ref = ex
        r = r_ref[...]
        xh = x1_ref[...] * r
        dh2 = accs[0]
        out[0][...] = dx2_ref[...] + _rms_bwd(xh, r, g_ref[...], dh2)
        accumulate(out[1], jnp.sum(dh2 * xh, axis=0, keepdims=True))

    dx1, g_norm2 = _mm(
        "mlp_up_bwd", [(da, w_up_f, True, None)],
        [(x1, _row(tr, d)), (r2, _row(tr, 1)), (norm2_g, _whole((1, d))), (dx2, _row(tr, d))],
        [(_sds((s_len, d), F32), _row(tr, d)), (_sds((1, d), F32), _whole((1, d)))],
        dh2_epi, m=s_len, tm=tr, n=d, tn=d, arbitrary=True)

    def dmerge_epi(accs, ex, out):
        dm = accs[0]
        sa, sb = _sigmoid(ex[0][...]), _sigmoid(ex[1][...])
        out[0][...] = (dm * sa).astype(BF16)
        out[1][...] = (dm * sb).astype(BF16)
        out[2][...] = (dm * ex[2][...] * sa * (1.0 - sa)).astype(BF16)
        out[3][...] = (dm * ex[3][...] * sb * (1.0 - sb)).astype(BF16)

    dya, dyb, dga, dgb = _mm(
        "out_proj_bwd", [(dx1, w_o_f, True, None)],
        [(prest, _tile(tm, 512, GA_OFF // 512)), (prest, _tile(tm, 512, GB_OFF // 512)), (ya, _tile(tm, 512)), (yb, _tile(tm, 512))],
        [(_sds((s_len, d), BF16), _tile(tm, 512))] * 4, dmerge_epi, m=s_len, tm=tm, n=d, tn=512)
    g_o = _mm_tn("grad_w_o", merged, dx1, tk=512, tn=d, ts=ts).reshape(N_DEV, d // N_DEV, d)
    g_a = _mm_tn("grad_w_a", o, dya, tk=FOX_W, tn=d // N_DEV, ts=ts, blocked=True)
    g_b = _mm_tn("grad_w_b", sg, dyb, tk=SGU_W, tn=d // N_DEV, ts=ts, blocked=True)

    def do_epi(accs, ex, out):
        do = accs[0]
        out[0][...] = do
        out[1][...] = _dot_f32(do * ex[0][...], ex[1][...])

    do, delta = _mm(
        "attn_out_bwd", [(dya, w_a_f, True, None)], [(o, _row(tm, FOX_W)), (head_sel, _whole((FOX_W, LANES)))],
        [(_sds((s_len, FOX_W), F32), _row(tm, FOX_W)), (_sds((s_len, LANES), F32), _row(tm, LANES))],
        do_epi, m=s_len, tm=tm, n=FOX_W, tn=FOX_W)
    (dsg,) = _mm("sgu_out_bwd", [(dyb, w_b_f, True, None)], [], [(_sds((s_len, SGU_W), F32), _tile(tm, SGU_W))],
                 store(F32), m=s_len, tm=tm, n=SGU_W, tn=SGU_W)

    du, dsv, dw_pairs, db_pos, g_ln_g, g_ln_b = _sgu_bwd(prest, dsg, ln_v_g, ln_v_b, w_stack, wt_stack, b_pair, tm)
    g_w_sgu = jnp.where(sgu_mask[None], dw_pairs.reshape(SGU_G, SGU_LEN, SGU_LEN), 0.0)
    g_b_sgu = jnp.transpose(jnp.sum(db_pos.reshape(SGU_LEN, SGU_G, SGU_W // SGU_G), axis=-1))

    lse_row = jnp.transpose(lse[:, :N_HEADS])
    delta_row = jnp.transpose(delta[:, :N_HEADS])
    dq, dc_rows = _attn_dq(qkv, do, c_col, c_row, lse, delta, ta, ta)
    dk, dv, dc_cols = _attn_dkv(qkv, do, c_col, c_row, lse_row, delta_row, ta, ta)
    dfl, g_bf = _forget_bwd(dc_rows, dc_cols, prest, b_f_pad, ta)

    dp = (dq, dk, dv, du, dsv, dga, dgb)
    g_in_cols = [_mm_tn("grad_w_in_%d" % k, h, seg, tk=d, tn=512, ts=ts) for k, seg in enumerate(dp)]
    g_f = _mm_tn("grad_w_in_f", h, dfl, tk=d, tn=LANES, ts=ts)[:, :N_HEADS]
    g_in_full = jnp.concatenate(g_in_cols[:3] + [g_f] + g_in_cols[3:], axis=1)
    g_in = jnp.transpose(g_in_full.reshape(d, N_DEV, IN_SHARD), (1, 0, 2))

    def dx_epi(accs, ex, out):
        x_ref, r_ref, g_ref, dx1_ref = ex
        dh = accs[0]
        for extra in accs[1:]:
            dh = dh + extra
        r = r_ref[...]
        xh = x_ref[...] * r
        out[0][...] = dx1_ref[...] + _rms_bwd(xh, r, g_ref[...], dh)
        accumulate(out[1], jnp.sum(dh * xh, axis=0, keepdims=True))

    rest_cols = ((du, U_OFF, 512), (dsv, SV_OFF, 512), (dga, GA_OFF, 1024), (dgb, GB_OFF, 1024), (dfl, F_OFF, LANES))
    dx_pairs = [(seg, w_qkv, True, (512 * k, 512 * (k + 1))) for k, seg in enumerate((dq, dk, dv))]
    dx_pairs += [(seg, w_rest, True, (lo, lo + width)) for seg, lo, width in rest_cols]
    grad_x, g_norm1 = _mm(
        "proj_bwd", dx_pairs,
        [(xs, _row(tr, d)), (r1, _row(tr, 1)), (norm1_g, _whole((1, d))), (dx1, _row(tr, d))],
        [(_sds((s_len, d), F32), _row(tr, d)), (_sds((1, d), F32), _whole((1, d)))],
        dx_epi, m=s_len, tm=tr, n=d, tn=d, arbitrary=True)

    my_c = lax.axis_index("c")
    my_chip = 2 * lax.axis_index("x") + lax.axis_index("y")
    idx = jnp.stack([my_c, my_chip]).astype(jnp.int32)
    grads8 = (g_in, g_a, g_b, g_o, g_up, g_down.reshape(N_DEV, D_FF // N_DEV, d))
    grads4 = [g.reshape((4, 2) + g.shape[1:]) for g in grads8]
    from_sibling = _pair_exchange("grad_pair_exchange", grads4)
    names = ("w_in", "w_a", "w_b", "w_o", "w_up", "w_down")
    parts16, owns = [], []
    for name, g4, recv in zip(names, grads4, from_sibling):
        p16, own = _pair_sum("grad_pair_sum_" + name, g4, recv, idx, min(128, g4.shape[2]))
        parts16.append(p16)
        owns.append(own)
    from_chips = _chip_exchange("grad_chip_exchange", parts16)
    moments_m = (m_w_in, m_w_a, m_w_b, m_w_o, m_w_up, m_w_down)
    moments_v = (v_w_in, v_w_a, v_w_b, v_w_o, v_w_up, v_w_down)
    big_out = {}
    for name, own, recv, w, m, v in zip(names, owns, from_chips, big, moments_m, moments_v):
        res = _adamw_shard("adamw_" + name, own, recv, w, m[0], v[0], min(128, own.shape[0]))
        big_out[name] = [t[None] for t in res]

    small_g = _pack_small((g_w_sgu, g_b_sgu, g_norm1, g_norm2, g_normf, g_ln_g, g_ln_b, g_bf[:, :N_HEADS]))
    small_all = _all_gather("gather_small_grads", small_g)
    small_w = _pack_small((w_sgu, b_sgu, norm1_g, norm2_g, normf_g, ln_v_g, ln_v_b, b_f))
    small_m = _pack_small((m_w_sgu, m_b_sgu, m_norm1_g, m_norm2_g, m_normf_g, m_ln_v_g, m_ln_v_b, m_b_f))
    small_v = _pack_small((v_w_sgu, v_b_sgu, v_norm1_g, v_norm2_g, v_normf_g, v_ln_v_g, v_ln_v_b, v_b_f))
    small_res = [_unpack_small(t) for t in _adamw_small(small_all, small_w, small_m, small_v)]
    small_names = [n for n, _ in _SMALL]
    small_out = {n: [res[k] for res in small_res] for k, n in enumerate(small_names)}

    loss = lax.psum(loss_part[0, 0], ("x", "y", "c"))

    order = ("norm1_g", "w_in", "b_f", "ln_v_g", "ln_v_b", "w_sgu", "b_sgu", "w_a", "w_b", "w_o", "norm2_g", "w_up",
             "w_down", "normf_g")
    table = {**big_out, **small_out}
    outs = [loss, grad_x[None]]
    for kind in range(4):
        outs += [table[n][kind] for n in order]
    return tuple(outs)
```

```python
import math

import jax
import jax.numpy as jnp
from jax import lax
from jax.experimental import pallas as pl
from jax.experimental.pallas import tpu as pltpu

F32 = jnp.float32
BF16 = jnp.bfloat16

N_DEV = 8
D_MODEL = 1024
N_HEADS = 8
HEAD_DIM = 64
FOX_W = N_HEADS * HEAD_DIM
SGU_G = 8
SGU_W = 512
SGU_LEN = 128
CHUNK = 64
D_FF = 4 * D_MODEL
IN_COLS = 3 * FOX_W + N_HEADS + 2 * SGU_W + 2 * D_MODEL
IN_SHARD = IN_COLS // N_DEV
LANES = 128
QKV_W = 3 * FOX_W
U_OFF, SV_OFF, GA_OFF, GB_OFF, F_OFF = 0, 512, 1024, 2048, 3072
REST_W = F_OFF + LANES
EPS = 1e-6
NEG = -1e30

ADAM_LR = 0.001
ADAM_B1 = 0.9
ADAM_B2 = 0.999
ADAM_EPS = 1e-08
ADAM_WD = 0.01
ADAM_STEP = 10

VMEM_LIMIT = 56 * 1024 * 1024
MESH = pl.DeviceIdType.MESH


def _params(sem=None):
    return pltpu.CompilerParams(dimension_semantics=sem, vmem_limit_bytes=VMEM_LIMIT)


def _dot(a, b):
    return jnp.dot(a, b, preferred_element_type=F32)


def _dot_nt(a, b):
    return lax.dot_general(a, b, (((1,), (1,)), ((), ())), preferred_element_type=F32)


def _dot_tn(a, b):
    return lax.dot_general(a, b, (((0,), (0,)), ((), ())), preferred_element_type=F32)


def _dot_f32(a, b):
    return jnp.dot(a, b, preferred_element_type=F32, precision=lax.Precision.HIGHEST)


def _sigmoid(x):
    return 1.0 / (1.0 + jnp.exp(-x))


def _log_sigmoid(z):
    return jnp.minimum(z, 0.0) - jnp.log(1.0 + jnp.exp(-jnp.abs(z)))


_GELU_K = math.sqrt(2.0 / math.pi)
_GELU_C = 0.044715


def _gelu(x):
    t = jnp.tanh(_GELU_K * (x + _GELU_C * (x * x * x)))
    return 0.5 * x * (1.0 + t)


def _gelu_grad(x):
    x2 = x * x
    t = jnp.tanh(_GELU_K * (x + _GELU_C * (x2 * x)))
    return 0.5 * (1.0 + t) + 0.5 * x * (1.0 - t * t) * (_GELU_K * (1.0 + 3.0 * _GELU_C * x2))


def _rms_bwd(xh, r, g, dy):
    gy = dy * g
    return r * (gy - xh * jnp.mean(xh * gy, axis=-1, keepdims=True))


def _lane_lt64(shape):
    return lax.broadcasted_iota(jnp.int32, shape, len(shape) - 1) < HEAD_DIM


def _all_gather(name, shard):
    def body(x_ref, out_ref, send_sems, recv_sems, local_sem):
        x, y, c = lax.axis_index("x"), lax.axis_index("y"), lax.axis_index("c")
        me, sibling = (x, y, c), (x, y, 1 - c)
        chips = [(1 - x, y), (x, 1 - y), (1 - x, 1 - y)]

        def rows(px, py, pc):
            return out_ref.at[4 * px + 2 * py + pc]

        def copy(k, block, to, src=None):
            return pltpu.make_async_remote_copy(
                src_ref=rows(*block) if src is None else src,
                dst_ref=rows(*block),
                send_sem=send_sems.at[k],
                recv_sem=recv_sems.at[k],
                device_id=to,
                device_id_type=MESH,
            )

        mine = pltpu.make_async_copy(x_ref, rows(*me), local_sem)
        mine.start()
        first = [copy(0, me, sibling, src=x_ref)]
        first += [copy(1 + j, me, (*chip, c), src=x_ref) for j, chip in enumerate(chips)]
        for cp in first:
            cp.start()
        passed = [copy(4 + j, (*chip, c), sibling) for j, chip in enumerate(chips)]
        for j, chip in enumerate(chips):
            copy(1 + j, (*chip, c), me).wait_recv()
            passed[j].start()
        copy(0, sibling, me).wait_recv()
        for j, chip in enumerate(chips):
            copy(4 + j, (*chip, 1 - c), me).wait_recv()
        for cp in first + passed:
            cp.wait_send()
        mine.wait()

    return pl.pallas_call(
        body,
        name=name,
        out_shape=jax.ShapeDtypeStruct((N_DEV,) + shard.shape, shard.dtype),
        in_specs=[pl.BlockSpec(memory_space=pl.ANY)],
        out_specs=pl.BlockSpec(memory_space=pl.ANY),
        scratch_shapes=[pltpu.SemaphoreType.DMA((7,)), pltpu.SemaphoreType.DMA((7,)), pltpu.SemaphoreType.DMA],
    )(shard)


def _pair_exchange(name, grads):
    n = len(grads)

    def body(*refs):
        g_refs, out_refs, send_sems, recv_sems = refs[:n], refs[n:2 * n], refs[2 * n], refs[2 * n + 1]
        x, y, c = lax.axis_index("x"), lax.axis_index("y"), lax.axis_index("c")
        copies = [
            pltpu.make_async_remote_copy(
                src_ref=g_refs[k].at[:, 1 - c],
                dst_ref=out_refs[k],
                send_sem=send_sems.at[k],
                recv_sem=recv_sems.at[k],
                device_id=(x, y, 1 - c),
                device_id_type=MESH,
            )
            for k in range(n)
        ]
        for cp in copies:
            cp.start()
        for cp in copies:
            cp.wait_recv()
        for cp in copies:
            cp.wait_send()

    return pl.pallas_call(
        body,
        name=name,
        out_shape=[jax.ShapeDtypeStruct((4,) + g.shape[2:], g.dtype) for g in grads],
        in_specs=[pl.BlockSpec(memory_space=pl.ANY)] * n,
        out_specs=[pl.BlockSpec(memory_space=pl.ANY)] * n,
        scratch_shapes=[pltpu.SemaphoreType.DMA((n,)), pltpu.SemaphoreType.DMA((n,))],
    )(*grads)


def _chip_exchange(name, parts):
    n = len(parts)

    def body(*refs):
        p_refs, out_refs, send_sems, recv_sems = refs[:n], refs[n:2 * n], refs[2 * n], refs[2 * n + 1]
        x, y, c = lax.axis_index("x"), lax.axis_index("y"), lax.axis_index("c")
        chips = [(1 - x, y), (x, 1 - y), (1 - x, 1 - y)]
        copies = []
        for k in range(n):
            for j, (px, py) in enumerate(chips):
                copies.append(pltpu.make_async_remote_copy(
                    src_ref=p_refs[k].at[2 * px + py],
                    dst_ref=out_refs[k].at[j],
                    send_sem=send_sems.at[3 * k + j],
                    recv_sem=recv_sems.at[3 * k + j],
                    device_id=(px, py, c),
                    device_id_type=MESH,
                ))
        for cp in copies:
            cp.start()
        for cp in copies:
            cp.wait_recv()
        for cp in copies:
            cp.wait_send()

    return pl.pallas_call(
        body,
        name=name,
        out_shape=[jax.ShapeDtypeStruct((3,) + p.shape[1:], p.dtype) for p in parts],
        in_specs=[pl.BlockSpec(memory_space=pl.ANY)] * n,
        out_specs=[pl.BlockSpec(memory_space=pl.ANY)] * n,
        scratch_shapes=[pltpu.SemaphoreType.DMA((3 * n,)), pltpu.SemaphoreType.DMA((3 * n,))],
    )(*parts)


def _mm(name, pairs, extras, outs, epi, *, m, tm, n, tn, arbitrary=False):
    nj = n // tn
    a_arrays, a_specs, b_arrays, b_specs, b_index = [], [], [], [], []
    for a, b, nt, cols in pairs:
        a_arrays.append(a)
        a_specs.append(pl.BlockSpec((tm, a.shape[1]), lambda i, j: (i, 0)))
        known = [k for k, other in enumerate(b_arrays) if other is b]
        if known:
            b_index.append(known[0])
            continue
        b_index.append(len(b_arrays))
        b_arrays.append(b)
        if cols is not None:
            assert nj == 1
            b_specs.append(pl.BlockSpec(b.shape, lambda i, j: (0, 0)))
        elif nt:
            b_specs.append(pl.BlockSpec((tn, b.shape[1]), lambda i, j: (j, 0)))
        else:
            b_specs.append(pl.BlockSpec((b.shape[0], tn), lambda i, j: (0, j)))
    arrays = a_arrays + b_arrays + [arr for arr, _ in extras]
    in_specs = a_specs + b_specs + [spec for _, spec in extras]
    n_a, n_b, n_extras = len(a_arrays), len(b_arrays), len(extras)

    def body(*refs):
        a_refs = refs[:n_a]
        b_refs = refs[n_a:n_a + n_b]
        ex = refs[n_a + n_b:n_a + n_b + n_extras]
        out = refs[n_a + n_b + n_extras:]
        accs = []
        for p, (_, _, nt, cols) in enumerate(pairs):
            av = a_refs[p][...]
            if av.dtype != BF16:
                av = av.astype(BF16)
            b_ref = b_refs[b_index[p]]
            bv = b_ref[...] if cols is None else b_ref[:, cols[0]:cols[1]]
            accs.append(_dot_nt(av, bv) if nt else _dot(av, bv))
        epi(accs, ex, out)

    sem = ("arbitrary", "arbitrary") if arbitrary else ("parallel", "parallel")
    return pl.pallas_call(
        body,
        name=name,
        grid=(m // tm, nj),
        in_specs=in_specs,
        out_specs=[spec for _, spec in outs],
        out_shape=[shape for shape, _ in outs],
        compiler_params=_params(sem),
    )(*arrays)


def _tile(tm, tn, off=0):
    return pl.BlockSpec((tm, tn), lambda i, j: (i, j + off))


def _row(tm, w, blk=0):
    return pl.BlockSpec((tm, w), lambda i, j: (i, blk))


def _whole(shape):
    zeros = (0,) * len(shape)
    return pl.BlockSpec(shape, lambda i, j: zeros)


def _sds(shape, dtype):
    return jax.ShapeDtypeStruct(shape, dtype)


def _mm_tn(name, a, g, *, tk, tn, ts, blocked=False):
    s_len, ka = a.shape
    n = g.shape[1]

    def body(a_ref, g_ref, o_ref):
        part = _dot_tn(a_ref[...].astype(BF16), g_ref[...].astype(BF16))

        @pl.when(pl.program_id(2) == 0)
        def _():
            o_ref[...] = part

        @pl.when(pl.program_id(2) > 0)
        def _():
            o_ref[...] += part

    if blocked:
        out_shape = _sds((n // tn, ka, tn), F32)
        out_spec = pl.BlockSpec((None, tk, tn), lambda i, j, s: (j, i, 0))
    else:
        out_shape = _sds((ka, n), F32)
        out_spec = pl.BlockSpec((tk, tn), lambda i, j, s: (i, j))
    return pl.pallas_call(
        body,
        name=name,
        grid=(ka // tk, n // tn, s_len // ts),
        in_specs=[pl.BlockSpec((ts, tk), lambda i, j, s: (s, i)), pl.BlockSpec((ts, tn), lambda i, j, s: (s, j))],
        out_specs=out_spec,
        out_shape=out_shape,
        compiler_params=_params(("parallel", "parallel", "arbitrary")),
    )(a, g)


def _rms_fwd(name, x, g, tm):
    s_len, d = x.shape

    def body(x_ref, g_ref, h_ref, r_ref):
        xv = x_ref[...]
        r = lax.rsqrt(jnp.mean(xv * xv, axis=-1, keepdims=True) + EPS)
        h_ref[...] = (xv * r * g_ref[...]).astype(BF16)
        r_ref[...] = r

    return pl.pallas_call(
        body,
        name=name,
        grid=(s_len // tm,),
        in_specs=[pl.BlockSpec((tm, d), lambda i: (i, 0)), pl.BlockSpec((1, d), lambda i: (0, 0))],
        out_specs=[pl.BlockSpec((tm, d), lambda i: (i, 0)), pl.BlockSpec((tm, 1), lambda i: (i, 0))],
        out_shape=[_sds((s_len, d), BF16), _sds((s_len, 1), F32)],
        compiler_params=_params(("parallel",)),
    )(x, g)


def _forget_cumsum(prest, b_f_pad, tc):
    s_len = prest.shape[0]

    def body(f_ref, b_ref, c_ref, carry):
        @pl.when(pl.program_id(0) == 0)
        def _():
            carry[...] = jnp.zeros_like(carry)

        logf = _log_sigmoid(f_ref[...] + b_ref[...])
        row = lax.broadcasted_iota(jnp.int32, (tc, tc), 0)
        col = lax.broadcasted_iota(jnp.int32, (tc, tc), 1)
        tri = (row >= col).astype(F32)
        c = _dot_f32(tri, logf) + carry[...]
        c_ref[...] = c
        carry[...] = c[tc - 1:tc, :]

    return pl.pallas_call(
        body,
        name="forget_cumsum",
        grid=(s_len // tc,),
        in_specs=[pl.BlockSpec((tc, LANES), lambda i: (i, F_OFF // LANES)), pl.BlockSpec((1, LANES), lambda i: (0, 0))],
        out_specs=pl.BlockSpec((tc, LANES), lambda i: (i, 0)),
        out_shape=_sds((s_len, LANES), F32),
        scratch_shapes=[pltpu.VMEM((1, LANES), F32)],
        compiler_params=_params(("arbitrary",)),
    )(prest, b_f_pad)


def _causal_t(i, j, tq, tk):
    key = j * tk + lax.broadcasted_iota(jnp.int32, (tk, tq), 0)
    query = i * tq + lax.broadcasted_iota(jnp.int32, (tk, tq), 1)
    return query >= key


def _attn_fwd(qkv, v_t, c_col, c_row, tq, tk):
    s_len = qkv.shape[0]
    nq, nk = s_len // tq, s_len // tk
    ratio = tq // tk

    def last_k(i):
        return (i + 1) * ratio - 1

    def body(q_ref, k_ref, vt_ref, cq_ref, ck_ref, o_ref, lse_ref, acc_t, m_sc, l_sc):
        i, j = pl.program_id(0), pl.program_id(1)

        @pl.when(j == 0)
        def _():
            acc_t[...] = jnp.zeros_like(acc_t)
            m_sc[...] = jnp.full_like(m_sc, NEG)
            l_sc[...] = jnp.zeros_like(l_sc)

        @pl.when(j <= last_k(i))
        def _():
            causal = _causal_t(i, j, tq, tk)
            lt64 = _lane_lt64((tk, LANES))
            top = lax.broadcasted_iota(jnp.int32, (LANES, tq), 0) < HEAD_DIM
            for p in range(N_HEADS // 2):
                lanes = slice(p * LANES, (p + 1) * LANES)
                q_pair = q_ref[:, lanes] * jnp.asarray(HEAD_DIM ** -0.5, BF16)
                k_pair = k_ref[:, lanes]
                vt_pair = vt_ref[lanes, :]
                zero = jnp.zeros_like(k_pair)
                alphas, pvs = [], []
                for half in range(2):
                    h = 2 * p + half
                    sel = lt64 if half == 0 else jnp.logical_not(lt64)
                    st = _dot_nt(jnp.where(sel, k_pair, zero), q_pair) + (cq_ref[h:h + 1, :] - ck_ref[:, h:h + 1])
                    st = jnp.where(causal, st, NEG)
                    m_old = m_sc[h:h + 1, :]
                    m_new = jnp.maximum(m_old, jnp.max(st, axis=0, keepdims=True))
                    alpha = jnp.exp(m_old - m_new)
                    pt = jnp.exp(st - m_new)
                    l_sc[h:h + 1, :] = alpha * l_sc[h:h + 1, :] + jnp.sum(pt, axis=0, keepdims=True)
                    m_sc[h:h + 1, :] = m_new
                    alphas.append(alpha)
                    pvs.append(_dot(vt_pair, pt.astype(BF16)))
                acc_t[lanes, :] = acc_t[lanes, :] * jnp.where(top, alphas[0], alphas[1]) + jnp.where(top, pvs[0], pvs[1])

        @pl.when(j == last_k(i))
        def _():
            top = lax.broadcasted_iota(jnp.int32, (LANES, tq), 0) < HEAD_DIM
            for p in range(N_HEADS // 2):
                lanes = slice(p * LANES, (p + 1) * LANES)
                l_pair = jnp.where(top, l_sc[2 * p:2 * p + 1, :], l_sc[2 * p + 1:2 * p + 2, :])
                o_ref[:, lanes] = jnp.transpose(acc_t[lanes, :] / l_pair)
            lse_ref[...] = m_sc[...] + jnp.log(l_sc[...])

    def k_blk(i, j):
        return jnp.minimum(j, last_k(i))

    return pl.pallas_call(
        body,
        name="attn_fwd",
        grid=(nq, nk),
        in_specs=[
            pl.BlockSpec((tq, FOX_W), lambda i, j: (i, 0)),
            pl.BlockSpec((tk, FOX_W), lambda i, j: (k_blk(i, j), 1)),
            pl.BlockSpec((FOX_W, tk), lambda i, j: (0, k_blk(i, j))),
            pl.BlockSpec((N_HEADS, tq), lambda i, j: (0, i)),
            pl.BlockSpec((tk, LANES), lambda i, j: (k_blk(i, j), 0)),
        ],
        out_specs=[pl.BlockSpec((tq, FOX_W), lambda i, j: (i, 0)), pl.BlockSpec((N_HEADS, tq), lambda i, j: (0, i))],
        out_shape=[_sds((s_len, FOX_W), F32), _sds((N_HEADS, s_len), F32)],
        scratch_shapes=[pltpu.VMEM((FOX_W, tq), F32), pltpu.VMEM((N_HEADS, tq), F32), pltpu.VMEM((N_HEADS, tq), F32)],
        compiler_params=_params(("parallel", "arbitrary")),
    )(qkv, qkv, v_t, c_row, c_col)


def _sgu_mix(vn, w_stack, lt64):
    outs = []
    for p in range(SGU_G // 2):
        r = _dot(w_stack[p], vn[:, p * LANES:(p + 1) * LANES])
        outs.append(jnp.where(lt64, r[:SGU_LEN], r[SGU_LEN:]))
    return jnp.concatenate(outs, axis=1)


def _sgu_norm(sv, ln_g, ln_b):
    svg = _gelu(sv)
    xc = svg - jnp.mean(svg, axis=-1, keepdims=True)
    rstd = lax.rsqrt(jnp.mean(xc * xc, axis=-1, keepdims=True) + EPS)
    xhat = xc * rstd
    return xhat, rstd, xhat * ln_g + ln_b


def _sgu_fwd(prest, ln_g, ln_b, w_stack, b_pair, tm):
    s_len = prest.shape[0]

    def body(u_ref, sv_ref, g_ref, b_ref, w_ref, bp_ref, sg_ref):
        lt64 = _lane_lt64((SGU_LEN, LANES))
        _, _, vn = _sgu_norm(sv_ref[...], g_ref[...], b_ref[...])
        vn = vn.astype(BF16)
        w_stack_v = [w_ref[p] for p in range(SGU_G // 2)]
        for w in range(tm // SGU_LEN):
            win = slice(w * SGU_LEN, (w + 1) * SGU_LEN)
            mixed = _sgu_mix(vn[win], w_stack_v, lt64) + bp_ref[...]
            sg_ref[win, :] = (_gelu(u_ref[win, :]) * mixed).astype(BF16)

    return pl.pallas_call(
        body,
        name="sgu_fwd",
        grid=(s_len // tm,),
        in_specs=[
            pl.BlockSpec((tm, SGU_W), lambda i: (i, U_OFF // SGU_W)),
            pl.BlockSpec((tm, SGU_W), lambda i: (i, SV_OFF // SGU_W)),
            pl.BlockSpec((1, SGU_W), lambda i: (0, 0)),
            pl.BlockSpec((1, SGU_W), lambda i: (0, 0)),
            pl.BlockSpec((SGU_G // 2, 2 * SGU_LEN, SGU_LEN), lambda i: (0, 0, 0)),
            pl.BlockSpec((SGU_LEN, SGU_W), lambda i: (0, 0)),
        ],
        out_specs=pl.BlockSpec((tm, SGU_W), lambda i: (i, 0)),
        out_shape=_sds((s_len, SGU_W), BF16),
        compiler_params=_params(("parallel",)),
    )(prest, prest, ln_g, ln_b, w_stack, b_pair)


def _sgu_bwd(prest, dsg, ln_g, ln_b, w_stack, wt_stack, b_pair, tm):
    s_len = prest.shape[0]
    n_pair = SGU_G // 2

    def body(u_ref, sv_ref, dsg_ref, g_ref, b_ref, w_ref, wt_ref, bp_ref,
             du_ref, dsv_ref, dw_ref, db_ref, dg_ref, dbeta_ref, dvn_sc):
        @pl.when(pl.program_id(0) == 0)
        def _():
            dw_ref[...] = jnp.zeros_like(dw_ref)
            db_ref[...] = jnp.zeros_like(db_ref)
            dg_ref[...] = jnp.zeros_like(dg_ref)
            dbeta_ref[...] = jnp.zeros_like(dbeta_ref)

        lt64 = _lane_lt64((SGU_LEN, LANES))
        sv = sv_ref[...]
        xhat, rstd, vn32 = _sgu_norm(sv, g_ref[...], b_ref[...])
        vn = vn32.astype(BF16)
        w_stack_v = [w_ref[p] for p in range(n_pair)]
        db = jnp.zeros((SGU_LEN, SGU_W), F32)
        for w in range(tm // SGU_LEN):
            win = slice(w * SGU_LEN, (w + 1) * SGU_LEN)
            u = u_ref[win, :]
            dsg_w = dsg_ref[win, :]
            mixed = _sgu_mix(vn[win], w_stack_v, lt64) + bp_ref[...]
            du_ref[win, :] = (dsg_w * mixed * _gelu_grad(u)).astype(BF16)
            dmixed = dsg_w * _gelu(u)
            db = db + dmixed
            dm16 = dmixed.astype(BF16)
            for p in range(n_pair):
                lanes = slice(p * LANES, (p + 1) * LANES)
                dmp = dm16[:, lanes]
                r = _dot(wt_ref[p], dmp)
                dvn_sc[win, lanes] = jnp.where(lt64, r[:SGU_LEN], r[SGU_LEN:])
                zero = jnp.zeros_like(dmp)
                dm_ab = jnp.concatenate([jnp.where(lt64, dmp, zero), jnp.where(lt64, zero, dmp)], axis=0)
                dw_ref[p] += _dot_nt(dm_ab, vn[win, lanes])
        db_ref[...] += db
        dvn = dvn_sc[...]
        dg_ref[...] += jnp.sum(dvn * xhat, axis=0, keepdims=True)
        dbeta_ref[...] += jnp.sum(dvn, axis=0, keepdims=True)
        dxh = dvn * g_ref[...]
        dsvg = rstd * (dxh - jnp.mean(dxh, axis=-1, keepdims=True) - xhat * jnp.mean(dxh * xhat, axis=-1, keepdims=True))
        dsv_ref[...] = (dsvg * _gelu_grad(sv)).astype(BF16)

    const2 = lambda i: (0, 0)
    const3 = lambda i: (0, 0, 0)
    return pl.pallas_call(
        body,
        name="sgu_bwd",
        grid=(s_len // tm,),
        in_specs=[
            pl.BlockSpec((tm, SGU_W), lambda i: (i, U_OFF // SGU_W)),
            pl.BlockSpec((tm, SGU_W), lambda i: (i, SV_OFF // SGU_W)),
            pl.BlockSpec((tm, SGU_W), lambda i: (i, 0)),
            pl.BlockSpec((1, SGU_W), const2),
            pl.BlockSpec((1, SGU_W), const2),
            pl.BlockSpec((n_pair, 2 * SGU_LEN, SGU_LEN), const3),
            pl.BlockSpec((n_pair, 2 * SGU_LEN, SGU_LEN), const3),
            pl.BlockSpec((SGU_LEN, SGU_W), const2),
        ],
        out_specs=[
            pl.BlockSpec((tm, SGU_W), lambda i: (i, 0)),
            pl.BlockSpec((tm, SGU_W), lambda i: (i, 0)),
            pl.BlockSpec((n_pair, 2 * SGU_LEN, SGU_LEN), const3),
            pl.BlockSpec((SGU_LEN, SGU_W), const2),
            pl.BlockSpec((1, SGU_W), const2),
            pl.BlockSpec((1, SGU_W), const2),
        ],
        out_shape=[
            _sds((s_len, SGU_W), BF16), _sds((s_len, SGU_W), BF16), _sds((n_pair, 2 * SGU_LEN, SGU_LEN), F32),
            _sds((SGU_LEN, SGU_W), F32), _sds((1, SGU_W), F32), _sds((1, SGU_W), F32),
        ],
        scratch_shapes=[pltpu.VMEM((tm, SGU_W), F32)],
        compiler_params=_params(("arbitrary",)),
    )(prest, prest, dsg, ln_g, ln_b, w_stack, wt_stack, b_pair)


def _attn_bwd(qkv, k_t, do, c_col, c_row, lse_row, delta_row, tq, tk):
    s_len = qkv.shape[0]
    nq, nk = s_len // tq, s_len // tk
    ratio = tq // tk
    scale = HEAD_DIM ** -0.5

    def last_k(i):
        return (i + 1) * ratio - 1

    def first_q(j):
        return j // ratio

    def body(q_ref, k_ref, v_ref, kt_ref, do_ref, cq_ref, ck_ref, lse_ref, dl_ref,
             dq_ref, dk_ref, dv_ref, dcr_ref, dcc_ref, dq_t, dk_acc, dv_acc, dcc_acc):
        j, i = pl.program_id(0), pl.program_id(1)

        @pl.when(jnp.logical_and(j == 0, i == 0))
        def _():
            dq_t[...] = jnp.zeros_like(dq_t)
            dcr_ref[...] = jnp.zeros_like(dcr_ref)

        @pl.when(i == 0)
        def _():
            dk_acc[...] = jnp.zeros_like(dk_acc)
            dv_acc[...] = jnp.zeros_like(dv_acc)
            dcc_acc[...] = jnp.zeros_like(dcc_acc)

        @pl.when(i >= first_q(j))
        def _():
            causal = _causal_t(i, j, tq, tk)
            lt64 = _lane_lt64((tk, LANES))
            lane = lax.broadcasted_iota(jnp.int32, (tk, LANES), 1)
            top = lax.broadcasted_iota(jnp.int32, (LANES, tq), 0) < HEAD_DIM
            dcc = jnp.zeros((tk, LANES), F32)
            for p in range(N_HEADS // 2):
                lanes = slice(p * LANES, (p + 1) * LANES)
                q_pair = q_ref[:, lanes] * jnp.asarray(scale, BF16)
                k_pair = k_ref[:, lanes]
                v_pair = v_ref[:, lanes]
                kt_pair = kt_ref[lanes, :]
                do_pair = do_ref[:, lanes].astype(BF16)
                zero = jnp.zeros_like(k_pair)
                dks, dvs, dqs = [], [], []
                for half in range(2):
                    h = 2 * p + half
                    sel = lt64 if half == 0 else jnp.logical_not(lt64)
                    st = _dot_nt(jnp.where(sel, k_pair, zero), q_pair) + (cq_ref[h:h + 1, :] - ck_ref[:, h:h + 1])
                    st = jnp.where(causal, st, NEG)
                    pt = jnp.exp(st - lse_ref[h:h + 1, :])
                    dvs.append(_dot(pt.astype(BF16), do_pair))
                    dpt = _dot_nt(jnp.where(sel, v_pair, zero), do_pair)
                    dst = pt * (dpt - dl_ref[h:h + 1, :])
                    dst16 = dst.astype(BF16)
                    dks.append(_dot(dst16, q_pair))
                    dqs.append(_dot(kt_pair, dst16))
                    dcc = jnp.where(lane == h, -jnp.sum(dst, axis=1, keepdims=True), dcc)
                    dcr_ref[i, h:h + 1, :] += jnp.sum(dst, axis=0, keepdims=True)
                dk_acc[:, lanes] += jnp.where(lt64, dks[0], dks[1])
                dv_acc[:, lanes] += jnp.where(lt64, dvs[0], dvs[1])
                dq_t[i, lanes, :] += jnp.where(top, dqs[0], dqs[1])
            dcc_acc[...] += dcc

        @pl.when(j == last_k(i))
        def _():
            rows = pl.ds(pl.multiple_of(i * tq, tq), tq)
            for p in range(N_HEADS // 2):
                lanes = slice(p * LANES, (p + 1) * LANES)
                dq_ref[rows, lanes] = (jnp.transpose(dq_t[i, lanes, :]) * scale).astype(BF16)

        @pl.when(i == nq - 1)
        def _():
            dk_ref[...] = dk_acc[...].astype(BF16)
            dv_ref[...] = dv_acc[...].astype(BF16)
            dcc_ref[...] = dcc_acc[...]

    def q_blk(j, i):
        return jnp.maximum(i, first_q(j))

    q_map = lambda j, i: (q_blk(j, i), 0)
    q_row_map = lambda j, i: (0, q_blk(j, i))
    k_row = lambda j, i: (j, 0)
    return pl.pallas_call(
        body,
        name="attn_bwd",
        grid=(nk, nq),
        in_specs=[
            pl.BlockSpec((tq, FOX_W), q_map),
            pl.BlockSpec((tk, FOX_W), lambda j, i: (j, 1)),
            pl.BlockSpec((tk, FOX_W), lambda j, i: (j, 2)),
            pl.BlockSpec((FOX_W, tk), lambda j, i: (0, j)),
            pl.BlockSpec((tq, FOX_W), q_map),
            pl.BlockSpec((N_HEADS, tq), q_row_map),
            pl.BlockSpec((tk, LANES), k_row),
            pl.BlockSpec((N_HEADS, tq), q_row_map),
            pl.BlockSpec((N_HEADS, tq), q_row_map),
        ],
        out_specs=[
            pl.BlockSpec((s_len, FOX_W), lambda j, i: (0, 0)),
            pl.BlockSpec((tk, FOX_W), k_row),
            pl.BlockSpec((tk, FOX_W), k_row),
            pl.BlockSpec((nq, N_HEADS, tq), lambda j, i: (0, 0, 0)),
            pl.BlockSpec((tk, LANES), k_row),
        ],
        out_shape=[_sds((s_len, FOX_W), BF16), _sds((s_len, FOX_W), BF16), _sds((s_len, FOX_W), BF16),
                   _sds((nq, N_HEADS, tq), F32), _sds((s_len, LANES), F32)],
        scratch_shapes=[pltpu.VMEM((nq, FOX_W, tq), F32), pltpu.VMEM((tk, FOX_W), F32), pltpu.VMEM((tk, FOX_W), F32),
                        pltpu.VMEM((tk, LANES), F32)],
        compiler_params=_params(("arbitrary", "arbitrary")),
    )(qkv, qkv, qkv, k_t, do, c_row, c_col, lse_row, delta_row)


def _forget_bwd(dc_rows, dc_cols, prest, b_f_pad, tc):
    s_len = dc_rows.shape[0]
    nb = s_len // tc

    def body(dcr_ref, dc_ref, f_ref, b_ref, df_ref, db_ref, carry):
        @pl.when(pl.program_id(0) == 0)
        def _():
            carry[...] = jnp.zeros_like(carry)
            db_ref[...] = jnp.zeros_like(db_ref)

        row = lax.broadcasted_iota(jnp.int32, (tc, tc), 0)
        col = lax.broadcasted_iota(jnp.int32, (tc, tc), 1)
        tri = (row <= col).astype(F32)
        dlogf = _dot_f32(tri, dcr_ref[...] + dc_ref[...]) + carry[...]
        carry[...] = dlogf[0:1, :]
        z = f_ref[...] + b_ref[...]
        lane = lax.broadcasted_iota(jnp.int32, (tc, LANES), 1)
        dz = jnp.where(lane < N_HEADS, dlogf * _sigmoid(-z), 0.0)
        df_ref[...] = dz.astype(BF16)
        db_ref[...] += jnp.sum(dz, axis=0, keepdims=True)

    rev = lambda i: (nb - 1 - i, 0)
    return pl.pallas_call(
        body,
        name="forget_bwd",
        grid=(nb,),
        in_specs=[
            pl.BlockSpec((tc, LANES), rev),
            pl.BlockSpec((tc, LANES), rev),
            pl.BlockSpec((tc, LANES), lambda i: (nb - 1 - i, F_OFF // LANES)),
            pl.BlockSpec((1, LANES), lambda i: (0, 0)),
        ],
        out_specs=[pl.BlockSpec((tc, LANES), rev), pl.BlockSpec((1, LANES), lambda i: (0, 0))],
        out_shape=[_sds((s_len, LANES), BF16), _sds((1, LANES), F32)],
        scratch_shapes=[pltpu.VMEM((1, LANES), F32)],
        compiler_params=_params(("arbitrary",)),
    )(dc_rows, dc_cols, prest, b_f_pad)


def _pair_sum(name, g4, recv, idx, tr):
    _, _, r, c = g4.shape

    def body(idx_ref, g_ref, r_ref, p16_ref, own_ref):
        k = pl.program_id(1)
        s = g_ref[...] + r_ref[...]
        p16_ref[...] = s.astype(BF16)

        @pl.when(k == idx_ref[1])
        def _():
            own_ref[...] = s

    return pl.pallas_call(
        body,
        name=name,
        grid_spec=pltpu.PrefetchScalarGridSpec(
            num_scalar_prefetch=1,
            grid=(r // tr, 4),
            in_specs=[
                pl.BlockSpec((None, None, tr, c), lambda i, k, idx: (k, idx[0], i, 0)),
                pl.BlockSpec((None, tr, c), lambda i, k, idx: (k, i, 0)),
            ],
            out_specs=[
                pl.BlockSpec((None, tr, c), lambda i, k, idx: (k, i, 0)),
                pl.BlockSpec((tr, c), lambda i, k, idx: (i, 0)),
            ],
        ),
        out_shape=[_sds((4, r, c), BF16), _sds((r, c), F32)],
        compiler_params=_params(("parallel", "arbitrary")),
    )(idx, g4, recv)


def _adamw_math(w, g, m, v):
    m2 = ADAM_B1 * m + (1.0 - ADAM_B1) * g
    v2 = ADAM_B2 * v + (1.0 - ADAM_B2) * (g * g)
    m_hat = m2 / (1.0 - ADAM_B1 ** ADAM_STEP)
    v_hat = v2 / (1.0 - ADAM_B2 ** ADAM_STEP)
    delta = -ADAM_LR * (m_hat / (jnp.sqrt(v_hat) + ADAM_EPS) + ADAM_WD * w)
    return delta, m2, v2


def _adamw_shard(name, own, recv, w, m, v, tr):
    r, c = own.shape

    def body(own_ref, recv_ref, w_ref, m_ref, v_ref, g_ref, d_ref, m2_ref, v2_ref):
        g = own_ref[...]
        for k in range(3):
            g = g + recv_ref[k].astype(F32)
        delta, m2, v2 = _adamw_math(w_ref[...], g, m_ref[...], v_ref[...])
        g_ref[...] = g
        d_ref[...] = delta
        m2_ref[...] = m2
        v2_ref[...] = v2

    spec = pl.BlockSpec((tr, c), lambda i: (i, 0))
    return pl.pallas_call(
        body,
        name=name,
        grid=(r // tr,),
        in_specs=[spec, pl.BlockSpec((3, tr, c), lambda i: (0, i, 0)), spec, spec, spec],
        out_specs=[spec] * 4,
        out_shape=[_sds((r, c), F32)] * 4,
        compiler_params=_params(("parallel",)),
    )(own, recv, w, m, v)


def _adamw_small(gathered, w, m, v):
    _, r, _ = gathered.shape

    def body(ga_ref, w_ref, m_ref, v_ref, g_ref, d_ref, m2_ref, v2_ref):
        g = ga_ref[0]
        for k in range(1, N_DEV):
            g = g + ga_ref[k]
        delta, m2, v2 = _adamw_math(w_ref[...], g, m_ref[...], v_ref[...])
        g_ref[...] = g
        d_ref[...] = delta
        m2_ref[...] = m2
        v2_ref[...] = v2

    spec = pl.BlockSpec((r, LANES), lambda i: (0, 0))
    return pl.pallas_call(
        body,
        name="adamw_small",
        grid=(1,),
        in_specs=[pl.BlockSpec((N_DEV, r, LANES), lambda i: (0, 0, 0)), spec, spec, spec],
        out_specs=[spec] * 4,
        out_shape=[_sds((r, LANES), F32)] * 4,
        compiler_params=_params(("arbitrary",)),
    )(gathered, w, m, v)


_SMALL = (("w_sgu", (1, SGU_G, SGU_LEN, SGU_LEN)), ("b_sgu", (1, SGU_G, SGU_LEN)), ("norm1_g", (1, D_MODEL)),
          ("norm2_g", (1, D_MODEL)), ("normf_g", (D_MODEL,)), ("ln_v_g", (1, SGU_W)), ("ln_v_b", (1, SGU_W)),
          ("b_f", (1, N_HEADS)))


def _pack_small(values):
    rows = []
    for val in values:
        flat = val.reshape(-1).astype(F32)
        pad = (-flat.shape[0]) % LANES
        rows.append(jnp.pad(flat, (0, pad)).reshape(-1, LANES))
    packed = jnp.concatenate(rows, axis=0)
    return jnp.pad(packed, ((0, (-packed.shape[0]) % 8), (0, 0)))


def _unpack_small(packed):
    out, row = [], 0
    for _, shape in _SMALL:
        size = math.prod(shape)
        n_rows = -(-size // LANES)
        out.append(packed[row:row + n_rows].reshape(-1)[:size].reshape(shape))
        row += n_rows
    return out


def kernel(x, norm1_g, w_in, b_f, ln_v_g, ln_v_b, w_sgu, b_sgu, w_a, w_b, w_o, norm2_g, w_up, w_down, normf_g, loss_target, m_norm1_g, m_w_in, m_b_f, m_ln_v_g, m_ln_v_b, m_w_sgu, m_b_sgu, m_w_a, m_w_b, m_w_o, m_norm2_g, m_w_up, m_w_down, m_normf_g, v_norm1_g, v_w_in, v_b_f, v_ln_v_g, v_ln_v_b, v_w_sgu, v_b_sgu, v_w_a, v_w_b, v_w_o, v_norm2_g, v_w_up, v_w_down, v_normf_g):
    xs = x[0]
    target = loss_target[0]
    s_len, d = xs.shape
    tm = min(512, s_len)
    tr = min(256, s_len)
    ta = min(256, s_len)

    big = (w_in[0], w_a[0], w_b[0], w_o[0], w_up[0], w_down[0])
    flat = jnp.concatenate([w.reshape(-1).astype(BF16) for w in big]).reshape(-1, D_MODEL)
    gathered = _all_gather("gather_weights", flat)
    row = 0
    full = []
    for w, col_sharded in zip(big, (True, True, True, False, True, False)):
        n_rows = w.size // D_MODEL
        blk = gathered[:, row:row + n_rows].reshape((N_DEV,) + w.shape)
        row += n_rows
        if col_sharded:
            full.append(jnp.transpose(blk, (1, 0, 2)).reshape(w.shape[0], N_DEV * w.shape[1]))
        else:
            full.append(blk.reshape(N_DEV * w.shape[0], w.shape[1]))
    w_in_f, w_a_f, w_b_f, w_o_f, w_up_f, w_down_f = full
    w_qkv = w_in_f[:, :QKV_W]
    f_lo = QKV_W
    u_lo = f_lo + N_HEADS
    w_rest = jnp.concatenate([w_in_f[:, u_lo:], jnp.pad(w_in_f[:, f_lo:u_lo], ((0, 0), (0, LANES - N_HEADS)))], axis=1)

    chunk_id = jnp.arange(SGU_LEN) // CHUNK
    sgu_mask = chunk_id[None, :] <= chunk_id[:, None]
    w_masked = jnp.where(sgu_mask[None], w_sgu[0], 0.0)
    w_stack = w_masked.reshape(SGU_G // 2, 2 * SGU_LEN, SGU_LEN).astype(BF16)
    wt_stack = jnp.transpose(w_masked, (0, 2, 1)).reshape(SGU_G // 2, 2 * SGU_LEN, SGU_LEN).astype(BF16)
    b_pair = jnp.transpose(jnp.repeat(b_sgu[0], SGU_W // SGU_G, axis=0))
    b_f_pad = jnp.pad(b_f, ((0, 0), (0, LANES - N_HEADS)))
    head_sel = (jnp.arange(FOX_W)[:, None] // HEAD_DIM == jnp.arange(LANES)[None, :]).astype(F32)

    h, r1 = _rms_fwd("rms1", xs, norm1_g, tm)

    def store(dtype):
        def epi(accs, ex, out):
            out[0][...] = accs[0].astype(dtype)
        return epi

    (qkv,) = _mm("proj_qkv", [(h, w_qkv, False, None)], [], [(_sds((s_len, QKV_W), BF16), _tile(tm, 512))],
                 store(BF16), m=s_len, tm=tm, n=QKV_W, tn=512)
    (prest,) = _mm("proj_rest", [(h, w_rest, False, None)], [], [(_sds((s_len, REST_W), F32), _tile(tm, 640))],
                   store(F32), m=s_len, tm=tm, n=REST_W, tn=640)

    c_col = _forget_cumsum(prest, b_f_pad, ta)
    c_row = jnp.transpose(c_col[:, :N_HEADS])
    k_t = jnp.transpose(qkv[:, FOX_W:2 * FOX_W])
    v_t = jnp.transpose(qkv[:, 2 * FOX_W:])
    o, lse_row = _attn_fwd(qkv, v_t, c_col, c_row, ta, ta)
    sg = _sgu_fwd(prest, ln_v_g, ln_v_b, w_stack, b_pair, tm)

    def merge_epi(accs, ex, out):
        ya, yb = accs
        sa, sb = _sigmoid(ex[0][...]), _sigmoid(ex[1][...])
        out[0][...] = (sa * ya + sb * yb).astype(BF16)
        out[1][...] = ya
        out[2][...] = yb

    merged, ya, yb = _mm(
        "merge", [(o, w_a_f, False, None), (sg, w_b_f, False, None)],
        [(prest, _tile(tm, 512, GA_OFF // 512)), (prest, _tile(tm, 512, GB_OFF // 512))],
        [(_sds((s_len, d), BF16), _tile(tm, 512)), (_sds((s_len, d), F32), _tile(tm, 512)), (_sds((s_len, d), F32), _tile(tm, 512))],
        merge_epi, m=s_len, tm=tm, n=d, tn=512)

    def resid_epi(accs, ex, out):
        out[0][...] = ex[0][...] + accs[0]

    (x1,) = _mm("out_proj", [(merged, w_o_f, False, None)], [(xs, _tile(tm, 512))],
                [(_sds((s_len, d), F32), _tile(tm, 512))], resid_epi, m=s_len, tm=tm, n=d, tn=512)

    h2, r2 = _rms_fwd("rms2", x1, norm2_g, tm)

    def up_epi(accs, ex, out):
        a = accs[0]
        out[0][...] = a
        out[1][...] = jnp.square(jnp.maximum(a, 0.0)).astype(BF16)

    a_up, act = _mm("mlp_up", [(h2, w_up_f, False, None)], [],
                    [(_sds((s_len, D_FF), F32), _tile(tm, 512)), (_sds((s_len, D_FF), BF16), _tile(tm, 512))],
                    up_epi, m=s_len, tm=tm, n=D_FF, tn=512)

    def first_step():
        return jnp.logical_and(pl.program_id(0) == 0, pl.program_id(1) == 0)

    def accumulate(ref, val):
        @pl.when(first_step())
        def _():
            ref[...] = val

        @pl.when(jnp.logical_not(first_step()))
        def _():
            ref[...] += val

    def final_epi(accs, ex, out):
        x1_ref, t_ref, g_ref = ex
        x2 = x1_ref[...] + accs[0]
        rf = lax.rsqrt(jnp.mean(x2 * x2, axis=-1, keepdims=True) + EPS)
        xh = x2 * rf
        gf = g_ref[...]
        err = xh * gf - t_ref[...]
        dy = err * (1.0 / d)
        out[0][...] = _rms_bwd(xh, rf, gf, dy)
        accumulate(out[1], jnp.sum(dy * xh, axis=0, keepdims=True))
        part = 0.5 * jnp.sum(jnp.sum(err * err, axis=-1, keepdims=True) * (1.0 / d), axis=0, keepdims=True)
        accumulate(out[2], jnp.broadcast_to(part, (1, LANES)))

    gf2 = normf_g.reshape(1, d)
    dx2, g_normf, loss_part = _mm(
        "mlp_down_loss", [(act, w_down_f, False, None)],
        [(x1, _row(tr, d)), (target, _row(tr, d)), (gf2, _whole((1, d)))],
        [(_sds((s_len, d), F32), _row(tr, d)), (_sds((1, d), F32), _whole((1, d))), (_sds((1, LANES), F32), _whole((1, LANES)))],
        final_epi, m=s_len, tm=tr, n=d, tn=d, arbitrary=True)

    def dact_epi(accs, ex, out):
        out[0][...] = (accs[0] * (2.0 * jnp.maximum(ex[0][...], 0.0))).astype(BF16)

    (da,) = _mm("mlp_down_bwd", [(dx2, w_down_f, True, None)], [(a_up, _tile(tm, 512))],
                [(_sds((s_len, D_FF), BF16), _tile(tm, 512))], dact_epi, m=s_len, tm=tm, n=D_FF, tn=512)
    ts = min(512, s_len)
    g_down = _mm_tn("grad_w_down", act, dx2, tk=1024, tn=d, ts=ts)
    g_up = _mm_tn("grad_w_up", h2, da, tk=d, tn=D_FF // N_DEV, ts=ts, blocked=True)

    def dh2_epi(accs, ex, out):
        x1_ref, r_ref, g_ref, dx2_ref = ex
        r = r_ref[...]
        xh = x1_ref[...] * r
        dh2 = accs[0]
        out[0][...] = dx2_ref[...] + _rms_bwd(xh, r, g_ref[...], dh2)
        accumulate(out[1], jnp.sum(dh2 * xh, axis=0, keepdims=True))

    dx1, g_norm2 = _mm(
        "mlp_up_bwd", [(da, w_up_f, True, None)],
        [(x1, _row(tr, d)), (r2, _row(tr, 1)), (norm2_g, _whole((1, d))), (dx2, _row(tr, d))],
        [(_sds((s_len, d), F32), _row(tr, d)), (_sds((1, d), F32), _whole((1, d)))],
        dh2_epi, m=s_len, tm=tr, n=d, tn=d, arbitrary=True)

    def dmerge_epi(accs, ex, out):
        dm = accs[0]
        sa, sb = _sigmoid(ex[0][...]), _sigmoid(ex[1][...])
        out[0][...] = (dm * sa).astype(BF16)
        out[1][...] = (dm * sb).astype(BF16)
        out[2][...] = (dm * ex[2][...] * sa * (1.0 - sa)).astype(BF16)
        out[3][...] = (dm * ex[3][...] * sb * (1.0 - sb)).astype(BF16)

    dya, dyb, dga, dgb = _mm(
        "out_proj_bwd", [(dx1, w_o_f, True, None)],
        [(prest, _tile(tm, 512, GA_OFF // 512)), (prest, _tile(tm, 512, GB_OFF // 512)), (ya, _tile(tm, 512)), (yb, _tile(tm, 512))],
        [(_sds((s_len, d), BF16), _tile(tm, 512))] * 4, dmerge_epi, m=s_len, tm=tm, n=d, tn=512)
    g_o = _mm_tn("grad_w_o", merged, dx1, tk=512, tn=d, ts=ts).reshape(N_DEV, d // N_DEV, d)
    def col_blocks(g):
        return jnp.transpose(g.reshape(g.shape[0], N_DEV, g.shape[1] // N_DEV), (1, 0, 2))

    g_a = col_blocks(_mm_tn("grad_w_a", o, dya, tk=FOX_W, tn=d, ts=ts))
    g_b = col_blocks(_mm_tn("grad_w_b", sg, dyb, tk=SGU_W, tn=d, ts=ts))

    def do_epi(accs, ex, out):
        do = accs[0]
        out[0][...] = do
        out[1][...] = _dot_f32(do * ex[0][...], ex[1][...])

    do, delta = _mm(
        "attn_out_bwd", [(dya, w_a_f, True, None)], [(o, _row(tm, FOX_W)), (head_sel, _whole((FOX_W, LANES)))],
        [(_sds((s_len, FOX_W), F32), _row(tm, FOX_W)), (_sds((s_len, LANES), F32), _row(tm, LANES))],
        do_epi, m=s_len, tm=tm, n=FOX_W, tn=FOX_W)
    (dsg,) = _mm("sgu_out_bwd", [(dyb, w_b_f, True, None)], [], [(_sds((s_len, SGU_W), F32), _tile(tm, SGU_W))],
                 store(F32), m=s_len, tm=tm, n=SGU_W, tn=SGU_W)

    du, dsv, dw_pairs, db_pos, g_ln_g, g_ln_b = _sgu_bwd(prest, dsg, ln_v_g, ln_v_b, w_stack, wt_stack, b_pair, tm)
    g_w_sgu = jnp.where(sgu_mask[None], dw_pairs.reshape(SGU_G, SGU_LEN, SGU_LEN), 0.0)
    g_b_sgu = jnp.transpose(jnp.sum(db_pos.reshape(SGU_LEN, SGU_G, SGU_W // SGU_G), axis=-1))

    delta_row = jnp.transpose(delta[:, :N_HEADS])
    dq, dk, dv, dc_rows_blk, dc_cols = _attn_bwd(qkv, k_t, do, c_col, c_row, lse_row, delta_row, ta, ta)
    dc_rows = jnp.pad(jnp.transpose(dc_rows_blk, (0, 2, 1)).reshape(s_len, N_HEADS), ((0, 0), (0, LANES - N_HEADS)))
    dfl, g_bf = _forget_bwd(dc_rows, dc_cols, prest, b_f_pad, ta)

    dp = (dq, dk, dv, du, dsv, dga, dgb)
    g_in_cols = [_mm_tn("grad_w_in_%d" % k, h, seg, tk=d, tn=512, ts=ts) for k, seg in enumerate(dp)]
    g_f = _mm_tn("grad_w_in_f", h, dfl, tk=d, tn=LANES, ts=ts)[:, :N_HEADS]
    g_in_full = jnp.concatenate(g_in_cols[:3] + [g_f] + g_in_cols[3:], axis=1)
    g_in = jnp.transpose(g_in_full.reshape(d, N_DEV, IN_SHARD), (1, 0, 2))

    def dx_epi(accs, ex, out):
        x_ref, r_ref, g_ref, dx1_ref = ex
        dh = accs[0]
        for extra in accs[1:]:
            dh = dh + extra
        r = r_ref[...]
        xh = x_ref[...] * r
        out[0][...] = dx1_ref[...] + _rms_bwd(xh, r, g_ref[...], dh)
        accumulate(out[1], jnp.sum(dh * xh, axis=0, keepdims=True))

    rest_cols = ((du, U_OFF, 512), (dsv, SV_OFF, 512), (dga, GA_OFF, 1024), (dgb, GB_OFF, 1024), (dfl, F_OFF, LANES))
    dx_pairs = [(seg, w_qkv, True, (512 * k, 512 * (k + 1))) for k, seg in enumerate((dq, dk, dv))]
    dx_pairs += [(seg, w_rest, True, (lo, lo + width)) for seg, lo, width in rest_cols]
    grad_x, g_norm1 = _mm(
        "proj_bwd", dx_pairs,
        [(xs, _row(tr, d)), (r1, _row(tr, 1)), (norm1_g, _whole((1, d))), (dx1, _row(tr, d))],
        [(_sds((s_len, d), F32), _row(tr, d)), (_sds((1, d), F32), _whole((1, d)))],
        dx_epi, m=s_len, tm=tr, n=d, tn=d, arbitrary=True)

    my_c = lax.axis_index("c")
    my_chip = 2 * lax.axis_index("x") + lax.axis_index("y")
    idx = jnp.stack([my_c, my_chip]).astype(jnp.int32)
    grads8 = (g_in, g_a, g_b, g_o, g_up, g_down.reshape(N_DEV, D_FF // N_DEV, d))
    grads4 = [g.reshape((4, 2) + g.shape[1:]) for g in grads8]
    from_sibling = _pair_exchange("grad_pair_exchange", grads4)
    names = ("w_in", "w_a", "w_b", "w_o", "w_up", "w_down")
    parts16, owns = [], []
    for name, g4, recv in zip(names, grads4, from_sibling):
        p16, own = _pair_sum("grad_pair_sum_" + name, g4, recv, idx, min(128, g4.shape[2]))
        parts16.append(p16)
        owns.append(own)
    from_chips = _chip_exchange("grad_chip_exchange", parts16)
    moments_m = (m_w_in, m_w_a, m_w_b, m_w_o, m_w_up, m_w_down)
    moments_v = (v_w_in, v_w_a, v_w_b, v_w_o, v_w_up, v_w_down)
    big_out = {}
    for name, own, recv, w, m, v in zip(names, owns, from_chips, big, moments_m, moments_v):
        res = _adamw_shard("adamw_" + name, own, recv, w, m[0], v[0], min(128, own.shape[0]))
        big_out[name] = [t[None] for t in res]

    small_g = _pack_small((g_w_sgu, g_b_sgu, g_norm1, g_norm2, g_normf, g_ln_g, g_ln_b, g_bf[:, :N_HEADS]))
    small_all = _all_gather("gather_small_grads", small_g)
    small_w = _pack_small((w_sgu, b_sgu, norm1_g, norm2_g, normf_g, ln_v_g, ln_v_b, b_f))
    small_m = _pack_small((m_w_sgu, m_b_sgu, m_norm1_g, m_norm2_g, m_normf_g, m_ln_v_g, m_ln_v_b, m_b_f))
    small_v = _pack_small((v_w_sgu, v_b_sgu, v_norm1_g, v_norm2_g, v_normf_g, v_ln_v_g, v_ln_v_b, v_b_f))
    small_res = [_unpack_small(t) for t in _adamw_small(small_all, small_w, small_m, small_v)]
    small_names = [n for n, _ in _SMALL]
    small_out = {n: [res[k] for res in small_res] for k, n in enumerate(small_names)}

    loss = lax.psum(loss_part[0, 0], ("x", "y", "c"))

    order = ("norm1_g", "w_in", "b_f", "ln_v_g", "ln_v_b", "w_sgu", "b_sgu", "w_a", "w_b", "w_o", "norm2_g", "w_up",
             "w_down", "normf_g")
    table = {**big_out, **small_out}
    outs = [loss, grad_x[None]]
    for kind in range(4):
        outs += [table[n][kind] for n in order]
    return tuple(outs)
```

```python
import math

import jax
import jax.numpy as jnp
from jax import lax
from jax.experimental import pallas as pl
from jax.experimental.pallas import tpu as pltpu

F32 = jnp.float32
BF16 = jnp.bfloat16

N_DEV = 8
D_MODEL = 1024
N_HEADS = 8
HEAD_DIM = 64
FOX_W = N_HEADS * HEAD_DIM
SGU_G = 8
SGU_W = 512
SGU_LEN = 128
CHUNK = 64
D_FF = 4 * D_MODEL
IN_COLS = 3 * FOX_W + N_HEADS + 2 * SGU_W + 2 * D_MODEL
IN_SHARD = IN_COLS // N_DEV
LANES = 128
QKV_W = 3 * FOX_W
U_OFF, SV_OFF, GA_OFF, GB_OFF, F_OFF = 0, 512, 1024, 2048, 3072
REST_W = F_OFF + LANES
EPS = 1e-6
NEG = -1e30

ADAM_LR = 0.001
ADAM_B1 = 0.9
ADAM_B2 = 0.999
ADAM_EPS = 1e-08
ADAM_WD = 0.01
ADAM_STEP = 10

VMEM_LIMIT = 56 * 1024 * 1024
MESH = pl.DeviceIdType.MESH


def _params(sem=None):
    return pltpu.CompilerParams(dimension_semantics=sem, vmem_limit_bytes=VMEM_LIMIT)


def _dot(a, b):
    return jnp.dot(a, b, preferred_element_type=F32)


def _dot_nt(a, b):
    return lax.dot_general(a, b, (((1,), (1,)), ((), ())), preferred_element_type=F32)


def _dot_tn(a, b):
    return lax.dot_general(a, b, (((0,), (0,)), ((), ())), preferred_element_type=F32)


def _dot_f32(a, b):
    return jnp.dot(a, b, preferred_element_type=F32, precision=lax.Precision.HIGHEST)


def _sigmoid(x):
    return 1.0 / (1.0 + jnp.exp(-x))


def _log_sigmoid(z):
    return jnp.minimum(z, 0.0) - jnp.log(1.0 + jnp.exp(-jnp.abs(z)))


_GELU_K = math.sqrt(2.0 / math.pi)
_GELU_C = 0.044715


def _gelu(x):
    t = jnp.tanh(_GELU_K * (x + _GELU_C * (x * x * x)))
    return 0.5 * x * (1.0 + t)


def _gelu_grad(x):
    x2 = x * x
    t = jnp.tanh(_GELU_K * (x + _GELU_C * (x2 * x)))
    return 0.5 * (1.0 + t) + 0.5 * x * (1.0 - t * t) * (_GELU_K * (1.0 + 3.0 * _GELU_C * x2))


def _rms_bwd(xh, r, g, dy):
    gy = dy * g
    return r * (gy - xh * jnp.mean(xh * gy, axis=-1, keepdims=True))


def _lane_lt64(shape):
    return lax.broadcasted_iota(jnp.int32, shape, len(shape) - 1) < HEAD_DIM


def _all_gather(name, shard):
    def body(x_ref, out_ref, send_sems, recv_sems, local_sem):
        x, y, c = lax.axis_index("x"), lax.axis_index("y"), lax.axis_index("c")
        me, sibling = (x, y, c), (x, y, 1 - c)
        chips = [(1 - x, y), (x, 1 - y), (1 - x, 1 - y)]

        def rows(px, py, pc):
            return out_ref.at[4 * px + 2 * py + pc]

        def copy(k, block, to, src=None):
            return pltpu.make_async_remote_copy(
                src_ref=rows(*block) if src is None else src,
                dst_ref=rows(*block),
                send_sem=send_sems.at[k],
                recv_sem=recv_sems.at[k],
                device_id=to,
                device_id_type=MESH,
            )

        mine = pltpu.make_async_copy(x_ref, rows(*me), local_sem)
        mine.start()
        first = [copy(0, me, sibling, src=x_ref)]
        first += [copy(1 + j, me, (*chip, c), src=x_ref) for j, chip in enumerate(chips)]
        for cp in first:
            cp.start()
        passed = [copy(4 + j, (*chip, c), sibling) for j, chip in enumerate(chips)]
        for j, chip in enumerate(chips):
            copy(1 + j, (*chip, c), me).wait_recv()
            passed[j].start()
        copy(0, sibling, me).wait_recv()
        for j, chip in enumerate(chips):
            copy(4 + j, (*chip, 1 - c), me).wait_recv()
        for cp in first + passed:
            cp.wait_send()
        mine.wait()

    return pl.pallas_call(
        body,
        name=name,
        out_shape=jax.ShapeDtypeStruct((N_DEV,) + shard.shape, shard.dtype),
        in_specs=[pl.BlockSpec(memory_space=pl.ANY)],
        out_specs=pl.BlockSpec(memory_space=pl.ANY),
        scratch_shapes=[pltpu.SemaphoreType.DMA((7,)), pltpu.SemaphoreType.DMA((7,)), pltpu.SemaphoreType.DMA],
    )(shard)


def _pair_exchange(name, grads):
    n = len(grads)

    def body(*refs):
        g_refs, out_refs, send_sems, recv_sems = refs[:n], refs[n:2 * n], refs[2 * n], refs[2 * n + 1]
        x, y, c = lax.axis_index("x"), lax.axis_index("y"), lax.axis_index("c")
        copies = [
            pltpu.make_async_remote_copy(
                src_ref=g_refs[k].at[:, 1 - c],
                dst_ref=out_refs[k],
                send_sem=send_sems.at[k],
                recv_sem=recv_sems.at[k],
                device_id=(x, y, 1 - c),
                device_id_type=MESH,
            )
            for k in range(n)
        ]
        for cp in copies:
            cp.start()
        for cp in copies:
            cp.wait_recv()
        for cp in copies:
            cp.wait_send()

    return pl.pallas_call(
        body,
        name=name,
        out_shape=[jax.ShapeDtypeStruct((4,) + g.shape[2:], g.dtype) for g in grads],
        in_specs=[pl.BlockSpec(memory_space=pl.ANY)] * n,
        out_specs=[pl.BlockSpec(memory_space=pl.ANY)] * n,
        scratch_shapes=[pltpu.SemaphoreType.DMA((n,)), pltpu.SemaphoreType.DMA((n,))],
    )(*grads)


def _chip_exchange(name, parts):
    n = len(parts)

    def body(*refs):
        p_refs, out_refs, send_sems, recv_sems = refs[:n], refs[n:2 * n], refs[2 * n], refs[2 * n + 1]
        x, y, c = lax.axis_index("x"), lax.axis_index("y"), lax.axis_index("c")
        chips = [(1 - x, y), (x, 1 - y), (1 - x, 1 - y)]
        copies = []
        for k in range(n):
            for j, (px, py) in enumerate(chips):
                copies.append(pltpu.make_async_remote_copy(
                    src_ref=p_refs[k].at[2 * px + py],
                    dst_ref=out_refs[k].at[j],
                    send_sem=send_sems.at[3 * k + j],
                    recv_sem=recv_sems.at[3 * k + j],
                    device_id=(px, py, c),
                    device_id_type=MESH,
                ))
        for cp in copies:
            cp.start()
        for cp in copies:
            cp.wait_recv()
        for cp in copies:
            cp.wait_send()

    return pl.pallas_call(
        body,
        name=name,
        out_shape=[jax.ShapeDtypeStruct((3,) + p.shape[1:], p.dtype) for p in parts],
        in_specs=[pl.BlockSpec(memory_space=pl.ANY)] * n,
        out_specs=[pl.BlockSpec(memory_space=pl.ANY)] * n,
        scratch_shapes=[pltpu.SemaphoreType.DMA((3 * n,)), pltpu.SemaphoreType.DMA((3 * n,))],
    )(*parts)


def _mm(name, pairs, extras, outs, epi, *, m, tm, n, tn, arbitrary=False):
    nj = n // tn
    a_arrays, a_specs, b_arrays, b_specs, b_index = [], [], [], [], []
    for a, b, nt, cols in pairs:
        a_arrays.append(a)
        a_specs.append(pl.BlockSpec((tm, a.shape[1]), lambda i, j: (i, 0)))
        known = [k for k, other in enumerate(b_arrays) if other is b]
        if known:
            b_index.append(known[0])
            continue
        b_index.append(len(b_arrays))
        b_arrays.append(b)
        if cols is not None:
            assert nj == 1
            b_specs.append(pl.BlockSpec(b.shape, lambda i, j: (0, 0)))
        elif nt:
            b_specs.append(pl.BlockSpec((tn, b.shape[1]), lambda i, j: (j, 0)))
        else:
            b_specs.append(pl.BlockSpec((b.shape[0], tn), lambda i, j: (0, j)))
    arrays = a_arrays + b_arrays + [arr for arr, _ in extras]
    in_specs = a_specs + b_specs + [spec for _, spec in extras]
    n_a, n_b, n_extras = len(a_arrays), len(b_arrays), len(extras)

    def body(*refs):
        a_refs = refs[:n_a]
        b_refs = refs[n_a:n_a + n_b]
        ex = refs[n_a + n_b:n_a + n_b + n_extras]
        out = refs[n_a + n_b + n_extras:]
        accs = []
        for p, (_, _, nt, cols) in enumerate(pairs):
            av = a_refs[p][...]
            if av.dtype != BF16:
                av = av.astype(BF16)
            b_ref = b_refs[b_index[p]]
            bv = b_ref[...] if cols is None else b_ref[:, cols[0]:cols[1]]
            accs.append(_dot_nt(av, bv) if nt else _dot(av, bv))
        epi(accs, ex, out)

    sem = ("arbitrary", "arbitrary") if arbitrary else ("parallel", "parallel")
    return pl.pallas_call(
        body,
        name=name,
        grid=(m // tm, nj),
        in_specs=in_specs,
        out_specs=[spec for _, spec in outs],
        out_shape=[shape for shape, _ in outs],
        compiler_params=_params(sem),
    )(*arrays)


def _tile(tm, tn, off=0):
    return pl.BlockSpec((tm, tn), lambda i, j: (i, j + off))


def _row(tm, w, blk=0):
    return pl.BlockSpec((tm, w), lambda i, j: (i, blk))


def _whole(shape):
    zeros = (0,) * len(shape)
    return pl.BlockSpec(shape, lambda i, j: zeros)


def _sds(shape, dtype):
    return jax.ShapeDtypeStruct(shape, dtype)


def _mm_tn(name, a, g, *, tk, tn, ts, blocked=False):
    s_len, ka = a.shape
    n = g.shape[1]

    def body(a_ref, g_ref, o_ref):
        part = _dot_tn(a_ref[...].astype(BF16), g_ref[...].astype(BF16))

        @pl.when(pl.program_id(2) == 0)
        def _():
            o_ref[...] = part

        @pl.when(pl.program_id(2) > 0)
        def _():
            o_ref[...] += part

    if blocked:
        out_shape = _sds((n // tn, ka, tn), F32)
        out_spec = pl.BlockSpec((None, tk, tn), lambda i, j, s: (j, i, 0))
    else:
        out_shape = _sds((ka, n), F32)
        out_spec = pl.BlockSpec((tk, tn), lambda i, j, s: (i, j))
    return pl.pallas_call(
        body,
        name=name,
        grid=(ka // tk, n // tn, s_len // ts),
        in_specs=[pl.BlockSpec((ts, tk), lambda i, j, s: (s, i)), pl.BlockSpec((ts, tn), lambda i, j, s: (s, j))],
        out_specs=out_spec,
        out_shape=out_shape,
        compiler_params=_params(("parallel", "parallel", "arbitrary")),
    )(a, g)


def _rms_fwd(name, x, g, tm):
    s_len, d = x.shape

    def body(x_ref, g_ref, h_ref, r_ref):
        xv = x_ref[...]
        r = lax.rsqrt(jnp.mean(xv * xv, axis=-1, keepdims=True) + EPS)
        h_ref[...] = (xv * r * g_ref[...]).astype(BF16)
        r_ref[...] = r

    return pl.pallas_call(
        body,
        name=name,
        grid=(s_len // tm,),
        in_specs=[pl.BlockSpec((tm, d), lambda i: (i, 0)), pl.BlockSpec((1, d), lambda i: (0, 0))],
        out_specs=[pl.BlockSpec((tm, d), lambda i: (i, 0)), pl.BlockSpec((tm, 1), lambda i: (i, 0))],
        out_shape=[_sds((s_len, d), BF16), _sds((s_len, 1), F32)],
        compiler_params=_params(("parallel",)),
    )(x, g)


def _forget_cumsum(prest, b_f_pad, tc):
    s_len = prest.shape[0]

    def body(f_ref, b_ref, c_ref, carry):
        @pl.when(pl.program_id(0) == 0)
        def _():
            carry[...] = jnp.zeros_like(carry)

        logf = _log_sigmoid(f_ref[...] + b_ref[...])
        row = lax.broadcasted_iota(jnp.int32, (tc, tc), 0)
        col = lax.broadcasted_iota(jnp.int32, (tc, tc), 1)
        tri = (row >= col).astype(F32)
        c = _dot_f32(tri, logf) + carry[...]
        c_ref[...] = c
        carry[...] = c[tc - 1:tc, :]

    return pl.pallas_call(
        body,
        name="forget_cumsum",
        grid=(s_len // tc,),
        in_specs=[pl.BlockSpec((tc, LANES), lambda i: (i, F_OFF // LANES)), pl.BlockSpec((1, LANES), lambda i: (0, 0))],
        out_specs=pl.BlockSpec((tc, LANES), lambda i: (i, 0)),
        out_shape=_sds((s_len, LANES), F32),
        scratch_shapes=[pltpu.VMEM((1, LANES), F32)],
        compiler_params=_params(("arbitrary",)),
    )(prest, b_f_pad)


def _stack_heads(pair, lt64):
    zero = jnp.zeros_like(pair)
    return jnp.concatenate([jnp.where(lt64, pair, zero), jnp.where(lt64, zero, pair)], axis=0)


def _score_tiles(q_ref, k_ref, cq_ref, ck_ref, st_sc, tk):
    lt64 = _lane_lt64((tk, LANES))
    for p in range(N_HEADS // 2):
        lanes = slice(p * LANES, (p + 1) * LANES)
        q_pair = q_ref[:, lanes] * jnp.asarray(HEAD_DIM ** -0.5, BF16)
        st2 = _dot_nt(_stack_heads(k_ref[:, lanes], lt64), q_pair)
        for half in range(2):
            h = 2 * p + half
            st_sc[h] = st2[half * tk:(half + 1) * tk] + (cq_ref[h] - ck_ref[:, h:h + 1])


def _mask_diagonal(st_sc, i, j, tq, tk):
    @pl.when((j + 1) * tk - 1 > i * tq)
    def _():
        key = j * tk + lax.broadcasted_iota(jnp.int32, (tk, tq), 0)
        query = i * tq + lax.broadcasted_iota(jnp.int32, (tk, tq), 1)
        st_sc[...] = jnp.where((query >= key)[None], st_sc[...], NEG)


def _attn_fwd(qkv, v_t, c_col, c_row, tq, tk):
    s_len = qkv.shape[0]
    ratio = tq // tk
    steps = [(i, j) for i in range(s_len // tq) for j in range((i + 1) * ratio)]
    i_tab = jnp.asarray([i for i, _ in steps], jnp.int32)
    j_tab = jnp.asarray([j for _, j in steps], jnp.int32)

    def body(i_ref, j_ref, q_ref, k_ref, vt_ref, cq_ref, ck_ref, o_ref, lse_ref, acc_t, m_sc, l_sc, st_sc, p_sc):
        i, j = i_ref[pl.program_id(0)], j_ref[pl.program_id(0)]

        @pl.when(j == 0)
        def _():
            acc_t[...] = jnp.zeros_like(acc_t)
            m_sc[...] = jnp.full_like(m_sc, NEG)
            l_sc[...] = jnp.zeros_like(l_sc)

        _score_tiles(q_ref, k_ref, cq_ref, ck_ref, st_sc, tk)
        _mask_diagonal(st_sc, i, j, tq, tk)
        st = st_sc[...]
        m_old = m_sc[...]
        m_new = jnp.maximum(m_old, jnp.max(st, axis=1, keepdims=True))
        alpha = jnp.exp(m_old - m_new)
        pt = jnp.exp(st - m_new)
        l_sc[...] = alpha * l_sc[...] + jnp.sum(pt, axis=1, keepdims=True)
        m_sc[...] = m_new
        p_sc[...] = pt.astype(BF16)
        top = lax.broadcasted_iota(jnp.int32, (LANES, tq), 0) < HEAD_DIM
        for p in range(N_HEADS // 2):
            lanes = slice(p * LANES, (p + 1) * LANES)
            vt_pair = vt_ref[lanes, :]
            pv = jnp.where(top, _dot(vt_pair, p_sc[2 * p]), _dot(vt_pair, p_sc[2 * p + 1]))
            acc_t[lanes, :] = acc_t[lanes, :] * jnp.where(top, alpha[2 * p], alpha[2 * p + 1]) + pv

        @pl.when(j == (i + 1) * ratio - 1)
        def _():
            for p in range(N_HEADS // 2):
                lanes = slice(p * LANES, (p + 1) * LANES)
                l_pair = jnp.where(top, l_sc[2 * p], l_sc[2 * p + 1])
                o_ref[:, lanes] = jnp.transpose(acc_t[lanes, :] / l_pair)
            lse_ref[...] = m_sc[...] + jnp.log(l_sc[...])

    stat = pltpu.VMEM((N_HEADS, 1, tq), F32)
    return pl.pallas_call(
        body,
        name="attn_fwd",
        grid_spec=pltpu.PrefetchScalarGridSpec(
            num_scalar_prefetch=2,
            grid=(len(steps),),
            in_specs=[
                pl.BlockSpec((tq, FOX_W), lambda n, it, jt: (it[n], 0)),
                pl.BlockSpec((tk, FOX_W), lambda n, it, jt: (jt[n], 1)),
                pl.BlockSpec((FOX_W, tk), lambda n, it, jt: (0, jt[n])),
                pl.BlockSpec((N_HEADS, 1, tq), lambda n, it, jt: (0, 0, it[n])),
                pl.BlockSpec((tk, LANES), lambda n, it, jt: (jt[n], 0)),
            ],
            out_specs=[
                pl.BlockSpec((tq, FOX_W), lambda n, it, jt: (it[n], 0)),
                pl.BlockSpec((N_HEADS, 1, tq), lambda n, it, jt: (0, 0, it[n])),
            ],
            scratch_shapes=[pltpu.VMEM((FOX_W, tq), F32), stat, stat, pltpu.VMEM((N_HEADS, tk, tq), F32),
                            pltpu.VMEM((N_HEADS, tk, tq), BF16)],
        ),
        out_shape=[_sds((s_len, FOX_W), F32), _sds((N_HEADS, 1, s_len), F32)],
        compiler_params=_params(("arbitrary",)),
    )(i_tab, j_tab, qkv, qkv, v_t, c_row, c_col)


def _sgu_mix(vn, w_stack, lt64):
    outs = []
    for p in range(SGU_G // 2):
        r = _dot(w_stack[p], vn[:, p * LANES:(p + 1) * LANES])
        outs.append(jnp.where(lt64, r[:SGU_LEN], r[SGU_LEN:]))
    return jnp.concatenate(outs, axis=1)


def _sgu_norm(sv, ln_g, ln_b):
    svg = _gelu(sv)
    xc = svg - jnp.mean(svg, axis=-1, keepdims=True)
    rstd = lax.rsqrt(jnp.mean(xc * xc, axis=-1, keepdims=True) + EPS)
    xhat = xc * rstd
    return xhat, rstd, xhat * ln_g + ln_b


def _sgu_fwd(prest, ln_g, ln_b, w_stack, b_pair, tm):
    s_len = prest.shape[0]

    def body(u_ref, sv_ref, g_ref, b_ref, w_ref, bp_ref, sg_ref):
        lt64 = _lane_lt64((SGU_LEN, LANES))
        _, _, vn = _sgu_norm(sv_ref[...], g_ref[...], b_ref[...])
        vn = vn.astype(BF16)
        w_stack_v = [w_ref[p] for p in range(SGU_G // 2)]
        for w in range(tm // SGU_LEN):
            win = slice(w * SGU_LEN, (w + 1) * SGU_LEN)
            mixed = _sgu_mix(vn[win], w_stack_v, lt64) + bp_ref[...]
            sg_ref[win, :] = (_gelu(u_ref[win, :]) * mixed).astype(BF16)

    return pl.pallas_call(
        body,
        name="sgu_fwd",
        grid=(s_len // tm,),
        in_specs=[
            pl.BlockSpec((tm, SGU_W), lambda i: (i, U_OFF // SGU_W)),
            pl.BlockSpec((tm, SGU_W), lambda i: (i, SV_OFF // SGU_W)),
            pl.BlockSpec((1, SGU_W), lambda i: (0, 0)),
            pl.BlockSpec((1, SGU_W), lambda i: (0, 0)),
            pl.BlockSpec((SGU_G // 2, 2 * SGU_LEN, SGU_LEN), lambda i: (0, 0, 0)),
            pl.BlockSpec((SGU_LEN, SGU_W), lambda i: (0, 0)),
        ],
        out_specs=pl.BlockSpec((tm, SGU_W), lambda i: (i, 0)),
        out_shape=_sds((s_len, SGU_W), BF16),
        compiler_params=_params(("parallel",)),
    )(prest, prest, ln_g, ln_b, w_stack, b_pair)


def _sgu_bwd(prest, dsg, ln_g, ln_b, w_stack, wt_stack, b_pair, tm):
    s_len = prest.shape[0]
    n_pair = SGU_G // 2

    def body(u_ref, sv_ref, dsg_ref, g_ref, b_ref, w_ref, wt_ref, bp_ref,
             du_ref, dsv_ref, dw_ref, db_ref, dg_ref, dbeta_ref, dvn_sc):
        @pl.when(pl.program_id(0) == 0)
        def _():
            dw_ref[...] = jnp.zeros_like(dw_ref)
            db_ref[...] = jnp.zeros_like(db_ref)
            dg_ref[...] = jnp.zeros_like(dg_ref)
            dbeta_ref[...] = jnp.zeros_like(dbeta_ref)

        lt64 = _lane_lt64((SGU_LEN, LANES))
        sv = sv_ref[...]
        xhat, rstd, vn32 = _sgu_norm(sv, g_ref[...], b_ref[...])
        vn = vn32.astype(BF16)
        w_stack_v = [w_ref[p] for p in range(n_pair)]
        db = jnp.zeros((SGU_LEN, SGU_W), F32)
        for w in range(tm // SGU_LEN):
            win = slice(w * SGU_LEN, (w + 1) * SGU_LEN)
            u = u_ref[win, :]
            dsg_w = dsg_ref[win, :]
            mixed = _sgu_mix(vn[win], w_stack_v, lt64) + bp_ref[...]
            du_ref[win, :] = (dsg_w * mixed * _gelu_grad(u)).astype(BF16)
            dmixed = dsg_w * _gelu(u)
            db = db + dmixed
            dm16 = dmixed.astype(BF16)
            for p in range(n_pair):
                lanes = slice(p * LANES, (p + 1) * LANES)
                dmp = dm16[:, lanes]
                r = _dot(wt_ref[p], dmp)
                dvn_sc[win, lanes] = jnp.where(lt64, r[:SGU_LEN], r[SGU_LEN:])
                zero = jnp.zeros_like(dmp)
                dm_ab = jnp.concatenate([jnp.where(lt64, dmp, zero), jnp.where(lt64, zero, dmp)], axis=0)
                dw_ref[p] += _dot_nt(dm_ab, vn[win, lanes])
        db_ref[...] += db
        dvn = dvn_sc[...]
        dg_ref[...] += jnp.sum(dvn * xhat, axis=0, keepdims=True)
        dbeta_ref[...] += jnp.sum(dvn, axis=0, keepdims=True)
        dxh = dvn * g_ref[...]
        dsvg = rstd * (dxh - jnp.mean(dxh, axis=-1, keepdims=True) - xhat * jnp.mean(dxh * xhat, axis=-1, keepdims=True))
        dsv_ref[...] = (dsvg * _gelu_grad(sv)).astype(BF16)

    const2 = lambda i: (0, 0)
    const3 = lambda i: (0, 0, 0)
    return pl.pallas_call(
        body,
        name="sgu_bwd",
        grid=(s_len // tm,),
        in_specs=[
            pl.BlockSpec((tm, SGU_W), lambda i: (i, U_OFF // SGU_W)),
            pl.BlockSpec((tm, SGU_W), lambda i: (i, SV_OFF // SGU_W)),
            pl.BlockSpec((tm, SGU_W), lambda i: (i, 0)),
            pl.BlockSpec((1, SGU_W), const2),
            pl.BlockSpec((1, SGU_W), const2),
            pl.BlockSpec((n_pair, 2 * SGU_LEN, SGU_LEN), const3),
            pl.BlockSpec((n_pair, 2 * SGU_LEN, SGU_LEN), const3),
            pl.BlockSpec((SGU_LEN, SGU_W), const2),
        ],
        out_specs=[
            pl.BlockSpec((tm, SGU_W), lambda i: (i, 0)),
            pl.BlockSpec((tm, SGU_W), lambda i: (i, 0)),
            pl.BlockSpec((n_pair, 2 * SGU_LEN, SGU_LEN), const3),
            pl.BlockSpec((SGU_LEN, SGU_W), const2),
            pl.BlockSpec((1, SGU_W), const2),
            pl.BlockSpec((1, SGU_W), const2),
        ],
        out_shape=[
            _sds((s_len, SGU_W), BF16), _sds((s_len, SGU_W), BF16), _sds((n_pair, 2 * SGU_LEN, SGU_LEN), F32),
            _sds((SGU_LEN, SGU_W), F32), _sds((1, SGU_W), F32), _sds((1, SGU_W), F32),
        ],
        scratch_shapes=[pltpu.VMEM((tm, SGU_W), F32)],
        compiler_params=_params(("arbitrary",)),
    )(prest, prest, dsg, ln_g, ln_b, w_stack, wt_stack, b_pair)


def _attn_bwd(qkv, k_t, do, c_col, c_row, lse_row, delta_row, tq, tk):
    s_len = qkv.shape[0]
    nq, nk = s_len // tq, s_len // tk
    ratio = tq // tk
    scale = HEAD_DIM ** -0.5
    steps = [(j, i) for j in range(nk) for i in range(j // ratio, nq)]
    j_tab = jnp.asarray([j for j, _ in steps], jnp.int32)
    i_tab = jnp.asarray([i for _, i in steps], jnp.int32)

    def body(j_ref, i_ref, q_ref, k_ref, v_ref, kt_ref, do_ref, cq_ref, ck_ref, lse_ref, dl_ref,
             dq_ref, dk_ref, dv_ref, dcr_ref, dcc_ref, dq_t, dk_acc, dv_acc, dcc_acc, st_sc, dpt_sc, p_sc, ds_sc):
        n = pl.program_id(0)
        j, i = j_ref[n], i_ref[n]

        @pl.when(n == 0)
        def _():
            dq_t[...] = jnp.zeros_like(dq_t)
            dcr_ref[...] = jnp.zeros_like(dcr_ref)

        @pl.when(i == j // ratio)
        def _():
            dk_acc[...] = jnp.zeros_like(dk_acc)
            dv_acc[...] = jnp.zeros_like(dv_acc)
            dcc_acc[...] = jnp.zeros_like(dcc_acc)

        lt64 = _lane_lt64((tk, LANES))
        _score_tiles(q_ref, k_ref, cq_ref, ck_ref, st_sc, tk)
        for p in range(N_HEADS // 2):
            lanes = slice(p * LANES, (p + 1) * LANES)
            dpt2 = _dot_nt(_stack_heads(v_ref[:, lanes], lt64), do_ref[:, lanes].astype(BF16))
            dpt_sc[2 * p] = dpt2[:tk]
            dpt_sc[2 * p + 1] = dpt2[tk:]
        _mask_diagonal(st_sc, i, j, tq, tk)

        pt = jnp.exp(st_sc[...] - lse_ref[...])
        dst = pt * (dpt_sc[...] - dl_ref[...])
        p_sc[...] = pt.astype(BF16)
        ds_sc[...] = dst.astype(BF16)
        dcr_ref[i] += jnp.sum(dst, axis=1, keepdims=True)
        col_sums = jnp.sum(dst, axis=2, keepdims=True)
        lane = lax.broadcasted_iota(jnp.int32, (tk, LANES), 1)
        dcc = jnp.zeros((tk, LANES), F32)
        for h in range(N_HEADS):
            dcc = jnp.where(lane == h, -col_sums[h], dcc)
        dcc_acc[...] += dcc

        for p in range(N_HEADS // 2):
            lanes = slice(p * LANES, (p + 1) * LANES)
            q_pair = q_ref[:, lanes] * jnp.asarray(scale, BF16)
            dv2 = _dot(p_sc[2 * p:2 * p + 2].reshape(2 * tk, tq), do_ref[:, lanes].astype(BF16))
            dv_acc[:, lanes] += jnp.where(lt64, dv2[:tk], dv2[tk:])
            dk2 = _dot(ds_sc[2 * p:2 * p + 2].reshape(2 * tk, tq), q_pair)
            dk_acc[:, lanes] += jnp.where(lt64, dk2[:tk], dk2[tk:])
            dq2 = _dot(kt_ref[lanes, :], jnp.concatenate([ds_sc[2 * p], ds_sc[2 * p + 1]], axis=1))
            top = lax.broadcasted_iota(jnp.int32, (LANES, tq), 0) < HEAD_DIM
            dq_t[i, lanes, :] += jnp.where(top, dq2[:, :tq], dq2[:, tq:])

        @pl.when(j == (i + 1) * ratio - 1)
        def _():
            rows = pl.ds(pl.multiple_of(i * tq, tq), tq)
            for p in range(N_HEADS // 2):
                lanes = slice(p * LANES, (p + 1) * LANES)
                dq_ref[rows, lanes] = (jnp.transpose(dq_t[i, lanes, :]) * scale).astype(BF16)

        @pl.when(i == nq - 1)
        def _():
            dk_ref[...] = dk_acc[...].astype(BF16)
            dv_ref[...] = dv_acc[...].astype(BF16)
            dcc_ref[...] = dcc_acc[...]

    q_map = lambda n, jt, it: (it[n], 0)
    q_stat = lambda n, jt, it: (0, 0, it[n])
    k_map = lambda n, jt, it: (jt[n], 0)
    tile = (N_HEADS, tk, tq)
    return pl.pallas_call(
        body,
        name="attn_bwd",
        grid_spec=pltpu.PrefetchScalarGridSpec(
            num_scalar_prefetch=2,
            grid=(len(steps),),
            in_specs=[
                pl.BlockSpec((tq, FOX_W), q_map),
                pl.BlockSpec((tk, FOX_W), lambda n, jt, it: (jt[n], 1)),
                pl.BlockSpec((tk, FOX_W), lambda n, jt, it: (jt[n], 2)),
                pl.BlockSpec((FOX_W, tk), lambda n, jt, it: (0, jt[n])),
                pl.BlockSpec((tq, FOX_W), q_map),
                pl.BlockSpec((N_HEADS, 1, tq), q_stat),
                pl.BlockSpec((tk, LANES), k_map),
                pl.BlockSpec((N_HEADS, 1, tq), q_stat),
                pl.BlockSpec((N_HEADS, 1, tq), q_stat),
            ],
            out_specs=[
                pl.BlockSpec((s_len, FOX_W), lambda n, jt, it: (0, 0)),
                pl.BlockSpec((tk, FOX_W), k_map),
                pl.BlockSpec((tk, FOX_W), k_map),
                pl.BlockSpec((nq, N_HEADS, 1, tq), lambda n, jt, it: (0, 0, 0, 0)),
                pl.BlockSpec((tk, LANES), k_map),
            ],
            scratch_shapes=[pltpu.VMEM((nq, FOX_W, tq), F32), pltpu.VMEM((tk, FOX_W), F32), pltpu.VMEM((tk, FOX_W), F32),
                            pltpu.VMEM((tk, LANES), F32), pltpu.VMEM(tile, F32), pltpu.VMEM(tile, F32),
                            pltpu.VMEM(tile, BF16), pltpu.VMEM(tile, BF16)],
        ),
        out_shape=[_sds((s_len, FOX_W), BF16), _sds((s_len, FOX_W), BF16), _sds((s_len, FOX_W), BF16),
                   _sds((nq, N_HEADS, 1, tq), F32), _sds((s_len, LANES), F32)],
        compiler_params=_params(("arbitrary",)),
    )(j_tab, i_tab, qkv, qkv, qkv, k_t, do, c_row, c_col, lse_row, delta_row)


def _forget_bwd(dc_rows, dc_cols, prest, b_f_pad, tc):
    s_len = dc_rows.shape[0]
    nb = s_len // tc

    def body(dcr_ref, dc_ref, f_ref, b_ref, df_ref, db_ref, carry):
        @pl.when(pl.program_id(0) == 0)
        def _():
            carry[...] = jnp.zeros_like(carry)
            db_ref[...] = jnp.zeros_like(db_ref)

        row = lax.broadcasted_iota(jnp.int32, (tc, tc), 0)
        col = lax.broadcasted_iota(jnp.int32, (tc, tc), 1)
        tri = (row <= col).astype(F32)
        dlogf = _dot_f32(tri, dcr_ref[...] + dc_ref[...]) + carry[...]
        carry[...] = dlogf[0:1, :]
        z = f_ref[...] + b_ref[...]
        lane = lax.broadcasted_iota(jnp.int32, (tc, LANES), 1)
        dz = jnp.where(lane < N_HEADS, dlogf * _sigmoid(-z), 0.0)
        df_ref[...] = dz.astype(BF16)
        db_ref[...] += jnp.sum(dz, axis=0, keepdims=True)

    rev = lambda i: (nb - 1 - i, 0)
    return pl.pallas_call(
        body,
        name="forget_bwd",
        grid=(nb,),
        in_specs=[
            pl.BlockSpec((tc, LANES), rev),
            pl.BlockSpec((tc, LANES), rev),
            pl.BlockSpec((tc, LANES), lambda i: (nb - 1 - i, F_OFF // LANES)),
            pl.BlockSpec((1, LANES), lambda i: (0, 0)),
        ],
        out_specs=[pl.BlockSpec((tc, LANES), rev), pl.BlockSpec((1, LANES), lambda i: (0, 0))],
        out_shape=[_sds((s_len, LANES), BF16), _sds((1, LANES), F32)],
        scratch_shapes=[pltpu.VMEM((1, LANES), F32)],
        compiler_params=_params(("arbitrary",)),
    )(dc_rows, dc_cols, prest, b_f_pad)


def _pair_sum(name, g4, recv, idx, tr):
    _, _, r, c = g4.shape

    def body(idx_ref, g_ref, r_ref, p16_ref, own_ref):
        k = pl.program_id(1)
        s = g_ref[...] + r_ref[...]
        p16_ref[...] = s.astype(BF16)

        @pl.when(k == idx_ref[1])
        def _():
            own_ref[...] = s

    return pl.pallas_call(
        body,
        name=name,
        grid_spec=pltpu.PrefetchScalarGridSpec(
            num_scalar_prefetch=1,
            grid=(r // tr, 4),
            in_specs=[
                pl.BlockSpec((None, None, tr, c), lambda i, k, idx: (k, idx[0], i, 0)),
                pl.BlockSpec((None, tr, c), lambda i, k, idx: (k, i, 0)),
            ],
            out_specs=[
                pl.BlockSpec((None, tr, c), lambda i, k, idx: (k, i, 0)),
                pl.BlockSpec((tr, c), lambda i, k, idx: (i, 0)),
            ],
        ),
        out_shape=[_sds((4, r, c), BF16), _sds((r, c), F32)],
        compiler_params=_params(("parallel", "arbitrary")),
    )(idx, g4, recv)


def _adamw_math(w, g, m, v):
    m2 = ADAM_B1 * m + (1.0 - ADAM_B1) * g
    v2 = ADAM_B2 * v + (1.0 - ADAM_B2) * (g * g)
    m_hat = m2 / (1.0 - ADAM_B1 ** ADAM_STEP)
    v_hat = v2 / (1.0 - ADAM_B2 ** ADAM_STEP)
    delta = -ADAM_LR * (m_hat / (jnp.sqrt(v_hat) + ADAM_EPS) + ADAM_WD * w)
    return delta, m2, v2


def _adamw_shard(name, own, recv, w, m, v, tr):
    r, c = own.shape

    def body(own_ref, recv_ref, w_ref, m_ref, v_ref, g_ref, d_ref, m2_ref, v2_ref):
        g = own_ref[...]
        for k in range(3):
            g = g + recv_ref[k].astype(F32)
        delta, m2, v2 = _adamw_math(w_ref[...], g, m_ref[...], v_ref[...])
        g_ref[...] = g
        d_ref[...] = delta
        m2_ref[...] = m2
        v2_ref[...] = v2

    spec = pl.BlockSpec((tr, c), lambda i: (i, 0))
    return pl.pallas_call(
        body,
        name=name,
        grid=(r // tr,),
        in_specs=[spec, pl.BlockSpec((3, tr, c), lambda i: (0, i, 0)), spec, spec, spec],
        out_specs=[spec] * 4,
        out_shape=[_sds((r, c), F32)] * 4,
        compiler_params=_params(("parallel",)),
    )(own, recv, w, m, v)


def _adamw_small(gathered, w, m, v):
    _, r, _ = gathered.shape

    def body(ga_ref, w_ref, m_ref, v_ref, g_ref, d_ref, m2_ref, v2_ref):
        g = ga_ref[0]
        for k in range(1, N_DEV):
            g = g + ga_ref[k]
        delta, m2, v2 = _adamw_math(w_ref[...], g, m_ref[...], v_ref[...])
        g_ref[...] = g
        d_ref[...] = delta
        m2_ref[...] = m2
        v2_ref[...] = v2

    spec = pl.BlockSpec((r, LANES), lambda i: (0, 0))
    return pl.pallas_call(
        body,
        name="adamw_small",
        grid=(1,),
        in_specs=[pl.BlockSpec((N_DEV, r, LANES), lambda i: (0, 0, 0)), spec, spec, spec],
        out_specs=[spec] * 4,
        out_shape=[_sds((r, LANES), F32)] * 4,
        compiler_params=_params(("arbitrary",)),
    )(gathered, w, m, v)


_SMALL = (("w_sgu", (1, SGU_G, SGU_LEN, SGU_LEN)), ("b_sgu", (1, SGU_G, SGU_LEN)), ("norm1_g", (1, D_MODEL)),
          ("norm2_g", (1, D_MODEL)), ("normf_g", (D_MODEL,)), ("ln_v_g", (1, SGU_W)), ("ln_v_b", (1, SGU_W)),
          ("b_f", (1, N_HEADS)))


def _pack_small(values):
    rows = []
    for val in values:
        flat = val.reshape(-1).astype(F32)
        pad = (-flat.shape[0]) % LANES
        rows.append(jnp.pad(flat, (0, pad)).reshape(-1, LANES))
    packed = jnp.concatenate(rows, axis=0)
    return jnp.pad(packed, ((0, (-packed.shape[0]) % 8), (0, 0)))


def _unpack_small(packed):
    out, row = [], 0
    for _, shape in _SMALL:
        size = math.prod(shape)
        n_rows = -(-size // LANES)
        out.append(packed[row:row + n_rows].reshape(-1)[:size].reshape(shape))
        row += n_rows
    return out


def kernel(x, norm1_g, w_in, b_f, ln_v_g, ln_v_b, w_sgu, b_sgu, w_a, w_b, w_o, norm2_g, w_up, w_down, normf_g, loss_target, m_norm1_g, m_w_in, m_b_f, m_ln_v_g, m_ln_v_b, m_w_sgu, m_b_sgu, m_w_a, m_w_b, m_w_o, m_norm2_g, m_w_up, m_w_down, m_normf_g, v_norm1_g, v_w_in, v_b_f, v_ln_v_g, v_ln_v_b, v_w_sgu, v_b_sgu, v_w_a, v_w_b, v_w_o, v_norm2_g, v_w_up, v_w_down, v_normf_g):
    xs = x[0]
    target = loss_target[0]
    s_len, d = xs.shape
    tm = min(512, s_len)
    tr = min(256, s_len)
    ta = min(256, s_len)

    big = (w_in[0], w_a[0], w_b[0], w_o[0], w_up[0], w_down[0])
    flat = jnp.concatenate([w.reshape(-1).astype(BF16) for w in big]).reshape(-1, D_MODEL)
    gathered = _all_gather("gather_weights", flat)
    row = 0
    full = []
    for w, col_sharded in zip(big, (True, True, True, False, True, False)):
        n_rows = w.size // D_MODEL
        blk = gathered[:, row:row + n_rows].reshape((N_DEV,) + w.shape)
        row += n_rows
        if col_sharded:
            full.append(jnp.transpose(blk, (1, 0, 2)).reshape(w.shape[0], N_DEV * w.shape[1]))
        else:
            full.append(blk.reshape(N_DEV * w.shape[0], w.shape[1]))
    w_in_f, w_a_f, w_b_f, w_o_f, w_up_f, w_down_f = full
    w_qkv = w_in_f[:, :QKV_W]
    f_lo = QKV_W
    u_lo = f_lo + N_HEADS
    w_rest = jnp.concatenate([w_in_f[:, u_lo:], jnp.pad(w_in_f[:, f_lo:u_lo], ((0, 0), (0, LANES - N_HEADS)))], axis=1)

    chunk_id = jnp.arange(SGU_LEN) // CHUNK
    sgu_mask = chunk_id[None, :] <= chunk_id[:, None]
    w_masked = jnp.where(sgu_mask[None], w_sgu[0], 0.0)
    w_stack = w_masked.reshape(SGU_G // 2, 2 * SGU_LEN, SGU_LEN).astype(BF16)
    wt_stack = jnp.transpose(w_masked, (0, 2, 1)).reshape(SGU_G // 2, 2 * SGU_LEN, SGU_LEN).astype(BF16)
    b_pair = jnp.transpose(jnp.repeat(b_sgu[0], SGU_W // SGU_G, axis=0))
    b_f_pad = jnp.pad(b_f, ((0, 0), (0, LANES - N_HEADS)))
    head_sel = (jnp.arange(FOX_W)[:, None] // HEAD_DIM == jnp.arange(LANES)[None, :]).astype(F32)

    h, r1 = _rms_fwd("rms1", xs, norm1_g, tm)

    def store(dtype):
        def epi(accs, ex, out):
            out[0][...] = accs[0].astype(dtype)
        return epi

    (qkv,) = _mm("proj_qkv", [(h, w_qkv, False, None)], [], [(_sds((s_len, QKV_W), BF16), _tile(tm, 512))],
                 store(BF16), m=s_len, tm=tm, n=QKV_W, tn=512)
    (prest,) = _mm("proj_rest", [(h, w_rest, False, None)], [], [(_sds((s_len, REST_W), F32), _tile(tm, 640))],
                   store(F32), m=s_len, tm=tm, n=REST_W, tn=640)

    c_col = _forget_cumsum(prest, b_f_pad, ta)
    c_row = jnp.transpose(c_col[:, :N_HEADS]).reshape(N_HEADS, 1, s_len)
    k_t = jnp.transpose(qkv[:, FOX_W:2 * FOX_W])
    v_t = jnp.transpose(qkv[:, 2 * FOX_W:])
    o, lse_row = _attn_fwd(qkv, v_t, c_col, c_row, ta, ta)
    sg = _sgu_fwd(prest, ln_v_g, ln_v_b, w_stack, b_pair, tm)

    def merge_epi(accs, ex, out):
        ya, yb = accs
        sa, sb = _sigmoid(ex[0][...]), _sigmoid(ex[1][...])
        out[0][...] = (sa * ya + sb * yb).astype(BF16)
        out[1][...] = ya
        out[2][...] = yb

    merged, ya, yb = _mm(
        "merge", [(o, w_a_f, False, None), (sg, w_b_f, False, None)],
        [(prest, _tile(tm, 512, GA_OFF // 512)), (prest, _tile(tm, 512, GB_OFF // 512))],
        [(_sds((s_len, d), BF16), _tile(tm, 512)), (_sds((s_len, d), F32), _tile(tm, 512)), (_sds((s_len, d), F32), _tile(tm, 512))],
        merge_epi, m=s_len, tm=tm, n=d, tn=512)

    def resid_epi(accs, ex, out):
        out[0][...] = ex[0][...] + accs[0]

    (x1,) = _mm("out_proj", [(merged, w_o_f, False, None)], [(xs, _tile(tm, 512))],
                [(_sds((s_len, d), F32), _tile(tm, 512))], resid_epi, m=s_len, tm=tm, n=d, tn=512)

    h2, r2 = _rms_fwd("rms2", x1, norm2_g, tm)

    def up_epi(accs, ex, out):
        a = accs[0]
        out[0][...] = a
        out[1][...] = jnp.square(jnp.maximum(a, 0.0)).astype(BF16)

    a_up, act = _mm("mlp_up", [(h2, w_up_f, False, None)], [],
                    [(_sds((s_len, D_FF), F32), _tile(tm, 512)), (_sds((s_len, D_FF), BF16), _tile(tm, 512))],
                    up_epi, m=s_len, tm=tm, n=D_FF, tn=512)

    def first_step():
        return jnp.logical_and(pl.program_id(0) == 0, pl.program_id(1) == 0)

    def accumulate(ref, val):
        @pl.when(first_step())
        def _():
            ref[...] = val

        @pl.when(jnp.logical_not(first_step()))
        def _():
            ref[...] += val

    def final_epi(accs, ex, out):
        x1_ref, t_ref, g_ref = ex
        x2 = x1_ref[...] + accs[0]
        rf = lax.rsqrt(jnp.mean(x2 * x2, axis=-1, keepdims=True) + EPS)
        xh = x2 * rf
        gf = g_ref[...]
        err = xh * gf - t_ref[...]
        dy = err * (1.0 / d)
        out[0][...] = _rms_bwd(xh, rf, gf, dy)
        accumulate(out[1], jnp.sum(dy * xh, axis=0, keepdims=True))
        part = 0.5 * jnp.sum(jnp.sum(err * err, axis=-1, keepdims=True) * (1.0 / d), axis=0, keepdims=True)
        accumulate(out[2], jnp.broadcast_to(part, (1, LANES)))

    gf2 = normf_g.reshape(1, d)
    dx2, g_normf, loss_part = _mm(
        "mlp_down_loss", [(act, w_down_f, False, None)],
        [(x1, _row(tr, d)), (target, _row(tr, d)), (gf2, _whole((1, d)))],
        [(_sds((s_len, d), F32), _row(tr, d)), (_sds((1, d), F32), _whole((1, d))), (_sds((1, LANES), F32), _whole((1, LANES)))],
        final_epi, m=s_len, tm=tr, n=d, tn=d, arbitrary=True)

    def dact_epi(accs, ex, out):
        out[0][...] = (accs[0] * (2.0 * jnp.maximum(ex[0][...], 0.0))).astype(BF16)

    (da,) = _mm("mlp_down_bwd", [(dx2, w_down_f, True, None)], [(a_up, _tile(tm, 512))],
                [(_sds((s_len, D_FF), BF16), _tile(tm, 512))], dact_epi, m=s_len, tm=tm, n=D_FF, tn=512)
    ts = min(512, s_len)
    g_down = _mm_tn("grad_w_down", act, dx2, tk=1024, tn=d, ts=ts)
    g_up = _mm_tn("grad_w_up", h2, da, tk=d, tn=D_FF // N_DEV, ts=ts, blocked=True)

    def dh2_epi(accs, ex, out):
        x1_ref, r_ref, g_ref, dx2_ref = ex
        r = r_ref[...]
        xh = x1_ref[...] * r
        dh2 = accs[0]
        out[0][...] = dx2_ref[...] + _rms_bwd(xh, r, g_ref[...], dh2)
        accumulate(out[1], jnp.sum(dh2 * xh, axis=0, keepdims=True))

    dx1, g_norm2 = _mm(
        "mlp_up_bwd", [(da, w_up_f, True, None)],
        [(x1, _row(tr, d)), (r2, _row(tr, 1)), (norm2_g, _whole((1, d))), (dx2, _row(tr, d))],
        [(_sds((s_len, d), F32), _row(tr, d)), (_sds((1, d), F32), _whole((1, d)))],
        dh2_epi, m=s_len, tm=tr, n=d, tn=d, arbitrary=True)

    def dmerge_epi(accs, ex, out):
        dm = accs[0]
        sa, sb = _sigmoid(ex[0][...]), _sigmoid(ex[1][...])
        out[0][...] = (dm * sa).astype(BF16)
        out[1][...] = (dm * sb).astype(BF16)
        out[2][...] = (dm * ex[2][...] * sa * (1.0 - sa)).astype(BF16)
        out[3][...] = (dm * ex[3][...] * sb * (1.0 - sb)).astype(BF16)

    dya, dyb, dga, dgb = _mm(
        "out_proj_bwd", [(dx1, w_o_f, True, None)],
        [(prest, _tile(tm, 512, GA_OFF // 512)), (prest, _tile(tm, 512, GB_OFF // 512)), (ya, _tile(tm, 512)), (yb, _tile(tm, 512))],
        [(_sds((s_len, d), BF16), _tile(tm, 512))] * 4, dmerge_epi, m=s_len, tm=tm, n=d, tn=512)
    g_o = _mm_tn("grad_w_o", merged, dx1, tk=512, tn=d, ts=ts).reshape(N_DEV, d // N_DEV, d)
    def col_blocks(g):
        return jnp.transpose(g.reshape(g.shape[0], N_DEV, g.shape[1] // N_DEV), (1, 0, 2))

    g_a = col_blocks(_mm_tn("grad_w_a", o, dya, tk=FOX_W, tn=d, ts=ts))
    g_b = col_blocks(_mm_tn("grad_w_b", sg, dyb, tk=SGU_W, tn=d, ts=ts))

    def do_epi(accs, ex, out):
        do = accs[0]
        out[0][...] = do
        out[1][...] = _dot_f32(do * ex[0][...], ex[1][...])

    do, delta = _mm(
        "attn_out_bwd", [(dya, w_a_f, True, None)], [(o, _row(tm, FOX_W)), (head_sel, _whole((FOX_W, LANES)))],
        [(_sds((s_len, FOX_W), F32), _row(tm, FOX_W)), (_sds((s_len, LANES), F32), _row(tm, LANES))],
        do_epi, m=s_len, tm=tm, n=FOX_W, tn=FOX_W)
    (dsg,) = _mm("sgu_out_bwd", [(dyb, w_b_f, True, None)], [], [(_sds((s_len, SGU_W), F32), _tile(tm, SGU_W))],
                 store(F32), m=s_len, tm=tm, n=SGU_W, tn=SGU_W)

    du, dsv, dw_pairs, db_pos, g_ln_g, g_ln_b = _sgu_bwd(prest, dsg, ln_v_g, ln_v_b, w_stack, wt_stack, b_pair, tm)
    g_w_sgu = jnp.where(sgu_mask[None], dw_pairs.reshape(SGU_G, SGU_LEN, SGU_LEN), 0.0)
    g_b_sgu = jnp.transpose(jnp.sum(db_pos.reshape(SGU_LEN, SGU_G, SGU_W // SGU_G), axis=-1))

    delta_row = jnp.transpose(delta[:, :N_HEADS]).reshape(N_HEADS, 1, s_len)
    dq, dk, dv, dc_rows_blk, dc_cols = _attn_bwd(qkv, k_t, do, c_col, c_row, lse_row, delta_row, ta, ta)
    dc_rows = jnp.transpose(dc_rows_blk.reshape(s_len // ta, N_HEADS, ta), (0, 2, 1)).reshape(s_len, N_HEADS)
    dc_rows = jnp.pad(dc_rows, ((0, 0), (0, LANES - N_HEADS)))
    dfl, g_bf = _forget_bwd(dc_rows, dc_cols, prest, b_f_pad, ta)

    dp = (dq, dk, dv, du, dsv, dga, dgb)
    g_in_cols = [_mm_tn("grad_w_in_%d" % k, h, seg, tk=d, tn=512, ts=ts) for k, seg in enumerate(dp)]
    g_f = _mm_tn("grad_w_in_f", h, dfl, tk=d, tn=LANES, ts=ts)[:, :N_HEADS]
    g_in_full = jnp.concatenate(g_in_cols[:3] + [g_f] + g_in_cols[3:], axis=1)
    g_in = jnp.transpose(g_in_full.reshape(d, N_DEV, IN_SHARD), (1, 0, 2))

    def dx_epi(accs, ex, out):
        x_ref, r_ref, g_ref, dx1_ref = ex
        dh = accs[0]
        for extra in accs[1:]:
            dh = dh + extra
        r = r_ref[...]
        xh = x_ref[...] * r
        out[0][...] = dx1_ref[...] + _rms_bwd(xh, r, g_ref[...], dh)
        accumulate(out[1], jnp.sum(dh * xh, axis=0, keepdims=True))

    rest_cols = ((du, U_OFF, 512), (dsv, SV_OFF, 512), (dga, GA_OFF, 1024), (dgb, GB_OFF, 1024), (dfl, F_OFF, LANES))
    dx_pairs = [(seg, w_qkv, True, (512 * k, 512 * (k + 1))) for k, seg in enumerate((dq, dk, dv))]
    dx_pairs += [(seg, w_rest, True, (lo, lo + width)) for seg, lo, width in rest_cols]
    grad_x, g_norm1 = _mm(
        "proj_bwd", dx_pairs,
        [(xs, _row(tr, d)), (r1, _row(tr, 1)), (norm1_g, _whole((1, d))), (dx1, _row(tr, d))],
        [(_sds((s_len, d), F32), _row(tr, d)), (_sds((1, d), F32), _whole((1, d)))],
        dx_epi, m=s_len, tm=tr, n=d, tn=d, arbitrary=True)

    my_c = lax.axis_index("c")
    my_chip = 2 * lax.axis_index("x") + lax.axis_index("y")
    idx = jnp.stack([my_c, my_chip]).astype(jnp.int32)
    grads8 = (g_in, g_a, g_b, g_o, g_up, g_down.reshape(N_DEV, D_FF // N_DEV, d))
    grads4 = [g.reshape((4, 2) + g.shape[1:]) for g in grads8]
    from_sibling = _pair_exchange("grad_pair_exchange", grads4)
    names = ("w_in", "w_a", "w_b", "w_o", "w_up", "w_down")
    parts16, owns = [], []
    for name, g4, recv in zip(names, grads4, from_sibling):
        p16, own = _pair_sum("grad_pair_sum_" + name, g4, recv, idx, min(128, g4.shape[2]))
        parts16.append(p16)
        owns.append(own)
    from_chips = _chip_exchange("grad_chip_exchange", parts16)
    moments_m = (m_w_in, m_w_a, m_w_b, m_w_o, m_w_up, m_w_down)
    moments_v = (v_w_in, v_w_a, v_w_b, v_w_o, v_w_up, v_w_down)
    big_out = {}
    for name, own, recv, w, m, v in zip(names, owns, from_chips, big, moments_m, moments_v):
        res = _adamw_shard("adamw_" + name, own, recv, w, m[0], v[0], min(128, own.shape[0]))
        big_out[name] = [t[None] for t in res]

    small_g = _pack_small((g_w_sgu, g_b_sgu, g_norm1, g_norm2, g_normf, g_ln_g, g_ln_b, g_bf[:, :N_HEADS]))
    small_all = _all_gather("gather_small_grads", small_g)
    small_w = _pack_small((w_sgu, b_sgu, norm1_g, norm2_g, normf_g, ln_v_g, ln_v_b, b_f))
    small_m = _pack_small((m_w_sgu, m_b_sgu, m_norm1_g, m_norm2_g, m_normf_g, m_ln_v_g, m_ln_v_b, m_b_f))
    small_v = _pack_small((v_w_sgu, v_b_sgu, v_norm1_g, v_norm2_g, v_normf_g, v_ln_v_g, v_ln_v_b, v_b_f))
    small_res = [_unpack_small(t) for t in _adamw_small(small_all, small_w, small_m, small_v)]
    small_names = [n for n, _ in _SMALL]
    small_out = {n: [res[k] for res in small_res] for k, n in enumerate(small_names)}

    loss = lax.psum(loss_part[0, 0], ("x", "y", "c"))

    order = ("norm1_g", "w_in", "b_f", "ln_v_g", "ln_v_b", "w_sgu", "b_sgu", "w_a", "w_b", "w_o", "norm2_g", "w_up",
             "w_down", "normf_g")
    table = {**big_out, **small_out}
    outs = [loss, grad_x[None]]
    for kind in range(4):
        outs += [table[n][kind] for n in order]
    return tuple(outs)
```

```python
import math

import jax
import jax.numpy as jnp
from jax import lax
from jax.experimental import pallas as pl
from jax.experimental.pallas import tpu as pltpu

F32 = jnp.float32
BF16 = jnp.bfloat16

N_DEV = 8
D_MODEL = 1024
N_HEADS = 8
HEAD_DIM = 64
FOX_W = N_HEADS * HEAD_DIM
SGU_G = 8
SGU_W = 512
SGU_LEN = 128
CHUNK = 64
D_FF = 4 * D_MODEL
IN_COLS = 3 * FOX_W + N_HEADS + 2 * SGU_W + 2 * D_MODEL
IN_SHARD = IN_COLS // N_DEV
LANES = 128
QKV_W = 3 * FOX_W
U_OFF, SV_OFF, GA_OFF, GB_OFF, F_OFF = 0, 512, 1024, 2048, 3072
REST_W = F_OFF + LANES
EPS = 1e-6
NEG = -1e30

ADAM_LR = 0.001
ADAM_B1 = 0.9
ADAM_B2 = 0.999
ADAM_EPS = 1e-08
ADAM_WD = 0.01
ADAM_STEP = 10

VMEM_LIMIT = 56 * 1024 * 1024
MESH = pl.DeviceIdType.MESH


def _params(sem=None):
    return pltpu.CompilerParams(dimension_semantics=sem, vmem_limit_bytes=VMEM_LIMIT)


def _dot(a, b):
    return jnp.dot(a, b, preferred_element_type=F32)


def _dot_nt(a, b):
    return lax.dot_general(a, b, (((1,), (1,)), ((), ())), preferred_element_type=F32)


def _dot_tn(a, b):
    return lax.dot_general(a, b, (((0,), (0,)), ((), ())), preferred_element_type=F32)


def _dot_f32(a, b):
    return jnp.dot(a, b, preferred_element_type=F32, precision=lax.Precision.HIGHEST)


def _sigmoid(x):
    return 1.0 / (1.0 + jnp.exp(-x))


def _log_sigmoid(z):
    return jnp.minimum(z, 0.0) - jnp.log(1.0 + jnp.exp(-jnp.abs(z)))


_GELU_K = math.sqrt(2.0 / math.pi)
_GELU_C = 0.044715


def _gelu(x):
    t = jnp.tanh(_GELU_K * (x + _GELU_C * (x * x * x)))
    return 0.5 * x * (1.0 + t)


def _gelu_grad(x):
    x2 = x * x
    t = jnp.tanh(_GELU_K * (x + _GELU_C * (x2 * x)))
    return 0.5 * (1.0 + t) + 0.5 * x * (1.0 - t * t) * (_GELU_K * (1.0 + 3.0 * _GELU_C * x2))


def _rms_bwd(xh, r, g, dy):
    gy = dy * g
    return r * (gy - xh * jnp.mean(xh * gy, axis=-1, keepdims=True))


def _lane_lt64(shape):
    return lax.broadcasted_iota(jnp.int32, shape, len(shape) - 1) < HEAD_DIM


class _Comm:
    def __init__(self, arrays, out_shapes, sems, start, finish, mid=None):
        self.arrays, self.out_shapes, self.sems = list(arrays), list(out_shapes), list(sems)
        self.start, self.mid, self.finish = start, mid, finish


def _comm_phase(plans, phase, in_refs, out_refs, sem_refs):
    ia = io = ks = 0
    for plan in plans:
        na, no, ns = len(plan.arrays), len(plan.out_shapes), len(plan.sems)
        fn = getattr(plan, phase)
        if fn is not None:
            fn(in_refs[ia:ia + na], out_refs[io:io + no], sem_refs[ks:ks + ns])
        ia, io, ks = ia + na, io + no, ks + ns


def _comm_operands(plans):
    arrays = [a for plan in plans for a in plan.arrays]
    out_shapes = [o for plan in plans for o in plan.out_shapes]
    sems = [s for plan in plans for s in plan.sems]
    return arrays, out_shapes, sems


_ANY = pl.BlockSpec(memory_space=pl.ANY)


def _run_comm(name, plans):
    arrays, out_shapes, sems = _comm_operands(plans)
    n_in, n_out = len(arrays), len(out_shapes)

    def body(*refs):
        parts = refs[:n_in], refs[n_in:n_in + n_out], refs[n_in + n_out:]
        for phase in ("start", "mid", "finish"):
            _comm_phase(plans, phase, *parts)

    return pl.pallas_call(
        body, name=name, out_shape=out_shapes, in_specs=[_ANY] * n_in, out_specs=[_ANY] * n_out, scratch_shapes=sems,
    )(*arrays)


def _gather_plan(shard):
    def setup(ins, outs, sems):
        (x_ref,), (out_ref,), (send_sems, recv_sems, local_sem) = ins, outs, sems
        x, y, c = lax.axis_index("x"), lax.axis_index("y"), lax.axis_index("c")
        me, sibling = (x, y, c), (x, y, 1 - c)
        chips = [(1 - x, y), (x, 1 - y), (1 - x, 1 - y)]

        def rows(px, py, pc):
            return out_ref.at[4 * px + 2 * py + pc]

        def copy(k, block, to, src=None):
            return pltpu.make_async_remote_copy(
                src_ref=rows(*block) if src is None else src,
                dst_ref=rows(*block),
                send_sem=send_sems.at[k],
                recv_sem=recv_sems.at[k],
                device_id=to,
                device_id_type=MESH,
            )

        mine = pltpu.make_async_copy(x_ref, rows(*me), local_sem)
        first = [copy(0, me, sibling, src=x_ref)]
        first += [copy(1 + j, me, (*chip, c), src=x_ref) for j, chip in enumerate(chips)]
        passed = [copy(4 + j, (*chip, c), sibling) for j, chip in enumerate(chips)]
        landed = [copy(1 + j, (*chip, c), me) for j, chip in enumerate(chips)]
        from_sibling = [copy(0, sibling, me)] + [copy(4 + j, (*chip, 1 - c), me) for j, chip in enumerate(chips)]
        return mine, first, passed, landed, from_sibling

    def start(ins, outs, sems):
        mine, first, _, _, _ = setup(ins, outs, sems)
        mine.start()
        for cp in first:
            cp.start()

    def mid(ins, outs, sems):
        _, _, passed, landed, _ = setup(ins, outs, sems)
        for arrived, onward in zip(landed, passed):
            arrived.wait_recv()
            onward.start()

    def finish(ins, outs, sems):
        mine, first, passed, _, from_sibling = setup(ins, outs, sems)
        for cp in from_sibling:
            cp.wait_recv()
        for cp in first + passed:
            cp.wait_send()
        mine.wait()

    return _Comm([shard], [jax.ShapeDtypeStruct((N_DEV,) + shard.shape, shard.dtype)],
                 [pltpu.SemaphoreType.DMA((7,)), pltpu.SemaphoreType.DMA((7,)), pltpu.SemaphoreType.DMA],
                 start, finish, mid)


def _start_all(copies):
    for cp in copies:
        cp.start()


def _wait_all(copies):
    for cp in copies:
        cp.wait_recv()
    for cp in copies:
        cp.wait_send()


def _pair_exchange_plan(grads):
    n = len(grads)

    def copies(ins, outs, sems):
        send_sems, recv_sems = sems
        x, y, c = lax.axis_index("x"), lax.axis_index("y"), lax.axis_index("c")
        return [
            pltpu.make_async_remote_copy(
                src_ref=ins[k].at[:, 1 - c],
                dst_ref=outs[k],
                send_sem=send_sems.at[k],
                recv_sem=recv_sems.at[k],
                device_id=(x, y, 1 - c),
                device_id_type=MESH,
            )
            for k in range(n)
        ]

    return _Comm(grads, [jax.ShapeDtypeStruct((4,) + g.shape[2:], g.dtype) for g in grads],
                 [pltpu.SemaphoreType.DMA((n,)), pltpu.SemaphoreType.DMA((n,))],
                 lambda *refs: _start_all(copies(*refs)), lambda *refs: _wait_all(copies(*refs)))


def _chip_exchange_plan(parts):
    n = len(parts)

    def copies(ins, outs, sems):
        send_sems, recv_sems = sems
        x, y, c = lax.axis_index("x"), lax.axis_index("y"), lax.axis_index("c")
        chips = [(1 - x, y), (x, 1 - y), (1 - x, 1 - y)]
        return [
            pltpu.make_async_remote_copy(
                src_ref=ins[k].at[2 * px + py],
                dst_ref=outs[k].at[j],
                send_sem=send_sems.at[3 * k + j],
                recv_sem=recv_sems.at[3 * k + j],
                device_id=(px, py, c),
                device_id_type=MESH,
            )
            for k in range(n) for j, (px, py) in enumerate(chips)
        ]

    return _Comm(parts, [jax.ShapeDtypeStruct((3,) + p.shape[1:], p.dtype) for p in parts],
                 [pltpu.SemaphoreType.DMA((3 * n,)), pltpu.SemaphoreType.DMA((3 * n,))],
                 lambda *refs: _start_all(copies(*refs)), lambda *refs: _wait_all(copies(*refs)))


def _mm(name, pairs, extras, outs, epi, *, m, tm, n, tn, arbitrary=False, comm=()):
    nj = n // tn
    a_arrays, a_specs, b_arrays, b_specs, b_index = [], [], [], [], []
    for a, b, nt, cols in pairs:
        a_arrays.append(a)
        a_specs.append(pl.BlockSpec((tm, a.shape[1]), lambda i, j: (i, 0)))
        known = [k for k, other in enumerate(b_arrays) if other is b]
        if known:
            b_index.append(known[0])
            continue
        b_index.append(len(b_arrays))
        b_arrays.append(b)
        if cols is not None:
            assert nj == 1
            b_specs.append(pl.BlockSpec(b.shape, lambda i, j: (0, 0)))
        elif nt:
            b_specs.append(pl.BlockSpec((tn, b.shape[1]), lambda i, j: (j, 0)))
        else:
            b_specs.append(pl.BlockSpec((b.shape[0], tn), lambda i, j: (0, j)))
    comm_arrays, comm_outs, comm_sems = _comm_operands(comm)
    arrays = a_arrays + b_arrays + [arr for arr, _ in extras] + comm_arrays
    in_specs = a_specs + b_specs + [spec for _, spec in extras] + [_ANY] * len(comm_arrays)
    n_a, n_b, n_extras, n_ci, n_out, n_co = len(a_arrays), len(b_arrays), len(extras), len(comm_arrays), len(outs), len(comm_outs)
    ni = m // tm

    def body(*refs):
        a_refs = refs[:n_a]
        b_refs = refs[n_a:n_a + n_b]
        ex = refs[n_a + n_b:n_a + n_b + n_extras]
        n_in = n_a + n_b + n_extras + n_ci
        comm_refs = refs[n_in - n_ci:n_in], refs[n_in + n_out:n_in + n_out + n_co], refs[n_in + n_out + n_co:]
        out = refs[n_in:n_in + n_out]
        if comm:
            @pl.when(jnp.logical_and(pl.program_id(0) == 0, pl.program_id(1) == 0))
            def _():
                _comm_phase(comm, "start", *comm_refs)

        accs = []
        for p, (_, _, nt, cols) in enumerate(pairs):
            av = a_refs[p][...]
            if av.dtype != BF16:
                av = av.astype(BF16)
            b_ref = b_refs[b_index[p]]
            bv = b_ref[...] if cols is None else b_ref[:, cols[0]:cols[1]]
            accs.append(_dot_nt(av, bv) if nt else _dot(av, bv))
        epi(accs, ex, out)
        if comm:
            @pl.when(jnp.logical_and(pl.program_id(0) == ni - 1, pl.program_id(1) == nj - 1))
            def _():
                _comm_phase(comm, "mid", *comm_refs)
                _comm_phase(comm, "finish", *comm_refs)

    sem = ("arbitrary", "arbitrary") if arbitrary or comm else ("parallel", "parallel")
    res = pl.pallas_call(
        body,
        name=name,
        grid=(ni, nj),
        in_specs=in_specs,
        out_specs=[spec for _, spec in outs] + [_ANY] * n_co,
        out_shape=[shape for shape, _ in outs] + comm_outs,
        scratch_shapes=comm_sems,
        compiler_params=_params(sem),
    )(*arrays)
    return (res[:n_out], res[n_out:]) if comm else res


def _tile(tm, tn, off=0):
    return pl.BlockSpec((tm, tn), lambda i, j: (i, j + off))


def _row(tm, w, blk=0):
    return pl.BlockSpec((tm, w), lambda i, j: (i, blk))


def _whole(shape):
    zeros = (0,) * len(shape)
    return pl.BlockSpec(shape, lambda i, j: zeros)


def _sds(shape, dtype):
    return jax.ShapeDtypeStruct(shape, dtype)


def _mm_tn(name, a, g, *, tk, tn, ts, blocked=False):
    s_len, ka = a.shape
    n = g.shape[1]

    def body(a_ref, g_ref, o_ref):
        part = _dot_tn(a_ref[...].astype(BF16), g_ref[...].astype(BF16))

        @pl.when(pl.program_id(2) == 0)
        def _():
            o_ref[...] = part

        @pl.when(pl.program_id(2) > 0)
        def _():
            o_ref[...] += part

    if blocked:
        out_shape = _sds((n // tn, ka, tn), F32)
        out_spec = pl.BlockSpec((None, tk, tn), lambda i, j, s: (j, i, 0))
    else:
        out_shape = _sds((ka, n), F32)
        out_spec = pl.BlockSpec((tk, tn), lambda i, j, s: (i, j))
    return pl.pallas_call(
        body,
        name=name,
        grid=(ka // tk, n // tn, s_len // ts),
        in_specs=[pl.BlockSpec((ts, tk), lambda i, j, s: (s, i)), pl.BlockSpec((ts, tn), lambda i, j, s: (s, j))],
        out_specs=out_spec,
        out_shape=out_shape,
        compiler_params=_params(("parallel", "parallel", "arbitrary")),
    )(a, g)


def _rms_fwd(name, x, g, tm):
    s_len, d = x.shape

    def body(x_ref, g_ref, h_ref, r_ref):
        xv = x_ref[...]
        r = lax.rsqrt(jnp.mean(xv * xv, axis=-1, keepdims=True) + EPS)
        h_ref[...] = (xv * r * g_ref[...]).astype(BF16)
        r_ref[...] = r

    return pl.pallas_call(
        body,
        name=name,
        grid=(s_len // tm,),
        in_specs=[pl.BlockSpec((tm, d), lambda i: (i, 0)), pl.BlockSpec((1, d), lambda i: (0, 0))],
        out_specs=[pl.BlockSpec((tm, d), lambda i: (i, 0)), pl.BlockSpec((tm, 1), lambda i: (i, 0))],
        out_shape=[_sds((s_len, d), BF16), _sds((s_len, 1), F32)],
        compiler_params=_params(("parallel",)),
    )(x, g)


def _forget_cumsum(prest, b_f_pad, tc):
    s_len = prest.shape[0]

    def body(f_ref, b_ref, c_ref, carry):
        @pl.when(pl.program_id(0) == 0)
        def _():
            carry[...] = jnp.zeros_like(carry)

        logf = _log_sigmoid(f_ref[...] + b_ref[...])
        row = lax.broadcasted_iota(jnp.int32, (tc, tc), 0)
        col = lax.broadcasted_iota(jnp.int32, (tc, tc), 1)
        tri = (row >= col).astype(F32)
        c = _dot_f32(tri, logf) + carry[...]
        c_ref[...] = c
        carry[...] = c[tc - 1:tc, :]

    return pl.pallas_call(
        body,
        name="forget_cumsum",
        grid=(s_len // tc,),
        in_specs=[pl.BlockSpec((tc, LANES), lambda i: (i, F_OFF // LANES)), pl.BlockSpec((1, LANES), lambda i: (0, 0))],
        out_specs=pl.BlockSpec((tc, LANES), lambda i: (i, 0)),
        out_shape=_sds((s_len, LANES), F32),
        scratch_shapes=[pltpu.VMEM((1, LANES), F32)],
        compiler_params=_params(("arbitrary",)),
    )(prest, b_f_pad)


def _stack_heads(pair, lt64):
    zero = jnp.zeros_like(pair)
    return jnp.concatenate([jnp.where(lt64, pair, zero), jnp.where(lt64, zero, pair)], axis=0)


def _score_tiles(q_ref, k_ref, cq_ref, ck_ref, st_sc, tk):
    lt64 = _lane_lt64((tk, LANES))
    for p in range(N_HEADS // 2):
        lanes = slice(p * LANES, (p + 1) * LANES)
        q_pair = q_ref[:, lanes] * jnp.asarray(HEAD_DIM ** -0.5, BF16)
        st2 = _dot_nt(_stack_heads(k_ref[:, lanes], lt64), q_pair)
        for half in range(2):
            h = 2 * p + half
            st_sc[h] = st2[half * tk:(half + 1) * tk] + (cq_ref[h] - ck_ref[:, h:h + 1])


def _mask_diagonal(st_sc, i, j, tq, tk):
    @pl.when((j + 1) * tk - 1 > i * tq)
    def _():
        key = j * tk + lax.broadcasted_iota(jnp.int32, (tk, tq), 0)
        query = i * tq + lax.broadcasted_iota(jnp.int32, (tk, tq), 1)
        st_sc[...] = jnp.where((query >= key)[None], st_sc[...], NEG)


def _attn_fwd(qkv, v_t, c_col, c_row, tq, tk, comm=()):
    s_len = qkv.shape[0]
    ratio = tq // tk
    steps = [(i, j) for i in range(s_len // tq) for j in range((i + 1) * ratio)]
    i_tab = jnp.asarray([i for i, _ in steps], jnp.int32)
    j_tab = jnp.asarray([j for _, j in steps], jnp.int32)

    comm_arrays, comm_outs, comm_sems = _comm_operands(comm)
    n_ci, n_co = len(comm_arrays), len(comm_outs)

    def body(i_ref, j_ref, q_ref, k_ref, vt_ref, cq_ref, ck_ref, *rest):
        comm_refs = rest[:n_ci], rest[n_ci + 2:n_ci + 2 + n_co], rest[n_ci + 2 + n_co + 5:]
        o_ref, lse_ref = rest[n_ci:n_ci + 2]
        acc_t, m_sc, l_sc, st_sc, p_sc = rest[n_ci + 2 + n_co:n_ci + 2 + n_co + 5]
        n = pl.program_id(0)
        i, j = i_ref[n], j_ref[n]
        for phase, at in (("start", 0), ("mid", (2 * len(steps)) // 3)):
            if comm:
                @pl.when(n == at)
                def _():
                    _comm_phase(comm, phase, *comm_refs)

        @pl.when(j == 0)
        def _():
            acc_t[...] = jnp.zeros_like(acc_t)
            m_sc[...] = jnp.full_like(m_sc, NEG)
            l_sc[...] = jnp.zeros_like(l_sc)

        _score_tiles(q_ref, k_ref, cq_ref, ck_ref, st_sc, tk)
        _mask_diagonal(st_sc, i, j, tq, tk)
        st = st_sc[...]
        m_old = m_sc[...]
        m_new = jnp.maximum(m_old, jnp.max(st, axis=1, keepdims=True))
        alpha = jnp.exp(m_old - m_new)
        pt = jnp.exp(st - m_new)
        l_sc[...] = alpha * l_sc[...] + jnp.sum(pt, axis=1, keepdims=True)
        m_sc[...] = m_new
        p_sc[...] = pt.astype(BF16)
        top = lax.broadcasted_iota(jnp.int32, (LANES, tq), 0) < HEAD_DIM
        for p in range(N_HEADS // 2):
            lanes = slice(p * LANES, (p + 1) * LANES)
            vt_pair = vt_ref[lanes, :]
            pv = jnp.where(top, _dot(vt_pair, p_sc[2 * p]), _dot(vt_pair, p_sc[2 * p + 1]))
            acc_t[lanes, :] = acc_t[lanes, :] * jnp.where(top, alpha[2 * p], alpha[2 * p + 1]) + pv

        @pl.when(j == (i + 1) * ratio - 1)
        def _():
            for p in range(N_HEADS // 2):
                lanes = slice(p * LANES, (p + 1) * LANES)
                l_pair = jnp.where(top, l_sc[2 * p], l_sc[2 * p + 1])
                o_ref[:, lanes] = jnp.transpose(acc_t[lanes, :] / l_pair)
            lse_ref[...] = m_sc[...] + jnp.log(l_sc[...])

        if comm:
            @pl.when(n == len(steps) - 1)
            def _():
                _comm_phase(comm, "finish", *comm_refs)

    stat = pltpu.VMEM((N_HEADS, 1, tq), F32)
    res = pl.pallas_call(
        body,
        name="attn_fwd",
        grid_spec=pltpu.PrefetchScalarGridSpec(
            num_scalar_prefetch=2,
            grid=(len(steps),),
            in_specs=[
                pl.BlockSpec((tq, FOX_W), lambda n, it, jt: (it[n], 0)),
                pl.BlockSpec((tk, FOX_W), lambda n, it, jt: (jt[n], 1)),
                pl.BlockSpec((FOX_W, tk), lambda n, it, jt: (0, jt[n])),
                pl.BlockSpec((N_HEADS, 1, tq), lambda n, it, jt: (0, 0, it[n])),
                pl.BlockSpec((tk, LANES), lambda n, it, jt: (jt[n], 0)),
            ] + [_ANY] * n_ci,
            out_specs=[
                pl.BlockSpec((tq, FOX_W), lambda n, it, jt: (it[n], 0)),
                pl.BlockSpec((N_HEADS, 1, tq), lambda n, it, jt: (0, 0, it[n])),
            ] + [_ANY] * n_co,
            scratch_shapes=[pltpu.VMEM((FOX_W, tq), F32), stat, stat, pltpu.VMEM((N_HEADS, tk, tq), F32),
                            pltpu.VMEM((N_HEADS, tk, tq), BF16)] + comm_sems,
        ),
        out_shape=[_sds((s_len, FOX_W), F32), _sds((N_HEADS, 1, s_len), F32)] + comm_outs,
        compiler_params=_params(("arbitrary",)),
    )(i_tab, j_tab, qkv, qkv, v_t, c_row, c_col, *comm_arrays)
    return res[0], res[1], res[2:]


def _sgu_mix(vn, w_stack, lt64):
    outs = []
    for p in range(SGU_G // 2):
        r = _dot(w_stack[p], vn[:, p * LANES:(p + 1) * LANES])
        outs.append(jnp.where(lt64, r[:SGU_LEN], r[SGU_LEN:]))
    return jnp.concatenate(outs, axis=1)


def _sgu_norm(sv, ln_g, ln_b):
    svg = _gelu(sv)
    xc = svg - jnp.mean(svg, axis=-1, keepdims=True)
    rstd = lax.rsqrt(jnp.mean(xc * xc, axis=-1, keepdims=True) + EPS)
    xhat = xc * rstd
    return xhat, rstd, xhat * ln_g + ln_b


def _sgu_fwd(prest, ln_g, ln_b, w_stack, b_pair, tm):
    s_len = prest.shape[0]

    def body(u_ref, sv_ref, g_ref, b_ref, w_ref, bp_ref, sg_ref):
        lt64 = _lane_lt64((SGU_LEN, LANES))
        _, _, vn = _sgu_norm(sv_ref[...], g_ref[...], b_ref[...])
        vn = vn.astype(BF16)
        w_stack_v = [w_ref[p] for p in range(SGU_G // 2)]
        for w in range(tm // SGU_LEN):
            win = slice(w * SGU_LEN, (w + 1) * SGU_LEN)
            mixed = _sgu_mix(vn[win], w_stack_v, lt64) + bp_ref[...]
            sg_ref[win, :] = (_gelu(u_ref[win, :]) * mixed).astype(BF16)

    return pl.pallas_call(
        body,
        name="sgu_fwd",
        grid=(s_len // tm,),
        in_specs=[
            pl.BlockSpec((tm, SGU_W), lambda i: (i, U_OFF // SGU_W)),
            pl.BlockSpec((tm, SGU_W), lambda i: (i, SV_OFF // SGU_W)),
            pl.BlockSpec((1, SGU_W), lambda i: (0, 0)),
            pl.BlockSpec((1, SGU_W), lambda i: (0, 0)),
            pl.BlockSpec((SGU_G // 2, 2 * SGU_LEN, SGU_LEN), lambda i: (0, 0, 0)),
            pl.BlockSpec((SGU_LEN, SGU_W), lambda i: (0, 0)),
        ],
        out_specs=pl.BlockSpec((tm, SGU_W), lambda i: (i, 0)),
        out_shape=_sds((s_len, SGU_W), BF16),
        compiler_params=_params(("parallel",)),
    )(prest, prest, ln_g, ln_b, w_stack, b_pair)


def _sgu_bwd(prest, dsg, ln_g, ln_b, w_stack, wt_stack, b_pair, tm):
    s_len = prest.shape[0]
    n_pair = SGU_G // 2

    def body(u_ref, sv_ref, dsg_ref, g_ref, b_ref, w_ref, wt_ref, bp_ref,
             du_ref, dsv_ref, dw_ref, db_ref, dg_ref, dbeta_ref, dvn_sc):
        @pl.when(pl.program_id(0) == 0)
        def _():
            dw_ref[...] = jnp.zeros_like(dw_ref)
            db_ref[...] = jnp.zeros_like(db_ref)
            dg_ref[...] = jnp.zeros_like(dg_ref)
            dbeta_ref[...] = jnp.zeros_like(dbeta_ref)

        lt64 = _lane_lt64((SGU_LEN, LANES))
        sv = sv_ref[...]
        xhat, rstd, vn32 = _sgu_norm(sv, g_ref[...], b_ref[...])
        vn = vn32.astype(BF16)
        w_stack_v = [w_ref[p] for p in range(n_pair)]
        db = jnp.zeros((SGU_LEN, SGU_W), F32)
        for w in range(tm // SGU_LEN):
            win = slice(w * SGU_LEN, (w + 1) * SGU_LEN)
            u = u_ref[win, :]
            dsg_w = dsg_ref[win, :]
            mixed = _sgu_mix(vn[win], w_stack_v, lt64) + bp_ref[...]
            du_ref[win, :] = (dsg_w * mixed * _gelu_grad(u)).astype(BF16)
            dmixed = dsg_w * _gelu(u)
            db = db + dmixed
            dm16 = dmixed.astype(BF16)
            for p in range(n_pair):
                lanes = slice(p * LANES, (p + 1) * LANES)
                dmp = dm16[:, lanes]
                r = _dot(wt_ref[p], dmp)
                dvn_sc[win, lanes] = jnp.where(lt64, r[:SGU_LEN], r[SGU_LEN:])
                zero = jnp.zeros_like(dmp)
                dm_ab = jnp.concatenate([jnp.where(lt64, dmp, zero), jnp.where(lt64, zero, dmp)], axis=0)
                dw_ref[p] += _dot_nt(dm_ab, vn[win, lanes])
        db_ref[...] += db
        dvn = dvn_sc[...]
        dg_ref[...] += jnp.sum(dvn * xhat, axis=0, keepdims=True)
        dbeta_ref[...] += jnp.sum(dvn, axis=0, keepdims=True)
        dxh = dvn * g_ref[...]
        dsvg = rstd * (dxh - jnp.mean(dxh, axis=-1, keepdims=True) - xhat * jnp.mean(dxh * xhat, axis=-1, keepdims=True))
        dsv_ref[...] = (dsvg * _gelu_grad(sv)).astype(BF16)

    const2 = lambda i: (0, 0)
    const3 = lambda i: (0, 0, 0)
    return pl.pallas_call(
        body,
        name="sgu_bwd",
        grid=(s_len // tm,),
        in_specs=[
            pl.BlockSpec((tm, SGU_W), lambda i: (i, U_OFF // SGU_W)),
            pl.BlockSpec((tm, SGU_W), lambda i: (i, SV_OFF // SGU_W)),
            pl.BlockSpec((tm, SGU_W), lambda i: (i, 0)),
            pl.BlockSpec((1, SGU_W), const2),
            pl.BlockSpec((1, SGU_W), const2),
            pl.BlockSpec((n_pair, 2 * SGU_LEN, SGU_LEN), const3),
            pl.BlockSpec((n_pair, 2 * SGU_LEN, SGU_LEN), const3),
            pl.BlockSpec((SGU_LEN, SGU_W), const2),
        ],
        out_specs=[
            pl.BlockSpec((tm, SGU_W), lambda i: (i, 0)),
            pl.BlockSpec((tm, SGU_W), lambda i: (i, 0)),
            pl.BlockSpec((n_pair, 2 * SGU_LEN, SGU_LEN), const3),
            pl.BlockSpec((SGU_LEN, SGU_W), const2),
            pl.BlockSpec((1, SGU_W), const2),
            pl.BlockSpec((1, SGU_W), const2),
        ],
        out_shape=[
            _sds((s_len, SGU_W), BF16), _sds((s_len, SGU_W), BF16), _sds((n_pair, 2 * SGU_LEN, SGU_LEN), F32),
            _sds((SGU_LEN, SGU_W), F32), _sds((1, SGU_W), F32), _sds((1, SGU_W), F32),
        ],
        scratch_shapes=[pltpu.VMEM((tm, SGU_W), F32)],
        compiler_params=_params(("arbitrary",)),
    )(prest, prest, dsg, ln_g, ln_b, w_stack, wt_stack, b_pair)


def _attn_bwd(qkv, k_t, do, c_col, c_row, lse_row, delta_row, tq, tk, comm=()):
    s_len = qkv.shape[0]
    nq, nk = s_len // tq, s_len // tk
    ratio = tq // tk
    scale = HEAD_DIM ** -0.5
    steps = [(j, i) for j in range(nk) for i in range(j // ratio, nq)]
    j_tab = jnp.asarray([j for j, _ in steps], jnp.int32)
    i_tab = jnp.asarray([i for _, i in steps], jnp.int32)

    comm_arrays, comm_outs, comm_sems = _comm_operands(comm)
    n_ci, n_co = len(comm_arrays), len(comm_outs)

    def body(j_ref, i_ref, q_ref, k_ref, v_ref, kt_ref, do_ref, cq_ref, ck_ref, lse_ref, dl_ref, *rest):
        comm_refs = rest[:n_ci], rest[n_ci + 5:n_ci + 5 + n_co], rest[n_ci + 5 + n_co + 8:]
        dq_ref, dk_ref, dv_ref, dcr_ref, dcc_ref = rest[n_ci:n_ci + 5]
        dq_t, dk_acc, dv_acc, dcc_acc, st_sc, dpt_sc, p_sc, ds_sc = rest[n_ci + 5 + n_co:n_ci + 5 + n_co + 8]
        n = pl.program_id(0)
        j, i = j_ref[n], i_ref[n]

        @pl.when(n == 0)
        def _():
            _comm_phase(comm, "start", *comm_refs)
            dq_t[...] = jnp.zeros_like(dq_t)
            dcr_ref[...] = jnp.zeros_like(dcr_ref)

        @pl.when(i == j // ratio)
        def _():
            dk_acc[...] = jnp.zeros_like(dk_acc)
            dv_acc[...] = jnp.zeros_like(dv_acc)
            dcc_acc[...] = jnp.zeros_like(dcc_acc)

        lt64 = _lane_lt64((tk, LANES))
        _score_tiles(q_ref, k_ref, cq_ref, ck_ref, st_sc, tk)
        for p in range(N_HEADS // 2):
            lanes = slice(p * LANES, (p + 1) * LANES)
            dpt2 = _dot_nt(_stack_heads(v_ref[:, lanes], lt64), do_ref[:, lanes].astype(BF16))
            dpt_sc[2 * p] = dpt2[:tk]
            dpt_sc[2 * p + 1] = dpt2[tk:]
        _mask_diagonal(st_sc, i, j, tq, tk)

        pt = jnp.exp(st_sc[...] - lse_ref[...])
        dst = pt * (dpt_sc[...] - dl_ref[...])
        p_sc[...] = pt.astype(BF16)
        ds_sc[...] = dst.astype(BF16)
        dcr_ref[i] += jnp.sum(dst, axis=1, keepdims=True)
        col_sums = jnp.sum(dst, axis=2, keepdims=True)
        lane = lax.broadcasted_iota(jnp.int32, (tk, LANES), 1)
        dcc = jnp.zeros((tk, LANES), F32)
        for h in range(N_HEADS):
            dcc = jnp.where(lane == h, -col_sums[h], dcc)
        dcc_acc[...] += dcc

        for p in range(N_HEADS // 2):
            lanes = slice(p * LANES, (p + 1) * LANES)
            q_pair = q_ref[:, lanes] * jnp.asarray(scale, BF16)
            dv2 = _dot(p_sc[2 * p:2 * p + 2].reshape(2 * tk, tq), do_ref[:, lanes].astype(BF16))
            dv_acc[:, lanes] += jnp.where(lt64, dv2[:tk], dv2[tk:])
            dk2 = _dot(ds_sc[2 * p:2 * p + 2].reshape(2 * tk, tq), q_pair)
            dk_acc[:, lanes] += jnp.where(lt64, dk2[:tk], dk2[tk:])
            dq2 = _dot(kt_ref[lanes, :], jnp.concatenate([ds_sc[2 * p], ds_sc[2 * p + 1]], axis=1))
            top = lax.broadcasted_iota(jnp.int32, (LANES, tq), 0) < HEAD_DIM
            dq_t[i, lanes, :] += jnp.where(top, dq2[:, :tq], dq2[:, tq:])

        @pl.when(j == (i + 1) * ratio - 1)
        def _():
            rows = pl.ds(pl.multiple_of(i * tq, tq), tq)
            for p in range(N_HEADS // 2):
                lanes = slice(p * LANES, (p + 1) * LANES)
                dq_ref[rows, lanes] = (jnp.transpose(dq_t[i, lanes, :]) * scale).astype(BF16)

        @pl.when(i == nq - 1)
        def _():
            dk_ref[...] = dk_acc[...].astype(BF16)
            dv_ref[...] = dv_acc[...].astype(BF16)
            dcc_ref[...] = dcc_acc[...]

        if comm:
            @pl.when(n == len(steps) - 1)
            def _():
                _comm_phase(comm, "mid", *comm_refs)
                _comm_phase(comm, "finish", *comm_refs)

    q_map = lambda n, jt, it: (it[n], 0)
    q_stat = lambda n, jt, it: (0, 0, it[n])
    k_map = lambda n, jt, it: (jt[n], 0)
    tile = (N_HEADS, tk, tq)
    res = pl.pallas_call(
        body,
        name="attn_bwd",
        grid_spec=pltpu.PrefetchScalarGridSpec(
            num_scalar_prefetch=2,
            grid=(len(steps),),
            in_specs=[
                pl.BlockSpec((tq, FOX_W), q_map),
                pl.BlockSpec((tk, FOX_W), lambda n, jt, it: (jt[n], 1)),
                pl.BlockSpec((tk, FOX_W), lambda n, jt, it: (jt[n], 2)),
                pl.BlockSpec((FOX_W, tk), lambda n, jt, it: (0, jt[n])),
                pl.BlockSpec((tq, FOX_W), q_map),
                pl.BlockSpec((N_HEADS, 1, tq), q_stat),
                pl.BlockSpec((tk, LANES), k_map),
                pl.BlockSpec((N_HEADS, 1, tq), q_stat),
                pl.BlockSpec((N_HEADS, 1, tq), q_stat),
            ] + [_ANY] * n_ci,
            out_specs=[
                pl.BlockSpec((s_len, FOX_W), lambda n, jt, it: (0, 0)),
                pl.BlockSpec((tk, FOX_W), k_map),
                pl.BlockSpec((tk, FOX_W), k_map),
                pl.BlockSpec((nq, N_HEADS, 1, tq), lambda n, jt, it: (0, 0, 0, 0)),
                pl.BlockSpec((tk, LANES), k_map),
            ] + [_ANY] * n_co,
            scratch_shapes=[pltpu.VMEM((nq, FOX_W, tq), F32), pltpu.VMEM((tk, FOX_W), F32), pltpu.VMEM((tk, FOX_W), F32),
                            pltpu.VMEM((tk, LANES), F32), pltpu.VMEM(tile, F32), pltpu.VMEM(tile, F32),
                            pltpu.VMEM(tile, BF16), pltpu.VMEM(tile, BF16)] + comm_sems,
        ),
        out_shape=[_sds((s_len, FOX_W), BF16), _sds((s_len, FOX_W), BF16), _sds((s_len, FOX_W), BF16),
                   _sds((nq, N_HEADS, 1, tq), F32), _sds((s_len, LANES), F32)] + comm_outs,
        compiler_params=_params(("arbitrary",)),
    )(j_tab, i_tab, qkv, qkv, qkv, k_t, do, c_row, c_col, lse_row, delta_row, *comm_arrays)
    return res[:5], res[5:]


def _forget_bwd(dc_rows, dc_cols, prest, b_f_pad, tc):
    s_len = dc_rows.shape[0]
    nb = s_len // tc

    def body(dcr_ref, dc_ref, f_ref, b_ref, df_ref, db_ref, carry):
        @pl.when(pl.program_id(0) == 0)
        def _():
            carry[...] = jnp.zeros_like(carry)
            db_ref[...] = jnp.zeros_like(db_ref)

        row = lax.broadcasted_iota(jnp.int32, (tc, tc), 0)
        col = lax.broadcasted_iota(jnp.int32, (tc, tc), 1)
        tri = (row <= col).astype(F32)
        dlogf = _dot_f32(tri, dcr_ref[...] + dc_ref[...]) + carry[...]
        carry[...] = dlogf[0:1, :]
        z = f_ref[...] + b_ref[...]
        lane = lax.broadcasted_iota(jnp.int32, (tc, LANES), 1)
        dz = jnp.where(lane < N_HEADS, dlogf * _sigmoid(-z), 0.0)
        df_ref[...] = dz.astype(BF16)
        db_ref[...] += jnp.sum(dz, axis=0, keepdims=True)

    rev = lambda i: (nb - 1 - i, 0)
    return pl.pallas_call(
        body,
        name="forget_bwd",
        grid=(nb,),
        in_specs=[
            pl.BlockSpec((tc, LANES), rev),
            pl.BlockSpec((tc, LANES), rev),
            pl.BlockSpec((tc, LANES), lambda i: (nb - 1 - i, F_OFF // LANES)),
            pl.BlockSpec((1, LANES), lambda i: (0, 0)),
        ],
        out_specs=[pl.BlockSpec((tc, LANES), rev), pl.BlockSpec((1, LANES), lambda i: (0, 0))],
        out_shape=[_sds((s_len, LANES), BF16), _sds((1, LANES), F32)],
        scratch_shapes=[pltpu.VMEM((1, LANES), F32)],
        compiler_params=_params(("arbitrary",)),
    )(dc_rows, dc_cols, prest, b_f_pad)


def _pair_sum(name, g4, recv, idx, tr):
    _, _, r, c = g4.shape

    def body(idx_ref, g_ref, r_ref, p16_ref, own_ref):
        k = pl.program_id(1)
        s = g_ref[...] + r_ref[...]
        p16_ref[...] = s.astype(BF16)

        @pl.when(k == idx_ref[1])
        def _():
            own_ref[...] = s

    return pl.pallas_call(
        body,
        name=name,
        grid_spec=pltpu.PrefetchScalarGridSpec(
            num_scalar_prefetch=1,
            grid=(r // tr, 4),
            in_specs=[
                pl.BlockSpec((None, None, tr, c), lambda i, k, idx: (k, idx[0], i, 0)),
                pl.BlockSpec((None, tr, c), lambda i, k, idx: (k, i, 0)),
            ],
            out_specs=[
                pl.BlockSpec((None, tr, c), lambda i, k, idx: (k, i, 0)),
                pl.BlockSpec((tr, c), lambda i, k, idx: (i, 0)),
            ],
        ),
        out_shape=[_sds((4, r, c), BF16), _sds((r, c), F32)],
        compiler_params=_params(("parallel", "arbitrary")),
    )(idx, g4, recv)


def _adamw_math(w, g, m, v):
    m2 = ADAM_B1 * m + (1.0 - ADAM_B1) * g
    v2 = ADAM_B2 * v + (1.0 - ADAM_B2) * (g * g)
    m_hat = m2 / (1.0 - ADAM_B1 ** ADAM_STEP)
    v_hat = v2 / (1.0 - ADAM_B2 ** ADAM_STEP)
    delta = -ADAM_LR * (m_hat / (jnp.sqrt(v_hat) + ADAM_EPS) + ADAM_WD * w)
    return delta, m2, v2


def _adamw_shard(name, own, recv, w, m, v, tr):
    r, c = own.shape

    def body(own_ref, recv_ref, w_ref, m_ref, v_ref, g_ref, d_ref, m2_ref, v2_ref):
        g = own_ref[...]
        for k in range(3):
            g = g + recv_ref[k].astype(F32)
        delta, m2, v2 = _adamw_math(w_ref[...], g, m_ref[...], v_ref[...])
        g_ref[...] = g
        d_ref[...] = delta
        m2_ref[...] = m2
        v2_ref[...] = v2

    spec = pl.BlockSpec((tr, c), lambda i: (i, 0))
    return pl.pallas_call(
        body,
        name=name,
        grid=(r // tr,),
        in_specs=[spec, pl.BlockSpec((3, tr, c), lambda i: (0, i, 0)), spec, spec, spec],
        out_specs=[spec] * 4,
        out_shape=[_sds((r, c), F32)] * 4,
        compiler_params=_params(("parallel",)),
    )(own, recv, w, m, v)


def _adamw_small(gathered, w, m, v):
    _, r, _ = gathered.shape

    def body(ga_ref, w_ref, m_ref, v_ref, g_ref, d_ref, m2_ref, v2_ref):
        g = ga_ref[0]
        for k in range(1, N_DEV):
            g = g + ga_ref[k]
        delta, m2, v2 = _adamw_math(w_ref[...], g, m_ref[...], v_ref[...])
        g_ref[...] = g
        d_ref[...] = delta
        m2_ref[...] = m2
        v2_ref[...] = v2

    spec = pl.BlockSpec((r, LANES), lambda i: (0, 0))
    return pl.pallas_call(
        body,
        name="adamw_small",
        grid=(1,),
        in_specs=[pl.BlockSpec((N_DEV, r, LANES), lambda i: (0, 0, 0)), spec, spec, spec],
        out_specs=[spec] * 4,
        out_shape=[_sds((r, LANES), F32)] * 4,
        compiler_params=_params(("arbitrary",)),
    )(gathered, w, m, v)


_SMALL = (("w_sgu", (1, SGU_G, SGU_LEN, SGU_LEN)), ("b_sgu", (1, SGU_G, SGU_LEN)), ("norm1_g", (1, D_MODEL)),
          ("norm2_g", (1, D_MODEL)), ("normf_g", (D_MODEL,)), ("ln_v_g", (1, SGU_W)), ("ln_v_b", (1, SGU_W)),
          ("b_f", (1, N_HEADS)))


def _pack_small(values):
    rows = []
    for val in values:
        flat = val.reshape(-1).astype(F32)
        pad = (-flat.shape[0]) % LANES
        rows.append(jnp.pad(flat, (0, pad)).reshape(-1, LANES))
    packed = jnp.concatenate(rows, axis=0)
    return jnp.pad(packed, ((0, (-packed.shape[0]) % 8), (0, 0)))


def _unpack_small(packed):
    out, row = [], 0
    for _, shape in _SMALL:
        size = math.prod(shape)
        n_rows = -(-size // LANES)
        out.append(packed[row:row + n_rows].reshape(-1)[:size].reshape(shape))
        row += n_rows
    return out


def kernel(x, norm1_g, w_in, b_f, ln_v_g, ln_v_b, w_sgu, b_sgu, w_a, w_b, w_o, norm2_g, w_up, w_down, normf_g, loss_target, m_norm1_g, m_w_in, m_b_f, m_ln_v_g, m_ln_v_b, m_w_sgu, m_b_sgu, m_w_a, m_w_b, m_w_o, m_norm2_g, m_w_up, m_w_down, m_normf_g, v_norm1_g, v_w_in, v_b_f, v_ln_v_g, v_ln_v_b, v_w_sgu, v_b_sgu, v_w_a, v_w_b, v_w_o, v_norm2_g, v_w_up, v_w_down, v_normf_g):
    xs = x[0]
    target = loss_target[0]
    s_len, d = xs.shape
    tm = min(512, s_len)
    tr = min(256, s_len)
    ta = min(256, s_len)

    big = (w_in[0], w_a[0], w_b[0], w_o[0], w_up[0], w_down[0])
    (w_in_g,) = _run_comm("gather_w_in", [_gather_plan(w_in[0].astype(BF16))])
    w_in_f = jnp.transpose(w_in_g, (1, 0, 2)).reshape(d, IN_COLS)
    later = big[1:]
    later_flat = jnp.concatenate([w.reshape(-1).astype(BF16) for w in later]).reshape(-1, D_MODEL)
    later_plan = _gather_plan(later_flat)

    def unflatten(gathered):
        row, full = 0, []
        for w, col_sharded in zip(later, (True, True, False, True, False)):
            n_rows = w.size // D_MODEL
            blk = gathered[:, row:row + n_rows].reshape((N_DEV,) + w.shape)
            row += n_rows
            if col_sharded:
                full.append(jnp.transpose(blk, (1, 0, 2)).reshape(w.shape[0], N_DEV * w.shape[1]))
            else:
                full.append(blk.reshape(N_DEV * w.shape[0], w.shape[1]))
        return full

    w_qkv = w_in_f[:, :QKV_W]
    f_lo = QKV_W
    u_lo = f_lo + N_HEADS
    w_rest = jnp.concatenate([w_in_f[:, u_lo:], jnp.pad(w_in_f[:, f_lo:u_lo], ((0, 0), (0, LANES - N_HEADS)))], axis=1)

    chunk_id = jnp.arange(SGU_LEN) // CHUNK
    sgu_mask = chunk_id[None, :] <= chunk_id[:, None]
    w_masked = jnp.where(sgu_mask[None], w_sgu[0], 0.0)
    w_stack = w_masked.reshape(SGU_G // 2, 2 * SGU_LEN, SGU_LEN).astype(BF16)
    wt_stack = jnp.transpose(w_masked, (0, 2, 1)).reshape(SGU_G // 2, 2 * SGU_LEN, SGU_LEN).astype(BF16)
    b_pair = jnp.transpose(jnp.repeat(b_sgu[0], SGU_W // SGU_G, axis=0))
    b_f_pad = jnp.pad(b_f, ((0, 0), (0, LANES - N_HEADS)))
    head_sel = (jnp.arange(FOX_W)[:, None] // HEAD_DIM == jnp.arange(LANES)[None, :]).astype(F32)

    h, r1 = _rms_fwd("rms1", xs, norm1_g, tm)

    def store(dtype):
        def epi(accs, ex, out):
            out[0][...] = accs[0].astype(dtype)
        return epi

    (qkv,) = _mm("proj_qkv", [(h, w_qkv, False, None)], [], [(_sds((s_len, QKV_W), BF16), _tile(tm, 512))],
                 store(BF16), m=s_len, tm=tm, n=QKV_W, tn=512)
    (prest,) = _mm("proj_rest", [(h, w_rest, False, None)], [], [(_sds((s_len, REST_W), F32), _tile(tm, 640))],
                   store(F32), m=s_len, tm=tm, n=REST_W, tn=640)

    c_col = _forget_cumsum(prest, b_f_pad, ta)
    c_row = jnp.transpose(c_col[:, :N_HEADS]).reshape(N_HEADS, 1, s_len)
    k_t = jnp.transpose(qkv[:, FOX_W:2 * FOX_W])
    v_t = jnp.transpose(qkv[:, 2 * FOX_W:])
    o, lse_row, (later_g,) = _attn_fwd(qkv, v_t, c_col, c_row, ta, ta, comm=[later_plan])
    w_a_f, w_b_f, w_o_f, w_up_f, w_down_f = unflatten(later_g)
    sg = _sgu_fwd(prest, ln_v_g, ln_v_b, w_stack, b_pair, tm)

    def merge_epi(accs, ex, out):
        ya, yb = accs
        sa, sb = _sigmoid(ex[0][...]), _sigmoid(ex[1][...])
        out[0][...] = (sa * ya + sb * yb).astype(BF16)
        out[1][...] = ya
        out[2][...] = yb

    merged, ya, yb = _mm(
        "merge", [(o, w_a_f, False, None), (sg, w_b_f, False, None)],
        [(prest, _tile(tm, 512, GA_OFF // 512)), (prest, _tile(tm, 512, GB_OFF // 512))],
        [(_sds((s_len, d), BF16), _tile(tm, 512)), (_sds((s_len, d), F32), _tile(tm, 512)), (_sds((s_len, d), F32), _tile(tm, 512))],
        merge_epi, m=s_len, tm=tm, n=d, tn=512)

    def resid_epi(accs, ex, out):
        out[0][...] = ex[0][...] + accs[0]

    (x1,) = _mm("out_proj", [(merged, w_o_f, False, None)], [(xs, _tile(tm, 512))],
                [(_sds((s_len, d), F32), _tile(tm, 512))], resid_epi, m=s_len, tm=tm, n=d, tn=512)

    h2, r2 = _rms_fwd("rms2", x1, norm2_g, tm)

    def up_epi(accs, ex, out):
        a = accs[0]
        out[0][...] = a
        out[1][...] = jnp.square(jnp.maximum(a, 0.0)).astype(BF16)

    a_up, act = _mm("mlp_up", [(h2, w_up_f, False, None)], [],
                    [(_sds((s_len, D_FF), F32), _tile(tm, 512)), (_sds((s_len, D_FF), BF16), _tile(tm, 512))],
                    up_epi, m=s_len, tm=tm, n=D_FF, tn=512)

    def first_step():
        return jnp.logical_and(pl.program_id(0) == 0, pl.program_id(1) == 0)

    def accumulate(ref, val):
        @pl.when(first_step())
        def _():
            ref[...] = val

        @pl.when(jnp.logical_not(first_step()))
        def _():
            ref[...] += val

    def final_epi(accs, ex, out):
        x1_ref, t_ref, g_ref = ex
        x2 = x1_ref[...] + accs[0]
        rf = lax.rsqrt(jnp.mean(x2 * x2, axis=-1, keepdims=True) + EPS)
        xh = x2 * rf
        gf = g_ref[...]
        err = xh * gf - t_ref[...]
        dy = err * (1.0 / d)
        out[0][...] = _rms_bwd(xh, rf, gf, dy)
        accumulate(out[1], jnp.sum(dy * xh, axis=0, keepdims=True))
        part = 0.5 * jnp.sum(jnp.sum(err * err, axis=-1, keepdims=True) * (1.0 / d), axis=0, keepdims=True)
        accumulate(out[2], jnp.broadcast_to(part, (1, LANES)))

    gf2 = normf_g.reshape(1, d)
    dx2, g_normf, loss_part = _mm(
        "mlp_down_loss", [(act, w_down_f, False, None)],
        [(x1, _row(tr, d)), (target, _row(tr, d)), (gf2, _whole((1, d)))],
        [(_sds((s_len, d), F32), _row(tr, d)), (_sds((1, d), F32), _whole((1, d))), (_sds((1, LANES), F32), _whole((1, LANES)))],
        final_epi, m=s_len, tm=tr, n=d, tn=d, arbitrary=True)

    def dact_epi(accs, ex, out):
        out[0][...] = (accs[0] * (2.0 * jnp.maximum(ex[0][...], 0.0))).astype(BF16)

    (da,) = _mm("mlp_down_bwd", [(dx2, w_down_f, True, None)], [(a_up, _tile(tm, 512))],
                [(_sds((s_len, D_FF), BF16), _tile(tm, 512))], dact_epi, m=s_len, tm=tm, n=D_FF, tn=512)
    ts = min(512, s_len)
    g_down = _mm_tn("grad_w_down", act, dx2, tk=1024, tn=d, ts=ts)
    g_up = _mm_tn("grad_w_up", h2, da, tk=d, tn=D_FF // N_DEV, ts=ts, blocked=True)

    def dh2_epi(accs, ex, out):
        x1_ref, r_ref, g_ref, dx2_ref = ex
        r = r_ref[...]
        xh = x1_ref[...] * r
        dh2 = accs[0]
        out[0][...] = dx2_ref[...] + _rms_bwd(xh, r, g_ref[...], dh2)
        accumulate(out[1], jnp.sum(dh2 * xh, axis=0, keepdims=True))

    my_c = lax.axis_index("c")
    my_chip = 2 * lax.axis_index("x") + lax.axis_index("y")
    idx = jnp.stack([my_c, my_chip]).astype(jnp.int32)
    parts16, owns = {}, {}

    def split_cores(g8):
        return g8.reshape((4, 2) + g8.shape[1:])

    def pair_sums(names, grads4, from_sibling):
        for name, g4, recv in zip(names, grads4, from_sibling):
            parts16[name], owns[name] = _pair_sum("grad_pair_sum_" + name, g4, recv, idx, min(128, g4.shape[2]))

    grads4_mlp = [split_cores(g_up), split_cores(g_down.reshape(N_DEV, D_FF // N_DEV, d))]
    (dx1, g_norm2), from_sibling = _mm(
        "mlp_up_bwd", [(da, w_up_f, True, None)],
        [(x1, _row(tr, d)), (r2, _row(tr, 1)), (norm2_g, _whole((1, d))), (dx2, _row(tr, d))],
        [(_sds((s_len, d), F32), _row(tr, d)), (_sds((1, d), F32), _whole((1, d)))],
        dh2_epi, m=s_len, tm=tr, n=d, tn=d, arbitrary=True, comm=[_pair_exchange_plan(grads4_mlp)])
    pair_sums(("w_up", "w_down"), grads4_mlp, from_sibling)

    def dmerge_epi(accs, ex, out):
        dm = accs[0]
        sa, sb = _sigmoid(ex[0][...]), _sigmoid(ex[1][...])
        out[0][...] = (dm * sa).astype(BF16)
        out[1][...] = (dm * sb).astype(BF16)
        out[2][...] = (dm * ex[2][...] * sa * (1.0 - sa)).astype(BF16)
        out[3][...] = (dm * ex[3][...] * sb * (1.0 - sb)).astype(BF16)

    dya, dyb, dga, dgb = _mm(
        "out_proj_bwd", [(dx1, w_o_f, True, None)],
        [(prest, _tile(tm, 512, GA_OFF // 512)), (prest, _tile(tm, 512, GB_OFF // 512)), (ya, _tile(tm, 512)), (yb, _tile(tm, 512))],
        [(_sds((s_len, d), BF16), _tile(tm, 512))] * 4, dmerge_epi, m=s_len, tm=tm, n=d, tn=512)
    g_o = _mm_tn("grad_w_o", merged, dx1, tk=512, tn=d, ts=ts).reshape(N_DEV, d // N_DEV, d)
    def col_blocks(g):
        return jnp.transpose(g.reshape(g.shape[0], N_DEV, g.shape[1] // N_DEV), (1, 0, 2))

    g_a = col_blocks(_mm_tn("grad_w_a", o, dya, tk=FOX_W, tn=d, ts=ts))
    g_b = col_blocks(_mm_tn("grad_w_b", sg, dyb, tk=SGU_W, tn=d, ts=ts))

    def do_epi(accs, ex, out):
        do = accs[0]
        out[0][...] = do
        out[1][...] = _dot_f32(do * ex[0][...], ex[1][...])

    grads4_mix = [split_cores(g) for g in (g_a, g_b, g_o)]
    (do, delta), from_sibling = _mm(
        "attn_out_bwd", [(dya, w_a_f, True, None)], [(o, _row(tm, FOX_W)), (head_sel, _whole((FOX_W, LANES)))],
        [(_sds((s_len, FOX_W), F32), _row(tm, FOX_W)), (_sds((s_len, LANES), F32), _row(tm, LANES))],
        do_epi, m=s_len, tm=tm, n=FOX_W, tn=FOX_W, comm=[_pair_exchange_plan(grads4_mix)])
    pair_sums(("w_a", "w_b", "w_o"), grads4_mix, from_sibling)
    (dsg,) = _mm("sgu_out_bwd", [(dyb, w_b_f, True, None)], [], [(_sds((s_len, SGU_W), F32), _tile(tm, SGU_W))],
                 store(F32), m=s_len, tm=tm, n=SGU_W, tn=SGU_W)

    du, dsv, dw_pairs, db_pos, g_ln_g, g_ln_b = _sgu_bwd(prest, dsg, ln_v_g, ln_v_b, w_stack, wt_stack, b_pair, tm)
    g_w_sgu = jnp.where(sgu_mask[None], dw_pairs.reshape(SGU_G, SGU_LEN, SGU_LEN), 0.0)
    g_b_sgu = jnp.transpose(jnp.sum(db_pos.reshape(SGU_LEN, SGU_G, SGU_W // SGU_G), axis=-1))

    delta_row = jnp.transpose(delta[:, :N_HEADS]).reshape(N_HEADS, 1, s_len)
    early = ("w_a", "w_b", "w_o", "w_up", "w_down")
    (dq, dk, dv, dc_rows_blk, dc_cols), from_chips_early = _attn_bwd(
        qkv, k_t, do, c_col, c_row, lse_row, delta_row, ta, ta, comm=[_chip_exchange_plan([parts16[n] for n in early])])
    dc_rows = jnp.transpose(dc_rows_blk.reshape(s_len // ta, N_HEADS, ta), (0, 2, 1)).reshape(s_len, N_HEADS)
    dc_rows = jnp.pad(dc_rows, ((0, 0), (0, LANES - N_HEADS)))
    dfl, g_bf = _forget_bwd(dc_rows, dc_cols, prest, b_f_pad, ta)

    dp = (dq, dk, dv, du, dsv, dga, dgb)
    g_in_cols = [_mm_tn("grad_w_in_%d" % k, h, seg, tk=d, tn=512, ts=ts) for k, seg in enumerate(dp)]
    g_f = _mm_tn("grad_w_in_f", h, dfl, tk=d, tn=LANES, ts=ts)[:, :N_HEADS]
    g_in_full = jnp.concatenate(g_in_cols[:3] + [g_f] + g_in_cols[3:], axis=1)
    g_in = jnp.transpose(g_in_full.reshape(d, N_DEV, IN_SHARD), (1, 0, 2))

    def dx_epi(accs, ex, out):
        x_ref, r_ref, g_ref, dx1_ref = ex
        dh = accs[0]
        for extra in accs[1:]:
            dh = dh + extra
        r = r_ref[...]
        xh = x_ref[...] * r
        out[0][...] = dx1_ref[...] + _rms_bwd(xh, r, g_ref[...], dh)
        accumulate(out[1], jnp.sum(dh * xh, axis=0, keepdims=True))

    rest_cols = ((du, U_OFF, 512), (dsv, SV_OFF, 512), (dga, GA_OFF, 1024), (dgb, GB_OFF, 1024), (dfl, F_OFF, LANES))
    dx_pairs = [(seg, w_qkv, True, (512 * k, 512 * (k + 1))) for k, seg in enumerate((dq, dk, dv))]
    dx_pairs += [(seg, w_rest, True, (lo, lo + width)) for seg, lo, width in rest_cols]
    grads4_in = [split_cores(g_in)]
    (grad_x, g_norm1), from_sibling = _mm(
        "proj_bwd", dx_pairs,
        [(xs, _row(tr, d)), (r1, _row(tr, 1)), (norm1_g, _whole((1, d))), (dx1, _row(tr, d))],
        [(_sds((s_len, d), F32), _row(tr, d)), (_sds((1, d), F32), _whole((1, d)))],
        dx_epi, m=s_len, tm=tr, n=d, tn=d, arbitrary=True, comm=[_pair_exchange_plan(grads4_in)])
    pair_sums(("w_in",), grads4_in, from_sibling)

    small_g = _pack_small((g_w_sgu, g_b_sgu, g_norm1, g_norm2, g_normf, g_ln_g, g_ln_b, g_bf[:, :N_HEADS]))
    from_chips_in, small_all = _run_comm("grad_last_exchange", [_chip_exchange_plan([parts16["w_in"]]), _gather_plan(small_g)])
    from_chips = dict(zip(early, from_chips_early), w_in=from_chips_in)

    names = ("w_in", "w_a", "w_b", "w_o", "w_up", "w_down")
    moments_m = (m_w_in, m_w_a, m_w_b, m_w_o, m_w_up, m_w_down)
    moments_v = (v_w_in, v_w_a, v_w_b, v_w_o, v_w_up, v_w_down)
    big_out = {}
    for name, w, m, v in zip(names, big, moments_m, moments_v):
        own = owns[name]
        res = _adamw_shard("adamw_" + name, own, from_chips[name], w, m[0], v[0], min(128, own.shape[0]))
        big_out[name] = [t[None] for t in res]

    small_w = _pack_small((w_sgu, b_sgu, norm1_g, norm2_g, normf_g, ln_v_g, ln_v_b, b_f))
    small_m = _pack_small((m_w_sgu, m_b_sgu, m_norm1_g, m_norm2_g, m_normf_g, m_ln_v_g, m_ln_v_b, m_b_f))
    small_v = _pack_small((v_w_sgu, v_b_sgu, v_norm1_g, v_norm2_g, v_normf_g, v_ln_v_g, v_ln_v_b, v_b_f))
    small_res = [_unpack_small(t) for t in _adamw_small(small_all, small_w, small_m, small_v)]
    small_names = [n for n, _ in _SMALL]
    small_out = {n: [res[k] for res in small_res] for k, n in enumerate(small_names)}

    loss = lax.psum(loss_part[0, 0], ("x", "y", "c"))

    order = ("norm1_g", "w_in", "b_f", "ln_v_g", "ln_v_b", "w_sgu", "b_sgu", "w_a", "w_b", "w_o", "norm2_g", "w_up",
             "w_down", "normf_g")
    table = {**big_out, **small_out}
    outs = [loss, grad_x[None]]
    for kind in range(4):
        outs += [table[n][kind] for n in order]
    return tuple(outs)
```

```python
import math

import jax
import jax.numpy as jnp
from jax import lax
from jax.experimental import pallas as pl
from jax.experimental.pallas import tpu as pltpu

F32 = jnp.float32
BF16 = jnp.bfloat16

N_DEV = 8
D_MODEL = 1024
N_HEADS = 8
HEAD_DIM = 64
FOX_W = N_HEADS * HEAD_DIM
SGU_G = 8
SGU_W = 512
SGU_LEN = 128
CHUNK = 64
D_FF = 4 * D_MODEL
IN_COLS = 3 * FOX_W + N_HEADS + 2 * SGU_W + 2 * D_MODEL
IN_SHARD = IN_COLS // N_DEV
LANES = 128
QKV_W = 3 * FOX_W
U_OFF, SV_OFF, GA_OFF, GB_OFF, F_OFF = 0, 512, 1024, 2048, 3072
REST_W = F_OFF + LANES
EPS = 1e-6
NEG = -1e30

ADAM_LR = 0.001
ADAM_B1 = 0.9
ADAM_B2 = 0.999
ADAM_EPS = 1e-08
ADAM_WD = 0.01
ADAM_STEP = 10

VMEM_LIMIT = 56 * 1024 * 1024
MESH = pl.DeviceIdType.MESH


def _params(sem=None):
    return pltpu.CompilerParams(dimension_semantics=sem, vmem_limit_bytes=VMEM_LIMIT)


def _dot(a, b):
    return jnp.dot(a, b, preferred_element_type=F32)


def _dot_nt(a, b):
    return lax.dot_general(a, b, (((1,), (1,)), ((), ())), preferred_element_type=F32)


def _dot_tn(a, b):
    return lax.dot_general(a, b, (((0,), (0,)), ((), ())), preferred_element_type=F32)


def _dot_f32(a, b):
    return jnp.dot(a, b, preferred_element_type=F32, precision=lax.Precision.HIGHEST)


def _sigmoid(x):
    return 1.0 / (1.0 + jnp.exp(-x))


def _log_sigmoid(z):
    return jnp.minimum(z, 0.0) - jnp.log(1.0 + jnp.exp(-jnp.abs(z)))


_GELU_K = math.sqrt(2.0 / math.pi)
_GELU_C = 0.044715


def _gelu(x):
    t = jnp.tanh(_GELU_K * (x + _GELU_C * (x * x * x)))
    return 0.5 * x * (1.0 + t)


def _gelu_grad(x):
    x2 = x * x
    t = jnp.tanh(_GELU_K * (x + _GELU_C * (x2 * x)))
    return 0.5 * (1.0 + t) + 0.5 * x * (1.0 - t * t) * (_GELU_K * (1.0 + 3.0 * _GELU_C * x2))


def _rms_bwd(xh, r, g, dy):
    gy = dy * g
    return r * (gy - xh * jnp.mean(xh * gy, axis=-1, keepdims=True))


def _lane_lt64(shape):
    return lax.broadcasted_iota(jnp.int32, shape, len(shape) - 1) < HEAD_DIM


class _Comm:
    def __init__(self, arrays, out_shapes, sems, start, finish, mid=None):
        self.arrays, self.out_shapes, self.sems = list(arrays), list(out_shapes), list(sems)
        self.start, self.mid, self.finish = start, mid, finish


def _comm_phase(plans, phase, in_refs, out_refs, sem_refs):
    ia = io = ks = 0
    for plan in plans:
        na, no, ns = len(plan.arrays), len(plan.out_shapes), len(plan.sems)
        fn = getattr(plan, phase)
        if fn is not None:
            fn(in_refs[ia:ia + na], out_refs[io:io + no], sem_refs[ks:ks + ns])
        ia, io, ks = ia + na, io + no, ks + ns


def _comm_operands(plans):
    arrays = [a for plan in plans for a in plan.arrays]
    out_shapes = [o for plan in plans for o in plan.out_shapes]
    sems = [s for plan in plans for s in plan.sems]
    return arrays, out_shapes, sems


_ANY = pl.BlockSpec(memory_space=pl.ANY)


def _run_comm(name, plans):
    arrays, out_shapes, sems = _comm_operands(plans)
    n_in, n_out = len(arrays), len(out_shapes)

    def body(*refs):
        parts = refs[:n_in], refs[n_in:n_in + n_out], refs[n_in + n_out:]
        for phase in ("start", "mid", "finish"):
            _comm_phase(plans, phase, *parts)

    return pl.pallas_call(
        body, name=name, out_shape=out_shapes, in_specs=[_ANY] * n_in, out_specs=[_ANY] * n_out, scratch_shapes=sems,
    )(*arrays)


def _gather_plan(shard):
    def setup(ins, outs, sems):
        (x_ref,), (out_ref,), (send_sems, recv_sems, local_sem) = ins, outs, sems
        x, y, c = lax.axis_index("x"), lax.axis_index("y"), lax.axis_index("c")
        me, sibling = (x, y, c), (x, y, 1 - c)
        chips = [(1 - x, y), (x, 1 - y), (1 - x, 1 - y)]

        def rows(px, py, pc):
            return out_ref.at[4 * px + 2 * py + pc]

        def copy(k, block, to, src=None):
            return pltpu.make_async_remote_copy(
                src_ref=rows(*block) if src is None else src,
                dst_ref=rows(*block),
                send_sem=send_sems.at[k],
                recv_sem=recv_sems.at[k],
                device_id=to,
                device_id_type=MESH,
            )

        mine = pltpu.make_async_copy(x_ref, rows(*me), local_sem)
        first = [copy(0, me, sibling, src=x_ref)]
        first += [copy(1 + j, me, (*chip, c), src=x_ref) for j, chip in enumerate(chips)]
        passed = [copy(4 + j, (*chip, c), sibling) for j, chip in enumerate(chips)]
        landed = [copy(1 + j, (*chip, c), me) for j, chip in enumerate(chips)]
        from_sibling = [copy(0, sibling, me)] + [copy(4 + j, (*chip, 1 - c), me) for j, chip in enumerate(chips)]
        return mine, first, passed, landed, from_sibling

    def start(ins, outs, sems):
        mine, first, _, _, _ = setup(ins, outs, sems)
        mine.start()
        for cp in first:
            cp.start()

    def mid(ins, outs, sems):
        _, _, passed, landed, _ = setup(ins, outs, sems)
        for arrived, onward in zip(landed, passed):
            arrived.wait_recv()
            onward.start()

    def finish(ins, outs, sems):
        mine, first, passed, _, from_sibling = setup(ins, outs, sems)
        for cp in from_sibling:
            cp.wait_recv()
        for cp in first + passed:
            cp.wait_send()
        mine.wait()

    return _Comm([shard], [jax.ShapeDtypeStruct((N_DEV,) + shard.shape, shard.dtype)],
                 [pltpu.SemaphoreType.DMA((7,)), pltpu.SemaphoreType.DMA((7,)), pltpu.SemaphoreType.DMA],
                 start, finish, mid)


def _start_all(copies):
    for cp in copies:
        cp.start()


def _wait_all(copies):
    for cp in copies:
        cp.wait_recv()
    for cp in copies:
        cp.wait_send()


def _pair_exchange_plan(grads):
    n = len(grads)

    def copies(ins, outs, sems):
        send_sems, recv_sems = sems
        x, y, c = lax.axis_index("x"), lax.axis_index("y"), lax.axis_index("c")
        return [
            pltpu.make_async_remote_copy(
                src_ref=ins[k].at[:, 1 - c],
                dst_ref=outs[k],
                send_sem=send_sems.at[k],
                recv_sem=recv_sems.at[k],
                device_id=(x, y, 1 - c),
                device_id_type=MESH,
            )
            for k in range(n)
        ]

    return _Comm(grads, [jax.ShapeDtypeStruct((4,) + g.shape[2:], g.dtype) for g in grads],
                 [pltpu.SemaphoreType.DMA((n,)), pltpu.SemaphoreType.DMA((n,))],
                 lambda *refs: _start_all(copies(*refs)), lambda *refs: _wait_all(copies(*refs)))


def _chip_exchange_plan(parts):
    n = len(parts)

    def copies(ins, outs, sems):
        send_sems, recv_sems = sems
        x, y, c = lax.axis_index("x"), lax.axis_index("y"), lax.axis_index("c")
        chips = [(1 - x, y), (x, 1 - y), (1 - x, 1 - y)]
        return [
            pltpu.make_async_remote_copy(
                src_ref=ins[k].at[2 * px + py],
                dst_ref=outs[k].at[j],
                send_sem=send_sems.at[3 * k + j],
                recv_sem=recv_sems.at[3 * k + j],
                device_id=(px, py, c),
                device_id_type=MESH,
            )
            for k in range(n) for j, (px, py) in enumerate(chips)
        ]

    return _Comm(parts, [jax.ShapeDtypeStruct((3,) + p.shape[1:], p.dtype) for p in parts],
                 [pltpu.SemaphoreType.DMA((3 * n,)), pltpu.SemaphoreType.DMA((3 * n,))],
                 lambda *refs: _start_all(copies(*refs)), lambda *refs: _wait_all(copies(*refs)))


def _mm(name, pairs, extras, outs, epi, *, m, tm, n, tn, arbitrary=False, comm=()):
    nj = n // tn
    a_arrays, a_specs, b_arrays, b_specs, b_index = [], [], [], [], []
    for a, b, nt, cols in pairs:
        a_arrays.append(a)
        a_specs.append(pl.BlockSpec((tm, a.shape[1]), lambda i, j: (i, 0)))
        known = [k for k, other in enumerate(b_arrays) if other is b]
        if known:
            b_index.append(known[0])
            continue
        b_index.append(len(b_arrays))
        b_arrays.append(b)
        if cols is not None:
            assert nj == 1
            b_specs.append(pl.BlockSpec(b.shape, lambda i, j: (0, 0)))
        elif nt:
            b_specs.append(pl.BlockSpec((tn, b.shape[1]), lambda i, j: (j, 0)))
        else:
            b_specs.append(pl.BlockSpec((b.shape[0], tn), lambda i, j: (0, j)))
    comm_arrays, comm_outs, comm_sems = _comm_operands(comm)
    arrays = a_arrays + b_arrays + [arr for arr, _ in extras] + comm_arrays
    in_specs = a_specs + b_specs + [spec for _, spec in extras] + [_ANY] * len(comm_arrays)
    n_a, n_b, n_extras, n_ci, n_out, n_co = len(a_arrays), len(b_arrays), len(extras), len(comm_arrays), len(outs), len(comm_outs)
    ni = m // tm

    def body(*refs):
        a_refs = refs[:n_a]
        b_refs = refs[n_a:n_a + n_b]
        ex = refs[n_a + n_b:n_a + n_b + n_extras]
        n_in = n_a + n_b + n_extras + n_ci
        comm_refs = refs[n_in - n_ci:n_in], refs[n_in + n_out:n_in + n_out + n_co], refs[n_in + n_out + n_co:]
        out = refs[n_in:n_in + n_out]
        if comm:
            @pl.when(jnp.logical_and(pl.program_id(0) == 0, pl.program_id(1) == 0))
            def _():
                _comm_phase(comm, "start", *comm_refs)

        accs = []
        for p, (_, _, nt, cols) in enumerate(pairs):
            av = a_refs[p][...]
            if av.dtype != BF16:
                av = av.astype(BF16)
            b_ref = b_refs[b_index[p]]
            bv = b_ref[...] if cols is None else b_ref[:, cols[0]:cols[1]]
            accs.append(_dot_nt(av, bv) if nt else _dot(av, bv))
        epi(accs, ex, out)
        if comm:
            @pl.when(jnp.logical_and(pl.program_id(0) == ni - 1, pl.program_id(1) == nj - 1))
            def _():
                _comm_phase(comm, "mid", *comm_refs)
                _comm_phase(comm, "finish", *comm_refs)

    sem = ("arbitrary", "arbitrary") if arbitrary or comm else ("parallel", "parallel")
    res = pl.pallas_call(
        body,
        name=name,
        grid=(ni, nj),
        in_specs=in_specs,
        out_specs=[spec for _, spec in outs] + [_ANY] * n_co,
        out_shape=[shape for shape, _ in outs] + comm_outs,
        scratch_shapes=comm_sems,
        compiler_params=_params(sem),
    )(*arrays)
    return (res[:n_out], res[n_out:]) if comm else res


def _tile(tm, tn, off=0):
    return pl.BlockSpec((tm, tn), lambda i, j: (i, j + off))


def _row(tm, w, blk=0):
    return pl.BlockSpec((tm, w), lambda i, j: (i, blk))


def _whole(shape):
    zeros = (0,) * len(shape)
    return pl.BlockSpec(shape, lambda i, j: zeros)


def _sds(shape, dtype):
    return jax.ShapeDtypeStruct(shape, dtype)


def _mm_tn(name, a, g, *, tk, tn, ts, blocked=False):
    s_len, ka = a.shape
    n = g.shape[1]

    def body(a_ref, g_ref, o_ref):
        part = _dot_tn(a_ref[...].astype(BF16), g_ref[...].astype(BF16))

        @pl.when(pl.program_id(2) == 0)
        def _():
            o_ref[...] = part

        @pl.when(pl.program_id(2) > 0)
        def _():
            o_ref[...] += part

    if blocked:
        out_shape = _sds((n // tn, ka, tn), F32)
        out_spec = pl.BlockSpec((None, tk, tn), lambda i, j, s: (j, i, 0))
    else:
        out_shape = _sds((ka, n), F32)
        out_spec = pl.BlockSpec((tk, tn), lambda i, j, s: (i, j))
    return pl.pallas_call(
        body,
        name=name,
        grid=(ka // tk, n // tn, s_len // ts),
        in_specs=[pl.BlockSpec((ts, tk), lambda i, j, s: (s, i)), pl.BlockSpec((ts, tn), lambda i, j, s: (s, j))],
        out_specs=out_spec,
        out_shape=out_shape,
        compiler_params=_params(("parallel", "parallel", "arbitrary")),
    )(a, g)


def _rms_fwd(name, x, g, tm):
    s_len, d = x.shape

    def body(x_ref, g_ref, h_ref, r_ref):
        xv = x_ref[...]
        r = lax.rsqrt(jnp.mean(xv * xv, axis=-1, keepdims=True) + EPS)
        h_ref[...] = (xv * r * g_ref[...]).astype(BF16)
        r_ref[...] = r

    return pl.pallas_call(
        body,
        name=name,
        grid=(s_len // tm,),
        in_specs=[pl.BlockSpec((tm, d), lambda i: (i, 0)), pl.BlockSpec((1, d), lambda i: (0, 0))],
        out_specs=[pl.BlockSpec((tm, d), lambda i: (i, 0)), pl.BlockSpec((tm, 1), lambda i: (i, 0))],
        out_shape=[_sds((s_len, d), BF16), _sds((s_len, 1), F32)],
        compiler_params=_params(("parallel",)),
    )(x, g)


def _forget_cumsum(prest, b_f_pad, tc):
    s_len = prest.shape[0]

    def body(f_ref, b_ref, c_ref, carry):
        @pl.when(pl.program_id(0) == 0)
        def _():
            carry[...] = jnp.zeros_like(carry)

        logf = _log_sigmoid(f_ref[...] + b_ref[...])
        row = lax.broadcasted_iota(jnp.int32, (tc, tc), 0)
        col = lax.broadcasted_iota(jnp.int32, (tc, tc), 1)
        tri = (row >= col).astype(F32)
        c = _dot_f32(tri, logf) + carry[...]
        c_ref[...] = c
        carry[...] = c[tc - 1:tc, :]

    return pl.pallas_call(
        body,
        name="forget_cumsum",
        grid=(s_len // tc,),
        in_specs=[pl.BlockSpec((tc, LANES), lambda i: (i, F_OFF // LANES)), pl.BlockSpec((1, LANES), lambda i: (0, 0))],
        out_specs=pl.BlockSpec((tc, LANES), lambda i: (i, 0)),
        out_shape=_sds((s_len, LANES), F32),
        scratch_shapes=[pltpu.VMEM((1, LANES), F32)],
        compiler_params=_params(("arbitrary",)),
    )(prest, b_f_pad)


def _stack_heads(pair, lt64):
    zero = jnp.zeros_like(pair)
    return jnp.concatenate([jnp.where(lt64, pair, zero), jnp.where(lt64, zero, pair)], axis=0)


def _score_tiles(q_ref, k_ref, cq_ref, ck_ref, st_sc, tk):
    lt64 = _lane_lt64((tk, LANES))
    for p in range(N_HEADS // 2):
        lanes = slice(p * LANES, (p + 1) * LANES)
        q_pair = q_ref[:, lanes] * jnp.asarray(HEAD_DIM ** -0.5, BF16)
        st2 = _dot_nt(_stack_heads(k_ref[:, lanes], lt64), q_pair)
        for half in range(2):
            h = 2 * p + half
            st_sc[h] = st2[half * tk:(half + 1) * tk] + (cq_ref[h] - ck_ref[:, h:h + 1])


def _mask_diagonal(st_sc, i, j, tq, tk):
    @pl.when((j + 1) * tk - 1 > i * tq)
    def _():
        key = j * tk + lax.broadcasted_iota(jnp.int32, (tk, tq), 0)
        query = i * tq + lax.broadcasted_iota(jnp.int32, (tk, tq), 1)
        st_sc[...] = jnp.where((query >= key)[None], st_sc[...], NEG)


def _attn_fwd(qkv, v_t, c_col, c_row, tq, tk, comm=()):
    s_len = qkv.shape[0]
    ratio = tq // tk
    steps = [(i, j) for i in range(s_len // tq) for j in range((i + 1) * ratio)]
    i_tab = jnp.asarray([i for i, _ in steps], jnp.int32)
    j_tab = jnp.asarray([j for _, j in steps], jnp.int32)

    comm_arrays, comm_outs, comm_sems = _comm_operands(comm)
    n_ci, n_co = len(comm_arrays), len(comm_outs)

    def body(i_ref, j_ref, q_ref, k_ref, vt_ref, cq_ref, ck_ref, *rest):
        comm_refs = rest[:n_ci], rest[n_ci + 2:n_ci + 2 + n_co], rest[n_ci + 2 + n_co + 5:]
        o_ref, lse_ref = rest[n_ci:n_ci + 2]
        acc_t, m_sc, l_sc, st_sc, p_sc = rest[n_ci + 2 + n_co:n_ci + 2 + n_co + 5]
        n = pl.program_id(0)
        i, j = i_ref[n], j_ref[n]
        for phase, at in (("start", 0), ("mid", (2 * len(steps)) // 3)):
            if comm:
                @pl.when(n == at)
                def _():
                    _comm_phase(comm, phase, *comm_refs)

        @pl.when(j == 0)
        def _():
            acc_t[...] = jnp.zeros_like(acc_t)
            m_sc[...] = jnp.full_like(m_sc, NEG)
            l_sc[...] = jnp.zeros_like(l_sc)

        _score_tiles(q_ref, k_ref, cq_ref, ck_ref, st_sc, tk)
        _mask_diagonal(st_sc, i, j, tq, tk)
        st = st_sc[...]
        m_old = m_sc[...]
        m_new = jnp.maximum(m_old, jnp.max(st, axis=1, keepdims=True))
        alpha = jnp.exp(m_old - m_new)
        pt = jnp.exp(st - m_new)
        l_sc[...] = alpha * l_sc[...] + jnp.sum(pt, axis=1, keepdims=True)
        m_sc[...] = m_new
        p_sc[...] = pt.astype(BF16)
        top = lax.broadcasted_iota(jnp.int32, (LANES, tq), 0) < HEAD_DIM
        for p in range(N_HEADS // 2):
            lanes = slice(p * LANES, (p + 1) * LANES)
            vt_pair = vt_ref[lanes, :]
            pv = jnp.where(top, _dot(vt_pair, p_sc[2 * p]), _dot(vt_pair, p_sc[2 * p + 1]))
            acc_t[lanes, :] = acc_t[lanes, :] * jnp.where(top, alpha[2 * p], alpha[2 * p + 1]) + pv

        @pl.when(j == (i + 1) * ratio - 1)
        def _():
            for p in range(N_HEADS // 2):
                lanes = slice(p * LANES, (p + 1) * LANES)
                l_pair = jnp.where(top, l_sc[2 * p], l_sc[2 * p + 1])
                o_ref[:, lanes] = jnp.transpose(acc_t[lanes, :] / l_pair)
            lse_ref[...] = m_sc[...] + jnp.log(l_sc[...])

        if comm:
            @pl.when(n == len(steps) - 1)
            def _():
                _comm_phase(comm, "finish", *comm_refs)

    stat = pltpu.VMEM((N_HEADS, 1, tq), F32)
    res = pl.pallas_call(
        body,
        name="attn_fwd",
        grid_spec=pltpu.PrefetchScalarGridSpec(
            num_scalar_prefetch=2,
            grid=(len(steps),),
            in_specs=[
                pl.BlockSpec((tq, FOX_W), lambda n, it, jt: (it[n], 0)),
                pl.BlockSpec((tk, FOX_W), lambda n, it, jt: (jt[n], 1)),
                pl.BlockSpec((FOX_W, tk), lambda n, it, jt: (0, jt[n])),
                pl.BlockSpec((N_HEADS, 1, tq), lambda n, it, jt: (0, 0, it[n])),
                pl.BlockSpec((tk, LANES), lambda n, it, jt: (jt[n], 0)),
            ] + [_ANY] * n_ci,
            out_specs=[
                pl.BlockSpec((tq, FOX_W), lambda n, it, jt: (it[n], 0)),
                pl.BlockSpec((N_HEADS, 1, tq), lambda n, it, jt: (0, 0, it[n])),
            ] + [_ANY] * n_co,
            scratch_shapes=[pltpu.VMEM((FOX_W, tq), F32), stat, stat, pltpu.VMEM((N_HEADS, tk, tq), F32),
                            pltpu.VMEM((N_HEADS, tk, tq), BF16)] + comm_sems,
        ),
        out_shape=[_sds((s_len, FOX_W), F32), _sds((N_HEADS, 1, s_len), F32)] + comm_outs,
        compiler_params=_params(("arbitrary",)),
    )(i_tab, j_tab, qkv, qkv, v_t, c_row, c_col, *comm_arrays)
    return res[0], res[1], res[2:]


def _sgu_mix(vn, w_stack, lt64):
    outs = []
    for p in range(SGU_G // 2):
        r = _dot(w_stack[p], vn[:, p * LANES:(p + 1) * LANES])
        outs.append(jnp.where(lt64, r[:SGU_LEN], r[SGU_LEN:]))
    return jnp.concatenate(outs, axis=1)


def _sgu_norm(sv, ln_g, ln_b):
    svg = _gelu(sv)
    xc = svg - jnp.mean(svg, axis=-1, keepdims=True)
    rstd = lax.rsqrt(jnp.mean(xc * xc, axis=-1, keepdims=True) + EPS)
    xhat = xc * rstd
    return xhat, rstd, xhat * ln_g + ln_b


def _sgu_fwd(prest, ln_g, ln_b, w_stack, b_pair, tm):
    s_len = prest.shape[0]

    def body(u_ref, sv_ref, g_ref, b_ref, w_ref, bp_ref, sg_ref):
        lt64 = _lane_lt64((SGU_LEN, LANES))
        _, _, vn = _sgu_norm(sv_ref[...], g_ref[...], b_ref[...])
        vn = vn.astype(BF16)
        w_stack_v = [w_ref[p] for p in range(SGU_G // 2)]
        for w in range(tm // SGU_LEN):
            win = slice(w * SGU_LEN, (w + 1) * SGU_LEN)
            mixed = _sgu_mix(vn[win], w_stack_v, lt64) + bp_ref[...]
            sg_ref[win, :] = (_gelu(u_ref[win, :]) * mixed).astype(BF16)

    return pl.pallas_call(
        body,
        name="sgu_fwd",
        grid=(s_len // tm,),
        in_specs=[
            pl.BlockSpec((tm, SGU_W), lambda i: (i, U_OFF // SGU_W)),
            pl.BlockSpec((tm, SGU_W), lambda i: (i, SV_OFF // SGU_W)),
            pl.BlockSpec((1, SGU_W), lambda i: (0, 0)),
            pl.BlockSpec((1, SGU_W), lambda i: (0, 0)),
            pl.BlockSpec((SGU_G // 2, 2 * SGU_LEN, SGU_LEN), lambda i: (0, 0, 0)),
            pl.BlockSpec((SGU_LEN, SGU_W), lambda i: (0, 0)),
        ],
        out_specs=pl.BlockSpec((tm, SGU_W), lambda i: (i, 0)),
        out_shape=_sds((s_len, SGU_W), BF16),
        compiler_params=_params(("parallel",)),
    )(prest, prest, ln_g, ln_b, w_stack, b_pair)


def _sgu_bwd(prest, dsg, ln_g, ln_b, w_stack, wt_stack, b_pair, tm):
    s_len = prest.shape[0]
    n_pair = SGU_G // 2

    def body(u_ref, sv_ref, dsg_ref, g_ref, b_ref, w_ref, wt_ref, bp_ref,
             du_ref, dsv_ref, dw_ref, db_ref, dg_ref, dbeta_ref, dvn_sc):
        @pl.when(pl.program_id(0) == 0)
        def _():
            dw_ref[...] = jnp.zeros_like(dw_ref)
            db_ref[...] = jnp.zeros_like(db_ref)
            dg_ref[...] = jnp.zeros_like(dg_ref)
            dbeta_ref[...] = jnp.zeros_like(dbeta_ref)

        lt64 = _lane_lt64((SGU_LEN, LANES))
        sv = sv_ref[...]
        xhat, rstd, vn32 = _sgu_norm(sv, g_ref[...], b_ref[...])
        vn = vn32.astype(BF16)
        w_stack_v = [w_ref[p] for p in range(n_pair)]
        db = jnp.zeros((SGU_LEN, SGU_W), F32)
        for w in range(tm // SGU_LEN):
            win = slice(w * SGU_LEN, (w + 1) * SGU_LEN)
            u = u_ref[win, :]
            dsg_w = dsg_ref[win, :]
            mixed = _sgu_mix(vn[win], w_stack_v, lt64) + bp_ref[...]
            du_ref[win, :] = (dsg_w * mixed * _gelu_grad(u)).astype(BF16)
            dmixed = dsg_w * _gelu(u)
            db = db + dmixed
            dm16 = dmixed.astype(BF16)
            for p in range(n_pair):
                lanes = slice(p * LANES, (p + 1) * LANES)
                dmp = dm16[:, lanes]
                r = _dot(wt_ref[p], dmp)
                dvn_sc[win, lanes] = jnp.where(lt64, r[:SGU_LEN], r[SGU_LEN:])
                zero = jnp.zeros_like(dmp)
                dm_ab = jnp.concatenate([jnp.where(lt64, dmp, zero), jnp.where(lt64, zero, dmp)], axis=0)
                dw_ref[p] += _dot_nt(dm_ab, vn[win, lanes])
        db_ref[...] += db
        dvn = dvn_sc[...]
        dg_ref[...] += jnp.sum(dvn * xhat, axis=0, keepdims=True)
        dbeta_ref[...] += jnp.sum(dvn, axis=0, keepdims=True)
        dxh = dvn * g_ref[...]
        dsvg = rstd * (dxh - jnp.mean(dxh, axis=-1, keepdims=True) - xhat * jnp.mean(dxh * xhat, axis=-1, keepdims=True))
        dsv_ref[...] = (dsvg * _gelu_grad(sv)).astype(BF16)

    const2 = lambda i: (0, 0)
    const3 = lambda i: (0, 0, 0)
    return pl.pallas_call(
        body,
        name="sgu_bwd",
        grid=(s_len // tm,),
        in_specs=[
            pl.BlockSpec((tm, SGU_W), lambda i: (i, U_OFF // SGU_W)),
            pl.BlockSpec((tm, SGU_W), lambda i: (i, SV_OFF // SGU_W)),
            pl.BlockSpec((tm, SGU_W), lambda i: (i, 0)),
            pl.BlockSpec((1, SGU_W), const2),
            pl.BlockSpec((1, SGU_W), const2),
            pl.BlockSpec((n_pair, 2 * SGU_LEN, SGU_LEN), const3),
            pl.BlockSpec((n_pair, 2 * SGU_LEN, SGU_LEN), const3),
            pl.BlockSpec((SGU_LEN, SGU_W), const2),
        ],
        out_specs=[
            pl.BlockSpec((tm, SGU_W), lambda i: (i, 0)),
            pl.BlockSpec((tm, SGU_W), lambda i: (i, 0)),
            pl.BlockSpec((n_pair, 2 * SGU_LEN, SGU_LEN), const3),
            pl.BlockSpec((SGU_LEN, SGU_W), const2),
            pl.BlockSpec((1, SGU_W), const2),
            pl.BlockSpec((1, SGU_W), const2),
        ],
        out_shape=[
            _sds((s_len, SGU_W), BF16), _sds((s_len, SGU_W), BF16), _sds((n_pair, 2 * SGU_LEN, SGU_LEN), F32),
            _sds((SGU_LEN, SGU_W), F32), _sds((1, SGU_W), F32), _sds((1, SGU_W), F32),
        ],
        scratch_shapes=[pltpu.VMEM((tm, SGU_W), F32)],
        compiler_params=_params(("arbitrary",)),
    )(prest, prest, dsg, ln_g, ln_b, w_stack, wt_stack, b_pair)


def _attn_bwd(qkv, k_t, do, c_col, c_row, lse_row, delta_row, tq, tk, comm=()):
    s_len = qkv.shape[0]
    nq, nk = s_len // tq, s_len // tk
    ratio = tq // tk
    scale = HEAD_DIM ** -0.5
    steps = [(j, i) for j in range(nk) for i in range(j // ratio, nq)]
    j_tab = jnp.asarray([j for j, _ in steps], jnp.int32)
    i_tab = jnp.asarray([i for _, i in steps], jnp.int32)

    comm_arrays, comm_outs, comm_sems = _comm_operands(comm)
    n_ci, n_co = len(comm_arrays), len(comm_outs)

    def body(j_ref, i_ref, q_ref, k_ref, v_ref, kt_ref, do_ref, cq_ref, ck_ref, lse_ref, dl_ref, *rest):
        comm_refs = rest[:n_ci], rest[n_ci + 5:n_ci + 5 + n_co], rest[n_ci + 5 + n_co + 8:]
        dq_ref, dk_ref, dv_ref, dcr_ref, dcc_ref = rest[n_ci:n_ci + 5]
        dq_t, dk_acc, dv_acc, dcc_acc, st_sc, dpt_sc, p_sc, ds_sc = rest[n_ci + 5 + n_co:n_ci + 5 + n_co + 8]
        n = pl.program_id(0)
        j, i = j_ref[n], i_ref[n]

        @pl.when(n == 0)
        def _():
            _comm_phase(comm, "start", *comm_refs)
            dq_t[...] = jnp.zeros_like(dq_t)
            dcr_ref[...] = jnp.zeros_like(dcr_ref)

        @pl.when(i == j // ratio)
        def _():
            dk_acc[...] = jnp.zeros_like(dk_acc)
            dv_acc[...] = jnp.zeros_like(dv_acc)
            dcc_acc[...] = jnp.zeros_like(dcc_acc)

        lt64 = _lane_lt64((tk, LANES))
        _score_tiles(q_ref, k_ref, cq_ref, ck_ref, st_sc, tk)
        for p in range(N_HEADS // 2):
            lanes = slice(p * LANES, (p + 1) * LANES)
            dpt2 = _dot_nt(_stack_heads(v_ref[:, lanes], lt64), do_ref[:, lanes].astype(BF16))
            dpt_sc[2 * p] = dpt2[:tk]
            dpt_sc[2 * p + 1] = dpt2[tk:]
        _mask_diagonal(st_sc, i, j, tq, tk)

        pt = jnp.exp(st_sc[...] - lse_ref[...])
        dst = pt * (dpt_sc[...] - dl_ref[...])
        p_sc[...] = pt.astype(BF16)
        ds_sc[...] = dst.astype(BF16)
        dcr_ref[i] += jnp.sum(dst, axis=1, keepdims=True)
        col_sums = jnp.sum(dst, axis=2, keepdims=True)
        lane = lax.broadcasted_iota(jnp.int32, (tk, LANES), 1)
        dcc = jnp.zeros((tk, LANES), F32)
        for h in range(N_HEADS):
            dcc = jnp.where(lane == h, -col_sums[h], dcc)
        dcc_acc[...] += dcc

        for p in range(N_HEADS // 2):
            lanes = slice(p * LANES, (p + 1) * LANES)
            q_pair = q_ref[:, lanes] * jnp.asarray(scale, BF16)
            dv2 = _dot(p_sc[2 * p:2 * p + 2].reshape(2 * tk, tq), do_ref[:, lanes].astype(BF16))
            dv_acc[:, lanes] += jnp.where(lt64, dv2[:tk], dv2[tk:])
            dk2 = _dot(ds_sc[2 * p:2 * p + 2].reshape(2 * tk, tq), q_pair)
            dk_acc[:, lanes] += jnp.where(lt64, dk2[:tk], dk2[tk:])
            dq2 = _dot(kt_ref[lanes, :], jnp.concatenate([ds_sc[2 * p], ds_sc[2 * p + 1]], axis=1))
            top = lax.broadcasted_iota(jnp.int32, (LANES, tq), 0) < HEAD_DIM
            dq_t[i, lanes, :] += jnp.where(top, dq2[:, :tq], dq2[:, tq:])

        @pl.when(j == (i + 1) * ratio - 1)
        def _():
            rows = pl.ds(pl.multiple_of(i * tq, tq), tq)
            for p in range(N_HEADS // 2):
                lanes = slice(p * LANES, (p + 1) * LANES)
                dq_ref[rows, lanes] = (jnp.transpose(dq_t[i, lanes, :]) * scale).astype(BF16)

        @pl.when(i == nq - 1)
        def _():
            dk_ref[...] = dk_acc[...].astype(BF16)
            dv_ref[...] = dv_acc[...].astype(BF16)
            dcc_ref[...] = dcc_acc[...]

        if comm:
            @pl.when(n == len(steps) - 1)
            def _():
                _comm_phase(comm, "mid", *comm_refs)
                _comm_phase(comm, "finish", *comm_refs)

    q_map = lambda n, jt, it: (it[n], 0)
    q_stat = lambda n, jt, it: (0, 0, it[n])
    k_map = lambda n, jt, it: (jt[n], 0)
    tile = (N_HEADS, tk, tq)
    res = pl.pallas_call(
        body,
        name="attn_bwd",
        grid_spec=pltpu.PrefetchScalarGridSpec(
            num_scalar_prefetch=2,
            grid=(len(steps),),
            in_specs=[
                pl.BlockSpec((tq, FOX_W), q_map),
                pl.BlockSpec((tk, FOX_W), lambda n, jt, it: (jt[n], 1)),
                pl.BlockSpec((tk, FOX_W), lambda n, jt, it: (jt[n], 2)),
                pl.BlockSpec((FOX_W, tk), lambda n, jt, it: (0, jt[n])),
                pl.BlockSpec((tq, FOX_W), q_map),
                pl.BlockSpec((N_HEADS, 1, tq), q_stat),
                pl.BlockSpec((tk, LANES), k_map),
                pl.BlockSpec((N_HEADS, 1, tq), q_stat),
                pl.BlockSpec((N_HEADS, 1, tq), q_stat),
            ] + [_ANY] * n_ci,
            out_specs=[
                pl.BlockSpec((s_len, FOX_W), lambda n, jt, it: (0, 0)),
                pl.BlockSpec((tk, FOX_W), k_map),
                pl.BlockSpec((tk, FOX_W), k_map),
                pl.BlockSpec((nq, N_HEADS, 1, tq), lambda n, jt, it: (0, 0, 0, 0)),
                pl.BlockSpec((tk, LANES), k_map),
            ] + [_ANY] * n_co,
            scratch_shapes=[pltpu.VMEM((nq, FOX_W, tq), F32), pltpu.VMEM((tk, FOX_W), F32), pltpu.VMEM((tk, FOX_W), F32),
                            pltpu.VMEM((tk, LANES), F32), pltpu.VMEM(tile, F32), pltpu.VMEM(tile, F32),
                            pltpu.VMEM(tile, BF16), pltpu.VMEM(tile, BF16)] + comm_sems,
        ),
        out_shape=[_sds((s_len, FOX_W), BF16), _sds((s_len, FOX_W), BF16), _sds((s_len, FOX_W), BF16),
                   _sds((nq, N_HEADS, 1, tq), F32), _sds((s_len, LANES), F32)] + comm_outs,
        compiler_params=_params(("arbitrary",)),
    )(j_tab, i_tab, qkv, qkv, qkv, k_t, do, c_row, c_col, lse_row, delta_row, *comm_arrays)
    return res[:5], res[5:]


def _forget_bwd(dc_rows, dc_cols, prest, b_f_pad, tc):
    s_len = dc_rows.shape[0]
    nb = s_len // tc

    def body(dcr_ref, dc_ref, f_ref, b_ref, df_ref, db_ref, carry):
        @pl.when(pl.program_id(0) == 0)
        def _():
            carry[...] = jnp.zeros_like(carry)
            db_ref[...] = jnp.zeros_like(db_ref)

        row = lax.broadcasted_iota(jnp.int32, (tc, tc), 0)
        col = lax.broadcasted_iota(jnp.int32, (tc, tc), 1)
        tri = (row <= col).astype(F32)
        dlogf = _dot_f32(tri, dcr_ref[...] + dc_ref[...]) + carry[...]
        carry[...] = dlogf[0:1, :]
        z = f_ref[...] + b_ref[...]
        lane = lax.broadcasted_iota(jnp.int32, (tc, LANES), 1)
        dz = jnp.where(lane < N_HEADS, dlogf * _sigmoid(-z), 0.0)
        df_ref[...] = dz.astype(BF16)
        db_ref[...] += jnp.sum(dz, axis=0, keepdims=True)

    rev = lambda i: (nb - 1 - i, 0)
    return pl.pallas_call(
        body,
        name="forget_bwd",
        grid=(nb,),
        in_specs=[
            pl.BlockSpec((tc, LANES), rev),
            pl.BlockSpec((tc, LANES), rev),
            pl.BlockSpec((tc, LANES), lambda i: (nb - 1 - i, F_OFF // LANES)),
            pl.BlockSpec((1, LANES), lambda i: (0, 0)),
        ],
        out_specs=[pl.BlockSpec((tc, LANES), rev), pl.BlockSpec((1, LANES), lambda i: (0, 0))],
        out_shape=[_sds((s_len, LANES), BF16), _sds((1, LANES), F32)],
        scratch_shapes=[pltpu.VMEM((1, LANES), F32)],
        compiler_params=_params(("arbitrary",)),
    )(dc_rows, dc_cols, prest, b_f_pad)


def _pair_sum(name, g4, recv, idx, tr):
    _, _, r, c = g4.shape

    def body(idx_ref, g_ref, r_ref, p16_ref, own_ref):
        k = pl.program_id(1)
        s = g_ref[...] + r_ref[...]
        p16_ref[...] = s.astype(BF16)

        @pl.when(k == idx_ref[1])
        def _():
            own_ref[...] = s

    return pl.pallas_call(
        body,
        name=name,
        grid_spec=pltpu.PrefetchScalarGridSpec(
            num_scalar_prefetch=1,
            grid=(r // tr, 4),
            in_specs=[
                pl.BlockSpec((None, None, tr, c), lambda i, k, idx: (k, idx[0], i, 0)),
                pl.BlockSpec((None, tr, c), lambda i, k, idx: (k, i, 0)),
            ],
            out_specs=[
                pl.BlockSpec((None, tr, c), lambda i, k, idx: (k, i, 0)),
                pl.BlockSpec((tr, c), lambda i, k, idx: (i, 0)),
            ],
        ),
        out_shape=[_sds((4, r, c), BF16), _sds((r, c), F32)],
        compiler_params=_params(("parallel", "arbitrary")),
    )(idx, g4, recv)


def _adamw_math(w, g, m, v):
    m2 = ADAM_B1 * m + (1.0 - ADAM_B1) * g
    v2 = ADAM_B2 * v + (1.0 - ADAM_B2) * (g * g)
    m_hat = m2 / (1.0 - ADAM_B1 ** ADAM_STEP)
    v_hat = v2 / (1.0 - ADAM_B2 ** ADAM_STEP)
    delta = -ADAM_LR * (m_hat / (jnp.sqrt(v_hat) + ADAM_EPS) + ADAM_WD * w)
    return delta, m2, v2


def _adamw_shard(name, own, recv, w, m, v, tr):
    r, c = own.shape

    def body(own_ref, recv_ref, w_ref, m_ref, v_ref, g_ref, d_ref, m2_ref, v2_ref):
        g = own_ref[...]
        for k in range(3):
            g = g + recv_ref[k].astype(F32)
        delta, m2, v2 = _adamw_math(w_ref[...], g, m_ref[...], v_ref[...])
        g_ref[...] = g
        d_ref[...] = delta
        m2_ref[...] = m2
        v2_ref[...] = v2

    spec = pl.BlockSpec((tr, c), lambda i: (i, 0))
    return pl.pallas_call(
        body,
        name=name,
        grid=(r // tr,),
        in_specs=[spec, pl.BlockSpec((3, tr, c), lambda i: (0, i, 0)), spec, spec, spec],
        out_specs=[spec] * 4,
        out_shape=[_sds((r, c), F32)] * 4,
        compiler_params=_params(("parallel",)),
    )(own, recv, w, m, v)


def _adamw_small(gathered, w, m, v):
    _, r, _ = gathered.shape

    def body(ga_ref, w_ref, m_ref, v_ref, g_ref, d_ref, m2_ref, v2_ref):
        g = ga_ref[0]
        for k in range(1, N_DEV):
            g = g + ga_ref[k]
        delta, m2, v2 = _adamw_math(w_ref[...], g, m_ref[...], v_ref[...])
        g_ref[...] = g
        d_ref[...] = delta
        m2_ref[...] = m2
        v2_ref[...] = v2

    spec = pl.BlockSpec((r, LANES), lambda i: (0, 0))
    return pl.pallas_call(
        body,
        name="adamw_small",
        grid=(1,),
        in_specs=[pl.BlockSpec((N_DEV, r, LANES), lambda i: (0, 0, 0)), spec, spec, spec],
        out_specs=[spec] * 4,
        out_shape=[_sds((r, LANES), F32)] * 4,
        compiler_params=_params(("arbitrary",)),
    )(gathered, w, m, v)


_SMALL = (("w_sgu", (1, SGU_G, SGU_LEN, SGU_LEN)), ("b_sgu", (1, SGU_G, SGU_LEN)), ("norm1_g", (1, D_MODEL)),
          ("norm2_g", (1, D_MODEL)), ("normf_g", (D_MODEL,)), ("ln_v_g", (1, SGU_W)), ("ln_v_b", (1, SGU_W)),
          ("b_f", (1, N_HEADS)))


def _pack_small(values):
    rows = []
    for val in values:
        flat = val.reshape(-1).astype(F32)
        pad = (-flat.shape[0]) % LANES
        rows.append(jnp.pad(flat, (0, pad)).reshape(-1, LANES))
    packed = jnp.concatenate(rows, axis=0)
    return jnp.pad(packed, ((0, (-packed.shape[0]) % 8), (0, 0)))


def _unpack_small(packed):
    out, row = [], 0
    for _, shape in _SMALL:
        size = math.prod(shape)
        n_rows = -(-size // LANES)
        out.append(packed[row:row + n_rows].reshape(-1)[:size].reshape(shape))
        row += n_rows
    return out


def kernel(x, norm1_g, w_in, b_f, ln_v_g, ln_v_b, w_sgu, b_sgu, w_a, w_b, w_o, norm2_g, w_up, w_down, normf_g, loss_target, m_norm1_g, m_w_in, m_b_f, m_ln_v_g, m_ln_v_b, m_w_sgu, m_b_sgu, m_w_a, m_w_b, m_w_o, m_norm2_g, m_w_up, m_w_down, m_normf_g, v_norm1_g, v_w_in, v_b_f, v_ln_v_g, v_ln_v_b, v_w_sgu, v_b_sgu, v_w_a, v_w_b, v_w_o, v_norm2_g, v_w_up, v_w_down, v_normf_g):
    xs = x[0]
    target = loss_target[0]
    s_len, d = xs.shape
    tm = min(512, s_len)
    tl = min(1024, s_len)
    tr = min(256, s_len)
    ta = min(512, s_len)
    tc = min(256, s_len)

    big = (w_in[0], w_a[0], w_b[0], w_o[0], w_up[0], w_down[0])
    (w_in_g,) = _run_comm("gather_w_in", [_gather_plan(w_in[0].astype(BF16))])
    w_in_f = jnp.transpose(w_in_g, (1, 0, 2)).reshape(d, IN_COLS)
    later = big[1:]
    later_flat = jnp.concatenate([w.reshape(-1).astype(BF16) for w in later]).reshape(-1, D_MODEL)
    later_plan = _gather_plan(later_flat)

    def unflatten(gathered):
        row, full = 0, []
        for w, col_sharded in zip(later, (True, True, False, True, False)):
            n_rows = w.size // D_MODEL
            blk = gathered[:, row:row + n_rows].reshape((N_DEV,) + w.shape)
            row += n_rows
            if col_sharded:
                full.append(jnp.transpose(blk, (1, 0, 2)).reshape(w.shape[0], N_DEV * w.shape[1]))
            else:
                full.append(blk.reshape(N_DEV * w.shape[0], w.shape[1]))
        return full

    w_qkv = w_in_f[:, :QKV_W]
    f_lo = QKV_W
    u_lo = f_lo + N_HEADS
    w_rest = jnp.concatenate([w_in_f[:, u_lo:], jnp.pad(w_in_f[:, f_lo:u_lo], ((0, 0), (0, LANES - N_HEADS)))], axis=1)

    chunk_id = jnp.arange(SGU_LEN) // CHUNK
    sgu_mask = chunk_id[None, :] <= chunk_id[:, None]
    w_masked = jnp.where(sgu_mask[None], w_sgu[0], 0.0)
    w_stack = w_masked.reshape(SGU_G // 2, 2 * SGU_LEN, SGU_LEN).astype(BF16)
    wt_stack = jnp.transpose(w_masked, (0, 2, 1)).reshape(SGU_G // 2, 2 * SGU_LEN, SGU_LEN).astype(BF16)
    b_pair = jnp.transpose(jnp.repeat(b_sgu[0], SGU_W // SGU_G, axis=0))
    b_f_pad = jnp.pad(b_f, ((0, 0), (0, LANES - N_HEADS)))
    head_sel = (jnp.arange(FOX_W)[:, None] // HEAD_DIM == jnp.arange(LANES)[None, :]).astype(F32)

    h, r1 = _rms_fwd("rms1", xs, norm1_g, tm)

    def store(dtype):
        def epi(accs, ex, out):
            out[0][...] = accs[0].astype(dtype)
        return epi

    (qkv,) = _mm("proj_qkv", [(h, w_qkv, False, None)], [], [(_sds((s_len, QKV_W), BF16), _tile(tl, 512))],
                 store(BF16), m=s_len, tm=tl, n=QKV_W, tn=512)
    (prest,) = _mm("proj_rest", [(h, w_rest, False, None)], [], [(_sds((s_len, REST_W), F32), _tile(tl, 640))],
                   store(F32), m=s_len, tm=tl, n=REST_W, tn=640)

    c_col = _forget_cumsum(prest, b_f_pad, tc)
    c_row = jnp.transpose(c_col[:, :N_HEADS]).reshape(N_HEADS, 1, s_len)
    k_t = jnp.transpose(qkv[:, FOX_W:2 * FOX_W])
    v_t = jnp.transpose(qkv[:, 2 * FOX_W:])
    o, lse_row, (later_g,) = _attn_fwd(qkv, v_t, c_col, c_row, ta, ta, comm=[later_plan])
    w_a_f, w_b_f, w_o_f, w_up_f, w_down_f = unflatten(later_g)
    sg = _sgu_fwd(prest, ln_v_g, ln_v_b, w_stack, b_pair, tm)

    def merge_epi(accs, ex, out):
        ya, yb = accs
        sa, sb = _sigmoid(ex[0][...]), _sigmoid(ex[1][...])
        out[0][...] = (sa * ya + sb * yb).astype(BF16)
        out[1][...] = ya.astype(BF16)
        out[2][...] = yb.astype(BF16)

    merged, ya, yb = _mm(
        "merge", [(o, w_a_f, False, None), (sg, w_b_f, False, None)],
        [(prest, _tile(tm, d, GA_OFF // d)), (prest, _tile(tm, d, GB_OFF // d))],
        [(_sds((s_len, d), BF16), _tile(tm, d))] * 3, merge_epi, m=s_len, tm=tm, n=d, tn=d)

    def resid_epi(accs, ex, out):
        out[0][...] = ex[0][...] + accs[0]

    (x1,) = _mm("out_proj", [(merged, w_o_f, False, None)], [(xs, _tile(tm, d))],
                [(_sds((s_len, d), F32), _tile(tm, d))], resid_epi, m=s_len, tm=tm, n=d, tn=d)

    h2, r2 = _rms_fwd("rms2", x1, norm2_g, tm)

    def up_epi(accs, ex, out):
        a = accs[0]
        out[0][...] = a.astype(BF16)
        out[1][...] = jnp.square(jnp.maximum(a, 0.0)).astype(BF16)

    a_up, act = _mm("mlp_up", [(h2, w_up_f, False, None)], [],
                    [(_sds((s_len, D_FF), BF16), _tile(tl, 512)), (_sds((s_len, D_FF), BF16), _tile(tl, 512))],
                    up_epi, m=s_len, tm=tl, n=D_FF, tn=512)

    def first_step():
        return jnp.logical_and(pl.program_id(0) == 0, pl.program_id(1) == 0)

    def accumulate(ref, val):
        @pl.when(first_step())
        def _():
            ref[...] = val

        @pl.when(jnp.logical_not(first_step()))
        def _():
            ref[...] += val

    def final_epi(accs, ex, out):
        x1_ref, t_ref, g_ref = ex
        x2 = x1_ref[...] + accs[0]
        rf = lax.rsqrt(jnp.mean(x2 * x2, axis=-1, keepdims=True) + EPS)
        xh = x2 * rf
        gf = g_ref[...]
        err = xh * gf - t_ref[...]
        dy = err * (1.0 / d)
        out[0][...] = _rms_bwd(xh, rf, gf, dy)
        accumulate(out[1], jnp.sum(dy * xh, axis=0, keepdims=True))
        part = 0.5 * jnp.sum(jnp.sum(err * err, axis=-1, keepdims=True) * (1.0 / d), axis=0, keepdims=True)
        accumulate(out[2], jnp.broadcast_to(part, (1, LANES)))

    gf2 = normf_g.reshape(1, d)
    dx2, g_normf, loss_part = _mm(
        "mlp_down_loss", [(act, w_down_f, False, None)],
        [(x1, _row(tr, d)), (target, _row(tr, d)), (gf2, _whole((1, d)))],
        [(_sds((s_len, d), F32), _row(tr, d)), (_sds((1, d), F32), _whole((1, d))), (_sds((1, LANES), F32), _whole((1, LANES)))],
        final_epi, m=s_len, tm=tr, n=d, tn=d, arbitrary=True)

    def dact_epi(accs, ex, out):
        out[0][...] = (accs[0] * (2.0 * jnp.maximum(ex[0][...], 0.0))).astype(BF16)

    (da,) = _mm("mlp_down_bwd", [(dx2, w_down_f, True, None)], [(a_up, _tile(tl, 512))],
                [(_sds((s_len, D_FF), BF16), _tile(tl, 512))], dact_epi, m=s_len, tm=tl, n=D_FF, tn=512)
    ts = min(512, s_len)
    g_down = _mm_tn("grad_w_down", act, dx2, tk=1024, tn=d, ts=ts)
    g_up = _mm_tn("grad_w_up", h2, da, tk=d, tn=D_FF // N_DEV, ts=ts, blocked=True)

    def dh2_epi(accs, ex, out):
        x1_ref, r_ref, g_ref, dx2_ref = ex
        r = r_ref[...]
        xh = x1_ref[...] * r
        dh2 = accs[0]
        out[0][...] = dx2_ref[...] + _rms_bwd(xh, r, g_ref[...], dh2)
        accumulate(out[1], jnp.sum(dh2 * xh, axis=0, keepdims=True))

    my_c = lax.axis_index("c")
    my_chip = 2 * lax.axis_index("x") + lax.axis_index("y")
    idx = jnp.stack([my_c, my_chip]).astype(jnp.int32)
    parts16, owns = {}, {}

    def split_cores(g8):
        return g8.reshape((4, 2) + g8.shape[1:])

    def pair_sums(names, grads4, from_sibling):
        for name, g4, recv in zip(names, grads4, from_sibling):
            parts16[name], owns[name] = _pair_sum("grad_pair_sum_" + name, g4, recv, idx, min(512, g4.shape[2]))

    grads4_mlp = [split_cores(g_up), split_cores(g_down.reshape(N_DEV, D_FF // N_DEV, d))]
    (dx1, g_norm2), from_sibling = _mm(
        "mlp_up_bwd", [(da, w_up_f, True, None)],
        [(x1, _row(tr, d)), (r2, _row(tr, 1)), (norm2_g, _whole((1, d))), (dx2, _row(tr, d))],
        [(_sds((s_len, d), F32), _row(tr, d)), (_sds((1, d), F32), _whole((1, d)))],
        dh2_epi, m=s_len, tm=tr, n=d, tn=d, arbitrary=True, comm=[_pair_exchange_plan(grads4_mlp)])
    pair_sums(("w_up", "w_down"), grads4_mlp, from_sibling)

    def dmerge_epi(accs, ex, out):
        dm = accs[0]
        sa, sb = _sigmoid(ex[0][...]), _sigmoid(ex[1][...])
        out[0][...] = (dm * sa).astype(BF16)
        out[1][...] = (dm * sb).astype(BF16)
        out[2][...] = (dm * ex[2][...] * sa * (1.0 - sa)).astype(BF16)
        out[3][...] = (dm * ex[3][...] * sb * (1.0 - sb)).astype(BF16)

    dya, dyb, dga, dgb = _mm(
        "out_proj_bwd", [(dx1, w_o_f, True, None)],
        [(prest, _tile(tm, d, GA_OFF // d)), (prest, _tile(tm, d, GB_OFF // d)), (ya, _tile(tm, d)), (yb, _tile(tm, d))],
        [(_sds((s_len, d), BF16), _tile(tm, d))] * 4, dmerge_epi, m=s_len, tm=tm, n=d, tn=d)
    g_o = _mm_tn("grad_w_o", merged, dx1, tk=512, tn=d, ts=ts).reshape(N_DEV, d // N_DEV, d)
    def col_blocks(g):
        return jnp.transpose(g.reshape(g.shape[0], N_DEV, g.shape[1] // N_DEV), (1, 0, 2))

    g_a = col_blocks(_mm_tn("grad_w_a", o, dya, tk=FOX_W, tn=d, ts=ts))
    g_b = col_blocks(_mm_tn("grad_w_b", sg, dyb, tk=SGU_W, tn=d, ts=ts))

    def do_epi(accs, ex, out):
        do = accs[0]
        out[0][...] = do
        out[1][...] = _dot_f32(do * ex[0][...], ex[1][...])

    grads4_mix = [split_cores(g) for g in (g_a, g_b, g_o)]
    (do, delta), from_sibling = _mm(
        "attn_out_bwd", [(dya, w_a_f, True, None)], [(o, _row(tm, FOX_W)), (head_sel, _whole((FOX_W, LANES)))],
        [(_sds((s_len, FOX_W), F32), _row(tm, FOX_W)), (_sds((s_len, LANES), F32), _row(tm, LANES))],
        do_epi, m=s_len, tm=tm, n=FOX_W, tn=FOX_W, comm=[_pair_exchange_plan(grads4_mix)])
    pair_sums(("w_a", "w_b", "w_o"), grads4_mix, from_sibling)
    (dsg,) = _mm("sgu_out_bwd", [(dyb, w_b_f, True, None)], [], [(_sds((s_len, SGU_W), F32), _tile(tm, SGU_W))],
                 store(F32), m=s_len, tm=tm, n=SGU_W, tn=SGU_W)

    du, dsv, dw_pairs, db_pos, g_ln_g, g_ln_b = _sgu_bwd(prest, dsg, ln_v_g, ln_v_b, w_stack, wt_stack, b_pair, tm)
    g_w_sgu = jnp.where(sgu_mask[None], dw_pairs.reshape(SGU_G, SGU_LEN, SGU_LEN), 0.0)
    g_b_sgu = jnp.transpose(jnp.sum(db_pos.reshape(SGU_LEN, SGU_G, SGU_W // SGU_G), axis=-1))

    delta_row = jnp.transpose(delta[:, :N_HEADS]).reshape(N_HEADS, 1, s_len)
    early = ("w_a", "w_b", "w_o", "w_up", "w_down")
    (dq, dk, dv, dc_rows_blk, dc_cols), from_chips_early = _attn_bwd(
        qkv, k_t, do, c_col, c_row, lse_row, delta_row, ta, ta, comm=[_chip_exchange_plan([parts16[n] for n in early])])
    dc_rows = jnp.transpose(dc_rows_blk.reshape(s_len // ta, N_HEADS, ta), (0, 2, 1)).reshape(s_len, N_HEADS)
    dc_rows = jnp.pad(dc_rows, ((0, 0), (0, LANES - N_HEADS)))
    dfl, g_bf = _forget_bwd(dc_rows, dc_cols, prest, b_f_pad, tc)

    dp = (dq, dk, dv, du, dsv, dga, dgb)
    g_in_cols = [_mm_tn("grad_w_in_%d" % k, h, seg, tk=d, tn=512, ts=ts) for k, seg in enumerate(dp)]
    g_f = _mm_tn("grad_w_in_f", h, dfl, tk=d, tn=LANES, ts=ts)[:, :N_HEADS]
    g_in_full = jnp.concatenate(g_in_cols[:3] + [g_f] + g_in_cols[3:], axis=1)
    g_in = jnp.transpose(g_in_full.reshape(d, N_DEV, IN_SHARD), (1, 0, 2))

    def dx_epi(accs, ex, out):
        x_ref, r_ref, g_ref, dx1_ref = ex
        dh = accs[0]
        for extra in accs[1:]:
            dh = dh + extra
        r = r_ref[...]
        xh = x_ref[...] * r
        out[0][...] = dx1_ref[...] + _rms_bwd(xh, r, g_ref[...], dh)
        accumulate(out[1], jnp.sum(dh * xh, axis=0, keepdims=True))

    rest_cols = ((du, U_OFF, 512), (dsv, SV_OFF, 512), (dga, GA_OFF, 1024), (dgb, GB_OFF, 1024), (dfl, F_OFF, LANES))
    dx_pairs = [(seg, w_qkv, True, (512 * k, 512 * (k + 1))) for k, seg in enumerate((dq, dk, dv))]
    dx_pairs += [(seg, w_rest, True, (lo, lo + width)) for seg, lo, width in rest_cols]
    grads4_in = [split_cores(g_in)]
    (grad_x, g_norm1), from_sibling = _mm(
        "proj_bwd", dx_pairs,
        [(xs, _row(tr, d)), (r1, _row(tr, 1)), (norm1_g, _whole((1, d))), (dx1, _row(tr, d))],
        [(_sds((s_len, d), F32), _row(tr, d)), (_sds((1, d), F32), _whole((1, d)))],
        dx_epi, m=s_len, tm=tr, n=d, tn=d, arbitrary=True, comm=[_pair_exchange_plan(grads4_in)])
    pair_sums(("w_in",), grads4_in, from_sibling)

    small_g = _pack_small((g_w_sgu, g_b_sgu, g_norm1, g_norm2, g_normf, g_ln_g, g_ln_b, g_bf[:, :N_HEADS]))
    from_chips_in, small_all = _run_comm("grad_last_exchange", [_chip_exchange_plan([parts16["w_in"]]), _gather_plan(small_g)])
    from_chips = dict(zip(early, from_chips_early), w_in=from_chips_in)

    names = ("w_in", "w_a", "w_b", "w_o", "w_up", "w_down")
    moments_m = (m_w_in, m_w_a, m_w_b, m_w_o, m_w_up, m_w_down)
    moments_v = (v_w_in, v_w_a, v_w_b, v_w_o, v_w_up, v_w_down)
    big_out = {}
    for name, w, m, v in zip(names, big, moments_m, moments_v):
        own = owns[name]
        res = _adamw_shard("adamw_" + name, own, from_chips[name], w, m[0], v[0], min(512, own.shape[0]))
        big_out[name] = [t[None] for t in res]

    small_w = _pack_small((w_sgu, b_sgu, norm1_g, norm2_g, normf_g, ln_v_g, ln_v_b, b_f))
    small_m = _pack_small((m_w_sgu, m_b_sgu, m_norm1_g, m_norm2_g, m_normf_g, m_ln_v_g, m_ln_v_b, m_b_f))
    small_v = _pack_small((v_w_sgu, v_b_sgu, v_norm1_g, v_norm2_g, v_normf_g, v_ln_v_g, v_ln_v_b, v_b_f))
    small_res = [_unpack_small(t) for t in _adamw_small(small_all, small_w, small_m, small_v)]
    small_names = [n for n, _ in _SMALL]
    small_out = {n: [res[k] for res in small_res] for k, n in enumerate(small_names)}

    loss = lax.psum(loss_part[0, 0], ("x", "y", "c"))

    order = ("norm1_g", "w_in", "b_f", "ln_v_g", "ln_v_b", "w_sgu", "b_sgu", "w_a", "w_b", "w_o", "norm2_g", "w_up",
             "w_down", "normf_g")
    table = {**big_out, **small_out}
    outs = [loss, grad_x[None]]
    for kind in range(4):
        outs += [table[n][kind] for n in order]
    return tuple(outs)
```

```python
import math

import jax
import jax.numpy as jnp
from jax import lax
from jax.experimental import pallas as pl
from jax.experimental.pallas import tpu as pltpu

F32 = jnp.float32
BF16 = jnp.bfloat16

N_DEV = 8
D_MODEL = 1024
N_HEADS = 8
HEAD_DIM = 64
FOX_W = N_HEADS * HEAD_DIM
SGU_G = 8
SGU_W = 512
SGU_LEN = 128
CHUNK = 64
D_FF = 4 * D_MODEL
IN_COLS = 3 * FOX_W + N_HEADS + 2 * SGU_W + 2 * D_MODEL
IN_SHARD = IN_COLS // N_DEV
LANES = 128
QKV_W = 3 * FOX_W
U_OFF, SV_OFF, GA_OFF, GB_OFF, F_OFF = 0, 512, 1024, 2048, 3072
REST_W = F_OFF + LANES
EPS = 1e-6
NEG = -1e30

ADAM_LR = 0.001
ADAM_B1 = 0.9
ADAM_B2 = 0.999
ADAM_EPS = 1e-08
ADAM_WD = 0.01
ADAM_STEP = 10

VMEM_LIMIT = 56 * 1024 * 1024
MESH = pl.DeviceIdType.MESH


def _params(sem=None):
    return pltpu.CompilerParams(dimension_semantics=sem, vmem_limit_bytes=VMEM_LIMIT)


def _dot(a, b):
    return jnp.dot(a, b, preferred_element_type=F32)


def _dot_nt(a, b):
    return lax.dot_general(a, b, (((1,), (1,)), ((), ())), preferred_element_type=F32)


def _dot_f32(a, b):
    return jnp.dot(a, b, preferred_element_type=F32, precision=lax.Precision.HIGHEST)


def _sigmoid(x):
    return 1.0 / (1.0 + jnp.exp(-x))


def _log_sigmoid(z):
    return jnp.minimum(z, 0.0) - jnp.log(1.0 + jnp.exp(-jnp.abs(z)))


_GELU_K = math.sqrt(2.0 / math.pi)
_GELU_C = 0.044715


def _gelu(x):
    t = jnp.tanh(_GELU_K * (x + _GELU_C * (x * x * x)))
    return 0.5 * x * (1.0 + t)


def _gelu_grad(x):
    x2 = x * x
    t = jnp.tanh(_GELU_K * (x + _GELU_C * (x2 * x)))
    return 0.5 * (1.0 + t) + 0.5 * x * (1.0 - t * t) * (_GELU_K * (1.0 + 3.0 * _GELU_C * x2))


def _rms_bwd(xh, r, g, dy):
    gy = dy * g
    return r * (gy - xh * jnp.mean(xh * gy, axis=-1, keepdims=True))


def _lane_lt64(shape):
    return lax.broadcasted_iota(jnp.int32, shape, len(shape) - 1) < HEAD_DIM


class _Comm:
    def __init__(self, arrays, out_shapes, sems, start, finish, mid=None):
        self.arrays, self.out_shapes, self.sems = list(arrays), list(out_shapes), list(sems)
        self.start, self.mid, self.finish = start, mid, finish


def _comm_phase(plans, phase, in_refs, out_refs, sem_refs):
    ia = io = ks = 0
    for plan in plans:
        na, no, ns = len(plan.arrays), len(plan.out_shapes), len(plan.sems)
        fn = getattr(plan, phase)
        if fn is not None:
            fn(in_refs[ia:ia + na], out_refs[io:io + no], sem_refs[ks:ks + ns])
        ia, io, ks = ia + na, io + no, ks + ns


def _comm_operands(plans):
    arrays = [a for plan in plans for a in plan.arrays]
    out_shapes = [o for plan in plans for o in plan.out_shapes]
    sems = [s for plan in plans for s in plan.sems]
    return arrays, out_shapes, sems


_ANY = pl.BlockSpec(memory_space=pl.ANY)


def _run_comm(name, plans):
    arrays, out_shapes, sems = _comm_operands(plans)
    n_in, n_out = len(arrays), len(out_shapes)

    def body(*refs):
        parts = refs[:n_in], refs[n_in:n_in + n_out], refs[n_in + n_out:]
        for phase in ("start", "mid", "finish"):
            _comm_phase(plans, phase, *parts)

    return pl.pallas_call(
        body, name=name, out_shape=out_shapes, in_specs=[_ANY] * n_in, out_specs=[_ANY] * n_out, scratch_shapes=sems,
    )(*arrays)


def _gather_plan(shard):
    def setup(ins, outs, sems):
        (x_ref,), (out_ref,), (send_sems, recv_sems, local_sem) = ins, outs, sems
        x, y, c = lax.axis_index("x"), lax.axis_index("y"), lax.axis_index("c")
        me, sibling = (x, y, c), (x, y, 1 - c)
        chips = [(1 - x, y), (x, 1 - y), (1 - x, 1 - y)]

        def rows(px, py, pc):
            return out_ref.at[4 * px + 2 * py + pc]

        def copy(k, block, to, src=None):
            return pltpu.make_async_remote_copy(
                src_ref=rows(*block) if src is None else src,
                dst_ref=rows(*block),
                send_sem=send_sems.at[k],
                recv_sem=recv_sems.at[k],
                device_id=to,
                device_id_type=MESH,
            )

        mine = pltpu.make_async_copy(x_ref, rows(*me), local_sem)
        first = [copy(0, me, sibling, src=x_ref)]
        first += [copy(1 + j, me, (*chip, c), src=x_ref) for j, chip in enumerate(chips)]
        passed = [copy(4 + j, (*chip, c), sibling) for j, chip in enumerate(chips)]
        landed = [copy(1 + j, (*chip, c), me) for j, chip in enumerate(chips)]
        from_sibling = [copy(0, sibling, me)] + [copy(4 + j, (*chip, 1 - c), me) for j, chip in enumerate(chips)]
        return mine, first, passed, landed, from_sibling

    def start(ins, outs, sems):
        mine, first, _, _, _ = setup(ins, outs, sems)
        mine.start()
        for cp in first:
            cp.start()

    def mid(ins, outs, sems):
        _, _, passed, landed, _ = setup(ins, outs, sems)
        for arrived, onward in zip(landed, passed):
            arrived.wait_recv()
            onward.start()

    def finish(ins, outs, sems):
        mine, first, passed, _, from_sibling = setup(ins, outs, sems)
        for cp in from_sibling:
            cp.wait_recv()
        for cp in first + passed:
            cp.wait_send()
        mine.wait()

    return _Comm([shard], [jax.ShapeDtypeStruct((N_DEV,) + shard.shape, shard.dtype)],
                 [pltpu.SemaphoreType.DMA((7,)), pltpu.SemaphoreType.DMA((7,)), pltpu.SemaphoreType.DMA],
                 start, finish, mid)


def _start_all(copies):
    for cp in copies:
        cp.start()


def _wait_all(copies):
    for cp in copies:
        cp.wait_recv()
    for cp in copies:
        cp.wait_send()


def _pair_exchange_plan(grads):
    n = len(grads)

    def copies(ins, outs, sems):
        send_sems, recv_sems = sems
        x, y, c = lax.axis_index("x"), lax.axis_index("y"), lax.axis_index("c")
        return [
            pltpu.make_async_remote_copy(
                src_ref=ins[k].at[:, 1 - c],
                dst_ref=outs[k],
                send_sem=send_sems.at[k],
                recv_sem=recv_sems.at[k],
                device_id=(x, y, 1 - c),
                device_id_type=MESH,
            )
            for k in range(n)
        ]

    return _Comm(grads, [jax.ShapeDtypeStruct((4,) + g.shape[2:], g.dtype) for g in grads],
                 [pltpu.SemaphoreType.DMA((n,)), pltpu.SemaphoreType.DMA((n,))],
                 lambda *refs: _start_all(copies(*refs)), lambda *refs: _wait_all(copies(*refs)))


def _chip_exchange_plan(parts):
    n = len(parts)

    def copies(ins, outs, sems):
        send_sems, recv_sems = sems
        x, y, c = lax.axis_index("x"), lax.axis_index("y"), lax.axis_index("c")
        chips = [(1 - x, y), (x, 1 - y), (1 - x, 1 - y)]
        return [
            pltpu.make_async_remote_copy(
                src_ref=ins[k].at[2 * px + py],
                dst_ref=outs[k].at[j],
                send_sem=send_sems.at[3 * k + j],
                recv_sem=recv_sems.at[3 * k + j],
                device_id=(px, py, c),
                device_id_type=MESH,
            )
            for k in range(n) for j, (px, py) in enumerate(chips)
        ]

    return _Comm(parts, [jax.ShapeDtypeStruct((3,) + p.shape[1:], p.dtype) for p in parts],
                 [pltpu.SemaphoreType.DMA((3 * n,)), pltpu.SemaphoreType.DMA((3 * n,))],
                 lambda *refs: _start_all(copies(*refs)), lambda *refs: _wait_all(copies(*refs)))


def _mm(name, pairs, extras, outs, epi, *, m, tm, n, tn, arbitrary=False, comm=()):
    nj = n // tn
    a_arrays, a_specs, b_arrays, b_specs, b_index = [], [], [], [], []
    for a, b, nt, cols in pairs:
        a_arrays.append(a)
        a_specs.append(pl.BlockSpec((tm, a.shape[1]), lambda i, j: (i, 0)))
        known = [k for k, other in enumerate(b_arrays) if other is b]
        if known:
            b_index.append(known[0])
            continue
        b_index.append(len(b_arrays))
        b_arrays.append(b)
        if cols is not None:
            assert nj == 1
            b_specs.append(pl.BlockSpec(b.shape, lambda i, j: (0, 0)))
        elif nt:
            b_specs.append(pl.BlockSpec((tn, b.shape[1]), lambda i, j: (j, 0)))
        else:
            b_specs.append(pl.BlockSpec((b.shape[0], tn), lambda i, j: (0, j)))
    comm_arrays, comm_outs, comm_sems = _comm_operands(comm)
    arrays = a_arrays + b_arrays + [arr for arr, _ in extras] + comm_arrays
    in_specs = a_specs + b_specs + [spec for _, spec in extras] + [_ANY] * len(comm_arrays)
    n_a, n_b, n_extras, n_ci, n_out, n_co = len(a_arrays), len(b_arrays), len(extras), len(comm_arrays), len(outs), len(comm_outs)
    ni = m // tm

    def body(*refs):
        a_refs = refs[:n_a]
        b_refs = refs[n_a:n_a + n_b]
        ex = refs[n_a + n_b:n_a + n_b + n_extras]
        n_in = n_a + n_b + n_extras + n_ci
        comm_refs = refs[n_in - n_ci:n_in], refs[n_in + n_out:n_in + n_out + n_co], refs[n_in + n_out + n_co:]
        out = refs[n_in:n_in + n_out]
        if comm:
            @pl.when(jnp.logical_and(pl.program_id(0) == 0, pl.program_id(1) == 0))
            def _():
                _comm_phase(comm, "start", *comm_refs)

        accs = []
        for p, (_, _, nt, cols) in enumerate(pairs):
            av = a_refs[p][...]
            if av.dtype != BF16:
                av = av.astype(BF16)
            b_ref = b_refs[b_index[p]]
            bv = b_ref[...] if cols is None else b_ref[:, cols[0]:cols[1]]
            accs.append(_dot_nt(av, bv) if nt else _dot(av, bv))
        epi(accs, ex, out)
        if comm:
            @pl.when(jnp.logical_and(pl.program_id(0) == ni - 1, pl.program_id(1) == nj - 1))
            def _():
                _comm_phase(comm, "mid", *comm_refs)
                _comm_phase(comm, "finish", *comm_refs)

    sem = ("arbitrary", "arbitrary") if arbitrary or comm else ("parallel", "parallel")
    res = pl.pallas_call(
        body,
        name=name,
        grid=(ni, nj),
        in_specs=in_specs,
        out_specs=[spec for _, spec in outs] + [_ANY] * n_co,
        out_shape=[shape for shape, _ in outs] + comm_outs,
        scratch_shapes=comm_sems,
        compiler_params=_params(sem),
    )(*arrays)
    return (res[:n_out], res[n_out:]) if comm else res


def _tile(tm, tn, off=0):
    return pl.BlockSpec((tm, tn), lambda i, j: (i, j + off))


def _row(tm, w, blk=0):
    return pl.BlockSpec((tm, w), lambda i, j: (i, blk))


def _whole(shape):
    zeros = (0,) * len(shape)
    return pl.BlockSpec(shape, lambda i, j: zeros)


def _sds(shape, dtype):
    return jax.ShapeDtypeStruct(shape, dtype)


def _tile_t(tm, tn):
    return pl.BlockSpec((tn, tm), lambda i, j: (j, i))


def _grad_w(name, a_t, g, *, tk, tn, ts, block_cols=None):
    ka, s_len = a_t.shape
    n = g.shape[1]
    width = tn if block_cols is None else block_cols

    def body(a_ref, g_ref, o_ref):
        first = pl.program_id(2) == 0
        gv = g_ref[...].astype(BF16)
        for b in range(tn // width):
            part = _dot(a_ref[...], gv[:, b * width:(b + 1) * width])
            dst = o_ref if block_cols is None else o_ref.at[b]

            @pl.when(first)
            def _():
                dst[...] = part

            @pl.when(jnp.logical_not(first))
            def _():
                dst[...] += part

    if block_cols is None:
        out_shape = _sds((ka, n), F32)
        out_spec = pl.BlockSpec((tk, tn), lambda i, j, s: (i, j))
    else:
        out_shape = _sds((n // width, ka, width), F32)
        out_spec = pl.BlockSpec((tn // width, tk, width), lambda i, j, s: (j, i, 0))
    return pl.pallas_call(
        body,
        name=name,
        grid=(ka // tk, n // tn, s_len // ts),
        in_specs=[pl.BlockSpec((tk, ts), lambda i, j, s: (i, s)), pl.BlockSpec((ts, tn), lambda i, j, s: (s, j))],
        out_specs=out_spec,
        out_shape=out_shape,
        compiler_params=_params(("parallel", "parallel", "arbitrary")),
    )(a_t, g)


def _rms_fwd(name, x, g, tm):
    s_len, d = x.shape

    def body(x_ref, g_ref, h_ref, ht_ref, r_ref):
        xv = x_ref[...]
        r = lax.rsqrt(jnp.mean(xv * xv, axis=-1, keepdims=True) + EPS)
        h = (xv * r * g_ref[...]).astype(BF16)
        h_ref[...] = h
        ht_ref[...] = jnp.transpose(h)
        r_ref[...] = r

    return pl.pallas_call(
        body,
        name=name,
        grid=(s_len // tm,),
        in_specs=[pl.BlockSpec((tm, d), lambda i: (i, 0)), pl.BlockSpec((1, d), lambda i: (0, 0))],
        out_specs=[pl.BlockSpec((tm, d), lambda i: (i, 0)), pl.BlockSpec((d, tm), lambda i: (0, i)),
                   pl.BlockSpec((tm, 1), lambda i: (i, 0))],
        out_shape=[_sds((s_len, d), BF16), _sds((d, s_len), BF16), _sds((s_len, 1), F32)],
        compiler_params=_params(("parallel",)),
    )(x, g)


def _forget_cumsum(prest, b_f_pad, tc):
    s_len = prest.shape[0]

    def body(f_ref, b_ref, c_ref, carry):
        @pl.when(pl.program_id(0) == 0)
        def _():
            carry[...] = jnp.zeros_like(carry)

        logf = _log_sigmoid(f_ref[...] + b_ref[...])
        row = lax.broadcasted_iota(jnp.int32, (tc, tc), 0)
        col = lax.broadcasted_iota(jnp.int32, (tc, tc), 1)
        tri = (row >= col).astype(F32)
        c = _dot_f32(tri, logf) + carry[...]
        c_ref[...] = c
        carry[...] = c[tc - 1:tc, :]

    return pl.pallas_call(
        body,
        name="forget_cumsum",
        grid=(s_len // tc,),
        in_specs=[pl.BlockSpec((tc, LANES), lambda i: (i, F_OFF // LANES)), pl.BlockSpec((1, LANES), lambda i: (0, 0))],
        out_specs=pl.BlockSpec((tc, LANES), lambda i: (i, 0)),
        out_shape=_sds((s_len, LANES), F32),
        scratch_shapes=[pltpu.VMEM((1, LANES), F32)],
        compiler_params=_params(("arbitrary",)),
    )(prest, b_f_pad)


def _stack_heads(pair, lt64):
    zero = jnp.zeros_like(pair)
    return jnp.concatenate([jnp.where(lt64, pair, zero), jnp.where(lt64, zero, pair)], axis=0)


def _score_tiles(q_ref, k_ref, cq_ref, ck_ref, st_sc, tk):
    lt64 = _lane_lt64((tk, LANES))
    for p in range(N_HEADS // 2):
        lanes = slice(p * LANES, (p + 1) * LANES)
        q_pair = q_ref[:, lanes] * jnp.asarray(HEAD_DIM ** -0.5, BF16)
        st2 = _dot_nt(_stack_heads(k_ref[:, lanes], lt64), q_pair)
        for half in range(2):
            h = 2 * p + half
            st_sc[h] = st2[half * tk:(half + 1) * tk] + (cq_ref[h] - ck_ref[:, h:h + 1])


def _mask_diagonal(st_sc, i, j, tq, tk):
    @pl.when((j + 1) * tk - 1 > i * tq)
    def _():
        key = j * tk + lax.broadcasted_iota(jnp.int32, (tk, tq), 0)
        query = i * tq + lax.broadcasted_iota(jnp.int32, (tk, tq), 1)
        st_sc[...] = jnp.where((query >= key)[None], st_sc[...], NEG)


def _attn_fwd(qkv, v_t, c_col, c_row, tq, tk, comm=()):
    s_len = qkv.shape[0]
    ratio = tq // tk
    steps = [(i, j) for i in range(s_len // tq) for j in range((i + 1) * ratio)]
    i_tab = jnp.asarray([i for i, _ in steps], jnp.int32)
    j_tab = jnp.asarray([j for _, j in steps], jnp.int32)

    comm_arrays, comm_outs, comm_sems = _comm_operands(comm)
    n_ci, n_co = len(comm_arrays), len(comm_outs)

    def body(i_ref, j_ref, q_ref, k_ref, vt_ref, cq_ref, ck_ref, *rest):
        comm_refs = rest[:n_ci], rest[n_ci + 3:n_ci + 3 + n_co], rest[n_ci + 3 + n_co + 5:]
        o_ref, ot_ref, lse_ref = rest[n_ci:n_ci + 3]
        acc_t, m_sc, l_sc, st_sc, p_sc = rest[n_ci + 3 + n_co:n_ci + 3 + n_co + 5]
        n = pl.program_id(0)
        i, j = i_ref[n], j_ref[n]
        for phase, at in (("start", 0), ("mid", (2 * len(steps)) // 3)):
            if comm:
                @pl.when(n == at)
                def _():
                    _comm_phase(comm, phase, *comm_refs)

        @pl.when(j == 0)
        def _():
            acc_t[...] = jnp.zeros_like(acc_t)
            m_sc[...] = jnp.full_like(m_sc, NEG)
            l_sc[...] = jnp.zeros_like(l_sc)

        _score_tiles(q_ref, k_ref, cq_ref, ck_ref, st_sc, tk)
        _mask_diagonal(st_sc, i, j, tq, tk)
        st = st_sc[...]
        m_old = m_sc[...]
        m_new = jnp.maximum(m_old, jnp.max(st, axis=1, keepdims=True))
        alpha = jnp.exp(m_old - m_new)
        pt = jnp.exp(st - m_new)
        l_sc[...] = alpha * l_sc[...] + jnp.sum(pt, axis=1, keepdims=True)
        m_sc[...] = m_new
        p_sc[...] = pt.astype(BF16)
        top = lax.broadcasted_iota(jnp.int32, (LANES, tq), 0) < HEAD_DIM
        for p in range(N_HEADS // 2):
            lanes = slice(p * LANES, (p + 1) * LANES)
            vt_pair = vt_ref[lanes, :]
            pv = jnp.where(top, _dot(vt_pair, p_sc[2 * p]), _dot(vt_pair, p_sc[2 * p + 1]))
            acc_t[lanes, :] = acc_t[lanes, :] * jnp.where(top, alpha[2 * p], alpha[2 * p + 1]) + pv

        @pl.when(j == (i + 1) * ratio - 1)
        def _():
            for p in range(N_HEADS // 2):
                lanes = slice(p * LANES, (p + 1) * LANES)
                l_pair = jnp.where(top, l_sc[2 * p], l_sc[2 * p + 1])
                o_t = acc_t[lanes, :] / l_pair
                o_ref[:, lanes] = jnp.transpose(o_t)
                ot_ref[lanes, :] = o_t.astype(BF16)
            lse_ref[...] = m_sc[...] + jnp.log(l_sc[...])

        if comm:
            @pl.when(n == len(steps) - 1)
            def _():
                _comm_phase(comm, "finish", *comm_refs)

    stat = pltpu.VMEM((N_HEADS, 1, tq), F32)
    res = pl.pallas_call(
        body,
        name="attn_fwd",
        grid_spec=pltpu.PrefetchScalarGridSpec(
            num_scalar_prefetch=2,
            grid=(len(steps),),
            in_specs=[
                pl.BlockSpec((tq, FOX_W), lambda n, it, jt: (it[n], 0)),
                pl.BlockSpec((tk, FOX_W), lambda n, it, jt: (jt[n], 1)),
                pl.BlockSpec((FOX_W, tk), lambda n, it, jt: (0, jt[n])),
                pl.BlockSpec((N_HEADS, 1, tq), lambda n, it, jt: (0, 0, it[n])),
                pl.BlockSpec((tk, LANES), lambda n, it, jt: (jt[n], 0)),
            ] + [_ANY] * n_ci,
            out_specs=[
                pl.BlockSpec((tq, FOX_W), lambda n, it, jt: (it[n], 0)),
                pl.BlockSpec((FOX_W, tq), lambda n, it, jt: (0, it[n])),
                pl.BlockSpec((N_HEADS, 1, tq), lambda n, it, jt: (0, 0, it[n])),
            ] + [_ANY] * n_co,
            scratch_shapes=[pltpu.VMEM((FOX_W, tq), F32), stat, stat, pltpu.VMEM((N_HEADS, tk, tq), F32),
                            pltpu.VMEM((N_HEADS, tk, tq), BF16)] + comm_sems,
        ),
        out_shape=[_sds((s_len, FOX_W), F32), _sds((FOX_W, s_len), BF16), _sds((N_HEADS, 1, s_len), F32)] + comm_outs,
        compiler_params=_params(("arbitrary",)),
    )(i_tab, j_tab, qkv, qkv, v_t, c_row, c_col, *comm_arrays)
    return res[0], res[1], res[2], res[3:]


def _sgu_mix(vn, w_stack, lt64):
    outs = []
    for p in range(SGU_G // 2):
        r = _dot(w_stack[p], vn[:, p * LANES:(p + 1) * LANES])
        outs.append(jnp.where(lt64, r[:SGU_LEN], r[SGU_LEN:]))
    return jnp.concatenate(outs, axis=1)


def _sgu_norm(sv, ln_g, ln_b):
    svg = _gelu(sv)
    xc = svg - jnp.mean(svg, axis=-1, keepdims=True)
    rstd = lax.rsqrt(jnp.mean(xc * xc, axis=-1, keepdims=True) + EPS)
    xhat = xc * rstd
    return xhat, rstd, xhat * ln_g + ln_b


def _sgu_fwd(prest, ln_g, ln_b, w_stack, b_pair, tm):
    s_len = prest.shape[0]

    def body(u_ref, sv_ref, g_ref, b_ref, w_ref, bp_ref, sg_ref, sgt_ref):
        lt64 = _lane_lt64((SGU_LEN, LANES))
        _, _, vn = _sgu_norm(sv_ref[...], g_ref[...], b_ref[...])
        vn = vn.astype(BF16)
        w_stack_v = [w_ref[p] for p in range(SGU_G // 2)]
        for w in range(tm // SGU_LEN):
            win = slice(w * SGU_LEN, (w + 1) * SGU_LEN)
            mixed = _sgu_mix(vn[win], w_stack_v, lt64) + bp_ref[...]
            sg = (_gelu(u_ref[win, :]) * mixed).astype(BF16)
            sg_ref[win, :] = sg
            sgt_ref[:, win] = jnp.transpose(sg)

    return pl.pallas_call(
        body,
        name="sgu_fwd",
        grid=(s_len // tm,),
        in_specs=[
            pl.BlockSpec((tm, SGU_W), lambda i: (i, U_OFF // SGU_W)),
            pl.BlockSpec((tm, SGU_W), lambda i: (i, SV_OFF // SGU_W)),
            pl.BlockSpec((1, SGU_W), lambda i: (0, 0)),
            pl.BlockSpec((1, SGU_W), lambda i: (0, 0)),
            pl.BlockSpec((SGU_G // 2, 2 * SGU_LEN, SGU_LEN), lambda i: (0, 0, 0)),
            pl.BlockSpec((SGU_LEN, SGU_W), lambda i: (0, 0)),
        ],
        out_specs=[pl.BlockSpec((tm, SGU_W), lambda i: (i, 0)), pl.BlockSpec((SGU_W, tm), lambda i: (0, i))],
        out_shape=[_sds((s_len, SGU_W), BF16), _sds((SGU_W, s_len), BF16)],
        compiler_params=_params(("parallel",)),
    )(prest, prest, ln_g, ln_b, w_stack, b_pair)


def _sgu_bwd(prest, dsg, ln_g, ln_b, w_stack, wt_stack, b_pair, tm):
    s_len = prest.shape[0]
    n_pair = SGU_G // 2

    def body(u_ref, sv_ref, dsg_ref, g_ref, b_ref, w_ref, wt_ref, bp_ref,
             du_ref, dsv_ref, dw_ref, db_ref, dg_ref, dbeta_ref, dvn_sc):
        @pl.when(pl.program_id(0) == 0)
        def _():
            dw_ref[...] = jnp.zeros_like(dw_ref)
            db_ref[...] = jnp.zeros_like(db_ref)
            dg_ref[...] = jnp.zeros_like(dg_ref)
            dbeta_ref[...] = jnp.zeros_like(dbeta_ref)

        lt64 = _lane_lt64((SGU_LEN, LANES))
        sv = sv_ref[...]
        xhat, rstd, vn32 = _sgu_norm(sv, g_ref[...], b_ref[...])
        vn = vn32.astype(BF16)
        w_stack_v = [w_ref[p] for p in range(n_pair)]
        db = jnp.zeros((SGU_LEN, SGU_W), F32)
        for w in range(tm // SGU_LEN):
            win = slice(w * SGU_LEN, (w + 1) * SGU_LEN)
            u = u_ref[win, :]
            dsg_w = dsg_ref[win, :]
            mixed = _sgu_mix(vn[win], w_stack_v, lt64) + bp_ref[...]
            du_ref[win, :] = (dsg_w * mixed * _gelu_grad(u)).astype(BF16)
            dmixed = dsg_w * _gelu(u)
            db = db + dmixed
            dm16 = dmixed.astype(BF16)
            for p in range(n_pair):
                lanes = slice(p * LANES, (p + 1) * LANES)
                dmp = dm16[:, lanes]
                r = _dot(wt_ref[p], dmp)
                dvn_sc[win, lanes] = jnp.where(lt64, r[:SGU_LEN], r[SGU_LEN:])
                zero = jnp.zeros_like(dmp)
                dm_ab = jnp.concatenate([jnp.where(lt64, dmp, zero), jnp.where(lt64, zero, dmp)], axis=0)
                dw_ref[p] += _dot_nt(dm_ab, vn[win, lanes])
        db_ref[...] += db
        dvn = dvn_sc[...]
        dg_ref[...] += jnp.sum(dvn * xhat, axis=0, keepdims=True)
        dbeta_ref[...] += jnp.sum(dvn, axis=0, keepdims=True)
        dxh = dvn * g_ref[...]
        dsvg = rstd * (dxh - jnp.mean(dxh, axis=-1, keepdims=True) - xhat * jnp.mean(dxh * xhat, axis=-1, keepdims=True))
        dsv_ref[...] = (dsvg * _gelu_grad(sv)).astype(BF16)

    const2 = lambda i: (0, 0)
    const3 = lambda i: (0, 0, 0)
    return pl.pallas_call(
        body,
        name="sgu_bwd",
        grid=(s_len // tm,),
        in_specs=[
            pl.BlockSpec((tm, SGU_W), lambda i: (i, U_OFF // SGU_W)),
            pl.BlockSpec((tm, SGU_W), lambda i: (i, SV_OFF // SGU_W)),
            pl.BlockSpec((tm, SGU_W), lambda i: (i, 0)),
            pl.BlockSpec((1, SGU_W), const2),
            pl.BlockSpec((1, SGU_W), const2),
            pl.BlockSpec((n_pair, 2 * SGU_LEN, SGU_LEN), const3),
            pl.BlockSpec((n_pair, 2 * SGU_LEN, SGU_LEN), const3),
            pl.BlockSpec((SGU_LEN, SGU_W), const2),
        ],
        out_specs=[
            pl.BlockSpec((tm, SGU_W), lambda i: (i, 0)),
            pl.BlockSpec((tm, SGU_W), lambda i: (i, 0)),
            pl.BlockSpec((n_pair, 2 * SGU_LEN, SGU_LEN), const3),
            pl.BlockSpec((SGU_LEN, SGU_W), const2),
            pl.BlockSpec((1, SGU_W), const2),
            pl.BlockSpec((1, SGU_W), const2),
        ],
        out_shape=[
            _sds((s_len, SGU_W), BF16), _sds((s_len, SGU_W), BF16), _sds((n_pair, 2 * SGU_LEN, SGU_LEN), F32),
            _sds((SGU_LEN, SGU_W), F32), _sds((1, SGU_W), F32), _sds((1, SGU_W), F32),
        ],
        scratch_shapes=[pltpu.VMEM((tm, SGU_W), F32)],
        compiler_params=_params(("arbitrary",)),
    )(prest, prest, dsg, ln_g, ln_b, w_stack, wt_stack, b_pair)


def _attn_bwd(qkv, k_t, do, c_col, c_row, lse_row, delta_row, tq, tk, comm=()):
    s_len = qkv.shape[0]
    nq, nk = s_len // tq, s_len // tk
    ratio = tq // tk
    scale = HEAD_DIM ** -0.5
    steps = [(j, i) for j in range(nk) for i in range(j // ratio, nq)]
    j_tab = jnp.asarray([j for j, _ in steps], jnp.int32)
    i_tab = jnp.asarray([i for _, i in steps], jnp.int32)

    comm_arrays, comm_outs, comm_sems = _comm_operands(comm)
    n_ci, n_co = len(comm_arrays), len(comm_outs)

    def body(j_ref, i_ref, q_ref, k_ref, v_ref, kt_ref, do_ref, cq_ref, ck_ref, lse_ref, dl_ref, *rest):
        comm_refs = rest[:n_ci], rest[n_ci + 5:n_ci + 5 + n_co], rest[n_ci + 5 + n_co + 8:]
        dq_ref, dk_ref, dv_ref, dcr_ref, dcc_ref = rest[n_ci:n_ci + 5]
        dq_t, dk_acc, dv_acc, dcc_acc, st_sc, dpt_sc, p_sc, ds_sc = rest[n_ci + 5 + n_co:n_ci + 5 + n_co + 8]
        n = pl.program_id(0)
        j, i = j_ref[n], i_ref[n]

        @pl.when(n == 0)
        def _():
            _comm_phase(comm, "start", *comm_refs)
            dq_t[...] = jnp.zeros_like(dq_t)
            dcr_ref[...] = jnp.zeros_like(dcr_ref)

        @pl.when(i == j // ratio)
        def _():
            dk_acc[...] = jnp.zeros_like(dk_acc)
            dv_acc[...] = jnp.zeros_like(dv_acc)
            dcc_acc[...] = jnp.zeros_like(dcc_acc)

        lt64 = _lane_lt64((tk, LANES))
        _score_tiles(q_ref, k_ref, cq_ref, ck_ref, st_sc, tk)
        for p in range(N_HEADS // 2):
            lanes = slice(p * LANES, (p + 1) * LANES)
            dpt2 = _dot_nt(_stack_heads(v_ref[:, lanes], lt64), do_ref[:, lanes].astype(BF16))
            dpt_sc[2 * p] = dpt2[:tk]
            dpt_sc[2 * p + 1] = dpt2[tk:]
        _mask_diagonal(st_sc, i, j, tq, tk)

        pt = jnp.exp(st_sc[...] - lse_ref[...])
        dst = pt * (dpt_sc[...] - dl_ref[...])
        p_sc[...] = pt.astype(BF16)
        ds_sc[...] = dst.astype(BF16)
        dcr_ref[i] += jnp.sum(dst, axis=1, keepdims=True)
        col_sums = jnp.sum(dst, axis=2, keepdims=True)
        lane = lax.broadcasted_iota(jnp.int32, (tk, LANES), 1)
        dcc = jnp.zeros((tk, LANES), F32)
        for h in range(N_HEADS):
            dcc = jnp.where(lane == h, -col_sums[h], dcc)
        dcc_acc[...] += dcc

        for p in range(N_HEADS // 2):
            lanes = slice(p * LANES, (p + 1) * LANES)
            q_pair = q_ref[:, lanes] * jnp.asarray(scale, BF16)
            dv2 = _dot(p_sc[2 * p:2 * p + 2].reshape(2 * tk, tq), do_ref[:, lanes].astype(BF16))
            dv_acc[:, lanes] += jnp.where(lt64, dv2[:tk], dv2[tk:])
            dk2 = _dot(ds_sc[2 * p:2 * p + 2].reshape(2 * tk, tq), q_pair)
            dk_acc[:, lanes] += jnp.where(lt64, dk2[:tk], dk2[tk:])
            dq2 = _dot(kt_ref[lanes, :], jnp.concatenate([ds_sc[2 * p], ds_sc[2 * p + 1]], axis=1))
            top = lax.broadcasted_iota(jnp.int32, (LANES, tq), 0) < HEAD_DIM
            dq_t[i, lanes, :] += jnp.where(top, dq2[:, :tq], dq2[:, tq:])

        @pl.when(j == (i + 1) * ratio - 1)
        def _():
            rows = pl.ds(pl.multiple_of(i * tq, tq), tq)
            for p in range(N_HEADS // 2):
                lanes = slice(p * LANES, (p + 1) * LANES)
                dq_ref[rows, lanes] = (jnp.transpose(dq_t[i, lanes, :]) * scale).astype(BF16)

        @pl.when(i == nq - 1)
        def _():
            dk_ref[...] = dk_acc[...].astype(BF16)
            dv_ref[...] = dv_acc[...].astype(BF16)
            dcc_ref[...] = dcc_acc[...]

        if comm:
            @pl.when(n == len(steps) - 1)
            def _():
                _comm_phase(comm, "mid", *comm_refs)
                _comm_phase(comm, "finish", *comm_refs)

    q_map = lambda n, jt, it: (it[n], 0)
    q_stat = lambda n, jt, it: (0, 0, it[n])
    k_map = lambda n, jt, it: (jt[n], 0)
    tile = (N_HEADS, tk, tq)
    res = pl.pallas_call(
        body,
        name="attn_bwd",
        grid_spec=pltpu.PrefetchScalarGridSpec(
            num_scalar_prefetch=2,
            grid=(len(steps),),
            in_specs=[
                pl.BlockSpec((tq, FOX_W), q_map),
                pl.BlockSpec((tk, FOX_W), lambda n, jt, it: (jt[n], 1)),
                pl.BlockSpec((tk, FOX_W), lambda n, jt, it: (jt[n], 2)),
                pl.BlockSpec((FOX_W, tk), lambda n, jt, it: (0, jt[n])),
                pl.BlockSpec((tq, FOX_W), q_map),
                pl.BlockSpec((N_HEADS, 1, tq), q_stat),
                pl.BlockSpec((tk, LANES), k_map),
                pl.BlockSpec((N_HEADS, 1, tq), q_stat),
                pl.BlockSpec((N_HEADS, 1, tq), q_stat),
            ] + [_ANY] * n_ci,
            out_specs=[
                pl.BlockSpec((s_len, FOX_W), lambda n, jt, it: (0, 0)),
                pl.BlockSpec((tk, FOX_W), k_map),
                pl.BlockSpec((tk, FOX_W), k_map),
                pl.BlockSpec((nq, N_HEADS, 1, tq), lambda n, jt, it: (0, 0, 0, 0)),
                pl.BlockSpec((tk, LANES), k_map),
            ] + [_ANY] * n_co,
            scratch_shapes=[pltpu.VMEM((nq, FOX_W, tq), F32), pltpu.VMEM((tk, FOX_W), F32), pltpu.VMEM((tk, FOX_W), F32),
                            pltpu.VMEM((tk, LANES), F32), pltpu.VMEM(tile, F32), pltpu.VMEM(tile, F32),
                            pltpu.VMEM(tile, BF16), pltpu.VMEM(tile, BF16)] + comm_sems,
        ),
        out_shape=[_sds((s_len, FOX_W), BF16), _sds((s_len, FOX_W), BF16), _sds((s_len, FOX_W), BF16),
                   _sds((nq, N_HEADS, 1, tq), F32), _sds((s_len, LANES), F32)] + comm_outs,
        compiler_params=_params(("arbitrary",)),
    )(j_tab, i_tab, qkv, qkv, qkv, k_t, do, c_row, c_col, lse_row, delta_row, *comm_arrays)
    return res[:5], res[5:]


def _forget_bwd(dc_rows, dc_cols, prest, b_f_pad, tc):
    s_len = dc_rows.shape[0]
    nb = s_len // tc

    def body(dcr_ref, dc_ref, f_ref, b_ref, df_ref, db_ref, carry):
        @pl.when(pl.program_id(0) == 0)
        def _():
            carry[...] = jnp.zeros_like(carry)
            db_ref[...] = jnp.zeros_like(db_ref)

        row = lax.broadcasted_iota(jnp.int32, (tc, tc), 0)
        col = lax.broadcasted_iota(jnp.int32, (tc, tc), 1)
        tri = (row <= col).astype(F32)
        dlogf = _dot_f32(tri, dcr_ref[...] + dc_ref[...]) + carry[...]
        carry[...] = dlogf[0:1, :]
        z = f_ref[...] + b_ref[...]
        lane = lax.broadcasted_iota(jnp.int32, (tc, LANES), 1)
        dz = jnp.where(lane < N_HEADS, dlogf * _sigmoid(-z), 0.0)
        df_ref[...] = dz.astype(BF16)
        db_ref[...] += jnp.sum(dz, axis=0, keepdims=True)

    rev = lambda i: (nb - 1 - i, 0)
    return pl.pallas_call(
        body,
        name="forget_bwd",
        grid=(nb,),
        in_specs=[
            pl.BlockSpec((tc, LANES), rev),
            pl.BlockSpec((tc, LANES), rev),
            pl.BlockSpec((tc, LANES), lambda i: (nb - 1 - i, F_OFF // LANES)),
            pl.BlockSpec((1, LANES), lambda i: (0, 0)),
        ],
        out_specs=[pl.BlockSpec((tc, LANES), rev), pl.BlockSpec((1, LANES), lambda i: (0, 0))],
        out_shape=[_sds((s_len, LANES), BF16), _sds((1, LANES), F32)],
        scratch_shapes=[pltpu.VMEM((1, LANES), F32)],
        compiler_params=_params(("arbitrary",)),
    )(dc_rows, dc_cols, prest, b_f_pad)


def _pair_sum(name, g4, recv, idx, tr):
    _, _, r, c = g4.shape

    def body(idx_ref, g_ref, r_ref, p16_ref, own_ref):
        k = pl.program_id(1)
        s = g_ref[...] + r_ref[...]
        p16_ref[...] = s.astype(BF16)

        @pl.when(k == idx_ref[1])
        def _():
            own_ref[...] = s

    return pl.pallas_call(
        body,
        name=name,
        grid_spec=pltpu.PrefetchScalarGridSpec(
            num_scalar_prefetch=1,
            grid=(r // tr, 4),
            in_specs=[
                pl.BlockSpec((None, None, tr, c), lambda i, k, idx: (k, idx[0], i, 0)),
                pl.BlockSpec((None, tr, c), lambda i, k, idx: (k, i, 0)),
            ],
            out_specs=[
                pl.BlockSpec((None, tr, c), lambda i, k, idx: (k, i, 0)),
                pl.BlockSpec((tr, c), lambda i, k, idx: (i, 0)),
            ],
        ),
        out_shape=[_sds((4, r, c), BF16), _sds((r, c), F32)],
        compiler_params=_params(("parallel", "arbitrary")),
    )(idx, g4, recv)


def _adamw_math(w, g, m, v):
    m2 = ADAM_B1 * m + (1.0 - ADAM_B1) * g
    v2 = ADAM_B2 * v + (1.0 - ADAM_B2) * (g * g)
    m_hat = m2 / (1.0 - ADAM_B1 ** ADAM_STEP)
    v_hat = v2 / (1.0 - ADAM_B2 ** ADAM_STEP)
    delta = -ADAM_LR * (m_hat / (jnp.sqrt(v_hat) + ADAM_EPS) + ADAM_WD * w)
    return delta, m2, v2


def _adamw_shard(name, own, recv, w, m, v, tr):
    r, c = own.shape

    def body(own_ref, recv_ref, w_ref, m_ref, v_ref, g_ref, d_ref, m2_ref, v2_ref):
        g = own_ref[...]
        for k in range(3):
            g = g + recv_ref[k].astype(F32)
        delta, m2, v2 = _adamw_math(w_ref[...], g, m_ref[...], v_ref[...])
        g_ref[...] = g
        d_ref[...] = delta
        m2_ref[...] = m2
        v2_ref[...] = v2

    spec = pl.BlockSpec((tr, c), lambda i: (i, 0))
    return pl.pallas_call(
        body,
        name=name,
        grid=(r // tr,),
        in_specs=[spec, pl.BlockSpec((3, tr, c), lambda i: (0, i, 0)), spec, spec, spec],
        out_specs=[spec] * 4,
        out_shape=[_sds((r, c), F32)] * 4,
        compiler_params=_params(("parallel",)),
    )(own, recv, w, m, v)


def _adamw_small(gathered, w, m, v):
    _, r, _ = gathered.shape

    def body(ga_ref, w_ref, m_ref, v_ref, g_ref, d_ref, m2_ref, v2_ref):
        g = ga_ref[0]
        for k in range(1, N_DEV):
            g = g + ga_ref[k]
        delta, m2, v2 = _adamw_math(w_ref[...], g, m_ref[...], v_ref[...])
        g_ref[...] = g
        d_ref[...] = delta
        m2_ref[...] = m2
        v2_ref[...] = v2

    spec = pl.BlockSpec((r, LANES), lambda i: (0, 0))
    return pl.pallas_call(
        body,
        name="adamw_small",
        grid=(1,),
        in_specs=[pl.BlockSpec((N_DEV, r, LANES), lambda i: (0, 0, 0)), spec, spec, spec],
        out_specs=[spec] * 4,
        out_shape=[_sds((r, LANES), F32)] * 4,
        compiler_params=_params(("arbitrary",)),
    )(gathered, w, m, v)


_SMALL = (("w_sgu", (1, SGU_G, SGU_LEN, SGU_LEN)), ("b_sgu", (1, SGU_G, SGU_LEN)), ("norm2_g", (1, D_MODEL)),
          ("normf_g", (D_MODEL,)), ("ln_v_g", (1, SGU_W)), ("ln_v_b", (1, SGU_W)), ("b_f", (1, N_HEADS)),
          ("norm1_g", (1, D_MODEL)))


def _pack_rows(values):
    rows = []
    for val in values:
        flat = val.reshape(-1).astype(F32)
        pad = (-flat.shape[0]) % LANES
        rows.append(jnp.pad(flat, (0, pad)).reshape(-1, LANES))
    packed = jnp.concatenate(rows, axis=0)
    return jnp.pad(packed, ((0, (-packed.shape[0]) % 8), (0, 0)))


def _pack_small(values):
    return jnp.concatenate([_pack_rows(values[:-1]), _pack_rows(values[-1:])], axis=0)


def _unpack_small(packed):
    out, row = [], 0
    for k, (_, shape) in enumerate(_SMALL):
        if k == len(_SMALL) - 1:
            row += (-row) % 8
        size = math.prod(shape)
        n_rows = -(-size // LANES)
        out.append(packed[row:row + n_rows].reshape(-1)[:size].reshape(shape))
        row += n_rows
    return out


def kernel(x, norm1_g, w_in, b_f, ln_v_g, ln_v_b, w_sgu, b_sgu, w_a, w_b, w_o, norm2_g, w_up, w_down, normf_g, loss_target, m_norm1_g, m_w_in, m_b_f, m_ln_v_g, m_ln_v_b, m_w_sgu, m_b_sgu, m_w_a, m_w_b, m_w_o, m_norm2_g, m_w_up, m_w_down, m_normf_g, v_norm1_g, v_w_in, v_b_f, v_ln_v_g, v_ln_v_b, v_w_sgu, v_b_sgu, v_w_a, v_w_b, v_w_o, v_norm2_g, v_w_up, v_w_down, v_normf_g):
    xs = x[0]
    target = loss_target[0]
    s_len, d = xs.shape
    tm = min(512, s_len)
    tl = min(1024, s_len)
    tr = min(256, s_len)
    ta = min(512, s_len)
    tc = min(256, s_len)

    big = (w_in[0], w_a[0], w_b[0], w_o[0], w_up[0], w_down[0])
    (w_in_g,) = _run_comm("gather_w_in", [_gather_plan(w_in[0].astype(BF16))])
    w_in_f = jnp.transpose(w_in_g, (1, 0, 2)).reshape(d, IN_COLS)
    later = big[1:]
    later_flat = jnp.concatenate([w.reshape(-1).astype(BF16) for w in later]).reshape(-1, D_MODEL)
    later_plan = _gather_plan(later_flat)

    def unflatten(gathered):
        row, full = 0, []
        for w, col_sharded in zip(later, (True, True, False, True, False)):
            n_rows = w.size // D_MODEL
            blk = gathered[:, row:row + n_rows].reshape((N_DEV,) + w.shape)
            row += n_rows
            if col_sharded:
                full.append(jnp.transpose(blk, (1, 0, 2)).reshape(w.shape[0], N_DEV * w.shape[1]))
            else:
                full.append(blk.reshape(N_DEV * w.shape[0], w.shape[1]))
        return full

    w_qkv = w_in_f[:, :QKV_W]
    f_lo = QKV_W
    u_lo = f_lo + N_HEADS
    w_rest = jnp.concatenate([w_in_f[:, u_lo:], jnp.pad(w_in_f[:, f_lo:u_lo], ((0, 0), (0, LANES - N_HEADS)))], axis=1)

    chunk_id = jnp.arange(SGU_LEN) // CHUNK
    sgu_mask = chunk_id[None, :] <= chunk_id[:, None]
    w_masked = jnp.where(sgu_mask[None], w_sgu[0], 0.0)
    w_stack = w_masked.reshape(SGU_G // 2, 2 * SGU_LEN, SGU_LEN).astype(BF16)
    wt_stack = jnp.transpose(w_masked, (0, 2, 1)).reshape(SGU_G // 2, 2 * SGU_LEN, SGU_LEN).astype(BF16)
    b_pair = jnp.transpose(jnp.repeat(b_sgu[0], SGU_W // SGU_G, axis=0))
    b_f_pad = jnp.pad(b_f, ((0, 0), (0, LANES - N_HEADS)))
    head_sel = (jnp.arange(FOX_W)[:, None] // HEAD_DIM == jnp.arange(LANES)[None, :]).astype(F32)

    h, h_t, r1 = _rms_fwd("rms1", xs, norm1_g, tm)

    def store(dtype):
        def epi(accs, ex, out):
            out[0][...] = accs[0].astype(dtype)
        return epi

    (qkv,) = _mm("proj_qkv", [(h, w_qkv, False, None)], [], [(_sds((s_len, QKV_W), BF16), _tile(tl, 512))],
                 store(BF16), m=s_len, tm=tl, n=QKV_W, tn=512)
    (prest,) = _mm("proj_rest", [(h, w_rest, False, None)], [], [(_sds((s_len, REST_W), F32), _tile(tl, 640))],
                   store(F32), m=s_len, tm=tl, n=REST_W, tn=640)

    c_col = _forget_cumsum(prest, b_f_pad, tc)
    c_row = jnp.transpose(c_col[:, :N_HEADS]).reshape(N_HEADS, 1, s_len)
    k_t = jnp.transpose(qkv[:, FOX_W:2 * FOX_W])
    v_t = jnp.transpose(qkv[:, 2 * FOX_W:])
    o, o_t, lse_row, (later_g,) = _attn_fwd(qkv, v_t, c_col, c_row, ta, ta, comm=[later_plan])
    w_a_f, w_b_f, w_o_f, w_up_f, w_down_f = unflatten(later_g)
    sg, sg_t = _sgu_fwd(prest, ln_v_g, ln_v_b, w_stack, b_pair, tm)

    def merge_epi(accs, ex, out):
        ya, yb = accs
        sa, sb = _sigmoid(ex[0][...]), _sigmoid(ex[1][...])
        merged = (sa * ya + sb * yb).astype(BF16)
        out[0][...] = merged
        out[1][...] = ya.astype(BF16)
        out[2][...] = yb.astype(BF16)
        out[3][...] = jnp.transpose(merged)

    merged, ya, yb, merged_t = _mm(
        "merge", [(o, w_a_f, False, None), (sg, w_b_f, False, None)],
        [(prest, _tile(tm, d, GA_OFF // d)), (prest, _tile(tm, d, GB_OFF // d))],
        [(_sds((s_len, d), BF16), _tile(tm, d))] * 3 + [(_sds((d, s_len), BF16), _tile_t(tm, d))],
        merge_epi, m=s_len, tm=tm, n=d, tn=d)

    def resid_epi(accs, ex, out):
        out[0][...] = ex[0][...] + accs[0]

    (x1,) = _mm("out_proj", [(merged, w_o_f, False, None)], [(xs, _tile(tm, d))],
                [(_sds((s_len, d), F32), _tile(tm, d))], resid_epi, m=s_len, tm=tm, n=d, tn=d)

    h2, h2_t, r2 = _rms_fwd("rms2", x1, norm2_g, tm)

    def up_epi(accs, ex, out):
        a = accs[0]
        out[0][...] = a.astype(BF16)
        act = jnp.square(jnp.maximum(a, 0.0)).astype(BF16)
        out[1][...] = act
        out[2][...] = jnp.transpose(act)

    a_up, act, act_t = _mm(
        "mlp_up", [(h2, w_up_f, False, None)], [],
        [(_sds((s_len, D_FF), BF16), _tile(tl, 512)), (_sds((s_len, D_FF), BF16), _tile(tl, 512)),
         (_sds((D_FF, s_len), BF16), _tile_t(tl, 512))],
        up_epi, m=s_len, tm=tl, n=D_FF, tn=512)

    def first_step():
        return jnp.logical_and(pl.program_id(0) == 0, pl.program_id(1) == 0)

    def accumulate(ref, val):
        @pl.when(first_step())
        def _():
            ref[...] = val

        @pl.when(jnp.logical_not(first_step()))
        def _():
            ref[...] += val

    def final_epi(accs, ex, out):
        x1_ref, t_ref, g_ref = ex
        x2 = x1_ref[...] + accs[0]
        rf = lax.rsqrt(jnp.mean(x2 * x2, axis=-1, keepdims=True) + EPS)
        xh = x2 * rf
        gf = g_ref[...]
        err = xh * gf - t_ref[...]
        dy = err * (1.0 / d)
        dx2 = _rms_bwd(xh, rf, gf, dy)
        out[0][...] = dx2
        accumulate(out[1], jnp.sum(dy * xh, axis=0, keepdims=True))
        part = 0.5 * jnp.sum(jnp.sum(err * err, axis=-1, keepdims=True) * (1.0 / d), axis=0, keepdims=True)
        accumulate(out[2], jnp.broadcast_to(part, (1, LANES)))
        out[3][...] = dx2.astype(BF16)

    gf2 = normf_g.reshape(1, d)
    dx2, g_normf, loss_part, dx2_16 = _mm(
        "mlp_down_loss", [(act, w_down_f, False, None)],
        [(x1, _row(tr, d)), (target, _row(tr, d)), (gf2, _whole((1, d)))],
        [(_sds((s_len, d), F32), _row(tr, d)), (_sds((1, d), F32), _whole((1, d))), (_sds((1, LANES), F32), _whole((1, LANES))),
         (_sds((s_len, d), BF16), _row(tr, d))],
        final_epi, m=s_len, tm=tr, n=d, tn=d, arbitrary=True)

    def dact_epi(accs, ex, out):
        out[0][...] = (accs[0] * (2.0 * jnp.maximum(ex[0][...], 0.0))).astype(BF16)

    (da,) = _mm("mlp_down_bwd", [(dx2_16, w_down_f, True, None)], [(a_up, _tile(tl, 512))],
                [(_sds((s_len, D_FF), BF16), _tile(tl, 512))], dact_epi, m=s_len, tm=tl, n=D_FF, tn=512)
    g_down = _grad_w("grad_w_down", act_t, dx2_16, tk=1024, tn=d, ts=tl)
    g_up = _grad_w("grad_w_up", h2_t, da, tk=d, tn=1024, ts=tl, block_cols=D_FF // N_DEV)

    def dh2_epi(accs, ex, out):
        x1_ref, r_ref, g_ref, dx2_ref = ex
        r = r_ref[...]
        xh = x1_ref[...] * r
        dh2 = accs[0]
        out[0][...] = dx2_ref[...] + _rms_bwd(xh, r, g_ref[...], dh2)
        accumulate(out[1], jnp.sum(dh2 * xh, axis=0, keepdims=True))

    my_c = lax.axis_index("c")
    my_chip = 2 * lax.axis_index("x") + lax.axis_index("y")
    idx = jnp.stack([my_c, my_chip]).astype(jnp.int32)
    parts16, owns = {}, {}

    def split_cores(g8):
        return g8.reshape((4, 2) + g8.shape[1:])

    def pair_sums(names, grads4, from_sibling):
        for name, g4, recv in zip(names, grads4, from_sibling):
            parts16[name], owns[name] = _pair_sum("grad_pair_sum_" + name, g4, recv, idx, min(512, g4.shape[2]))

    grads4_mlp = [split_cores(g_up), split_cores(g_down.reshape(N_DEV, D_FF // N_DEV, d))]
    (dx1, g_norm2), from_sibling = _mm(
        "mlp_up_bwd", [(da, w_up_f, True, None)],
        [(x1, _row(tr, d)), (r2, _row(tr, 1)), (norm2_g, _whole((1, d))), (dx2, _row(tr, d))],
        [(_sds((s_len, d), F32), _row(tr, d)), (_sds((1, d), F32), _whole((1, d)))],
        dh2_epi, m=s_len, tm=tr, n=d, tn=d, arbitrary=True, comm=[_pair_exchange_plan(grads4_mlp)])
    pair_sums(("w_up", "w_down"), grads4_mlp, from_sibling)

    def dmerge_epi(accs, ex, out):
        dm = accs[0]
        sa, sb = _sigmoid(ex[0][...]), _sigmoid(ex[1][...])
        out[0][...] = (dm * sa).astype(BF16)
        out[1][...] = (dm * sb).astype(BF16)
        out[2][...] = (dm * ex[2][...] * sa * (1.0 - sa)).astype(BF16)
        out[3][...] = (dm * ex[3][...] * sb * (1.0 - sb)).astype(BF16)

    dya, dyb, dga, dgb = _mm(
        "out_proj_bwd", [(dx1, w_o_f, True, None)],
        [(prest, _tile(tm, d, GA_OFF // d)), (prest, _tile(tm, d, GB_OFF // d)), (ya, _tile(tm, d)), (yb, _tile(tm, d))],
        [(_sds((s_len, d), BF16), _tile(tm, d))] * 4, dmerge_epi, m=s_len, tm=tm, n=d, tn=d)
    g_o = _grad_w("grad_w_o", merged_t, dx1, tk=d, tn=d, ts=tl).reshape(N_DEV, d // N_DEV, d)
    def col_blocks(g):
        return jnp.transpose(g.reshape(g.shape[0], N_DEV, g.shape[1] // N_DEV), (1, 0, 2))

    g_a = col_blocks(_grad_w("grad_w_a", o_t, dya, tk=FOX_W, tn=d, ts=tl))
    g_b = col_blocks(_grad_w("grad_w_b", sg_t, dyb, tk=SGU_W, tn=d, ts=tl))

    def do_epi(accs, ex, out):
        do = accs[0]
        out[0][...] = do
        out[1][...] = _dot_f32(do * ex[0][...], ex[1][...])

    grads4_mix = [split_cores(g) for g in (g_a, g_b, g_o)]
    (do, delta), from_sibling = _mm(
        "attn_out_bwd", [(dya, w_a_f, True, None)], [(o, _row(tm, FOX_W)), (head_sel, _whole((FOX_W, LANES)))],
        [(_sds((s_len, FOX_W), F32), _row(tm, FOX_W)), (_sds((s_len, LANES), F32), _row(tm, LANES))],
        do_epi, m=s_len, tm=tm, n=FOX_W, tn=FOX_W, comm=[_pair_exchange_plan(grads4_mix)])
    pair_sums(("w_a", "w_b", "w_o"), grads4_mix, from_sibling)
    (dsg,) = _mm("sgu_out_bwd", [(dyb, w_b_f, True, None)], [], [(_sds((s_len, SGU_W), F32), _tile(tm, SGU_W))],
                 store(F32), m=s_len, tm=tm, n=SGU_W, tn=SGU_W)

    du, dsv, dw_pairs, db_pos, g_ln_g, g_ln_b = _sgu_bwd(prest, dsg, ln_v_g, ln_v_b, w_stack, wt_stack, b_pair, tm)
    g_w_sgu = jnp.where(sgu_mask[None], dw_pairs.reshape(SGU_G, SGU_LEN, SGU_LEN), 0.0)
    g_b_sgu = jnp.transpose(jnp.sum(db_pos.reshape(SGU_LEN, SGU_G, SGU_W // SGU_G), axis=-1))

    delta_row = jnp.transpose(delta[:, :N_HEADS]).reshape(N_HEADS, 1, s_len)
    early = ("w_a", "w_b", "w_o", "w_up", "w_down")
    (dq, dk, dv, dc_rows_blk, dc_cols), from_chips_early = _attn_bwd(
        qkv, k_t, do, c_col, c_row, lse_row, delta_row, ta, ta, comm=[_chip_exchange_plan([parts16[n] for n in early])])
    dc_rows = jnp.transpose(dc_rows_blk.reshape(s_len // ta, N_HEADS, ta), (0, 2, 1)).reshape(s_len, N_HEADS)
    dc_rows = jnp.pad(dc_rows, ((0, 0), (0, LANES - N_HEADS)))
    dfl, g_bf = _forget_bwd(dc_rows, dc_cols, prest, b_f_pad, tc)

    dp = (dq, dk, dv, du, dsv, dga, dgb)
    g_in_cols = [_grad_w("grad_w_in_%d" % k, h_t, seg, tk=d, tn=seg.shape[1], ts=tl) for k, seg in enumerate(dp)]
    g_f = _grad_w("grad_w_in_f", h_t, dfl, tk=d, tn=LANES, ts=tl)[:, :N_HEADS]
    g_in_full = jnp.concatenate(g_in_cols[:3] + [g_f] + g_in_cols[3:], axis=1)
    g_in = jnp.transpose(g_in_full.reshape(d, N_DEV, IN_SHARD), (1, 0, 2))

    def dx_epi(accs, ex, out):
        x_ref, r_ref, g_ref, dx1_ref = ex
        dh = accs[0]
        for extra in accs[1:]:
            dh = dh + extra
        r = r_ref[...]
        xh = x_ref[...] * r
        out[0][...] = dx1_ref[...] + _rms_bwd(xh, r, g_ref[...], dh)
        accumulate(out[1], jnp.sum(dh * xh, axis=0, keepdims=True))

    rest_cols = ((du, U_OFF, 512), (dsv, SV_OFF, 512), (dga, GA_OFF, 1024), (dgb, GB_OFF, 1024), (dfl, F_OFF, LANES))
    dx_pairs = [(seg, w_qkv, True, (512 * k, 512 * (k + 1))) for k, seg in enumerate((dq, dk, dv))]
    dx_pairs += [(seg, w_rest, True, (lo, lo + width)) for seg, lo, width in rest_cols]
    grads4_in = [split_cores(g_in)]
    pair_sums(("w_in",), grads4_in, _run_comm("grad_pair_exchange_w_in", [_pair_exchange_plan(grads4_in)]))
    small_g = _pack_rows((g_w_sgu, g_b_sgu, g_norm2, g_normf, g_ln_g, g_ln_b, g_bf[:, :N_HEADS]))
    (grad_x, g_norm1), (from_chips_in, small_all) = _mm(
        "proj_bwd", dx_pairs,
        [(xs, _row(tr, d)), (r1, _row(tr, 1)), (norm1_g, _whole((1, d))), (dx1, _row(tr, d))],
        [(_sds((s_len, d), F32), _row(tr, d)), (_sds((1, d), F32), _whole((1, d)))],
        dx_epi, m=s_len, tm=tr, n=d, tn=d, arbitrary=True,
        comm=[_chip_exchange_plan([parts16["w_in"]]), _gather_plan(small_g)])
    (norm1_all,) = _run_comm("gather_grad_norm1", [_gather_plan(_pack_rows((g_norm1,)))])
    small_all = jnp.concatenate([small_all, norm1_all], axis=1)
    from_chips = dict(zip(early, from_chips_early), w_in=from_chips_in)

    names = ("w_in", "w_a", "w_b", "w_o", "w_up", "w_down")
    moments_m = (m_w_in, m_w_a, m_w_b, m_w_o, m_w_up, m_w_down)
    moments_v = (v_w_in, v_w_a, v_w_b, v_w_o, v_w_up, v_w_down)
    big_out = {}
    for name, w, m, v in zip(names, big, moments_m, moments_v):
        own = owns[name]
        res = _adamw_shard("adamw_" + name, own, from_chips[name], w, m[0], v[0], min(512, own.shape[0]))
        big_out[name] = [t[None] for t in res]

    small_w = _pack_small((w_sgu, b_sgu, norm2_g, normf_g, ln_v_g, ln_v_b, b_f, norm1_g))
    small_m = _pack_small((m_w_sgu, m_b_sgu, m_norm2_g, m_normf_g, m_ln_v_g, m_ln_v_b, m_b_f, m_norm1_g))
    small_v = _pack_small((v_w_sgu, v_b_sgu, v_norm2_g, v_normf_g, v_ln_v_g, v_ln_v_b, v_b_f, v_norm1_g))
    small_res = [_unpack_small(t) for t in _adamw_small(small_all, small_w, small_m, small_v)]
    small_names = [n for n, _ in _SMALL]
    small_out = {n: [res[k] for res in small_res] for k, n in enumerate(small_names)}

    loss = lax.psum(loss_part[0, 0], ("x", "y", "c"))

    order = ("norm1_g", "w_in", "b_f", "ln_v_g", "ln_v_b", "w_sgu", "b_sgu", "w_a", "w_b", "w_o", "norm2_g", "w_up",
             "w_down", "normf_g")
    table = {**big_out, **small_out}
    outs = [loss, grad_x[None]]
    for kind in range(4):
        outs += [table[n][kind] for n in order]
    return tuple(outs)
```

```python
import math

import jax
import jax.numpy as jnp
from jax import lax
from jax.experimental import pallas as pl
from jax.experimental.pallas import tpu as pltpu

F32 = jnp.float32
BF16 = jnp.bfloat16

N_DEV = 8
D_MODEL = 1024
N_HEADS = 8
HEAD_DIM = 64
FOX_W = N_HEADS * HEAD_DIM
SGU_G = 8
SGU_W = 512
SGU_LEN = 128
CHUNK = 64
D_FF = 4 * D_MODEL
IN_COLS = 3 * FOX_W + N_HEADS + 2 * SGU_W + 2 * D_MODEL
IN_SHARD = IN_COLS // N_DEV
LANES = 128
QKV_W = 3 * FOX_W
U_OFF, SV_OFF, GA_OFF, GB_OFF, F_OFF = 0, 512, 1024, 2048, 3072
REST_W = F_OFF + LANES
EPS = 1e-6
NEG = -1e30

ADAM_LR = 0.001
ADAM_B1 = 0.9
ADAM_B2 = 0.999
ADAM_EPS = 1e-08
ADAM_WD = 0.01
ADAM_STEP = 10

VMEM_LIMIT = 56 * 1024 * 1024
MESH = pl.DeviceIdType.MESH


def _params(sem=None):
    return pltpu.CompilerParams(dimension_semantics=sem, vmem_limit_bytes=VMEM_LIMIT)


def _dot(a, b):
    return jnp.dot(a, b, preferred_element_type=F32)


def _dot_nt(a, b):
    return lax.dot_general(a, b, (((1,), (1,)), ((), ())), preferred_element_type=F32)


def _dot_f32(a, b):
    return jnp.dot(a, b, preferred_element_type=F32, precision=lax.Precision.HIGHEST)


def _sigmoid(x):
    return 1.0 / (1.0 + jnp.exp(-x))


def _log_sigmoid(z):
    return jnp.minimum(z, 0.0) - jnp.log(1.0 + jnp.exp(-jnp.abs(z)))


_GELU_K = math.sqrt(2.0 / math.pi)
_GELU_C = 0.044715


def _gelu(x):
    t = jnp.tanh(_GELU_K * (x + _GELU_C * (x * x * x)))
    return 0.5 * x * (1.0 + t)


def _gelu_grad(x):
    x2 = x * x
    t = jnp.tanh(_GELU_K * (x + _GELU_C * (x2 * x)))
    return 0.5 * (1.0 + t) + 0.5 * x * (1.0 - t * t) * (_GELU_K * (1.0 + 3.0 * _GELU_C * x2))


def _rms_bwd(xh, r, g, dy):
    gy = dy * g
    return r * (gy - xh * jnp.mean(xh * gy, axis=-1, keepdims=True))


def _lane_lt64(shape):
    return lax.broadcasted_iota(jnp.int32, shape, len(shape) - 1) < HEAD_DIM


class _Comm:
    def __init__(self, arrays, out_shapes, sems, start, finish, mid=None):
        self.arrays, self.out_shapes, self.sems = list(arrays), list(out_shapes), list(sems)
        self.start, self.mid, self.finish = start, mid, finish


def _comm_phase(plans, phase, in_refs, out_refs, sem_refs):
    ia = io = ks = 0
    for plan in plans:
        na, no, ns = len(plan.arrays), len(plan.out_shapes), len(plan.sems)
        fn = getattr(plan, phase)
        if fn is not None:
            fn(in_refs[ia:ia + na], out_refs[io:io + no], sem_refs[ks:ks + ns])
        ia, io, ks = ia + na, io + no, ks + ns


def _comm_operands(plans):
    arrays = [a for plan in plans for a in plan.arrays]
    out_shapes = [o for plan in plans for o in plan.out_shapes]
    sems = [s for plan in plans for s in plan.sems]
    return arrays, out_shapes, sems


_ANY = pl.BlockSpec(memory_space=pl.ANY)


def _run_comm(name, plans):
    arrays, out_shapes, sems = _comm_operands(plans)
    n_in, n_out = len(arrays), len(out_shapes)

    def body(*refs):
        parts = refs[:n_in], refs[n_in:n_in + n_out], refs[n_in + n_out:]
        for phase in ("start", "mid", "finish"):
            _comm_phase(plans, phase, *parts)

    return pl.pallas_call(
        body, name=name, out_shape=out_shapes, in_specs=[_ANY] * n_in, out_specs=[_ANY] * n_out, scratch_shapes=sems,
    )(*arrays)


def _gather_plan(shard):
    def setup(ins, outs, sems):
        (x_ref,), (out_ref,), (send_sems, recv_sems, local_sem) = ins, outs, sems
        x, y, c = lax.axis_index("x"), lax.axis_index("y"), lax.axis_index("c")
        me, sibling = (x, y, c), (x, y, 1 - c)
        chips = [(1 - x, y), (x, 1 - y), (1 - x, 1 - y)]

        def rows(px, py, pc):
            return out_ref.at[4 * px + 2 * py + pc]

        def copy(k, block, to, src=None):
            return pltpu.make_async_remote_copy(
                src_ref=rows(*block) if src is None else src,
                dst_ref=rows(*block),
                send_sem=send_sems.at[k],
                recv_sem=recv_sems.at[k],
                device_id=to,
                device_id_type=MESH,
            )

        mine = pltpu.make_async_copy(x_ref, rows(*me), local_sem)
        first = [copy(0, me, sibling, src=x_ref)]
        first += [copy(1 + j, me, (*chip, c), src=x_ref) for j, chip in enumerate(chips)]
        passed = [copy(4 + j, (*chip, c), sibling) for j, chip in enumerate(chips)]
        landed = [copy(1 + j, (*chip, c), me) for j, chip in enumerate(chips)]
        from_sibling = [copy(0, sibling, me)] + [copy(4 + j, (*chip, 1 - c), me) for j, chip in enumerate(chips)]
        return mine, first, passed, landed, from_sibling

    def start(ins, outs, sems):
        mine, first, _, _, _ = setup(ins, outs, sems)
        mine.start()
        for cp in first:
            cp.start()

    def mid(ins, outs, sems):
        _, _, passed, landed, _ = setup(ins, outs, sems)
        for arrived, onward in zip(landed, passed):
            arrived.wait_recv()
            onward.start()

    def finish(ins, outs, sems):
        mine, first, passed, _, from_sibling = setup(ins, outs, sems)
        for cp in from_sibling:
            cp.wait_recv()
        for cp in first + passed:
            cp.wait_send()
        mine.wait()

    return _Comm([shard], [jax.ShapeDtypeStruct((N_DEV,) + shard.shape, shard.dtype)],
                 [pltpu.SemaphoreType.DMA((7,)), pltpu.SemaphoreType.DMA((7,)), pltpu.SemaphoreType.DMA],
                 start, finish, mid)


def _start_all(copies):
    for cp in copies:
        cp.start()


def _wait_all(copies):
    for cp in copies:
        cp.wait_recv()
    for cp in copies:
        cp.wait_send()


def _pair_exchange_plan(grads):
    n = len(grads)

    def copies(ins, outs, sems):
        send_sems, recv_sems = sems
        x, y, c = lax.axis_index("x"), lax.axis_index("y"), lax.axis_index("c")
        return [
            pltpu.make_async_remote_copy(
                src_ref=ins[k].at[:, 1 - c],
                dst_ref=outs[k],
                send_sem=send_sems.at[k],
                recv_sem=recv_sems.at[k],
                device_id=(x, y, 1 - c),
                device_id_type=MESH,
            )
            for k in range(n)
        ]

    return _Comm(grads, [jax.ShapeDtypeStruct((4,) + g.shape[2:], g.dtype) for g in grads],
                 [pltpu.SemaphoreType.DMA((n,)), pltpu.SemaphoreType.DMA((n,))],
                 lambda *refs: _start_all(copies(*refs)), lambda *refs: _wait_all(copies(*refs)))


def _chip_exchange_plan(parts):
    n = len(parts)

    def copies(ins, outs, sems):
        send_sems, recv_sems = sems
        x, y, c = lax.axis_index("x"), lax.axis_index("y"), lax.axis_index("c")
        chips = [(1 - x, y), (x, 1 - y), (1 - x, 1 - y)]
        return [
            pltpu.make_async_remote_copy(
                src_ref=ins[k].at[2 * px + py],
                dst_ref=outs[k].at[j],
                send_sem=send_sems.at[3 * k + j],
                recv_sem=recv_sems.at[3 * k + j],
                device_id=(px, py, c),
                device_id_type=MESH,
            )
            for k in range(n) for j, (px, py) in enumerate(chips)
        ]

    return _Comm(parts, [jax.ShapeDtypeStruct((3,) + p.shape[1:], p.dtype) for p in parts],
                 [pltpu.SemaphoreType.DMA((3 * n,)), pltpu.SemaphoreType.DMA((3 * n,))],
                 lambda *refs: _start_all(copies(*refs)), lambda *refs: _wait_all(copies(*refs)))


def _mm(name, pairs, extras, outs, epi, *, m, tm, n, tn, arbitrary=False, comm=()):
    nj = n // tn
    a_arrays, a_specs, b_arrays, b_specs, b_index = [], [], [], [], []
    for a, b, nt, cols in pairs:
        a_arrays.append(a)
        a_specs.append(pl.BlockSpec((tm, a.shape[1]), lambda i, j: (i, 0)))
        known = [k for k, other in enumerate(b_arrays) if other is b]
        if known:
            b_index.append(known[0])
            continue
        b_index.append(len(b_arrays))
        b_arrays.append(b)
        if cols is not None:
            assert nj == 1
            b_specs.append(pl.BlockSpec(b.shape, lambda i, j: (0, 0)))
        elif nt:
            b_specs.append(pl.BlockSpec((tn, b.shape[1]), lambda i, j: (j, 0)))
        else:
            b_specs.append(pl.BlockSpec((b.shape[0], tn), lambda i, j: (0, j)))
    comm_arrays, comm_outs, comm_sems = _comm_operands(comm)
    arrays = a_arrays + b_arrays + [arr for arr, _ in extras] + comm_arrays
    in_specs = a_specs + b_specs + [spec for _, spec in extras] + [_ANY] * len(comm_arrays)
    n_a, n_b, n_extras, n_ci, n_out, n_co = len(a_arrays), len(b_arrays), len(extras), len(comm_arrays), len(outs), len(comm_outs)
    ni = m // tm

    def body(*refs):
        a_refs = refs[:n_a]
        b_refs = refs[n_a:n_a + n_b]
        ex = refs[n_a + n_b:n_a + n_b + n_extras]
        n_in = n_a + n_b + n_extras + n_ci
        comm_refs = refs[n_in - n_ci:n_in], refs[n_in + n_out:n_in + n_out + n_co], refs[n_in + n_out + n_co:]
        out = refs[n_in:n_in + n_out]
        if comm:
            @pl.when(jnp.logical_and(pl.program_id(0) == 0, pl.program_id(1) == 0))
            def _():
                _comm_phase(comm, "start", *comm_refs)

        accs = []
        for p, (_, _, nt, cols) in enumerate(pairs):
            av = a_refs[p][...]
            if av.dtype != BF16:
                av = av.astype(BF16)
            b_ref = b_refs[b_index[p]]
            bv = b_ref[...] if cols is None else b_ref[:, cols[0]:cols[1]]
            accs.append(_dot_nt(av, bv) if nt else _dot(av, bv))
        epi(accs, ex, out)
        if comm:
            mid_row = ni // 2 if ni >= 3 else ni - 1
            mid_col = 0 if ni >= 3 else nj - 1

            @pl.when(jnp.logical_and(pl.program_id(0) == mid_row, pl.program_id(1) == mid_col))
            def _():
                _comm_phase(comm, "mid", *comm_refs)

            @pl.when(jnp.logical_and(pl.program_id(0) == ni - 1, pl.program_id(1) == nj - 1))
            def _():
                _comm_phase(comm, "finish", *comm_refs)

    sem = ("arbitrary", "arbitrary") if arbitrary or comm else ("parallel", "parallel")
    res = pl.pallas_call(
        body,
        name=name,
        grid=(ni, nj),
        in_specs=in_specs,
        out_specs=[spec for _, spec in outs] + [_ANY] * n_co,
        out_shape=[shape for shape, _ in outs] + comm_outs,
        scratch_shapes=comm_sems,
        compiler_params=_params(sem),
    )(*arrays)
    return (res[:n_out], res[n_out:]) if comm else res


def _tile(tm, tn, off=0):
    return pl.BlockSpec((tm, tn), lambda i, j: (i, j + off))


def _row(tm, w, blk=0):
    return pl.BlockSpec((tm, w), lambda i, j: (i, blk))


def _whole(shape):
    zeros = (0,) * len(shape)
    return pl.BlockSpec(shape, lambda i, j: zeros)


def _sds(shape, dtype):
    return jax.ShapeDtypeStruct(shape, dtype)


def _tile_t(tm, tn):
    return pl.BlockSpec((tn, tm), lambda i, j: (j, i))


def _grad_w(name, a_t, g, *, tk, tn, ts, block_cols=None):
    ka, s_len = a_t.shape
    n = g.shape[1]
    width = tn if block_cols is None else block_cols

    def body(a_ref, g_ref, o_ref):
        first = pl.program_id(2) == 0
        gv = g_ref[...].astype(BF16)
        for b in range(tn // width):
            part = _dot(a_ref[...], gv[:, b * width:(b + 1) * width])
            dst = o_ref if block_cols is None else o_ref.at[b]

            @pl.when(first)
            def _():
                dst[...] = part

            @pl.when(jnp.logical_not(first))
            def _():
                dst[...] += part

    if block_cols is None:
        out_shape = _sds((ka, n), F32)
        out_spec = pl.BlockSpec((tk, tn), lambda i, j, s: (i, j))
    else:
        out_shape = _sds((n // width, ka, width), F32)
        out_spec = pl.BlockSpec((tn // width, tk, width), lambda i, j, s: (j, i, 0))
    return pl.pallas_call(
        body,
        name=name,
        grid=(ka // tk, n // tn, s_len // ts),
        in_specs=[pl.BlockSpec((tk, ts), lambda i, j, s: (i, s)), pl.BlockSpec((ts, tn), lambda i, j, s: (s, j))],
        out_specs=out_spec,
        out_shape=out_shape,
        compiler_params=_params(("parallel", "parallel", "arbitrary")),
    )(a_t, g)


def _rms_fwd(name, x, g, tm, comm=()):
    s_len, d = x.shape
    steps = s_len // tm
    comm_arrays, comm_outs, comm_sems = _comm_operands(comm)
    n_ci, n_co = len(comm_arrays), len(comm_outs)

    def body(x_ref, g_ref, *rest):
        comm_refs = rest[:n_ci], rest[n_ci + 3:n_ci + 3 + n_co], rest[n_ci + 3 + n_co:]
        h_ref, ht_ref, r_ref = rest[n_ci:n_ci + 3]
        for phase, at in (("start", 0), ("mid", steps // 2)):
            if comm:
                @pl.when(pl.program_id(0) == at)
                def _():
                    _comm_phase(comm, phase, *comm_refs)

        xv = x_ref[...]
        r = lax.rsqrt(jnp.mean(xv * xv, axis=-1, keepdims=True) + EPS)
        h = (xv * r * g_ref[...]).astype(BF16)
        h_ref[...] = h
        ht_ref[...] = jnp.transpose(h)
        r_ref[...] = r
        if comm:
            @pl.when(pl.program_id(0) == steps - 1)
            def _():
                _comm_phase(comm, "finish", *comm_refs)

    res = pl.pallas_call(
        body,
        name=name,
        grid=(steps,),
        in_specs=[pl.BlockSpec((tm, d), lambda i: (i, 0)), pl.BlockSpec((1, d), lambda i: (0, 0))] + [_ANY] * n_ci,
        out_specs=[pl.BlockSpec((tm, d), lambda i: (i, 0)), pl.BlockSpec((d, tm), lambda i: (0, i)),
                   pl.BlockSpec((tm, 1), lambda i: (i, 0))] + [_ANY] * n_co,
        out_shape=[_sds((s_len, d), BF16), _sds((d, s_len), BF16), _sds((s_len, 1), F32)] + comm_outs,
        scratch_shapes=comm_sems,
        compiler_params=_params(("arbitrary",) if comm else ("parallel",)),
    )(x, g, *comm_arrays)
    return res[0], res[1], res[2], res[3:]


def _forget_cumsum(prest, b_f_pad, tc):
    s_len = prest.shape[0]

    def body(f_ref, b_ref, c_ref, carry):
        @pl.when(pl.program_id(0) == 0)
        def _():
            carry[...] = jnp.zeros_like(carry)

        logf = _log_sigmoid(f_ref[...] + b_ref[...])
        row = lax.broadcasted_iota(jnp.int32, (tc, tc), 0)
        col = lax.broadcasted_iota(jnp.int32, (tc, tc), 1)
        tri = (row >= col).astype(F32)
        c = _dot_f32(tri, logf) + carry[...]
        c_ref[...] = c
        carry[...] = c[tc - 1:tc, :]

    return pl.pallas_call(
        body,
        name="forget_cumsum",
        grid=(s_len // tc,),
        in_specs=[pl.BlockSpec((tc, LANES), lambda i: (i, 0)), pl.BlockSpec((1, LANES), lambda i: (0, 0))],
        out_specs=pl.BlockSpec((tc, LANES), lambda i: (i, 0)),
        out_shape=_sds((s_len, LANES), F32),
        scratch_shapes=[pltpu.VMEM((1, LANES), F32)],
        compiler_params=_params(("arbitrary",)),
    )(prest, b_f_pad)


def _stack_heads(pair, lt64):
    zero = jnp.zeros_like(pair)
    return jnp.concatenate([jnp.where(lt64, pair, zero), jnp.where(lt64, zero, pair)], axis=0)


def _score_tiles(q_ref, k_ref, cq_ref, ck_ref, st_sc, tk):
    lt64 = _lane_lt64((tk, LANES))
    for p in range(N_HEADS // 2):
        lanes = slice(p * LANES, (p + 1) * LANES)
        q_pair = q_ref[:, lanes] * jnp.asarray(HEAD_DIM ** -0.5, BF16)
        st2 = _dot_nt(_stack_heads(k_ref[:, lanes], lt64), q_pair)
        for half in range(2):
            h = 2 * p + half
            st_sc[h] = st2[half * tk:(half + 1) * tk] + (cq_ref[h] - ck_ref[:, h:h + 1])


def _mask_diagonal(st_sc, i, j, tq, tk):
    @pl.when((j + 1) * tk - 1 > i * tq)
    def _():
        key = j * tk + lax.broadcasted_iota(jnp.int32, (tk, tq), 0)
        query = i * tq + lax.broadcasted_iota(jnp.int32, (tk, tq), 1)
        st_sc[...] = jnp.where((query >= key)[None], st_sc[...], NEG)


def _attn_fwd(qkv, v_t, c_col, c_row, tq, tk, comm=()):
    s_len = qkv.shape[0]
    ratio = tq // tk
    steps = [(i, j) for i in range(s_len // tq) for j in range((i + 1) * ratio)]
    i_tab = jnp.asarray([i for i, _ in steps], jnp.int32)
    j_tab = jnp.asarray([j for _, j in steps], jnp.int32)

    comm_arrays, comm_outs, comm_sems = _comm_operands(comm)
    n_ci, n_co = len(comm_arrays), len(comm_outs)

    def body(i_ref, j_ref, q_ref, k_ref, vt_ref, cq_ref, ck_ref, *rest):
        comm_refs = rest[:n_ci], rest[n_ci + 3:n_ci + 3 + n_co], rest[n_ci + 3 + n_co + 5:]
        o_ref, ot_ref, lse_ref = rest[n_ci:n_ci + 3]
        acc_t, m_sc, l_sc, st_sc, p_sc = rest[n_ci + 3 + n_co:n_ci + 3 + n_co + 5]
        n = pl.program_id(0)
        i, j = i_ref[n], j_ref[n]
        for phase, at in (("start", 0), ("mid", (2 * len(steps)) // 3)):
            if comm:
                @pl.when(n == at)
                def _():
                    _comm_phase(comm, phase, *comm_refs)

        @pl.when(j == 0)
        def _():
            acc_t[...] = jnp.zeros_like(acc_t)
            m_sc[...] = jnp.full_like(m_sc, NEG)
            l_sc[...] = jnp.zeros_like(l_sc)

        _score_tiles(q_ref, k_ref, cq_ref, ck_ref, st_sc, tk)
        _mask_diagonal(st_sc, i, j, tq, tk)
        st = st_sc[...]
        m_old = m_sc[...]
        m_new = jnp.maximum(m_old, jnp.max(st, axis=1, keepdims=True))
        alpha = jnp.exp(m_old - m_new)
        pt = jnp.exp(st - m_new)
        l_sc[...] = alpha * l_sc[...] + jnp.sum(pt, axis=1, keepdims=True)
        m_sc[...] = m_new
        p_sc[...] = pt.astype(BF16)
        top = lax.broadcasted_iota(jnp.int32, (LANES, tq), 0) < HEAD_DIM
        for p in range(N_HEADS // 2):
            lanes = slice(p * LANES, (p + 1) * LANES)
            vt_pair = vt_ref[lanes, :]
            pv = jnp.where(top, _dot(vt_pair, p_sc[2 * p]), _dot(vt_pair, p_sc[2 * p + 1]))
            acc_t[lanes, :] = acc_t[lanes, :] * jnp.where(top, alpha[2 * p], alpha[2 * p + 1]) + pv

        @pl.when(j == (i + 1) * ratio - 1)
        def _():
            for p in range(N_HEADS // 2):
                lanes = slice(p * LANES, (p + 1) * LANES)
                l_pair = jnp.where(top, l_sc[2 * p], l_sc[2 * p + 1])
                o_t = acc_t[lanes, :] / l_pair
                o_ref[:, lanes] = jnp.transpose(o_t)
                ot_ref[lanes, :] = o_t.astype(BF16)
            lse_ref[...] = m_sc[...] + jnp.log(l_sc[...])

        if comm:
            @pl.when(n == len(steps) - 1)
            def _():
                _comm_phase(comm, "finish", *comm_refs)

    stat = pltpu.VMEM((N_HEADS, 1, tq), F32)
    res = pl.pallas_call(
        body,
        name="attn_fwd",
        grid_spec=pltpu.PrefetchScalarGridSpec(
            num_scalar_prefetch=2,
            grid=(len(steps),),
            in_specs=[
                pl.BlockSpec((tq, FOX_W), lambda n, it, jt: (it[n], 0)),
                pl.BlockSpec((tk, FOX_W), lambda n, it, jt: (jt[n], 1)),
                pl.BlockSpec((FOX_W, tk), lambda n, it, jt: (0, jt[n])),
                pl.BlockSpec((N_HEADS, 1, tq), lambda n, it, jt: (0, 0, it[n])),
                pl.BlockSpec((tk, LANES), lambda n, it, jt: (jt[n], 0)),
            ] + [_ANY] * n_ci,
            out_specs=[
                pl.BlockSpec((tq, FOX_W), lambda n, it, jt: (it[n], 0)),
                pl.BlockSpec((FOX_W, tq), lambda n, it, jt: (0, it[n])),
                pl.BlockSpec((N_HEADS, 1, tq), lambda n, it, jt: (0, 0, it[n])),
            ] + [_ANY] * n_co,
            scratch_shapes=[pltpu.VMEM((FOX_W, tq), F32), stat, stat, pltpu.VMEM((N_HEADS, tk, tq), F32),
                            pltpu.VMEM((N_HEADS, tk, tq), BF16)] + comm_sems,
        ),
        out_shape=[_sds((s_len, FOX_W), F32), _sds((FOX_W, s_len), BF16), _sds((N_HEADS, 1, s_len), F32)] + comm_outs,
        compiler_params=_params(("arbitrary",)),
    )(i_tab, j_tab, qkv, qkv, v_t, c_row, c_col, *comm_arrays)
    return res[0], res[1], res[2], res[3:]


def _sgu_mix(vn, w_stack, lt64):
    outs = []
    for p in range(SGU_G // 2):
        r = _dot(w_stack[p], vn[:, p * LANES:(p + 1) * LANES])
        outs.append(jnp.where(lt64, r[:SGU_LEN], r[SGU_LEN:]))
    return jnp.concatenate(outs, axis=1)


def _sgu_norm(sv, ln_g, ln_b):
    svg = _gelu(sv)
    xc = svg - jnp.mean(svg, axis=-1, keepdims=True)
    rstd = lax.rsqrt(jnp.mean(xc * xc, axis=-1, keepdims=True) + EPS)
    xhat = xc * rstd
    return xhat, rstd, xhat * ln_g + ln_b


def _sgu_fwd(prest, ln_g, ln_b, w_stack, b_pair, tm):
    s_len = prest.shape[0]

    def body(u_ref, sv_ref, g_ref, b_ref, w_ref, bp_ref, sg_ref, sgt_ref):
        lt64 = _lane_lt64((SGU_LEN, LANES))
        _, _, vn = _sgu_norm(sv_ref[...].astype(F32), g_ref[...], b_ref[...])
        vn = vn.astype(BF16)
        w_stack_v = [w_ref[p] for p in range(SGU_G // 2)]
        for w in range(tm // SGU_LEN):
            win = slice(w * SGU_LEN, (w + 1) * SGU_LEN)
            mixed = _sgu_mix(vn[win], w_stack_v, lt64) + bp_ref[...]
            sg = (_gelu(u_ref[win, :].astype(F32)) * mixed).astype(BF16)
            sg_ref[win, :] = sg
            sgt_ref[:, win] = jnp.transpose(sg)

    return pl.pallas_call(
        body,
        name="sgu_fwd",
        grid=(s_len // tm,),
        in_specs=[
            pl.BlockSpec((tm, SGU_W), lambda i: (i, U_OFF // SGU_W)),
            pl.BlockSpec((tm, SGU_W), lambda i: (i, SV_OFF // SGU_W)),
            pl.BlockSpec((1, SGU_W), lambda i: (0, 0)),
            pl.BlockSpec((1, SGU_W), lambda i: (0, 0)),
            pl.BlockSpec((SGU_G // 2, 2 * SGU_LEN, SGU_LEN), lambda i: (0, 0, 0)),
            pl.BlockSpec((SGU_LEN, SGU_W), lambda i: (0, 0)),
        ],
        out_specs=[pl.BlockSpec((tm, SGU_W), lambda i: (i, 0)), pl.BlockSpec((SGU_W, tm), lambda i: (0, i))],
        out_shape=[_sds((s_len, SGU_W), BF16), _sds((SGU_W, s_len), BF16)],
        compiler_params=_params(("parallel",)),
    )(prest, prest, ln_g, ln_b, w_stack, b_pair)


def _sgu_bwd(prest, dsg, ln_g, ln_b, w_stack, wt_stack, b_pair, tm):
    s_len = prest.shape[0]
    n_pair = SGU_G // 2

    def body(u_ref, sv_ref, dsg_ref, g_ref, b_ref, w_ref, wt_ref, bp_ref,
             du_ref, dsv_ref, dw_ref, db_ref, dg_ref, dbeta_ref, dvn_sc):
        @pl.when(pl.program_id(0) == 0)
        def _():
            dw_ref[...] = jnp.zeros_like(dw_ref)
            db_ref[...] = jnp.zeros_like(db_ref)
            dg_ref[...] = jnp.zeros_like(dg_ref)
            dbeta_ref[...] = jnp.zeros_like(dbeta_ref)

        lt64 = _lane_lt64((SGU_LEN, LANES))
        sv = sv_ref[...].astype(F32)
        xhat, rstd, vn32 = _sgu_norm(sv, g_ref[...], b_ref[...])
        vn = vn32.astype(BF16)
        w_stack_v = [w_ref[p] for p in range(n_pair)]
        db = jnp.zeros((SGU_LEN, SGU_W), F32)
        for w in range(tm // SGU_LEN):
            win = slice(w * SGU_LEN, (w + 1) * SGU_LEN)
            u = u_ref[win, :].astype(F32)
            dsg_w = dsg_ref[win, :]
            mixed = _sgu_mix(vn[win], w_stack_v, lt64) + bp_ref[...]
            du_ref[win, :] = (dsg_w * mixed * _gelu_grad(u)).astype(BF16)
            dmixed = dsg_w * _gelu(u)
            db = db + dmixed
            dm16 = dmixed.astype(BF16)
            for p in range(n_pair):
                lanes = slice(p * LANES, (p + 1) * LANES)
                dmp = dm16[:, lanes]
                r = _dot(wt_ref[p], dmp)
                dvn_sc[win, lanes] = jnp.where(lt64, r[:SGU_LEN], r[SGU_LEN:])
                zero = jnp.zeros_like(dmp)
                dm_ab = jnp.concatenate([jnp.where(lt64, dmp, zero), jnp.where(lt64, zero, dmp)], axis=0)
                dw_ref[p] += _dot_nt(dm_ab, vn[win, lanes])
        db_ref[...] += db
        dvn = dvn_sc[...]
        dg_ref[...] += jnp.sum(dvn * xhat, axis=0, keepdims=True)
        dbeta_ref[...] += jnp.sum(dvn, axis=0, keepdims=True)
        dxh = dvn * g_ref[...]
        dsvg = rstd * (dxh - jnp.mean(dxh, axis=-1, keepdims=True) - xhat * jnp.mean(dxh * xhat, axis=-1, keepdims=True))
        dsv_ref[...] = (dsvg * _gelu_grad(sv)).astype(BF16)

    const2 = lambda i: (0, 0)
    const3 = lambda i: (0, 0, 0)
    return pl.pallas_call(
        body,
        name="sgu_bwd",
        grid=(s_len // tm,),
        in_specs=[
            pl.BlockSpec((tm, SGU_W), lambda i: (i, U_OFF // SGU_W)),
            pl.BlockSpec((tm, SGU_W), lambda i: (i, SV_OFF // SGU_W)),
            pl.BlockSpec((tm, SGU_W), lambda i: (i, 0)),
            pl.BlockSpec((1, SGU_W), const2),
            pl.BlockSpec((1, SGU_W), const2),
            pl.BlockSpec((n_pair, 2 * SGU_LEN, SGU_LEN), const3),
            pl.BlockSpec((n_pair, 2 * SGU_LEN, SGU_LEN), const3),
            pl.BlockSpec((SGU_LEN, SGU_W), const2),
        ],
        out_specs=[
            pl.BlockSpec((tm, SGU_W), lambda i: (i, 0)),
            pl.BlockSpec((tm, SGU_W), lambda i: (i, 0)),
            pl.BlockSpec((n_pair, 2 * SGU_LEN, SGU_LEN), const3),
            pl.BlockSpec((SGU_LEN, SGU_W), const2),
            pl.BlockSpec((1, SGU_W), const2),
            pl.BlockSpec((1, SGU_W), const2),
        ],
        out_shape=[
            _sds((s_len, SGU_W), BF16), _sds((s_len, SGU_W), BF16), _sds((n_pair, 2 * SGU_LEN, SGU_LEN), F32),
            _sds((SGU_LEN, SGU_W), F32), _sds((1, SGU_W), F32), _sds((1, SGU_W), F32),
        ],
        scratch_shapes=[pltpu.VMEM((tm, SGU_W), F32)],
        compiler_params=_params(("arbitrary",)),
    )(prest, prest, dsg, ln_g, ln_b, w_stack, wt_stack, b_pair)


def _attn_bwd(qkv, k_t, do, c_col, c_row, lse_row, delta_row, tq, tk, comm=()):
    s_len = qkv.shape[0]
    nq, nk = s_len // tq, s_len // tk
    ratio = tq // tk
    scale = HEAD_DIM ** -0.5
    steps = [(j, i) for j in range(nk) for i in range(j // ratio, nq)]
    j_tab = jnp.asarray([j for j, _ in steps], jnp.int32)
    i_tab = jnp.asarray([i for _, i in steps], jnp.int32)

    comm_arrays, comm_outs, comm_sems = _comm_operands(comm)
    n_ci, n_co = len(comm_arrays), len(comm_outs)

    def body(j_ref, i_ref, q_ref, k_ref, v_ref, kt_ref, do_ref, cq_ref, ck_ref, lse_ref, dl_ref, *rest):
        comm_refs = rest[:n_ci], rest[n_ci + 5:n_ci + 5 + n_co], rest[n_ci + 5 + n_co + 8:]
        dq_ref, dk_ref, dv_ref, dcr_ref, dcc_ref = rest[n_ci:n_ci + 5]
        dq_t, dk_acc, dv_acc, dcc_acc, st_sc, dpt_sc, p_sc, ds_sc = rest[n_ci + 5 + n_co:n_ci + 5 + n_co + 8]
        n = pl.program_id(0)
        j, i = j_ref[n], i_ref[n]

        @pl.when(n == 0)
        def _():
            _comm_phase(comm, "start", *comm_refs)
            dq_t[...] = jnp.zeros_like(dq_t)
            dcr_ref[...] = jnp.zeros_like(dcr_ref)

        @pl.when(i == j // ratio)
        def _():
            dk_acc[...] = jnp.zeros_like(dk_acc)
            dv_acc[...] = jnp.zeros_like(dv_acc)
            dcc_acc[...] = jnp.zeros_like(dcc_acc)

        lt64 = _lane_lt64((tk, LANES))
        _score_tiles(q_ref, k_ref, cq_ref, ck_ref, st_sc, tk)
        for p in range(N_HEADS // 2):
            lanes = slice(p * LANES, (p + 1) * LANES)
            dpt2 = _dot_nt(_stack_heads(v_ref[:, lanes], lt64), do_ref[:, lanes].astype(BF16))
            dpt_sc[2 * p] = dpt2[:tk]
            dpt_sc[2 * p + 1] = dpt2[tk:]
        _mask_diagonal(st_sc, i, j, tq, tk)

        pt = jnp.exp(st_sc[...] - lse_ref[...])
        dst = pt * (dpt_sc[...] - dl_ref[...])
        p_sc[...] = pt.astype(BF16)
        ds_sc[...] = dst.astype(BF16)
        dcr_ref[i] += jnp.sum(dst, axis=1, keepdims=True)
        col_sums = jnp.sum(dst, axis=2, keepdims=True)
        lane = lax.broadcasted_iota(jnp.int32, (tk, LANES), 1)
        dcc = jnp.zeros((tk, LANES), F32)
        for h in range(N_HEADS):
            dcc = jnp.where(lane == h, -col_sums[h], dcc)
        dcc_acc[...] += dcc

        for p in range(N_HEADS // 2):
            lanes = slice(p * LANES, (p + 1) * LANES)
            q_pair = q_ref[:, lanes] * jnp.asarray(scale, BF16)
            dv2 = _dot(p_sc[2 * p:2 * p + 2].reshape(2 * tk, tq), do_ref[:, lanes].astype(BF16))
            dv_acc[:, lanes] += jnp.where(lt64, dv2[:tk], dv2[tk:])
            dk2 = _dot(ds_sc[2 * p:2 * p + 2].reshape(2 * tk, tq), q_pair)
            dk_acc[:, lanes] += jnp.where(lt64, dk2[:tk], dk2[tk:])
            dq2 = _dot(kt_ref[lanes, :], jnp.concatenate([ds_sc[2 * p], ds_sc[2 * p + 1]], axis=1))
            top = lax.broadcasted_iota(jnp.int32, (LANES, tq), 0) < HEAD_DIM
            dq_t[i, lanes, :] += jnp.where(top, dq2[:, :tq], dq2[:, tq:])

        @pl.when(j == (i + 1) * ratio - 1)
        def _():
            rows = pl.ds(pl.multiple_of(i * tq, tq), tq)
            for p in range(N_HEADS // 2):
                lanes = slice(p * LANES, (p + 1) * LANES)
                dq_ref[rows, lanes] = (jnp.transpose(dq_t[i, lanes, :]) * scale).astype(BF16)

        @pl.when(i == nq - 1)
        def _():
            dk_ref[...] = dk_acc[...].astype(BF16)
            dv_ref[...] = dv_acc[...].astype(BF16)
            dcc_ref[...] = dcc_acc[...]

        if comm:
            @pl.when(n == len(steps) - 1)
            def _():
                _comm_phase(comm, "mid", *comm_refs)
                _comm_phase(comm, "finish", *comm_refs)

    q_map = lambda n, jt, it: (it[n], 0)
    q_stat = lambda n, jt, it: (0, 0, it[n])
    k_map = lambda n, jt, it: (jt[n], 0)
    tile = (N_HEADS, tk, tq)
    res = pl.pallas_call(
        body,
        name="attn_bwd",
        grid_spec=pltpu.PrefetchScalarGridSpec(
            num_scalar_prefetch=2,
            grid=(len(steps),),
            in_specs=[
                pl.BlockSpec((tq, FOX_W), q_map),
                pl.BlockSpec((tk, FOX_W), lambda n, jt, it: (jt[n], 1)),
                pl.BlockSpec((tk, FOX_W), lambda n, jt, it: (jt[n], 2)),
                pl.BlockSpec((FOX_W, tk), lambda n, jt, it: (0, jt[n])),
                pl.BlockSpec((tq, FOX_W), q_map),
                pl.BlockSpec((N_HEADS, 1, tq), q_stat),
                pl.BlockSpec((tk, LANES), k_map),
                pl.BlockSpec((N_HEADS, 1, tq), q_stat),
                pl.BlockSpec((N_HEADS, 1, tq), q_stat),
            ] + [_ANY] * n_ci,
            out_specs=[
                pl.BlockSpec((s_len, FOX_W), lambda n, jt, it: (0, 0)),
                pl.BlockSpec((tk, FOX_W), k_map),
                pl.BlockSpec((tk, FOX_W), k_map),
                pl.BlockSpec((nq, N_HEADS, 1, tq), lambda n, jt, it: (0, 0, 0, 0)),
                pl.BlockSpec((tk, LANES), k_map),
            ] + [_ANY] * n_co,
            scratch_shapes=[pltpu.VMEM((nq, FOX_W, tq), F32), pltpu.VMEM((tk, FOX_W), F32), pltpu.VMEM((tk, FOX_W), F32),
                            pltpu.VMEM((tk, LANES), F32), pltpu.VMEM(tile, F32), pltpu.VMEM(tile, F32),
                            pltpu.VMEM(tile, BF16), pltpu.VMEM(tile, BF16)] + comm_sems,
        ),
        out_shape=[_sds((s_len, FOX_W), BF16), _sds((s_len, FOX_W), BF16), _sds((s_len, FOX_W), BF16),
                   _sds((nq, N_HEADS, 1, tq), F32), _sds((s_len, LANES), F32)] + comm_outs,
        compiler_params=_params(("arbitrary",)),
    )(j_tab, i_tab, qkv, qkv, qkv, k_t, do, c_row, c_col, lse_row, delta_row, *comm_arrays)
    return res[:5], res[5:]


def _forget_bwd(dc_rows, dc_cols, prest, b_f_pad, tc):
    s_len = dc_rows.shape[0]
    nb = s_len // tc

    def body(dcr_ref, dc_ref, f_ref, b_ref, df_ref, db_ref, carry):
        @pl.when(pl.program_id(0) == 0)
        def _():
            carry[...] = jnp.zeros_like(carry)
            db_ref[...] = jnp.zeros_like(db_ref)

        row = lax.broadcasted_iota(jnp.int32, (tc, tc), 0)
        col = lax.broadcasted_iota(jnp.int32, (tc, tc), 1)
        tri = (row <= col).astype(F32)
        dlogf = _dot_f32(tri, dcr_ref[...] + dc_ref[...]) + carry[...]
        carry[...] = dlogf[0:1, :]
        z = f_ref[...] + b_ref[...]
        lane = lax.broadcasted_iota(jnp.int32, (tc, LANES), 1)
        dz = jnp.where(lane < N_HEADS, dlogf * _sigmoid(-z), 0.0)
        df_ref[...] = dz.astype(BF16)
        db_ref[...] += jnp.sum(dz, axis=0, keepdims=True)

    rev = lambda i: (nb - 1 - i, 0)
    return pl.pallas_call(
        body,
        name="forget_bwd",
        grid=(nb,),
        in_specs=[
            pl.BlockSpec((tc, LANES), rev),
            pl.BlockSpec((tc, LANES), rev),
            pl.BlockSpec((tc, LANES), rev),
            pl.BlockSpec((1, LANES), lambda i: (0, 0)),
        ],
        out_specs=[pl.BlockSpec((tc, LANES), rev), pl.BlockSpec((1, LANES), lambda i: (0, 0))],
        out_shape=[_sds((s_len, LANES), BF16), _sds((1, LANES), F32)],
        scratch_shapes=[pltpu.VMEM((1, LANES), F32)],
        compiler_params=_params(("arbitrary",)),
    )(dc_rows, dc_cols, prest, b_f_pad)


def _pair_sum(name, g4, recv, idx, tr):
    _, _, r, c = g4.shape

    def body(idx_ref, g_ref, r_ref, p16_ref, own_ref):
        k = pl.program_id(1)
        s = g_ref[...] + r_ref[...]
        p16_ref[...] = s.astype(BF16)

        @pl.when(k == idx_ref[1])
        def _():
            own_ref[...] = s

    return pl.pallas_call(
        body,
        name=name,
        grid_spec=pltpu.PrefetchScalarGridSpec(
            num_scalar_prefetch=1,
            grid=(r // tr, 4),
            in_specs=[
                pl.BlockSpec((None, None, tr, c), lambda i, k, idx: (k, idx[0], i, 0)),
                pl.BlockSpec((None, tr, c), lambda i, k, idx: (k, i, 0)),
            ],
            out_specs=[
                pl.BlockSpec((None, tr, c), lambda i, k, idx: (k, i, 0)),
                pl.BlockSpec((tr, c), lambda i, k, idx: (i, 0)),
            ],
        ),
        out_shape=[_sds((4, r, c), BF16), _sds((r, c), F32)],
        compiler_params=_params(("parallel", "arbitrary")),
    )(idx, g4, recv)


def _adamw_math(w, g, m, v):
    m2 = ADAM_B1 * m + (1.0 - ADAM_B1) * g
    v2 = ADAM_B2 * v + (1.0 - ADAM_B2) * (g * g)
    m_hat = m2 / (1.0 - ADAM_B1 ** ADAM_STEP)
    v_hat = v2 / (1.0 - ADAM_B2 ** ADAM_STEP)
    delta = -ADAM_LR * (m_hat / (jnp.sqrt(v_hat) + ADAM_EPS) + ADAM_WD * w)
    return delta, m2, v2


def _adamw_shard(name, own, recv, w, m, v, tr):
    r, c = own.shape

    def body(own_ref, recv_ref, w_ref, m_ref, v_ref, g_ref, d_ref, m2_ref, v2_ref):
        g = own_ref[...]
        for k in range(3):
            g = g + recv_ref[k].astype(F32)
        delta, m2, v2 = _adamw_math(w_ref[...], g, m_ref[...], v_ref[...])
        g_ref[...] = g
        d_ref[...] = delta
        m2_ref[...] = m2
        v2_ref[...] = v2

    spec = pl.BlockSpec((tr, c), lambda i: (i, 0))
    return pl.pallas_call(
        body,
        name=name,
        grid=(r // tr,),
        in_specs=[spec, pl.BlockSpec((3, tr, c), lambda i: (0, i, 0)), spec, spec, spec],
        out_specs=[spec] * 4,
        out_shape=[_sds((r, c), F32)] * 4,
        compiler_params=_params(("parallel",)),
    )(own, recv, w, m, v)


def _adamw_small(gathered, w, m, v):
    _, r, _ = gathered.shape

    def body(ga_ref, w_ref, m_ref, v_ref, g_ref, d_ref, m2_ref, v2_ref):
        g = ga_ref[0]
        for k in range(1, N_DEV):
            g = g + ga_ref[k]
        delta, m2, v2 = _adamw_math(w_ref[...], g, m_ref[...], v_ref[...])
        g_ref[...] = g
        d_ref[...] = delta
        m2_ref[...] = m2
        v2_ref[...] = v2

    spec = pl.BlockSpec((r, LANES), lambda i: (0, 0))
    return pl.pallas_call(
        body,
        name="adamw_small",
        grid=(1,),
        in_specs=[pl.BlockSpec((N_DEV, r, LANES), lambda i: (0, 0, 0)), spec, spec, spec],
        out_specs=[spec] * 4,
        out_shape=[_sds((r, LANES), F32)] * 4,
        compiler_params=_params(("arbitrary",)),
    )(gathered, w, m, v)


_SMALL = (("w_sgu", (1, SGU_G, SGU_LEN, SGU_LEN)), ("b_sgu", (1, SGU_G, SGU_LEN)), ("norm2_g", (1, D_MODEL)),
          ("normf_g", (D_MODEL,)), ("ln_v_g", (1, SGU_W)), ("ln_v_b", (1, SGU_W)), ("b_f", (1, N_HEADS)),
          ("norm1_g", (1, D_MODEL)))


def _pack_rows(values):
    rows = []
    for val in values:
        flat = val.reshape(-1).astype(F32)
        pad = (-flat.shape[0]) % LANES
        rows.append(jnp.pad(flat, (0, pad)).reshape(-1, LANES))
    packed = jnp.concatenate(rows, axis=0)
    return jnp.pad(packed, ((0, (-packed.shape[0]) % 8), (0, 0)))


def _pack_small(values):
    return jnp.concatenate([_pack_rows(values[:-1]), _pack_rows(values[-1:])], axis=0)


def _unpack_small(packed):
    out, row = [], 0
    for k, (_, shape) in enumerate(_SMALL):
        if k == len(_SMALL) - 1:
            row += (-row) % 8
        size = math.prod(shape)
        n_rows = -(-size // LANES)
        out.append(packed[row:row + n_rows].reshape(-1)[:size].reshape(shape))
        row += n_rows
    return out


def kernel(x, norm1_g, w_in, b_f, ln_v_g, ln_v_b, w_sgu, b_sgu, w_a, w_b, w_o, norm2_g, w_up, w_down, normf_g, loss_target, m_norm1_g, m_w_in, m_b_f, m_ln_v_g, m_ln_v_b, m_w_sgu, m_b_sgu, m_w_a, m_w_b, m_w_o, m_norm2_g, m_w_up, m_w_down, m_normf_g, v_norm1_g, v_w_in, v_b_f, v_ln_v_g, v_ln_v_b, v_w_sgu, v_b_sgu, v_w_a, v_w_b, v_w_o, v_norm2_g, v_w_up, v_w_down, v_normf_g):
    xs = x[0]
    target = loss_target[0]
    s_len, d = xs.shape
    tm = min(512, s_len)
    tl = min(1024, s_len)
    tr = min(256, s_len)
    ta = min(512, s_len)
    tc = min(256, s_len)

    big = (w_in[0], w_a[0], w_b[0], w_o[0], w_up[0], w_down[0])
    h, h_t, r1, (w_in_g,) = _rms_fwd("rms1", xs, norm1_g, tm, comm=[_gather_plan(w_in[0].astype(BF16))])
    w_in_f = jnp.transpose(w_in_g, (1, 0, 2)).reshape(d, IN_COLS)
    later = big[1:]
    later_flat = jnp.concatenate([w.reshape(-1).astype(BF16) for w in later]).reshape(-1, D_MODEL)
    later_plan = _gather_plan(later_flat)

    def unflatten(gathered):
        row, full = 0, []
        for w, col_sharded in zip(later, (True, True, False, True, False)):
            n_rows = w.size // D_MODEL
            blk = gathered[:, row:row + n_rows].reshape((N_DEV,) + w.shape)
            row += n_rows
            if col_sharded:
                full.append(jnp.transpose(blk, (1, 0, 2)).reshape(w.shape[0], N_DEV * w.shape[1]))
            else:
                full.append(blk.reshape(N_DEV * w.shape[0], w.shape[1]))
        return full

    w_qkv = w_in_f[:, :QKV_W]
    f_lo = QKV_W
    u_lo = f_lo + N_HEADS
    w_rest = jnp.concatenate([w_in_f[:, u_lo:], jnp.pad(w_in_f[:, f_lo:u_lo], ((0, 0), (0, LANES - N_HEADS)))], axis=1)

    chunk_id = jnp.arange(SGU_LEN) // CHUNK
    sgu_mask = chunk_id[None, :] <= chunk_id[:, None]
    w_masked = jnp.where(sgu_mask[None], w_sgu[0], 0.0)
    w_stack = w_masked.reshape(SGU_G // 2, 2 * SGU_LEN, SGU_LEN).astype(BF16)
    wt_stack = jnp.transpose(w_masked, (0, 2, 1)).reshape(SGU_G // 2, 2 * SGU_LEN, SGU_LEN).astype(BF16)
    b_pair = jnp.transpose(jnp.repeat(b_sgu[0], SGU_W // SGU_G, axis=0))
    b_f_pad = jnp.pad(b_f, ((0, 0), (0, LANES - N_HEADS)))
    head_sel = (jnp.arange(FOX_W)[:, None] // HEAD_DIM == jnp.arange(LANES)[None, :]).astype(F32)

    def store(dtype):
        def epi(accs, ex, out):
            out[0][...] = accs[0].astype(dtype)
        return epi

    (qkv,) = _mm("proj_qkv", [(h, w_qkv, False, None)], [], [(_sds((s_len, QKV_W), BF16), _tile(tl, 512))],
                 store(BF16), m=s_len, tm=tl, n=QKV_W, tn=512)
    rest_tn = 640
    f_tile, f_lane = F_OFF // rest_tn, F_OFF % rest_tn

    def rest_epi(accs, ex, out):
        out[0][...] = accs[0].astype(BF16)

        @pl.when(pl.program_id(1) == f_tile)
        def _():
            out[1][...] = accs[0][:, f_lane:f_lane + LANES]

    prest, f_logit = _mm("proj_rest", [(h, w_rest, False, None)], [],
                         [(_sds((s_len, REST_W), BF16), _tile(tl, rest_tn)), (_sds((s_len, LANES), F32), _row(tl, LANES))],
                         rest_epi, m=s_len, tm=tl, n=REST_W, tn=rest_tn, arbitrary=True)

    c_col = _forget_cumsum(f_logit, b_f_pad, tc)
    c_row = jnp.transpose(c_col[:, :N_HEADS]).reshape(N_HEADS, 1, s_len)
    k_t = jnp.transpose(qkv[:, FOX_W:2 * FOX_W])
    v_t = jnp.transpose(qkv[:, 2 * FOX_W:])
    o, o_t, lse_row, (later_g,) = _attn_fwd(qkv, v_t, c_col, c_row, ta, ta, comm=[later_plan])
    w_a_f, w_b_f, w_o_f, w_up_f, w_down_f = unflatten(later_g)
    sg, sg_t = _sgu_fwd(prest, ln_v_g, ln_v_b, w_stack, b_pair, tm)

    def merge_epi(accs, ex, out):
        ya, yb = accs
        sa, sb = _sigmoid(ex[0][...].astype(F32)), _sigmoid(ex[1][...].astype(F32))
        merged = (sa * ya + sb * yb).astype(BF16)
        out[0][...] = merged
        out[1][...] = ya.astype(BF16)
        out[2][...] = yb.astype(BF16)
        out[3][...] = jnp.transpose(merged)

    merged, ya, yb, merged_t = _mm(
        "merge", [(o, w_a_f, False, None), (sg, w_b_f, False, None)],
        [(prest, _tile(tm, d, GA_OFF // d)), (prest, _tile(tm, d, GB_OFF // d))],
        [(_sds((s_len, d), BF16), _tile(tm, d))] * 3 + [(_sds((d, s_len), BF16), _tile_t(tm, d))],
        merge_epi, m=s_len, tm=tm, n=d, tn=d)

    def resid_epi(accs, ex, out):
        out[0][...] = ex[0][...] + accs[0]

    (x1,) = _mm("out_proj", [(merged, w_o_f, False, None)], [(xs, _tile(tm, d))],
                [(_sds((s_len, d), F32), _tile(tm, d))], resid_epi, m=s_len, tm=tm, n=d, tn=d)

    h2, h2_t, r2, _ = _rms_fwd("rms2", x1, norm2_g, tm)

    def up_epi(accs, ex, out):
        a = accs[0]
        out[0][...] = a.astype(BF16)
        act = jnp.square(jnp.maximum(a, 0.0)).astype(BF16)
        out[1][...] = act
        out[2][...] = jnp.transpose(act)

    a_up, act, act_t = _mm(
        "mlp_up", [(h2, w_up_f, False, None)], [],
        [(_sds((s_len, D_FF), BF16), _tile(tl, 512)), (_sds((s_len, D_FF), BF16), _tile(tl, 512)),
         (_sds((D_FF, s_len), BF16), _tile_t(tl, 512))],
        up_epi, m=s_len, tm=tl, n=D_FF, tn=512)

    def first_step():
        return jnp.logical_and(pl.program_id(0) == 0, pl.program_id(1) == 0)

    def accumulate(ref, val):
        @pl.when(first_step())
        def _():
            ref[...] = val

        @pl.when(jnp.logical_not(first_step()))
        def _():
            ref[...] += val

    def final_epi(accs, ex, out):
        x1_ref, t_ref, g_ref = ex
        x2 = x1_ref[...] + accs[0]
        rf = lax.rsqrt(jnp.mean(x2 * x2, axis=-1, keepdims=True) + EPS)
        xh = x2 * rf
        gf = g_ref[...]
        err = xh * gf - t_ref[...]
        dy = err * (1.0 / d)
        dx2 = _rms_bwd(xh, rf, gf, dy)
        out[0][...] = dx2
        accumulate(out[1], jnp.sum(dy * xh, axis=0, keepdims=True))
        part = 0.5 * jnp.sum(jnp.sum(err * err, axis=-1, keepdims=True) * (1.0 / d), axis=0, keepdims=True)
        accumulate(out[2], jnp.broadcast_to(part, (1, LANES)))
        out[3][...] = dx2.astype(BF16)

    gf2 = normf_g.reshape(1, d)
    dx2, g_normf, loss_part, dx2_16 = _mm(
        "mlp_down_loss", [(act, w_down_f, False, None)],
        [(x1, _row(tr, d)), (target, _row(tr, d)), (gf2, _whole((1, d)))],
        [(_sds((s_len, d), F32), _row(tr, d)), (_sds((1, d), F32), _whole((1, d))), (_sds((1, LANES), F32), _whole((1, LANES))),
         (_sds((s_len, d), BF16), _row(tr, d))],
        final_epi, m=s_len, tm=tr, n=d, tn=d, arbitrary=True)

    def dact_epi(accs, ex, out):
        out[0][...] = (accs[0] * (2.0 * jnp.maximum(ex[0][...], 0.0))).astype(BF16)

    (da,) = _mm("mlp_down_bwd", [(dx2_16, w_down_f, True, None)], [(a_up, _tile(tl, 512))],
                [(_sds((s_len, D_FF), BF16), _tile(tl, 512))], dact_epi, m=s_len, tm=tl, n=D_FF, tn=512)
    g_down = _grad_w("grad_w_down", act_t, dx2_16, tk=1024, tn=d, ts=tl)
    g_up = _grad_w("grad_w_up", h2_t, da, tk=d, tn=1024, ts=tl, block_cols=D_FF // N_DEV)

    def dh2_epi(accs, ex, out):
        x1_ref, r_ref, g_ref, dx2_ref = ex
        r = r_ref[...]
        xh = x1_ref[...] * r
        dh2 = accs[0]
        out[0][...] = dx2_ref[...] + _rms_bwd(xh, r, g_ref[...], dh2)
        accumulate(out[1], jnp.sum(dh2 * xh, axis=0, keepdims=True))

    my_c = lax.axis_index("c")
    my_chip = 2 * lax.axis_index("x") + lax.axis_index("y")
    idx = jnp.stack([my_c, my_chip]).astype(jnp.int32)
    parts16, owns = {}, {}

    def split_cores(g8):
        return g8.reshape((4, 2) + g8.shape[1:])

    def pair_sums(names, grads4, from_sibling):
        for name, g4, recv in zip(names, grads4, from_sibling):
            parts16[name], owns[name] = _pair_sum("grad_pair_sum_" + name, g4, recv, idx, min(512, g4.shape[2]))

    grads4_mlp = [split_cores(g_up), split_cores(g_down.reshape(N_DEV, D_FF // N_DEV, d))]
    (dx1, g_norm2), from_sibling = _mm(
        "mlp_up_bwd", [(da, w_up_f, True, None)],
        [(x1, _row(tr, d)), (r2, _row(tr, 1)), (norm2_g, _whole((1, d))), (dx2, _row(tr, d))],
        [(_sds((s_len, d), F32), _row(tr, d)), (_sds((1, d), F32), _whole((1, d)))],
        dh2_epi, m=s_len, tm=tr, n=d, tn=d, arbitrary=True, comm=[_pair_exchange_plan(grads4_mlp)])
    pair_sums(("w_up", "w_down"), grads4_mlp, from_sibling)

    def dmerge_epi(accs, ex, out):
        dm = accs[0]
        sa, sb = _sigmoid(ex[0][...].astype(F32)), _sigmoid(ex[1][...].astype(F32))
        out[0][...] = (dm * sa).astype(BF16)
        out[1][...] = (dm * sb).astype(BF16)
        out[2][...] = (dm * ex[2][...] * sa * (1.0 - sa)).astype(BF16)
        out[3][...] = (dm * ex[3][...] * sb * (1.0 - sb)).astype(BF16)

    dya, dyb, dga, dgb = _mm(
        "out_proj_bwd", [(dx1, w_o_f, True, None)],
        [(prest, _tile(tm, d, GA_OFF // d)), (prest, _tile(tm, d, GB_OFF // d)), (ya, _tile(tm, d)), (yb, _tile(tm, d))],
        [(_sds((s_len, d), BF16), _tile(tm, d))] * 4, dmerge_epi, m=s_len, tm=tm, n=d, tn=d)
    g_o = _grad_w("grad_w_o", merged_t, dx1, tk=d, tn=d, ts=tl).reshape(N_DEV, d // N_DEV, d)
    def col_blocks(g):
        return jnp.transpose(g.reshape(g.shape[0], N_DEV, g.shape[1] // N_DEV), (1, 0, 2))

    g_a = col_blocks(_grad_w("grad_w_a", o_t, dya, tk=FOX_W, tn=d, ts=tl))
    g_b = col_blocks(_grad_w("grad_w_b", sg_t, dyb, tk=SGU_W, tn=d, ts=tl))

    def do_epi(accs, ex, out):
        do = accs[0]
        out[0][...] = do
        out[1][...] = _dot_f32(do * ex[0][...], ex[1][...])

    grads4_mix = [split_cores(g) for g in (g_a, g_b, g_o)]
    (do, delta), from_sibling = _mm(
        "attn_out_bwd", [(dya, w_a_f, True, None)], [(o, _row(tm, FOX_W)), (head_sel, _whole((FOX_W, LANES)))],
        [(_sds((s_len, FOX_W), F32), _row(tm, FOX_W)), (_sds((s_len, LANES), F32), _row(tm, LANES))],
        do_epi, m=s_len, tm=tm, n=FOX_W, tn=FOX_W, comm=[_pair_exchange_plan(grads4_mix)])
    pair_sums(("w_a", "w_b", "w_o"), grads4_mix, from_sibling)
    (dsg,) = _mm("sgu_out_bwd", [(dyb, w_b_f, True, None)], [], [(_sds((s_len, SGU_W), F32), _tile(tm, SGU_W))],
                 store(F32), m=s_len, tm=tm, n=SGU_W, tn=SGU_W)

    du, dsv, dw_pairs, db_pos, g_ln_g, g_ln_b = _sgu_bwd(prest, dsg, ln_v_g, ln_v_b, w_stack, wt_stack, b_pair, tm)
    g_w_sgu = jnp.where(sgu_mask[None], dw_pairs.reshape(SGU_G, SGU_LEN, SGU_LEN), 0.0)
    g_b_sgu = jnp.transpose(jnp.sum(db_pos.reshape(SGU_LEN, SGU_G, SGU_W // SGU_G), axis=-1))

    delta_row = jnp.transpose(delta[:, :N_HEADS]).reshape(N_HEADS, 1, s_len)
    early = ("w_a", "w_b", "w_o", "w_up", "w_down")
    (dq, dk, dv, dc_rows_blk, dc_cols), from_chips_early = _attn_bwd(
        qkv, k_t, do, c_col, c_row, lse_row, delta_row, ta, ta, comm=[_chip_exchange_plan([parts16[n] for n in early])])
    dc_rows = jnp.transpose(dc_rows_blk.reshape(s_len // ta, N_HEADS, ta), (0, 2, 1)).reshape(s_len, N_HEADS)
    dc_rows = jnp.pad(dc_rows, ((0, 0), (0, LANES - N_HEADS)))
    dfl, g_bf = _forget_bwd(dc_rows, dc_cols, f_logit, b_f_pad, tc)

    dp = (dq, dk, dv, du, dsv, dga, dgb)
    g_in_cols = [_grad_w("grad_w_in_%d" % k, h_t, seg, tk=d, tn=seg.shape[1], ts=tl) for k, seg in enumerate(dp)]
    g_f = _grad_w("grad_w_in_f", h_t, dfl, tk=d, tn=LANES, ts=tl)[:, :N_HEADS]
    g_in_full = jnp.concatenate(g_in_cols[:3] + [g_f] + g_in_cols[3:], axis=1)
    g_in = jnp.transpose(g_in_full.reshape(d, N_DEV, IN_SHARD), (1, 0, 2))

    def dx_epi(accs, ex, out):
        x_ref, r_ref, g_ref, dx1_ref = ex
        dh = accs[0]
        for extra in accs[1:]:
            dh = dh + extra
        r = r_ref[...]
        xh = x_ref[...] * r
        out[0][...] = dx1_ref[...] + _rms_bwd(xh, r, g_ref[...], dh)
        accumulate(out[1], jnp.sum(dh * xh, axis=0, keepdims=True))

    rest_cols = ((du, U_OFF, 512), (dsv, SV_OFF, 512), (dga, GA_OFF, 1024), (dgb, GB_OFF, 1024), (dfl, F_OFF, LANES))
    dx_pairs = [(seg, w_qkv, True, (512 * k, 512 * (k + 1))) for k, seg in enumerate((dq, dk, dv))]
    dx_pairs += [(seg, w_rest, True, (lo, lo + width)) for seg, lo, width in rest_cols]
    grads4_in = [split_cores(g_in)]
    pair_sums(("w_in",), grads4_in, _run_comm("grad_pair_exchange_w_in", [_pair_exchange_plan(grads4_in)]))
    small_g = _pack_rows((g_w_sgu, g_b_sgu, g_norm2, g_normf, g_ln_g, g_ln_b, g_bf[:, :N_HEADS]))
    (grad_x, g_norm1), (from_chips_in, small_all) = _mm(
        "proj_bwd", dx_pairs,
        [(xs, _row(tr, d)), (r1, _row(tr, 1)), (norm1_g, _whole((1, d))), (dx1, _row(tr, d))],
        [(_sds((s_len, d), F32), _row(tr, d)), (_sds((1, d), F32), _whole((1, d)))],
        dx_epi, m=s_len, tm=tr, n=d, tn=d, arbitrary=True,
        comm=[_chip_exchange_plan([parts16["w_in"]]), _gather_plan(small_g)])
    (norm1_all,) = _run_comm("gather_grad_norm1", [_gather_plan(_pack_rows((g_norm1,)))])
    small_all = jnp.concatenate([small_all, norm1_all], axis=1)
    from_chips = dict(zip(early, from_chips_early), w_in=from_chips_in)

    names = ("w_in", "w_a", "w_b", "w_o", "w_up", "w_down")
    moments_m = (m_w_in, m_w_a, m_w_b, m_w_o, m_w_up, m_w_down)
    moments_v = (v_w_in, v_w_a, v_w_b, v_w_o, v_w_up, v_w_down)
    big_out = {}
    for name, w, m, v in zip(names, big, moments_m, moments_v):
        own = owns[name]
        res = _adamw_shard("adamw_" + name, own, from_chips[name], w, m[0], v[0], min(512, own.shape[0]))
        big_out[name] = [t[None] for t in res]

    small_w = _pack_small((w_sgu, b_sgu, norm2_g, normf_g, ln_v_g, ln_v_b, b_f, norm1_g))
    small_m = _pack_small((m_w_sgu, m_b_sgu, m_norm2_g, m_normf_g, m_ln_v_g, m_ln_v_b, m_b_f, m_norm1_g))
    small_v = _pack_small((v_w_sgu, v_b_sgu, v_norm2_g, v_normf_g, v_ln_v_g, v_ln_v_b, v_b_f, v_norm1_g))
    small_res = [_unpack_small(t) for t in _adamw_small(small_all, small_w, small_m, small_v)]
    small_names = [n for n, _ in _SMALL]
    small_out = {n: [res[k] for res in small_res] for k, n in enumerate(small_names)}

    loss = lax.psum(loss_part[0, 0], ("x", "y", "c"))

    order = ("norm1_g", "w_in", "b_f", "ln_v_g", "ln_v_b", "w_sgu", "b_sgu", "w_a", "w_b", "w_o", "norm2_g", "w_up",
             "w_down", "normf_g")
    table = {**big_out, **small_out}
    outs = [loss, grad_x[None]]
    for kind in range(4):
        outs += [table[n][kind] for n in order]
    return tuple(outs)
```

```python
import math

import jax
import jax.numpy as jnp
from jax import lax
from jax.experimental import pallas as pl
from jax.experimental.pallas import tpu as pltpu

F32 = jnp.float32
BF16 = jnp.bfloat16

N_DEV = 8
D_MODEL = 1024
N_HEADS = 8
HEAD_DIM = 64
FOX_W = N_HEADS * HEAD_DIM
SGU_G = 8
SGU_W = 512
SGU_LEN = 128
CHUNK = 64
D_FF = 4 * D_MODEL
IN_COLS = 3 * FOX_W + N_HEADS + 2 * SGU_W + 2 * D_MODEL
IN_SHARD = IN_COLS // N_DEV
LANES = 128
QKV_W = 3 * FOX_W
U_OFF, SV_OFF, GA_OFF, GB_OFF, F_OFF = 0, 512, 1024, 2048, 3072
REST_W = F_OFF + LANES
EPS = 1e-6
NEG = -1e30

ADAM_LR = 0.001
ADAM_B1 = 0.9
ADAM_B2 = 0.999
ADAM_EPS = 1e-08
ADAM_WD = 0.01
ADAM_STEP = 10

VMEM_LIMIT = 56 * 1024 * 1024
MESH = pl.DeviceIdType.MESH


def _params(sem=None):
    return pltpu.CompilerParams(dimension_semantics=sem, vmem_limit_bytes=VMEM_LIMIT)


def _dot(a, b):
    return jnp.dot(a, b, preferred_element_type=F32)


def _dot_nt(a, b):
    return lax.dot_general(a, b, (((1,), (1,)), ((), ())), preferred_element_type=F32)


def _dot_f32(a, b):
    return jnp.dot(a, b, preferred_element_type=F32, precision=lax.Precision.HIGHEST)


def _sigmoid(x):
    return 1.0 / (1.0 + jnp.exp(-x))


def _log_sigmoid(z):
    return jnp.minimum(z, 0.0) - jnp.log(1.0 + jnp.exp(-jnp.abs(z)))


_GELU_K = math.sqrt(2.0 / math.pi)
_GELU_C = 0.044715


def _gelu(x):
    t = jnp.tanh(_GELU_K * (x + _GELU_C * (x * x * x)))
    return 0.5 * x * (1.0 + t)


def _gelu_grad(x):
    x2 = x * x
    t = jnp.tanh(_GELU_K * (x + _GELU_C * (x2 * x)))
    return 0.5 * (1.0 + t) + 0.5 * x * (1.0 - t * t) * (_GELU_K * (1.0 + 3.0 * _GELU_C * x2))


def _rms_bwd(xh, r, g, dy):
    gy = dy * g
    return r * (gy - xh * jnp.mean(xh * gy, axis=-1, keepdims=True))


def _lane_lt64(shape):
    return lax.broadcasted_iota(jnp.int32, shape, len(shape) - 1) < HEAD_DIM


class _Comm:
    def __init__(self, arrays, out_shapes, sems, start, finish, mid=None):
        self.arrays, self.out_shapes, self.sems = list(arrays), list(out_shapes), list(sems)
        self.start, self.mid, self.finish = start, mid, finish


def _comm_phase(plans, phase, in_refs, out_refs, sem_refs):
    ia = io = ks = 0
    for plan in plans:
        na, no, ns = len(plan.arrays), len(plan.out_shapes), len(plan.sems)
        fn = getattr(plan, phase)
        if fn is not None:
            fn(in_refs[ia:ia + na], out_refs[io:io + no], sem_refs[ks:ks + ns])
        ia, io, ks = ia + na, io + no, ks + ns


def _comm_operands(plans):
    arrays = [a for plan in plans for a in plan.arrays]
    out_shapes = [o for plan in plans for o in plan.out_shapes]
    sems = [s for plan in plans for s in plan.sems]
    return arrays, out_shapes, sems


_ANY = pl.BlockSpec(memory_space=pl.ANY)


def _run_comm(name, plans):
    arrays, out_shapes, sems = _comm_operands(plans)
    n_in, n_out = len(arrays), len(out_shapes)

    def body(*refs):
        parts = refs[:n_in], refs[n_in:n_in + n_out], refs[n_in + n_out:]
        for phase in ("start", "mid", "finish"):
            _comm_phase(plans, phase, *parts)

    return pl.pallas_call(
        body, name=name, out_shape=out_shapes, in_specs=[_ANY] * n_in, out_specs=[_ANY] * n_out, scratch_shapes=sems,
    )(*arrays)


def _gather_plan(shard):
    def setup(ins, outs, sems):
        (x_ref,), (out_ref,), (send_sems, recv_sems, local_sem) = ins, outs, sems
        x, y, c = lax.axis_index("x"), lax.axis_index("y"), lax.axis_index("c")
        me, sibling = (x, y, c), (x, y, 1 - c)
        chips = [(1 - x, y), (x, 1 - y), (1 - x, 1 - y)]

        def rows(px, py, pc):
            return out_ref.at[4 * px + 2 * py + pc]

        def copy(k, block, to, src=None):
            return pltpu.make_async_remote_copy(
                src_ref=rows(*block) if src is None else src,
                dst_ref=rows(*block),
                send_sem=send_sems.at[k],
                recv_sem=recv_sems.at[k],
                device_id=to,
                device_id_type=MESH,
            )

        mine = pltpu.make_async_copy(x_ref, rows(*me), local_sem)
        first = [copy(0, me, sibling, src=x_ref)]
        first += [copy(1 + j, me, (*chip, c), src=x_ref) for j, chip in enumerate(chips)]
        passed = [copy(4 + j, (*chip, c), sibling) for j, chip in enumerate(chips)]
        landed = [copy(1 + j, (*chip, c), me) for j, chip in enumerate(chips)]
        from_sibling = [copy(0, sibling, me)] + [copy(4 + j, (*chip, 1 - c), me) for j, chip in enumerate(chips)]
        return mine, first, passed, landed, from_sibling

    def start(ins, outs, sems):
        mine, first, _, _, _ = setup(ins, outs, sems)
        mine.start()
        for cp in first:
            cp.start()

    def mid(ins, outs, sems):
        _, _, passed, landed, _ = setup(ins, outs, sems)
        for arrived, onward in zip(landed, passed):
            arrived.wait_recv()
            onward.start()

    def finish(ins, outs, sems):
        mine, first, passed, _, from_sibling = setup(ins, outs, sems)
        for cp in from_sibling:
            cp.wait_recv()
        for cp in first + passed:
            cp.wait_send()
        mine.wait()

    return _Comm([shard], [jax.ShapeDtypeStruct((N_DEV,) + shard.shape, shard.dtype)],
                 [pltpu.SemaphoreType.DMA((7,)), pltpu.SemaphoreType.DMA((7,)), pltpu.SemaphoreType.DMA],
                 start, finish, mid)


def _start_all(copies):
    for cp in copies:
        cp.start()


def _wait_all(copies):
    for cp in copies:
        cp.wait_recv()
    for cp in copies:
        cp.wait_send()


def _pair_exchange_plan(grads):
    n = len(grads)

    def copies(ins, outs, sems):
        send_sems, recv_sems = sems
        x, y, c = lax.axis_index("x"), lax.axis_index("y"), lax.axis_index("c")
        return [
            pltpu.make_async_remote_copy(
                src_ref=ins[k].at[:, 1 - c],
                dst_ref=outs[k],
                send_sem=send_sems.at[k],
                recv_sem=recv_sems.at[k],
                device_id=(x, y, 1 - c),
                device_id_type=MESH,
            )
            for k in range(n)
        ]

    return _Comm(grads, [jax.ShapeDtypeStruct((4,) + g.shape[2:], g.dtype) for g in grads],
                 [pltpu.SemaphoreType.DMA((n,)), pltpu.SemaphoreType.DMA((n,))],
                 lambda *refs: _start_all(copies(*refs)), lambda *refs: _wait_all(copies(*refs)))


def _chip_exchange_plan(parts):
    n = len(parts)

    def copies(ins, outs, sems):
        send_sems, recv_sems = sems
        x, y, c = lax.axis_index("x"), lax.axis_index("y"), lax.axis_index("c")
        chips = [(1 - x, y), (x, 1 - y), (1 - x, 1 - y)]
        return [
            pltpu.make_async_remote_copy(
                src_ref=ins[k].at[2 * px + py],
                dst_ref=outs[k].at[j],
                send_sem=send_sems.at[3 * k + j],
                recv_sem=recv_sems.at[3 * k + j],
                device_id=(px, py, c),
                device_id_type=MESH,
            )
            for k in range(n) for j, (px, py) in enumerate(chips)
        ]

    return _Comm(parts, [jax.ShapeDtypeStruct((3,) + p.shape[1:], p.dtype) for p in parts],
                 [pltpu.SemaphoreType.DMA((3 * n,)), pltpu.SemaphoreType.DMA((3 * n,))],
                 lambda *refs: _start_all(copies(*refs)), lambda *refs: _wait_all(copies(*refs)))


def _mm(name, pairs, extras, outs, epi, *, m, tm, n, tn, arbitrary=False, comm=()):
    nj = n // tn
    a_arrays, a_specs, b_arrays, b_specs, b_index = [], [], [], [], []
    for a, b, nt, cols in pairs:
        a_arrays.append(a)
        a_specs.append(pl.BlockSpec((tm, a.shape[1]), lambda i, j: (i, 0)))
        known = [k for k, other in enumerate(b_arrays) if other is b]
        if known:
            b_index.append(known[0])
            continue
        b_index.append(len(b_arrays))
        b_arrays.append(b)
        if cols is not None:
            assert nj == 1
            b_specs.append(pl.BlockSpec(b.shape, lambda i, j: (0, 0)))
        elif nt:
            b_specs.append(pl.BlockSpec((tn, b.shape[1]), lambda i, j: (j, 0)))
        else:
            b_specs.append(pl.BlockSpec((b.shape[0], tn), lambda i, j: (0, j)))
    comm_arrays, comm_outs, comm_sems = _comm_operands(comm)
    arrays = a_arrays + b_arrays + [arr for arr, _ in extras] + comm_arrays
    in_specs = a_specs + b_specs + [spec for _, spec in extras] + [_ANY] * len(comm_arrays)
    n_a, n_b, n_extras, n_ci, n_out, n_co = len(a_arrays), len(b_arrays), len(extras), len(comm_arrays), len(outs), len(comm_outs)
    ni = m // tm

    def body(*refs):
        a_refs = refs[:n_a]
        b_refs = refs[n_a:n_a + n_b]
        ex = refs[n_a + n_b:n_a + n_b + n_extras]
        n_in = n_a + n_b + n_extras + n_ci
        comm_refs = refs[n_in - n_ci:n_in], refs[n_in + n_out:n_in + n_out + n_co], refs[n_in + n_out + n_co:]
        out = refs[n_in:n_in + n_out]
        if comm:
            @pl.when(jnp.logical_and(pl.program_id(0) == 0, pl.program_id(1) == 0))
            def _():
                _comm_phase(comm, "start", *comm_refs)

        accs = []
        for p, (_, _, nt, cols) in enumerate(pairs):
            av = a_refs[p][...]
            if av.dtype != BF16:
                av = av.astype(BF16)
            b_ref = b_refs[b_index[p]]
            bv = b_ref[...] if cols is None else b_ref[:, cols[0]:cols[1]]
            accs.append(_dot_nt(av, bv) if nt else _dot(av, bv))
        epi(accs, ex, out)
        if comm:
            mid_row = ni // 2 if ni >= 3 else ni - 1
            mid_col = 0 if ni >= 3 else nj - 1

            @pl.when(jnp.logical_and(pl.program_id(0) == mid_row, pl.program_id(1) == mid_col))
            def _():
                _comm_phase(comm, "mid", *comm_refs)

            @pl.when(jnp.logical_and(pl.program_id(0) == ni - 1, pl.program_id(1) == nj - 1))
            def _():
                _comm_phase(comm, "finish", *comm_refs)

    sem = ("arbitrary", "arbitrary") if arbitrary or comm else ("parallel", "parallel")
    res = pl.pallas_call(
        body,
        name=name,
        grid=(ni, nj),
        in_specs=in_specs,
        out_specs=[spec for _, spec in outs] + [_ANY] * n_co,
        out_shape=[shape for shape, _ in outs] + comm_outs,
        scratch_shapes=comm_sems,
        compiler_params=_params(sem),
    )(*arrays)
    return (res[:n_out], res[n_out:]) if comm else res


def _tile(tm, tn, off=0):
    return pl.BlockSpec((tm, tn), lambda i, j: (i, j + off))


def _row(tm, w, blk=0):
    return pl.BlockSpec((tm, w), lambda i, j: (i, blk))


def _whole(shape):
    zeros = (0,) * len(shape)
    return pl.BlockSpec(shape, lambda i, j: zeros)


def _sds(shape, dtype):
    return jax.ShapeDtypeStruct(shape, dtype)


def _tile_t(tm, tn):
    return pl.BlockSpec((tn, tm), lambda i, j: (j, i))


def _grad_w(name, a_t, g, *, tk, tn, ts, block_cols=None):
    ka, s_len = a_t.shape
    n = g.shape[1]
    width = tn if block_cols is None else block_cols

    def body(a_ref, g_ref, o_ref):
        first = pl.program_id(2) == 0
        gv = g_ref[...].astype(BF16)
        for b in range(tn // width):
            part = _dot(a_ref[...], gv[:, b * width:(b + 1) * width])
            dst = o_ref if block_cols is None else o_ref.at[b]

            @pl.when(first)
            def _():
                dst[...] = part

            @pl.when(jnp.logical_not(first))
            def _():
                dst[...] += part

    if block_cols is None:
        out_shape = _sds((ka, n), F32)
        out_spec = pl.BlockSpec((tk, tn), lambda i, j, s: (i, j))
    else:
        out_shape = _sds((n // width, ka, width), F32)
        out_spec = pl.BlockSpec((tn // width, tk, width), lambda i, j, s: (j, i, 0))
    return pl.pallas_call(
        body,
        name=name,
        grid=(ka // tk, n // tn, s_len // ts),
        in_specs=[pl.BlockSpec((tk, ts), lambda i, j, s: (i, s)), pl.BlockSpec((ts, tn), lambda i, j, s: (s, j))],
        out_specs=out_spec,
        out_shape=out_shape,
        compiler_params=_params(("parallel", "parallel", "arbitrary")),
    )(a_t, g)


def _rms_fwd(name, x, g, tm, comm=()):
    s_len, d = x.shape
    steps = s_len // tm
    comm_arrays, comm_outs, comm_sems = _comm_operands(comm)
    n_ci, n_co = len(comm_arrays), len(comm_outs)

    def body(x_ref, g_ref, *rest):
        comm_refs = rest[:n_ci], rest[n_ci + 3:n_ci + 3 + n_co], rest[n_ci + 3 + n_co:]
        h_ref, ht_ref, r_ref = rest[n_ci:n_ci + 3]
        for phase, at in (("start", 0), ("mid", steps // 2)):
            if comm:
                @pl.when(pl.program_id(0) == at)
                def _():
                    _comm_phase(comm, phase, *comm_refs)

        xv = x_ref[...]
        r = lax.rsqrt(jnp.mean(xv * xv, axis=-1, keepdims=True) + EPS)
        h = (xv * r * g_ref[...]).astype(BF16)
        h_ref[...] = h
        ht_ref[...] = jnp.transpose(h)
        r_ref[...] = r
        if comm:
            @pl.when(pl.program_id(0) == steps - 1)
            def _():
                _comm_phase(comm, "finish", *comm_refs)

    res = pl.pallas_call(
        body,
        name=name,
        grid=(steps,),
        in_specs=[pl.BlockSpec((tm, d), lambda i: (i, 0)), pl.BlockSpec((1, d), lambda i: (0, 0))] + [_ANY] * n_ci,
        out_specs=[pl.BlockSpec((tm, d), lambda i: (i, 0)), pl.BlockSpec((d, tm), lambda i: (0, i)),
                   pl.BlockSpec((tm, 1), lambda i: (i, 0))] + [_ANY] * n_co,
        out_shape=[_sds((s_len, d), BF16), _sds((d, s_len), BF16), _sds((s_len, 1), F32)] + comm_outs,
        scratch_shapes=comm_sems,
        compiler_params=_params(("arbitrary",) if comm else ("parallel",)),
    )(x, g, *comm_arrays)
    return res[0], res[1], res[2], res[3:]


def _forget_cumsum(prest, b_f_pad, tc):
    s_len = prest.shape[0]

    def body(f_ref, b_ref, c_ref, carry):
        @pl.when(pl.program_id(0) == 0)
        def _():
            carry[...] = jnp.zeros_like(carry)

        logf = _log_sigmoid(f_ref[...] + b_ref[...])
        row = lax.broadcasted_iota(jnp.int32, (tc, tc), 0)
        col = lax.broadcasted_iota(jnp.int32, (tc, tc), 1)
        tri = (row >= col).astype(F32)
        c = _dot_f32(tri, logf) + carry[...]
        c_ref[...] = c
        carry[...] = c[tc - 1:tc, :]

    return pl.pallas_call(
        body,
        name="forget_cumsum",
        grid=(s_len // tc,),
        in_specs=[pl.BlockSpec((tc, LANES), lambda i: (i, 0)), pl.BlockSpec((1, LANES), lambda i: (0, 0))],
        out_specs=pl.BlockSpec((tc, LANES), lambda i: (i, 0)),
        out_shape=_sds((s_len, LANES), F32),
        scratch_shapes=[pltpu.VMEM((1, LANES), F32)],
        compiler_params=_params(("arbitrary",)),
    )(prest, b_f_pad)


def _stack_heads(pair, lt64):
    zero = jnp.zeros_like(pair)
    return jnp.concatenate([jnp.where(lt64, pair, zero), jnp.where(lt64, zero, pair)], axis=0)


def _score_tiles(q_ref, k_ref, cq_ref, ck_ref, st_sc, tk):
    lt64 = _lane_lt64((tk, LANES))
    for p in range(N_HEADS // 2):
        lanes = slice(p * LANES, (p + 1) * LANES)
        q_pair = q_ref[:, lanes] * jnp.asarray(HEAD_DIM ** -0.5, BF16)
        st2 = _dot_nt(_stack_heads(k_ref[:, lanes], lt64), q_pair)
        for half in range(2):
            h = 2 * p + half
            st_sc[h] = st2[half * tk:(half + 1) * tk] + (cq_ref[h] - ck_ref[:, h:h + 1])


def _mask_diagonal(st_sc, i, j, tq, tk):
    @pl.when((j + 1) * tk - 1 > i * tq)
    def _():
        key = j * tk + lax.broadcasted_iota(jnp.int32, (tk, tq), 0)
        query = i * tq + lax.broadcasted_iota(jnp.int32, (tk, tq), 1)
        st_sc[...] = jnp.where((query >= key)[None], st_sc[...], NEG)


def _attn_fwd(qkv, v_t, c_col, c_row, tq, tk, comm=()):
    s_len = qkv.shape[0]
    ratio = tq // tk
    steps = [(i, j) for i in range(s_len // tq) for j in range((i + 1) * ratio)]
    i_tab = jnp.asarray([i for i, _ in steps], jnp.int32)
    j_tab = jnp.asarray([j for _, j in steps], jnp.int32)

    comm_arrays, comm_outs, comm_sems = _comm_operands(comm)
    n_ci, n_co = len(comm_arrays), len(comm_outs)

    def body(i_ref, j_ref, q_ref, k_ref, vt_ref, cq_ref, ck_ref, *rest):
        comm_refs = rest[:n_ci], rest[n_ci + 3:n_ci + 3 + n_co], rest[n_ci + 3 + n_co + 5:]
        o_ref, ot_ref, lse_ref = rest[n_ci:n_ci + 3]
        acc_t, m_sc, l_sc, st_sc, p_sc = rest[n_ci + 3 + n_co:n_ci + 3 + n_co + 5]
        n = pl.program_id(0)
        i, j = i_ref[n], j_ref[n]
        for phase, at in (("start", 0), ("mid", (2 * len(steps)) // 3)):
            if comm:
                @pl.when(n == at)
                def _():
                    _comm_phase(comm, phase, *comm_refs)

        @pl.when(j == 0)
        def _():
            acc_t[...] = jnp.zeros_like(acc_t)
            m_sc[...] = jnp.full_like(m_sc, NEG)
            l_sc[...] = jnp.zeros_like(l_sc)

        _score_tiles(q_ref, k_ref, cq_ref, ck_ref, st_sc, tk)
        _mask_diagonal(st_sc, i, j, tq, tk)
        st = st_sc[...]
        m_old = m_sc[...]
        m_new = jnp.maximum(m_old, jnp.max(st, axis=1, keepdims=True))
        alpha = jnp.exp(m_old - m_new)
        pt = jnp.exp(st - m_new)
        l_sc[...] = alpha * l_sc[...] + jnp.sum(pt, axis=1, keepdims=True)
        m_sc[...] = m_new
        p_sc[...] = pt.astype(BF16)
        top = lax.broadcasted_iota(jnp.int32, (LANES, tq), 0) < HEAD_DIM
        for p in range(N_HEADS // 2):
            lanes = slice(p * LANES, (p + 1) * LANES)
            vt_pair = vt_ref[lanes, :]
            pv = jnp.where(top, _dot(vt_pair, p_sc[2 * p]), _dot(vt_pair, p_sc[2 * p + 1]))
            acc_t[lanes, :] = acc_t[lanes, :] * jnp.where(top, alpha[2 * p], alpha[2 * p + 1]) + pv

        @pl.when(j == (i + 1) * ratio - 1)
        def _():
            for p in range(N_HEADS // 2):
                lanes = slice(p * LANES, (p + 1) * LANES)
                l_pair = jnp.where(top, l_sc[2 * p], l_sc[2 * p + 1])
                o_t = acc_t[lanes, :] / l_pair
                o_ref[:, lanes] = jnp.transpose(o_t)
                ot_ref[lanes, :] = o_t.astype(BF16)
            lse_ref[...] = m_sc[...] + jnp.log(l_sc[...])

        if comm:
            @pl.when(n == len(steps) - 1)
            def _():
                _comm_phase(comm, "finish", *comm_refs)

    stat = pltpu.VMEM((N_HEADS, 1, tq), F32)
    res = pl.pallas_call(
        body,
        name="attn_fwd",
        grid_spec=pltpu.PrefetchScalarGridSpec(
            num_scalar_prefetch=2,
            grid=(len(steps),),
            in_specs=[
                pl.BlockSpec((tq, FOX_W), lambda n, it, jt: (it[n], 0)),
                pl.BlockSpec((tk, FOX_W), lambda n, it, jt: (jt[n], 1)),
                pl.BlockSpec((FOX_W, tk), lambda n, it, jt: (0, jt[n])),
                pl.BlockSpec((N_HEADS, 1, tq), lambda n, it, jt: (0, 0, it[n])),
                pl.BlockSpec((tk, LANES), lambda n, it, jt: (jt[n], 0)),
            ] + [_ANY] * n_ci,
            out_specs=[
                pl.BlockSpec((tq, FOX_W), lambda n, it, jt: (it[n], 0)),
                pl.BlockSpec((FOX_W, tq), lambda n, it, jt: (0, it[n])),
                pl.BlockSpec((N_HEADS, 1, tq), lambda n, it, jt: (0, 0, it[n])),
            ] + [_ANY] * n_co,
            scratch_shapes=[pltpu.VMEM((FOX_W, tq), F32), stat, stat, pltpu.VMEM((N_HEADS, tk, tq), F32),
                            pltpu.VMEM((N_HEADS, tk, tq), BF16)] + comm_sems,
        ),
        out_shape=[_sds((s_len, FOX_W), F32), _sds((FOX_W, s_len), BF16), _sds((N_HEADS, 1, s_len), F32)] + comm_outs,
        compiler_params=_params(("arbitrary",)),
    )(i_tab, j_tab, qkv, qkv, v_t, c_row, c_col, *comm_arrays)
    return res[0], res[1], res[2], res[3:]


def _sgu_mix(vn, w_stack, lt64):
    outs = []
    for p in range(SGU_G // 2):
        r = _dot(w_stack[p], vn[:, p * LANES:(p + 1) * LANES])
        outs.append(jnp.where(lt64, r[:SGU_LEN], r[SGU_LEN:]))
    return jnp.concatenate(outs, axis=1)


def _sgu_norm(sv, ln_g, ln_b):
    svg = _gelu(sv)
    xc = svg - jnp.mean(svg, axis=-1, keepdims=True)
    rstd = lax.rsqrt(jnp.mean(xc * xc, axis=-1, keepdims=True) + EPS)
    xhat = xc * rstd
    return xhat, rstd, xhat * ln_g + ln_b


def _sgu_fwd(prest, ln_g, ln_b, w_stack, b_pair, tm):
    s_len = prest.shape[0]

    def body(u_ref, sv_ref, g_ref, b_ref, w_ref, bp_ref, sg_ref, sgt_ref):
        lt64 = _lane_lt64((SGU_LEN, LANES))
        _, _, vn = _sgu_norm(sv_ref[...].astype(F32), g_ref[...], b_ref[...])
        vn = vn.astype(BF16)
        w_stack_v = [w_ref[p] for p in range(SGU_G // 2)]
        for w in range(tm // SGU_LEN):
            win = slice(w * SGU_LEN, (w + 1) * SGU_LEN)
            mixed = _sgu_mix(vn[win], w_stack_v, lt64) + bp_ref[...]
            sg = (_gelu(u_ref[win, :].astype(F32)) * mixed).astype(BF16)
            sg_ref[win, :] = sg
            sgt_ref[:, win] = jnp.transpose(sg)

    return pl.pallas_call(
        body,
        name="sgu_fwd",
        grid=(s_len // tm,),
        in_specs=[
            pl.BlockSpec((tm, SGU_W), lambda i: (i, U_OFF // SGU_W)),
            pl.BlockSpec((tm, SGU_W), lambda i: (i, SV_OFF // SGU_W)),
            pl.BlockSpec((1, SGU_W), lambda i: (0, 0)),
            pl.BlockSpec((1, SGU_W), lambda i: (0, 0)),
            pl.BlockSpec((SGU_G // 2, 2 * SGU_LEN, SGU_LEN), lambda i: (0, 0, 0)),
            pl.BlockSpec((SGU_LEN, SGU_W), lambda i: (0, 0)),
        ],
        out_specs=[pl.BlockSpec((tm, SGU_W), lambda i: (i, 0)), pl.BlockSpec((SGU_W, tm), lambda i: (0, i))],
        out_shape=[_sds((s_len, SGU_W), BF16), _sds((SGU_W, s_len), BF16)],
        compiler_params=_params(("parallel",)),
    )(prest, prest, ln_g, ln_b, w_stack, b_pair)


def _sgu_bwd(prest, dsg, ln_g, ln_b, w_stack, wt_stack, b_pair, tm):
    s_len = prest.shape[0]
    n_pair = SGU_G // 2

    def body(u_ref, sv_ref, dsg_ref, g_ref, b_ref, w_ref, wt_ref, bp_ref,
             du_ref, dsv_ref, dw_ref, db_ref, dg_ref, dbeta_ref, dvn_sc):
        @pl.when(pl.program_id(0) == 0)
        def _():
            dw_ref[...] = jnp.zeros_like(dw_ref)
            db_ref[...] = jnp.zeros_like(db_ref)
            dg_ref[...] = jnp.zeros_like(dg_ref)
            dbeta_ref[...] = jnp.zeros_like(dbeta_ref)

        lt64 = _lane_lt64((SGU_LEN, LANES))
        sv = sv_ref[...].astype(F32)
        xhat, rstd, vn32 = _sgu_norm(sv, g_ref[...], b_ref[...])
        vn = vn32.astype(BF16)
        w_stack_v = [w_ref[p] for p in range(n_pair)]
        db = jnp.zeros((SGU_LEN, SGU_W), F32)
        for w in range(tm // SGU_LEN):
            win = slice(w * SGU_LEN, (w + 1) * SGU_LEN)
            u = u_ref[win, :].astype(F32)
            dsg_w = dsg_ref[win, :]
            mixed = _sgu_mix(vn[win], w_stack_v, lt64) + bp_ref[...]
            du_ref[win, :] = (dsg_w * mixed * _gelu_grad(u)).astype(BF16)
            dmixed = dsg_w * _gelu(u)
            db = db + dmixed
            dm16 = dmixed.astype(BF16)
            for p in range(n_pair):
                lanes = slice(p * LANES, (p + 1) * LANES)
                dmp = dm16[:, lanes]
                r = _dot(wt_ref[p], dmp)
                dvn_sc[win, lanes] = jnp.where(lt64, r[:SGU_LEN], r[SGU_LEN:])
                zero = jnp.zeros_like(dmp)
                dm_ab = jnp.concatenate([jnp.where(lt64, dmp, zero), jnp.where(lt64, zero, dmp)], axis=0)
                dw_ref[p] += _dot_nt(dm_ab, vn[win, lanes])
        db_ref[...] += db
        dvn = dvn_sc[...]
        dg_ref[...] += jnp.sum(dvn * xhat, axis=0, keepdims=True)
        dbeta_ref[...] += jnp.sum(dvn, axis=0, keepdims=True)
        dxh = dvn * g_ref[...]
        dsvg = rstd * (dxh - jnp.mean(dxh, axis=-1, keepdims=True) - xhat * jnp.mean(dxh * xhat, axis=-1, keepdims=True))
        dsv_ref[...] = (dsvg * _gelu_grad(sv)).astype(BF16)

    const2 = lambda i: (0, 0)
    const3 = lambda i: (0, 0, 0)
    return pl.pallas_call(
        body,
        name="sgu_bwd",
        grid=(s_len // tm,),
        in_specs=[
            pl.BlockSpec((tm, SGU_W), lambda i: (i, U_OFF // SGU_W)),
            pl.BlockSpec((tm, SGU_W), lambda i: (i, SV_OFF // SGU_W)),
            pl.BlockSpec((tm, SGU_W), lambda i: (i, 0)),
            pl.BlockSpec((1, SGU_W), const2),
            pl.BlockSpec((1, SGU_W), const2),
            pl.BlockSpec((n_pair, 2 * SGU_LEN, SGU_LEN), const3),
            pl.BlockSpec((n_pair, 2 * SGU_LEN, SGU_LEN), const3),
            pl.BlockSpec((SGU_LEN, SGU_W), const2),
        ],
        out_specs=[
            pl.BlockSpec((tm, SGU_W), lambda i: (i, 0)),
            pl.BlockSpec((tm, SGU_W), lambda i: (i, 0)),
            pl.BlockSpec((n_pair, 2 * SGU_LEN, SGU_LEN), const3),
            pl.BlockSpec((SGU_LEN, SGU_W), const2),
            pl.BlockSpec((1, SGU_W), const2),
            pl.BlockSpec((1, SGU_W), const2),
        ],
        out_shape=[
            _sds((s_len, SGU_W), BF16), _sds((s_len, SGU_W), BF16), _sds((n_pair, 2 * SGU_LEN, SGU_LEN), F32),
            _sds((SGU_LEN, SGU_W), F32), _sds((1, SGU_W), F32), _sds((1, SGU_W), F32),
        ],
        scratch_shapes=[pltpu.VMEM((tm, SGU_W), F32)],
        compiler_params=_params(("arbitrary",)),
    )(prest, prest, dsg, ln_g, ln_b, w_stack, wt_stack, b_pair)


def _attn_bwd(qkv, k_t, do, c_col, c_row, lse_row, delta_row, tq, tk, comm=()):
    s_len = qkv.shape[0]
    nq, nk = s_len // tq, s_len // tk
    ratio = tq // tk
    scale = HEAD_DIM ** -0.5
    steps = [(j, i) for j in range(nk) for i in range(j // ratio, nq)]
    j_tab = jnp.asarray([j for j, _ in steps], jnp.int32)
    i_tab = jnp.asarray([i for _, i in steps], jnp.int32)

    comm_arrays, comm_outs, comm_sems = _comm_operands(comm)
    n_ci, n_co = len(comm_arrays), len(comm_outs)

    def body(j_ref, i_ref, q_ref, k_ref, v_ref, kt_ref, do_ref, cq_ref, ck_ref, lse_ref, dl_ref, *rest):
        comm_refs = rest[:n_ci], rest[n_ci + 5:n_ci + 5 + n_co], rest[n_ci + 5 + n_co + 8:]
        dq_ref, dk_ref, dv_ref, dcr_ref, dcc_ref = rest[n_ci:n_ci + 5]
        dq_t, dk_acc, dv_acc, dcc_acc, st_sc, dpt_sc, p_sc, ds_sc = rest[n_ci + 5 + n_co:n_ci + 5 + n_co + 8]
        n = pl.program_id(0)
        j, i = j_ref[n], i_ref[n]

        @pl.when(n == 0)
        def _():
            _comm_phase(comm, "start", *comm_refs)
            dq_t[...] = jnp.zeros_like(dq_t)
            dcr_ref[...] = jnp.zeros_like(dcr_ref)

        @pl.when(i == j // ratio)
        def _():
            dk_acc[...] = jnp.zeros_like(dk_acc)
            dv_acc[...] = jnp.zeros_like(dv_acc)
            dcc_acc[...] = jnp.zeros_like(dcc_acc)

        lt64 = _lane_lt64((tk, LANES))
        _score_tiles(q_ref, k_ref, cq_ref, ck_ref, st_sc, tk)
        for p in range(N_HEADS // 2):
            lanes = slice(p * LANES, (p + 1) * LANES)
            dpt2 = _dot_nt(_stack_heads(v_ref[:, lanes], lt64), do_ref[:, lanes].astype(BF16))
            dpt_sc[2 * p] = dpt2[:tk]
            dpt_sc[2 * p + 1] = dpt2[tk:]
        _mask_diagonal(st_sc, i, j, tq, tk)

        pt = jnp.exp(st_sc[...] - lse_ref[...])
        dst = pt * (dpt_sc[...] - dl_ref[...])
        p_sc[...] = pt.astype(BF16)
        ds_sc[...] = dst.astype(BF16)
        dcr_ref[i] += jnp.sum(dst, axis=1, keepdims=True)
        col_sums = jnp.sum(dst, axis=2, keepdims=True)
        lane = lax.broadcasted_iota(jnp.int32, (tk, LANES), 1)
        dcc = jnp.zeros((tk, LANES), F32)
        for h in range(N_HEADS):
            dcc = jnp.where(lane == h, -col_sums[h], dcc)
        dcc_acc[...] += dcc

        for p in range(N_HEADS // 2):
            lanes = slice(p * LANES, (p + 1) * LANES)
            q_pair = q_ref[:, lanes] * jnp.asarray(scale, BF16)
            dv2 = _dot(p_sc[2 * p:2 * p + 2].reshape(2 * tk, tq), do_ref[:, lanes].astype(BF16))
            dv_acc[:, lanes] += jnp.where(lt64, dv2[:tk], dv2[tk:])
            dk2 = _dot(ds_sc[2 * p:2 * p + 2].reshape(2 * tk, tq), q_pair)
            dk_acc[:, lanes] += jnp.where(lt64, dk2[:tk], dk2[tk:])
            dq2 = _dot(kt_ref[lanes, :], jnp.concatenate([ds_sc[2 * p], ds_sc[2 * p + 1]], axis=1))
            top = lax.broadcasted_iota(jnp.int32, (LANES, tq), 0) < HEAD_DIM
            dq_t[i, lanes, :] += jnp.where(top, dq2[:, :tq], dq2[:, tq:])

        @pl.when(j == (i + 1) * ratio - 1)
        def _():
            rows = pl.ds(pl.multiple_of(i * tq, tq), tq)
            for p in range(N_HEADS // 2):
                lanes = slice(p * LANES, (p + 1) * LANES)
                dq_ref[rows, lanes] = (jnp.transpose(dq_t[i, lanes, :]) * scale).astype(BF16)

        @pl.when(i == nq - 1)
        def _():
            dk_ref[...] = dk_acc[...].astype(BF16)
            dv_ref[...] = dv_acc[...].astype(BF16)
            dcc_ref[...] = dcc_acc[...]

        if comm:
            @pl.when(n == len(steps) // 2)
            def _():
                _comm_phase(comm, "mid", *comm_refs)

            @pl.when(n == len(steps) - 1)
            def _():
                _comm_phase(comm, "finish", *comm_refs)

    q_map = lambda n, jt, it: (it[n], 0)
    q_stat = lambda n, jt, it: (0, 0, it[n])
    k_map = lambda n, jt, it: (jt[n], 0)
    tile = (N_HEADS, tk, tq)
    res = pl.pallas_call(
        body,
        name="attn_bwd",
        grid_spec=pltpu.PrefetchScalarGridSpec(
            num_scalar_prefetch=2,
            grid=(len(steps),),
            in_specs=[
                pl.BlockSpec((tq, FOX_W), q_map),
                pl.BlockSpec((tk, FOX_W), lambda n, jt, it: (jt[n], 1)),
                pl.BlockSpec((tk, FOX_W), lambda n, jt, it: (jt[n], 2)),
                pl.BlockSpec((FOX_W, tk), lambda n, jt, it: (0, jt[n])),
                pl.BlockSpec((tq, FOX_W), q_map),
                pl.BlockSpec((N_HEADS, 1, tq), q_stat),
                pl.BlockSpec((tk, LANES), k_map),
                pl.BlockSpec((N_HEADS, 1, tq), q_stat),
                pl.BlockSpec((N_HEADS, 1, tq), q_stat),
            ] + [_ANY] * n_ci,
            out_specs=[
                pl.BlockSpec((s_len, FOX_W), lambda n, jt, it: (0, 0)),
                pl.BlockSpec((tk, FOX_W), k_map),
                pl.BlockSpec((tk, FOX_W), k_map),
                pl.BlockSpec((nq, N_HEADS, 1, tq), lambda n, jt, it: (0, 0, 0, 0)),
                pl.BlockSpec((tk, LANES), k_map),
            ] + [_ANY] * n_co,
            scratch_shapes=[pltpu.VMEM((nq, FOX_W, tq), F32), pltpu.VMEM((tk, FOX_W), F32), pltpu.VMEM((tk, FOX_W), F32),
                            pltpu.VMEM((tk, LANES), F32), pltpu.VMEM(tile, F32), pltpu.VMEM(tile, F32),
                            pltpu.VMEM(tile, BF16), pltpu.VMEM(tile, BF16)] + comm_sems,
        ),
        out_shape=[_sds((s_len, FOX_W), BF16), _sds((s_len, FOX_W), BF16), _sds((s_len, FOX_W), BF16),
                   _sds((nq, N_HEADS, 1, tq), F32), _sds((s_len, LANES), F32)] + comm_outs,
        compiler_params=_params(("arbitrary",)),
    )(j_tab, i_tab, qkv, qkv, qkv, k_t, do, c_row, c_col, lse_row, delta_row, *comm_arrays)
    return res[:5], res[5:]


def _forget_bwd(dc_rows, dc_cols, prest, b_f_pad, tc):
    s_len = dc_rows.shape[0]
    nb = s_len // tc

    def body(dcr_ref, dc_ref, f_ref, b_ref, df_ref, db_ref, carry):
        @pl.when(pl.program_id(0) == 0)
        def _():
            carry[...] = jnp.zeros_like(carry)
            db_ref[...] = jnp.zeros_like(db_ref)

        row = lax.broadcasted_iota(jnp.int32, (tc, tc), 0)
        col = lax.broadcasted_iota(jnp.int32, (tc, tc), 1)
        tri = (row <= col).astype(F32)
        dlogf = _dot_f32(tri, dcr_ref[...] + dc_ref[...]) + carry[...]
        carry[...] = dlogf[0:1, :]
        z = f_ref[...] + b_ref[...]
        lane = lax.broadcasted_iota(jnp.int32, (tc, LANES), 1)
        dz = jnp.where(lane < N_HEADS, dlogf * _sigmoid(-z), 0.0)
        df_ref[...] = dz.astype(BF16)
        db_ref[...] += jnp.sum(dz, axis=0, keepdims=True)

    rev = lambda i: (nb - 1 - i, 0)
    return pl.pallas_call(
        body,
        name="forget_bwd",
        grid=(nb,),
        in_specs=[
            pl.BlockSpec((tc, LANES), rev),
            pl.BlockSpec((tc, LANES), rev),
            pl.BlockSpec((tc, LANES), rev),
            pl.BlockSpec((1, LANES), lambda i: (0, 0)),
        ],
        out_specs=[pl.BlockSpec((tc, LANES), rev), pl.BlockSpec((1, LANES), lambda i: (0, 0))],
        out_shape=[_sds((s_len, LANES), BF16), _sds((1, LANES), F32)],
        scratch_shapes=[pltpu.VMEM((1, LANES), F32)],
        compiler_params=_params(("arbitrary",)),
    )(dc_rows, dc_cols, prest, b_f_pad)


def _pair_sum(name, g4, recv, idx, tr):
    _, _, r, c = g4.shape

    def body(idx_ref, g_ref, r_ref, p16_ref, own_ref):
        k = pl.program_id(1)
        s = g_ref[...] + r_ref[...]
        p16_ref[...] = s.astype(BF16)

        @pl.when(k == idx_ref[1])
        def _():
            own_ref[...] = s

    return pl.pallas_call(
        body,
        name=name,
        grid_spec=pltpu.PrefetchScalarGridSpec(
            num_scalar_prefetch=1,
            grid=(r // tr, 4),
            in_specs=[
                pl.BlockSpec((None, None, tr, c), lambda i, k, idx: (k, idx[0], i, 0)),
                pl.BlockSpec((None, tr, c), lambda i, k, idx: (k, i, 0)),
            ],
            out_specs=[
                pl.BlockSpec((None, tr, c), lambda i, k, idx: (k, i, 0)),
                pl.BlockSpec((tr, c), lambda i, k, idx: (i, 0)),
            ],
        ),
        out_shape=[_sds((4, r, c), BF16), _sds((r, c), F32)],
        compiler_params=_params(("parallel", "arbitrary")),
    )(idx, g4, recv)


def _adamw_math(w, g, m, v):
    m2 = ADAM_B1 * m + (1.0 - ADAM_B1) * g
    v2 = ADAM_B2 * v + (1.0 - ADAM_B2) * (g * g)
    m_hat = m2 / (1.0 - ADAM_B1 ** ADAM_STEP)
    v_hat = v2 / (1.0 - ADAM_B2 ** ADAM_STEP)
    delta = -ADAM_LR * (m_hat / (jnp.sqrt(v_hat) + ADAM_EPS) + ADAM_WD * w)
    return delta, m2, v2


def _adamw_shard(name, own, recv, w, m, v, tr):
    r, c = own.shape

    def body(own_ref, recv_ref, w_ref, m_ref, v_ref, g_ref, d_ref, m2_ref, v2_ref):
        g = own_ref[...]
        for k in range(3):
            g = g + recv_ref[k].astype(F32)
        delta, m2, v2 = _adamw_math(w_ref[...], g, m_ref[...], v_ref[...])
        g_ref[...] = g
        d_ref[...] = delta
        m2_ref[...] = m2
        v2_ref[...] = v2

    spec = pl.BlockSpec((tr, c), lambda i: (i, 0))
    return pl.pallas_call(
        body,
        name=name,
        grid=(r // tr,),
        in_specs=[spec, pl.BlockSpec((3, tr, c), lambda i: (0, i, 0)), spec, spec, spec],
        out_specs=[spec] * 4,
        out_shape=[_sds((r, c), F32)] * 4,
        compiler_params=_params(("parallel",)),
    )(own, recv, w, m, v)


def _adamw_small(gathered, w, m, v):
    _, r, _ = gathered.shape

    def body(ga_ref, w_ref, m_ref, v_ref, g_ref, d_ref, m2_ref, v2_ref):
        g = ga_ref[0]
        for k in range(1, N_DEV):
            g = g + ga_ref[k]
        delta, m2, v2 = _adamw_math(w_ref[...], g, m_ref[...], v_ref[...])
        g_ref[...] = g
        d_ref[...] = delta
        m2_ref[...] = m2
        v2_ref[...] = v2

    spec = pl.BlockSpec((r, LANES), lambda i: (0, 0))
    return pl.pallas_call(
        body,
        name="adamw_small",
        grid=(1,),
        in_specs=[pl.BlockSpec((N_DEV, r, LANES), lambda i: (0, 0, 0)), spec, spec, spec],
        out_specs=[spec] * 4,
        out_shape=[_sds((r, LANES), F32)] * 4,
        compiler_params=_params(("arbitrary",)),
    )(gathered, w, m, v)


_SMALL = (("w_sgu", (1, SGU_G, SGU_LEN, SGU_LEN)), ("b_sgu", (1, SGU_G, SGU_LEN)), ("norm2_g", (1, D_MODEL)),
          ("normf_g", (D_MODEL,)), ("ln_v_g", (1, SGU_W)), ("ln_v_b", (1, SGU_W)), ("b_f", (1, N_HEADS)),
          ("norm1_g", (1, D_MODEL)), ("loss", ()))
_N_EARLY = 6


def _pack_rows(values):
    rows = []
    for val in values:
        flat = val.reshape(-1).astype(F32)
        pad = (-flat.shape[0]) % LANES
        rows.append(jnp.pad(flat, (0, pad)).reshape(-1, LANES))
    packed = jnp.concatenate(rows, axis=0)
    return jnp.pad(packed, ((0, (-packed.shape[0]) % 8), (0, 0)))


def _pack_small(values):
    return jnp.concatenate([_pack_rows(values[:_N_EARLY]), _pack_rows(values[_N_EARLY:])], axis=0)


def _unpack_small(packed):
    out, row = [], 0
    for k, (_, shape) in enumerate(_SMALL):
        if k == _N_EARLY:
            row += (-row) % 8
        size = math.prod(shape)
        n_rows = -(-size // LANES)
        out.append(packed[row:row + n_rows].reshape(-1)[:size].reshape(shape))
        row += n_rows
    return out


def kernel(x, norm1_g, w_in, b_f, ln_v_g, ln_v_b, w_sgu, b_sgu, w_a, w_b, w_o, norm2_g, w_up, w_down, normf_g, loss_target, m_norm1_g, m_w_in, m_b_f, m_ln_v_g, m_ln_v_b, m_w_sgu, m_b_sgu, m_w_a, m_w_b, m_w_o, m_norm2_g, m_w_up, m_w_down, m_normf_g, v_norm1_g, v_w_in, v_b_f, v_ln_v_g, v_ln_v_b, v_w_sgu, v_b_sgu, v_w_a, v_w_b, v_w_o, v_norm2_g, v_w_up, v_w_down, v_normf_g):
    xs = x[0]
    target = loss_target[0]
    s_len, d = xs.shape
    tm = min(512, s_len)
    tl = min(1024, s_len)
    tr = min(256, s_len)
    ta = min(512, s_len)
    tc = min(256, s_len)

    big = (w_in[0], w_a[0], w_b[0], w_o[0], w_up[0], w_down[0])
    h, h_t, r1, (w_in_g,) = _rms_fwd("rms1", xs, norm1_g, tm, comm=[_gather_plan(w_in[0].astype(BF16))])
    w_in_f = jnp.transpose(w_in_g, (1, 0, 2)).reshape(d, IN_COLS)
    later = big[1:]
    later_flat = jnp.concatenate([w.reshape(-1).astype(BF16) for w in later]).reshape(-1, D_MODEL)
    later_plan = _gather_plan(later_flat)

    def unflatten(gathered):
        row, full = 0, []
        for w, col_sharded in zip(later, (True, True, False, True, False)):
            n_rows = w.size // D_MODEL
            blk = gathered[:, row:row + n_rows].reshape((N_DEV,) + w.shape)
            row += n_rows
            if col_sharded:
                full.append(jnp.transpose(blk, (1, 0, 2)).reshape(w.shape[0], N_DEV * w.shape[1]))
            else:
                full.append(blk.reshape(N_DEV * w.shape[0], w.shape[1]))
        return full

    w_qkv = w_in_f[:, :QKV_W]
    f_lo = QKV_W
    u_lo = f_lo + N_HEADS
    w_rest = jnp.concatenate([w_in_f[:, u_lo:], jnp.pad(w_in_f[:, f_lo:u_lo], ((0, 0), (0, LANES - N_HEADS)))], axis=1)

    chunk_id = jnp.arange(SGU_LEN) // CHUNK
    sgu_mask = chunk_id[None, :] <= chunk_id[:, None]
    w_masked = jnp.where(sgu_mask[None], w_sgu[0], 0.0)
    w_stack = w_masked.reshape(SGU_G // 2, 2 * SGU_LEN, SGU_LEN).astype(BF16)
    wt_stack = jnp.transpose(w_masked, (0, 2, 1)).reshape(SGU_G // 2, 2 * SGU_LEN, SGU_LEN).astype(BF16)
    b_pair = jnp.transpose(jnp.repeat(b_sgu[0], SGU_W // SGU_G, axis=0))
    b_f_pad = jnp.pad(b_f, ((0, 0), (0, LANES - N_HEADS)))
    head_sel = (jnp.arange(FOX_W)[:, None] // HEAD_DIM == jnp.arange(LANES)[None, :]).astype(F32)

    def store(dtype):
        def epi(accs, ex, out):
            out[0][...] = accs[0].astype(dtype)
        return epi

    (qkv,) = _mm("proj_qkv", [(h, w_qkv, False, None)], [], [(_sds((s_len, QKV_W), BF16), _tile(tl, 512))],
                 store(BF16), m=s_len, tm=tl, n=QKV_W, tn=512)
    rest_tn = 640
    f_tile, f_lane = F_OFF // rest_tn, F_OFF % rest_tn

    def rest_epi(accs, ex, out):
        out[0][...] = accs[0].astype(BF16)

        @pl.when(pl.program_id(1) == f_tile)
        def _():
            out[1][...] = accs[0][:, f_lane:f_lane + LANES]

    prest, f_logit = _mm("proj_rest", [(h, w_rest, False, None)], [],
                         [(_sds((s_len, REST_W), BF16), _tile(tl, rest_tn)), (_sds((s_len, LANES), F32), _row(tl, LANES))],
                         rest_epi, m=s_len, tm=tl, n=REST_W, tn=rest_tn, arbitrary=True)

    c_col = _forget_cumsum(f_logit, b_f_pad, tc)
    c_row = jnp.transpose(c_col[:, :N_HEADS]).reshape(N_HEADS, 1, s_len)
    k_t = jnp.transpose(qkv[:, FOX_W:2 * FOX_W])
    v_t = jnp.transpose(qkv[:, 2 * FOX_W:])
    o, o_t, lse_row, (later_g,) = _attn_fwd(qkv, v_t, c_col, c_row, ta, ta, comm=[later_plan])
    w_a_f, w_b_f, w_o_f, w_up_f, w_down_f = unflatten(later_g)
    sg, sg_t = _sgu_fwd(prest, ln_v_g, ln_v_b, w_stack, b_pair, tm)

    def merge_epi(accs, ex, out):
        ya, yb = accs
        sa, sb = _sigmoid(ex[0][...].astype(F32)), _sigmoid(ex[1][...].astype(F32))
        merged = (sa * ya + sb * yb).astype(BF16)
        out[0][...] = merged
        out[1][...] = ya.astype(BF16)
        out[2][...] = yb.astype(BF16)
        out[3][...] = jnp.transpose(merged)

    merged, ya, yb, merged_t = _mm(
        "merge", [(o, w_a_f, False, None), (sg, w_b_f, False, None)],
        [(prest, _tile(tm, d, GA_OFF // d)), (prest, _tile(tm, d, GB_OFF // d))],
        [(_sds((s_len, d), BF16), _tile(tm, d))] * 3 + [(_sds((d, s_len), BF16), _tile_t(tm, d))],
        merge_epi, m=s_len, tm=tm, n=d, tn=d)

    def resid_epi(accs, ex, out):
        out[0][...] = ex[0][...] + accs[0]

    (x1,) = _mm("out_proj", [(merged, w_o_f, False, None)], [(xs, _tile(tm, d))],
                [(_sds((s_len, d), F32), _tile(tm, d))], resid_epi, m=s_len, tm=tm, n=d, tn=d)

    h2, h2_t, r2, _ = _rms_fwd("rms2", x1, norm2_g, tm)

    def up_epi(accs, ex, out):
        a = accs[0]
        out[0][...] = a.astype(BF16)
        act = jnp.square(jnp.maximum(a, 0.0)).astype(BF16)
        out[1][...] = act
        out[2][...] = jnp.transpose(act)

    a_up, act, act_t = _mm(
        "mlp_up", [(h2, w_up_f, False, None)], [],
        [(_sds((s_len, D_FF), BF16), _tile(tl, 512)), (_sds((s_len, D_FF), BF16), _tile(tl, 512)),
         (_sds((D_FF, s_len), BF16), _tile_t(tl, 512))],
        up_epi, m=s_len, tm=tl, n=D_FF, tn=512)

    def first_step():
        return jnp.logical_and(pl.program_id(0) == 0, pl.program_id(1) == 0)

    def accumulate(ref, val):
        @pl.when(first_step())
        def _():
            ref[...] = val

        @pl.when(jnp.logical_not(first_step()))
        def _():
            ref[...] += val

    def final_epi(accs, ex, out):
        x1_ref, t_ref, g_ref = ex
        x2 = x1_ref[...] + accs[0]
        rf = lax.rsqrt(jnp.mean(x2 * x2, axis=-1, keepdims=True) + EPS)
        xh = x2 * rf
        gf = g_ref[...]
        err = xh * gf - t_ref[...]
        dy = err * (1.0 / d)
        dx2 = _rms_bwd(xh, rf, gf, dy)
        out[0][...] = dx2
        accumulate(out[1], jnp.sum(dy * xh, axis=0, keepdims=True))
        part = 0.5 * jnp.sum(jnp.sum(err * err, axis=-1, keepdims=True) * (1.0 / d), axis=0, keepdims=True)
        accumulate(out[2], jnp.broadcast_to(part, (1, LANES)))
        out[3][...] = dx2.astype(BF16)

    gf2 = normf_g.reshape(1, d)
    dx2, g_normf, loss_part, dx2_16 = _mm(
        "mlp_down_loss", [(act, w_down_f, False, None)],
        [(x1, _row(tr, d)), (target, _row(tr, d)), (gf2, _whole((1, d)))],
        [(_sds((s_len, d), F32), _row(tr, d)), (_sds((1, d), F32), _whole((1, d))), (_sds((1, LANES), F32), _whole((1, LANES))),
         (_sds((s_len, d), BF16), _row(tr, d))],
        final_epi, m=s_len, tm=tr, n=d, tn=d, arbitrary=True)

    def dact_epi(accs, ex, out):
        out[0][...] = (accs[0] * (2.0 * jnp.maximum(ex[0][...], 0.0))).astype(BF16)

    (da,) = _mm("mlp_down_bwd", [(dx2_16, w_down_f, True, None)], [(a_up, _tile(tl, 512))],
                [(_sds((s_len, D_FF), BF16), _tile(tl, 512))], dact_epi, m=s_len, tm=tl, n=D_FF, tn=512)
    g_down = _grad_w("grad_w_down", act_t, dx2_16, tk=1024, tn=d, ts=tl)
    g_up = _grad_w("grad_w_up", h2_t, da, tk=d, tn=1024, ts=tl, block_cols=D_FF // N_DEV)

    def dh2_epi(accs, ex, out):
        x1_ref, r_ref, g_ref, dx2_ref = ex
        r = r_ref[...]
        xh = x1_ref[...] * r
        dh2 = accs[0]
        out[0][...] = dx2_ref[...] + _rms_bwd(xh, r, g_ref[...], dh2)
        accumulate(out[1], jnp.sum(dh2 * xh, axis=0, keepdims=True))

    my_c = lax.axis_index("c")
    my_chip = 2 * lax.axis_index("x") + lax.axis_index("y")
    idx = jnp.stack([my_c, my_chip]).astype(jnp.int32)
    parts16, owns = {}, {}

    def split_cores(g8):
        return g8.reshape((4, 2) + g8.shape[1:])

    def pair_sums(names, grads4, from_sibling):
        for name, g4, recv in zip(names, grads4, from_sibling):
            parts16[name], owns[name] = _pair_sum("grad_pair_sum_" + name, g4, recv, idx, min(512, g4.shape[2]))

    grads4_mlp = [split_cores(g_up), split_cores(g_down.reshape(N_DEV, D_FF // N_DEV, d))]
    (dx1, g_norm2), from_sibling = _mm(
        "mlp_up_bwd", [(da, w_up_f, True, None)],
        [(x1, _row(tr, d)), (r2, _row(tr, 1)), (norm2_g, _whole((1, d))), (dx2, _row(tr, d))],
        [(_sds((s_len, d), F32), _row(tr, d)), (_sds((1, d), F32), _whole((1, d)))],
        dh2_epi, m=s_len, tm=tr, n=d, tn=d, arbitrary=True, comm=[_pair_exchange_plan(grads4_mlp)])
    pair_sums(("w_up", "w_down"), grads4_mlp, from_sibling)

    def dmerge_epi(accs, ex, out):
        dm = accs[0]
        sa, sb = _sigmoid(ex[0][...].astype(F32)), _sigmoid(ex[1][...].astype(F32))
        out[0][...] = (dm * sa).astype(BF16)
        out[1][...] = (dm * sb).astype(BF16)
        out[2][...] = (dm * ex[2][...] * sa * (1.0 - sa)).astype(BF16)
        out[3][...] = (dm * ex[3][...] * sb * (1.0 - sb)).astype(BF16)

    dya, dyb, dga, dgb = _mm(
        "out_proj_bwd", [(dx1, w_o_f, True, None)],
        [(prest, _tile(tm, d, GA_OFF // d)), (prest, _tile(tm, d, GB_OFF // d)), (ya, _tile(tm, d)), (yb, _tile(tm, d))],
        [(_sds((s_len, d), BF16), _tile(tm, d))] * 4, dmerge_epi, m=s_len, tm=tm, n=d, tn=d)
    g_o = _grad_w("grad_w_o", merged_t, dx1, tk=d, tn=d, ts=tl).reshape(N_DEV, d // N_DEV, d)
    def col_blocks(g):
        return jnp.transpose(g.reshape(g.shape[0], N_DEV, g.shape[1] // N_DEV), (1, 0, 2))

    g_a = col_blocks(_grad_w("grad_w_a", o_t, dya, tk=FOX_W, tn=d, ts=tl))
    g_b = col_blocks(_grad_w("grad_w_b", sg_t, dyb, tk=SGU_W, tn=d, ts=tl))

    def do_epi(accs, ex, out):
        do = accs[0]
        out[0][...] = do
        out[1][...] = _dot_f32(do * ex[0][...], ex[1][...])

    grads4_mix = [split_cores(g) for g in (g_a, g_b, g_o)]
    (do, delta), from_sibling = _mm(
        "attn_out_bwd", [(dya, w_a_f, True, None)], [(o, _row(tm, FOX_W)), (head_sel, _whole((FOX_W, LANES)))],
        [(_sds((s_len, FOX_W), F32), _row(tm, FOX_W)), (_sds((s_len, LANES), F32), _row(tm, LANES))],
        do_epi, m=s_len, tm=tm, n=FOX_W, tn=FOX_W, comm=[_pair_exchange_plan(grads4_mix)])
    pair_sums(("w_a", "w_b", "w_o"), grads4_mix, from_sibling)
    (dsg,) = _mm("sgu_out_bwd", [(dyb, w_b_f, True, None)], [], [(_sds((s_len, SGU_W), F32), _tile(tm, SGU_W))],
                 store(F32), m=s_len, tm=tm, n=SGU_W, tn=SGU_W)

    du, dsv, dw_pairs, db_pos, g_ln_g, g_ln_b = _sgu_bwd(prest, dsg, ln_v_g, ln_v_b, w_stack, wt_stack, b_pair, tm)
    g_w_sgu = jnp.where(sgu_mask[None], dw_pairs.reshape(SGU_G, SGU_LEN, SGU_LEN), 0.0)
    g_b_sgu = jnp.transpose(jnp.sum(db_pos.reshape(SGU_LEN, SGU_G, SGU_W // SGU_G), axis=-1))

    delta_row = jnp.transpose(delta[:, :N_HEADS]).reshape(N_HEADS, 1, s_len)
    early = ("w_a", "w_b", "w_o", "w_up", "w_down")
    small_early = _pack_rows((g_w_sgu, g_b_sgu, g_norm2, g_normf, g_ln_g, g_ln_b))
    (dq, dk, dv, dc_rows_blk, dc_cols), (small_early_all, *from_chips_early) = _attn_bwd(
        qkv, k_t, do, c_col, c_row, lse_row, delta_row, ta, ta,
        comm=[_gather_plan(small_early), _chip_exchange_plan([parts16[n] for n in early])])
    dc_rows = jnp.transpose(dc_rows_blk.reshape(s_len // ta, N_HEADS, ta), (0, 2, 1)).reshape(s_len, N_HEADS)
    dc_rows = jnp.pad(dc_rows, ((0, 0), (0, LANES - N_HEADS)))
    dfl, g_bf = _forget_bwd(dc_rows, dc_cols, f_logit, b_f_pad, tc)

    dp = (dq, dk, dv, du, dsv, dga, dgb)
    g_in_cols = [_grad_w("grad_w_in_%d" % k, h_t, seg, tk=d, tn=seg.shape[1], ts=tl) for k, seg in enumerate(dp)]
    g_f = _grad_w("grad_w_in_f", h_t, dfl, tk=d, tn=LANES, ts=tl)[:, :N_HEADS]
    g_in_full = jnp.concatenate(g_in_cols[:3] + [g_f] + g_in_cols[3:], axis=1)
    g_in = jnp.transpose(g_in_full.reshape(d, N_DEV, IN_SHARD), (1, 0, 2))

    def dx_epi(accs, ex, out):
        x_ref, r_ref, g_ref, dx1_ref = ex
        dh = accs[0]
        for extra in accs[1:]:
            dh = dh + extra
        r = r_ref[...]
        xh = x_ref[...] * r
        out[0][...] = dx1_ref[...] + _rms_bwd(xh, r, g_ref[...], dh)
        accumulate(out[1], jnp.sum(dh * xh, axis=0, keepdims=True))

    rest_cols = ((du, U_OFF, 512), (dsv, SV_OFF, 512), (dga, GA_OFF, 1024), (dgb, GB_OFF, 1024), (dfl, F_OFF, LANES))
    dx_pairs = [(seg, w_qkv, True, (512 * k, 512 * (k + 1))) for k, seg in enumerate((dq, dk, dv))]
    dx_pairs += [(seg, w_rest, True, (lo, lo + width)) for seg, lo, width in rest_cols]
    grads4_in = [split_cores(g_in)]
    pair_sums(("w_in",), grads4_in, _run_comm("grad_pair_exchange_w_in", [_pair_exchange_plan(grads4_in)]))
    (grad_x, g_norm1), (from_chips_in,) = _mm(
        "proj_bwd", dx_pairs,
        [(xs, _row(tr, d)), (r1, _row(tr, 1)), (norm1_g, _whole((1, d))), (dx1, _row(tr, d))],
        [(_sds((s_len, d), F32), _row(tr, d)), (_sds((1, d), F32), _whole((1, d)))],
        dx_epi, m=s_len, tm=tr, n=d, tn=d, arbitrary=True, comm=[_chip_exchange_plan([parts16["w_in"]])])
    small_late = _pack_rows((g_bf[:, :N_HEADS], g_norm1, loss_part[0, 0]))
    (small_late_all,) = _run_comm("gather_last_grads", [_gather_plan(small_late)])
    small_all = jnp.concatenate([small_early_all, small_late_all], axis=1)
    from_chips = dict(zip(early, from_chips_early), w_in=from_chips_in)

    names = ("w_in", "w_a", "w_b", "w_o", "w_up", "w_down")
    moments_m = (m_w_in, m_w_a, m_w_b, m_w_o, m_w_up, m_w_down)
    moments_v = (v_w_in, v_w_a, v_w_b, v_w_o, v_w_up, v_w_down)
    big_out = {}
    for name, w, m, v in zip(names, big, moments_m, moments_v):
        own = owns[name]
        res = _adamw_shard("adamw_" + name, own, from_chips[name], w, m[0], v[0], min(512, own.shape[0]))
        big_out[name] = [t[None] for t in res]

    zero = jnp.zeros((), F32)
    small_w = _pack_small((w_sgu, b_sgu, norm2_g, normf_g, ln_v_g, ln_v_b, b_f, norm1_g, zero))
    small_m = _pack_small((m_w_sgu, m_b_sgu, m_norm2_g, m_normf_g, m_ln_v_g, m_ln_v_b, m_b_f, m_norm1_g, zero))
    small_v = _pack_small((v_w_sgu, v_b_sgu, v_norm2_g, v_normf_g, v_ln_v_g, v_ln_v_b, v_b_f, v_norm1_g, zero))
    small_res = [_unpack_small(t) for t in _adamw_small(small_all, small_w, small_m, small_v)]
    small_names = [n for n, _ in _SMALL]
    small_out = {n: [res[k] for res in small_res] for k, n in enumerate(small_names)}
    loss = small_out["loss"][0]

    order = ("norm1_g", "w_in", "b_f", "ln_v_g", "ln_v_b", "w_sgu", "b_sgu", "w_a", "w_b", "w_o", "norm2_g", "w_up",
             "w_down", "normf_g")
    table = {**big_out, **small_out}
    outs = [loss, grad_x[None]]
    for kind in range(4):
        outs += [table[n][kind] for n in order]
    return tuple(outs)
```

```python
import math

import jax
import jax.numpy as jnp
from jax import lax
from jax.experimental import pallas as pl
from jax.experimental.pallas import tpu as pltpu

F32 = jnp.float32
BF16 = jnp.bfloat16

N_DEV = 8
D_MODEL = 1024
N_HEADS = 8
HEAD_DIM = 64
FOX_W = N_HEADS * HEAD_DIM
SGU_G = 8
SGU_W = 512
SGU_LEN = 128
CHUNK = 64
D_FF = 4 * D_MODEL
IN_COLS = 3 * FOX_W + N_HEADS + 2 * SGU_W + 2 * D_MODEL
IN_SHARD = IN_COLS // N_DEV
LANES = 128
QKV_W = 3 * FOX_W
U_OFF, SV_OFF, GA_OFF, GB_OFF, F_OFF = 0, 512, 1024, 2048, 3072
REST_W = F_OFF + LANES
EPS = 1e-6
NEG = -1e30

ADAM_LR = 0.001
ADAM_B1 = 0.9
ADAM_B2 = 0.999
ADAM_EPS = 1e-08
ADAM_WD = 0.01
ADAM_STEP = 10

VMEM_LIMIT = 56 * 1024 * 1024
MESH = pl.DeviceIdType.MESH


def _params(sem=None):
    return pltpu.CompilerParams(dimension_semantics=sem, vmem_limit_bytes=VMEM_LIMIT)


def _dot(a, b):
    return jnp.dot(a, b, preferred_element_type=F32)


def _dot_nt(a, b):
    return lax.dot_general(a, b, (((1,), (1,)), ((), ())), preferred_element_type=F32)


def _dot_f32(a, b):
    return jnp.dot(a, b, preferred_element_type=F32, precision=lax.Precision.HIGHEST)


def _sigmoid(x):
    return 1.0 / (1.0 + jnp.exp(-x))


def _log_sigmoid(z):
    return jnp.minimum(z, 0.0) - jnp.log(1.0 + jnp.exp(-jnp.abs(z)))


_GELU_K = math.sqrt(2.0 / math.pi)
_GELU_C = 0.044715


def _gelu(x):
    t = jnp.tanh(_GELU_K * (x + _GELU_C * (x * x * x)))
    return 0.5 * x * (1.0 + t)


def _gelu_grad(x):
    x2 = x * x
    t = jnp.tanh(_GELU_K * (x + _GELU_C * (x2 * x)))
    return 0.5 * (1.0 + t) + 0.5 * x * (1.0 - t * t) * (_GELU_K * (1.0 + 3.0 * _GELU_C * x2))


def _rms_bwd(xh, r, g, dy):
    gy = dy * g
    return r * (gy - xh * jnp.mean(xh * gy, axis=-1, keepdims=True))


def _lane_lt64(shape):
    return lax.broadcasted_iota(jnp.int32, shape, len(shape) - 1) < HEAD_DIM


class _Comm:
    def __init__(self, arrays, out_shapes, sems, start, finish, mid=None):
        self.arrays, self.out_shapes, self.sems = list(arrays), list(out_shapes), list(sems)
        self.start, self.mid, self.finish = start, mid, finish


def _comm_phase(plans, phase, in_refs, out_refs, sem_refs):
    ia = io = ks = 0
    for plan in plans:
        na, no, ns = len(plan.arrays), len(plan.out_shapes), len(plan.sems)
        fn = getattr(plan, phase)
        if fn is not None:
            fn(in_refs[ia:ia + na], out_refs[io:io + no], sem_refs[ks:ks + ns])
        ia, io, ks = ia + na, io + no, ks + ns


def _comm_operands(plans):
    arrays = [a for plan in plans for a in plan.arrays]
    out_shapes = [o for plan in plans for o in plan.out_shapes]
    sems = [s for plan in plans for s in plan.sems]
    return arrays, out_shapes, sems


_ANY = pl.BlockSpec(memory_space=pl.ANY)


def _run_comm(name, plans):
    arrays, out_shapes, sems = _comm_operands(plans)
    n_in, n_out = len(arrays), len(out_shapes)

    def body(*refs):
        parts = refs[:n_in], refs[n_in:n_in + n_out], refs[n_in + n_out:]
        for phase in ("start", "mid", "finish"):
            _comm_phase(plans, phase, *parts)

    return pl.pallas_call(
        body, name=name, out_shape=out_shapes, in_specs=[_ANY] * n_in, out_specs=[_ANY] * n_out, scratch_shapes=sems,
    )(*arrays)


def _gather_plan(shard):
    def setup(ins, outs, sems):
        (x_ref,), (out_ref,), (send_sems, recv_sems, local_sem) = ins, outs, sems
        x, y, c = lax.axis_index("x"), lax.axis_index("y"), lax.axis_index("c")
        me, sibling = (x, y, c), (x, y, 1 - c)
        chips = [(1 - x, y), (x, 1 - y), (1 - x, 1 - y)]

        def rows(px, py, pc):
            return out_ref.at[4 * px + 2 * py + pc]

        def copy(k, block, to, src=None):
            return pltpu.make_async_remote_copy(
                src_ref=rows(*block) if src is None else src,
                dst_ref=rows(*block),
                send_sem=send_sems.at[k],
                recv_sem=recv_sems.at[k],
                device_id=to,
                device_id_type=MESH,
            )

        mine = pltpu.make_async_copy(x_ref, rows(*me), local_sem)
        first = [copy(0, me, sibling, src=x_ref)]
        first += [copy(1 + j, me, (*chip, c), src=x_ref) for j, chip in enumerate(chips)]
        passed = [copy(4 + j, (*chip, c), sibling) for j, chip in enumerate(chips)]
        landed = [copy(1 + j, (*chip, c), me) for j, chip in enumerate(chips)]
        from_sibling = [copy(0, sibling, me)] + [copy(4 + j, (*chip, 1 - c), me) for j, chip in enumerate(chips)]
        return mine, first, passed, landed, from_sibling

    def start(ins, outs, sems):
        mine, first, _, _, _ = setup(ins, outs, sems)
        mine.start()
        for cp in first:
            cp.start()

    def mid(ins, outs, sems):
        _, _, passed, landed, _ = setup(ins, outs, sems)
        for arrived, onward in zip(landed, passed):
            arrived.wait_recv()
            onward.start()

    def finish(ins, outs, sems):
        mine, first, passed, _, from_sibling = setup(ins, outs, sems)
        for cp in from_sibling:
            cp.wait_recv()
        for cp in first + passed:
            cp.wait_send()
        mine.wait()

    return _Comm([shard], [jax.ShapeDtypeStruct((N_DEV,) + shard.shape, shard.dtype)],
                 [pltpu.SemaphoreType.DMA((7,)), pltpu.SemaphoreType.DMA((7,)), pltpu.SemaphoreType.DMA],
                 start, finish, mid)


def _start_all(copies):
    for cp in copies:
        cp.start()


def _wait_all(copies):
    for cp in copies:
        cp.wait_recv()
    for cp in copies:
        cp.wait_send()


def _pair_exchange_plan(grads):
    n = len(grads)

    def copies(ins, outs, sems):
        send_sems, recv_sems = sems
        x, y, c = lax.axis_index("x"), lax.axis_index("y"), lax.axis_index("c")
        return [
            pltpu.make_async_remote_copy(
                src_ref=ins[k].at[:, 1 - c],
                dst_ref=outs[k],
                send_sem=send_sems.at[k],
                recv_sem=recv_sems.at[k],
                device_id=(x, y, 1 - c),
                device_id_type=MESH,
            )
            for k in range(n)
        ]

    return _Comm(grads, [jax.ShapeDtypeStruct((4,) + g.shape[2:], g.dtype) for g in grads],
                 [pltpu.SemaphoreType.DMA((n,)), pltpu.SemaphoreType.DMA((n,))],
                 lambda *refs: _start_all(copies(*refs)), lambda *refs: _wait_all(copies(*refs)))


def _chip_exchange_plan(parts):
    n = len(parts)

    def copies(ins, outs, sems):
        send_sems, recv_sems = sems
        x, y, c = lax.axis_index("x"), lax.axis_index("y"), lax.axis_index("c")
        chips = [(1 - x, y), (x, 1 - y), (1 - x, 1 - y)]
        return [
            pltpu.make_async_remote_copy(
                src_ref=ins[k].at[2 * px + py],
                dst_ref=outs[k].at[j],
                send_sem=send_sems.at[3 * k + j],
                recv_sem=recv_sems.at[3 * k + j],
                device_id=(px, py, c),
                device_id_type=MESH,
            )
            for k in range(n) for j, (px, py) in enumerate(chips)
        ]

    return _Comm(parts, [jax.ShapeDtypeStruct((3,) + p.shape[1:], p.dtype) for p in parts],
                 [pltpu.SemaphoreType.DMA((3 * n,)), pltpu.SemaphoreType.DMA((3 * n,))],
                 lambda *refs: _start_all(copies(*refs)), lambda *refs: _wait_all(copies(*refs)))


def _mm(name, pairs, extras, outs, epi, *, m, tm, n, tn, arbitrary=False, comm=()):
    nj = n // tn
    a_arrays, a_specs, b_arrays, b_specs, b_index = [], [], [], [], []
    for a, b, nt, cols in pairs:
        a_arrays.append(a)
        a_specs.append(pl.BlockSpec((tm, a.shape[1]), lambda i, j: (i, 0)))
        known = [k for k, other in enumerate(b_arrays) if other is b]
        if known:
            b_index.append(known[0])
            continue
        b_index.append(len(b_arrays))
        b_arrays.append(b)
        if cols is not None:
            assert nj == 1
            b_specs.append(pl.BlockSpec(b.shape, lambda i, j: (0, 0)))
        elif nt:
            b_specs.append(pl.BlockSpec((tn, b.shape[1]), lambda i, j: (j, 0)))
        else:
            b_specs.append(pl.BlockSpec((b.shape[0], tn), lambda i, j: (0, j)))
    comm_arrays, comm_outs, comm_sems = _comm_operands(comm)
    arrays = a_arrays + b_arrays + [arr for arr, _ in extras] + comm_arrays
    in_specs = a_specs + b_specs + [spec for _, spec in extras] + [_ANY] * len(comm_arrays)
    n_a, n_b, n_extras, n_ci, n_out, n_co = len(a_arrays), len(b_arrays), len(extras), len(comm_arrays), len(outs), len(comm_outs)
    ni = m // tm

    def body(*refs):
        a_refs = refs[:n_a]
        b_refs = refs[n_a:n_a + n_b]
        ex = refs[n_a + n_b:n_a + n_b + n_extras]
        n_in = n_a + n_b + n_extras + n_ci
        comm_refs = refs[n_in - n_ci:n_in], refs[n_in + n_out:n_in + n_out + n_co], refs[n_in + n_out + n_co:]
        out = refs[n_in:n_in + n_out]
        if comm:
            @pl.when(jnp.logical_and(pl.program_id(0) == 0, pl.program_id(1) == 0))
            def _():
                _comm_phase(comm, "start", *comm_refs)

        accs = []
        for p, (_, _, nt, cols) in enumerate(pairs):
            av = a_refs[p][...]
            if av.dtype != BF16:
                av = av.astype(BF16)
            b_ref = b_refs[b_index[p]]
            bv = b_ref[...] if cols is None else b_ref[:, cols[0]:cols[1]]
            accs.append(_dot_nt(av, bv) if nt else _dot(av, bv))
        epi(accs, ex, out)
        if comm:
            mid_row = ni // 2 if ni >= 3 else ni - 1
            mid_col = 0 if ni >= 3 else nj - 1

            @pl.when(jnp.logical_and(pl.program_id(0) == mid_row, pl.program_id(1) == mid_col))
            def _():
                _comm_phase(comm, "mid", *comm_refs)

            @pl.when(jnp.logical_and(pl.program_id(0) == ni - 1, pl.program_id(1) == nj - 1))
            def _():
                _comm_phase(comm, "finish", *comm_refs)

    sem = ("arbitrary", "arbitrary") if arbitrary or comm else ("parallel", "parallel")
    res = pl.pallas_call(
        body,
        name=name,
        grid=(ni, nj),
        in_specs=in_specs,
        out_specs=[spec for _, spec in outs] + [_ANY] * n_co,
        out_shape=[shape for shape, _ in outs] + comm_outs,
        scratch_shapes=comm_sems,
        compiler_params=_params(sem),
    )(*arrays)
    return (res[:n_out], res[n_out:]) if comm else res


def _tile(tm, tn, off=0):
    return pl.BlockSpec((tm, tn), lambda i, j: (i, j + off))


def _row(tm, w, blk=0):
    return pl.BlockSpec((tm, w), lambda i, j: (i, blk))


def _whole(shape):
    zeros = (0,) * len(shape)
    return pl.BlockSpec(shape, lambda i, j: zeros)


def _sds(shape, dtype):
    return jax.ShapeDtypeStruct(shape, dtype)


def _tile_t(tm, tn):
    return pl.BlockSpec((tn, tm), lambda i, j: (j, i))


def _grad_w(name, a_t, g, *, tk, tn, ts, block_cols=None):
    ka, s_len = a_t.shape
    n = g.shape[1]
    width = tn if block_cols is None else block_cols

    def body(a_ref, g_ref, o_ref):
        first = pl.program_id(2) == 0
        gv = g_ref[...].astype(BF16)
        for b in range(tn // width):
            part = _dot(a_ref[...], gv[:, b * width:(b + 1) * width])
            dst = o_ref if block_cols is None else o_ref.at[b]

            @pl.when(first)
            def _():
                dst[...] = part

            @pl.when(jnp.logical_not(first))
            def _():
                dst[...] += part

    if block_cols is None:
        out_shape = _sds((ka, n), F32)
        out_spec = pl.BlockSpec((tk, tn), lambda i, j, s: (i, j))
    else:
        out_shape = _sds((n // width, ka, width), F32)
        out_spec = pl.BlockSpec((tn // width, tk, width), lambda i, j, s: (j, i, 0))
    return pl.pallas_call(
        body,
        name=name,
        grid=(ka // tk, n // tn, s_len // ts),
        in_specs=[pl.BlockSpec((tk, ts), lambda i, j, s: (i, s)), pl.BlockSpec((ts, tn), lambda i, j, s: (s, j))],
        out_specs=out_spec,
        out_shape=out_shape,
        compiler_params=_params(("parallel", "parallel", "arbitrary")),
    )(a_t, g)


def _rms_fwd(name, x, g, tm, comm=()):
    s_len, d = x.shape
    steps = s_len // tm
    comm_arrays, comm_outs, comm_sems = _comm_operands(comm)
    n_ci, n_co = len(comm_arrays), len(comm_outs)

    def body(x_ref, g_ref, *rest):
        comm_refs = rest[:n_ci], rest[n_ci + 3:n_ci + 3 + n_co], rest[n_ci + 3 + n_co:]
        h_ref, ht_ref, r_ref = rest[n_ci:n_ci + 3]
        for phase, at in (("start", 0), ("mid", steps // 2)):
            if comm:
                @pl.when(pl.program_id(0) == at)
                def _():
                    _comm_phase(comm, phase, *comm_refs)

        xv = x_ref[...]
        r = lax.rsqrt(jnp.mean(xv * xv, axis=-1, keepdims=True) + EPS)
        h = (xv * r * g_ref[...]).astype(BF16)
        h_ref[...] = h
        ht_ref[...] = jnp.transpose(h)
        r_ref[...] = r
        if comm:
            @pl.when(pl.program_id(0) == steps - 1)
            def _():
                _comm_phase(comm, "finish", *comm_refs)

    res = pl.pallas_call(
        body,
        name=name,
        grid=(steps,),
        in_specs=[pl.BlockSpec((tm, d), lambda i: (i, 0)), pl.BlockSpec((1, d), lambda i: (0, 0))] + [_ANY] * n_ci,
        out_specs=[pl.BlockSpec((tm, d), lambda i: (i, 0)), pl.BlockSpec((d, tm), lambda i: (0, i)),
                   pl.BlockSpec((tm, 1), lambda i: (i, 0))] + [_ANY] * n_co,
        out_shape=[_sds((s_len, d), BF16), _sds((d, s_len), BF16), _sds((s_len, 1), F32)] + comm_outs,
        scratch_shapes=comm_sems,
        compiler_params=_params(("arbitrary",) if comm else ("parallel",)),
    )(x, g, *comm_arrays)
    return res[0], res[1], res[2], res[3:]


def _forget_cumsum(prest, b_f_pad, tc):
    s_len = prest.shape[0]

    def body(f_ref, b_ref, c_ref, carry):
        @pl.when(pl.program_id(0) == 0)
        def _():
            carry[...] = jnp.zeros_like(carry)

        logf = _log_sigmoid(f_ref[...] + b_ref[...])
        row = lax.broadcasted_iota(jnp.int32, (tc, tc), 0)
        col = lax.broadcasted_iota(jnp.int32, (tc, tc), 1)
        tri = (row >= col).astype(F32)
        c = _dot_f32(tri, logf) + carry[...]
        c_ref[...] = c
        carry[...] = c[tc - 1:tc, :]

    return pl.pallas_call(
        body,
        name="forget_cumsum",
        grid=(s_len // tc,),
        in_specs=[pl.BlockSpec((tc, LANES), lambda i: (i, 0)), pl.BlockSpec((1, LANES), lambda i: (0, 0))],
        out_specs=pl.BlockSpec((tc, LANES), lambda i: (i, 0)),
        out_shape=_sds((s_len, LANES), F32),
        scratch_shapes=[pltpu.VMEM((1, LANES), F32)],
        compiler_params=_params(("arbitrary",)),
    )(prest, b_f_pad)


def _stack_heads(pair, lt64):
    zero = jnp.zeros_like(pair)
    return jnp.concatenate([jnp.where(lt64, pair, zero), jnp.where(lt64, zero, pair)], axis=0)


def _score_tiles(q_ref, k_ref, ck_ref, st_sc, tk):
    lt64 = _lane_lt64((tk, LANES))
    for p in range(N_HEADS // 2):
        lanes = slice(p * LANES, (p + 1) * LANES)
        q_pair = q_ref[:, lanes] * jnp.asarray(HEAD_DIM ** -0.5, BF16)
        st2 = _dot_nt(_stack_heads(k_ref[:, lanes], lt64), q_pair)
        for half in range(2):
            h = 2 * p + half
            st_sc[h] = st2[half * tk:(half + 1) * tk] - ck_ref[:, h:h + 1]


def _mask_diagonal(st_sc, i, j, tq, tk):
    @pl.when((j + 1) * tk - 1 > i * tq)
    def _():
        key = j * tk + lax.broadcasted_iota(jnp.int32, (tk, tq), 0)
        query = i * tq + lax.broadcasted_iota(jnp.int32, (tk, tq), 1)
        st_sc[...] = jnp.where((query >= key)[None], st_sc[...], NEG)


def _attn_fwd(qkv, v_t, c_col, tq, tk, comm=()):
    s_len = qkv.shape[0]
    ratio = tq // tk
    steps = [(i, j) for i in range(s_len // tq) for j in range((i + 1) * ratio)]
    i_tab = jnp.asarray([i for i, _ in steps], jnp.int32)
    j_tab = jnp.asarray([j for _, j in steps], jnp.int32)

    comm_arrays, comm_outs, comm_sems = _comm_operands(comm)
    n_ci, n_co = len(comm_arrays), len(comm_outs)

    def body(i_ref, j_ref, q_ref, k_ref, vt_ref, ck_ref, *rest):
        comm_refs = rest[:n_ci], rest[n_ci + 3:n_ci + 3 + n_co], rest[n_ci + 3 + n_co + 5:]
        o_ref, ot_ref, lse_ref = rest[n_ci:n_ci + 3]
        acc_t, m_sc, l_sc, st_sc, p_sc = rest[n_ci + 3 + n_co:n_ci + 3 + n_co + 5]
        n = pl.program_id(0)
        i, j = i_ref[n], j_ref[n]
        for phase, at in (("start", 0), ("mid", (2 * len(steps)) // 3)):
            if comm:
                @pl.when(n == at)
                def _():
                    _comm_phase(comm, phase, *comm_refs)

        @pl.when(j == 0)
        def _():
            acc_t[...] = jnp.zeros_like(acc_t)
            m_sc[...] = jnp.full_like(m_sc, NEG)
            l_sc[...] = jnp.zeros_like(l_sc)

        _score_tiles(q_ref, k_ref, ck_ref, st_sc, tk)
        _mask_diagonal(st_sc, i, j, tq, tk)
        st = st_sc[...]
        m_old = m_sc[...]
        m_new = jnp.maximum(m_old, jnp.max(st, axis=1, keepdims=True))
        alpha = jnp.exp(m_old - m_new)
        pt = jnp.exp(st - m_new)
        l_sc[...] = alpha * l_sc[...] + jnp.sum(pt, axis=1, keepdims=True)
        m_sc[...] = m_new
        p_sc[...] = pt.astype(BF16)
        top = lax.broadcasted_iota(jnp.int32, (LANES, tq), 0) < HEAD_DIM
        for p in range(N_HEADS // 2):
            lanes = slice(p * LANES, (p + 1) * LANES)
            vt_pair = vt_ref[lanes, :]
            pv = jnp.where(top, _dot(vt_pair, p_sc[2 * p]), _dot(vt_pair, p_sc[2 * p + 1]))
            acc_t[lanes, :] = acc_t[lanes, :] * jnp.where(top, alpha[2 * p], alpha[2 * p + 1]) + pv

        @pl.when(j == (i + 1) * ratio - 1)
        def _():
            for p in range(N_HEADS // 2):
                lanes = slice(p * LANES, (p + 1) * LANES)
                l_pair = jnp.where(top, l_sc[2 * p], l_sc[2 * p + 1])
                o_t = acc_t[lanes, :] / l_pair
                o_ref[:, lanes] = jnp.transpose(o_t)
                ot_ref[lanes, :] = o_t.astype(BF16)
            lse_ref[...] = m_sc[...] + jnp.log(l_sc[...])

        if comm:
            @pl.when(n == len(steps) - 1)
            def _():
                _comm_phase(comm, "finish", *comm_refs)

    stat = pltpu.VMEM((N_HEADS, 1, tq), F32)
    res = pl.pallas_call(
        body,
        name="attn_fwd",
        grid_spec=pltpu.PrefetchScalarGridSpec(
            num_scalar_prefetch=2,
            grid=(len(steps),),
            in_specs=[
                pl.BlockSpec((tq, FOX_W), lambda n, it, jt: (it[n], 0)),
                pl.BlockSpec((tk, FOX_W), lambda n, it, jt: (jt[n], 1)),
                pl.BlockSpec((FOX_W, tk), lambda n, it, jt: (0, jt[n])),
                pl.BlockSpec((tk, LANES), lambda n, it, jt: (jt[n], 0)),
            ] + [_ANY] * n_ci,
            out_specs=[
                pl.BlockSpec((tq, FOX_W), lambda n, it, jt: (it[n], 0)),
                pl.BlockSpec((FOX_W, tq), lambda n, it, jt: (0, it[n])),
                pl.BlockSpec((N_HEADS, 1, tq), lambda n, it, jt: (0, 0, it[n])),
            ] + [_ANY] * n_co,
            scratch_shapes=[pltpu.VMEM((FOX_W, tq), F32), stat, stat, pltpu.VMEM((N_HEADS, tk, tq), F32),
                            pltpu.VMEM((N_HEADS, tk, tq), BF16)] + comm_sems,
        ),
        out_shape=[_sds((s_len, FOX_W), F32), _sds((FOX_W, s_len), BF16), _sds((N_HEADS, 1, s_len), F32)] + comm_outs,
        compiler_params=_params(("arbitrary",)),
    )(i_tab, j_tab, qkv, qkv, v_t, c_col, *comm_arrays)
    return res[0], res[1], res[2], res[3:]


def _sgu_mix(vn, w_stack, lt64):
    outs = []
    for p in range(SGU_G // 2):
        r = _dot(w_stack[p], vn[:, p * LANES:(p + 1) * LANES])
        outs.append(jnp.where(lt64, r[:SGU_LEN], r[SGU_LEN:]))
    return jnp.concatenate(outs, axis=1)


def _sgu_norm(sv, ln_g, ln_b):
    svg = _gelu(sv)
    xc = svg - jnp.mean(svg, axis=-1, keepdims=True)
    rstd = lax.rsqrt(jnp.mean(xc * xc, axis=-1, keepdims=True) + EPS)
    xhat = xc * rstd
    return xhat, rstd, xhat * ln_g + ln_b


def _sgu_fwd(prest, ln_g, ln_b, w_stack, b_pair, tm):
    s_len = prest.shape[0]

    def body(u_ref, sv_ref, g_ref, b_ref, w_ref, bp_ref, sg_ref, sgt_ref):
        lt64 = _lane_lt64((SGU_LEN, LANES))
        _, _, vn = _sgu_norm(sv_ref[...].astype(F32), g_ref[...], b_ref[...])
        vn = vn.astype(BF16)
        w_stack_v = [w_ref[p] for p in range(SGU_G // 2)]
        for w in range(tm // SGU_LEN):
            win = slice(w * SGU_LEN, (w + 1) * SGU_LEN)
            mixed = _sgu_mix(vn[win], w_stack_v, lt64) + bp_ref[...]
            sg = (_gelu(u_ref[win, :].astype(F32)) * mixed).astype(BF16)
            sg_ref[win, :] = sg
            sgt_ref[:, win] = jnp.transpose(sg)

    return pl.pallas_call(
        body,
        name="sgu_fwd",
        grid=(s_len // tm,),
        in_specs=[
            pl.BlockSpec((tm, SGU_W), lambda i: (i, U_OFF // SGU_W)),
            pl.BlockSpec((tm, SGU_W), lambda i: (i, SV_OFF // SGU_W)),
            pl.BlockSpec((1, SGU_W), lambda i: (0, 0)),
            pl.BlockSpec((1, SGU_W), lambda i: (0, 0)),
            pl.BlockSpec((SGU_G // 2, 2 * SGU_LEN, SGU_LEN), lambda i: (0, 0, 0)),
            pl.BlockSpec((SGU_LEN, SGU_W), lambda i: (0, 0)),
        ],
        out_specs=[pl.BlockSpec((tm, SGU_W), lambda i: (i, 0)), pl.BlockSpec((SGU_W, tm), lambda i: (0, i))],
        out_shape=[_sds((s_len, SGU_W), BF16), _sds((SGU_W, s_len), BF16)],
        compiler_params=_params(("parallel",)),
    )(prest, prest, ln_g, ln_b, w_stack, b_pair)


def _sgu_bwd(prest, dsg, ln_g, ln_b, w_stack, wt_stack, b_pair, tm):
    s_len = prest.shape[0]
    n_pair = SGU_G // 2

    def body(u_ref, sv_ref, dsg_ref, g_ref, b_ref, w_ref, wt_ref, bp_ref,
             du_ref, dsv_ref, dw_ref, db_ref, dg_ref, dbeta_ref, dvn_sc):
        @pl.when(pl.program_id(0) == 0)
        def _():
            dw_ref[...] = jnp.zeros_like(dw_ref)
            db_ref[...] = jnp.zeros_like(db_ref)
            dg_ref[...] = jnp.zeros_like(dg_ref)
            dbeta_ref[...] = jnp.zeros_like(dbeta_ref)

        lt64 = _lane_lt64((SGU_LEN, LANES))
        sv = sv_ref[...].astype(F32)
        xhat, rstd, vn32 = _sgu_norm(sv, g_ref[...], b_ref[...])
        vn = vn32.astype(BF16)
        w_stack_v = [w_ref[p] for p in range(n_pair)]
        db = jnp.zeros((SGU_LEN, SGU_W), F32)
        for w in range(tm // SGU_LEN):
            win = slice(w * SGU_LEN, (w + 1) * SGU_LEN)
            u = u_ref[win, :].astype(F32)
            dsg_w = dsg_ref[win, :]
            mixed = _sgu_mix(vn[win], w_stack_v, lt64) + bp_ref[...]
            du_ref[win, :] = (dsg_w * mixed * _gelu_grad(u)).astype(BF16)
            dmixed = dsg_w * _gelu(u)
            db = db + dmixed
            dm16 = dmixed.astype(BF16)
            for p in range(n_pair):
                lanes = slice(p * LANES, (p + 1) * LANES)
                dmp = dm16[:, lanes]
                r = _dot(wt_ref[p], dmp)
                dvn_sc[win, lanes] = jnp.where(lt64, r[:SGU_LEN], r[SGU_LEN:])
                zero = jnp.zeros_like(dmp)
                dm_ab = jnp.concatenate([jnp.where(lt64, dmp, zero), jnp.where(lt64, zero, dmp)], axis=0)
                dw_ref[p] += _dot_nt(dm_ab, vn[win, lanes])
        db_ref[...] += db
        dvn = dvn_sc[...]
        dg_ref[...] += jnp.sum(dvn * xhat, axis=0, keepdims=True)
        dbeta_ref[...] += jnp.sum(dvn, axis=0, keepdims=True)
        dxh = dvn * g_ref[...]
        dsvg = rstd * (dxh - jnp.mean(dxh, axis=-1, keepdims=True) - xhat * jnp.mean(dxh * xhat, axis=-1, keepdims=True))
        dsv_ref[...] = (dsvg * _gelu_grad(sv)).astype(BF16)

    const2 = lambda i: (0, 0)
    const3 = lambda i: (0, 0, 0)
    return pl.pallas_call(
        body,
        name="sgu_bwd",
        grid=(s_len // tm,),
        in_specs=[
            pl.BlockSpec((tm, SGU_W), lambda i: (i, U_OFF // SGU_W)),
            pl.BlockSpec((tm, SGU_W), lambda i: (i, SV_OFF // SGU_W)),
            pl.BlockSpec((tm, SGU_W), lambda i: (i, 0)),
            pl.BlockSpec((1, SGU_W), const2),
            pl.BlockSpec((1, SGU_W), const2),
            pl.BlockSpec((n_pair, 2 * SGU_LEN, SGU_LEN), const3),
            pl.BlockSpec((n_pair, 2 * SGU_LEN, SGU_LEN), const3),
            pl.BlockSpec((SGU_LEN, SGU_W), const2),
        ],
        out_specs=[
            pl.BlockSpec((tm, SGU_W), lambda i: (i, 0)),
            pl.BlockSpec((tm, SGU_W), lambda i: (i, 0)),
            pl.BlockSpec((n_pair, 2 * SGU_LEN, SGU_LEN), const3),
            pl.BlockSpec((SGU_LEN, SGU_W), const2),
            pl.BlockSpec((1, SGU_W), const2),
            pl.BlockSpec((1, SGU_W), const2),
        ],
        out_shape=[
            _sds((s_len, SGU_W), BF16), _sds((s_len, SGU_W), BF16), _sds((n_pair, 2 * SGU_LEN, SGU_LEN), F32),
            _sds((SGU_LEN, SGU_W), F32), _sds((1, SGU_W), F32), _sds((1, SGU_W), F32),
        ],
        scratch_shapes=[pltpu.VMEM((tm, SGU_W), F32)],
        compiler_params=_params(("arbitrary",)),
    )(prest, prest, dsg, ln_g, ln_b, w_stack, wt_stack, b_pair)


def _attn_bwd(qkv, k_t, do, c_col, lse_row, delta_row, tq, tk, comm=()):
    s_len = qkv.shape[0]
    nq, nk = s_len // tq, s_len // tk
    ratio = tq // tk
    scale = HEAD_DIM ** -0.5
    steps = [(j, i) for j in range(nk) for i in range(j // ratio, nq)]
    j_tab = jnp.asarray([j for j, _ in steps], jnp.int32)
    i_tab = jnp.asarray([i for _, i in steps], jnp.int32)

    comm_arrays, comm_outs, comm_sems = _comm_operands(comm)
    n_ci, n_co = len(comm_arrays), len(comm_outs)

    def body(j_ref, i_ref, q_ref, k_ref, v_ref, kt_ref, do_ref, ck_ref, lse_ref, dl_ref, *rest):
        comm_refs = rest[:n_ci], rest[n_ci + 5:n_ci + 5 + n_co], rest[n_ci + 5 + n_co + 8:]
        dq_ref, dk_ref, dv_ref, dcr_ref, dcc_ref = rest[n_ci:n_ci + 5]
        dq_t, dk_acc, dv_acc, dcc_acc, st_sc, dpt_sc, p_sc, ds_sc = rest[n_ci + 5 + n_co:n_ci + 5 + n_co + 8]
        n = pl.program_id(0)
        j, i = j_ref[n], i_ref[n]

        @pl.when(n == 0)
        def _():
            _comm_phase(comm, "start", *comm_refs)
            dq_t[...] = jnp.zeros_like(dq_t)
            dcr_ref[...] = jnp.zeros_like(dcr_ref)

        @pl.when(i == j // ratio)
        def _():
            dk_acc[...] = jnp.zeros_like(dk_acc)
            dv_acc[...] = jnp.zeros_like(dv_acc)
            dcc_acc[...] = jnp.zeros_like(dcc_acc)

        lt64 = _lane_lt64((tk, LANES))
        _score_tiles(q_ref, k_ref, ck_ref, st_sc, tk)
        for p in range(N_HEADS // 2):
            lanes = slice(p * LANES, (p + 1) * LANES)
            dpt2 = _dot_nt(_stack_heads(v_ref[:, lanes], lt64), do_ref[:, lanes].astype(BF16))
            dpt_sc[2 * p] = dpt2[:tk]
            dpt_sc[2 * p + 1] = dpt2[tk:]
        _mask_diagonal(st_sc, i, j, tq, tk)

        pt = jnp.exp(st_sc[...] - lse_ref[...])
        dst = pt * (dpt_sc[...] - dl_ref[...])
        p_sc[...] = pt.astype(BF16)
        ds_sc[...] = dst.astype(BF16)
        dcr_ref[i] += jnp.sum(dst, axis=1, keepdims=True)
        col_sums = jnp.sum(dst, axis=2, keepdims=True)
        lane = lax.broadcasted_iota(jnp.int32, (tk, LANES), 1)
        dcc = jnp.zeros((tk, LANES), F32)
        for h in range(N_HEADS):
            dcc = jnp.where(lane == h, -col_sums[h], dcc)
        dcc_acc[...] += dcc

        for p in range(N_HEADS // 2):
            lanes = slice(p * LANES, (p + 1) * LANES)
            q_pair = q_ref[:, lanes] * jnp.asarray(scale, BF16)
            dv2 = _dot(p_sc[2 * p:2 * p + 2].reshape(2 * tk, tq), do_ref[:, lanes].astype(BF16))
            dv_acc[:, lanes] += jnp.where(lt64, dv2[:tk], dv2[tk:])
            dk2 = _dot(ds_sc[2 * p:2 * p + 2].reshape(2 * tk, tq), q_pair)
            dk_acc[:, lanes] += jnp.where(lt64, dk2[:tk], dk2[tk:])
            dq2 = _dot(kt_ref[lanes, :], jnp.concatenate([ds_sc[2 * p], ds_sc[2 * p + 1]], axis=1))
            top = lax.broadcasted_iota(jnp.int32, (LANES, tq), 0) < HEAD_DIM
            dq_t[i, lanes, :] += jnp.where(top, dq2[:, :tq], dq2[:, tq:])

        @pl.when(j == (i + 1) * ratio - 1)
        def _():
            rows = pl.ds(pl.multiple_of(i * tq, tq), tq)
            for p in range(N_HEADS // 2):
                lanes = slice(p * LANES, (p + 1) * LANES)
                dq_ref[rows, lanes] = (jnp.transpose(dq_t[i, lanes, :]) * scale).astype(BF16)

        @pl.when(i == nq - 1)
        def _():
            dk_ref[...] = dk_acc[...].astype(BF16)
            dv_ref[...] = dv_acc[...].astype(BF16)
            dcc_ref[...] = dcc_acc[...]

        if comm:
            @pl.when(n == len(steps) // 2)
            def _():
                _comm_phase(comm, "mid", *comm_refs)

            @pl.when(n == len(steps) - 1)
            def _():
                _comm_phase(comm, "finish", *comm_refs)

    q_map = lambda n, jt, it: (it[n], 0)
    q_stat = lambda n, jt, it: (0, 0, it[n])
    k_map = lambda n, jt, it: (jt[n], 0)
    tile = (N_HEADS, tk, tq)
    res = pl.pallas_call(
        body,
        name="attn_bwd",
        grid_spec=pltpu.PrefetchScalarGridSpec(
            num_scalar_prefetch=2,
            grid=(len(steps),),
            in_specs=[
                pl.BlockSpec((tq, FOX_W), q_map),
                pl.BlockSpec((tk, FOX_W), lambda n, jt, it: (jt[n], 1)),
                pl.BlockSpec((tk, FOX_W), lambda n, jt, it: (jt[n], 2)),
                pl.BlockSpec((FOX_W, tk), lambda n, jt, it: (0, jt[n])),
                pl.BlockSpec((tq, FOX_W), q_map),
                pl.BlockSpec((tk, LANES), k_map),
                pl.BlockSpec((N_HEADS, 1, tq), q_stat),
                pl.BlockSpec((N_HEADS, 1, tq), q_stat),
            ] + [_ANY] * n_ci,
            out_specs=[
                pl.BlockSpec((s_len, FOX_W), lambda n, jt, it: (0, 0)),
                pl.BlockSpec((tk, FOX_W), k_map),
                pl.BlockSpec((tk, FOX_W), k_map),
                pl.BlockSpec((nq, N_HEADS, 1, tq), lambda n, jt, it: (0, 0, 0, 0)),
                pl.BlockSpec((tk, LANES), k_map),
            ] + [_ANY] * n_co,
            scratch_shapes=[pltpu.VMEM((nq, FOX_W, tq), F32), pltpu.VMEM((tk, FOX_W), F32), pltpu.VMEM((tk, FOX_W), F32),
                            pltpu.VMEM((tk, LANES), F32), pltpu.VMEM(tile, F32), pltpu.VMEM(tile, F32),
                            pltpu.VMEM(tile, BF16), pltpu.VMEM(tile, BF16)] + comm_sems,
        ),
        out_shape=[_sds((s_len, FOX_W), BF16), _sds((s_len, FOX_W), BF16), _sds((s_len, FOX_W), BF16),
                   _sds((nq, N_HEADS, 1, tq), F32), _sds((s_len, LANES), F32)] + comm_outs,
        compiler_params=_params(("arbitrary",)),
    )(j_tab, i_tab, qkv, qkv, qkv, k_t, do, c_col, lse_row, delta_row, *comm_arrays)
    return res[:5], res[5:]


def _forget_bwd(dc_rows, dc_cols, prest, b_f_pad, tc):
    s_len = dc_rows.shape[0]
    nb = s_len // tc

    def body(dcr_ref, dc_ref, f_ref, b_ref, df_ref, db_ref, carry):
        @pl.when(pl.program_id(0) == 0)
        def _():
            carry[...] = jnp.zeros_like(carry)
            db_ref[...] = jnp.zeros_like(db_ref)

        row = lax.broadcasted_iota(jnp.int32, (tc, tc), 0)
        col = lax.broadcasted_iota(jnp.int32, (tc, tc), 1)
        tri = (row <= col).astype(F32)
        dlogf = _dot_f32(tri, dcr_ref[...] + dc_ref[...]) + carry[...]
        carry[...] = dlogf[0:1, :]
        z = f_ref[...] + b_ref[...]
        lane = lax.broadcasted_iota(jnp.int32, (tc, LANES), 1)
        dz = jnp.where(lane < N_HEADS, dlogf * _sigmoid(-z), 0.0)
        df_ref[...] = dz.astype(BF16)
        db_ref[...] += jnp.sum(dz, axis=0, keepdims=True)

    rev = lambda i: (nb - 1 - i, 0)
    return pl.pallas_call(
        body,
        name="forget_bwd",
        grid=(nb,),
        in_specs=[
            pl.BlockSpec((tc, LANES), rev),
            pl.BlockSpec((tc, LANES), rev),
            pl.BlockSpec((tc, LANES), rev),
            pl.BlockSpec((1, LANES), lambda i: (0, 0)),
        ],
        out_specs=[pl.BlockSpec((tc, LANES), rev), pl.BlockSpec((1, LANES), lambda i: (0, 0))],
        out_shape=[_sds((s_len, LANES), BF16), _sds((1, LANES), F32)],
        scratch_shapes=[pltpu.VMEM((1, LANES), F32)],
        compiler_params=_params(("arbitrary",)),
    )(dc_rows, dc_cols, prest, b_f_pad)


def _pair_sum(name, g4, recv, idx, tr):
    _, _, r, c = g4.shape

    def body(idx_ref, g_ref, r_ref, p16_ref, own_ref):
        k = pl.program_id(1)
        s = g_ref[...] + r_ref[...]
        p16_ref[...] = s.astype(BF16)

        @pl.when(k == idx_ref[1])
        def _():
            own_ref[...] = s

    return pl.pallas_call(
        body,
        name=name,
        grid_spec=pltpu.PrefetchScalarGridSpec(
            num_scalar_prefetch=1,
            grid=(r // tr, 4),
            in_specs=[
                pl.BlockSpec((None, None, tr, c), lambda i, k, idx: (k, idx[0], i, 0)),
                pl.BlockSpec((None, tr, c), lambda i, k, idx: (k, i, 0)),
            ],
            out_specs=[
                pl.BlockSpec((None, tr, c), lambda i, k, idx: (k, i, 0)),
                pl.BlockSpec((tr, c), lambda i, k, idx: (i, 0)),
            ],
        ),
        out_shape=[_sds((4, r, c), BF16), _sds((r, c), F32)],
        compiler_params=_params(("parallel", "arbitrary")),
    )(idx, g4, recv)


def _adamw_math(w, g, m, v):
    m2 = ADAM_B1 * m + (1.0 - ADAM_B1) * g
    v2 = ADAM_B2 * v + (1.0 - ADAM_B2) * (g * g)
    m_hat = m2 / (1.0 - ADAM_B1 ** ADAM_STEP)
    v_hat = v2 / (1.0 - ADAM_B2 ** ADAM_STEP)
    delta = -ADAM_LR * (m_hat / (jnp.sqrt(v_hat) + ADAM_EPS) + ADAM_WD * w)
    return delta, m2, v2


def _adamw_shard(name, own, recv, w, m, v, tr):
    r, c = own.shape

    def body(own_ref, recv_ref, w_ref, m_ref, v_ref, g_ref, d_ref, m2_ref, v2_ref):
        g = own_ref[...]
        for k in range(3):
            g = g + recv_ref[k].astype(F32)
        delta, m2, v2 = _adamw_math(w_ref[...], g, m_ref[...], v_ref[...])
        g_ref[...] = g
        d_ref[...] = delta
        m2_ref[...] = m2
        v2_ref[...] = v2

    spec = pl.BlockSpec((tr, c), lambda i: (i, 0))
    return pl.pallas_call(
        body,
        name=name,
        grid=(r // tr,),
        in_specs=[spec, pl.BlockSpec((3, tr, c), lambda i: (0, i, 0)), spec, spec, spec],
        out_specs=[spec] * 4,
        out_shape=[_sds((r, c), F32)] * 4,
        compiler_params=_params(("parallel",)),
    )(own, recv, w, m, v)


def _adamw_small(gathered, w, m, v):
    _, r, _ = gathered.shape

    def body(ga_ref, w_ref, m_ref, v_ref, g_ref, d_ref, m2_ref, v2_ref):
        g = ga_ref[0]
        for k in range(1, N_DEV):
            g = g + ga_ref[k]
        delta, m2, v2 = _adamw_math(w_ref[...], g, m_ref[...], v_ref[...])
        g_ref[...] = g
        d_ref[...] = delta
        m2_ref[...] = m2
        v2_ref[...] = v2

    spec = pl.BlockSpec((r, LANES), lambda i: (0, 0))
    return pl.pallas_call(
        body,
        name="adamw_small",
        grid=(1,),
        in_specs=[pl.BlockSpec((N_DEV, r, LANES), lambda i: (0, 0, 0)), spec, spec, spec],
        out_specs=[spec] * 4,
        out_shape=[_sds((r, LANES), F32)] * 4,
        compiler_params=_params(("arbitrary",)),
    )(gathered, w, m, v)


_SMALL = (("w_sgu", (1, SGU_G, SGU_LEN, SGU_LEN)), ("b_sgu", (1, SGU_G, SGU_LEN)), ("norm2_g", (1, D_MODEL)),
          ("normf_g", (D_MODEL,)), ("ln_v_g", (1, SGU_W)), ("ln_v_b", (1, SGU_W)), ("b_f", (1, N_HEADS)),
          ("norm1_g", (1, D_MODEL)), ("loss", ()))
_N_EARLY = 6


def _pack_rows(values):
    rows = []
    for val in values:
        flat = val.reshape(-1).astype(F32)
        pad = (-flat.shape[0]) % LANES
        rows.append(jnp.pad(flat, (0, pad)).reshape(-1, LANES))
    packed = jnp.concatenate(rows, axis=0)
    return jnp.pad(packed, ((0, (-packed.shape[0]) % 8), (0, 0)))


def _pack_small(values):
    return jnp.concatenate([_pack_rows(values[:_N_EARLY]), _pack_rows(values[_N_EARLY:])], axis=0)


def _unpack_small(packed):
    out, row = [], 0
    for k, (_, shape) in enumerate(_SMALL):
        if k == _N_EARLY:
            row += (-row) % 8
        size = math.prod(shape)
        n_rows = -(-size // LANES)
        out.append(packed[row:row + n_rows].reshape(-1)[:size].reshape(shape))
        row += n_rows
    return out


def kernel(x, norm1_g, w_in, b_f, ln_v_g, ln_v_b, w_sgu, b_sgu, w_a, w_b, w_o, norm2_g, w_up, w_down, normf_g, loss_target, m_norm1_g, m_w_in, m_b_f, m_ln_v_g, m_ln_v_b, m_w_sgu, m_b_sgu, m_w_a, m_w_b, m_w_o, m_norm2_g, m_w_up, m_w_down, m_normf_g, v_norm1_g, v_w_in, v_b_f, v_ln_v_g, v_ln_v_b, v_w_sgu, v_b_sgu, v_w_a, v_w_b, v_w_o, v_norm2_g, v_w_up, v_w_down, v_normf_g):
    xs = x[0]
    target = loss_target[0]
    s_len, d = xs.shape
    tm = min(512, s_len)
    tl = min(1024, s_len)
    tr = min(256, s_len)
    ta = min(512, s_len)
    tc = min(256, s_len)

    big = (w_in[0], w_a[0], w_b[0], w_o[0], w_up[0], w_down[0])
    h, h_t, r1, (w_in_g,) = _rms_fwd("rms1", xs, norm1_g, tm, comm=[_gather_plan(w_in[0].astype(BF16))])
    w_in_f = jnp.transpose(w_in_g, (1, 0, 2)).reshape(d, IN_COLS)
    later = big[1:]
    later_flat = jnp.concatenate([w.reshape(-1).astype(BF16) for w in later]).reshape(-1, D_MODEL)
    later_plan = _gather_plan(later_flat)

    def unflatten(gathered):
        row, full = 0, []
        for w, col_sharded in zip(later, (True, True, False, True, False)):
            n_rows = w.size // D_MODEL
            blk = gathered[:, row:row + n_rows].reshape((N_DEV,) + w.shape)
            row += n_rows
            if col_sharded:
                full.append(jnp.transpose(blk, (1, 0, 2)).reshape(w.shape[0], N_DEV * w.shape[1]))
            else:
                full.append(blk.reshape(N_DEV * w.shape[0], w.shape[1]))
        return full

    w_qkv = w_in_f[:, :QKV_W]
    f_lo = QKV_W
    u_lo = f_lo + N_HEADS
    w_rest = jnp.concatenate([w_in_f[:, u_lo:], jnp.pad(w_in_f[:, f_lo:u_lo], ((0, 0), (0, LANES - N_HEADS)))], axis=1)

    chunk_id = jnp.arange(SGU_LEN) // CHUNK
    sgu_mask = chunk_id[None, :] <= chunk_id[:, None]
    w_masked = jnp.where(sgu_mask[None], w_sgu[0], 0.0)
    w_stack = w_masked.reshape(SGU_G // 2, 2 * SGU_LEN, SGU_LEN).astype(BF16)
    wt_stack = jnp.transpose(w_masked, (0, 2, 1)).reshape(SGU_G // 2, 2 * SGU_LEN, SGU_LEN).astype(BF16)
    b_pair = jnp.transpose(jnp.repeat(b_sgu[0], SGU_W // SGU_G, axis=0))
    b_f_pad = jnp.pad(b_f, ((0, 0), (0, LANES - N_HEADS)))
    head_sel = (jnp.arange(FOX_W)[:, None] // HEAD_DIM == jnp.arange(LANES)[None, :]).astype(F32)

    def store(dtype):
        def epi(accs, ex, out):
            out[0][...] = accs[0].astype(dtype)
        return epi

    def qkv_epi(accs, ex, out):
        tile = accs[0].astype(BF16)
        out[0][...] = tile
        for col, ref in ((1, out[1]), (2, out[2])):
            @pl.when(pl.program_id(1) == col)
            def _():
                ref[...] = jnp.transpose(tile)

    t_spec = pl.BlockSpec((FOX_W, tl), lambda i, j: (0, i))
    qkv, k_t, v_t = _mm("proj_qkv", [(h, w_qkv, False, None)], [],
                        [(_sds((s_len, QKV_W), BF16), _tile(tl, FOX_W)), (_sds((FOX_W, s_len), BF16), t_spec),
                         (_sds((FOX_W, s_len), BF16), t_spec)],
                        qkv_epi, m=s_len, tm=tl, n=QKV_W, tn=FOX_W, arbitrary=True)
    rest_tn = 640
    f_tile, f_lane = F_OFF // rest_tn, F_OFF % rest_tn

    def rest_epi(accs, ex, out):
        out[0][...] = accs[0].astype(BF16)

        @pl.when(pl.program_id(1) == f_tile)
        def _():
            out[1][...] = accs[0][:, f_lane:f_lane + LANES]

    prest, f_logit = _mm("proj_rest", [(h, w_rest, False, None)], [],
                         [(_sds((s_len, REST_W), BF16), _tile(tl, rest_tn)), (_sds((s_len, LANES), F32), _row(tl, LANES))],
                         rest_epi, m=s_len, tm=tl, n=REST_W, tn=rest_tn, arbitrary=True)

    c_col = _forget_cumsum(f_logit, b_f_pad, tc)
    o, o_t, lse_row, (later_g,) = _attn_fwd(qkv, v_t, c_col, ta, ta, comm=[later_plan])
    w_a_f, w_b_f, w_o_f, w_up_f, w_down_f = unflatten(later_g)
    sg, sg_t = _sgu_fwd(prest, ln_v_g, ln_v_b, w_stack, b_pair, tm)

    def merge_epi(accs, ex, out):
        ya, yb = accs
        sa, sb = _sigmoid(ex[0][...].astype(F32)), _sigmoid(ex[1][...].astype(F32))
        merged = (sa * ya + sb * yb).astype(BF16)
        out[0][...] = merged
        out[1][...] = ya.astype(BF16)
        out[2][...] = yb.astype(BF16)
        out[3][...] = jnp.transpose(merged)

    merged, ya, yb, merged_t = _mm(
        "merge", [(o, w_a_f, False, None), (sg, w_b_f, False, None)],
        [(prest, _tile(tm, d, GA_OFF // d)), (prest, _tile(tm, d, GB_OFF // d))],
        [(_sds((s_len, d), BF16), _tile(tm, d))] * 3 + [(_sds((d, s_len), BF16), _tile_t(tm, d))],
        merge_epi, m=s_len, tm=tm, n=d, tn=d)

    def resid_epi(accs, ex, out):
        out[0][...] = ex[0][...] + accs[0]

    (x1,) = _mm("out_proj", [(merged, w_o_f, False, None)], [(xs, _tile(tm, d))],
                [(_sds((s_len, d), F32), _tile(tm, d))], resid_epi, m=s_len, tm=tm, n=d, tn=d)

    h2, h2_t, r2, _ = _rms_fwd("rms2", x1, norm2_g, tm)

    def up_epi(accs, ex, out):
        a = accs[0]
        out[0][...] = a.astype(BF16)
        act = jnp.square(jnp.maximum(a, 0.0)).astype(BF16)
        out[1][...] = act
        out[2][...] = jnp.transpose(act)

    a_up, act, act_t = _mm(
        "mlp_up", [(h2, w_up_f, False, None)], [],
        [(_sds((s_len, D_FF), BF16), _tile(tl, 512)), (_sds((s_len, D_FF), BF16), _tile(tl, 512)),
         (_sds((D_FF, s_len), BF16), _tile_t(tl, 512))],
        up_epi, m=s_len, tm=tl, n=D_FF, tn=512)

    def first_step():
        return jnp.logical_and(pl.program_id(0) == 0, pl.program_id(1) == 0)

    def accumulate(ref, val):
        @pl.when(first_step())
        def _():
            ref[...] = val

        @pl.when(jnp.logical_not(first_step()))
        def _():
            ref[...] += val

    def final_epi(accs, ex, out):
        x1_ref, t_ref, g_ref = ex
        x2 = x1_ref[...] + accs[0]
        rf = lax.rsqrt(jnp.mean(x2 * x2, axis=-1, keepdims=True) + EPS)
        xh = x2 * rf
        gf = g_ref[...]
        err = xh * gf - t_ref[...]
        dy = err * (1.0 / d)
        dx2 = _rms_bwd(xh, rf, gf, dy)
        out[0][...] = dx2
        accumulate(out[1], jnp.sum(dy * xh, axis=0, keepdims=True))
        part = 0.5 * jnp.sum(jnp.sum(err * err, axis=-1, keepdims=True) * (1.0 / d), axis=0, keepdims=True)
        accumulate(out[2], jnp.broadcast_to(part, (1, LANES)))
        out[3][...] = dx2.astype(BF16)

    gf2 = normf_g.reshape(1, d)
    dx2, g_normf, loss_part, dx2_16 = _mm(
        "mlp_down_loss", [(act, w_down_f, False, None)],
        [(x1, _row(tr, d)), (target, _row(tr, d)), (gf2, _whole((1, d)))],
        [(_sds((s_len, d), F32), _row(tr, d)), (_sds((1, d), F32), _whole((1, d))), (_sds((1, LANES), F32), _whole((1, LANES))),
         (_sds((s_len, d), BF16), _row(tr, d))],
        final_epi, m=s_len, tm=tr, n=d, tn=d, arbitrary=True)

    def dact_epi(accs, ex, out):
        out[0][...] = (accs[0] * (2.0 * jnp.maximum(ex[0][...], 0.0))).astype(BF16)

    (da,) = _mm("mlp_down_bwd", [(dx2_16, w_down_f, True, None)], [(a_up, _tile(tl, 512))],
                [(_sds((s_len, D_FF), BF16), _tile(tl, 512))], dact_epi, m=s_len, tm=tl, n=D_FF, tn=512)
    g_down = _grad_w("grad_w_down", act_t, dx2_16, tk=1024, tn=d, ts=tl)
    g_up = _grad_w("grad_w_up", h2_t, da, tk=d, tn=1024, ts=tl, block_cols=D_FF // N_DEV)

    def dh2_epi(accs, ex, out):
        x1_ref, r_ref, g_ref, dx2_ref = ex
        r = r_ref[...]
        xh = x1_ref[...] * r
        dh2 = accs[0]
        out[0][...] = dx2_ref[...] + _rms_bwd(xh, r, g_ref[...], dh2)
        accumulate(out[1], jnp.sum(dh2 * xh, axis=0, keepdims=True))

    my_c = lax.axis_index("c")
    my_chip = 2 * lax.axis_index("x") + lax.axis_index("y")
    idx = jnp.stack([my_c, my_chip]).astype(jnp.int32)
    parts16, owns = {}, {}

    def split_cores(g8):
        return g8.reshape((4, 2) + g8.shape[1:])

    def pair_sums(names, grads4, from_sibling):
        for name, g4, recv in zip(names, grads4, from_sibling):
            parts16[name], owns[name] = _pair_sum("grad_pair_sum_" + name, g4, recv, idx, min(512, g4.shape[2]))

    grads4_mlp = [split_cores(g_up), split_cores(g_down.reshape(N_DEV, D_FF // N_DEV, d))]
    (dx1, g_norm2), from_sibling = _mm(
        "mlp_up_bwd", [(da, w_up_f, True, None)],
        [(x1, _row(tr, d)), (r2, _row(tr, 1)), (norm2_g, _whole((1, d))), (dx2, _row(tr, d))],
        [(_sds((s_len, d), F32), _row(tr, d)), (_sds((1, d), F32), _whole((1, d)))],
        dh2_epi, m=s_len, tm=tr, n=d, tn=d, arbitrary=True, comm=[_pair_exchange_plan(grads4_mlp)])
    pair_sums(("w_up", "w_down"), grads4_mlp, from_sibling)

    def dmerge_epi(accs, ex, out):
        dm = accs[0]
        sa, sb = _sigmoid(ex[0][...].astype(F32)), _sigmoid(ex[1][...].astype(F32))
        out[0][...] = (dm * sa).astype(BF16)
        out[1][...] = (dm * sb).astype(BF16)
        out[2][...] = (dm * ex[2][...] * sa * (1.0 - sa)).astype(BF16)
        out[3][...] = (dm * ex[3][...] * sb * (1.0 - sb)).astype(BF16)

    dya, dyb, dga, dgb = _mm(
        "out_proj_bwd", [(dx1, w_o_f, True, None)],
        [(prest, _tile(tm, d, GA_OFF // d)), (prest, _tile(tm, d, GB_OFF // d)), (ya, _tile(tm, d)), (yb, _tile(tm, d))],
        [(_sds((s_len, d), BF16), _tile(tm, d))] * 4, dmerge_epi, m=s_len, tm=tm, n=d, tn=d)
    g_o = _grad_w("grad_w_o", merged_t, dx1, tk=d, tn=d, ts=tl).reshape(N_DEV, d // N_DEV, d)
    def col_blocks(g):
        return jnp.transpose(g.reshape(g.shape[0], N_DEV, g.shape[1] // N_DEV), (1, 0, 2))

    g_a = col_blocks(_grad_w("grad_w_a", o_t, dya, tk=FOX_W, tn=d, ts=tl))
    g_b = col_blocks(_grad_w("grad_w_b", sg_t, dyb, tk=SGU_W, tn=d, ts=tl))

    def do_epi(accs, ex, out):
        do = accs[0]
        out[0][...] = do
        out[1][...] = _dot_f32(do * ex[0][...], ex[1][...])

    grads4_mix = [split_cores(g) for g in (g_a, g_b, g_o)]
    (do, delta), from_sibling = _mm(
        "attn_out_bwd", [(dya, w_a_f, True, None)], [(o, _row(tm, FOX_W)), (head_sel, _whole((FOX_W, LANES)))],
        [(_sds((s_len, FOX_W), F32), _row(tm, FOX_W)), (_sds((s_len, LANES), F32), _row(tm, LANES))],
        do_epi, m=s_len, tm=tm, n=FOX_W, tn=FOX_W, comm=[_pair_exchange_plan(grads4_mix)])
    pair_sums(("w_a", "w_b", "w_o"), grads4_mix, from_sibling)
    (dsg,) = _mm("sgu_out_bwd", [(dyb, w_b_f, True, None)], [], [(_sds((s_len, SGU_W), F32), _tile(tm, SGU_W))],
                 store(F32), m=s_len, tm=tm, n=SGU_W, tn=SGU_W)

    du, dsv, dw_pairs, db_pos, g_ln_g, g_ln_b = _sgu_bwd(prest, dsg, ln_v_g, ln_v_b, w_stack, wt_stack, b_pair, tm)
    g_w_sgu = jnp.where(sgu_mask[None], dw_pairs.reshape(SGU_G, SGU_LEN, SGU_LEN), 0.0)
    g_b_sgu = jnp.transpose(jnp.sum(db_pos.reshape(SGU_LEN, SGU_G, SGU_W // SGU_G), axis=-1))

    delta_row = jnp.transpose(delta[:, :N_HEADS]).reshape(N_HEADS, 1, s_len)
    early = ("w_a", "w_b", "w_o", "w_up", "w_down")
    small_early = _pack_rows((g_w_sgu, g_b_sgu, g_norm2, g_normf, g_ln_g, g_ln_b))
    (dq, dk, dv, dc_rows_blk, dc_cols), (small_early_all, *from_chips_early) = _attn_bwd(
        qkv, k_t, do, c_col, lse_row, delta_row, ta, ta,
        comm=[_gather_plan(small_early), _chip_exchange_plan([parts16[n] for n in early])])
    dc_rows = jnp.transpose(dc_rows_blk.reshape(s_len // ta, N_HEADS, ta), (0, 2, 1)).reshape(s_len, N_HEADS)
    dc_rows = jnp.pad(dc_rows, ((0, 0), (0, LANES - N_HEADS)))
    dfl, g_bf = _forget_bwd(dc_rows, dc_cols, f_logit, b_f_pad, tc)

    dp = (dq, dk, dv, du, dsv, dga, dgb)
    g_in_cols = [_grad_w("grad_w_in_%d" % k, h_t, seg, tk=d, tn=seg.shape[1], ts=tl) for k, seg in enumerate(dp)]
    g_f = _grad_w("grad_w_in_f", h_t, dfl, tk=d, tn=LANES, ts=tl)[:, :N_HEADS]
    g_in_full = jnp.concatenate(g_in_cols[:3] + [g_f] + g_in_cols[3:], axis=1)
    g_in = jnp.transpose(g_in_full.reshape(d, N_DEV, IN_SHARD), (1, 0, 2))

    def dx_epi(accs, ex, out):
        x_ref, r_ref, g_ref, dx1_ref = ex
        dh = accs[0]
        for extra in accs[1:]:
            dh = dh + extra
        r = r_ref[...]
        xh = x_ref[...] * r
        out[0][...] = dx1_ref[...] + _rms_bwd(xh, r, g_ref[...], dh)
        accumulate(out[1], jnp.sum(dh * xh, axis=0, keepdims=True))

    rest_cols = ((du, U_OFF, 512), (dsv, SV_OFF, 512), (dga, GA_OFF, 1024), (dgb, GB_OFF, 1024), (dfl, F_OFF, LANES))
    dx_pairs = [(seg, w_qkv, True, (512 * k, 512 * (k + 1))) for k, seg in enumerate((dq, dk, dv))]
    dx_pairs += [(seg, w_rest, True, (lo, lo + width)) for seg, lo, width in rest_cols]
    grads4_in = [split_cores(g_in)]
    pair_sums(("w_in",), grads4_in, _run_comm("grad_pair_exchange_w_in", [_pair_exchange_plan(grads4_in)]))
    (grad_x, g_norm1), (from_chips_in,) = _mm(
        "proj_bwd", dx_pairs,
        [(xs, _row(tr, d)), (r1, _row(tr, 1)), (norm1_g, _whole((1, d))), (dx1, _row(tr, d))],
        [(_sds((s_len, d), F32), _row(tr, d)), (_sds((1, d), F32), _whole((1, d)))],
        dx_epi, m=s_len, tm=tr, n=d, tn=d, arbitrary=True, comm=[_chip_exchange_plan([parts16["w_in"]])])
    small_late = _pack_rows((g_bf[:, :N_HEADS], g_norm1, loss_part[0, 0]))
    (small_late_all,) = _run_comm("gather_last_grads", [_gather_plan(small_late)])
    small_all = jnp.concatenate([small_early_all, small_late_all], axis=1)
    from_chips = dict(zip(early, from_chips_early), w_in=from_chips_in)

    names = ("w_in", "w_a", "w_b", "w_o", "w_up", "w_down")
    moments_m = (m_w_in, m_w_a, m_w_b, m_w_o, m_w_up, m_w_down)
    moments_v = (v_w_in, v_w_a, v_w_b, v_w_o, v_w_up, v_w_down)
    big_out = {}
    for name, w, m, v in zip(names, big, moments_m, moments_v):
        own = owns[name]
        res = _adamw_shard("adamw_" + name, own, from_chips[name], w, m[0], v[0], min(512, own.shape[0]))
        big_out[name] = [t[None] for t in res]

    zero = jnp.zeros((), F32)
    small_w = _pack_small((w_sgu, b_sgu, norm2_g, normf_g, ln_v_g, ln_v_b, b_f, norm1_g, zero))
    small_m = _pack_small((m_w_sgu, m_b_sgu, m_norm2_g, m_normf_g, m_ln_v_g, m_ln_v_b, m_b_f, m_norm1_g, zero))
    small_v = _pack_small((v_w_sgu, v_b_sgu, v_norm2_g, v_normf_g, v_ln_v_g, v_ln_v_b, v_b_f, v_norm1_g, zero))
    small_res = [_unpack_small(t) for t in _adamw_small(small_all, small_w, small_m, small_v)]
    small_names = [n for n, _ in _SMALL]
    small_out = {n: [res[k] for res in small_res] for k, n in enumerate(small_names)}
    loss = small_out["loss"][0]

    order = ("norm1_g", "w_in", "b_f", "ln_v_g", "ln_v_b", "w_sgu", "b_sgu", "w_a", "w_b", "w_o", "norm2_g", "w_up",
             "w_down", "normf_g")
    table = {**big_out, **small_out}
    outs = [loss, grad_x[None]]
    for kind in range(4):
        outs += [table[n][kind] for n in order]
    return tuple(outs)
```

```python
import math

import jax
import jax.numpy as jnp
from jax import lax
from jax.experimental import pallas as pl
from jax.experimental.pallas import tpu as pltpu

F32 = jnp.float32
BF16 = jnp.bfloat16

N_DEV = 8
D_MODEL = 1024
N_HEADS = 8
HEAD_DIM = 64
FOX_W = N_HEADS * HEAD_DIM
SGU_G = 8
SGU_W = 512
SGU_LEN = 128
CHUNK = 64
D_FF = 4 * D_MODEL
IN_COLS = 3 * FOX_W + N_HEADS + 2 * SGU_W + 2 * D_MODEL
IN_SHARD = IN_COLS // N_DEV
LANES = 128
QKV_W = 3 * FOX_W
U_OFF, SV_OFF, GA_OFF, GB_OFF, F_OFF = 0, 512, 1024, 2048, 3072
REST_W = F_OFF + LANES
EPS = 1e-6
NEG = -1e30

ADAM_LR = 0.001
ADAM_B1 = 0.9
ADAM_B2 = 0.999
ADAM_EPS = 1e-08
ADAM_WD = 0.01
ADAM_STEP = 10

VMEM_LIMIT = 56 * 1024 * 1024
MESH = pl.DeviceIdType.MESH


def _params(sem=None):
    return pltpu.CompilerParams(dimension_semantics=sem, vmem_limit_bytes=VMEM_LIMIT)


def _dot(a, b):
    return jnp.dot(a, b, preferred_element_type=F32)


def _dot_nt(a, b):
    return lax.dot_general(a, b, (((1,), (1,)), ((), ())), preferred_element_type=F32)


def _dot_f32(a, b):
    return jnp.dot(a, b, preferred_element_type=F32, precision=lax.Precision.HIGHEST)


def _sigmoid(x):
    return 1.0 / (1.0 + jnp.exp(-x))


def _log_sigmoid(z):
    return jnp.minimum(z, 0.0) - jnp.log(1.0 + jnp.exp(-jnp.abs(z)))


_GELU_K = math.sqrt(2.0 / math.pi)
_GELU_C = 0.044715


def _gelu(x):
    t = jnp.tanh(_GELU_K * (x + _GELU_C * (x * x * x)))
    return 0.5 * x * (1.0 + t)


def _gelu_grad(x):
    x2 = x * x
    t = jnp.tanh(_GELU_K * (x + _GELU_C * (x2 * x)))
    return 0.5 * (1.0 + t) + 0.5 * x * (1.0 - t * t) * (_GELU_K * (1.0 + 3.0 * _GELU_C * x2))


def _rms_bwd(xh, r, g, dy):
    gy = dy * g
    return r * (gy - xh * jnp.mean(xh * gy, axis=-1, keepdims=True))


def _lane_lt64(shape):
    return lax.broadcasted_iota(jnp.int32, shape, len(shape) - 1) < HEAD_DIM


class _Comm:
    def __init__(self, arrays, out_shapes, sems, start, finish, mid=None):
        self.arrays, self.out_shapes, self.sems = list(arrays), list(out_shapes), list(sems)
        self.start, self.mid, self.finish = start, mid, finish


def _comm_phase(plans, phase, in_refs, out_refs, sem_refs):
    ia = io = ks = 0
    for plan in plans:
        na, no, ns = len(plan.arrays), len(plan.out_shapes), len(plan.sems)
        fn = getattr(plan, phase)
        if fn is not None:
            fn(in_refs[ia:ia + na], out_refs[io:io + no], sem_refs[ks:ks + ns])
        ia, io, ks = ia + na, io + no, ks + ns


def _comm_operands(plans):
    arrays = [a for plan in plans for a in plan.arrays]
    out_shapes = [o for plan in plans for o in plan.out_shapes]
    sems = [s for plan in plans for s in plan.sems]
    return arrays, out_shapes, sems


_ANY = pl.BlockSpec(memory_space=pl.ANY)


def _run_comm(name, plans):
    arrays, out_shapes, sems = _comm_operands(plans)
    n_in, n_out = len(arrays), len(out_shapes)

    def body(*refs):
        parts = refs[:n_in], refs[n_in:n_in + n_out], refs[n_in + n_out:]
        for phase in ("start", "mid", "finish"):
            _comm_phase(plans, phase, *parts)

    return pl.pallas_call(
        body, name=name, out_shape=out_shapes, in_specs=[_ANY] * n_in, out_specs=[_ANY] * n_out, scratch_shapes=sems,
    )(*arrays)


def _gather_plan(shard):
    def setup(ins, outs, sems):
        (x_ref,), (out_ref,), (send_sems, recv_sems, local_sem) = ins, outs, sems
        x, y, c = lax.axis_index("x"), lax.axis_index("y"), lax.axis_index("c")
        me, sibling = (x, y, c), (x, y, 1 - c)
        chips = [(1 - x, y), (x, 1 - y), (1 - x, 1 - y)]

        def rows(px, py, pc):
            return out_ref.at[4 * px + 2 * py + pc]

        def copy(k, block, to, src=None):
            return pltpu.make_async_remote_copy(
                src_ref=rows(*block) if src is None else src,
                dst_ref=rows(*block),
                send_sem=send_sems.at[k],
                recv_sem=recv_sems.at[k],
                device_id=to,
                device_id_type=MESH,
            )

        mine = pltpu.make_async_copy(x_ref, rows(*me), local_sem)
        first = [copy(0, me, sibling, src=x_ref)]
        first += [copy(1 + j, me, (*chip, c), src=x_ref) for j, chip in enumerate(chips)]
        passed = [copy(4 + j, (*chip, c), sibling) for j, chip in enumerate(chips)]
        landed = [copy(1 + j, (*chip, c), me) for j, chip in enumerate(chips)]
        from_sibling = [copy(0, sibling, me)] + [copy(4 + j, (*chip, 1 - c), me) for j, chip in enumerate(chips)]
        return mine, first, passed, landed, from_sibling

    def start(ins, outs, sems):
        mine, first, _, _, _ = setup(ins, outs, sems)
        mine.start()
        for cp in first:
            cp.start()

    def mid(ins, outs, sems):
        _, _, passed, landed, _ = setup(ins, outs, sems)
        for arrived, onward in zip(landed, passed):
            arrived.wait_recv()
            onward.start()

    def finish(ins, outs, sems):
        mine, first, passed, _, from_sibling = setup(ins, outs, sems)
        for cp in from_sibling:
            cp.wait_recv()
        for cp in first + passed:
            cp.wait_send()
        mine.wait()

    return _Comm([shard], [jax.ShapeDtypeStruct((N_DEV,) + shard.shape, shard.dtype)],
                 [pltpu.SemaphoreType.DMA((7,)), pltpu.SemaphoreType.DMA((7,)), pltpu.SemaphoreType.DMA],
                 start, finish, mid)


def _start_all(copies):
    for cp in copies:
        cp.start()


def _wait_all(copies):
    for cp in copies:
        cp.wait_recv()
    for cp in copies:
        cp.wait_send()


def _pair_exchange_plan(grads):
    n = len(grads)

    def copies(ins, outs, sems):
        send_sems, recv_sems = sems
        x, y, c = lax.axis_index("x"), lax.axis_index("y"), lax.axis_index("c")
        return [
            pltpu.make_async_remote_copy(
                src_ref=ins[k].at[:, 1 - c],
                dst_ref=outs[k],
                send_sem=send_sems.at[k],
                recv_sem=recv_sems.at[k],
                device_id=(x, y, 1 - c),
                device_id_type=MESH,
            )
            for k in range(n)
        ]

    return _Comm(grads, [jax.ShapeDtypeStruct((4,) + g.shape[2:], g.dtype) for g in grads],
                 [pltpu.SemaphoreType.DMA((n,)), pltpu.SemaphoreType.DMA((n,))],
                 lambda *refs: _start_all(copies(*refs)), lambda *refs: _wait_all(copies(*refs)))


def _chip_exchange_plan(parts):
    n = len(parts)

    def copies(ins, outs, sems):
        send_sems, recv_sems = sems
        x, y, c = lax.axis_index("x"), lax.axis_index("y"), lax.axis_index("c")
        chips = [(1 - x, y), (x, 1 - y), (1 - x, 1 - y)]
        return [
            pltpu.make_async_remote_copy(
                src_ref=ins[k].at[2 * px + py],
                dst_ref=outs[k].at[j],
                send_sem=send_sems.at[3 * k + j],
                recv_sem=recv_sems.at[3 * k + j],
                device_id=(px, py, c),
                device_id_type=MESH,
            )
            for k in range(n) for j, (px, py) in enumerate(chips)
        ]

    return _Comm(parts, [jax.ShapeDtypeStruct((3,) + p.shape[1:], p.dtype) for p in parts],
                 [pltpu.SemaphoreType.DMA((3 * n,)), pltpu.SemaphoreType.DMA((3 * n,))],
                 lambda *refs: _start_all(copies(*refs)), lambda *refs: _wait_all(copies(*refs)))


def _mm(name, pairs, extras, outs, epi, *, m, tm, n, tn, arbitrary=False, comm=()):
    nj = n // tn
    a_arrays, a_specs, b_arrays, b_specs, b_index = [], [], [], [], []
    for a, b, nt, cols in pairs:
        a_arrays.append(a)
        a_specs.append(pl.BlockSpec((tm, a.shape[1]), lambda i, j: (i, 0)))
        known = [k for k, other in enumerate(b_arrays) if other is b]
        if known:
            b_index.append(known[0])
            continue
        b_index.append(len(b_arrays))
        b_arrays.append(b)
        if cols is not None:
            assert nj == 1
            b_specs.append(pl.BlockSpec(b.shape, lambda i, j: (0, 0)))
        elif nt:
            b_specs.append(pl.BlockSpec((tn, b.shape[1]), lambda i, j: (j, 0)))
        else:
            b_specs.append(pl.BlockSpec((b.shape[0], tn), lambda i, j: (0, j)))
    comm_arrays, comm_outs, comm_sems = _comm_operands(comm)
    arrays = a_arrays + b_arrays + [arr for arr, _ in extras] + comm_arrays
    in_specs = a_specs + b_specs + [spec for _, spec in extras] + [_ANY] * len(comm_arrays)
    n_a, n_b, n_extras, n_ci, n_out, n_co = len(a_arrays), len(b_arrays), len(extras), len(comm_arrays), len(outs), len(comm_outs)
    ni = m // tm

    def body(*refs):
        a_refs = refs[:n_a]
        b_refs = refs[n_a:n_a + n_b]
        ex = refs[n_a + n_b:n_a + n_b + n_extras]
        n_in = n_a + n_b + n_extras + n_ci
        comm_refs = refs[n_in - n_ci:n_in], refs[n_in + n_out:n_in + n_out + n_co], refs[n_in + n_out + n_co:]
        out = refs[n_in:n_in + n_out]
        if comm:
            @pl.when(jnp.logical_and(pl.program_id(0) == 0, pl.program_id(1) == 0))
            def _():
                _comm_phase(comm, "start", *comm_refs)

        accs = []
        for p, (_, _, nt, cols) in enumerate(pairs):
            av = a_refs[p][...]
            if av.dtype != BF16:
                av = av.astype(BF16)
            b_ref = b_refs[b_index[p]]
            bv = b_ref[...] if cols is None else b_ref[:, cols[0]:cols[1]]
            accs.append(_dot_nt(av, bv) if nt else _dot(av, bv))
        epi(accs, ex, out)
        if comm:
            mid_row = ni // 2 if ni >= 3 else ni - 1
            mid_col = 0 if ni >= 3 else nj - 1

            @pl.when(jnp.logical_and(pl.program_id(0) == mid_row, pl.program_id(1) == mid_col))
            def _():
                _comm_phase(comm, "mid", *comm_refs)

            @pl.when(jnp.logical_and(pl.program_id(0) == ni - 1, pl.program_id(1) == nj - 1))
            def _():
                _comm_phase(comm, "finish", *comm_refs)

    sem = ("arbitrary", "arbitrary") if arbitrary or comm else ("parallel", "parallel")
    res = pl.pallas_call(
        body,
        name=name,
        grid=(ni, nj),
        in_specs=in_specs,
        out_specs=[spec for _, spec in outs] + [_ANY] * n_co,
        out_shape=[shape for shape, _ in outs] + comm_outs,
        scratch_shapes=comm_sems,
        compiler_params=_params(sem),
    )(*arrays)
    return (res[:n_out], res[n_out:]) if comm else res


def _tile(tm, tn, off=0):
    return pl.BlockSpec((tm, tn), lambda i, j: (i, j + off))


def _row(tm, w, blk=0):
    return pl.BlockSpec((tm, w), lambda i, j: (i, blk))


def _whole(shape):
    zeros = (0,) * len(shape)
    return pl.BlockSpec(shape, lambda i, j: zeros)


def _sds(shape, dtype):
    return jax.ShapeDtypeStruct(shape, dtype)


def _tile_t(tm, tn):
    return pl.BlockSpec((tn, tm), lambda i, j: (j, i))


def _grad_w(name, a_t, g, *, tk, tn, ts, block_cols=None):
    ka, s_len = a_t.shape
    n = g.shape[1]
    width = tn if block_cols is None else block_cols

    def body(a_ref, g_ref, o_ref):
        first = pl.program_id(2) == 0
        gv = g_ref[...].astype(BF16)
        for b in range(tn // width):
            part = _dot(a_ref[...], gv[:, b * width:(b + 1) * width])
            dst = o_ref if block_cols is None else o_ref.at[b]

            @pl.when(first)
            def _():
                dst[...] = part

            @pl.when(jnp.logical_not(first))
            def _():
                dst[...] += part

    if block_cols is None:
        out_shape = _sds((ka, n), F32)
        out_spec = pl.BlockSpec((tk, tn), lambda i, j, s: (i, j))
    else:
        out_shape = _sds((n // width, ka, width), F32)
        out_spec = pl.BlockSpec((tn // width, tk, width), lambda i, j, s: (j, i, 0))
    return pl.pallas_call(
        body,
        name=name,
        grid=(ka // tk, n // tn, s_len // ts),
        in_specs=[pl.BlockSpec((tk, ts), lambda i, j, s: (i, s)), pl.BlockSpec((ts, tn), lambda i, j, s: (s, j))],
        out_specs=out_spec,
        out_shape=out_shape,
        compiler_params=_params(("parallel", "parallel", "arbitrary")),
    )(a_t, g)


def _rms_fwd(name, x, g, tm, comm=()):
    s_len, d = x.shape
    steps = s_len // tm
    comm_arrays, comm_outs, comm_sems = _comm_operands(comm)
    n_ci, n_co = len(comm_arrays), len(comm_outs)

    def body(x_ref, g_ref, *rest):
        comm_refs = rest[:n_ci], rest[n_ci + 3:n_ci + 3 + n_co], rest[n_ci + 3 + n_co:]
        h_ref, ht_ref, r_ref = rest[n_ci:n_ci + 3]
        for phase, at in (("start", 0), ("mid", steps // 2)):
            if comm:
                @pl.when(pl.program_id(0) == at)
                def _():
                    _comm_phase(comm, phase, *comm_refs)

        xv = x_ref[...]
        r = lax.rsqrt(jnp.mean(xv * xv, axis=-1, keepdims=True) + EPS)
        h = (xv * r * g_ref[...]).astype(BF16)
        h_ref[...] = h
        ht_ref[...] = jnp.transpose(h)
        r_ref[...] = r
        if comm:
            @pl.when(pl.program_id(0) == steps - 1)
            def _():
                _comm_phase(comm, "finish", *comm_refs)

    res = pl.pallas_call(
        body,
        name=name,
        grid=(steps,),
        in_specs=[pl.BlockSpec((tm, d), lambda i: (i, 0)), pl.BlockSpec((1, d), lambda i: (0, 0))] + [_ANY] * n_ci,
        out_specs=[pl.BlockSpec((tm, d), lambda i: (i, 0)), pl.BlockSpec((d, tm), lambda i: (0, i)),
                   pl.BlockSpec((tm, 1), lambda i: (i, 0))] + [_ANY] * n_co,
        out_shape=[_sds((s_len, d), BF16), _sds((d, s_len), BF16), _sds((s_len, 1), F32)] + comm_outs,
        scratch_shapes=comm_sems,
        compiler_params=_params(("arbitrary",) if comm else ("parallel",)),
    )(x, g, *comm_arrays)
    return res[0], res[1], res[2], res[3:]


def _forget_cumsum(prest, b_f_pad, tc):
    s_len = prest.shape[0]

    def body(f_ref, b_ref, c_ref, carry):
        @pl.when(pl.program_id(0) == 0)
        def _():
            carry[...] = jnp.zeros_like(carry)

        logf = _log_sigmoid(f_ref[...] + b_ref[...])
        row = lax.broadcasted_iota(jnp.int32, (tc, tc), 0)
        col = lax.broadcasted_iota(jnp.int32, (tc, tc), 1)
        tri = (row >= col).astype(F32)
        c = _dot_f32(tri, logf) + carry[...]
        c_ref[...] = c
        carry[...] = c[tc - 1:tc, :]

    return pl.pallas_call(
        body,
        name="forget_cumsum",
        grid=(s_len // tc,),
        in_specs=[pl.BlockSpec((tc, LANES), lambda i: (i, 0)), pl.BlockSpec((1, LANES), lambda i: (0, 0))],
        out_specs=pl.BlockSpec((tc, LANES), lambda i: (i, 0)),
        out_shape=_sds((s_len, LANES), F32),
        scratch_shapes=[pltpu.VMEM((1, LANES), F32)],
        compiler_params=_params(("arbitrary",)),
    )(prest, b_f_pad)


def _stack_heads(pair, lt64):
    zero = jnp.zeros_like(pair)
    return jnp.concatenate([jnp.where(lt64, pair, zero), jnp.where(lt64, zero, pair)], axis=0)


def _score_tiles(q_ref, k_ref, ck_ref, st_sc, tk):
    lt64 = _lane_lt64((tk, LANES))
    for p in range(N_HEADS // 2):
        lanes = slice(p * LANES, (p + 1) * LANES)
        q_pair = q_ref[:, lanes] * jnp.asarray(HEAD_DIM ** -0.5, BF16)
        st2 = _dot_nt(_stack_heads(k_ref[:, lanes], lt64), q_pair)
        for half in range(2):
            h = 2 * p + half
            st_sc[h] = st2[half * tk:(half + 1) * tk] - ck_ref[:, h:h + 1]


ROW_CHUNK = 64


def _row_chunks(tk):
    rc = min(ROW_CHUNK, tk)
    return [slice(r, r + rc) for r in range(0, tk, rc)]


def _by_sublane(x):
    return x.reshape(x.shape[0] // 8, 8, x.shape[1])


def _softmax_update(st_sc, p_sc, m_sc, l_sc, tk, tq):
    alphas = []
    for h in range(N_HEADS):
        top8 = jnp.full((8, tq), NEG, F32)
        for rows in _row_chunks(tk):
            top8 = jnp.maximum(top8, jnp.max(_by_sublane(st_sc[h, rows, :]), axis=0))
        m_old = m_sc[h]
        m_new = jnp.maximum(m_old, jnp.max(top8, axis=0, keepdims=True))
        sum8 = jnp.zeros((8, tq), F32)
        for rows in _row_chunks(tk):
            pt = jnp.exp(st_sc[h, rows, :] - m_new)
            p_sc[h, rows, :] = pt.astype(BF16)
            sum8 = sum8 + jnp.sum(_by_sublane(pt), axis=0)
        alpha = jnp.exp(m_old - m_new)
        l_sc[h] = alpha * l_sc[h] + jnp.sum(sum8, axis=0, keepdims=True)
        m_sc[h] = m_new
        alphas.append(alpha)
    return alphas


def _mask_diagonal(st_sc, i, j, tq, tk):
    @pl.when((j + 1) * tk - 1 > i * tq)
    def _():
        key = j * tk + lax.broadcasted_iota(jnp.int32, (tk, tq), 0)
        query = i * tq + lax.broadcasted_iota(jnp.int32, (tk, tq), 1)
        st_sc[...] = jnp.where((query >= key)[None], st_sc[...], NEG)


def _attn_fwd(qkv, v_t, c_col, tq, tk, comm=()):
    s_len = qkv.shape[0]
    ratio = tq // tk
    steps = [(i, j) for i in range(s_len // tq) for j in range((i + 1) * ratio)]
    i_tab = jnp.asarray([i for i, _ in steps], jnp.int32)
    j_tab = jnp.asarray([j for _, j in steps], jnp.int32)

    comm_arrays, comm_outs, comm_sems = _comm_operands(comm)
    n_ci, n_co = len(comm_arrays), len(comm_outs)

    def body(i_ref, j_ref, q_ref, k_ref, vt_ref, ck_ref, *rest):
        comm_refs = rest[:n_ci], rest[n_ci + 3:n_ci + 3 + n_co], rest[n_ci + 3 + n_co + 5:]
        o_ref, ot_ref, lse_ref = rest[n_ci:n_ci + 3]
        acc_t, m_sc, l_sc, st_sc, p_sc = rest[n_ci + 3 + n_co:n_ci + 3 + n_co + 5]
        n = pl.program_id(0)
        i, j = i_ref[n], j_ref[n]
        for phase, at in (("start", 0), ("mid", (2 * len(steps)) // 3)):
            if comm:
                @pl.when(n == at)
                def _():
                    _comm_phase(comm, phase, *comm_refs)

        @pl.when(j == 0)
        def _():
            acc_t[...] = jnp.zeros_like(acc_t)
            m_sc[...] = jnp.full_like(m_sc, NEG)
            l_sc[...] = jnp.zeros_like(l_sc)

        _score_tiles(q_ref, k_ref, ck_ref, st_sc, tk)
        _mask_diagonal(st_sc, i, j, tq, tk)
        alpha = _softmax_update(st_sc, p_sc, m_sc, l_sc, tk, tq)
        top = lax.broadcasted_iota(jnp.int32, (LANES, tq), 0) < HEAD_DIM
        for p in range(N_HEADS // 2):
            lanes = slice(p * LANES, (p + 1) * LANES)
            vt_pair = vt_ref[lanes, :]
            pv = jnp.where(top, _dot(vt_pair, p_sc[2 * p]), _dot(vt_pair, p_sc[2 * p + 1]))
            acc_t[lanes, :] = acc_t[lanes, :] * jnp.where(top, alpha[2 * p], alpha[2 * p + 1]) + pv

        @pl.when(j == (i + 1) * ratio - 1)
        def _():
            for p in range(N_HEADS // 2):
                lanes = slice(p * LANES, (p + 1) * LANES)
                l_pair = jnp.where(top, l_sc[2 * p], l_sc[2 * p + 1])
                o_t = acc_t[lanes, :] / l_pair
                o_ref[:, lanes] = jnp.transpose(o_t)
                ot_ref[lanes, :] = o_t.astype(BF16)
            lse_ref[...] = m_sc[...] + jnp.log(l_sc[...])

        if comm:
            @pl.when(n == len(steps) - 1)
            def _():
                _comm_phase(comm, "finish", *comm_refs)

    stat = pltpu.VMEM((N_HEADS, 1, tq), F32)
    res = pl.pallas_call(
        body,
        name="attn_fwd",
        grid_spec=pltpu.PrefetchScalarGridSpec(
            num_scalar_prefetch=2,
            grid=(len(steps),),
            in_specs=[
                pl.BlockSpec((tq, FOX_W), lambda n, it, jt: (it[n], 0)),
                pl.BlockSpec((tk, FOX_W), lambda n, it, jt: (jt[n], 1)),
                pl.BlockSpec((FOX_W, tk), lambda n, it, jt: (0, jt[n])),
                pl.BlockSpec((tk, LANES), lambda n, it, jt: (jt[n], 0)),
            ] + [_ANY] * n_ci,
            out_specs=[
                pl.BlockSpec((tq, FOX_W), lambda n, it, jt: (it[n], 0)),
                pl.BlockSpec((FOX_W, tq), lambda n, it, jt: (0, it[n])),
                pl.BlockSpec((N_HEADS, 1, tq), lambda n, it, jt: (0, 0, it[n])),
            ] + [_ANY] * n_co,
            scratch_shapes=[pltpu.VMEM((FOX_W, tq), F32), stat, stat, pltpu.VMEM((N_HEADS, tk, tq), F32),
                            pltpu.VMEM((N_HEADS, tk, tq), BF16)] + comm_sems,
        ),
        out_shape=[_sds((s_len, FOX_W), F32), _sds((FOX_W, s_len), BF16), _sds((N_HEADS, 1, s_len), F32)] + comm_outs,
        compiler_params=_params(("arbitrary",)),
    )(i_tab, j_tab, qkv, qkv, v_t, c_col, *comm_arrays)
    return res[0], res[1], res[2], res[3:]


def _sgu_mix(vn, w_stack, lt64):
    outs = []
    for p in range(SGU_G // 2):
        r = _dot(w_stack[p], vn[:, p * LANES:(p + 1) * LANES])
        outs.append(jnp.where(lt64, r[:SGU_LEN], r[SGU_LEN:]))
    return jnp.concatenate(outs, axis=1)


def _sgu_norm(sv, ln_g, ln_b):
    svg = _gelu(sv)
    xc = svg - jnp.mean(svg, axis=-1, keepdims=True)
    rstd = lax.rsqrt(jnp.mean(xc * xc, axis=-1, keepdims=True) + EPS)
    xhat = xc * rstd
    return xhat, rstd, xhat * ln_g + ln_b


def _sgu_fwd(prest, ln_g, ln_b, w_stack, b_pair, tm):
    s_len = prest.shape[0]

    def body(u_ref, sv_ref, g_ref, b_ref, w_ref, bp_ref, sg_ref, sgt_ref):
        lt64 = _lane_lt64((SGU_LEN, LANES))
        _, _, vn = _sgu_norm(sv_ref[...].astype(F32), g_ref[...], b_ref[...])
        vn = vn.astype(BF16)
        w_stack_v = [w_ref[p] for p in range(SGU_G // 2)]
        for w in range(tm // SGU_LEN):
            win = slice(w * SGU_LEN, (w + 1) * SGU_LEN)
            mixed = _sgu_mix(vn[win], w_stack_v, lt64) + bp_ref[...]
            sg = (_gelu(u_ref[win, :].astype(F32)) * mixed).astype(BF16)
            sg_ref[win, :] = sg
            sgt_ref[:, win] = jnp.transpose(sg)

    return pl.pallas_call(
        body,
        name="sgu_fwd",
        grid=(s_len // tm,),
        in_specs=[
            pl.BlockSpec((tm, SGU_W), lambda i: (i, U_OFF // SGU_W)),
            pl.BlockSpec((tm, SGU_W), lambda i: (i, SV_OFF // SGU_W)),
            pl.BlockSpec((1, SGU_W), lambda i: (0, 0)),
            pl.BlockSpec((1, SGU_W), lambda i: (0, 0)),
            pl.BlockSpec((SGU_G // 2, 2 * SGU_LEN, SGU_LEN), lambda i: (0, 0, 0)),
            pl.BlockSpec((SGU_LEN, SGU_W), lambda i: (0, 0)),
        ],
        out_specs=[pl.BlockSpec((tm, SGU_W), lambda i: (i, 0)), pl.BlockSpec((SGU_W, tm), lambda i: (0, i))],
        out_shape=[_sds((s_len, SGU_W), BF16), _sds((SGU_W, s_len), BF16)],
        compiler_params=_params(("parallel",)),
    )(prest, prest, ln_g, ln_b, w_stack, b_pair)


def _sgu_bwd(prest, dsg, ln_g, ln_b, w_stack, wt_stack, b_pair, tm):
    s_len = prest.shape[0]
    n_pair = SGU_G // 2

    def body(u_ref, sv_ref, dsg_ref, g_ref, b_ref, w_ref, wt_ref, bp_ref,
             du_ref, dsv_ref, dw_ref, db_ref, dg_ref, dbeta_ref, dvn_sc):
        @pl.when(pl.program_id(0) == 0)
        def _():
            dw_ref[...] = jnp.zeros_like(dw_ref)
            db_ref[...] = jnp.zeros_like(db_ref)
            dg_ref[...] = jnp.zeros_like(dg_ref)
            dbeta_ref[...] = jnp.zeros_like(dbeta_ref)

        lt64 = _lane_lt64((SGU_LEN, LANES))
        sv = sv_ref[...].astype(F32)
        xhat, rstd, vn32 = _sgu_norm(sv, g_ref[...], b_ref[...])
        vn = vn32.astype(BF16)
        w_stack_v = [w_ref[p] for p in range(n_pair)]
        db = jnp.zeros((SGU_LEN, SGU_W), F32)
        for w in range(tm // SGU_LEN):
            win = slice(w * SGU_LEN, (w + 1) * SGU_LEN)
            u = u_ref[win, :].astype(F32)
            dsg_w = dsg_ref[win, :]
            mixed = _sgu_mix(vn[win], w_stack_v, lt64) + bp_ref[...]
            du_ref[win, :] = (dsg_w * mixed * _gelu_grad(u)).astype(BF16)
            dmixed = dsg_w * _gelu(u)
            db = db + dmixed
            dm16 = dmixed.astype(BF16)
            for p in range(n_pair):
                lanes = slice(p * LANES, (p + 1) * LANES)
                dmp = dm16[:, lanes]
                r = _dot(wt_ref[p], dmp)
                dvn_sc[win, lanes] = jnp.where(lt64, r[:SGU_LEN], r[SGU_LEN:])
                zero = jnp.zeros_like(dmp)
                dm_ab = jnp.concatenate([jnp.where(lt64, dmp, zero), jnp.where(lt64, zero, dmp)], axis=0)
                dw_ref[p] += _dot_nt(dm_ab, vn[win, lanes])
        db_ref[...] += db
        dvn = dvn_sc[...]
        dg_ref[...] += jnp.sum(dvn * xhat, axis=0, keepdims=True)
        dbeta_ref[...] += jnp.sum(dvn, axis=0, keepdims=True)
        dxh = dvn * g_ref[...]
        dsvg = rstd * (dxh - jnp.mean(dxh, axis=-1, keepdims=True) - xhat * jnp.mean(dxh * xhat, axis=-1, keepdims=True))
        dsv_ref[...] = (dsvg * _gelu_grad(sv)).astype(BF16)

    const2 = lambda i: (0, 0)
    const3 = lambda i: (0, 0, 0)
    return pl.pallas_call(
        body,
        name="sgu_bwd",
        grid=(s_len // tm,),
        in_specs=[
            pl.BlockSpec((tm, SGU_W), lambda i: (i, U_OFF // SGU_W)),
            pl.BlockSpec((tm, SGU_W), lambda i: (i, SV_OFF // SGU_W)),
            pl.BlockSpec((tm, SGU_W), lambda i: (i, 0)),
            pl.BlockSpec((1, SGU_W), const2),
            pl.BlockSpec((1, SGU_W), const2),
            pl.BlockSpec((n_pair, 2 * SGU_LEN, SGU_LEN), const3),
            pl.BlockSpec((n_pair, 2 * SGU_LEN, SGU_LEN), const3),
            pl.BlockSpec((SGU_LEN, SGU_W), const2),
        ],
        out_specs=[
            pl.BlockSpec((tm, SGU_W), lambda i: (i, 0)),
            pl.BlockSpec((tm, SGU_W), lambda i: (i, 0)),
            pl.BlockSpec((n_pair, 2 * SGU_LEN, SGU_LEN), const3),
            pl.BlockSpec((SGU_LEN, SGU_W), const2),
            pl.BlockSpec((1, SGU_W), const2),
            pl.BlockSpec((1, SGU_W), const2),
        ],
        out_shape=[
            _sds((s_len, SGU_W), BF16), _sds((s_len, SGU_W), BF16), _sds((n_pair, 2 * SGU_LEN, SGU_LEN), F32),
            _sds((SGU_LEN, SGU_W), F32), _sds((1, SGU_W), F32), _sds((1, SGU_W), F32),
        ],
        scratch_shapes=[pltpu.VMEM((tm, SGU_W), F32)],
        compiler_params=_params(("arbitrary",)),
    )(prest, prest, dsg, ln_g, ln_b, w_stack, wt_stack, b_pair)


def _attn_bwd(qkv, k_t, do, c_col, lse_row, delta_row, tq, tk, comm=()):
    s_len = qkv.shape[0]
    nq, nk = s_len // tq, s_len // tk
    ratio = tq // tk
    scale = HEAD_DIM ** -0.5
    steps = [(j, i) for j in range(nk) for i in range(j // ratio, nq)]
    j_tab = jnp.asarray([j for j, _ in steps], jnp.int32)
    i_tab = jnp.asarray([i for _, i in steps], jnp.int32)

    comm_arrays, comm_outs, comm_sems = _comm_operands(comm)
    n_ci, n_co = len(comm_arrays), len(comm_outs)

    def body(j_ref, i_ref, q_ref, k_ref, v_ref, kt_ref, do_ref, ck_ref, lse_ref, dl_ref, *rest):
        comm_refs = rest[:n_ci], rest[n_ci + 5:n_ci + 5 + n_co], rest[n_ci + 5 + n_co + 8:]
        dq_ref, dk_ref, dv_ref, dcr_ref, dcc_ref = rest[n_ci:n_ci + 5]
        dq_t, dk_acc, dv_acc, dcc_acc, st_sc, dpt_sc, p_sc, ds_sc = rest[n_ci + 5 + n_co:n_ci + 5 + n_co + 8]
        n = pl.program_id(0)
        j, i = j_ref[n], i_ref[n]

        @pl.when(n == 0)
        def _():
            _comm_phase(comm, "start", *comm_refs)
            dq_t[...] = jnp.zeros_like(dq_t)
            dcr_ref[...] = jnp.zeros_like(dcr_ref)

        @pl.when(i == j // ratio)
        def _():
            dk_acc[...] = jnp.zeros_like(dk_acc)
            dv_acc[...] = jnp.zeros_like(dv_acc)
            dcc_acc[...] = jnp.zeros_like(dcc_acc)

        lt64 = _lane_lt64((tk, LANES))
        _score_tiles(q_ref, k_ref, ck_ref, st_sc, tk)
        for p in range(N_HEADS // 2):
            lanes = slice(p * LANES, (p + 1) * LANES)
            dpt2 = _dot_nt(_stack_heads(v_ref[:, lanes], lt64), do_ref[:, lanes].astype(BF16))
            dpt_sc[2 * p] = dpt2[:tk]
            dpt_sc[2 * p + 1] = dpt2[tk:]
        _mask_diagonal(st_sc, i, j, tq, tk)

        pt = jnp.exp(st_sc[...] - lse_ref[...])
        dst = pt * (dpt_sc[...] - dl_ref[...])
        p_sc[...] = pt.astype(BF16)
        ds_sc[...] = dst.astype(BF16)
        dcr_ref[i] += jnp.sum(dst, axis=1, keepdims=True)
        col_sums = jnp.sum(dst, axis=2, keepdims=True)
        lane = lax.broadcasted_iota(jnp.int32, (tk, LANES), 1)
        dcc = jnp.zeros((tk, LANES), F32)
        for h in range(N_HEADS):
            dcc = jnp.where(lane == h, -col_sums[h], dcc)
        dcc_acc[...] += dcc

        for p in range(N_HEADS // 2):
            lanes = slice(p * LANES, (p + 1) * LANES)
            q_pair = q_ref[:, lanes] * jnp.asarray(scale, BF16)
            dv2 = _dot(p_sc[2 * p:2 * p + 2].reshape(2 * tk, tq), do_ref[:, lanes].astype(BF16))
            dv_acc[:, lanes] += jnp.where(lt64, dv2[:tk], dv2[tk:])
            dk2 = _dot(ds_sc[2 * p:2 * p + 2].reshape(2 * tk, tq), q_pair)
            dk_acc[:, lanes] += jnp.where(lt64, dk2[:tk], dk2[tk:])
            dq2 = _dot(kt_ref[lanes, :], jnp.concatenate([ds_sc[2 * p], ds_sc[2 * p + 1]], axis=1))
            top = lax.broadcasted_iota(jnp.int32, (LANES, tq), 0) < HEAD_DIM
            dq_t[i, lanes, :] += jnp.where(top, dq2[:, :tq], dq2[:, tq:])

        @pl.when(j == (i + 1) * ratio - 1)
        def _():
            rows = pl.ds(pl.multiple_of(i * tq, tq), tq)
            for p in range(N_HEADS // 2):
                lanes = slice(p * LANES, (p + 1) * LANES)
                dq_ref[rows, lanes] = (jnp.transpose(dq_t[i, lanes, :]) * scale).astype(BF16)

        @pl.when(i == nq - 1)
        def _():
            dk_ref[...] = dk_acc[...].astype(BF16)
            dv_ref[...] = dv_acc[...].astype(BF16)
            dcc_ref[...] = dcc_acc[...]

        if comm:
            @pl.when(n == len(steps) // 2)
            def _():
                _comm_phase(comm, "mid", *comm_refs)

            @pl.when(n == len(steps) - 1)
            def _():
                _comm_phase(comm, "finish", *comm_refs)

    q_map = lambda n, jt, it: (it[n], 0)
    q_stat = lambda n, jt, it: (0, 0, it[n])
    k_map = lambda n, jt, it: (jt[n], 0)
    tile = (N_HEADS, tk, tq)
    res = pl.pallas_call(
        body,
        name="attn_bwd",
        grid_spec=pltpu.PrefetchScalarGridSpec(
            num_scalar_prefetch=2,
            grid=(len(steps),),
            in_specs=[
                pl.BlockSpec((tq, FOX_W), q_map),
                pl.BlockSpec((tk, FOX_W), lambda n, jt, it: (jt[n], 1)),
                pl.BlockSpec((tk, FOX_W), lambda n, jt, it: (jt[n], 2)),
                pl.BlockSpec((FOX_W, tk), lambda n, jt, it: (0, jt[n])),
                pl.BlockSpec((tq, FOX_W), q_map),
                pl.BlockSpec((tk, LANES), k_map),
                pl.BlockSpec((N_HEADS, 1, tq), q_stat),
                pl.BlockSpec((N_HEADS, 1, tq), q_stat),
            ] + [_ANY] * n_ci,
            out_specs=[
                pl.BlockSpec((s_len, FOX_W), lambda n, jt, it: (0, 0)),
                pl.BlockSpec((tk, FOX_W), k_map),
                pl.BlockSpec((tk, FOX_W), k_map),
                pl.BlockSpec((nq, N_HEADS, 1, tq), lambda n, jt, it: (0, 0, 0, 0)),
                pl.BlockSpec((tk, LANES), k_map),
            ] + [_ANY] * n_co,
            scratch_shapes=[pltpu.VMEM((nq, FOX_W, tq), F32), pltpu.VMEM((tk, FOX_W), F32), pltpu.VMEM((tk, FOX_W), F32),
                            pltpu.VMEM((tk, LANES), F32), pltpu.VMEM(tile, F32), pltpu.VMEM(tile, F32),
                            pltpu.VMEM(tile, BF16), pltpu.VMEM(tile, BF16)] + comm_sems,
        ),
        out_shape=[_sds((s_len, FOX_W), BF16), _sds((s_len, FOX_W), BF16), _sds((s_len, FOX_W), BF16),
                   _sds((nq, N_HEADS, 1, tq), F32), _sds((s_len, LANES), F32)] + comm_outs,
        compiler_params=_params(("arbitrary",)),
    )(j_tab, i_tab, qkv, qkv, qkv, k_t, do, c_col, lse_row, delta_row, *comm_arrays)
    return res[:5], res[5:]


def _forget_bwd(dc_rows, dc_cols, prest, b_f_pad, tc):
    s_len = dc_rows.shape[0]
    nb = s_len // tc

    def body(dcr_ref, dc_ref, f_ref, b_ref, df_ref, db_ref, carry):
        @pl.when(pl.program_id(0) == 0)
        def _():
            carry[...] = jnp.zeros_like(carry)
            db_ref[...] = jnp.zeros_like(db_ref)

        row = lax.broadcasted_iota(jnp.int32, (tc, tc), 0)
        col = lax.broadcasted_iota(jnp.int32, (tc, tc), 1)
        tri = (row <= col).astype(F32)
        dlogf = _dot_f32(tri, dcr_ref[...] + dc_ref[...]) + carry[...]
        carry[...] = dlogf[0:1, :]
        z = f_ref[...] + b_ref[...]
        lane = lax.broadcasted_iota(jnp.int32, (tc, LANES), 1)
        dz = jnp.where(lane < N_HEADS, dlogf * _sigmoid(-z), 0.0)
        df_ref[...] = dz.astype(BF16)
        db_ref[...] += jnp.sum(dz, axis=0, keepdims=True)

    rev = lambda i: (nb - 1 - i, 0)
    return pl.pallas_call(
        body,
        name="forget_bwd",
        grid=(nb,),
        in_specs=[
            pl.BlockSpec((tc, LANES), rev),
            pl.BlockSpec((tc, LANES), rev),
            pl.BlockSpec((tc, LANES), rev),
            pl.BlockSpec((1, LANES), lambda i: (0, 0)),
        ],
        out_specs=[pl.BlockSpec((tc, LANES), rev), pl.BlockSpec((1, LANES), lambda i: (0, 0))],
        out_shape=[_sds((s_len, LANES), BF16), _sds((1, LANES), F32)],
        scratch_shapes=[pltpu.VMEM((1, LANES), F32)],
        compiler_params=_params(("arbitrary",)),
    )(dc_rows, dc_cols, prest, b_f_pad)


def _pair_sum(name, g4, recv, idx, tr):
    _, _, r, c = g4.shape

    def body(idx_ref, g_ref, r_ref, p16_ref, own_ref):
        k = pl.program_id(1)
        s = g_ref[...] + r_ref[...]
        p16_ref[...] = s.astype(BF16)

        @pl.when(k == idx_ref[1])
        def _():
            own_ref[...] = s

    return pl.pallas_call(
        body,
        name=name,
        grid_spec=pltpu.PrefetchScalarGridSpec(
            num_scalar_prefetch=1,
            grid=(r // tr, 4),
            in_specs=[
                pl.BlockSpec((None, None, tr, c), lambda i, k, idx: (k, idx[0], i, 0)),
                pl.BlockSpec((None, tr, c), lambda i, k, idx: (k, i, 0)),
            ],
            out_specs=[
                pl.BlockSpec((None, tr, c), lambda i, k, idx: (k, i, 0)),
                pl.BlockSpec((tr, c), lambda i, k, idx: (i, 0)),
            ],
        ),
        out_shape=[_sds((4, r, c), BF16), _sds((r, c), F32)],
        compiler_params=_params(("parallel", "arbitrary")),
    )(idx, g4, recv)


def _adamw_math(w, g, m, v):
    m2 = ADAM_B1 * m + (1.0 - ADAM_B1) * g
    v2 = ADAM_B2 * v + (1.0 - ADAM_B2) * (g * g)
    m_hat = m2 / (1.0 - ADAM_B1 ** ADAM_STEP)
    v_hat = v2 / (1.0 - ADAM_B2 ** ADAM_STEP)
    delta = -ADAM_LR * (m_hat / (jnp.sqrt(v_hat) + ADAM_EPS) + ADAM_WD * w)
    return delta, m2, v2


def _adamw_shard(name, own, recv, w, m, v, tr):
    r, c = own.shape

    def body(own_ref, recv_ref, w_ref, m_ref, v_ref, g_ref, d_ref, m2_ref, v2_ref):
        g = own_ref[...]
        for k in range(3):
            g = g + recv_ref[k].astype(F32)
        delta, m2, v2 = _adamw_math(w_ref[...], g, m_ref[...], v_ref[...])
        g_ref[...] = g
        d_ref[...] = delta
        m2_ref[...] = m2
        v2_ref[...] = v2

    spec = pl.BlockSpec((tr, c), lambda i: (i, 0))
    return pl.pallas_call(
        body,
        name=name,
        grid=(r // tr,),
        in_specs=[spec, pl.BlockSpec((3, tr, c), lambda i: (0, i, 0)), spec, spec, spec],
        out_specs=[spec] * 4,
        out_shape=[_sds((r, c), F32)] * 4,
        compiler_params=_params(("parallel",)),
    )(own, recv, w, m, v)


def _adamw_small(gathered, w, m, v):
    _, r, _ = gathered.shape

    def body(ga_ref, w_ref, m_ref, v_ref, g_ref, d_ref, m2_ref, v2_ref):
        g = ga_ref[0]
        for k in range(1, N_DEV):
            g = g + ga_ref[k]
        delta, m2, v2 = _adamw_math(w_ref[...], g, m_ref[...], v_ref[...])
        g_ref[...] = g
        d_ref[...] = delta
        m2_ref[...] = m2
        v2_ref[...] = v2

    spec = pl.BlockSpec((r, LANES), lambda i: (0, 0))
    return pl.pallas_call(
        body,
        name="adamw_small",
        grid=(1,),
        in_specs=[pl.BlockSpec((N_DEV, r, LANES), lambda i: (0, 0, 0)), spec, spec, spec],
        out_specs=[spec] * 4,
        out_shape=[_sds((r, LANES), F32)] * 4,
        compiler_params=_params(("arbitrary",)),
    )(gathered, w, m, v)


_SMALL = (("w_sgu", (1, SGU_G, SGU_LEN, SGU_LEN)), ("b_sgu", (1, SGU_G, SGU_LEN)), ("norm2_g", (1, D_MODEL)),
          ("normf_g", (D_MODEL,)), ("ln_v_g", (1, SGU_W)), ("ln_v_b", (1, SGU_W)), ("b_f", (1, N_HEADS)),
          ("norm1_g", (1, D_MODEL)), ("loss", ()))
_N_EARLY = 6


def _pack_rows(values):
    rows = []
    for val in values:
        flat = val.reshape(-1).astype(F32)
        pad = (-flat.shape[0]) % LANES
        rows.append(jnp.pad(flat, (0, pad)).reshape(-1, LANES))
    packed = jnp.concatenate(rows, axis=0)
    return jnp.pad(packed, ((0, (-packed.shape[0]) % 8), (0, 0)))


def _pack_small(values):
    return jnp.concatenate([_pack_rows(values[:_N_EARLY]), _pack_rows(values[_N_EARLY:])], axis=0)


def _unpack_small(packed):
    out, row = [], 0
    for k, (_, shape) in enumerate(_SMALL):
        if k == _N_EARLY:
            row += (-row) % 8
        size = math.prod(shape)
        n_rows = -(-size // LANES)
        out.append(packed[row:row + n_rows].reshape(-1)[:size].reshape(shape))
        row += n_rows
    return out


def kernel(x, norm1_g, w_in, b_f, ln_v_g, ln_v_b, w_sgu, b_sgu, w_a, w_b, w_o, norm2_g, w_up, w_down, normf_g, loss_target, m_norm1_g, m_w_in, m_b_f, m_ln_v_g, m_ln_v_b, m_w_sgu, m_b_sgu, m_w_a, m_w_b, m_w_o, m_norm2_g, m_w_up, m_w_down, m_normf_g, v_norm1_g, v_w_in, v_b_f, v_ln_v_g, v_ln_v_b, v_w_sgu, v_b_sgu, v_w_a, v_w_b, v_w_o, v_norm2_g, v_w_up, v_w_down, v_normf_g):
    xs = x[0]
    target = loss_target[0]
    s_len, d = xs.shape
    tm = min(512, s_len)
    tl = min(1024, s_len)
    tr = min(256, s_len)
    ta = min(512, s_len)
    tc = min(512, s_len)

    big = (w_in[0], w_a[0], w_b[0], w_o[0], w_up[0], w_down[0])
    h, h_t, r1, (w_in_g,) = _rms_fwd("rms1", xs, norm1_g, tm, comm=[_gather_plan(w_in[0].astype(BF16))])
    w_in_f = jnp.transpose(w_in_g, (1, 0, 2)).reshape(d, IN_COLS)
    later = big[1:]
    later_flat = jnp.concatenate([w.reshape(-1).astype(BF16) for w in later]).reshape(-1, D_MODEL)
    later_plan = _gather_plan(later_flat)

    def unflatten(gathered):
        row, full = 0, []
        for w, col_sharded in zip(later, (True, True, False, True, False)):
            n_rows = w.size // D_MODEL
            blk = gathered[:, row:row + n_rows].reshape((N_DEV,) + w.shape)
            row += n_rows
            if col_sharded:
                full.append(jnp.transpose(blk, (1, 0, 2)).reshape(w.shape[0], N_DEV * w.shape[1]))
            else:
                full.append(blk.reshape(N_DEV * w.shape[0], w.shape[1]))
        return full

    w_qkv = w_in_f[:, :QKV_W]
    f_lo = QKV_W
    u_lo = f_lo + N_HEADS
    w_rest = jnp.concatenate([w_in_f[:, u_lo:], jnp.pad(w_in_f[:, f_lo:u_lo], ((0, 0), (0, LANES - N_HEADS)))], axis=1)

    chunk_id = jnp.arange(SGU_LEN) // CHUNK
    sgu_mask = chunk_id[None, :] <= chunk_id[:, None]
    w_masked = jnp.where(sgu_mask[None], w_sgu[0], 0.0)
    w_stack = w_masked.reshape(SGU_G // 2, 2 * SGU_LEN, SGU_LEN).astype(BF16)
    wt_stack = jnp.transpose(w_masked, (0, 2, 1)).reshape(SGU_G // 2, 2 * SGU_LEN, SGU_LEN).astype(BF16)
    b_pair = jnp.transpose(jnp.repeat(b_sgu[0], SGU_W // SGU_G, axis=0))
    b_f_pad = jnp.pad(b_f, ((0, 0), (0, LANES - N_HEADS)))
    head_sel = (jnp.arange(FOX_W)[:, None] // HEAD_DIM == jnp.arange(LANES)[None, :]).astype(F32)

    def store(dtype):
        def epi(accs, ex, out):
            out[0][...] = accs[0].astype(dtype)
        return epi

    def qkv_epi(accs, ex, out):
        tile = accs[0].astype(BF16)
        out[0][...] = tile
        for col, ref in ((1, out[1]), (2, out[2])):
            @pl.when(pl.program_id(1) == col)
            def _():
                ref[...] = jnp.transpose(tile)

    t_spec = pl.BlockSpec((FOX_W, tl), lambda i, j: (0, i))
    qkv, k_t, v_t = _mm("proj_qkv", [(h, w_qkv, False, None)], [],
                        [(_sds((s_len, QKV_W), BF16), _tile(tl, FOX_W)), (_sds((FOX_W, s_len), BF16), t_spec),
                         (_sds((FOX_W, s_len), BF16), t_spec)],
                        qkv_epi, m=s_len, tm=tl, n=QKV_W, tn=FOX_W, arbitrary=True)
    rest_tn = 640
    f_tile, f_lane = F_OFF // rest_tn, F_OFF % rest_tn

    def rest_epi(accs, ex, out):
        out[0][...] = accs[0].astype(BF16)

        @pl.when(pl.program_id(1) == f_tile)
        def _():
            out[1][...] = accs[0][:, f_lane:f_lane + LANES]

    prest, f_logit = _mm("proj_rest", [(h, w_rest, False, None)], [],
                         [(_sds((s_len, REST_W), BF16), _tile(tl, rest_tn)), (_sds((s_len, LANES), F32), _row(tl, LANES))],
                         rest_epi, m=s_len, tm=tl, n=REST_W, tn=rest_tn, arbitrary=True)

    c_col = _forget_cumsum(f_logit, b_f_pad, tc)
    o, o_t, lse_row, (later_g,) = _attn_fwd(qkv, v_t, c_col, ta, ta, comm=[later_plan])
    w_a_f, w_b_f, w_o_f, w_up_f, w_down_f = unflatten(later_g)
    sg, sg_t = _sgu_fwd(prest, ln_v_g, ln_v_b, w_stack, b_pair, tm)

    def merge_epi(accs, ex, out):
        ya, yb = accs
        sa, sb = _sigmoid(ex[0][...].astype(F32)), _sigmoid(ex[1][...].astype(F32))
        merged = (sa * ya + sb * yb).astype(BF16)
        out[0][...] = merged
        out[1][...] = ya.astype(BF16)
        out[2][...] = yb.astype(BF16)
        out[3][...] = jnp.transpose(merged)

    merged, ya, yb, merged_t = _mm(
        "merge", [(o, w_a_f, False, None), (sg, w_b_f, False, None)],
        [(prest, _tile(tm, d, GA_OFF // d)), (prest, _tile(tm, d, GB_OFF // d))],
        [(_sds((s_len, d), BF16), _tile(tm, d))] * 3 + [(_sds((d, s_len), BF16), _tile_t(tm, d))],
        merge_epi, m=s_len, tm=tm, n=d, tn=d)

    def resid_epi(accs, ex, out):
        x1v = ex[0][...] + accs[0]
        out[0][...] = x1v
        r = lax.rsqrt(jnp.mean(x1v * x1v, axis=-1, keepdims=True) + EPS)
        h2v = (x1v * r * ex[1][...]).astype(BF16)
        out[1][...] = h2v
        out[2][...] = jnp.transpose(h2v)
        out[3][...] = r

    x1, h2, h2_t, r2 = _mm(
        "out_proj", [(merged, w_o_f, False, None)], [(xs, _tile(tm, d)), (norm2_g, _whole((1, d)))],
        [(_sds((s_len, d), F32), _tile(tm, d)), (_sds((s_len, d), BF16), _tile(tm, d)),
         (_sds((d, s_len), BF16), _tile_t(tm, d)), (_sds((s_len, 1), F32), _row(tm, 1))],
        resid_epi, m=s_len, tm=tm, n=d, tn=d)

    def up_epi(accs, ex, out):
        act = jnp.square(jnp.maximum(accs[0], 0.0)).astype(BF16)
        out[0][...] = act
        out[1][...] = jnp.transpose(act)

    act, act_t = _mm(
        "mlp_up", [(h2, w_up_f, False, None)], [],
        [(_sds((s_len, D_FF), BF16), _tile(tl, 512)), (_sds((D_FF, s_len), BF16), _tile_t(tl, 512))],
        up_epi, m=s_len, tm=tl, n=D_FF, tn=512)

    def first_step():
        return jnp.logical_and(pl.program_id(0) == 0, pl.program_id(1) == 0)

    def accumulate(ref, val):
        @pl.when(first_step())
        def _():
            ref[...] = val

        @pl.when(jnp.logical_not(first_step()))
        def _():
            ref[...] += val

    def final_epi(accs, ex, out):
        x1_ref, t_ref, g_ref = ex
        x2 = x1_ref[...] + accs[0]
        rf = lax.rsqrt(jnp.mean(x2 * x2, axis=-1, keepdims=True) + EPS)
        xh = x2 * rf
        gf = g_ref[...]
        err = xh * gf - t_ref[...]
        dy = err * (1.0 / d)
        dx2 = _rms_bwd(xh, rf, gf, dy)
        out[0][...] = dx2
        accumulate(out[1], jnp.sum(dy * xh, axis=0, keepdims=True))
        part = 0.5 * jnp.sum(jnp.sum(err * err, axis=-1, keepdims=True) * (1.0 / d), axis=0, keepdims=True)
        accumulate(out[2], jnp.broadcast_to(part, (1, LANES)))
        out[3][...] = dx2.astype(BF16)

    gf2 = normf_g.reshape(1, d)
    dx2, g_normf, loss_part, dx2_16 = _mm(
        "mlp_down_loss", [(act, w_down_f, False, None)],
        [(x1, _row(tr, d)), (target, _row(tr, d)), (gf2, _whole((1, d)))],
        [(_sds((s_len, d), F32), _row(tr, d)), (_sds((1, d), F32), _whole((1, d))), (_sds((1, LANES), F32), _whole((1, LANES))),
         (_sds((s_len, d), BF16), _row(tr, d))],
        final_epi, m=s_len, tm=tr, n=d, tn=d, arbitrary=True)

    def dact_epi(accs, ex, out):
        out[0][...] = (accs[0] * (2.0 * jnp.sqrt(ex[0][...].astype(F32)))).astype(BF16)

    (da,) = _mm("mlp_down_bwd", [(dx2_16, w_down_f, True, None)], [(act, _tile(tl, 512))],
                [(_sds((s_len, D_FF), BF16), _tile(tl, 512))], dact_epi, m=s_len, tm=tl, n=D_FF, tn=512)
    g_down = _grad_w("grad_w_down", act_t, dx2_16, tk=1024, tn=d, ts=tl)
    g_up = _grad_w("grad_w_up", h2_t, da, tk=d, tn=1024, ts=tl, block_cols=D_FF // N_DEV)

    def dh2_epi(accs, ex, out):
        x1_ref, r_ref, g_ref, dx2_ref = ex
        r = r_ref[...]
        xh = x1_ref[...] * r
        dh2 = accs[0]
        out[0][...] = dx2_ref[...] + _rms_bwd(xh, r, g_ref[...], dh2)
        accumulate(out[1], jnp.sum(dh2 * xh, axis=0, keepdims=True))

    my_c = lax.axis_index("c")
    my_chip = 2 * lax.axis_index("x") + lax.axis_index("y")
    idx = jnp.stack([my_c, my_chip]).astype(jnp.int32)
    parts16, owns = {}, {}

    def split_cores(g8):
        return g8.reshape((4, 2) + g8.shape[1:])

    def pair_sums(names, grads4, from_sibling):
        for name, g4, recv in zip(names, grads4, from_sibling):
            parts16[name], owns[name] = _pair_sum("grad_pair_sum_" + name, g4, recv, idx, min(512, g4.shape[2]))

    grads4_mlp = [split_cores(g_up), split_cores(g_down.reshape(N_DEV, D_FF // N_DEV, d))]
    (dx1, g_norm2), from_sibling = _mm(
        "mlp_up_bwd", [(da, w_up_f, True, None)],
        [(x1, _row(tr, d)), (r2, _row(tr, 1)), (norm2_g, _whole((1, d))), (dx2, _row(tr, d))],
        [(_sds((s_len, d), F32), _row(tr, d)), (_sds((1, d), F32), _whole((1, d)))],
        dh2_epi, m=s_len, tm=tr, n=d, tn=d, arbitrary=True, comm=[_pair_exchange_plan(grads4_mlp)])
    pair_sums(("w_up", "w_down"), grads4_mlp, from_sibling)

    def dmerge_epi(accs, ex, out):
        dm = accs[0]
        sa, sb = _sigmoid(ex[0][...].astype(F32)), _sigmoid(ex[1][...].astype(F32))
        out[0][...] = (dm * sa).astype(BF16)
        out[1][...] = (dm * sb).astype(BF16)
        out[2][...] = (dm * ex[2][...] * sa * (1.0 - sa)).astype(BF16)
        out[3][...] = (dm * ex[3][...] * sb * (1.0 - sb)).astype(BF16)

    dya, dyb, dga, dgb = _mm(
        "out_proj_bwd", [(dx1, w_o_f, True, None)],
        [(prest, _tile(tm, d, GA_OFF // d)), (prest, _tile(tm, d, GB_OFF // d)), (ya, _tile(tm, d)), (yb, _tile(tm, d))],
        [(_sds((s_len, d), BF16), _tile(tm, d))] * 4, dmerge_epi, m=s_len, tm=tm, n=d, tn=d)
    g_o = _grad_w("grad_w_o", merged_t, dx1, tk=d, tn=d, ts=tl).reshape(N_DEV, d // N_DEV, d)
    def col_blocks(g):
        return jnp.transpose(g.reshape(g.shape[0], N_DEV, g.shape[1] // N_DEV), (1, 0, 2))

    g_a = col_blocks(_grad_w("grad_w_a", o_t, dya, tk=FOX_W, tn=d, ts=tl))
    g_b = col_blocks(_grad_w("grad_w_b", sg_t, dyb, tk=SGU_W, tn=d, ts=tl))

    def do_epi(accs, ex, out):
        do = accs[0]
        out[0][...] = do
        out[1][...] = _dot_f32(do * ex[0][...], ex[1][...])

    grads4_mix = [split_cores(g) for g in (g_a, g_b, g_o)]
    (do, delta), from_sibling = _mm(
        "attn_out_bwd", [(dya, w_a_f, True, None)], [(o, _row(tm, FOX_W)), (head_sel, _whole((FOX_W, LANES)))],
        [(_sds((s_len, FOX_W), F32), _row(tm, FOX_W)), (_sds((s_len, LANES), F32), _row(tm, LANES))],
        do_epi, m=s_len, tm=tm, n=FOX_W, tn=FOX_W, comm=[_pair_exchange_plan(grads4_mix)])
    pair_sums(("w_a", "w_b", "w_o"), grads4_mix, from_sibling)
    (dsg,) = _mm("sgu_out_bwd", [(dyb, w_b_f, True, None)], [], [(_sds((s_len, SGU_W), F32), _tile(tm, SGU_W))],
                 store(F32), m=s_len, tm=tm, n=SGU_W, tn=SGU_W)

    du, dsv, dw_pairs, db_pos, g_ln_g, g_ln_b = _sgu_bwd(prest, dsg, ln_v_g, ln_v_b, w_stack, wt_stack, b_pair, tm)
    g_w_sgu = jnp.where(sgu_mask[None], dw_pairs.reshape(SGU_G, SGU_LEN, SGU_LEN), 0.0)
    g_b_sgu = jnp.transpose(jnp.sum(db_pos.reshape(SGU_LEN, SGU_G, SGU_W // SGU_G), axis=-1))

    delta_row = jnp.transpose(delta[:, :N_HEADS]).reshape(N_HEADS, 1, s_len)
    early = ("w_a", "w_b", "w_o", "w_up", "w_down")
    small_early = _pack_rows((g_w_sgu, g_b_sgu, g_norm2, g_normf, g_ln_g, g_ln_b))
    (dq, dk, dv, dc_rows_blk, dc_cols), (small_early_all, *from_chips_early) = _attn_bwd(
        qkv, k_t, do, c_col, lse_row, delta_row, ta, ta,
        comm=[_gather_plan(small_early), _chip_exchange_plan([parts16[n] for n in early])])
    dc_rows = jnp.transpose(dc_rows_blk.reshape(s_len // ta, N_HEADS, ta), (0, 2, 1)).reshape(s_len, N_HEADS)
    dc_rows = jnp.pad(dc_rows, ((0, 0), (0, LANES - N_HEADS)))
    dfl, g_bf = _forget_bwd(dc_rows, dc_cols, f_logit, b_f_pad, tc)

    dp = (dq, dk, dv, du, dsv, dga, dgb)
    g_in_cols = [_grad_w("grad_w_in_%d" % k, h_t, seg, tk=d, tn=seg.shape[1], ts=tl) for k, seg in enumerate(dp)]
    g_f = _grad_w("grad_w_in_f", h_t, dfl, tk=d, tn=LANES, ts=tl)[:, :N_HEADS]
    g_in_full = jnp.concatenate(g_in_cols[:3] + [g_f] + g_in_cols[3:], axis=1)
    g_in = jnp.transpose(g_in_full.reshape(d, N_DEV, IN_SHARD), (1, 0, 2))

    def dx_epi(accs, ex, out):
        x_ref, r_ref, g_ref, dx1_ref = ex
        dh = accs[0]
        for extra in accs[1:]:
            dh = dh + extra
        r = r_ref[...]
        xh = x_ref[...] * r
        out[0][...] = dx1_ref[...] + _rms_bwd(xh, r, g_ref[...], dh)
        accumulate(out[1], jnp.sum(dh * xh, axis=0, keepdims=True))

    rest_cols = ((du, U_OFF, 512), (dsv, SV_OFF, 512), (dga, GA_OFF, 1024), (dgb, GB_OFF, 1024), (dfl, F_OFF, LANES))
    dx_pairs = [(seg, w_qkv, True, (512 * k, 512 * (k + 1))) for k, seg in enumerate((dq, dk, dv))]
    dx_pairs += [(seg, w_rest, True, (lo, lo + width)) for seg, lo, width in rest_cols]
    grads4_in = [split_cores(g_in)]
    pair_sums(("w_in",), grads4_in, _run_comm("grad_pair_exchange_w_in", [_pair_exchange_plan(grads4_in)]))
    (grad_x, g_norm1), (from_chips_in,) = _mm(
        "proj_bwd", dx_pairs,
        [(xs, _row(tr, d)), (r1, _row(tr, 1)), (norm1_g, _whole((1, d))), (dx1, _row(tr, d))],
        [(_sds((s_len, d), F32), _row(tr, d)), (_sds((1, d), F32), _whole((1, d)))],
        dx_epi, m=s_len, tm=tr, n=d, tn=d, arbitrary=True, comm=[_chip_exchange_plan([parts16["w_in"]])])
    small_late = _pack_rows((g_bf[:, :N_HEADS], g_norm1, loss_part[0, 0]))
    (small_late_all,) = _run_comm("gather_last_grads", [_gather_plan(small_late)])
    small_all = jnp.concatenate([small_early_all, small_late_all], axis=1)
    from_chips = dict(zip(early, from_chips_early), w_in=from_chips_in)

    names = ("w_in", "w_a", "w_b", "w_o", "w_up", "w_down")
    moments_m = (m_w_in, m_w_a, m_w_b, m_w_o, m_w_up, m_w_down)
    moments_v = (v_w_in, v_w_a, v_w_b, v_w_o, v_w_up, v_w_down)
    big_out = {}
    for name, w, m, v in zip(names, big, moments_m, moments_v):
        own = owns[name]
        res = _adamw_shard("adamw_" + name, own, from_chips[name], w, m[0], v[0], min(512, own.shape[0]))
        big_out[name] = [t[None] for t in res]

    zero = jnp.zeros((), F32)
    small_w = _pack_small((w_sgu, b_sgu, norm2_g, normf_g, ln_v_g, ln_v_b, b_f, norm1_g, zero))
    small_m = _pack_small((m_w_sgu, m_b_sgu, m_norm2_g, m_normf_g, m_ln_v_g, m_ln_v_b, m_b_f, m_norm1_g, zero))
    small_v = _pack_small((v_w_sgu, v_b_sgu, v_norm2_g, v_normf_g, v_ln_v_g, v_ln_v_b, v_b_f, v_norm1_g, zero))
    small_res = [_unpack_small(t) for t in _adamw_small(small_all, small_w, small_m, small_v)]
    small_names = [n for n, _ in _SMALL]
    small_out = {n: [res[k] for res in small_res] for k, n in enumerate(small_names)}
    loss = small_out["loss"][0]

    order = ("norm1_g", "w_in", "b_f", "ln_v_g", "ln_v_b", "w_sgu", "b_sgu", "w_a", "w_b", "w_o", "norm2_g", "w_up",
             "w_down", "normf_g")
    table = {**big_out, **small_out}
    outs = [loss, grad_x[None]]
    for kind in range(4):
        outs += [table[n][kind] for n in order]
    return tuple(outs)
```

```python
import math

import jax
import jax.numpy as jnp
from jax import lax
from jax.experimental import pallas as pl
from jax.experimental.pallas import tpu as pltpu

F32 = jnp.float32
BF16 = jnp.bfloat16

N_DEV = 8
D_MODEL = 1024
N_HEADS = 8
HEAD_DIM = 64
FOX_W = N_HEADS * HEAD_DIM
SGU_G = 8
SGU_W = 512
SGU_LEN = 128
CHUNK = 64
D_FF = 4 * D_MODEL
IN_COLS = 3 * FOX_W + N_HEADS + 2 * SGU_W + 2 * D_MODEL
IN_SHARD = IN_COLS // N_DEV
LANES = 128
QKV_W = 3 * FOX_W
U_OFF, SV_OFF, GA_OFF, GB_OFF, F_OFF = 0, 512, 1024, 2048, 3072
REST_W = F_OFF + LANES
EPS = 1e-6
NEG = -1e30

ADAM_LR = 0.001
ADAM_B1 = 0.9
ADAM_B2 = 0.999
ADAM_EPS = 1e-08
ADAM_WD = 0.01
ADAM_STEP = 10

VMEM_LIMIT = 56 * 1024 * 1024
MESH = pl.DeviceIdType.MESH


def _params(sem=None):
    return pltpu.CompilerParams(dimension_semantics=sem, vmem_limit_bytes=VMEM_LIMIT)


def _dot(a, b):
    return jnp.dot(a, b, preferred_element_type=F32)


def _dot_nt(a, b):
    return lax.dot_general(a, b, (((1,), (1,)), ((), ())), preferred_element_type=F32)


def _dot_f32(a, b):
    return jnp.dot(a, b, preferred_element_type=F32, precision=lax.Precision.HIGHEST)


def _sigmoid(x):
    return 1.0 / (1.0 + jnp.exp(-x))


def _log_sigmoid(z):
    return jnp.minimum(z, 0.0) - jnp.log(1.0 + jnp.exp(-jnp.abs(z)))


_GELU_K = math.sqrt(2.0 / math.pi)
_GELU_C = 0.044715


def _gelu(x):
    t = jnp.tanh(_GELU_K * (x + _GELU_C * (x * x * x)))
    return 0.5 * x * (1.0 + t)


def _gelu_grad(x):
    x2 = x * x
    t = jnp.tanh(_GELU_K * (x + _GELU_C * (x2 * x)))
    return 0.5 * (1.0 + t) + 0.5 * x * (1.0 - t * t) * (_GELU_K * (1.0 + 3.0 * _GELU_C * x2))


def _rms_bwd(xh, r, g, dy):
    gy = dy * g
    return r * (gy - xh * jnp.mean(xh * gy, axis=-1, keepdims=True))


def _lane_lt64(shape):
    return lax.broadcasted_iota(jnp.int32, shape, len(shape) - 1) < HEAD_DIM


class _Comm:
    def __init__(self, arrays, out_shapes, sems, start, finish, mid=None):
        self.arrays, self.out_shapes, self.sems = list(arrays), list(out_shapes), list(sems)
        self.start, self.mid, self.finish = start, mid, finish


def _comm_phase(plans, phase, in_refs, out_refs, sem_refs):
    ia = io = ks = 0
    for plan in plans:
        na, no, ns = len(plan.arrays), len(plan.out_shapes), len(plan.sems)
        fn = getattr(plan, phase)
        if fn is not None:
            fn(in_refs[ia:ia + na], out_refs[io:io + no], sem_refs[ks:ks + ns])
        ia, io, ks = ia + na, io + no, ks + ns


def _comm_operands(plans):
    arrays = [a for plan in plans for a in plan.arrays]
    out_shapes = [o for plan in plans for o in plan.out_shapes]
    sems = [s for plan in plans for s in plan.sems]
    return arrays, out_shapes, sems


_ANY = pl.BlockSpec(memory_space=pl.ANY)


def _run_comm(name, plans):
    arrays, out_shapes, sems = _comm_operands(plans)
    n_in, n_out = len(arrays), len(out_shapes)

    def body(*refs):
        parts = refs[:n_in], refs[n_in:n_in + n_out], refs[n_in + n_out:]
        for phase in ("start", "mid", "finish"):
            _comm_phase(plans, phase, *parts)

    return pl.pallas_call(
        body, name=name, out_shape=out_shapes, in_specs=[_ANY] * n_in, out_specs=[_ANY] * n_out, scratch_shapes=sems,
    )(*arrays)


def _gather_plan(shard):
    def setup(ins, outs, sems):
        (x_ref,), (out_ref,), (send_sems, recv_sems, local_sem) = ins, outs, sems
        x, y, c = lax.axis_index("x"), lax.axis_index("y"), lax.axis_index("c")
        me, sibling = (x, y, c), (x, y, 1 - c)
        chips = [(1 - x, y), (x, 1 - y), (1 - x, 1 - y)]

        def rows(px, py, pc):
            return out_ref.at[4 * px + 2 * py + pc]

        def copy(k, block, to, src=None):
            return pltpu.make_async_remote_copy(
                src_ref=rows(*block) if src is None else src,
                dst_ref=rows(*block),
                send_sem=send_sems.at[k],
                recv_sem=recv_sems.at[k],
                device_id=to,
                device_id_type=MESH,
            )

        mine = pltpu.make_async_copy(x_ref, rows(*me), local_sem)
        first = [copy(0, me, sibling, src=x_ref)]
        first += [copy(1 + j, me, (*chip, c), src=x_ref) for j, chip in enumerate(chips)]
        passed = [copy(4 + j, (*chip, c), sibling) for j, chip in enumerate(chips)]
        landed = [copy(1 + j, (*chip, c), me) for j, chip in enumerate(chips)]
        from_sibling = [copy(0, sibling, me)] + [copy(4 + j, (*chip, 1 - c), me) for j, chip in enumerate(chips)]
        return mine, first, passed, landed, from_sibling

    def start(ins, outs, sems):
        mine, first, _, _, _ = setup(ins, outs, sems)
        mine.start()
        for cp in first:
            cp.start()

    def mid(ins, outs, sems):
        _, _, passed, landed, _ = setup(ins, outs, sems)
        for arrived, onward in zip(landed, passed):
            arrived.wait_recv()
            onward.start()

    def finish(ins, outs, sems):
        mine, first, passed, _, from_sibling = setup(ins, outs, sems)
        for cp in from_sibling:
            cp.wait_recv()
        for cp in first + passed:
            cp.wait_send()
        mine.wait()

    return _Comm([shard], [jax.ShapeDtypeStruct((N_DEV,) + shard.shape, shard.dtype)],
                 [pltpu.SemaphoreType.DMA((7,)), pltpu.SemaphoreType.DMA((7,)), pltpu.SemaphoreType.DMA],
                 start, finish, mid)


def _start_all(copies):
    for cp in copies:
        cp.start()


def _wait_all(copies):
    for cp in copies:
        cp.wait_recv()
    for cp in copies:
        cp.wait_send()


def _pair_exchange_plan(grads):
    n = len(grads)

    def copies(ins, outs, sems):
        send_sems, recv_sems = sems
        x, y, c = lax.axis_index("x"), lax.axis_index("y"), lax.axis_index("c")
        return [
            pltpu.make_async_remote_copy(
                src_ref=ins[k].at[:, 1 - c],
                dst_ref=outs[k],
                send_sem=send_sems.at[k],
                recv_sem=recv_sems.at[k],
                device_id=(x, y, 1 - c),
                device_id_type=MESH,
            )
            for k in range(n)
        ]

    return _Comm(grads, [jax.ShapeDtypeStruct((4,) + g.shape[2:], g.dtype) for g in grads],
                 [pltpu.SemaphoreType.DMA((n,)), pltpu.SemaphoreType.DMA((n,))],
                 lambda *refs: _start_all(copies(*refs)), lambda *refs: _wait_all(copies(*refs)))


def _chip_exchange_plan(parts):
    n = len(parts)

    def copies(ins, outs, sems):
        send_sems, recv_sems = sems
        x, y, c = lax.axis_index("x"), lax.axis_index("y"), lax.axis_index("c")
        chips = [(1 - x, y), (x, 1 - y), (1 - x, 1 - y)]
        return [
            pltpu.make_async_remote_copy(
                src_ref=ins[k].at[2 * px + py],
                dst_ref=outs[k].at[j],
                send_sem=send_sems.at[3 * k + j],
                recv_sem=recv_sems.at[3 * k + j],
                device_id=(px, py, c),
                device_id_type=MESH,
            )
            for k in range(n) for j, (px, py) in enumerate(chips)
        ]

    return _Comm(parts, [jax.ShapeDtypeStruct((3,) + p.shape[1:], p.dtype) for p in parts],
                 [pltpu.SemaphoreType.DMA((3 * n,)), pltpu.SemaphoreType.DMA((3 * n,))],
                 lambda *refs: _start_all(copies(*refs)), lambda *refs: _wait_all(copies(*refs)))


def _mm(name, pairs, extras, outs, epi, *, m, tm, n, tn, arbitrary=False, comm=()):
    nj = n // tn
    a_arrays, a_specs, b_arrays, b_specs, b_index = [], [], [], [], []
    for a, b, nt, cols in pairs:
        a_arrays.append(a)
        a_specs.append(pl.BlockSpec((tm, a.shape[1]), lambda i, j: (i, 0)))
        known = [k for k, other in enumerate(b_arrays) if other is b]
        if known:
            b_index.append(known[0])
            continue
        b_index.append(len(b_arrays))
        b_arrays.append(b)
        if cols is not None:
            assert nj == 1
            b_specs.append(pl.BlockSpec(b.shape, lambda i, j: (0, 0)))
        elif nt:
            b_specs.append(pl.BlockSpec((tn, b.shape[1]), lambda i, j: (j, 0)))
        else:
            b_specs.append(pl.BlockSpec((b.shape[0], tn), lambda i, j: (0, j)))
    comm_arrays, comm_outs, comm_sems = _comm_operands(comm)
    arrays = a_arrays + b_arrays + [arr for arr, _ in extras] + comm_arrays
    in_specs = a_specs + b_specs + [spec for _, spec in extras] + [_ANY] * len(comm_arrays)
    n_a, n_b, n_extras, n_ci, n_out, n_co = len(a_arrays), len(b_arrays), len(extras), len(comm_arrays), len(outs), len(comm_outs)
    ni = m // tm

    def body(*refs):
        a_refs = refs[:n_a]
        b_refs = refs[n_a:n_a + n_b]
        ex = refs[n_a + n_b:n_a + n_b + n_extras]
        n_in = n_a + n_b + n_extras + n_ci
        comm_refs = refs[n_in - n_ci:n_in], refs[n_in + n_out:n_in + n_out + n_co], refs[n_in + n_out + n_co:]
        out = refs[n_in:n_in + n_out]
        if comm:
            @pl.when(jnp.logical_and(pl.program_id(0) == 0, pl.program_id(1) == 0))
            def _():
                _comm_phase(comm, "start", *comm_refs)

        accs = []
        for p, (_, _, nt, cols) in enumerate(pairs):
            av = a_refs[p][...]
            if av.dtype != BF16:
                av = av.astype(BF16)
            b_ref = b_refs[b_index[p]]
            if cols is None:
                bv = b_ref[...]
            else:
                bv = b_ref[:, cols[0]:cols[1]] if nt else b_ref[cols[0]:cols[1], :]
            accs.append(_dot_nt(av, bv) if nt else _dot(av, bv))
        epi(accs, ex, out)
        if comm:
            mid_row = ni // 2 if ni >= 3 else ni - 1
            mid_col = 0 if ni >= 3 else nj - 1

            @pl.when(jnp.logical_and(pl.program_id(0) == mid_row, pl.program_id(1) == mid_col))
            def _():
                _comm_phase(comm, "mid", *comm_refs)

            @pl.when(jnp.logical_and(pl.program_id(0) == ni - 1, pl.program_id(1) == nj - 1))
            def _():
                _comm_phase(comm, "finish", *comm_refs)

    sem = ("arbitrary", "arbitrary") if arbitrary or comm else ("parallel", "parallel")
    res = pl.pallas_call(
        body,
        name=name,
        grid=(ni, nj),
        in_specs=in_specs,
        out_specs=[spec for _, spec in outs] + [_ANY] * n_co,
        out_shape=[shape for shape, _ in outs] + comm_outs,
        scratch_shapes=comm_sems,
        compiler_params=_params(sem),
    )(*arrays)
    return (res[:n_out], res[n_out:]) if comm else res


def _tile(tm, tn, off=0):
    return pl.BlockSpec((tm, tn), lambda i, j: (i, j + off))


def _row(tm, w, blk=0):
    return pl.BlockSpec((tm, w), lambda i, j: (i, blk))


def _whole(shape):
    zeros = (0,) * len(shape)
    return pl.BlockSpec(shape, lambda i, j: zeros)


def _sds(shape, dtype):
    return jax.ShapeDtypeStruct(shape, dtype)


def _tile_t(tm, tn):
    return pl.BlockSpec((tn, tm), lambda i, j: (j, i))


def _grad_w(name, a_t, g, *, tk, tn, ts, block_cols=None):
    ka, s_len = a_t.shape
    n = g.shape[1]
    width = tn if block_cols is None else block_cols

    def body(a_ref, g_ref, o_ref):
        first = pl.program_id(2) == 0
        gv = g_ref[...].astype(BF16)
        for b in range(tn // width):
            part = _dot(a_ref[...], gv[:, b * width:(b + 1) * width])
            dst = o_ref if block_cols is None else o_ref.at[b]

            @pl.when(first)
            def _():
                dst[...] = part

            @pl.when(jnp.logical_not(first))
            def _():
                dst[...] += part

    if block_cols is None:
        out_shape = _sds((ka, n), F32)
        out_spec = pl.BlockSpec((tk, tn), lambda i, j, s: (i, j))
    else:
        out_shape = _sds((n // width, ka, width), F32)
        out_spec = pl.BlockSpec((tn // width, tk, width), lambda i, j, s: (j, i, 0))
    return pl.pallas_call(
        body,
        name=name,
        grid=(ka // tk, n // tn, s_len // ts),
        in_specs=[pl.BlockSpec((tk, ts), lambda i, j, s: (i, s)), pl.BlockSpec((ts, tn), lambda i, j, s: (s, j))],
        out_specs=out_spec,
        out_shape=out_shape,
        compiler_params=_params(("parallel", "parallel", "arbitrary")),
    )(a_t, g)


def _rms_fwd(name, x, g, tm, comm=()):
    s_len, d = x.shape
    steps = s_len // tm
    comm_arrays, comm_outs, comm_sems = _comm_operands(comm)
    n_ci, n_co = len(comm_arrays), len(comm_outs)

    def body(x_ref, g_ref, *rest):
        comm_refs = rest[:n_ci], rest[n_ci + 3:n_ci + 3 + n_co], rest[n_ci + 3 + n_co:]
        h_ref, ht_ref, r_ref = rest[n_ci:n_ci + 3]
        for phase, at in (("start", 0), ("mid", steps // 2)):
            if comm:
                @pl.when(pl.program_id(0) == at)
                def _():
                    _comm_phase(comm, phase, *comm_refs)

        xv = x_ref[...]
        r = lax.rsqrt(jnp.mean(xv * xv, axis=-1, keepdims=True) + EPS)
        h = (xv * r * g_ref[...]).astype(BF16)
        h_ref[...] = h
        ht_ref[...] = jnp.transpose(h)
        r_ref[...] = r
        if comm:
            @pl.when(pl.program_id(0) == steps - 1)
            def _():
                _comm_phase(comm, "finish", *comm_refs)

    res = pl.pallas_call(
        body,
        name=name,
        grid=(steps,),
        in_specs=[pl.BlockSpec((tm, d), lambda i: (i, 0)), pl.BlockSpec((1, d), lambda i: (0, 0))] + [_ANY] * n_ci,
        out_specs=[pl.BlockSpec((tm, d), lambda i: (i, 0)), pl.BlockSpec((d, tm), lambda i: (0, i)),
                   pl.BlockSpec((tm, 1), lambda i: (i, 0))] + [_ANY] * n_co,
        out_shape=[_sds((s_len, d), BF16), _sds((d, s_len), BF16), _sds((s_len, 1), F32)] + comm_outs,
        scratch_shapes=comm_sems,
        compiler_params=_params(("arbitrary",) if comm else ("parallel",)),
    )(x, g, *comm_arrays)
    return res[0], res[1], res[2], res[3:]


def _forget_cumsum(prest, b_f_pad, tc):
    s_len = prest.shape[0]

    def body(f_ref, b_ref, c_ref, carry):
        @pl.when(pl.program_id(0) == 0)
        def _():
            carry[...] = jnp.zeros_like(carry)

        logf = _log_sigmoid(f_ref[...] + b_ref[...])
        row = lax.broadcasted_iota(jnp.int32, (tc, tc), 0)
        col = lax.broadcasted_iota(jnp.int32, (tc, tc), 1)
        tri = (row >= col).astype(F32)
        c = _dot_f32(tri, logf) + carry[...]
        c_ref[...] = c
        carry[...] = c[tc - 1:tc, :]

    return pl.pallas_call(
        body,
        name="forget_cumsum",
        grid=(s_len // tc,),
        in_specs=[pl.BlockSpec((tc, LANES), lambda i: (i, 0)), pl.BlockSpec((1, LANES), lambda i: (0, 0))],
        out_specs=pl.BlockSpec((tc, LANES), lambda i: (i, 0)),
        out_shape=_sds((s_len, LANES), F32),
        scratch_shapes=[pltpu.VMEM((1, LANES), F32)],
        compiler_params=_params(("arbitrary",)),
    )(prest, b_f_pad)


def _stack_heads(pair, lt64):
    zero = jnp.zeros_like(pair)
    return jnp.concatenate([jnp.where(lt64, pair, zero), jnp.where(lt64, zero, pair)], axis=0)


def _score_tiles(q_ref, k_ref, ck_ref, st_sc, tk):
    lt64 = _lane_lt64((tk, LANES))
    for p in range(N_HEADS // 2):
        lanes = slice(p * LANES, (p + 1) * LANES)
        q_pair = q_ref[:, lanes] * jnp.asarray(HEAD_DIM ** -0.5, BF16)
        st2 = _dot_nt(_stack_heads(k_ref[:, lanes], lt64), q_pair)
        for half in range(2):
            h = 2 * p + half
            st_sc[h] = st2[half * tk:(half + 1) * tk] - ck_ref[:, h:h + 1]


ROW_CHUNK = 64


def _row_chunks(tk):
    rc = min(ROW_CHUNK, tk)
    return [slice(r, r + rc) for r in range(0, tk, rc)]


def _by_sublane(x):
    return x.reshape(x.shape[0] // 8, 8, x.shape[1])


def _softmax_update(st_sc, p_sc, m_sc, l_sc, tk, tq):
    alphas = []
    for h in range(N_HEADS):
        top8 = jnp.full((8, tq), NEG, F32)
        for rows in _row_chunks(tk):
            top8 = jnp.maximum(top8, jnp.max(_by_sublane(st_sc[h, rows, :]), axis=0))
        m_old = m_sc[h]
        m_new = jnp.maximum(m_old, jnp.max(top8, axis=0, keepdims=True))
        sum8 = jnp.zeros((8, tq), F32)
        for rows in _row_chunks(tk):
            pt = jnp.exp(st_sc[h, rows, :] - m_new)
            p_sc[h, rows, :] = pt.astype(BF16)
            sum8 = sum8 + jnp.sum(_by_sublane(pt), axis=0)
        alpha = jnp.exp(m_old - m_new)
        l_sc[h] = alpha * l_sc[h] + jnp.sum(sum8, axis=0, keepdims=True)
        m_sc[h] = m_new
        alphas.append(alpha)
    return alphas


def _mask_diagonal(st_sc, i, j, tq, tk):
    @pl.when((j + 1) * tk - 1 > i * tq)
    def _():
        key = j * tk + lax.broadcasted_iota(jnp.int32, (tk, tq), 0)
        query = i * tq + lax.broadcasted_iota(jnp.int32, (tk, tq), 1)
        st_sc[...] = jnp.where((query >= key)[None], st_sc[...], NEG)


def _attn_fwd(qkv, v_t, c_col, tq, tk, comm=()):
    s_len = qkv.shape[0]
    ratio = tq // tk
    steps = [(i, j) for i in range(s_len // tq) for j in range((i + 1) * ratio)]
    i_tab = jnp.asarray([i for i, _ in steps], jnp.int32)
    j_tab = jnp.asarray([j for _, j in steps], jnp.int32)

    comm_arrays, comm_outs, comm_sems = _comm_operands(comm)
    n_ci, n_co = len(comm_arrays), len(comm_outs)

    def body(i_ref, j_ref, q_ref, k_ref, vt_ref, ck_ref, *rest):
        comm_refs = rest[:n_ci], rest[n_ci + 3:n_ci + 3 + n_co], rest[n_ci + 3 + n_co + 5:]
        o_ref, ot_ref, lse_ref = rest[n_ci:n_ci + 3]
        acc_t, m_sc, l_sc, st_sc, p_sc = rest[n_ci + 3 + n_co:n_ci + 3 + n_co + 5]
        n = pl.program_id(0)
        i, j = i_ref[n], j_ref[n]
        for phase, at in (("start", 0), ("mid", (2 * len(steps)) // 3)):
            if comm:
                @pl.when(n == at)
                def _():
                    _comm_phase(comm, phase, *comm_refs)

        @pl.when(j == 0)
        def _():
            acc_t[...] = jnp.zeros_like(acc_t)
            m_sc[...] = jnp.full_like(m_sc, NEG)
            l_sc[...] = jnp.zeros_like(l_sc)

        _score_tiles(q_ref, k_ref, ck_ref, st_sc, tk)
        _mask_diagonal(st_sc, i, j, tq, tk)
        alpha = _softmax_update(st_sc, p_sc, m_sc, l_sc, tk, tq)
        top = lax.broadcasted_iota(jnp.int32, (LANES, tq), 0) < HEAD_DIM
        for p in range(N_HEADS // 2):
            lanes = slice(p * LANES, (p + 1) * LANES)
            vt_pair = vt_ref[lanes, :]
            pv = jnp.where(top, _dot(vt_pair, p_sc[2 * p]), _dot(vt_pair, p_sc[2 * p + 1]))
            acc_t[lanes, :] = acc_t[lanes, :] * jnp.where(top, alpha[2 * p], alpha[2 * p + 1]) + pv

        @pl.when(j == (i + 1) * ratio - 1)
        def _():
            for p in range(N_HEADS // 2):
                lanes = slice(p * LANES, (p + 1) * LANES)
                l_pair = jnp.where(top, l_sc[2 * p], l_sc[2 * p + 1])
                o_t = acc_t[lanes, :] / l_pair
                o_ref[:, lanes] = jnp.transpose(o_t)
                ot_ref[lanes, :] = o_t.astype(BF16)
            lse_ref[...] = m_sc[...] + jnp.log(l_sc[...])

        if comm:
            @pl.when(n == len(steps) - 1)
            def _():
                _comm_phase(comm, "finish", *comm_refs)

    stat = pltpu.VMEM((N_HEADS, 1, tq), F32)
    res = pl.pallas_call(
        body,
        name="attn_fwd",
        grid_spec=pltpu.PrefetchScalarGridSpec(
            num_scalar_prefetch=2,
            grid=(len(steps),),
            in_specs=[
                pl.BlockSpec((tq, FOX_W), lambda n, it, jt: (it[n], 0)),
                pl.BlockSpec((tk, FOX_W), lambda n, it, jt: (jt[n], 1)),
                pl.BlockSpec((FOX_W, tk), lambda n, it, jt: (0, jt[n])),
                pl.BlockSpec((tk, LANES), lambda n, it, jt: (jt[n], 0)),
            ] + [_ANY] * n_ci,
            out_specs=[
                pl.BlockSpec((tq, FOX_W), lambda n, it, jt: (it[n], 0)),
                pl.BlockSpec((FOX_W, tq), lambda n, it, jt: (0, it[n])),
                pl.BlockSpec((N_HEADS, 1, tq), lambda n, it, jt: (0, 0, it[n])),
            ] + [_ANY] * n_co,
            scratch_shapes=[pltpu.VMEM((FOX_W, tq), F32), stat, stat, pltpu.VMEM((N_HEADS, tk, tq), F32),
                            pltpu.VMEM((N_HEADS, tk, tq), BF16)] + comm_sems,
        ),
        out_shape=[_sds((s_len, FOX_W), F32), _sds((FOX_W, s_len), BF16), _sds((N_HEADS, 1, s_len), F32)] + comm_outs,
        compiler_params=_params(("arbitrary",)),
    )(i_tab, j_tab, qkv, qkv, v_t, c_col, *comm_arrays)
    return res[0], res[1], res[2], res[3:]


def _sgu_mix(vn, w_stack, lt64):
    outs = []
    for p in range(SGU_G // 2):
        r = _dot(w_stack[p], vn[:, p * LANES:(p + 1) * LANES])
        outs.append(jnp.where(lt64, r[:SGU_LEN], r[SGU_LEN:]))
    return jnp.concatenate(outs, axis=1)


def _sgu_norm(sv, ln_g, ln_b):
    svg = _gelu(sv)
    xc = svg - jnp.mean(svg, axis=-1, keepdims=True)
    rstd = lax.rsqrt(jnp.mean(xc * xc, axis=-1, keepdims=True) + EPS)
    xhat = xc * rstd
    return xhat, rstd, xhat * ln_g + ln_b


def _sgu_fwd(prest, ln_g, ln_b, w_stack, b_pair, tm):
    s_len = prest.shape[0]

    def body(u_ref, sv_ref, g_ref, b_ref, w_ref, bp_ref, sg_ref, sgt_ref):
        lt64 = _lane_lt64((SGU_LEN, LANES))
        _, _, vn = _sgu_norm(sv_ref[...].astype(F32), g_ref[...], b_ref[...])
        vn = vn.astype(BF16)
        w_stack_v = [w_ref[p] for p in range(SGU_G // 2)]
        for w in range(tm // SGU_LEN):
            win = slice(w * SGU_LEN, (w + 1) * SGU_LEN)
            mixed = _sgu_mix(vn[win], w_stack_v, lt64) + bp_ref[...]
            sg = (_gelu(u_ref[win, :].astype(F32)) * mixed).astype(BF16)
            sg_ref[win, :] = sg
            sgt_ref[:, win] = jnp.transpose(sg)

    return pl.pallas_call(
        body,
        name="sgu_fwd",
        grid=(s_len // tm,),
        in_specs=[
            pl.BlockSpec((tm, SGU_W), lambda i: (i, U_OFF // SGU_W)),
            pl.BlockSpec((tm, SGU_W), lambda i: (i, SV_OFF // SGU_W)),
            pl.BlockSpec((1, SGU_W), lambda i: (0, 0)),
            pl.BlockSpec((1, SGU_W), lambda i: (0, 0)),
            pl.BlockSpec((SGU_G // 2, 2 * SGU_LEN, SGU_LEN), lambda i: (0, 0, 0)),
            pl.BlockSpec((SGU_LEN, SGU_W), lambda i: (0, 0)),
        ],
        out_specs=[pl.BlockSpec((tm, SGU_W), lambda i: (i, 0)), pl.BlockSpec((SGU_W, tm), lambda i: (0, i))],
        out_shape=[_sds((s_len, SGU_W), BF16), _sds((SGU_W, s_len), BF16)],
        compiler_params=_params(("parallel",)),
    )(prest, prest, ln_g, ln_b, w_stack, b_pair)


def _sgu_bwd(prest, dsg, ln_g, ln_b, w_stack, wt_stack, b_pair, tm):
    s_len = prest.shape[0]
    n_pair = SGU_G // 2

    def body(u_ref, sv_ref, dsg_ref, g_ref, b_ref, w_ref, wt_ref, bp_ref,
             du_ref, dsv_ref, dw_ref, db_ref, dg_ref, dbeta_ref, dut_ref, dsvt_ref, dvn_sc):
        @pl.when(pl.program_id(0) == 0)
        def _():
            dw_ref[...] = jnp.zeros_like(dw_ref)
            db_ref[...] = jnp.zeros_like(db_ref)
            dg_ref[...] = jnp.zeros_like(dg_ref)
            dbeta_ref[...] = jnp.zeros_like(dbeta_ref)

        lt64 = _lane_lt64((SGU_LEN, LANES))
        sv = sv_ref[...].astype(F32)
        xhat, rstd, vn32 = _sgu_norm(sv, g_ref[...], b_ref[...])
        vn = vn32.astype(BF16)
        w_stack_v = [w_ref[p] for p in range(n_pair)]
        db = jnp.zeros((SGU_LEN, SGU_W), F32)
        for w in range(tm // SGU_LEN):
            win = slice(w * SGU_LEN, (w + 1) * SGU_LEN)
            u = u_ref[win, :].astype(F32)
            dsg_w = dsg_ref[win, :]
            mixed = _sgu_mix(vn[win], w_stack_v, lt64) + bp_ref[...]
            du = (dsg_w * mixed * _gelu_grad(u)).astype(BF16)
            du_ref[win, :] = du
            dut_ref[:, win] = jnp.transpose(du)
            dmixed = dsg_w * _gelu(u)
            db = db + dmixed
            dm16 = dmixed.astype(BF16)
            for p in range(n_pair):
                lanes = slice(p * LANES, (p + 1) * LANES)
                dmp = dm16[:, lanes]
                r = _dot(wt_ref[p], dmp)
                dvn_sc[win, lanes] = jnp.where(lt64, r[:SGU_LEN], r[SGU_LEN:])
                zero = jnp.zeros_like(dmp)
                dm_ab = jnp.concatenate([jnp.where(lt64, dmp, zero), jnp.where(lt64, zero, dmp)], axis=0)
                dw_ref[p] += _dot_nt(dm_ab, vn[win, lanes])
        db_ref[...] += db
        dvn = dvn_sc[...]
        dg_ref[...] += jnp.sum(dvn * xhat, axis=0, keepdims=True)
        dbeta_ref[...] += jnp.sum(dvn, axis=0, keepdims=True)
        dxh = dvn * g_ref[...]
        dsvg = rstd * (dxh - jnp.mean(dxh, axis=-1, keepdims=True) - xhat * jnp.mean(dxh * xhat, axis=-1, keepdims=True))
        dsv = (dsvg * _gelu_grad(sv)).astype(BF16)
        dsv_ref[...] = dsv
        dsvt_ref[...] = jnp.transpose(dsv)

    const2 = lambda i: (0, 0)
    const3 = lambda i: (0, 0, 0)
    return pl.pallas_call(
        body,
        name="sgu_bwd",
        grid=(s_len // tm,),
        in_specs=[
            pl.BlockSpec((tm, SGU_W), lambda i: (i, U_OFF // SGU_W)),
            pl.BlockSpec((tm, SGU_W), lambda i: (i, SV_OFF // SGU_W)),
            pl.BlockSpec((tm, SGU_W), lambda i: (i, 0)),
            pl.BlockSpec((1, SGU_W), const2),
            pl.BlockSpec((1, SGU_W), const2),
            pl.BlockSpec((n_pair, 2 * SGU_LEN, SGU_LEN), const3),
            pl.BlockSpec((n_pair, 2 * SGU_LEN, SGU_LEN), const3),
            pl.BlockSpec((SGU_LEN, SGU_W), const2),
        ],
        out_specs=[
            pl.BlockSpec((tm, SGU_W), lambda i: (i, 0)),
            pl.BlockSpec((tm, SGU_W), lambda i: (i, 0)),
            pl.BlockSpec((n_pair, 2 * SGU_LEN, SGU_LEN), const3),
            pl.BlockSpec((SGU_LEN, SGU_W), const2),
            pl.BlockSpec((1, SGU_W), const2),
            pl.BlockSpec((1, SGU_W), const2),
            pl.BlockSpec((SGU_W, tm), lambda i: (0, i)),
            pl.BlockSpec((SGU_W, tm), lambda i: (0, i)),
        ],
        out_shape=[
            _sds((s_len, SGU_W), BF16), _sds((s_len, SGU_W), BF16), _sds((n_pair, 2 * SGU_LEN, SGU_LEN), F32),
            _sds((SGU_LEN, SGU_W), F32), _sds((1, SGU_W), F32), _sds((1, SGU_W), F32),
            _sds((SGU_W, s_len), BF16), _sds((SGU_W, s_len), BF16),
        ],
        scratch_shapes=[pltpu.VMEM((tm, SGU_W), F32)],
        compiler_params=_params(("arbitrary",)),
    )(prest, prest, dsg, ln_g, ln_b, w_stack, wt_stack, b_pair)


def _attn_bwd(qkv, k_t, do, c_col, lse_row, delta_row, tq, tk, comm=()):
    s_len = qkv.shape[0]
    nq, nk = s_len // tq, s_len // tk
    ratio = tq // tk
    scale = HEAD_DIM ** -0.5
    steps = [(j, i) for j in range(nk) for i in range(j // ratio, nq)]
    j_tab = jnp.asarray([j for j, _ in steps], jnp.int32)
    i_tab = jnp.asarray([i for _, i in steps], jnp.int32)

    comm_arrays, comm_outs, comm_sems = _comm_operands(comm)
    n_ci, n_co = len(comm_arrays), len(comm_outs)

    def body(j_ref, i_ref, q_ref, k_ref, v_ref, kt_ref, do_ref, ck_ref, lse_ref, dl_ref, *rest):
        comm_refs = rest[:n_ci], rest[n_ci + 7:n_ci + 7 + n_co], rest[n_ci + 7 + n_co + 8:]
        dq_ref, dk_ref, dv_ref, dcr_ref, dcc_ref, dkt_ref, dvt_ref = rest[n_ci:n_ci + 7]
        dq_t, dk_acc, dv_acc, dcc_acc, st_sc, dpt_sc, p_sc, ds_sc = rest[n_ci + 7 + n_co:n_ci + 7 + n_co + 8]
        n = pl.program_id(0)
        j, i = j_ref[n], i_ref[n]

        @pl.when(n == 0)
        def _():
            _comm_phase(comm, "start", *comm_refs)
            dq_t[...] = jnp.zeros_like(dq_t)
            dcr_ref[...] = jnp.zeros_like(dcr_ref)

        @pl.when(i == j // ratio)
        def _():
            dk_acc[...] = jnp.zeros_like(dk_acc)
            dv_acc[...] = jnp.zeros_like(dv_acc)
            dcc_acc[...] = jnp.zeros_like(dcc_acc)

        lt64 = _lane_lt64((tk, LANES))
        _score_tiles(q_ref, k_ref, ck_ref, st_sc, tk)
        for p in range(N_HEADS // 2):
            lanes = slice(p * LANES, (p + 1) * LANES)
            dpt2 = _dot_nt(_stack_heads(v_ref[:, lanes], lt64), do_ref[:, lanes].astype(BF16))
            dpt_sc[2 * p] = dpt2[:tk]
            dpt_sc[2 * p + 1] = dpt2[tk:]
        _mask_diagonal(st_sc, i, j, tq, tk)

        pt = jnp.exp(st_sc[...] - lse_ref[...])
        dst = pt * (dpt_sc[...] - dl_ref[...])
        p_sc[...] = pt.astype(BF16)
        ds_sc[...] = dst.astype(BF16)
        dcr_ref[i] += jnp.sum(dst, axis=1, keepdims=True)
        col_sums = jnp.sum(dst, axis=2, keepdims=True)
        lane = lax.broadcasted_iota(jnp.int32, (tk, LANES), 1)
        dcc = jnp.zeros((tk, LANES), F32)
        for h in range(N_HEADS):
            dcc = jnp.where(lane == h, -col_sums[h], dcc)
        dcc_acc[...] += dcc

        for p in range(N_HEADS // 2):
            lanes = slice(p * LANES, (p + 1) * LANES)
            q_pair = q_ref[:, lanes] * jnp.asarray(scale, BF16)
            dv2 = _dot(p_sc[2 * p:2 * p + 2].reshape(2 * tk, tq), do_ref[:, lanes].astype(BF16))
            dv_acc[:, lanes] += jnp.where(lt64, dv2[:tk], dv2[tk:])
            dk2 = _dot(ds_sc[2 * p:2 * p + 2].reshape(2 * tk, tq), q_pair)
            dk_acc[:, lanes] += jnp.where(lt64, dk2[:tk], dk2[tk:])
            dq2 = _dot(kt_ref[lanes, :], jnp.concatenate([ds_sc[2 * p], ds_sc[2 * p + 1]], axis=1))
            top = lax.broadcasted_iota(jnp.int32, (LANES, tq), 0) < HEAD_DIM
            dq_t[i, lanes, :] += jnp.where(top, dq2[:, :tq], dq2[:, tq:])

        @pl.when(j == (i + 1) * ratio - 1)
        def _():
            rows = pl.ds(pl.multiple_of(i * tq, tq), tq)
            for p in range(N_HEADS // 2):
                lanes = slice(p * LANES, (p + 1) * LANES)
                dq_ref[rows, lanes] = (jnp.transpose(dq_t[i, lanes, :]) * scale).astype(BF16)

        @pl.when(i == nq - 1)
        def _():
            dk16, dv16 = dk_acc[...].astype(BF16), dv_acc[...].astype(BF16)
            dk_ref[...] = dk16
            dv_ref[...] = dv16
            dkt_ref[...] = jnp.transpose(dk16)
            dvt_ref[...] = jnp.transpose(dv16)
            dcc_ref[...] = dcc_acc[...]

        if comm:
            @pl.when(n == len(steps) // 2)
            def _():
                _comm_phase(comm, "mid", *comm_refs)

            @pl.when(n == len(steps) - 1)
            def _():
                _comm_phase(comm, "finish", *comm_refs)

    q_map = lambda n, jt, it: (it[n], 0)
    q_stat = lambda n, jt, it: (0, 0, it[n])
    k_map = lambda n, jt, it: (jt[n], 0)
    tile = (N_HEADS, tk, tq)
    res = pl.pallas_call(
        body,
        name="attn_bwd",
        grid_spec=pltpu.PrefetchScalarGridSpec(
            num_scalar_prefetch=2,
            grid=(len(steps),),
            in_specs=[
                pl.BlockSpec((tq, FOX_W), q_map),
                pl.BlockSpec((tk, FOX_W), lambda n, jt, it: (jt[n], 1)),
                pl.BlockSpec((tk, FOX_W), lambda n, jt, it: (jt[n], 2)),
                pl.BlockSpec((FOX_W, tk), lambda n, jt, it: (0, jt[n])),
                pl.BlockSpec((tq, FOX_W), q_map),
                pl.BlockSpec((tk, LANES), k_map),
                pl.BlockSpec((N_HEADS, 1, tq), q_stat),
                pl.BlockSpec((N_HEADS, 1, tq), q_stat),
            ] + [_ANY] * n_ci,
            out_specs=[
                pl.BlockSpec((s_len, FOX_W), lambda n, jt, it: (0, 0)),
                pl.BlockSpec((tk, FOX_W), k_map),
                pl.BlockSpec((tk, FOX_W), k_map),
                pl.BlockSpec((nq, N_HEADS, 1, tq), lambda n, jt, it: (0, 0, 0, 0)),
                pl.BlockSpec((tk, LANES), k_map),
                pl.BlockSpec((FOX_W, tk), lambda n, jt, it: (0, jt[n])),
                pl.BlockSpec((FOX_W, tk), lambda n, jt, it: (0, jt[n])),
            ] + [_ANY] * n_co,
            scratch_shapes=[pltpu.VMEM((nq, FOX_W, tq), F32), pltpu.VMEM((tk, FOX_W), F32), pltpu.VMEM((tk, FOX_W), F32),
                            pltpu.VMEM((tk, LANES), F32), pltpu.VMEM(tile, F32), pltpu.VMEM(tile, F32),
                            pltpu.VMEM(tile, BF16), pltpu.VMEM(tile, BF16)] + comm_sems,
        ),
        out_shape=[_sds((s_len, FOX_W), BF16), _sds((s_len, FOX_W), BF16), _sds((s_len, FOX_W), BF16),
                   _sds((nq, N_HEADS, 1, tq), F32), _sds((s_len, LANES), F32),
                   _sds((FOX_W, s_len), BF16), _sds((FOX_W, s_len), BF16)] + comm_outs,
        compiler_params=_params(("arbitrary",)),
    )(j_tab, i_tab, qkv, qkv, qkv, k_t, do, c_col, lse_row, delta_row, *comm_arrays)
    return res[:7], res[7:]


def _forget_bwd(dc_rows, dc_cols, prest, b_f_pad, tc):
    s_len = dc_rows.shape[0]
    nb = s_len // tc

    def body(dcr_ref, dc_ref, f_ref, b_ref, df_ref, db_ref, dft_ref, carry):
        @pl.when(pl.program_id(0) == 0)
        def _():
            carry[...] = jnp.zeros_like(carry)
            db_ref[...] = jnp.zeros_like(db_ref)

        row = lax.broadcasted_iota(jnp.int32, (tc, tc), 0)
        col = lax.broadcasted_iota(jnp.int32, (tc, tc), 1)
        tri = (row <= col).astype(F32)
        dlogf = _dot_f32(tri, dcr_ref[...] + dc_ref[...]) + carry[...]
        carry[...] = dlogf[0:1, :]
        z = f_ref[...] + b_ref[...]
        lane = lax.broadcasted_iota(jnp.int32, (tc, LANES), 1)
        dz = jnp.where(lane < N_HEADS, dlogf * _sigmoid(-z), 0.0)
        df_ref[...] = dz.astype(BF16)
        dft_ref[...] = jnp.transpose(dz).astype(BF16)
        db_ref[...] += jnp.sum(dz, axis=0, keepdims=True)

    rev = lambda i: (nb - 1 - i, 0)
    return pl.pallas_call(
        body,
        name="forget_bwd",
        grid=(nb,),
        in_specs=[
            pl.BlockSpec((tc, LANES), rev),
            pl.BlockSpec((tc, LANES), rev),
            pl.BlockSpec((tc, LANES), rev),
            pl.BlockSpec((1, LANES), lambda i: (0, 0)),
        ],
        out_specs=[pl.BlockSpec((tc, LANES), rev), pl.BlockSpec((1, LANES), lambda i: (0, 0)),
                   pl.BlockSpec((LANES, tc), lambda i: (0, nb - 1 - i))],
        out_shape=[_sds((s_len, LANES), BF16), _sds((1, LANES), F32), _sds((LANES, s_len), BF16)],
        scratch_shapes=[pltpu.VMEM((1, LANES), F32)],
        compiler_params=_params(("arbitrary",)),
    )(dc_rows, dc_cols, prest, b_f_pad)


def _pair_sum(name, g4, recv, idx, tr):
    _, _, r, c = g4.shape

    def body(idx_ref, g_ref, r_ref, p16_ref, own_ref):
        k = pl.program_id(1)
        s = g_ref[...] + r_ref[...]
        p16_ref[...] = s.astype(BF16)

        @pl.when(k == idx_ref[1])
        def _():
            own_ref[...] = s

    return pl.pallas_call(
        body,
        name=name,
        grid_spec=pltpu.PrefetchScalarGridSpec(
            num_scalar_prefetch=1,
            grid=(r // tr, 4),
            in_specs=[
                pl.BlockSpec((None, None, tr, c), lambda i, k, idx: (k, idx[0], i, 0)),
                pl.BlockSpec((None, tr, c), lambda i, k, idx: (k, i, 0)),
            ],
            out_specs=[
                pl.BlockSpec((None, tr, c), lambda i, k, idx: (k, i, 0)),
                pl.BlockSpec((tr, c), lambda i, k, idx: (i, 0)),
            ],
        ),
        out_shape=[_sds((4, r, c), BF16), _sds((r, c), F32)],
        compiler_params=_params(("parallel", "arbitrary")),
    )(idx, g4, recv)


def _adamw_math(w, g, m, v):
    m2 = ADAM_B1 * m + (1.0 - ADAM_B1) * g
    v2 = ADAM_B2 * v + (1.0 - ADAM_B2) * (g * g)
    m_hat = m2 / (1.0 - ADAM_B1 ** ADAM_STEP)
    v_hat = v2 / (1.0 - ADAM_B2 ** ADAM_STEP)
    delta = -ADAM_LR * (m_hat / (jnp.sqrt(v_hat) + ADAM_EPS) + ADAM_WD * w)
    return delta, m2, v2


def _adamw_shard(name, own, recv, w, m, v, tr):
    r, c = own.shape

    def body(own_ref, recv_ref, w_ref, m_ref, v_ref, g_ref, d_ref, m2_ref, v2_ref):
        g = own_ref[...]
        for k in range(3):
            g = g + recv_ref[k].astype(F32)
        delta, m2, v2 = _adamw_math(w_ref[...], g, m_ref[...], v_ref[...])
        g_ref[...] = g
        d_ref[...] = delta
        m2_ref[...] = m2
        v2_ref[...] = v2

    spec = pl.BlockSpec((tr, c), lambda i: (i, 0))
    return pl.pallas_call(
        body,
        name=name,
        grid=(r // tr,),
        in_specs=[spec, pl.BlockSpec((3, tr, c), lambda i: (0, i, 0)), spec, spec, spec],
        out_specs=[spec] * 4,
        out_shape=[_sds((r, c), F32)] * 4,
        compiler_params=_params(("parallel",)),
    )(own, recv, w, m, v)


def _adamw_small(gathered, w, m, v):
    _, r, _ = gathered.shape

    def body(ga_ref, w_ref, m_ref, v_ref, g_ref, d_ref, m2_ref, v2_ref):
        g = ga_ref[0]
        for k in range(1, N_DEV):
            g = g + ga_ref[k]
        delta, m2, v2 = _adamw_math(w_ref[...], g, m_ref[...], v_ref[...])
        g_ref[...] = g
        d_ref[...] = delta
        m2_ref[...] = m2
        v2_ref[...] = v2

    spec = pl.BlockSpec((r, LANES), lambda i: (0, 0))
    return pl.pallas_call(
        body,
        name="adamw_small",
        grid=(1,),
        in_specs=[pl.BlockSpec((N_DEV, r, LANES), lambda i: (0, 0, 0)), spec, spec, spec],
        out_specs=[spec] * 4,
        out_shape=[_sds((r, LANES), F32)] * 4,
        compiler_params=_params(("arbitrary",)),
    )(gathered, w, m, v)


_SMALL = (("w_sgu", (1, SGU_G, SGU_LEN, SGU_LEN)), ("b_sgu", (1, SGU_G, SGU_LEN)), ("norm2_g", (1, D_MODEL)),
          ("normf_g", (D_MODEL,)), ("ln_v_g", (1, SGU_W)), ("ln_v_b", (1, SGU_W)), ("b_f", (1, N_HEADS)),
          ("norm1_g", (1, D_MODEL)), ("loss", ()))
_N_EARLY = 6


def _pack_rows(values):
    rows = []
    for val in values:
        flat = val.reshape(-1).astype(F32)
        pad = (-flat.shape[0]) % LANES
        rows.append(jnp.pad(flat, (0, pad)).reshape(-1, LANES))
    packed = jnp.concatenate(rows, axis=0)
    return jnp.pad(packed, ((0, (-packed.shape[0]) % 8), (0, 0)))


def _pack_small(values):
    return jnp.concatenate([_pack_rows(values[:_N_EARLY]), _pack_rows(values[_N_EARLY:])], axis=0)


def _unpack_small(packed):
    out, row = [], 0
    for k, (_, shape) in enumerate(_SMALL):
        if k == _N_EARLY:
            row += (-row) % 8
        size = math.prod(shape)
        n_rows = -(-size // LANES)
        out.append(packed[row:row + n_rows].reshape(-1)[:size].reshape(shape))
        row += n_rows
    return out


def kernel(x, norm1_g, w_in, b_f, ln_v_g, ln_v_b, w_sgu, b_sgu, w_a, w_b, w_o, norm2_g, w_up, w_down, normf_g, loss_target, m_norm1_g, m_w_in, m_b_f, m_ln_v_g, m_ln_v_b, m_w_sgu, m_b_sgu, m_w_a, m_w_b, m_w_o, m_norm2_g, m_w_up, m_w_down, m_normf_g, v_norm1_g, v_w_in, v_b_f, v_ln_v_g, v_ln_v_b, v_w_sgu, v_b_sgu, v_w_a, v_w_b, v_w_o, v_norm2_g, v_w_up, v_w_down, v_normf_g):
    xs = x[0]
    target = loss_target[0]
    s_len, d = xs.shape
    tm = min(512, s_len)
    tl = min(1024, s_len)
    tr = min(256, s_len)
    ta = min(512, s_len)
    tc = min(512, s_len)

    w_in_t = jnp.transpose(w_in[0])
    big = (w_in_t, w_a[0], w_b[0], w_o[0], w_up[0], w_down[0])
    h, h_t, r1, (w_in_g,) = _rms_fwd("rms1", xs, norm1_g, tm, comm=[_gather_plan(w_in_t.astype(BF16))])
    w_in_f = w_in_g.reshape(IN_COLS, d)
    later = big[1:]
    later_flat = jnp.concatenate([w.reshape(-1).astype(BF16) for w in later]).reshape(-1, D_MODEL)
    later_plan = _gather_plan(later_flat)

    def unflatten(gathered):
        row, full = 0, []
        for w, col_sharded in zip(later, (True, True, False, True, False)):
            n_rows = w.size // D_MODEL
            blk = gathered[:, row:row + n_rows].reshape((N_DEV,) + w.shape)
            row += n_rows
            if col_sharded:
                full.append(jnp.transpose(blk, (1, 0, 2)).reshape(w.shape[0], N_DEV * w.shape[1]))
            else:
                full.append(blk.reshape(N_DEV * w.shape[0], w.shape[1]))
        return full

    w_qkv = w_in_f[:QKV_W]
    f_lo = QKV_W
    u_lo = f_lo + N_HEADS
    w_rest = jnp.concatenate([w_in_f[u_lo:], jnp.pad(w_in_f[f_lo:u_lo], ((0, LANES - N_HEADS), (0, 0)))], axis=0)

    chunk_id = jnp.arange(SGU_LEN) // CHUNK
    sgu_mask = chunk_id[None, :] <= chunk_id[:, None]
    w_masked = jnp.where(sgu_mask[None], w_sgu[0], 0.0)
    w_stack = w_masked.reshape(SGU_G // 2, 2 * SGU_LEN, SGU_LEN).astype(BF16)
    wt_stack = jnp.transpose(w_masked, (0, 2, 1)).reshape(SGU_G // 2, 2 * SGU_LEN, SGU_LEN).astype(BF16)
    b_pair = jnp.transpose(jnp.repeat(b_sgu[0], SGU_W // SGU_G, axis=0))
    b_f_pad = jnp.pad(b_f, ((0, 0), (0, LANES - N_HEADS)))
    head_sel = (jnp.arange(FOX_W)[:, None] // HEAD_DIM == jnp.arange(LANES)[None, :]).astype(F32)

    def store(dtype):
        def epi(accs, ex, out):
            out[0][...] = accs[0].astype(dtype)
        return epi

    def qkv_epi(accs, ex, out):
        tile = accs[0].astype(BF16)
        out[0][...] = tile
        for col, ref in ((1, out[1]), (2, out[2])):
            @pl.when(pl.program_id(1) == col)
            def _():
                ref[...] = jnp.transpose(tile)

    t_spec = pl.BlockSpec((FOX_W, tl), lambda i, j: (0, i))
    qkv, k_t, v_t = _mm("proj_qkv", [(h, w_qkv, True, None)], [],
                        [(_sds((s_len, QKV_W), BF16), _tile(tl, FOX_W)), (_sds((FOX_W, s_len), BF16), t_spec),
                         (_sds((FOX_W, s_len), BF16), t_spec)],
                        qkv_epi, m=s_len, tm=tl, n=QKV_W, tn=FOX_W, arbitrary=True)
    rest_tn = 640
    f_tile, f_lane = F_OFF // rest_tn, F_OFF % rest_tn

    def rest_epi(accs, ex, out):
        out[0][...] = accs[0].astype(BF16)

        @pl.when(pl.program_id(1) == f_tile)
        def _():
            out[1][...] = accs[0][:, f_lane:f_lane + LANES]

    prest, f_logit = _mm("proj_rest", [(h, w_rest, True, None)], [],
                         [(_sds((s_len, REST_W), BF16), _tile(tl, rest_tn)), (_sds((s_len, LANES), F32), _row(tl, LANES))],
                         rest_epi, m=s_len, tm=tl, n=REST_W, tn=rest_tn, arbitrary=True)

    c_col = _forget_cumsum(f_logit, b_f_pad, tc)
    o, o_t, lse_row, (later_g,) = _attn_fwd(qkv, v_t, c_col, ta, ta, comm=[later_plan])
    w_a_f, w_b_f, w_o_f, w_up_f, w_down_f = unflatten(later_g)
    sg, sg_t = _sgu_fwd(prest, ln_v_g, ln_v_b, w_stack, b_pair, tm)

    def merge_epi(accs, ex, out):
        ya, yb = accs
        sa, sb = _sigmoid(ex[0][...].astype(F32)), _sigmoid(ex[1][...].astype(F32))
        merged = (sa * ya + sb * yb).astype(BF16)
        out[0][...] = merged
        out[1][...] = ya.astype(BF16)
        out[2][...] = yb.astype(BF16)
        out[3][...] = jnp.transpose(merged)

    merged, ya, yb, merged_t = _mm(
        "merge", [(o, w_a_f, False, None), (sg, w_b_f, False, None)],
        [(prest, _tile(tm, d, GA_OFF // d)), (prest, _tile(tm, d, GB_OFF // d))],
        [(_sds((s_len, d), BF16), _tile(tm, d))] * 3 + [(_sds((d, s_len), BF16), _tile_t(tm, d))],
        merge_epi, m=s_len, tm=tm, n=d, tn=d)

    def resid_epi(accs, ex, out):
        x1v = ex[0][...] + accs[0]
        out[0][...] = x1v
        r = lax.rsqrt(jnp.mean(x1v * x1v, axis=-1, keepdims=True) + EPS)
        h2v = (x1v * r * ex[1][...]).astype(BF16)
        out[1][...] = h2v
        out[2][...] = jnp.transpose(h2v)
        out[3][...] = r

    x1, h2, h2_t, r2 = _mm(
        "out_proj", [(merged, w_o_f, False, None)], [(xs, _tile(tm, d)), (norm2_g, _whole((1, d)))],
        [(_sds((s_len, d), F32), _tile(tm, d)), (_sds((s_len, d), BF16), _tile(tm, d)),
         (_sds((d, s_len), BF16), _tile_t(tm, d)), (_sds((s_len, 1), F32), _row(tm, 1))],
        resid_epi, m=s_len, tm=tm, n=d, tn=d)

    def up_epi(accs, ex, out):
        act = jnp.square(jnp.maximum(accs[0], 0.0)).astype(BF16)
        out[0][...] = act
        out[1][...] = jnp.transpose(act)

    act, act_t = _mm(
        "mlp_up", [(h2, w_up_f, False, None)], [],
        [(_sds((s_len, D_FF), BF16), _tile(tl, 512)), (_sds((D_FF, s_len), BF16), _tile_t(tl, 512))],
        up_epi, m=s_len, tm=tl, n=D_FF, tn=512)

    def first_step():
        return jnp.logical_and(pl.program_id(0) == 0, pl.program_id(1) == 0)

    def accumulate(ref, val):
        @pl.when(first_step())
        def _():
            ref[...] = val

        @pl.when(jnp.logical_not(first_step()))
        def _():
            ref[...] += val

    def final_epi(accs, ex, out):
        x1_ref, t_ref, g_ref = ex
        x2 = x1_ref[...] + accs[0]
        rf = lax.rsqrt(jnp.mean(x2 * x2, axis=-1, keepdims=True) + EPS)
        xh = x2 * rf
        gf = g_ref[...]
        err = xh * gf - t_ref[...]
        dy = err * (1.0 / d)
        dx2 = _rms_bwd(xh, rf, gf, dy)
        out[0][...] = dx2
        accumulate(out[1], jnp.sum(dy * xh, axis=0, keepdims=True))
        part = 0.5 * jnp.sum(jnp.sum(err * err, axis=-1, keepdims=True) * (1.0 / d), axis=0, keepdims=True)
        accumulate(out[2], jnp.broadcast_to(part, (1, LANES)))
        out[3][...] = dx2.astype(BF16)

    gf2 = normf_g.reshape(1, d)
    dx2, g_normf, loss_part, dx2_16 = _mm(
        "mlp_down_loss", [(act, w_down_f, False, None)],
        [(x1, _row(tr, d)), (target, _row(tr, d)), (gf2, _whole((1, d)))],
        [(_sds((s_len, d), F32), _row(tr, d)), (_sds((1, d), F32), _whole((1, d))), (_sds((1, LANES), F32), _whole((1, LANES))),
         (_sds((s_len, d), BF16), _row(tr, d))],
        final_epi, m=s_len, tm=tr, n=d, tn=d, arbitrary=True)

    def dact_epi(accs, ex, out):
        out[0][...] = (accs[0] * (2.0 * jnp.sqrt(ex[0][...].astype(F32)))).astype(BF16)

    (da,) = _mm("mlp_down_bwd", [(dx2_16, w_down_f, True, None)], [(act, _tile(tl, 512))],
                [(_sds((s_len, D_FF), BF16), _tile(tl, 512))], dact_epi, m=s_len, tm=tl, n=D_FF, tn=512)
    g_down = _grad_w("grad_w_down", act_t, dx2_16, tk=1024, tn=d, ts=tl)
    g_up = _grad_w("grad_w_up", h2_t, da, tk=d, tn=1024, ts=tl, block_cols=D_FF // N_DEV)

    def dh2_epi(accs, ex, out):
        x1_ref, r_ref, g_ref, dx2_ref = ex
        r = r_ref[...]
        xh = x1_ref[...] * r
        dh2 = accs[0]
        out[0][...] = dx2_ref[...] + _rms_bwd(xh, r, g_ref[...], dh2)
        accumulate(out[1], jnp.sum(dh2 * xh, axis=0, keepdims=True))

    my_c = lax.axis_index("c")
    my_chip = 2 * lax.axis_index("x") + lax.axis_index("y")
    idx = jnp.stack([my_c, my_chip]).astype(jnp.int32)
    parts16, owns = {}, {}

    def split_cores(g8):
        return g8.reshape((4, 2) + g8.shape[1:])

    def row_tile(r):
        return 512 if r % 512 == 0 else r

    def pair_sums(names, grads4, from_sibling):
        for name, g4, recv in zip(names, grads4, from_sibling):
            parts16[name], owns[name] = _pair_sum("grad_pair_sum_" + name, g4, recv, idx, row_tile(g4.shape[2]))

    grads4_mlp = [split_cores(g_up), split_cores(g_down.reshape(N_DEV, D_FF // N_DEV, d))]
    (dx1, g_norm2), from_sibling = _mm(
        "mlp_up_bwd", [(da, w_up_f, True, None)],
        [(x1, _row(tr, d)), (r2, _row(tr, 1)), (norm2_g, _whole((1, d))), (dx2, _row(tr, d))],
        [(_sds((s_len, d), F32), _row(tr, d)), (_sds((1, d), F32), _whole((1, d)))],
        dh2_epi, m=s_len, tm=tr, n=d, tn=d, arbitrary=True, comm=[_pair_exchange_plan(grads4_mlp)])
    pair_sums(("w_up", "w_down"), grads4_mlp, from_sibling)

    def dmerge_epi(accs, ex, out):
        dm = accs[0]
        sa, sb = _sigmoid(ex[0][...].astype(F32)), _sigmoid(ex[1][...].astype(F32))
        out[0][...] = (dm * sa).astype(BF16)
        out[1][...] = (dm * sb).astype(BF16)
        dga = (dm * ex[2][...] * sa * (1.0 - sa)).astype(BF16)
        dgb = (dm * ex[3][...] * sb * (1.0 - sb)).astype(BF16)
        out[2][...] = dga
        out[3][...] = dgb
        out[4][...] = jnp.transpose(dga)
        out[5][...] = jnp.transpose(dgb)

    dya, dyb, dga, dgb, dga_t, dgb_t = _mm(
        "out_proj_bwd", [(dx1, w_o_f, True, None)],
        [(prest, _tile(tm, d, GA_OFF // d)), (prest, _tile(tm, d, GB_OFF // d)), (ya, _tile(tm, d)), (yb, _tile(tm, d))],
        [(_sds((s_len, d), BF16), _tile(tm, d))] * 4 + [(_sds((d, s_len), BF16), _tile_t(tm, d))] * 2,
        dmerge_epi, m=s_len, tm=tm, n=d, tn=d)
    g_o = _grad_w("grad_w_o", merged_t, dx1, tk=d, tn=d, ts=tl).reshape(N_DEV, d // N_DEV, d)
    def col_blocks(g):
        return jnp.transpose(g.reshape(g.shape[0], N_DEV, g.shape[1] // N_DEV), (1, 0, 2))

    g_a = col_blocks(_grad_w("grad_w_a", o_t, dya, tk=FOX_W, tn=d, ts=tl))
    g_b = col_blocks(_grad_w("grad_w_b", sg_t, dyb, tk=SGU_W, tn=d, ts=tl))

    def do_epi(accs, ex, out):
        do = accs[0]
        out[0][...] = do
        out[1][...] = _dot_f32(do * ex[0][...], ex[1][...])

    grads4_mix = [split_cores(g) for g in (g_a, g_b, g_o)]
    (do, delta), from_sibling = _mm(
        "attn_out_bwd", [(dya, w_a_f, True, None)], [(o, _row(tm, FOX_W)), (head_sel, _whole((FOX_W, LANES)))],
        [(_sds((s_len, FOX_W), F32), _row(tm, FOX_W)), (_sds((s_len, LANES), F32), _row(tm, LANES))],
        do_epi, m=s_len, tm=tm, n=FOX_W, tn=FOX_W, comm=[_pair_exchange_plan(grads4_mix)])
    pair_sums(("w_a", "w_b", "w_o"), grads4_mix, from_sibling)
    (dsg,) = _mm("sgu_out_bwd", [(dyb, w_b_f, True, None)], [], [(_sds((s_len, SGU_W), F32), _tile(tm, SGU_W))],
                 store(F32), m=s_len, tm=tm, n=SGU_W, tn=SGU_W)

    du, dsv, dw_pairs, db_pos, g_ln_g, g_ln_b, du_t, dsv_t = _sgu_bwd(
        prest, dsg, ln_v_g, ln_v_b, w_stack, wt_stack, b_pair, tm)
    g_w_sgu = jnp.where(sgu_mask[None], dw_pairs.reshape(SGU_G, SGU_LEN, SGU_LEN), 0.0)
    g_b_sgu = jnp.transpose(jnp.sum(db_pos.reshape(SGU_LEN, SGU_G, SGU_W // SGU_G), axis=-1))

    delta_row = jnp.transpose(delta[:, :N_HEADS]).reshape(N_HEADS, 1, s_len)
    early = ("w_a", "w_b", "w_o", "w_up", "w_down")
    small_early = _pack_rows((g_w_sgu, g_b_sgu, g_norm2, g_normf, g_ln_g, g_ln_b))
    (dq, dk, dv, dc_rows_blk, dc_cols, dk_t, dv_t), (small_early_all, *from_chips_early) = _attn_bwd(
        qkv, k_t, do, c_col, lse_row, delta_row, ta, ta,
        comm=[_gather_plan(small_early), _chip_exchange_plan([parts16[n] for n in early])])
    dc_rows = jnp.transpose(dc_rows_blk.reshape(s_len // ta, N_HEADS, ta), (0, 2, 1)).reshape(s_len, N_HEADS)
    dc_rows = jnp.pad(dc_rows, ((0, 0), (0, LANES - N_HEADS)))
    dfl, g_bf, dfl_t = _forget_bwd(dc_rows, dc_cols, f_logit, b_f_pad, tc)

    dp_t = (jnp.transpose(dq), dk_t, dv_t, dfl_t, du_t, dsv_t, dga_t, dgb_t)
    g_in_rows = [_grad_w("grad_w_in_%d" % k, seg_t, h, tk=seg_t.shape[0], tn=d, ts=tl) for k, seg_t in enumerate(dp_t)]
    g_in_rows[3] = g_in_rows[3][:N_HEADS]
    g_in = jnp.concatenate(g_in_rows, axis=0).reshape(N_DEV, IN_SHARD, d)

    def dx_epi(accs, ex, out):
        x_ref, r_ref, g_ref, dx1_ref = ex
        dh = accs[0]
        for extra in accs[1:]:
            dh = dh + extra
        r = r_ref[...]
        xh = x_ref[...] * r
        out[0][...] = dx1_ref[...] + _rms_bwd(xh, r, g_ref[...], dh)
        accumulate(out[1], jnp.sum(dh * xh, axis=0, keepdims=True))

    rest_cols = ((du, U_OFF, 512), (dsv, SV_OFF, 512), (dga, GA_OFF, 1024), (dgb, GB_OFF, 1024), (dfl, F_OFF, LANES))
    dx_pairs = [(seg, w_qkv, False, (512 * k, 512 * (k + 1))) for k, seg in enumerate((dq, dk, dv))]
    dx_pairs += [(seg, w_rest, False, (lo, lo + width)) for seg, lo, width in rest_cols]
    grads4_in = [split_cores(g_in)]
    pair_sums(("w_in",), grads4_in, _run_comm("grad_pair_exchange_w_in", [_pair_exchange_plan(grads4_in)]))
    (grad_x, g_norm1), (from_chips_in,) = _mm(
        "proj_bwd", dx_pairs,
        [(xs, _row(tr, d)), (r1, _row(tr, 1)), (norm1_g, _whole((1, d))), (dx1, _row(tr, d))],
        [(_sds((s_len, d), F32), _row(tr, d)), (_sds((1, d), F32), _whole((1, d)))],
        dx_epi, m=s_len, tm=tr, n=d, tn=d, arbitrary=True, comm=[_chip_exchange_plan([parts16["w_in"]])])
    small_late = _pack_rows((g_bf[:, :N_HEADS], g_norm1, loss_part[0, 0]))
    (small_late_all,) = _run_comm("gather_last_grads", [_gather_plan(small_late)])
    small_all = jnp.concatenate([small_early_all, small_late_all], axis=1)
    from_chips = dict(zip(early, from_chips_early), w_in=from_chips_in)

    names = ("w_in", "w_a", "w_b", "w_o", "w_up", "w_down")
    moments_m = (m_w_in, m_w_a, m_w_b, m_w_o, m_w_up, m_w_down)
    moments_v = (v_w_in, v_w_a, v_w_b, v_w_o, v_w_up, v_w_down)
    big_out = {}
    for name, w, m, v in zip(names, big, moments_m, moments_v):
        own = owns[name]
        transposed = name == "w_in"
        m0, v0 = (jnp.transpose(m[0]), jnp.transpose(v[0])) if transposed else (m[0], v[0])
        res = _adamw_shard("adamw_" + name, own, from_chips[name], w, m0, v0, row_tile(own.shape[0]))
        big_out[name] = [(jnp.transpose(t) if transposed else t)[None] for t in res]

    zero = jnp.zeros((), F32)
    small_w = _pack_small((w_sgu, b_sgu, norm2_g, normf_g, ln_v_g, ln_v_b, b_f, norm1_g, zero))
    small_m = _pack_small((m_w_sgu, m_b_sgu, m_norm2_g, m_normf_g, m_ln_v_g, m_ln_v_b, m_b_f, m_norm1_g, zero))
    small_v = _pack_small((v_w_sgu, v_b_sgu, v_norm2_g, v_normf_g, v_ln_v_g, v_ln_v_b, v_b_f, v_norm1_g, zero))
    small_res = [_unpack_small(t) for t in _adamw_small(small_all, small_w, small_m, small_v)]
    small_names = [n for n, _ in _SMALL]
    small_out = {n: [res[k] for res in small_res] for k, n in enumerate(small_names)}
    loss = small_out["loss"][0]

    order = ("norm1_g", "w_in", "b_f", "ln_v_g", "ln_v_b", "w_sgu", "b_sgu", "w_a", "w_b", "w_o", "norm2_g", "w_up",
             "w_down", "normf_g")
    table = {**big_out, **small_out}
    outs = [loss, grad_x[None]]
    for kind in range(4):
        outs += [table[n][kind] for n in order]
    return tuple(outs)
```

```python
import math

import jax
import jax.numpy as jnp
from jax import lax
from jax.experimental import pallas as pl
from jax.experimental.pallas import tpu as pltpu

F32 = jnp.float32
BF16 = jnp.bfloat16

N_DEV = 8
D_MODEL = 1024
N_HEADS = 8
HEAD_DIM = 64
FOX_W = N_HEADS * HEAD_DIM
SGU_G = 8
SGU_W = 512
SGU_LEN = 128
CHUNK = 64
D_FF = 4 * D_MODEL
IN_COLS = 3 * FOX_W + N_HEADS + 2 * SGU_W + 2 * D_MODEL
IN_SHARD = IN_COLS // N_DEV
LANES = 128
QKV_W = 3 * FOX_W
U_OFF, SV_OFF, GA_OFF, GB_OFF, F_OFF = 0, 512, 1024, 2048, 3072
REST_W = F_OFF + LANES
EPS = 1e-6
NEG = -1e30

ADAM_LR = 0.001
ADAM_B1 = 0.9
ADAM_B2 = 0.999
ADAM_EPS = 1e-08
ADAM_WD = 0.01
ADAM_STEP = 10

VMEM_LIMIT = 56 * 1024 * 1024
MESH = pl.DeviceIdType.MESH


def _params(sem=None):
    return pltpu.CompilerParams(dimension_semantics=sem, vmem_limit_bytes=VMEM_LIMIT)


def _dot(a, b):
    return jnp.dot(a, b, preferred_element_type=F32)


def _dot_nt(a, b):
    return lax.dot_general(a, b, (((1,), (1,)), ((), ())), preferred_element_type=F32)


def _dot_f32(a, b):
    return jnp.dot(a, b, preferred_element_type=F32, precision=lax.Precision.HIGHEST)


def _sigmoid(x):
    return 1.0 / (1.0 + jnp.exp(-x))


def _log_sigmoid(z):
    return jnp.minimum(z, 0.0) - jnp.log(1.0 + jnp.exp(-jnp.abs(z)))


_GELU_K = math.sqrt(2.0 / math.pi)
_GELU_C = 0.044715


def _gelu(x):
    t = jnp.tanh(_GELU_K * (x + _GELU_C * (x * x * x)))
    return 0.5 * x * (1.0 + t)


def _gelu_grad(x):
    x2 = x * x
    t = jnp.tanh(_GELU_K * (x + _GELU_C * (x2 * x)))
    return 0.5 * (1.0 + t) + 0.5 * x * (1.0 - t * t) * (_GELU_K * (1.0 + 3.0 * _GELU_C * x2))


def _rms_bwd(xh, r, g, dy):
    gy = dy * g
    return r * (gy - xh * jnp.mean(xh * gy, axis=-1, keepdims=True))


def _lane_lt64(shape):
    return lax.broadcasted_iota(jnp.int32, shape, len(shape) - 1) < HEAD_DIM


class _Comm:
    def __init__(self, arrays, out_shapes, sems, start, finish, mid=None):
        self.arrays, self.out_shapes, self.sems = list(arrays), list(out_shapes), list(sems)
        self.start, self.mid, self.finish = start, mid, finish


def _comm_phase(plans, phase, in_refs, out_refs, sem_refs):
    ia = io = ks = 0
    for plan in plans:
        na, no, ns = len(plan.arrays), len(plan.out_shapes), len(plan.sems)
        fn = getattr(plan, phase)
        if fn is not None:
            fn(in_refs[ia:ia + na], out_refs[io:io + no], sem_refs[ks:ks + ns])
        ia, io, ks = ia + na, io + no, ks + ns


def _comm_operands(plans):
    arrays = [a for plan in plans for a in plan.arrays]
    out_shapes = [o for plan in plans for o in plan.out_shapes]
    sems = [s for plan in plans for s in plan.sems]
    return arrays, out_shapes, sems


_ANY = pl.BlockSpec(memory_space=pl.ANY)


def _run_comm(name, plans):
    arrays, out_shapes, sems = _comm_operands(plans)
    n_in, n_out = len(arrays), len(out_shapes)

    def body(*refs):
        parts = refs[:n_in], refs[n_in:n_in + n_out], refs[n_in + n_out:]
        for phase in ("start", "mid", "finish"):
            _comm_phase(plans, phase, *parts)

    return pl.pallas_call(
        body, name=name, out_shape=out_shapes, in_specs=[_ANY] * n_in, out_specs=[_ANY] * n_out, scratch_shapes=sems,
    )(*arrays)


def _gather_plan(shard):
    def setup(ins, outs, sems):
        (x_ref,), (out_ref,), (send_sems, recv_sems, local_sem) = ins, outs, sems
        x, y, c = lax.axis_index("x"), lax.axis_index("y"), lax.axis_index("c")
        me, sibling = (x, y, c), (x, y, 1 - c)
        chips = [(1 - x, y), (x, 1 - y), (1 - x, 1 - y)]

        def rows(px, py, pc):
            return out_ref.at[4 * px + 2 * py + pc]

        def copy(k, block, to, src=None):
            return pltpu.make_async_remote_copy(
                src_ref=rows(*block) if src is None else src,
                dst_ref=rows(*block),
                send_sem=send_sems.at[k],
                recv_sem=recv_sems.at[k],
                device_id=to,
                device_id_type=MESH,
            )

        mine = pltpu.make_async_copy(x_ref, rows(*me), local_sem)
        first = [copy(0, me, sibling, src=x_ref)]
        first += [copy(1 + j, me, (*chip, c), src=x_ref) for j, chip in enumerate(chips)]
        passed = [copy(4 + j, (*chip, c), sibling) for j, chip in enumerate(chips)]
        landed = [copy(1 + j, (*chip, c), me) for j, chip in enumerate(chips)]
        from_sibling = [copy(0, sibling, me)] + [copy(4 + j, (*chip, 1 - c), me) for j, chip in enumerate(chips)]
        return mine, first, passed, landed, from_sibling

    def start(ins, outs, sems):
        mine, first, _, _, _ = setup(ins, outs, sems)
        mine.start()
        for cp in first:
            cp.start()

    def mid(ins, outs, sems):
        _, _, passed, landed, _ = setup(ins, outs, sems)
        for arrived, onward in zip(landed, passed):
            arrived.wait_recv()
            onward.start()

    def finish(ins, outs, sems):
        mine, first, passed, _, from_sibling = setup(ins, outs, sems)
        for cp in from_sibling:
            cp.wait_recv()
        for cp in first + passed:
            cp.wait_send()
        mine.wait()

    return _Comm([shard], [jax.ShapeDtypeStruct((N_DEV,) + shard.shape, shard.dtype)],
                 [pltpu.SemaphoreType.DMA((7,)), pltpu.SemaphoreType.DMA((7,)), pltpu.SemaphoreType.DMA],
                 start, finish, mid)


def _start_all(copies):
    for cp in copies:
        cp.start()


def _wait_all(copies):
    for cp in copies:
        cp.wait_recv()
    for cp in copies:
        cp.wait_send()


def _pair_exchange_plan(grads):
    n = len(grads)

    def copies(ins, outs, sems):
        send_sems, recv_sems = sems
        x, y, c = lax.axis_index("x"), lax.axis_index("y"), lax.axis_index("c")
        return [
            pltpu.make_async_remote_copy(
                src_ref=ins[k].at[:, 1 - c],
                dst_ref=outs[k],
                send_sem=send_sems.at[k],
                recv_sem=recv_sems.at[k],
                device_id=(x, y, 1 - c),
                device_id_type=MESH,
            )
            for k in range(n)
        ]

    return _Comm(grads, [jax.ShapeDtypeStruct((4,) + g.shape[2:], g.dtype) for g in grads],
                 [pltpu.SemaphoreType.DMA((n,)), pltpu.SemaphoreType.DMA((n,))],
                 lambda *refs: _start_all(copies(*refs)), lambda *refs: _wait_all(copies(*refs)))


def _chip_exchange_plan(parts):
    n = len(parts)

    def copies(ins, outs, sems):
        send_sems, recv_sems = sems
        x, y, c = lax.axis_index("x"), lax.axis_index("y"), lax.axis_index("c")
        chips = [(1 - x, y), (x, 1 - y), (1 - x, 1 - y)]
        return [
            pltpu.make_async_remote_copy(
                src_ref=ins[k].at[2 * px + py],
                dst_ref=outs[k].at[j],
                send_sem=send_sems.at[3 * k + j],
                recv_sem=recv_sems.at[3 * k + j],
                device_id=(px, py, c),
                device_id_type=MESH,
            )
            for k in range(n) for j, (px, py) in enumerate(chips)
        ]

    return _Comm(parts, [jax.ShapeDtypeStruct((3,) + p.shape[1:], p.dtype) for p in parts],
                 [pltpu.SemaphoreType.DMA((3 * n,)), pltpu.SemaphoreType.DMA((3 * n,))],
                 lambda *refs: _start_all(copies(*refs)), lambda *refs: _wait_all(copies(*refs)))


def _mm(name, pairs, extras, outs, epi, *, m, tm, n, tn, arbitrary=False, comm=()):
    nj = n // tn
    a_arrays, a_specs, b_arrays, b_specs, b_index = [], [], [], [], []
    for a, b, nt, cols in pairs:
        a_arrays.append(a)
        a_specs.append(pl.BlockSpec((tm, a.shape[1]), lambda i, j: (i, 0)))
        known = [k for k, other in enumerate(b_arrays) if other is b]
        if known:
            b_index.append(known[0])
            continue
        b_index.append(len(b_arrays))
        b_arrays.append(b)
        if cols is not None:
            assert nj == 1
            b_specs.append(pl.BlockSpec(b.shape, lambda i, j: (0, 0)))
        elif nt:
            b_specs.append(pl.BlockSpec((tn, b.shape[1]), lambda i, j: (j, 0)))
        else:
            b_specs.append(pl.BlockSpec((b.shape[0], tn), lambda i, j: (0, j)))
    comm_arrays, comm_outs, comm_sems = _comm_operands(comm)
    arrays = a_arrays + b_arrays + [arr for arr, _ in extras] + comm_arrays
    in_specs = a_specs + b_specs + [spec for _, spec in extras] + [_ANY] * len(comm_arrays)
    n_a, n_b, n_extras, n_ci, n_out, n_co = len(a_arrays), len(b_arrays), len(extras), len(comm_arrays), len(outs), len(comm_outs)
    ni = m // tm

    def body(*refs):
        a_refs = refs[:n_a]
        b_refs = refs[n_a:n_a + n_b]
        ex = refs[n_a + n_b:n_a + n_b + n_extras]
        n_in = n_a + n_b + n_extras + n_ci
        comm_refs = refs[n_in - n_ci:n_in], refs[n_in + n_out:n_in + n_out + n_co], refs[n_in + n_out + n_co:]
        out = refs[n_in:n_in + n_out]
        if comm:
            @pl.when(jnp.logical_and(pl.program_id(0) == 0, pl.program_id(1) == 0))
            def _():
                _comm_phase(comm, "start", *comm_refs)

        accs = []
        for p, (_, _, nt, cols) in enumerate(pairs):
            av = a_refs[p][...]
            if av.dtype != BF16:
                av = av.astype(BF16)
            b_ref = b_refs[b_index[p]]
            if cols is None:
                bv = b_ref[...]
            else:
                bv = b_ref[:, cols[0]:cols[1]] if nt else b_ref[cols[0]:cols[1], :]
            accs.append(_dot_nt(av, bv) if nt else _dot(av, bv))
        epi(accs, ex, out)
        if comm:
            mid_row = ni // 2 if ni >= 3 else ni - 1
            mid_col = 0 if ni >= 3 else nj - 1

            @pl.when(jnp.logical_and(pl.program_id(0) == mid_row, pl.program_id(1) == mid_col))
            def _():
                _comm_phase(comm, "mid", *comm_refs)

            @pl.when(jnp.logical_and(pl.program_id(0) == ni - 1, pl.program_id(1) == nj - 1))
            def _():
                _comm_phase(comm, "finish", *comm_refs)

    sem = ("arbitrary", "arbitrary") if arbitrary or comm else ("parallel", "parallel")
    res = pl.pallas_call(
        body,
        name=name,
        grid=(ni, nj),
        in_specs=in_specs,
        out_specs=[spec for _, spec in outs] + [_ANY] * n_co,
        out_shape=[shape for shape, _ in outs] + comm_outs,
        scratch_shapes=comm_sems,
        compiler_params=_params(sem),
    )(*arrays)
    return (res[:n_out], res[n_out:]) if comm else res


def _tile(tm, tn, off=0):
    return pl.BlockSpec((tm, tn), lambda i, j: (i, j + off))


def _row(tm, w, blk=0):
    return pl.BlockSpec((tm, w), lambda i, j: (i, blk))


def _whole(shape):
    zeros = (0,) * len(shape)
    return pl.BlockSpec(shape, lambda i, j: zeros)


def _sds(shape, dtype):
    return jax.ShapeDtypeStruct(shape, dtype)


def _tile_t(tm, tn):
    return pl.BlockSpec((tn, tm), lambda i, j: (j, i))


def _grad_w(name, a_t, g, *, tk, tn, ts, block_cols=None):
    ka, s_len = a_t.shape
    n = g.shape[1]
    width = tn if block_cols is None else block_cols

    def body(a_ref, g_ref, o_ref):
        first = pl.program_id(2) == 0
        gv = g_ref[...].astype(BF16)
        for b in range(tn // width):
            part = _dot(a_ref[...], gv[:, b * width:(b + 1) * width])
            dst = o_ref if block_cols is None else o_ref.at[b]

            @pl.when(first)
            def _():
                dst[...] = part

            @pl.when(jnp.logical_not(first))
            def _():
                dst[...] += part

    if block_cols is None:
        out_shape = _sds((ka, n), F32)
        out_spec = pl.BlockSpec((tk, tn), lambda i, j, s: (i, j))
    else:
        out_shape = _sds((n // width, ka, width), F32)
        out_spec = pl.BlockSpec((tn // width, tk, width), lambda i, j, s: (j, i, 0))
    return pl.pallas_call(
        body,
        name=name,
        grid=(ka // tk, n // tn, s_len // ts),
        in_specs=[pl.BlockSpec((tk, ts), lambda i, j, s: (i, s)), pl.BlockSpec((ts, tn), lambda i, j, s: (s, j))],
        out_specs=out_spec,
        out_shape=out_shape,
        compiler_params=_params(("parallel", "parallel", "arbitrary")),
    )(a_t, g)


def _rms_fwd(name, x, g, tm, comm=()):
    s_len, d = x.shape
    steps = s_len // tm
    comm_arrays, comm_outs, comm_sems = _comm_operands(comm)
    n_ci, n_co = len(comm_arrays), len(comm_outs)

    def body(x_ref, g_ref, *rest):
        comm_refs = rest[:n_ci], rest[n_ci + 3:n_ci + 3 + n_co], rest[n_ci + 3 + n_co:]
        h_ref, ht_ref, r_ref = rest[n_ci:n_ci + 3]
        for phase, at in (("start", 0), ("mid", steps // 2)):
            if comm:
                @pl.when(pl.program_id(0) == at)
                def _():
                    _comm_phase(comm, phase, *comm_refs)

        xv = x_ref[...]
        r = lax.rsqrt(jnp.mean(xv * xv, axis=-1, keepdims=True) + EPS)
        h = (xv * r * g_ref[...]).astype(BF16)
        h_ref[...] = h
        ht_ref[...] = jnp.transpose(h)
        r_ref[...] = r
        if comm:
            @pl.when(pl.program_id(0) == steps - 1)
            def _():
                _comm_phase(comm, "finish", *comm_refs)

    res = pl.pallas_call(
        body,
        name=name,
        grid=(steps,),
        in_specs=[pl.BlockSpec((tm, d), lambda i: (i, 0)), pl.BlockSpec((1, d), lambda i: (0, 0))] + [_ANY] * n_ci,
        out_specs=[pl.BlockSpec((tm, d), lambda i: (i, 0)), pl.BlockSpec((d, tm), lambda i: (0, i)),
                   pl.BlockSpec((tm, 1), lambda i: (i, 0))] + [_ANY] * n_co,
        out_shape=[_sds((s_len, d), BF16), _sds((d, s_len), BF16), _sds((s_len, 1), F32)] + comm_outs,
        scratch_shapes=comm_sems,
        compiler_params=_params(("arbitrary",) if comm else ("parallel",)),
    )(x, g, *comm_arrays)
    return res[0], res[1], res[2], res[3:]


def _forget_cumsum(prest, b_f_pad, tc):
    s_len = prest.shape[0]

    def body(f_ref, b_ref, c_ref, carry):
        @pl.when(pl.program_id(0) == 0)
        def _():
            carry[...] = jnp.zeros_like(carry)

        logf = _log_sigmoid(f_ref[...] + b_ref[...])
        row = lax.broadcasted_iota(jnp.int32, (tc, tc), 0)
        col = lax.broadcasted_iota(jnp.int32, (tc, tc), 1)
        tri = (row >= col).astype(F32)
        c = _dot_f32(tri, logf) + carry[...]
        c_ref[...] = c
        carry[...] = c[tc - 1:tc, :]

    return pl.pallas_call(
        body,
        name="forget_cumsum",
        grid=(s_len // tc,),
        in_specs=[pl.BlockSpec((tc, LANES), lambda i: (i, 0)), pl.BlockSpec((1, LANES), lambda i: (0, 0))],
        out_specs=pl.BlockSpec((tc, LANES), lambda i: (i, 0)),
        out_shape=_sds((s_len, LANES), F32),
        scratch_shapes=[pltpu.VMEM((1, LANES), F32)],
        compiler_params=_params(("arbitrary",)),
    )(prest, b_f_pad)


def _stack_heads(pair, lt64):
    zero = jnp.zeros_like(pair)
    return jnp.concatenate([jnp.where(lt64, pair, zero), jnp.where(lt64, zero, pair)], axis=0)


def _score_tiles(q_ref, k_ref, ck_ref, st_sc, tk):
    lt64 = _lane_lt64((tk, LANES))
    for p in range(N_HEADS // 2):
        lanes = slice(p * LANES, (p + 1) * LANES)
        q_pair = q_ref[:, lanes] * jnp.asarray(HEAD_DIM ** -0.5, BF16)
        st2 = _dot_nt(_stack_heads(k_ref[:, lanes], lt64), q_pair)
        for half in range(2):
            h = 2 * p + half
            st_sc[h] = st2[half * tk:(half + 1) * tk] - ck_ref[:, h:h + 1]


ROW_CHUNK = 64


def _row_chunks(tk):
    rc = min(ROW_CHUNK, tk)
    return [slice(r, r + rc) for r in range(0, tk, rc)]


def _by_sublane(x):
    return x.reshape(x.shape[0] // 8, 8, x.shape[1])


def _softmax_update(st_sc, p_sc, m_sc, l_sc, tk, tq):
    alphas = []
    for h in range(N_HEADS):
        top8 = jnp.full((8, tq), NEG, F32)
        for rows in _row_chunks(tk):
            top8 = jnp.maximum(top8, jnp.max(_by_sublane(st_sc[h, rows, :]), axis=0))
        m_old = m_sc[h]
        m_new = jnp.maximum(m_old, jnp.max(top8, axis=0, keepdims=True))
        sum8 = jnp.zeros((8, tq), F32)
        for rows in _row_chunks(tk):
            pt = jnp.exp(st_sc[h, rows, :] - m_new)
            p_sc[h, rows, :] = pt.astype(BF16)
            sum8 = sum8 + jnp.sum(_by_sublane(pt), axis=0)
        alpha = jnp.exp(m_old - m_new)
        l_sc[h] = alpha * l_sc[h] + jnp.sum(sum8, axis=0, keepdims=True)
        m_sc[h] = m_new
        alphas.append(alpha)
    return alphas


def _mask_diagonal(st_sc, i, j, tq, tk):
    @pl.when((j + 1) * tk - 1 > i * tq)
    def _():
        key = j * tk + lax.broadcasted_iota(jnp.int32, (tk, tq), 0)
        query = i * tq + lax.broadcasted_iota(jnp.int32, (tk, tq), 1)
        st_sc[...] = jnp.where((query >= key)[None], st_sc[...], NEG)


def _attn_fwd(qkv, v_t, c_col, tq, tk, comm=()):
    s_len = qkv.shape[0]
    ratio = tq // tk
    steps = [(i, j) for i in range(s_len // tq) for j in range((i + 1) * ratio)]
    i_tab = jnp.asarray([i for i, _ in steps], jnp.int32)
    j_tab = jnp.asarray([j for _, j in steps], jnp.int32)

    comm_arrays, comm_outs, comm_sems = _comm_operands(comm)
    n_ci, n_co = len(comm_arrays), len(comm_outs)

    def body(i_ref, j_ref, q_ref, k_ref, vt_ref, ck_ref, *rest):
        comm_refs = rest[:n_ci], rest[n_ci + 3:n_ci + 3 + n_co], rest[n_ci + 3 + n_co + 5:]
        o_ref, ot_ref, lse_ref = rest[n_ci:n_ci + 3]
        acc_t, m_sc, l_sc, st_sc, p_sc = rest[n_ci + 3 + n_co:n_ci + 3 + n_co + 5]
        n = pl.program_id(0)
        i, j = i_ref[n], j_ref[n]
        for phase, at in (("start", 0), ("mid", (2 * len(steps)) // 3)):
            if comm:
                @pl.when(n == at)
                def _():
                    _comm_phase(comm, phase, *comm_refs)

        @pl.when(j == 0)
        def _():
            acc_t[...] = jnp.zeros_like(acc_t)
            m_sc[...] = jnp.full_like(m_sc, NEG)
            l_sc[...] = jnp.zeros_like(l_sc)

        _score_tiles(q_ref, k_ref, ck_ref, st_sc, tk)
        _mask_diagonal(st_sc, i, j, tq, tk)
        alpha = _softmax_update(st_sc, p_sc, m_sc, l_sc, tk, tq)
        top = lax.broadcasted_iota(jnp.int32, (LANES, tq), 0) < HEAD_DIM
        for p in range(N_HEADS // 2):
            lanes = slice(p * LANES, (p + 1) * LANES)
            vt_pair = vt_ref[lanes, :]
            pv = jnp.where(top, _dot(vt_pair, p_sc[2 * p]), _dot(vt_pair, p_sc[2 * p + 1]))
            acc_t[lanes, :] = acc_t[lanes, :] * jnp.where(top, alpha[2 * p], alpha[2 * p + 1]) + pv

        @pl.when(j == (i + 1) * ratio - 1)
        def _():
            for p in range(N_HEADS // 2):
                lanes = slice(p * LANES, (p + 1) * LANES)
                l_pair = jnp.where(top, l_sc[2 * p], l_sc[2 * p + 1])
                o_t = acc_t[lanes, :] / l_pair
                o_ref[:, lanes] = jnp.transpose(o_t)
                ot_ref[lanes, :] = o_t.astype(BF16)
            lse_ref[...] = m_sc[...] + jnp.log(l_sc[...])

        if comm:
            @pl.when(n == len(steps) - 1)
            def _():
                _comm_phase(comm, "finish", *comm_refs)

    stat = pltpu.VMEM((N_HEADS, 1, tq), F32)
    res = pl.pallas_call(
        body,
        name="attn_fwd",
        grid_spec=pltpu.PrefetchScalarGridSpec(
            num_scalar_prefetch=2,
            grid=(len(steps),),
            in_specs=[
                pl.BlockSpec((tq, FOX_W), lambda n, it, jt: (it[n], 0)),
                pl.BlockSpec((tk, FOX_W), lambda n, it, jt: (jt[n], 1)),
                pl.BlockSpec((FOX_W, tk), lambda n, it, jt: (0, jt[n])),
                pl.BlockSpec((tk, LANES), lambda n, it, jt: (jt[n], 0)),
            ] + [_ANY] * n_ci,
            out_specs=[
                pl.BlockSpec((tq, FOX_W), lambda n, it, jt: (it[n], 0)),
                pl.BlockSpec((FOX_W, tq), lambda n, it, jt: (0, it[n])),
                pl.BlockSpec((N_HEADS, 1, tq), lambda n, it, jt: (0, 0, it[n])),
            ] + [_ANY] * n_co,
            scratch_shapes=[pltpu.VMEM((FOX_W, tq), F32), stat, stat, pltpu.VMEM((N_HEADS, tk, tq), F32),
                            pltpu.VMEM((N_HEADS, tk, tq), BF16)] + comm_sems,
        ),
        out_shape=[_sds((s_len, FOX_W), F32), _sds((FOX_W, s_len), BF16), _sds((N_HEADS, 1, s_len), F32)] + comm_outs,
        compiler_params=_params(("arbitrary",)),
    )(i_tab, j_tab, qkv, qkv, v_t, c_col, *comm_arrays)
    return res[0], res[1], res[2], res[3:]


def _sgu_mix(vn, w_stack, lt64):
    outs = []
    for p in range(SGU_G // 2):
        r = _dot(w_stack[p], vn[:, p * LANES:(p + 1) * LANES])
        outs.append(jnp.where(lt64, r[:SGU_LEN], r[SGU_LEN:]))
    return jnp.concatenate(outs, axis=1)


def _sgu_norm(sv, ln_g, ln_b):
    svg = _gelu(sv)
    xc = svg - jnp.mean(svg, axis=-1, keepdims=True)
    rstd = lax.rsqrt(jnp.mean(xc * xc, axis=-1, keepdims=True) + EPS)
    xhat = xc * rstd
    return xhat, rstd, xhat * ln_g + ln_b


def _sgu_fwd(prest, ln_g, ln_b, w_stack, b_pair, tm):
    s_len = prest.shape[0]

    def body(u_ref, sv_ref, g_ref, b_ref, w_ref, bp_ref, sg_ref, sgt_ref):
        lt64 = _lane_lt64((SGU_LEN, LANES))
        _, _, vn = _sgu_norm(sv_ref[...].astype(F32), g_ref[...], b_ref[...])
        vn = vn.astype(BF16)
        w_stack_v = [w_ref[p] for p in range(SGU_G // 2)]
        for w in range(tm // SGU_LEN):
            win = slice(w * SGU_LEN, (w + 1) * SGU_LEN)
            mixed = _sgu_mix(vn[win], w_stack_v, lt64) + bp_ref[...]
            sg = (_gelu(u_ref[win, :].astype(F32)) * mixed).astype(BF16)
            sg_ref[win, :] = sg
            sgt_ref[:, win] = jnp.transpose(sg)

    return pl.pallas_call(
        body,
        name="sgu_fwd",
        grid=(s_len // tm,),
        in_specs=[
            pl.BlockSpec((tm, SGU_W), lambda i: (i, U_OFF // SGU_W)),
            pl.BlockSpec((tm, SGU_W), lambda i: (i, SV_OFF // SGU_W)),
            pl.BlockSpec((1, SGU_W), lambda i: (0, 0)),
            pl.BlockSpec((1, SGU_W), lambda i: (0, 0)),
            pl.BlockSpec((SGU_G // 2, 2 * SGU_LEN, SGU_LEN), lambda i: (0, 0, 0)),
            pl.BlockSpec((SGU_LEN, SGU_W), lambda i: (0, 0)),
        ],
        out_specs=[pl.BlockSpec((tm, SGU_W), lambda i: (i, 0)), pl.BlockSpec((SGU_W, tm), lambda i: (0, i))],
        out_shape=[_sds((s_len, SGU_W), BF16), _sds((SGU_W, s_len), BF16)],
        compiler_params=_params(("parallel",)),
    )(prest, prest, ln_g, ln_b, w_stack, b_pair)


def _sgu_bwd(prest, dsg, ln_g, ln_b, w_stack, wt_stack, b_pair, tm):
    s_len = prest.shape[0]
    n_pair = SGU_G // 2

    def body(u_ref, sv_ref, dsg_ref, g_ref, b_ref, w_ref, wt_ref, bp_ref,
             du_ref, dsv_ref, dw_ref, db_ref, dg_ref, dbeta_ref, dut_ref, dsvt_ref, dvn_sc):
        @pl.when(pl.program_id(0) == 0)
        def _():
            dw_ref[...] = jnp.zeros_like(dw_ref)
            db_ref[...] = jnp.zeros_like(db_ref)
            dg_ref[...] = jnp.zeros_like(dg_ref)
            dbeta_ref[...] = jnp.zeros_like(dbeta_ref)

        lt64 = _lane_lt64((SGU_LEN, LANES))
        sv = sv_ref[...].astype(F32)
        xhat, rstd, vn32 = _sgu_norm(sv, g_ref[...], b_ref[...])
        vn = vn32.astype(BF16)
        w_stack_v = [w_ref[p] for p in range(n_pair)]
        db = jnp.zeros((SGU_LEN, SGU_W), F32)
        for w in range(tm // SGU_LEN):
            win = slice(w * SGU_LEN, (w + 1) * SGU_LEN)
            u = u_ref[win, :].astype(F32)
            dsg_w = dsg_ref[win, :]
            mixed = _sgu_mix(vn[win], w_stack_v, lt64) + bp_ref[...]
            du = (dsg_w * mixed * _gelu_grad(u)).astype(BF16)
            du_ref[win, :] = du
            dut_ref[:, win] = jnp.transpose(du)
            dmixed = dsg_w * _gelu(u)
            db = db + dmixed
            dm16 = dmixed.astype(BF16)
            for p in range(n_pair):
                lanes = slice(p * LANES, (p + 1) * LANES)
                dmp = dm16[:, lanes]
                r = _dot(wt_ref[p], dmp)
                dvn_sc[win, lanes] = jnp.where(lt64, r[:SGU_LEN], r[SGU_LEN:])
                zero = jnp.zeros_like(dmp)
                dm_ab = jnp.concatenate([jnp.where(lt64, dmp, zero), jnp.where(lt64, zero, dmp)], axis=0)
                dw_ref[p] += _dot_nt(dm_ab, vn[win, lanes])
        db_ref[...] += db
        dvn = dvn_sc[...]
        dg_ref[...] += jnp.sum(dvn * xhat, axis=0, keepdims=True)
        dbeta_ref[...] += jnp.sum(dvn, axis=0, keepdims=True)
        dxh = dvn * g_ref[...]
        dsvg = rstd * (dxh - jnp.mean(dxh, axis=-1, keepdims=True) - xhat * jnp.mean(dxh * xhat, axis=-1, keepdims=True))
        dsv = (dsvg * _gelu_grad(sv)).astype(BF16)
        dsv_ref[...] = dsv
        dsvt_ref[...] = jnp.transpose(dsv)

    const2 = lambda i: (0, 0)
    const3 = lambda i: (0, 0, 0)
    return pl.pallas_call(
        body,
        name="sgu_bwd",
        grid=(s_len // tm,),
        in_specs=[
            pl.BlockSpec((tm, SGU_W), lambda i: (i, U_OFF // SGU_W)),
            pl.BlockSpec((tm, SGU_W), lambda i: (i, SV_OFF // SGU_W)),
            pl.BlockSpec((tm, SGU_W), lambda i: (i, 0)),
            pl.BlockSpec((1, SGU_W), const2),
            pl.BlockSpec((1, SGU_W), const2),
            pl.BlockSpec((n_pair, 2 * SGU_LEN, SGU_LEN), const3),
            pl.BlockSpec((n_pair, 2 * SGU_LEN, SGU_LEN), const3),
            pl.BlockSpec((SGU_LEN, SGU_W), const2),
        ],
        out_specs=[
            pl.BlockSpec((tm, SGU_W), lambda i: (i, 0)),
            pl.BlockSpec((tm, SGU_W), lambda i: (i, 0)),
            pl.BlockSpec((n_pair, 2 * SGU_LEN, SGU_LEN), const3),
            pl.BlockSpec((SGU_LEN, SGU_W), const2),
            pl.BlockSpec((1, SGU_W), const2),
            pl.BlockSpec((1, SGU_W), const2),
            pl.BlockSpec((SGU_W, tm), lambda i: (0, i)),
            pl.BlockSpec((SGU_W, tm), lambda i: (0, i)),
        ],
        out_shape=[
            _sds((s_len, SGU_W), BF16), _sds((s_len, SGU_W), BF16), _sds((n_pair, 2 * SGU_LEN, SGU_LEN), F32),
            _sds((SGU_LEN, SGU_W), F32), _sds((1, SGU_W), F32), _sds((1, SGU_W), F32),
            _sds((SGU_W, s_len), BF16), _sds((SGU_W, s_len), BF16),
        ],
        scratch_shapes=[pltpu.VMEM((tm, SGU_W), F32)],
        compiler_params=_params(("arbitrary",)),
    )(prest, prest, dsg, ln_g, ln_b, w_stack, wt_stack, b_pair)


def _attn_bwd(qkv, k_t, do, c_col, lse_row, delta_row, tq, tk, comm=()):
    s_len = qkv.shape[0]
    nq, nk = s_len // tq, s_len // tk
    ratio = tq // tk
    scale = HEAD_DIM ** -0.5
    steps = [(j, i) for j in range(nk) for i in range(j // ratio, nq)]
    j_tab = jnp.asarray([j for j, _ in steps], jnp.int32)
    i_tab = jnp.asarray([i for _, i in steps], jnp.int32)

    comm_arrays, comm_outs, comm_sems = _comm_operands(comm)
    n_ci, n_co = len(comm_arrays), len(comm_outs)

    def body(j_ref, i_ref, q_ref, k_ref, v_ref, kt_ref, do_ref, ck_ref, lse_ref, dl_ref, *rest):
        comm_refs = rest[:n_ci], rest[n_ci + 7:n_ci + 7 + n_co], rest[n_ci + 7 + n_co + 8:]
        dq_ref, dk_ref, dv_ref, dcr_ref, dcc_ref, dkt_ref, dvt_ref = rest[n_ci:n_ci + 7]
        dq_t, dk_acc, dv_acc, dcc_acc, st_sc, dpt_sc, p_sc, ds_sc = rest[n_ci + 7 + n_co:n_ci + 7 + n_co + 8]
        n = pl.program_id(0)
        j, i = j_ref[n], i_ref[n]

        @pl.when(n == 0)
        def _():
            _comm_phase(comm, "start", *comm_refs)
            dq_t[...] = jnp.zeros_like(dq_t)
            dcr_ref[...] = jnp.zeros_like(dcr_ref)

        @pl.when(i == j // ratio)
        def _():
            dk_acc[...] = jnp.zeros_like(dk_acc)
            dv_acc[...] = jnp.zeros_like(dv_acc)
            dcc_acc[...] = jnp.zeros_like(dcc_acc)

        lt64 = _lane_lt64((tk, LANES))
        _score_tiles(q_ref, k_ref, ck_ref, st_sc, tk)
        for p in range(N_HEADS // 2):
            lanes = slice(p * LANES, (p + 1) * LANES)
            dpt2 = _dot_nt(_stack_heads(v_ref[:, lanes], lt64), do_ref[:, lanes].astype(BF16))
            dpt_sc[2 * p] = dpt2[:tk]
            dpt_sc[2 * p + 1] = dpt2[tk:]
        _mask_diagonal(st_sc, i, j, tq, tk)

        pt = jnp.exp(st_sc[...] - lse_ref[...])
        dst = pt * (dpt_sc[...] - dl_ref[...])
        p_sc[...] = pt.astype(BF16)
        ds_sc[...] = dst.astype(BF16)
        dcr_ref[i] += jnp.sum(dst, axis=1, keepdims=True)
        col_sums = jnp.sum(dst, axis=2, keepdims=True)
        lane = lax.broadcasted_iota(jnp.int32, (tk, LANES), 1)
        dcc = jnp.zeros((tk, LANES), F32)
        for h in range(N_HEADS):
            dcc = jnp.where(lane == h, -col_sums[h], dcc)
        dcc_acc[...] += dcc

        for p in range(N_HEADS // 2):
            lanes = slice(p * LANES, (p + 1) * LANES)
            q_pair = q_ref[:, lanes] * jnp.asarray(scale, BF16)
            dv2 = _dot(p_sc[2 * p:2 * p + 2].reshape(2 * tk, tq), do_ref[:, lanes].astype(BF16))
            dv_acc[:, lanes] += jnp.where(lt64, dv2[:tk], dv2[tk:])
            dk2 = _dot(ds_sc[2 * p:2 * p + 2].reshape(2 * tk, tq), q_pair)
            dk_acc[:, lanes] += jnp.where(lt64, dk2[:tk], dk2[tk:])
            dq2 = _dot(kt_ref[lanes, :], jnp.concatenate([ds_sc[2 * p], ds_sc[2 * p + 1]], axis=1))
            top = lax.broadcasted_iota(jnp.int32, (LANES, tq), 0) < HEAD_DIM
            dq_t[i, lanes, :] += jnp.where(top, dq2[:, :tq], dq2[:, tq:])

        @pl.when(j == (i + 1) * ratio - 1)
        def _():
            rows = pl.ds(pl.multiple_of(i * tq, tq), tq)
            for p in range(N_HEADS // 2):
                lanes = slice(p * LANES, (p + 1) * LANES)
                dq_ref[rows, lanes] = (jnp.transpose(dq_t[i, lanes, :]) * scale).astype(BF16)

        @pl.when(i == nq - 1)
        def _():
            dk16, dv16 = dk_acc[...].astype(BF16), dv_acc[...].astype(BF16)
            dk_ref[...] = dk16
            dv_ref[...] = dv16
            dkt_ref[...] = jnp.transpose(dk16)
            dvt_ref[...] = jnp.transpose(dv16)
            dcc_ref[...] = dcc_acc[...]

        if comm:
            @pl.when(n == len(steps) // 2)
            def _():
                _comm_phase(comm, "mid", *comm_refs)

            @pl.when(n == len(steps) - 1)
            def _():
                _comm_phase(comm, "finish", *comm_refs)

    q_map = lambda n, jt, it: (it[n], 0)
    q_stat = lambda n, jt, it: (0, 0, it[n])
    k_map = lambda n, jt, it: (jt[n], 0)
    tile = (N_HEADS, tk, tq)
    res = pl.pallas_call(
        body,
        name="attn_bwd",
        grid_spec=pltpu.PrefetchScalarGridSpec(
            num_scalar_prefetch=2,
            grid=(len(steps),),
            in_specs=[
                pl.BlockSpec((tq, FOX_W), q_map),
                pl.BlockSpec((tk, FOX_W), lambda n, jt, it: (jt[n], 1)),
                pl.BlockSpec((tk, FOX_W), lambda n, jt, it: (jt[n], 2)),
                pl.BlockSpec((FOX_W, tk), lambda n, jt, it: (0, jt[n])),
                pl.BlockSpec((tq, FOX_W), q_map),
                pl.BlockSpec((tk, LANES), k_map),
                pl.BlockSpec((N_HEADS, 1, tq), q_stat),
                pl.BlockSpec((N_HEADS, 1, tq), q_stat),
            ] + [_ANY] * n_ci,
            out_specs=[
                pl.BlockSpec((s_len, FOX_W), lambda n, jt, it: (0, 0)),
                pl.BlockSpec((tk, FOX_W), k_map),
                pl.BlockSpec((tk, FOX_W), k_map),
                pl.BlockSpec((nq, N_HEADS, 1, tq), lambda n, jt, it: (0, 0, 0, 0)),
                pl.BlockSpec((tk, LANES), k_map),
                pl.BlockSpec((FOX_W, tk), lambda n, jt, it: (0, jt[n])),
                pl.BlockSpec((FOX_W, tk), lambda n, jt, it: (0, jt[n])),
            ] + [_ANY] * n_co,
            scratch_shapes=[pltpu.VMEM((nq, FOX_W, tq), F32), pltpu.VMEM((tk, FOX_W), F32), pltpu.VMEM((tk, FOX_W), F32),
                            pltpu.VMEM((tk, LANES), F32), pltpu.VMEM(tile, F32), pltpu.VMEM(tile, F32),
                            pltpu.VMEM(tile, BF16), pltpu.VMEM(tile, BF16)] + comm_sems,
        ),
        out_shape=[_sds((s_len, FOX_W), BF16), _sds((s_len, FOX_W), BF16), _sds((s_len, FOX_W), BF16),
                   _sds((nq, N_HEADS, 1, tq), F32), _sds((s_len, LANES), F32),
                   _sds((FOX_W, s_len), BF16), _sds((FOX_W, s_len), BF16)] + comm_outs,
        compiler_params=_params(("arbitrary",)),
    )(j_tab, i_tab, qkv, qkv, qkv, k_t, do, c_col, lse_row, delta_row, *comm_arrays)
    return res[:7], res[7:]


def _forget_bwd(dc_rows, dc_cols, prest, b_f_pad, tc):
    s_len = dc_rows.shape[0]
    nb = s_len // tc

    def body(dcr_ref, dc_ref, f_ref, b_ref, df_ref, db_ref, dft_ref, carry):
        @pl.when(pl.program_id(0) == 0)
        def _():
            carry[...] = jnp.zeros_like(carry)
            db_ref[...] = jnp.zeros_like(db_ref)

        row = lax.broadcasted_iota(jnp.int32, (tc, tc), 0)
        col = lax.broadcasted_iota(jnp.int32, (tc, tc), 1)
        tri = (row <= col).astype(F32)
        dlogf = _dot_f32(tri, dcr_ref[...] + dc_ref[...]) + carry[...]
        carry[...] = dlogf[0:1, :]
        z = f_ref[...] + b_ref[...]
        lane = lax.broadcasted_iota(jnp.int32, (tc, LANES), 1)
        dz = jnp.where(lane < N_HEADS, dlogf * _sigmoid(-z), 0.0)
        df_ref[...] = dz.astype(BF16)
        dft_ref[...] = jnp.transpose(dz).astype(BF16)
        db_ref[...] += jnp.sum(dz, axis=0, keepdims=True)

    rev = lambda i: (nb - 1 - i, 0)
    return pl.pallas_call(
        body,
        name="forget_bwd",
        grid=(nb,),
        in_specs=[
            pl.BlockSpec((tc, LANES), rev),
            pl.BlockSpec((tc, LANES), rev),
            pl.BlockSpec((tc, LANES), rev),
            pl.BlockSpec((1, LANES), lambda i: (0, 0)),
        ],
        out_specs=[pl.BlockSpec((tc, LANES), rev), pl.BlockSpec((1, LANES), lambda i: (0, 0)),
                   pl.BlockSpec((LANES, tc), lambda i: (0, nb - 1 - i))],
        out_shape=[_sds((s_len, LANES), BF16), _sds((1, LANES), F32), _sds((LANES, s_len), BF16)],
        scratch_shapes=[pltpu.VMEM((1, LANES), F32)],
        compiler_params=_params(("arbitrary",)),
    )(dc_rows, dc_cols, prest, b_f_pad)


def _pair_sum(name, g4, recv, idx, tr):
    _, _, r, c = g4.shape

    def body(idx_ref, g_ref, r_ref, p16_ref, own_ref):
        k = pl.program_id(1)
        s = g_ref[...] + r_ref[...]
        p16_ref[...] = s.astype(BF16)

        @pl.when(k == idx_ref[1])
        def _():
            own_ref[...] = s

    return pl.pallas_call(
        body,
        name=name,
        grid_spec=pltpu.PrefetchScalarGridSpec(
            num_scalar_prefetch=1,
            grid=(r // tr, 4),
            in_specs=[
                pl.BlockSpec((None, None, tr, c), lambda i, k, idx: (k, idx[0], i, 0)),
                pl.BlockSpec((None, tr, c), lambda i, k, idx: (k, i, 0)),
            ],
            out_specs=[
                pl.BlockSpec((None, tr, c), lambda i, k, idx: (k, i, 0)),
                pl.BlockSpec((tr, c), lambda i, k, idx: (i, 0)),
            ],
        ),
        out_shape=[_sds((4, r, c), BF16), _sds((r, c), F32)],
        compiler_params=_params(("parallel", "arbitrary")),
    )(idx, g4, recv)


def _adamw_math(w, g, m, v):
    m2 = ADAM_B1 * m + (1.0 - ADAM_B1) * g
    v2 = ADAM_B2 * v + (1.0 - ADAM_B2) * (g * g)
    m_hat = m2 / (1.0 - ADAM_B1 ** ADAM_STEP)
    v_hat = v2 / (1.0 - ADAM_B2 ** ADAM_STEP)
    delta = -ADAM_LR * (m_hat / (jnp.sqrt(v_hat) + ADAM_EPS) + ADAM_WD * w)
    return delta, m2, v2


def _adamw_shard(name, own, recv, w, m, v, tr):
    r, c = own.shape

    def body(own_ref, recv_ref, w_ref, m_ref, v_ref, g_ref, d_ref, m2_ref, v2_ref):
        g = own_ref[...]
        for k in range(3):
            g = g + recv_ref[k].astype(F32)
        delta, m2, v2 = _adamw_math(w_ref[...], g, m_ref[...], v_ref[...])
        g_ref[...] = g
        d_ref[...] = delta
        m2_ref[...] = m2
        v2_ref[...] = v2

    spec = pl.BlockSpec((tr, c), lambda i: (i, 0))
    return pl.pallas_call(
        body,
        name=name,
        grid=(r // tr,),
        in_specs=[spec, pl.BlockSpec((3, tr, c), lambda i: (0, i, 0)), spec, spec, spec],
        out_specs=[spec] * 4,
        out_shape=[_sds((r, c), F32)] * 4,
        compiler_params=_params(("parallel",)),
    )(own, recv, w, m, v)


def _adamw_small(gathered, w, m, v):
    _, r, _ = gathered.shape

    def body(ga_ref, w_ref, m_ref, v_ref, g_ref, d_ref, m2_ref, v2_ref):
        g = ga_ref[0]
        for k in range(1, N_DEV):
            g = g + ga_ref[k]
        delta, m2, v2 = _adamw_math(w_ref[...], g, m_ref[...], v_ref[...])
        g_ref[...] = g
        d_ref[...] = delta
        m2_ref[...] = m2
        v2_ref[...] = v2

    spec = pl.BlockSpec((r, LANES), lambda i: (0, 0))
    return pl.pallas_call(
        body,
        name="adamw_small",
        grid=(1,),
        in_specs=[pl.BlockSpec((N_DEV, r, LANES), lambda i: (0, 0, 0)), spec, spec, spec],
        out_specs=[spec] * 4,
        out_shape=[_sds((r, LANES), F32)] * 4,
        compiler_params=_params(("arbitrary",)),
    )(gathered, w, m, v)


_SMALL = (("w_sgu", (1, SGU_G, SGU_LEN, SGU_LEN)), ("b_sgu", (1, SGU_G, SGU_LEN)), ("norm2_g", (1, D_MODEL)),
          ("normf_g", (D_MODEL,)), ("ln_v_g", (1, SGU_W)), ("ln_v_b", (1, SGU_W)), ("b_f", (1, N_HEADS)),
          ("norm1_g", (1, D_MODEL)), ("loss", ()))
_N_EARLY = 6


def _pack_rows(values):
    rows = []
    for val in values:
        flat = val.reshape(-1).astype(F32)
        pad = (-flat.shape[0]) % LANES
        rows.append(jnp.pad(flat, (0, pad)).reshape(-1, LANES))
    packed = jnp.concatenate(rows, axis=0)
    return jnp.pad(packed, ((0, (-packed.shape[0]) % 8), (0, 0)))


def _pack_small(values):
    return jnp.concatenate([_pack_rows(values[:_N_EARLY]), _pack_rows(values[_N_EARLY:])], axis=0)


def _unpack_small(packed):
    out, row = [], 0
    for k, (_, shape) in enumerate(_SMALL):
        if k == _N_EARLY:
            row += (-row) % 8
        size = math.prod(shape)
        n_rows = -(-size // LANES)
        out.append(packed[row:row + n_rows].reshape(-1)[:size].reshape(shape))
        row += n_rows
    return out


def kernel(x, norm1_g, w_in, b_f, ln_v_g, ln_v_b, w_sgu, b_sgu, w_a, w_b, w_o, norm2_g, w_up, w_down, normf_g, loss_target, m_norm1_g, m_w_in, m_b_f, m_ln_v_g, m_ln_v_b, m_w_sgu, m_b_sgu, m_w_a, m_w_b, m_w_o, m_norm2_g, m_w_up, m_w_down, m_normf_g, v_norm1_g, v_w_in, v_b_f, v_ln_v_g, v_ln_v_b, v_w_sgu, v_b_sgu, v_w_a, v_w_b, v_w_o, v_norm2_g, v_w_up, v_w_down, v_normf_g):
    xs = x[0]
    target = loss_target[0]
    s_len, d = xs.shape
    tm = min(512, s_len)
    tl = min(1024, s_len)
    tr = min(256, s_len)
    ta = min(512, s_len)
    tc = min(512, s_len)

    w_in_t = jnp.transpose(w_in[0])
    lin = (IN_SHARD * d // LANES, LANES)
    big = (w_in_t.reshape(lin), w_a[0], w_b[0], w_o[0], w_up[0], w_down[0])
    h, h_t, r1, (w_in_g,) = _rms_fwd("rms1", xs, norm1_g, tm, comm=[_gather_plan(w_in_t.astype(BF16))])
    w_in_f = w_in_g.reshape(IN_COLS, d)
    later_shards = [jnp.transpose(w_a[0]), jnp.transpose(w_b[0]), w_o[0], jnp.transpose(w_up[0]), w_down[0]]
    later_plans = [_gather_plan(w.astype(BF16)) for w in later_shards]

    def unflatten(gathered):
        return [g.reshape(N_DEV * g.shape[1], g.shape[2]) for g in gathered]

    w_qkv = w_in_f[:QKV_W]
    f_lo = QKV_W
    u_lo = f_lo + N_HEADS
    w_rest = jnp.concatenate([w_in_f[u_lo:], jnp.pad(w_in_f[f_lo:u_lo], ((0, LANES - N_HEADS), (0, 0)))], axis=0)

    chunk_id = jnp.arange(SGU_LEN) // CHUNK
    sgu_mask = chunk_id[None, :] <= chunk_id[:, None]
    w_masked = jnp.where(sgu_mask[None], w_sgu[0], 0.0)
    w_stack = w_masked.reshape(SGU_G // 2, 2 * SGU_LEN, SGU_LEN).astype(BF16)
    wt_stack = jnp.transpose(w_masked, (0, 2, 1)).reshape(SGU_G // 2, 2 * SGU_LEN, SGU_LEN).astype(BF16)
    b_pair = jnp.transpose(jnp.repeat(b_sgu[0], SGU_W // SGU_G, axis=0))
    b_f_pad = jnp.pad(b_f, ((0, 0), (0, LANES - N_HEADS)))
    head_sel = (jnp.arange(FOX_W)[:, None] // HEAD_DIM == jnp.arange(LANES)[None, :]).astype(F32)

    def store(dtype):
        def epi(accs, ex, out):
            out[0][...] = accs[0].astype(dtype)
        return epi

    def qkv_epi(accs, ex, out):
        tile = accs[0].astype(BF16)
        out[0][...] = tile
        for col, ref in ((1, out[1]), (2, out[2])):
            @pl.when(pl.program_id(1) == col)
            def _():
                ref[...] = jnp.transpose(tile)

    t_spec = pl.BlockSpec((FOX_W, tl), lambda i, j: (0, i))
    qkv, k_t, v_t = _mm("proj_qkv", [(h, w_qkv, True, None)], [],
                        [(_sds((s_len, QKV_W), BF16), _tile(tl, FOX_W)), (_sds((FOX_W, s_len), BF16), t_spec),
                         (_sds((FOX_W, s_len), BF16), t_spec)],
                        qkv_epi, m=s_len, tm=tl, n=QKV_W, tn=FOX_W, arbitrary=True)
    rest_tn = 640
    f_tile, f_lane = F_OFF // rest_tn, F_OFF % rest_tn

    def rest_epi(accs, ex, out):
        out[0][...] = accs[0].astype(BF16)

        @pl.when(pl.program_id(1) == f_tile)
        def _():
            out[1][...] = accs[0][:, f_lane:f_lane + LANES]

    prest, f_logit = _mm("proj_rest", [(h, w_rest, True, None)], [],
                         [(_sds((s_len, REST_W), BF16), _tile(tl, rest_tn)), (_sds((s_len, LANES), F32), _row(tl, LANES))],
                         rest_epi, m=s_len, tm=tl, n=REST_W, tn=rest_tn, arbitrary=True)

    c_col = _forget_cumsum(f_logit, b_f_pad, tc)
    o, o_t, lse_row, later_g = _attn_fwd(qkv, v_t, c_col, ta, ta, comm=later_plans)
    w_a_t, w_b_t, w_o_f, w_up_t, w_down_f = unflatten(later_g)
    sg, sg_t = _sgu_fwd(prest, ln_v_g, ln_v_b, w_stack, b_pair, tm)

    def merge_epi(accs, ex, out):
        ya, yb = accs
        sa, sb = _sigmoid(ex[0][...].astype(F32)), _sigmoid(ex[1][...].astype(F32))
        merged = (sa * ya + sb * yb).astype(BF16)
        out[0][...] = merged
        out[1][...] = ya.astype(BF16)
        out[2][...] = yb.astype(BF16)
        out[3][...] = jnp.transpose(merged)

    merged, ya, yb, merged_t = _mm(
        "merge", [(o, w_a_t, True, None), (sg, w_b_t, True, None)],
        [(prest, _tile(tm, d, GA_OFF // d)), (prest, _tile(tm, d, GB_OFF // d))],
        [(_sds((s_len, d), BF16), _tile(tm, d))] * 3 + [(_sds((d, s_len), BF16), _tile_t(tm, d))],
        merge_epi, m=s_len, tm=tm, n=d, tn=d)

    def resid_epi(accs, ex, out):
        x1v = ex[0][...] + accs[0]
        out[0][...] = x1v
        r = lax.rsqrt(jnp.mean(x1v * x1v, axis=-1, keepdims=True) + EPS)
        h2v = (x1v * r * ex[1][...]).astype(BF16)
        out[1][...] = h2v
        out[2][...] = jnp.transpose(h2v)
        out[3][...] = r

    x1, h2, h2_t, r2 = _mm(
        "out_proj", [(merged, w_o_f, False, None)], [(xs, _tile(tm, d)), (norm2_g, _whole((1, d)))],
        [(_sds((s_len, d), F32), _tile(tm, d)), (_sds((s_len, d), BF16), _tile(tm, d)),
         (_sds((d, s_len), BF16), _tile_t(tm, d)), (_sds((s_len, 1), F32), _row(tm, 1))],
        resid_epi, m=s_len, tm=tm, n=d, tn=d)

    def up_epi(accs, ex, out):
        act = jnp.square(jnp.maximum(accs[0], 0.0)).astype(BF16)
        out[0][...] = act
        out[1][...] = jnp.transpose(act)

    act, act_t = _mm(
        "mlp_up", [(h2, w_up_t, True, None)], [],
        [(_sds((s_len, D_FF), BF16), _tile(tl, 512)), (_sds((D_FF, s_len), BF16), _tile_t(tl, 512))],
        up_epi, m=s_len, tm=tl, n=D_FF, tn=512)

    def first_step():
        return jnp.logical_and(pl.program_id(0) == 0, pl.program_id(1) == 0)

    def accumulate(ref, val):
        @pl.when(first_step())
        def _():
            ref[...] = val

        @pl.when(jnp.logical_not(first_step()))
        def _():
            ref[...] += val

    def final_epi(accs, ex, out):
        x1_ref, t_ref, g_ref = ex
        x2 = x1_ref[...] + accs[0]
        rf = lax.rsqrt(jnp.mean(x2 * x2, axis=-1, keepdims=True) + EPS)
        xh = x2 * rf
        gf = g_ref[...]
        err = xh * gf - t_ref[...]
        dy = err * (1.0 / d)
        dx2 = _rms_bwd(xh, rf, gf, dy)
        out[0][...] = dx2
        accumulate(out[1], jnp.sum(dy * xh, axis=0, keepdims=True))
        part = 0.5 * jnp.sum(jnp.sum(err * err, axis=-1, keepdims=True) * (1.0 / d), axis=0, keepdims=True)
        accumulate(out[2], jnp.broadcast_to(part, (1, LANES)))
        out[3][...] = dx2.astype(BF16)

    gf2 = normf_g.reshape(1, d)
    dx2, g_normf, loss_part, dx2_16 = _mm(
        "mlp_down_loss", [(act, w_down_f, False, None)],
        [(x1, _row(tr, d)), (target, _row(tr, d)), (gf2, _whole((1, d)))],
        [(_sds((s_len, d), F32), _row(tr, d)), (_sds((1, d), F32), _whole((1, d))), (_sds((1, LANES), F32), _whole((1, LANES))),
         (_sds((s_len, d), BF16), _row(tr, d))],
        final_epi, m=s_len, tm=tr, n=d, tn=d, arbitrary=True)

    def dact_epi(accs, ex, out):
        out[0][...] = (accs[0] * (2.0 * jnp.sqrt(ex[0][...].astype(F32)))).astype(BF16)

    (da,) = _mm("mlp_down_bwd", [(dx2_16, w_down_f, True, None)], [(act, _tile(tl, 512))],
                [(_sds((s_len, D_FF), BF16), _tile(tl, 512))], dact_epi, m=s_len, tm=tl, n=D_FF, tn=512)
    g_down = _grad_w("grad_w_down", act_t, dx2_16, tk=1024, tn=d, ts=tl)
    g_up = _grad_w("grad_w_up", h2_t, da, tk=d, tn=1024, ts=tl, block_cols=D_FF // N_DEV)

    def dh2_epi(accs, ex, out):
        x1_ref, r_ref, g_ref, dx2_ref = ex
        r = r_ref[...]
        xh = x1_ref[...] * r
        dh2 = accs[0]
        out[0][...] = dx2_ref[...] + _rms_bwd(xh, r, g_ref[...], dh2)
        accumulate(out[1], jnp.sum(dh2 * xh, axis=0, keepdims=True))

    my_c = lax.axis_index("c")
    my_chip = 2 * lax.axis_index("x") + lax.axis_index("y")
    idx = jnp.stack([my_c, my_chip]).astype(jnp.int32)
    parts16, owns = {}, {}

    def split_cores(g8):
        return g8.reshape((4, 2) + g8.shape[1:])

    def row_tile(r):
        return 512 if r % 512 == 0 else r

    def pair_sums(names, grads4, from_sibling):
        for name, g4, recv in zip(names, grads4, from_sibling):
            parts16[name], owns[name] = _pair_sum("grad_pair_sum_" + name, g4, recv, idx, row_tile(g4.shape[2]))

    grads4_mlp = [split_cores(g_up), split_cores(g_down.reshape(N_DEV, D_FF // N_DEV, d))]
    (dx1, g_norm2), from_sibling = _mm(
        "mlp_up_bwd", [(da, w_up_t, False, None)],
        [(x1, _row(tr, d)), (r2, _row(tr, 1)), (norm2_g, _whole((1, d))), (dx2, _row(tr, d))],
        [(_sds((s_len, d), F32), _row(tr, d)), (_sds((1, d), F32), _whole((1, d)))],
        dh2_epi, m=s_len, tm=tr, n=d, tn=d, arbitrary=True, comm=[_pair_exchange_plan(grads4_mlp)])
    pair_sums(("w_up", "w_down"), grads4_mlp, from_sibling)

    def dmerge_epi(accs, ex, out):
        dm = accs[0]
        sa, sb = _sigmoid(ex[0][...].astype(F32)), _sigmoid(ex[1][...].astype(F32))
        out[0][...] = (dm * sa).astype(BF16)
        out[1][...] = (dm * sb).astype(BF16)
        dga = (dm * ex[2][...] * sa * (1.0 - sa)).astype(BF16)
        dgb = (dm * ex[3][...] * sb * (1.0 - sb)).astype(BF16)
        out[2][...] = dga
        out[3][...] = dgb
        out[4][...] = jnp.transpose(dga)
        out[5][...] = jnp.transpose(dgb)

    dya, dyb, dga, dgb, dga_t, dgb_t = _mm(
        "out_proj_bwd", [(dx1, w_o_f, True, None)],
        [(prest, _tile(tm, d, GA_OFF // d)), (prest, _tile(tm, d, GB_OFF // d)), (ya, _tile(tm, d)), (yb, _tile(tm, d))],
        [(_sds((s_len, d), BF16), _tile(tm, d))] * 4 + [(_sds((d, s_len), BF16), _tile_t(tm, d))] * 2,
        dmerge_epi, m=s_len, tm=tm, n=d, tn=d)
    g_o = _grad_w("grad_w_o", merged_t, dx1, tk=d, tn=d, ts=tl).reshape(N_DEV, d // N_DEV, d)
    def col_blocks(g):
        return jnp.transpose(g.reshape(g.shape[0], N_DEV, g.shape[1] // N_DEV), (1, 0, 2))

    g_a = col_blocks(_grad_w("grad_w_a", o_t, dya, tk=FOX_W, tn=d, ts=tl))
    g_b = col_blocks(_grad_w("grad_w_b", sg_t, dyb, tk=SGU_W, tn=d, ts=tl))

    def do_epi(accs, ex, out):
        do = accs[0]
        out[0][...] = do
        out[1][...] = _dot_f32(do * ex[0][...], ex[1][...])

    grads4_mix = [split_cores(g) for g in (g_a, g_b, g_o)]
    (do, delta), from_sibling = _mm(
        "attn_out_bwd", [(dya, w_a_t, False, None)], [(o, _row(tm, FOX_W)), (head_sel, _whole((FOX_W, LANES)))],
        [(_sds((s_len, FOX_W), F32), _row(tm, FOX_W)), (_sds((s_len, LANES), F32), _row(tm, LANES))],
        do_epi, m=s_len, tm=tm, n=FOX_W, tn=FOX_W, comm=[_pair_exchange_plan(grads4_mix)])
    pair_sums(("w_a", "w_b", "w_o"), grads4_mix, from_sibling)
    (dsg,) = _mm("sgu_out_bwd", [(dyb, w_b_t, False, None)], [], [(_sds((s_len, SGU_W), F32), _tile(tm, SGU_W))],
                 store(F32), m=s_len, tm=tm, n=SGU_W, tn=SGU_W)

    du, dsv, dw_pairs, db_pos, g_ln_g, g_ln_b, du_t, dsv_t = _sgu_bwd(
        prest, dsg, ln_v_g, ln_v_b, w_stack, wt_stack, b_pair, tm)
    g_w_sgu = jnp.where(sgu_mask[None], dw_pairs.reshape(SGU_G, SGU_LEN, SGU_LEN), 0.0)
    g_b_sgu = jnp.transpose(jnp.sum(db_pos.reshape(SGU_LEN, SGU_G, SGU_W // SGU_G), axis=-1))

    delta_row = jnp.transpose(delta[:, :N_HEADS]).reshape(N_HEADS, 1, s_len)
    early = ("w_a", "w_b", "w_o", "w_up", "w_down")
    small_early = _pack_rows((g_w_sgu, g_b_sgu, g_norm2, g_normf, g_ln_g, g_ln_b))
    (dq, dk, dv, dc_rows_blk, dc_cols, dk_t, dv_t), (small_early_all, *from_chips_early) = _attn_bwd(
        qkv, k_t, do, c_col, lse_row, delta_row, ta, ta,
        comm=[_gather_plan(small_early), _chip_exchange_plan([parts16[n] for n in early])])
    dc_rows = jnp.transpose(dc_rows_blk.reshape(s_len // ta, N_HEADS, ta), (0, 2, 1)).reshape(s_len, N_HEADS)
    dc_rows = jnp.pad(dc_rows, ((0, 0), (0, LANES - N_HEADS)))
    dfl, g_bf, dfl_t = _forget_bwd(dc_rows, dc_cols, f_logit, b_f_pad, tc)

    dp_t = (jnp.transpose(dq), dk_t, dv_t, dfl_t, du_t, dsv_t, dga_t, dgb_t)
    g_in_rows = [_grad_w("grad_w_in_%d" % k, seg_t, h, tk=seg_t.shape[0], tn=d, ts=tl) for k, seg_t in enumerate(dp_t)]
    g_in_rows[3] = g_in_rows[3][:N_HEADS]
    g_in = jnp.concatenate(g_in_rows, axis=0).reshape((N_DEV,) + lin)

    def dx_epi(accs, ex, out):
        x_ref, r_ref, g_ref, dx1_ref = ex
        dh = accs[0]
        for extra in accs[1:]:
            dh = dh + extra
        r = r_ref[...]
        xh = x_ref[...] * r
        out[0][...] = dx1_ref[...] + _rms_bwd(xh, r, g_ref[...], dh)
        accumulate(out[1], jnp.sum(dh * xh, axis=0, keepdims=True))

    rest_cols = ((du, U_OFF, 512), (dsv, SV_OFF, 512), (dga, GA_OFF, 1024), (dgb, GB_OFF, 1024), (dfl, F_OFF, LANES))
    dx_pairs = [(seg, w_qkv, False, (512 * k, 512 * (k + 1))) for k, seg in enumerate((dq, dk, dv))]
    dx_pairs += [(seg, w_rest, False, (lo, lo + width)) for seg, lo, width in rest_cols]
    grads4_in = [split_cores(g_in)]
    pair_sums(("w_in",), grads4_in, _run_comm("grad_pair_exchange_w_in", [_pair_exchange_plan(grads4_in)]))
    (grad_x, g_norm1), (from_chips_in,) = _mm(
        "proj_bwd", dx_pairs,
        [(xs, _row(tr, d)), (r1, _row(tr, 1)), (norm1_g, _whole((1, d))), (dx1, _row(tr, d))],
        [(_sds((s_len, d), F32), _row(tr, d)), (_sds((1, d), F32), _whole((1, d)))],
        dx_epi, m=s_len, tm=tr, n=d, tn=d, arbitrary=True, comm=[_chip_exchange_plan([parts16["w_in"]])])
    small_late = _pack_rows((g_bf[:, :N_HEADS], g_norm1, loss_part[0, 0]))
    (small_late_all,) = _run_comm("gather_last_grads", [_gather_plan(small_late)])
    small_all = jnp.concatenate([small_early_all, small_late_all], axis=1)
    from_chips = dict(zip(early, from_chips_early), w_in=from_chips_in)

    names = ("w_in", "w_a", "w_b", "w_o", "w_up", "w_down")
    moments_m = (m_w_in, m_w_a, m_w_b, m_w_o, m_w_up, m_w_down)
    moments_v = (v_w_in, v_w_a, v_w_b, v_w_o, v_w_up, v_w_down)
    big_out = {}
    for name, w, m, v in zip(names, big, moments_m, moments_v):
        own = owns[name]
        transposed = name == "w_in"
        m0, v0 = (jnp.transpose(m[0]).reshape(lin), jnp.transpose(v[0]).reshape(lin)) if transposed else (m[0], v[0])
        res = _adamw_shard("adamw_" + name, own, from_chips[name], w, m0, v0, row_tile(own.shape[0]))
        big_out[name] = [(jnp.transpose(t.reshape(IN_SHARD, d)) if transposed else t)[None] for t in res]

    zero = jnp.zeros((), F32)
    small_w = _pack_small((w_sgu, b_sgu, norm2_g, normf_g, ln_v_g, ln_v_b, b_f, norm1_g, zero))
    small_m = _pack_small((m_w_sgu, m_b_sgu, m_norm2_g, m_normf_g, m_ln_v_g, m_ln_v_b, m_b_f, m_norm1_g, zero))
    small_v = _pack_small((v_w_sgu, v_b_sgu, v_norm2_g, v_normf_g, v_ln_v_g, v_ln_v_b, v_b_f, v_norm1_g, zero))
    small_res = [_unpack_small(t) for t in _adamw_small(small_all, small_w, small_m, small_v)]
    small_names = [n for n, _ in _SMALL]
    small_out = {n: [res[k] for res in small_res] for k, n in enumerate(small_names)}
    loss = small_out["loss"][0]

    order = ("norm1_g", "w_in", "b_f", "ln_v_g", "ln_v_b", "w_sgu", "b_sgu", "w_a", "w_b", "w_o", "norm2_g", "w_up",
             "w_down", "normf_g")
    table = {**big_out, **small_out}
    outs = [loss, grad_x[None]]
    for kind in range(4):
        outs += [table[n][kind] for n in order]
    return tuple(outs)
```

```python
import math

import jax
import jax.numpy as jnp
from jax import lax
from jax.experimental import pallas as pl
from jax.experimental.pallas import tpu as pltpu

F32 = jnp.float32
BF16 = jnp.bfloat16

N_DEV = 8
D_MODEL = 1024
N_HEADS = 8
HEAD_DIM = 64
FOX_W = N_HEADS * HEAD_DIM
SGU_G = 8
SGU_W = 512
SGU_LEN = 128
CHUNK = 64
D_FF = 4 * D_MODEL
IN_COLS = 3 * FOX_W + N_HEADS + 2 * SGU_W + 2 * D_MODEL
IN_SHARD = IN_COLS // N_DEV
LANES = 128
QKV_W = 3 * FOX_W
U_OFF, SV_OFF, GA_OFF, GB_OFF, F_OFF = 0, 512, 1024, 2048, 3072
REST_W = F_OFF + LANES
EPS = 1e-6
NEG = -1e30

ADAM_LR = 0.001
ADAM_B1 = 0.9
ADAM_B2 = 0.999
ADAM_EPS = 1e-08
ADAM_WD = 0.01
ADAM_STEP = 10

VMEM_LIMIT = 56 * 1024 * 1024
MESH = pl.DeviceIdType.MESH


def _params(sem=None):
    return pltpu.CompilerParams(dimension_semantics=sem, vmem_limit_bytes=VMEM_LIMIT)


def _dot(a, b):
    return jnp.dot(a, b, preferred_element_type=F32)


def _dot_nt(a, b):
    return lax.dot_general(a, b, (((1,), (1,)), ((), ())), preferred_element_type=F32)


def _dot_f32(a, b):
    return jnp.dot(a, b, preferred_element_type=F32, precision=lax.Precision.HIGHEST)


def _sigmoid(x):
    return 1.0 / (1.0 + jnp.exp(-x))


def _log_sigmoid(z):
    return jnp.minimum(z, 0.0) - jnp.log(1.0 + jnp.exp(-jnp.abs(z)))


_GELU_K = math.sqrt(2.0 / math.pi)
_GELU_C = 0.044715


def _gelu(x):
    t = jnp.tanh(_GELU_K * (x + _GELU_C * (x * x * x)))
    return 0.5 * x * (1.0 + t)


def _gelu_grad(x):
    x2 = x * x
    t = jnp.tanh(_GELU_K * (x + _GELU_C * (x2 * x)))
    return 0.5 * (1.0 + t) + 0.5 * x * (1.0 - t * t) * (_GELU_K * (1.0 + 3.0 * _GELU_C * x2))


def _rms_bwd(xh, r, g, dy):
    gy = dy * g
    return r * (gy - xh * jnp.mean(xh * gy, axis=-1, keepdims=True))


def _lane_lt64(shape):
    return lax.broadcasted_iota(jnp.int32, shape, len(shape) - 1) < HEAD_DIM


class _Comm:
    def __init__(self, arrays, out_shapes, sems, start, finish, mid=None):
        self.arrays, self.out_shapes, self.sems = list(arrays), list(out_shapes), list(sems)
        self.start, self.mid, self.finish = start, mid, finish


def _comm_phase(plans, phase, in_refs, out_refs, sem_refs):
    ia = io = ks = 0
    for plan in plans:
        na, no, ns = len(plan.arrays), len(plan.out_shapes), len(plan.sems)
        fn = getattr(plan, phase)
        if fn is not None:
            fn(in_refs[ia:ia + na], out_refs[io:io + no], sem_refs[ks:ks + ns])
        ia, io, ks = ia + na, io + no, ks + ns


def _comm_operands(plans):
    arrays = [a for plan in plans for a in plan.arrays]
    out_shapes = [o for plan in plans for o in plan.out_shapes]
    sems = [s for plan in plans for s in plan.sems]
    return arrays, out_shapes, sems


_ANY = pl.BlockSpec(memory_space=pl.ANY)


def _run_comm(name, plans):
    arrays, out_shapes, sems = _comm_operands(plans)
    n_in, n_out = len(arrays), len(out_shapes)

    def body(*refs):
        parts = refs[:n_in], refs[n_in:n_in + n_out], refs[n_in + n_out:]
        for phase in ("start", "mid", "finish"):
            _comm_phase(plans, phase, *parts)

    return pl.pallas_call(
        body, name=name, out_shape=out_shapes, in_specs=[_ANY] * n_in, out_specs=[_ANY] * n_out, scratch_shapes=sems,
    )(*arrays)


def _gather_plan(shard):
    def setup(ins, outs, sems):
        (x_ref,), (out_ref,), (send_sems, recv_sems, local_sem) = ins, outs, sems
        x, y, c = lax.axis_index("x"), lax.axis_index("y"), lax.axis_index("c")
        me, sibling = (x, y, c), (x, y, 1 - c)
        chips = [(1 - x, y), (x, 1 - y), (1 - x, 1 - y)]

        def rows(px, py, pc):
            return out_ref.at[4 * px + 2 * py + pc]

        def copy(k, block, to, src=None):
            return pltpu.make_async_remote_copy(
                src_ref=rows(*block) if src is None else src,
                dst_ref=rows(*block),
                send_sem=send_sems.at[k],
                recv_sem=recv_sems.at[k],
                device_id=to,
                device_id_type=MESH,
            )

        mine = pltpu.make_async_copy(x_ref, rows(*me), local_sem)
        first = [copy(0, me, sibling, src=x_ref)]
        first += [copy(1 + j, me, (*chip, c), src=x_ref) for j, chip in enumerate(chips)]
        passed = [copy(4 + j, (*chip, c), sibling) for j, chip in enumerate(chips)]
        landed = [copy(1 + j, (*chip, c), me) for j, chip in enumerate(chips)]
        from_sibling = [copy(0, sibling, me)] + [copy(4 + j, (*chip, 1 - c), me) for j, chip in enumerate(chips)]
        return mine, first, passed, landed, from_sibling

    def start(ins, outs, sems):
        mine, first, _, _, _ = setup(ins, outs, sems)
        mine.start()
        for cp in first:
            cp.start()

    def mid(ins, outs, sems):
        _, _, passed, landed, _ = setup(ins, outs, sems)
        for arrived, onward in zip(landed, passed):
            arrived.wait_recv()
            onward.start()

    def finish(ins, outs, sems):
        mine, first, passed, _, from_sibling = setup(ins, outs, sems)
        for cp in from_sibling:
            cp.wait_recv()
        for cp in first + passed:
            cp.wait_send()
        mine.wait()

    return _Comm([shard], [jax.ShapeDtypeStruct((N_DEV,) + shard.shape, shard.dtype)],
                 [pltpu.SemaphoreType.DMA((7,)), pltpu.SemaphoreType.DMA((7,)), pltpu.SemaphoreType.DMA],
                 start, finish, mid)


def _start_all(copies):
    for cp in copies:
        cp.start()


def _wait_all(copies):
    for cp in copies:
        cp.wait_recv()
    for cp in copies:
        cp.wait_send()


def _pair_exchange_plan(grads):
    n = len(grads)

    def copies(ins, outs, sems):
        send_sems, recv_sems = sems
        x, y, c = lax.axis_index("x"), lax.axis_index("y"), lax.axis_index("c")
        return [
            pltpu.make_async_remote_copy(
                src_ref=ins[k].at[:, 1 - c],
                dst_ref=outs[k],
                send_sem=send_sems.at[k],
                recv_sem=recv_sems.at[k],
                device_id=(x, y, 1 - c),
                device_id_type=MESH,
            )
            for k in range(n)
        ]

    return _Comm(grads, [jax.ShapeDtypeStruct((4,) + g.shape[2:], g.dtype) for g in grads],
                 [pltpu.SemaphoreType.DMA((n,)), pltpu.SemaphoreType.DMA((n,))],
                 lambda *refs: _start_all(copies(*refs)), lambda *refs: _wait_all(copies(*refs)))


def _chip_exchange_plan(parts):
    n = len(parts)

    def copies(ins, outs, sems):
        send_sems, recv_sems = sems
        x, y, c = lax.axis_index("x"), lax.axis_index("y"), lax.axis_index("c")
        chips = [(1 - x, y), (x, 1 - y), (1 - x, 1 - y)]
        return [
            pltpu.make_async_remote_copy(
                src_ref=ins[k].at[2 * px + py],
                dst_ref=outs[k].at[j],
                send_sem=send_sems.at[3 * k + j],
                recv_sem=recv_sems.at[3 * k + j],
                device_id=(px, py, c),
                device_id_type=MESH,
            )
            for k in range(n) for j, (px, py) in enumerate(chips)
        ]

    return _Comm(parts, [jax.ShapeDtypeStruct((3,) + p.shape[1:], p.dtype) for p in parts],
                 [pltpu.SemaphoreType.DMA((3 * n,)), pltpu.SemaphoreType.DMA((3 * n,))],
                 lambda *refs: _start_all(copies(*refs)), lambda *refs: _wait_all(copies(*refs)))


def _mm(name, pairs, extras, outs, epi, *, m, tm, n, tn, arbitrary=False, comm=()):
    nj = n // tn
    a_arrays, a_specs, b_arrays, b_specs, b_index = [], [], [], [], []
    for a, b, nt, cols in pairs:
        a_arrays.append(a)
        a_specs.append(pl.BlockSpec((tm, a.shape[1]), lambda i, j: (i, 0)))
        known = [k for k, other in enumerate(b_arrays) if other is b]
        if known:
            b_index.append(known[0])
            continue
        b_index.append(len(b_arrays))
        b_arrays.append(b)
        if cols is not None:
            assert nj == 1
            b_specs.append(pl.BlockSpec(b.shape, lambda i, j: (0, 0)))
        elif nt:
            b_specs.append(pl.BlockSpec((tn, b.shape[1]), lambda i, j: (j, 0)))
        else:
            b_specs.append(pl.BlockSpec((b.shape[0], tn), lambda i, j: (0, j)))
    comm_arrays, comm_outs, comm_sems = _comm_operands(comm)
    arrays = a_arrays + b_arrays + [arr for arr, _ in extras] + comm_arrays
    in_specs = a_specs + b_specs + [spec for _, spec in extras] + [_ANY] * len(comm_arrays)
    n_a, n_b, n_extras, n_ci, n_out, n_co = len(a_arrays), len(b_arrays), len(extras), len(comm_arrays), len(outs), len(comm_outs)
    ni = m // tm

    def body(*refs):
        a_refs = refs[:n_a]
        b_refs = refs[n_a:n_a + n_b]
        ex = refs[n_a + n_b:n_a + n_b + n_extras]
        n_in = n_a + n_b + n_extras + n_ci
        comm_refs = refs[n_in - n_ci:n_in], refs[n_in + n_out:n_in + n_out + n_co], refs[n_in + n_out + n_co:]
        out = refs[n_in:n_in + n_out]
        if comm:
            @pl.when(jnp.logical_and(pl.program_id(0) == 0, pl.program_id(1) == 0))
            def _():
                _comm_phase(comm, "start", *comm_refs)

        accs = []
        for p, (_, _, nt, cols) in enumerate(pairs):
            av = a_refs[p][...]
            if av.dtype != BF16:
                av = av.astype(BF16)
            b_ref = b_refs[b_index[p]]
            if cols is None:
                bv = b_ref[...]
            else:
                bv = b_ref[:, cols[0]:cols[1]] if nt else b_ref[cols[0]:cols[1], :]
            accs.append(_dot_nt(av, bv) if nt else _dot(av, bv))
        epi(accs, ex, out)
        if comm:
            mid_row = ni // 2 if ni >= 3 else ni - 1
            mid_col = 0 if ni >= 3 else nj - 1

            @pl.when(jnp.logical_and(pl.program_id(0) == mid_row, pl.program_id(1) == mid_col))
            def _():
                _comm_phase(comm, "mid", *comm_refs)

            @pl.when(jnp.logical_and(pl.program_id(0) == ni - 1, pl.program_id(1) == nj - 1))
            def _():
                _comm_phase(comm, "finish", *comm_refs)

    sem = ("arbitrary", "arbitrary") if arbitrary or comm else ("parallel", "parallel")
    res = pl.pallas_call(
        body,
        name=name,
        grid=(ni, nj),
        in_specs=in_specs,
        out_specs=[spec for _, spec in outs] + [_ANY] * n_co,
        out_shape=[shape for shape, _ in outs] + comm_outs,
        scratch_shapes=comm_sems,
        compiler_params=_params(sem),
    )(*arrays)
    return (res[:n_out], res[n_out:]) if comm else res


def _tile(tm, tn, off=0):
    return pl.BlockSpec((tm, tn), lambda i, j: (i, j + off))


def _row(tm, w, blk=0):
    return pl.BlockSpec((tm, w), lambda i, j: (i, blk))


def _whole(shape):
    zeros = (0,) * len(shape)
    return pl.BlockSpec(shape, lambda i, j: zeros)


def _sds(shape, dtype):
    return jax.ShapeDtypeStruct(shape, dtype)


def _tile_t(tm, tn):
    return pl.BlockSpec((tn, tm), lambda i, j: (j, i))


def _grad_w(name, a_t, g, *, tk, tn, ts, block_cols=None):
    ka, s_len = a_t.shape
    n = g.shape[1]
    width = tn if block_cols is None else block_cols

    def body(a_ref, g_ref, o_ref):
        first = pl.program_id(2) == 0
        gv = g_ref[...].astype(BF16)
        for b in range(tn // width):
            part = _dot(a_ref[...], gv[:, b * width:(b + 1) * width])
            dst = o_ref if block_cols is None else o_ref.at[b]

            @pl.when(first)
            def _():
                dst[...] = part

            @pl.when(jnp.logical_not(first))
            def _():
                dst[...] += part

    if block_cols is None:
        out_shape = _sds((ka, n), F32)
        out_spec = pl.BlockSpec((tk, tn), lambda i, j, s: (i, j))
    else:
        out_shape = _sds((n // width, ka, width), F32)
        out_spec = pl.BlockSpec((tn // width, tk, width), lambda i, j, s: (j, i, 0))
    return pl.pallas_call(
        body,
        name=name,
        grid=(ka // tk, n // tn, s_len // ts),
        in_specs=[pl.BlockSpec((tk, ts), lambda i, j, s: (i, s)), pl.BlockSpec((ts, tn), lambda i, j, s: (s, j))],
        out_specs=out_spec,
        out_shape=out_shape,
        compiler_params=_params(("parallel", "parallel", "arbitrary")),
    )(a_t, g)


def _rms_fwd(name, x, g, tm, comm=()):
    s_len, d = x.shape
    steps = s_len // tm
    comm_arrays, comm_outs, comm_sems = _comm_operands(comm)
    n_ci, n_co = len(comm_arrays), len(comm_outs)

    def body(x_ref, g_ref, *rest):
        comm_refs = rest[:n_ci], rest[n_ci + 3:n_ci + 3 + n_co], rest[n_ci + 3 + n_co:]
        h_ref, ht_ref, r_ref = rest[n_ci:n_ci + 3]
        for phase, at in (("start", 0), ("mid", steps // 2)):
            if comm:
                @pl.when(pl.program_id(0) == at)
                def _():
                    _comm_phase(comm, phase, *comm_refs)

        xv = x_ref[...]
        r = lax.rsqrt(jnp.mean(xv * xv, axis=-1, keepdims=True) + EPS)
        h = (xv * r * g_ref[...]).astype(BF16)
        h_ref[...] = h
        ht_ref[...] = jnp.transpose(h)
        r_ref[...] = r
        if comm:
            @pl.when(pl.program_id(0) == steps - 1)
            def _():
                _comm_phase(comm, "finish", *comm_refs)

    res = pl.pallas_call(
        body,
        name=name,
        grid=(steps,),
        in_specs=[pl.BlockSpec((tm, d), lambda i: (i, 0)), pl.BlockSpec((1, d), lambda i: (0, 0))] + [_ANY] * n_ci,
        out_specs=[pl.BlockSpec((tm, d), lambda i: (i, 0)), pl.BlockSpec((d, tm), lambda i: (0, i)),
                   pl.BlockSpec((tm, 1), lambda i: (i, 0))] + [_ANY] * n_co,
        out_shape=[_sds((s_len, d), BF16), _sds((d, s_len), BF16), _sds((s_len, 1), F32)] + comm_outs,
        scratch_shapes=comm_sems,
        compiler_params=_params(("arbitrary",) if comm else ("parallel",)),
    )(x, g, *comm_arrays)
    return res[0], res[1], res[2], res[3:]


def _forget_cumsum(prest, b_f_pad, tc):
    s_len = prest.shape[0]

    def body(f_ref, b_ref, c_ref, carry):
        @pl.when(pl.program_id(0) == 0)
        def _():
            carry[...] = jnp.zeros_like(carry)

        logf = _log_sigmoid(f_ref[...] + b_ref[...])
        row = lax.broadcasted_iota(jnp.int32, (tc, tc), 0)
        col = lax.broadcasted_iota(jnp.int32, (tc, tc), 1)
        tri = (row >= col).astype(F32)
        c = _dot_f32(tri, logf) + carry[...]
        c_ref[...] = c
        carry[...] = c[tc - 1:tc, :]

    return pl.pallas_call(
        body,
        name="forget_cumsum",
        grid=(s_len // tc,),
        in_specs=[pl.BlockSpec((tc, LANES), lambda i: (i, 0)), pl.BlockSpec((1, LANES), lambda i: (0, 0))],
        out_specs=pl.BlockSpec((tc, LANES), lambda i: (i, 0)),
        out_shape=_sds((s_len, LANES), F32),
        scratch_shapes=[pltpu.VMEM((1, LANES), F32)],
        compiler_params=_params(("arbitrary",)),
    )(prest, b_f_pad)


def _stack_heads(pair, lt64):
    zero = jnp.zeros_like(pair)
    return jnp.concatenate([jnp.where(lt64, pair, zero), jnp.where(lt64, zero, pair)], axis=0)


def _score_tiles(q_ref, k_ref, ck_ref, st_sc, tk):
    lt64 = _lane_lt64((tk, LANES))
    for p in range(N_HEADS // 2):
        lanes = slice(p * LANES, (p + 1) * LANES)
        q_pair = q_ref[:, lanes] * jnp.asarray(HEAD_DIM ** -0.5, BF16)
        st2 = _dot_nt(_stack_heads(k_ref[:, lanes], lt64), q_pair)
        for half in range(2):
            h = 2 * p + half
            st_sc[h] = st2[half * tk:(half + 1) * tk] - ck_ref[:, h:h + 1]


ROW_CHUNK = 64


def _row_chunks(tk):
    rc = min(ROW_CHUNK, tk)
    return [slice(r, r + rc) for r in range(0, tk, rc)]


def _by_sublane(x):
    return x.reshape(x.shape[0] // 8, 8, x.shape[1])


def _softmax_update(st_sc, p_sc, m_sc, l_sc, tk, tq):
    alphas = []
    for h in range(N_HEADS):
        top8 = jnp.full((8, tq), NEG, F32)
        for rows in _row_chunks(tk):
            top8 = jnp.maximum(top8, jnp.max(_by_sublane(st_sc[h, rows, :]), axis=0))
        m_old = m_sc[h]
        m_new = jnp.maximum(m_old, jnp.max(top8, axis=0, keepdims=True))
        sum8 = jnp.zeros((8, tq), F32)
        for rows in _row_chunks(tk):
            pt = jnp.exp(st_sc[h, rows, :] - m_new)
            p_sc[h, rows, :] = pt.astype(BF16)
            sum8 = sum8 + jnp.sum(_by_sublane(pt), axis=0)
        alpha = jnp.exp(m_old - m_new)
        l_sc[h] = alpha * l_sc[h] + jnp.sum(sum8, axis=0, keepdims=True)
        m_sc[h] = m_new
        alphas.append(alpha)
    return alphas


def _mask_diagonal(st_sc, i, j, tq, tk):
    @pl.when((j + 1) * tk - 1 > i * tq)
    def _():
        key = j * tk + lax.broadcasted_iota(jnp.int32, (tk, tq), 0)
        query = i * tq + lax.broadcasted_iota(jnp.int32, (tk, tq), 1)
        st_sc[...] = jnp.where((query >= key)[None], st_sc[...], NEG)


def _attn_fwd(qkv, v_t, c_col, tq, tk, comm=()):
    s_len = qkv.shape[0]
    ratio = tq // tk
    steps = [(i, j) for i in range(s_len // tq) for j in range((i + 1) * ratio)]
    i_tab = jnp.asarray([i for i, _ in steps], jnp.int32)
    j_tab = jnp.asarray([j for _, j in steps], jnp.int32)

    comm_arrays, comm_outs, comm_sems = _comm_operands(comm)
    n_ci, n_co = len(comm_arrays), len(comm_outs)

    def body(i_ref, j_ref, q_ref, k_ref, vt_ref, ck_ref, *rest):
        comm_refs = rest[:n_ci], rest[n_ci + 3:n_ci + 3 + n_co], rest[n_ci + 3 + n_co + 5:]
        o_ref, ot_ref, lse_ref = rest[n_ci:n_ci + 3]
        acc_t, m_sc, l_sc, st_sc, p_sc = rest[n_ci + 3 + n_co:n_ci + 3 + n_co + 5]
        n = pl.program_id(0)
        i, j = i_ref[n], j_ref[n]
        for phase, at in (("start", 0), ("mid", (2 * len(steps)) // 3)):
            if comm:
                @pl.when(n == at)
                def _():
                    _comm_phase(comm, phase, *comm_refs)

        @pl.when(j == 0)
        def _():
            acc_t[...] = jnp.zeros_like(acc_t)
            m_sc[...] = jnp.full_like(m_sc, NEG)
            l_sc[...] = jnp.zeros_like(l_sc)

        _score_tiles(q_ref, k_ref, ck_ref, st_sc, tk)
        _mask_diagonal(st_sc, i, j, tq, tk)
        alpha = _softmax_update(st_sc, p_sc, m_sc, l_sc, tk, tq)
        top = lax.broadcasted_iota(jnp.int32, (LANES, tq), 0) < HEAD_DIM
        for p in range(N_HEADS // 2):
            lanes = slice(p * LANES, (p + 1) * LANES)
            vt_pair = vt_ref[lanes, :]
            pv = jnp.where(top, _dot(vt_pair, p_sc[2 * p]), _dot(vt_pair, p_sc[2 * p + 1]))
            acc_t[lanes, :] = acc_t[lanes, :] * jnp.where(top, alpha[2 * p], alpha[2 * p + 1]) + pv

        @pl.when(j == (i + 1) * ratio - 1)
        def _():
            for p in range(N_HEADS // 2):
                lanes = slice(p * LANES, (p + 1) * LANES)
                l_pair = jnp.where(top, l_sc[2 * p], l_sc[2 * p + 1])
                o_t = acc_t[lanes, :] / l_pair
                o_ref[:, lanes] = jnp.transpose(o_t)
                ot_ref[lanes, :] = o_t.astype(BF16)
            lse_ref[...] = m_sc[...] + jnp.log(l_sc[...])

        if comm:
            @pl.when(n == len(steps) - 1)
            def _():
                _comm_phase(comm, "finish", *comm_refs)

    stat = pltpu.VMEM((N_HEADS, 1, tq), F32)
    res = pl.pallas_call(
        body,
        name="attn_fwd",
        grid_spec=pltpu.PrefetchScalarGridSpec(
            num_scalar_prefetch=2,
            grid=(len(steps),),
            in_specs=[
                pl.BlockSpec((tq, FOX_W), lambda n, it, jt: (it[n], 0)),
                pl.BlockSpec((tk, FOX_W), lambda n, it, jt: (jt[n], 1)),
                pl.BlockSpec((FOX_W, tk), lambda n, it, jt: (0, jt[n])),
                pl.BlockSpec((tk, LANES), lambda n, it, jt: (jt[n], 0)),
            ] + [_ANY] * n_ci,
            out_specs=[
                pl.BlockSpec((tq, FOX_W), lambda n, it, jt: (it[n], 0)),
                pl.BlockSpec((FOX_W, tq), lambda n, it, jt: (0, it[n])),
                pl.BlockSpec((N_HEADS, 1, tq), lambda n, it, jt: (0, 0, it[n])),
            ] + [_ANY] * n_co,
            scratch_shapes=[pltpu.VMEM((FOX_W, tq), F32), stat, stat, pltpu.VMEM((N_HEADS, tk, tq), F32),
                            pltpu.VMEM((N_HEADS, tk, tq), BF16)] + comm_sems,
        ),
        out_shape=[_sds((s_len, FOX_W), F32), _sds((FOX_W, s_len), BF16), _sds((N_HEADS, 1, s_len), F32)] + comm_outs,
        compiler_params=_params(("arbitrary",)),
    )(i_tab, j_tab, qkv, qkv, v_t, c_col, *comm_arrays)
    return res[0], res[1], res[2], res[3:]


def _sgu_mix(vn, w_stack, lt64):
    outs = []
    for p in range(SGU_G // 2):
        r = _dot(w_stack[p], vn[:, p * LANES:(p + 1) * LANES])
        outs.append(jnp.where(lt64, r[:SGU_LEN], r[SGU_LEN:]))
    return jnp.concatenate(outs, axis=1)


def _sgu_norm(sv, ln_g, ln_b):
    svg = _gelu(sv)
    xc = svg - jnp.mean(svg, axis=-1, keepdims=True)
    rstd = lax.rsqrt(jnp.mean(xc * xc, axis=-1, keepdims=True) + EPS)
    xhat = xc * rstd
    return xhat, rstd, xhat * ln_g + ln_b


def _sgu_fwd(prest, ln_g, ln_b, w_stack, b_pair, tm):
    s_len = prest.shape[0]

    def body(u_ref, sv_ref, g_ref, b_ref, w_ref, bp_ref, sg_ref, sgt_ref):
        lt64 = _lane_lt64((SGU_LEN, LANES))
        _, _, vn = _sgu_norm(sv_ref[...].astype(F32), g_ref[...], b_ref[...])
        vn = vn.astype(BF16)
        w_stack_v = [w_ref[p] for p in range(SGU_G // 2)]
        for w in range(tm // SGU_LEN):
            win = slice(w * SGU_LEN, (w + 1) * SGU_LEN)
            mixed = _sgu_mix(vn[win], w_stack_v, lt64) + bp_ref[...]
            sg = (_gelu(u_ref[win, :].astype(F32)) * mixed).astype(BF16)
            sg_ref[win, :] = sg
            sgt_ref[:, win] = jnp.transpose(sg)

    return pl.pallas_call(
        body,
        name="sgu_fwd",
        grid=(s_len // tm,),
        in_specs=[
            pl.BlockSpec((tm, SGU_W), lambda i: (i, U_OFF // SGU_W)),
            pl.BlockSpec((tm, SGU_W), lambda i: (i, SV_OFF // SGU_W)),
            pl.BlockSpec((1, SGU_W), lambda i: (0, 0)),
            pl.BlockSpec((1, SGU_W), lambda i: (0, 0)),
            pl.BlockSpec((SGU_G // 2, 2 * SGU_LEN, SGU_LEN), lambda i: (0, 0, 0)),
            pl.BlockSpec((SGU_LEN, SGU_W), lambda i: (0, 0)),
        ],
        out_specs=[pl.BlockSpec((tm, SGU_W), lambda i: (i, 0)), pl.BlockSpec((SGU_W, tm), lambda i: (0, i))],
        out_shape=[_sds((s_len, SGU_W), BF16), _sds((SGU_W, s_len), BF16)],
        compiler_params=_params(("parallel",)),
    )(prest, prest, ln_g, ln_b, w_stack, b_pair)


def _sgu_bwd(prest, dsg, ln_g, ln_b, w_stack, wt_stack, b_pair, tm):
    s_len = prest.shape[0]
    n_pair = SGU_G // 2

    def body(u_ref, sv_ref, dsg_ref, g_ref, b_ref, w_ref, wt_ref, bp_ref,
             du_ref, dsv_ref, dw_ref, db_ref, dg_ref, dbeta_ref, dut_ref, dsvt_ref, dvn_sc):
        @pl.when(pl.program_id(0) == 0)
        def _():
            dw_ref[...] = jnp.zeros_like(dw_ref)
            db_ref[...] = jnp.zeros_like(db_ref)
            dg_ref[...] = jnp.zeros_like(dg_ref)
            dbeta_ref[...] = jnp.zeros_like(dbeta_ref)

        lt64 = _lane_lt64((SGU_LEN, LANES))
        sv = sv_ref[...].astype(F32)
        xhat, rstd, vn32 = _sgu_norm(sv, g_ref[...], b_ref[...])
        vn = vn32.astype(BF16)
        w_stack_v = [w_ref[p] for p in range(n_pair)]
        db = jnp.zeros((SGU_LEN, SGU_W), F32)
        for w in range(tm // SGU_LEN):
            win = slice(w * SGU_LEN, (w + 1) * SGU_LEN)
            u = u_ref[win, :].astype(F32)
            dsg_w = dsg_ref[win, :]
            mixed = _sgu_mix(vn[win], w_stack_v, lt64) + bp_ref[...]
            du = (dsg_w * mixed * _gelu_grad(u)).astype(BF16)
            du_ref[win, :] = du
            dut_ref[:, win] = jnp.transpose(du)
            dmixed = dsg_w * _gelu(u)
            db = db + dmixed
            dm16 = dmixed.astype(BF16)
            for p in range(n_pair):
                lanes = slice(p * LANES, (p + 1) * LANES)
                dmp = dm16[:, lanes]
                r = _dot(wt_ref[p], dmp)
                dvn_sc[win, lanes] = jnp.where(lt64, r[:SGU_LEN], r[SGU_LEN:])
                zero = jnp.zeros_like(dmp)
                dm_ab = jnp.concatenate([jnp.where(lt64, dmp, zero), jnp.where(lt64, zero, dmp)], axis=0)
                dw_ref[p] += _dot_nt(dm_ab, vn[win, lanes])
        db_ref[...] += db
        dvn = dvn_sc[...]
        dg_ref[...] += jnp.sum(dvn * xhat, axis=0, keepdims=True)
        dbeta_ref[...] += jnp.sum(dvn, axis=0, keepdims=True)
        dxh = dvn * g_ref[...]
        dsvg = rstd * (dxh - jnp.mean(dxh, axis=-1, keepdims=True) - xhat * jnp.mean(dxh * xhat, axis=-1, keepdims=True))
        dsv = (dsvg * _gelu_grad(sv)).astype(BF16)
        dsv_ref[...] = dsv
        dsvt_ref[...] = jnp.transpose(dsv)

    const2 = lambda i: (0, 0)
    const3 = lambda i: (0, 0, 0)
    return pl.pallas_call(
        body,
        name="sgu_bwd",
        grid=(s_len // tm,),
        in_specs=[
            pl.BlockSpec((tm, SGU_W), lambda i: (i, U_OFF // SGU_W)),
            pl.BlockSpec((tm, SGU_W), lambda i: (i, SV_OFF // SGU_W)),
            pl.BlockSpec((tm, SGU_W), lambda i: (i, 0)),
            pl.BlockSpec((1, SGU_W), const2),
            pl.BlockSpec((1, SGU_W), const2),
            pl.BlockSpec((n_pair, 2 * SGU_LEN, SGU_LEN), const3),
            pl.BlockSpec((n_pair, 2 * SGU_LEN, SGU_LEN), const3),
            pl.BlockSpec((SGU_LEN, SGU_W), const2),
        ],
        out_specs=[
            pl.BlockSpec((tm, SGU_W), lambda i: (i, 0)),
            pl.BlockSpec((tm, SGU_W), lambda i: (i, 0)),
            pl.BlockSpec((n_pair, 2 * SGU_LEN, SGU_LEN), const3),
            pl.BlockSpec((SGU_LEN, SGU_W), const2),
            pl.BlockSpec((1, SGU_W), const2),
            pl.BlockSpec((1, SGU_W), const2),
            pl.BlockSpec((SGU_W, tm), lambda i: (0, i)),
            pl.BlockSpec((SGU_W, tm), lambda i: (0, i)),
        ],
        out_shape=[
            _sds((s_len, SGU_W), BF16), _sds((s_len, SGU_W), BF16), _sds((n_pair, 2 * SGU_LEN, SGU_LEN), F32),
            _sds((SGU_LEN, SGU_W), F32), _sds((1, SGU_W), F32), _sds((1, SGU_W), F32),
            _sds((SGU_W, s_len), BF16), _sds((SGU_W, s_len), BF16),
        ],
        scratch_shapes=[pltpu.VMEM((tm, SGU_W), F32)],
        compiler_params=_params(("arbitrary",)),
    )(prest, prest, dsg, ln_g, ln_b, w_stack, wt_stack, b_pair)


def _attn_bwd(qkv, k_t, do, c_col, lse_row, delta_row, tq, tk, comm=()):
    s_len = qkv.shape[0]
    nq, nk = s_len // tq, s_len // tk
    ratio = tq // tk
    scale = HEAD_DIM ** -0.5
    steps = [(j, i) for j in range(nk) for i in range(j // ratio, nq)]
    j_tab = jnp.asarray([j for j, _ in steps], jnp.int32)
    i_tab = jnp.asarray([i for _, i in steps], jnp.int32)

    comm_arrays, comm_outs, comm_sems = _comm_operands(comm)
    n_ci, n_co = len(comm_arrays), len(comm_outs)

    def body(j_ref, i_ref, q_ref, k_ref, v_ref, kt_ref, do_ref, ck_ref, lse_ref, dl_ref, *rest):
        comm_refs = rest[:n_ci], rest[n_ci + 7:n_ci + 7 + n_co], rest[n_ci + 7 + n_co + 8:]
        dq_ref, dk_ref, dv_ref, dcr_ref, dcc_ref, dkt_ref, dvt_ref = rest[n_ci:n_ci + 7]
        dq_t, dk_acc, dv_acc, dcc_acc, st_sc, dpt_sc, p_sc, ds_sc = rest[n_ci + 7 + n_co:n_ci + 7 + n_co + 8]
        n = pl.program_id(0)
        j, i = j_ref[n], i_ref[n]

        @pl.when(n == 0)
        def _():
            _comm_phase(comm, "start", *comm_refs)
            dq_t[...] = jnp.zeros_like(dq_t)
            dcr_ref[...] = jnp.zeros_like(dcr_ref)

        @pl.when(i == j // ratio)
        def _():
            dk_acc[...] = jnp.zeros_like(dk_acc)
            dv_acc[...] = jnp.zeros_like(dv_acc)
            dcc_acc[...] = jnp.zeros_like(dcc_acc)

        lt64 = _lane_lt64((tk, LANES))
        _score_tiles(q_ref, k_ref, ck_ref, st_sc, tk)
        for p in range(N_HEADS // 2):
            lanes = slice(p * LANES, (p + 1) * LANES)
            dpt2 = _dot_nt(_stack_heads(v_ref[:, lanes], lt64), do_ref[:, lanes].astype(BF16))
            dpt_sc[2 * p] = dpt2[:tk]
            dpt_sc[2 * p + 1] = dpt2[tk:]
        _mask_diagonal(st_sc, i, j, tq, tk)

        pt = jnp.exp(st_sc[...] - lse_ref[...])
        dst = pt * (dpt_sc[...] - dl_ref[...])
        p_sc[...] = pt.astype(BF16)
        ds_sc[...] = dst.astype(BF16)
        dcr_ref[i] += jnp.sum(dst, axis=1, keepdims=True)
        col_sums = jnp.sum(dst, axis=2, keepdims=True)
        lane = lax.broadcasted_iota(jnp.int32, (tk, LANES), 1)
        dcc = jnp.zeros((tk, LANES), F32)
        for h in range(N_HEADS):
            dcc = jnp.where(lane == h, -col_sums[h], dcc)
        dcc_acc[...] += dcc

        for p in range(N_HEADS // 2):
            lanes = slice(p * LANES, (p + 1) * LANES)
            q_pair = q_ref[:, lanes] * jnp.asarray(scale, BF16)
            dv2 = _dot(p_sc[2 * p:2 * p + 2].reshape(2 * tk, tq), do_ref[:, lanes].astype(BF16))
            dv_acc[:, lanes] += jnp.where(lt64, dv2[:tk], dv2[tk:])
            dk2 = _dot(ds_sc[2 * p:2 * p + 2].reshape(2 * tk, tq), q_pair)
            dk_acc[:, lanes] += jnp.where(lt64, dk2[:tk], dk2[tk:])
            dq2 = _dot(kt_ref[lanes, :], jnp.concatenate([ds_sc[2 * p], ds_sc[2 * p + 1]], axis=1))
            top = lax.broadcasted_iota(jnp.int32, (LANES, tq), 0) < HEAD_DIM
            dq_t[i, lanes, :] += jnp.where(top, dq2[:, :tq], dq2[:, tq:])

        @pl.when(j == (i + 1) * ratio - 1)
        def _():
            rows = pl.ds(pl.multiple_of(i * tq, tq), tq)
            for p in range(N_HEADS // 2):
                lanes = slice(p * LANES, (p + 1) * LANES)
                dq_ref[rows, lanes] = (jnp.transpose(dq_t[i, lanes, :]) * scale).astype(BF16)

        @pl.when(i == nq - 1)
        def _():
            dk16, dv16 = dk_acc[...].astype(BF16), dv_acc[...].astype(BF16)
            dk_ref[...] = dk16
            dv_ref[...] = dv16
            dkt_ref[...] = jnp.transpose(dk16)
            dvt_ref[...] = jnp.transpose(dv16)
            dcc_ref[...] = dcc_acc[...]

        if comm:
            @pl.when(n == len(steps) // 2)
            def _():
                _comm_phase(comm, "mid", *comm_refs)

            @pl.when(n == len(steps) - 1)
            def _():
                _comm_phase(comm, "finish", *comm_refs)

    q_map = lambda n, jt, it: (it[n], 0)
    q_stat = lambda n, jt, it: (0, 0, it[n])
    k_map = lambda n, jt, it: (jt[n], 0)
    tile = (N_HEADS, tk, tq)
    res = pl.pallas_call(
        body,
        name="attn_bwd",
        grid_spec=pltpu.PrefetchScalarGridSpec(
            num_scalar_prefetch=2,
            grid=(len(steps),),
            in_specs=[
                pl.BlockSpec((tq, FOX_W), q_map),
                pl.BlockSpec((tk, FOX_W), lambda n, jt, it: (jt[n], 1)),
                pl.BlockSpec((tk, FOX_W), lambda n, jt, it: (jt[n], 2)),
                pl.BlockSpec((FOX_W, tk), lambda n, jt, it: (0, jt[n])),
                pl.BlockSpec((tq, FOX_W), q_map),
                pl.BlockSpec((tk, LANES), k_map),
                pl.BlockSpec((N_HEADS, 1, tq), q_stat),
                pl.BlockSpec((N_HEADS, 1, tq), q_stat),
            ] + [_ANY] * n_ci,
            out_specs=[
                pl.BlockSpec((s_len, FOX_W), lambda n, jt, it: (0, 0)),
                pl.BlockSpec((tk, FOX_W), k_map),
                pl.BlockSpec((tk, FOX_W), k_map),
                pl.BlockSpec((nq, N_HEADS, 1, tq), lambda n, jt, it: (0, 0, 0, 0)),
                pl.BlockSpec((tk, LANES), k_map),
                pl.BlockSpec((FOX_W, tk), lambda n, jt, it: (0, jt[n])),
                pl.BlockSpec((FOX_W, tk), lambda n, jt, it: (0, jt[n])),
            ] + [_ANY] * n_co,
            scratch_shapes=[pltpu.VMEM((nq, FOX_W, tq), F32), pltpu.VMEM((tk, FOX_W), F32), pltpu.VMEM((tk, FOX_W), F32),
                            pltpu.VMEM((tk, LANES), F32), pltpu.VMEM(tile, F32), pltpu.VMEM(tile, F32),
                            pltpu.VMEM(tile, BF16), pltpu.VMEM(tile, BF16)] + comm_sems,
        ),
        out_shape=[_sds((s_len, FOX_W), BF16), _sds((s_len, FOX_W), BF16), _sds((s_len, FOX_W), BF16),
                   _sds((nq, N_HEADS, 1, tq), F32), _sds((s_len, LANES), F32),
                   _sds((FOX_W, s_len), BF16), _sds((FOX_W, s_len), BF16)] + comm_outs,
        compiler_params=_params(("arbitrary",)),
    )(j_tab, i_tab, qkv, qkv, qkv, k_t, do, c_col, lse_row, delta_row, *comm_arrays)
    return res[:7], res[7:]


def _forget_bwd(dc_rows, dc_cols, prest, b_f_pad, tc):
    s_len = dc_rows.shape[0]
    nb = s_len // tc

    def body(dcr_ref, dc_ref, f_ref, b_ref, df_ref, db_ref, dft_ref, carry):
        @pl.when(pl.program_id(0) == 0)
        def _():
            carry[...] = jnp.zeros_like(carry)
            db_ref[...] = jnp.zeros_like(db_ref)

        row = lax.broadcasted_iota(jnp.int32, (tc, tc), 0)
        col = lax.broadcasted_iota(jnp.int32, (tc, tc), 1)
        tri = (row <= col).astype(F32)
        dlogf = _dot_f32(tri, dcr_ref[...] + dc_ref[...]) + carry[...]
        carry[...] = dlogf[0:1, :]
        z = f_ref[...] + b_ref[...]
        lane = lax.broadcasted_iota(jnp.int32, (tc, LANES), 1)
        dz = jnp.where(lane < N_HEADS, dlogf * _sigmoid(-z), 0.0)
        df_ref[...] = dz.astype(BF16)
        dft_ref[...] = jnp.transpose(dz).astype(BF16)
        db_ref[...] += jnp.sum(dz, axis=0, keepdims=True)

    rev = lambda i: (nb - 1 - i, 0)
    return pl.pallas_call(
        body,
        name="forget_bwd",
        grid=(nb,),
        in_specs=[
            pl.BlockSpec((tc, LANES), rev),
            pl.BlockSpec((tc, LANES), rev),
            pl.BlockSpec((tc, LANES), rev),
            pl.BlockSpec((1, LANES), lambda i: (0, 0)),
        ],
        out_specs=[pl.BlockSpec((tc, LANES), rev), pl.BlockSpec((1, LANES), lambda i: (0, 0)),
                   pl.BlockSpec((LANES, tc), lambda i: (0, nb - 1 - i))],
        out_shape=[_sds((s_len, LANES), BF16), _sds((1, LANES), F32), _sds((LANES, s_len), BF16)],
        scratch_shapes=[pltpu.VMEM((1, LANES), F32)],
        compiler_params=_params(("arbitrary",)),
    )(dc_rows, dc_cols, prest, b_f_pad)


def _pair_sum(name, g4, recv, idx, tr):
    _, _, r, c = g4.shape

    def body(idx_ref, g_ref, r_ref, p16_ref, own_ref):
        k = pl.program_id(1)
        s = g_ref[...] + r_ref[...]
        p16_ref[...] = s.astype(BF16)

        @pl.when(k == idx_ref[1])
        def _():
            own_ref[...] = s

    return pl.pallas_call(
        body,
        name=name,
        grid_spec=pltpu.PrefetchScalarGridSpec(
            num_scalar_prefetch=1,
            grid=(r // tr, 4),
            in_specs=[
                pl.BlockSpec((None, None, tr, c), lambda i, k, idx: (k, idx[0], i, 0)),
                pl.BlockSpec((None, tr, c), lambda i, k, idx: (k, i, 0)),
            ],
            out_specs=[
                pl.BlockSpec((None, tr, c), lambda i, k, idx: (k, i, 0)),
                pl.BlockSpec((tr, c), lambda i, k, idx: (i, 0)),
            ],
        ),
        out_shape=[_sds((4, r, c), BF16), _sds((r, c), F32)],
        compiler_params=_params(("parallel", "arbitrary")),
    )(idx, g4, recv)


def _adamw_math(w, g, m, v):
    m2 = ADAM_B1 * m + (1.0 - ADAM_B1) * g
    v2 = ADAM_B2 * v + (1.0 - ADAM_B2) * (g * g)
    m_hat = m2 / (1.0 - ADAM_B1 ** ADAM_STEP)
    v_hat = v2 / (1.0 - ADAM_B2 ** ADAM_STEP)
    delta = -ADAM_LR * (m_hat / (jnp.sqrt(v_hat) + ADAM_EPS) + ADAM_WD * w)
    return delta, m2, v2


def _adamw_shard(name, own, recv, w, m, v, tr):
    r, c = own.shape

    def body(own_ref, recv_ref, w_ref, m_ref, v_ref, g_ref, d_ref, m2_ref, v2_ref):
        g = own_ref[...]
        for k in range(3):
            g = g + recv_ref[k].astype(F32)
        delta, m2, v2 = _adamw_math(w_ref[...], g, m_ref[...], v_ref[...])
        g_ref[...] = g
        d_ref[...] = delta
        m2_ref[...] = m2
        v2_ref[...] = v2

    spec = pl.BlockSpec((tr, c), lambda i: (i, 0))
    return pl.pallas_call(
        body,
        name=name,
        grid=(r // tr,),
        in_specs=[spec, pl.BlockSpec((3, tr, c), lambda i: (0, i, 0)), spec, spec, spec],
        out_specs=[spec] * 4,
        out_shape=[_sds((r, c), F32)] * 4,
        compiler_params=_params(("parallel",)),
    )(own, recv, w, m, v)


def _adamw_small(name, gathered, first_row, w, m, v):
    r = w.shape[0]
    assert first_row % r == 0

    def body(ga_ref, w_ref, m_ref, v_ref, g_ref, d_ref, m2_ref, v2_ref):
        g = ga_ref[0]
        for k in range(1, N_DEV):
            g = g + ga_ref[k]
        delta, m2, v2 = _adamw_math(w_ref[...], g, m_ref[...], v_ref[...])
        g_ref[...] = g
        d_ref[...] = delta
        m2_ref[...] = m2
        v2_ref[...] = v2

    spec = pl.BlockSpec((r, LANES), lambda i: (0, 0))
    return pl.pallas_call(
        body,
        name=name,
        grid=(1,),
        in_specs=[pl.BlockSpec((N_DEV, r, LANES), lambda i: (0, first_row // r, 0)), spec, spec, spec],
        out_specs=[spec] * 4,
        out_shape=[_sds((r, LANES), F32)] * 4,
        compiler_params=_params(("arbitrary",)),
    )(gathered, w, m, v)


_TINY_EARLY = (("b_sgu", (1, SGU_G, SGU_LEN)), ("norm2_g", (1, D_MODEL)), ("normf_g", (D_MODEL,)),
               ("ln_v_g", (1, SGU_W)), ("ln_v_b", (1, SGU_W)))
_TINY_LATE = (("b_f", (1, N_HEADS)), ("norm1_g", (1, D_MODEL)), ("loss", ()))


def _pack_rows(values):
    rows = []
    for val in values:
        flat = val.reshape(-1).astype(F32)
        pad = (-flat.shape[0]) % LANES
        rows.append(jnp.pad(flat, (0, pad)).reshape(-1, LANES))
    packed = jnp.concatenate(rows, axis=0)
    return jnp.pad(packed, ((0, (-packed.shape[0]) % 8), (0, 0)))


def _unpack_rows(packed, group):
    out, row = {}, 0
    for name, shape in group:
        size = math.prod(shape)
        n_rows = -(-size // LANES)
        out[name] = packed[row:row + n_rows].reshape(-1)[:size].reshape(shape)
        row += n_rows
    return out


def kernel(x, norm1_g, w_in, b_f, ln_v_g, ln_v_b, w_sgu, b_sgu, w_a, w_b, w_o, norm2_g, w_up, w_down, normf_g, loss_target, m_norm1_g, m_w_in, m_b_f, m_ln_v_g, m_ln_v_b, m_w_sgu, m_b_sgu, m_w_a, m_w_b, m_w_o, m_norm2_g, m_w_up, m_w_down, m_normf_g, v_norm1_g, v_w_in, v_b_f, v_ln_v_g, v_ln_v_b, v_w_sgu, v_b_sgu, v_w_a, v_w_b, v_w_o, v_norm2_g, v_w_up, v_w_down, v_normf_g):
    xs = x[0]
    target = loss_target[0]
    s_len, d = xs.shape
    tm = min(512, s_len)
    tl = min(1024, s_len)
    tr = min(512, s_len)
    ta = min(512, s_len)
    tc = min(512, s_len)

    w_in_t = jnp.transpose(w_in[0])
    lin = (IN_SHARD * d // LANES, LANES)
    big = (w_in_t.reshape(lin), w_a[0], w_b[0], w_o[0], w_up[0], w_down[0])
    h, h_t, r1, (w_in_g,) = _rms_fwd("rms1", xs, norm1_g, tm, comm=[_gather_plan(w_in_t.astype(BF16))])
    w_in_f = w_in_g.reshape(IN_COLS, d)
    later_shards = [jnp.transpose(w_a[0]), jnp.transpose(w_b[0]), w_o[0], jnp.transpose(w_up[0]), w_down[0]]
    later_plans = [_gather_plan(w.astype(BF16)) for w in later_shards]

    def unflatten(gathered):
        return [g.reshape(N_DEV * g.shape[1], g.shape[2]) for g in gathered]

    w_qkv = w_in_f[:QKV_W]
    f_lo = QKV_W
    u_lo = f_lo + N_HEADS
    w_rest = jnp.concatenate([w_in_f[u_lo:], jnp.pad(w_in_f[f_lo:u_lo], ((0, LANES - N_HEADS), (0, 0)))], axis=0)

    chunk_id = jnp.arange(SGU_LEN) // CHUNK
    sgu_mask = chunk_id[None, :] <= chunk_id[:, None]
    w_masked = jnp.where(sgu_mask[None], w_sgu[0], 0.0)
    w_stack = w_masked.reshape(SGU_G // 2, 2 * SGU_LEN, SGU_LEN).astype(BF16)
    wt_stack = jnp.transpose(w_masked, (0, 2, 1)).reshape(SGU_G // 2, 2 * SGU_LEN, SGU_LEN).astype(BF16)
    b_pair = jnp.transpose(jnp.repeat(b_sgu[0], SGU_W // SGU_G, axis=0))
    b_f_pad = jnp.pad(b_f, ((0, 0), (0, LANES - N_HEADS)))
    head_sel = (jnp.arange(FOX_W)[:, None] // HEAD_DIM == jnp.arange(LANES)[None, :]).astype(F32)

    def store(dtype):
        def epi(accs, ex, out):
            out[0][...] = accs[0].astype(dtype)
        return epi

    def qkv_epi(accs, ex, out):
        tile = accs[0].astype(BF16)
        out[0][...] = tile
        for col, ref in ((1, out[1]), (2, out[2])):
            @pl.when(pl.program_id(1) == col)
            def _():
                ref[...] = jnp.transpose(tile)

    t_spec = pl.BlockSpec((FOX_W, tl), lambda i, j: (0, i))
    qkv, k_t, v_t = _mm("proj_qkv", [(h, w_qkv, True, None)], [],
                        [(_sds((s_len, QKV_W), BF16), _tile(tl, FOX_W)), (_sds((FOX_W, s_len), BF16), t_spec),
                         (_sds((FOX_W, s_len), BF16), t_spec)],
                        qkv_epi, m=s_len, tm=tl, n=QKV_W, tn=FOX_W, arbitrary=True)
    rest_tn = 640
    f_tile, f_lane = F_OFF // rest_tn, F_OFF % rest_tn

    def rest_epi(accs, ex, out):
        out[0][...] = accs[0].astype(BF16)

        @pl.when(pl.program_id(1) == f_tile)
        def _():
            out[1][...] = accs[0][:, f_lane:f_lane + LANES]

    prest, f_logit = _mm("proj_rest", [(h, w_rest, True, None)], [],
                         [(_sds((s_len, REST_W), BF16), _tile(tl, rest_tn)), (_sds((s_len, LANES), F32), _row(tl, LANES))],
                         rest_epi, m=s_len, tm=tl, n=REST_W, tn=rest_tn, arbitrary=True)

    c_col = _forget_cumsum(f_logit, b_f_pad, tc)
    o, o_t, lse_row, later_g = _attn_fwd(qkv, v_t, c_col, ta, ta, comm=later_plans)
    w_a_t, w_b_t, w_o_f, w_up_t, w_down_f = unflatten(later_g)
    sg, sg_t = _sgu_fwd(prest, ln_v_g, ln_v_b, w_stack, b_pair, tm)

    def merge_epi(accs, ex, out):
        ya, yb = accs
        sa, sb = _sigmoid(ex[0][...].astype(F32)), _sigmoid(ex[1][...].astype(F32))
        merged = (sa * ya + sb * yb).astype(BF16)
        out[0][...] = merged
        out[1][...] = ya.astype(BF16)
        out[2][...] = yb.astype(BF16)
        out[3][...] = jnp.transpose(merged)

    merged, ya, yb, merged_t = _mm(
        "merge", [(o, w_a_t, True, None), (sg, w_b_t, True, None)],
        [(prest, _tile(tm, d, GA_OFF // d)), (prest, _tile(tm, d, GB_OFF // d))],
        [(_sds((s_len, d), BF16), _tile(tm, d))] * 3 + [(_sds((d, s_len), BF16), _tile_t(tm, d))],
        merge_epi, m=s_len, tm=tm, n=d, tn=d)

    def resid_epi(accs, ex, out):
        x1v = ex[0][...] + accs[0]
        out[0][...] = x1v
        r = lax.rsqrt(jnp.mean(x1v * x1v, axis=-1, keepdims=True) + EPS)
        h2v = (x1v * r * ex[1][...]).astype(BF16)
        out[1][...] = h2v
        out[2][...] = jnp.transpose(h2v)
        out[3][...] = r

    x1, h2, h2_t, r2 = _mm(
        "out_proj", [(merged, w_o_f, False, None)], [(xs, _tile(tm, d)), (norm2_g, _whole((1, d)))],
        [(_sds((s_len, d), F32), _tile(tm, d)), (_sds((s_len, d), BF16), _tile(tm, d)),
         (_sds((d, s_len), BF16), _tile_t(tm, d)), (_sds((s_len, 1), F32), _row(tm, 1))],
        resid_epi, m=s_len, tm=tm, n=d, tn=d)

    def up_epi(accs, ex, out):
        act = jnp.square(jnp.maximum(accs[0], 0.0)).astype(BF16)
        out[0][...] = act
        out[1][...] = jnp.transpose(act)

    act, act_t = _mm(
        "mlp_up", [(h2, w_up_t, True, None)], [],
        [(_sds((s_len, D_FF), BF16), _tile(tl, 512)), (_sds((D_FF, s_len), BF16), _tile_t(tl, 512))],
        up_epi, m=s_len, tm=tl, n=D_FF, tn=512)

    def first_step():
        return jnp.logical_and(pl.program_id(0) == 0, pl.program_id(1) == 0)

    def accumulate(ref, val):
        @pl.when(first_step())
        def _():
            ref[...] = val

        @pl.when(jnp.logical_not(first_step()))
        def _():
            ref[...] += val

    def final_epi(accs, ex, out):
        x1_ref, t_ref, g_ref = ex
        x2 = x1_ref[...] + accs[0]
        rf = lax.rsqrt(jnp.mean(x2 * x2, axis=-1, keepdims=True) + EPS)
        xh = x2 * rf
        gf = g_ref[...]
        err = xh * gf - t_ref[...]
        dy = err * (1.0 / d)
        dx2 = _rms_bwd(xh, rf, gf, dy)
        out[0][...] = dx2
        accumulate(out[1], jnp.sum(dy * xh, axis=0, keepdims=True))
        part = 0.5 * jnp.sum(jnp.sum(err * err, axis=-1, keepdims=True) * (1.0 / d), axis=0, keepdims=True)
        accumulate(out[2], jnp.broadcast_to(part, (1, LANES)))
        out[3][...] = dx2.astype(BF16)

    gf2 = normf_g.reshape(1, d)
    dx2, g_normf, loss_part, dx2_16 = _mm(
        "mlp_down_loss", [(act, w_down_f, False, None)],
        [(x1, _row(tr, d)), (target, _row(tr, d)), (gf2, _whole((1, d)))],
        [(_sds((s_len, d), F32), _row(tr, d)), (_sds((1, d), F32), _whole((1, d))), (_sds((1, LANES), F32), _whole((1, LANES))),
         (_sds((s_len, d), BF16), _row(tr, d))],
        final_epi, m=s_len, tm=tr, n=d, tn=d, arbitrary=True)

    def dact_epi(accs, ex, out):
        out[0][...] = (accs[0] * (2.0 * jnp.sqrt(ex[0][...].astype(F32)))).astype(BF16)

    (da,) = _mm("mlp_down_bwd", [(dx2_16, w_down_f, True, None)], [(act, _tile(tl, 512))],
                [(_sds((s_len, D_FF), BF16), _tile(tl, 512))], dact_epi, m=s_len, tm=tl, n=D_FF, tn=512)
    g_down = _grad_w("grad_w_down", act_t, dx2_16, tk=1024, tn=d, ts=tl)
    g_up = _grad_w("grad_w_up", h2_t, da, tk=d, tn=1024, ts=tl, block_cols=D_FF // N_DEV)

    def dh2_epi(accs, ex, out):
        x1_ref, r_ref, g_ref, dx2_ref = ex
        r = r_ref[...]
        xh = x1_ref[...] * r
        dh2 = accs[0]
        out[0][...] = dx2_ref[...] + _rms_bwd(xh, r, g_ref[...], dh2)
        accumulate(out[1], jnp.sum(dh2 * xh, axis=0, keepdims=True))

    my_c = lax.axis_index("c")
    my_chip = 2 * lax.axis_index("x") + lax.axis_index("y")
    idx = jnp.stack([my_c, my_chip]).astype(jnp.int32)
    parts16, owns = {}, {}

    def split_cores(g8):
        return g8.reshape((4, 2) + g8.shape[1:])

    def row_tile(r):
        return 512 if r % 512 == 0 else r

    def pair_sums(names, grads4, from_sibling):
        for name, g4, recv in zip(names, grads4, from_sibling):
            parts16[name], owns[name] = _pair_sum("grad_pair_sum_" + name, g4, recv, idx, row_tile(g4.shape[2]))

    grads4_mlp = [split_cores(g_up), split_cores(g_down.reshape(N_DEV, D_FF // N_DEV, d))]
    (dx1, g_norm2), from_sibling = _mm(
        "mlp_up_bwd", [(da, w_up_t, False, None)],
        [(x1, _row(tr, d)), (r2, _row(tr, 1)), (norm2_g, _whole((1, d))), (dx2, _row(tr, d))],
        [(_sds((s_len, d), F32), _row(tr, d)), (_sds((1, d), F32), _whole((1, d)))],
        dh2_epi, m=s_len, tm=tr, n=d, tn=d, arbitrary=True, comm=[_pair_exchange_plan(grads4_mlp)])
    pair_sums(("w_up", "w_down"), grads4_mlp, from_sibling)

    def dmerge_epi(accs, ex, out):
        dm = accs[0]
        sa, sb = _sigmoid(ex[0][...].astype(F32)), _sigmoid(ex[1][...].astype(F32))
        out[0][...] = (dm * sa).astype(BF16)
        out[1][...] = (dm * sb).astype(BF16)
        dga = (dm * ex[2][...] * sa * (1.0 - sa)).astype(BF16)
        dgb = (dm * ex[3][...] * sb * (1.0 - sb)).astype(BF16)
        out[2][...] = dga
        out[3][...] = dgb
        out[4][...] = jnp.transpose(dga)
        out[5][...] = jnp.transpose(dgb)

    dya, dyb, dga, dgb, dga_t, dgb_t = _mm(
        "out_proj_bwd", [(dx1, w_o_f, True, None)],
        [(prest, _tile(tm, d, GA_OFF // d)), (prest, _tile(tm, d, GB_OFF // d)), (ya, _tile(tm, d)), (yb, _tile(tm, d))],
        [(_sds((s_len, d), BF16), _tile(tm, d))] * 4 + [(_sds((d, s_len), BF16), _tile_t(tm, d))] * 2,
        dmerge_epi, m=s_len, tm=tm, n=d, tn=d)
    g_o = _grad_w("grad_w_o", merged_t, dx1, tk=d, tn=d, ts=tl).reshape(N_DEV, d // N_DEV, d)
    def col_blocks(g):
        return jnp.transpose(g.reshape(g.shape[0], N_DEV, g.shape[1] // N_DEV), (1, 0, 2))

    g_a = col_blocks(_grad_w("grad_w_a", o_t, dya, tk=FOX_W, tn=d, ts=tl))
    g_b = col_blocks(_grad_w("grad_w_b", sg_t, dyb, tk=SGU_W, tn=d, ts=tl))

    def do_epi(accs, ex, out):
        do = accs[0]
        out[0][...] = do
        out[1][...] = _dot_f32(do * ex[0][...], ex[1][...])

    grads4_mix = [split_cores(g) for g in (g_a, g_b, g_o)]
    (do, delta), from_sibling = _mm(
        "attn_out_bwd", [(dya, w_a_t, False, None)], [(o, _row(tm, FOX_W)), (head_sel, _whole((FOX_W, LANES)))],
        [(_sds((s_len, FOX_W), F32), _row(tm, FOX_W)), (_sds((s_len, LANES), F32), _row(tm, LANES))],
        do_epi, m=s_len, tm=tm, n=FOX_W, tn=FOX_W, comm=[_pair_exchange_plan(grads4_mix)])
    pair_sums(("w_a", "w_b", "w_o"), grads4_mix, from_sibling)
    (dsg,) = _mm("sgu_out_bwd", [(dyb, w_b_t, False, None)], [], [(_sds((s_len, SGU_W), F32), _tile(tm, SGU_W))],
                 store(F32), m=s_len, tm=tm, n=SGU_W, tn=SGU_W)

    du, dsv, dw_pairs, db_pos, g_ln_g, g_ln_b, du_t, dsv_t = _sgu_bwd(
        prest, dsg, ln_v_g, ln_v_b, w_stack, wt_stack, b_pair, tm)
    g_w_sgu = jnp.where(sgu_mask[None], dw_pairs.reshape(SGU_G, SGU_LEN, SGU_LEN), 0.0)
    g_b_sgu = jnp.transpose(jnp.sum(db_pos.reshape(SGU_LEN, SGU_G, SGU_W // SGU_G), axis=-1))

    delta_row = jnp.transpose(delta[:, :N_HEADS]).reshape(N_HEADS, 1, s_len)
    early = ("w_a", "w_b", "w_o", "w_up", "w_down")
    small_early = _pack_rows((g_w_sgu, g_b_sgu, g_norm2, g_normf, g_ln_g, g_ln_b))
    (dq, dk, dv, dc_rows_blk, dc_cols, dk_t, dv_t), (small_early_all, *from_chips_early) = _attn_bwd(
        qkv, k_t, do, c_col, lse_row, delta_row, ta, ta,
        comm=[_gather_plan(small_early), _chip_exchange_plan([parts16[n] for n in early])])
    dc_rows = jnp.transpose(dc_rows_blk.reshape(s_len // ta, N_HEADS, ta), (0, 2, 1)).reshape(s_len, N_HEADS)
    dc_rows = jnp.pad(dc_rows, ((0, 0), (0, LANES - N_HEADS)))
    dfl, g_bf, dfl_t = _forget_bwd(dc_rows, dc_cols, f_logit, b_f_pad, tc)

    dp_t = (jnp.transpose(dq), dk_t, dv_t, dfl_t, du_t, dsv_t, dga_t, dgb_t)
    g_in_rows = [_grad_w("grad_w_in_%d" % k, seg_t, h, tk=seg_t.shape[0], tn=d, ts=tl) for k, seg_t in enumerate(dp_t)]
    g_in_rows[3] = g_in_rows[3][:N_HEADS]
    g_in = jnp.concatenate(g_in_rows, axis=0).reshape((N_DEV,) + lin)

    def dx_epi(accs, ex, out):
        x_ref, r_ref, g_ref, dx1_ref = ex
        dh = accs[0]
        for extra in accs[1:]:
            dh = dh + extra
        r = r_ref[...]
        xh = x_ref[...] * r
        out[0][...] = dx1_ref[...] + _rms_bwd(xh, r, g_ref[...], dh)
        accumulate(out[1], jnp.sum(dh * xh, axis=0, keepdims=True))

    rest_cols = ((du, U_OFF, 512), (dsv, SV_OFF, 512), (dga, GA_OFF, 1024), (dgb, GB_OFF, 1024), (dfl, F_OFF, LANES))
    dx_pairs = [(seg, w_qkv, False, (512 * k, 512 * (k + 1))) for k, seg in enumerate((dq, dk, dv))]
    dx_pairs += [(seg, w_rest, False, (lo, lo + width)) for seg, lo, width in rest_cols]
    grads4_in = [split_cores(g_in)]
    pair_sums(("w_in",), grads4_in, _run_comm("grad_pair_exchange_w_in", [_pair_exchange_plan(grads4_in)]))
    (grad_x, g_norm1), (from_chips_in,) = _mm(
        "proj_bwd", dx_pairs,
        [(xs, _row(tr, d)), (r1, _row(tr, 1)), (norm1_g, _whole((1, d))), (dx1, _row(tr, d))],
        [(_sds((s_len, d), F32), _row(tr, d)), (_sds((1, d), F32), _whole((1, d)))],
        dx_epi, m=s_len, tm=tr, n=d, tn=d, arbitrary=True, comm=[_chip_exchange_plan([parts16["w_in"]])])
    small_late = _pack_rows((g_bf[:, :N_HEADS], g_norm1, loss_part[0, 0]))
    (small_late_all,) = _run_comm("gather_last_grads", [_gather_plan(small_late)])
    from_chips = dict(zip(early, from_chips_early), w_in=from_chips_in)

    names = ("w_in", "w_a", "w_b", "w_o", "w_up", "w_down")
    moments_m = (m_w_in, m_w_a, m_w_b, m_w_o, m_w_up, m_w_down)
    moments_v = (v_w_in, v_w_a, v_w_b, v_w_o, v_w_up, v_w_down)
    big_out = {}
    for name, w, m, v in zip(names, big, moments_m, moments_v):
        own = owns[name]
        transposed = name == "w_in"
        m0, v0 = (jnp.transpose(m[0]).reshape(lin), jnp.transpose(v[0]).reshape(lin)) if transposed else (m[0], v[0])
        res = _adamw_shard("adamw_" + name, own, from_chips[name], w, m0, v0, row_tile(own.shape[0]))
        big_out[name] = [(jnp.transpose(t.reshape(IN_SHARD, d)) if transposed else t)[None] for t in res]

    zero = jnp.zeros((), F32)
    sgu_rows = (SGU_G * SGU_LEN, LANES)
    res_sgu = _adamw_small("adamw_w_sgu", small_early_all, 0, w_sgu.reshape(sgu_rows), m_w_sgu.reshape(sgu_rows),
                           v_w_sgu.reshape(sgu_rows))
    small_out = {"w_sgu": [t.reshape(w_sgu.shape) for t in res_sgu]}
    res_early = _adamw_small(
        "adamw_tiny_early", small_early_all, sgu_rows[0], _pack_rows((b_sgu, norm2_g, normf_g, ln_v_g, ln_v_b)),
        _pack_rows((m_b_sgu, m_norm2_g, m_normf_g, m_ln_v_g, m_ln_v_b)),
        _pack_rows((v_b_sgu, v_norm2_g, v_normf_g, v_ln_v_g, v_ln_v_b)))
    res_late = _adamw_small(
        "adamw_tiny_late", small_late_all, 0, _pack_rows((b_f, norm1_g, zero)), _pack_rows((m_b_f, m_norm1_g, zero)),
        _pack_rows((v_b_f, v_norm1_g, zero)))
    for res, group in ((res_early, _TINY_EARLY), (res_late, _TINY_LATE)):
        unpacked = [_unpack_rows(t, group) for t in res]
        small_out.update({name: [u[name] for u in unpacked] for name, _ in group})
    loss = small_out["loss"][0]

    order = ("norm1_g", "w_in", "b_f", "ln_v_g", "ln_v_b", "w_sgu", "b_sgu", "w_a", "w_b", "w_o", "norm2_g", "w_up",
             "w_down", "normf_g")
    table = {**big_out, **small_out}
    outs = [loss, grad_x[None]]
    for kind in range(4):
        outs += [table[n][kind] for n in order]
    return tuple(outs)
```

```python
import math

import jax
import jax.numpy as jnp
from jax import lax
from jax.experimental import pallas as pl
from jax.experimental.pallas import tpu as pltpu

F32 = jnp.float32
BF16 = jnp.bfloat16

N_DEV = 8
D_MODEL = 1024
N_HEADS = 8
HEAD_DIM = 64
FOX_W = N_HEADS * HEAD_DIM
SGU_G = 8
SGU_W = 512
SGU_LEN = 128
CHUNK = 64
D_FF = 4 * D_MODEL
IN_COLS = 3 * FOX_W + N_HEADS + 2 * SGU_W + 2 * D_MODEL
IN_SHARD = IN_COLS // N_DEV
LANES = 128
QKV_W = 3 * FOX_W
U_OFF, SV_OFF, GA_OFF, GB_OFF, F_OFF = 0, 512, 1024, 2048, 3072
REST_W = F_OFF + LANES
EPS = 1e-6
NEG = -1e30

ADAM_LR = 0.001
ADAM_B1 = 0.9
ADAM_B2 = 0.999
ADAM_EPS = 1e-08
ADAM_WD = 0.01
ADAM_STEP = 10

VMEM_LIMIT = 56 * 1024 * 1024
MESH = pl.DeviceIdType.MESH


def _params(sem=None):
    return pltpu.CompilerParams(dimension_semantics=sem, vmem_limit_bytes=VMEM_LIMIT)


def _dot(a, b):
    return jnp.dot(a, b, preferred_element_type=F32)


def _dot_nt(a, b):
    return lax.dot_general(a, b, (((1,), (1,)), ((), ())), preferred_element_type=F32)


def _dot_f32(a, b):
    return jnp.dot(a, b, preferred_element_type=F32, precision=lax.Precision.HIGHEST)


def _sigmoid(x):
    return 1.0 / (1.0 + jnp.exp(-x))


def _log_sigmoid(z):
    return jnp.minimum(z, 0.0) - jnp.log(1.0 + jnp.exp(-jnp.abs(z)))


_GELU_K = math.sqrt(2.0 / math.pi)
_GELU_C = 0.044715


def _gelu(x):
    t = jnp.tanh(_GELU_K * (x + _GELU_C * (x * x * x)))
    return 0.5 * x * (1.0 + t)


def _gelu_grad(x):
    x2 = x * x
    t = jnp.tanh(_GELU_K * (x + _GELU_C * (x2 * x)))
    return 0.5 * (1.0 + t) + 0.5 * x * (1.0 - t * t) * (_GELU_K * (1.0 + 3.0 * _GELU_C * x2))


def _rms_bwd(xh, r, g, dy):
    gy = dy * g
    return r * (gy - xh * jnp.mean(xh * gy, axis=-1, keepdims=True))


def _lane_lt64(shape):
    return lax.broadcasted_iota(jnp.int32, shape, len(shape) - 1) < HEAD_DIM


class _Comm:
    def __init__(self, arrays, out_shapes, sems, start, finish, mid=None):
        self.arrays, self.out_shapes, self.sems = list(arrays), list(out_shapes), list(sems)
        self.start, self.mid, self.finish = start, mid, finish


def _comm_phase(plans, phase, in_refs, out_refs, sem_refs):
    ia = io = ks = 0
    for plan in plans:
        na, no, ns = len(plan.arrays), len(plan.out_shapes), len(plan.sems)
        fn = getattr(plan, phase)
        if fn is not None:
            fn(in_refs[ia:ia + na], out_refs[io:io + no], sem_refs[ks:ks + ns])
        ia, io, ks = ia + na, io + no, ks + ns


def _comm_operands(plans):
    arrays = [a for plan in plans for a in plan.arrays]
    out_shapes = [o for plan in plans for o in plan.out_shapes]
    sems = [s for plan in plans for s in plan.sems]
    return arrays, out_shapes, sems


_ANY = pl.BlockSpec(memory_space=pl.ANY)


def _run_comm(name, plans):
    arrays, out_shapes, sems = _comm_operands(plans)
    n_in, n_out = len(arrays), len(out_shapes)

    def body(*refs):
        parts = refs[:n_in], refs[n_in:n_in + n_out], refs[n_in + n_out:]
        for phase in ("start", "mid", "finish"):
            _comm_phase(plans, phase, *parts)

    return pl.pallas_call(
        body, name=name, out_shape=out_shapes, in_specs=[_ANY] * n_in, out_specs=[_ANY] * n_out, scratch_shapes=sems,
    )(*arrays)


def _gather_plan(shard):
    def setup(ins, outs, sems):
        (x_ref,), (out_ref,), (send_sems, recv_sems, local_sem) = ins, outs, sems
        x, y, c = lax.axis_index("x"), lax.axis_index("y"), lax.axis_index("c")
        me, sibling = (x, y, c), (x, y, 1 - c)
        chips = [(1 - x, y), (x, 1 - y), (1 - x, 1 - y)]

        def rows(px, py, pc):
            return out_ref.at[4 * px + 2 * py + pc]

        def copy(k, block, to, src=None):
            return pltpu.make_async_remote_copy(
                src_ref=rows(*block) if src is None else src,
                dst_ref=rows(*block),
                send_sem=send_sems.at[k],
                recv_sem=recv_sems.at[k],
                device_id=to,
                device_id_type=MESH,
            )

        mine = pltpu.make_async_copy(x_ref, rows(*me), local_sem)
        first = [copy(0, me, sibling, src=x_ref)]
        first += [copy(1 + j, me, (*chip, c), src=x_ref) for j, chip in enumerate(chips)]
        passed = [copy(4 + j, (*chip, c), sibling) for j, chip in enumerate(chips)]
        landed = [copy(1 + j, (*chip, c), me) for j, chip in enumerate(chips)]
        from_sibling = [copy(0, sibling, me)] + [copy(4 + j, (*chip, 1 - c), me) for j, chip in enumerate(chips)]
        return mine, first, passed, landed, from_sibling

    def start(ins, outs, sems):
        mine, first, _, _, _ = setup(ins, outs, sems)
        mine.start()
        for cp in first:
            cp.start()

    def mid(ins, outs, sems):
        _, _, passed, landed, _ = setup(ins, outs, sems)
        for arrived, onward in zip(landed, passed):
            arrived.wait_recv()
            onward.start()

    def finish(ins, outs, sems):
        mine, first, passed, _, from_sibling = setup(ins, outs, sems)
        for cp in from_sibling:
            cp.wait_recv()
        for cp in first + passed:
            cp.wait_send()
        mine.wait()

    return _Comm([shard], [jax.ShapeDtypeStruct((N_DEV,) + shard.shape, shard.dtype)],
                 [pltpu.SemaphoreType.DMA((7,)), pltpu.SemaphoreType.DMA((7,)), pltpu.SemaphoreType.DMA],
                 start, finish, mid)


def _start_all(copies):
    for cp in copies:
        cp.start()


def _wait_all(copies):
    for cp in copies:
        cp.wait_recv()
    for cp in copies:
        cp.wait_send()


def _pair_exchange_plan(grads):
    n = len(grads)

    def copies(ins, outs, sems):
        send_sems, recv_sems = sems
        x, y, c = lax.axis_index("x"), lax.axis_index("y"), lax.axis_index("c")
        return [
            pltpu.make_async_remote_copy(
                src_ref=ins[k].at[:, 1 - c],
                dst_ref=outs[k],
                send_sem=send_sems.at[k],
                recv_sem=recv_sems.at[k],
                device_id=(x, y, 1 - c),
                device_id_type=MESH,
            )
            for k in range(n)
        ]

    return _Comm(grads, [jax.ShapeDtypeStruct((4,) + g.shape[2:], g.dtype) for g in grads],
                 [pltpu.SemaphoreType.DMA((n,)), pltpu.SemaphoreType.DMA((n,))],
                 lambda *refs: _start_all(copies(*refs)), lambda *refs: _wait_all(copies(*refs)))


def _chip_exchange_plan(parts):
    n = len(parts)

    def copies(ins, outs, sems):
        send_sems, recv_sems = sems
        x, y, c = lax.axis_index("x"), lax.axis_index("y"), lax.axis_index("c")
        chips = [(1 - x, y), (x, 1 - y), (1 - x, 1 - y)]
        return [
            pltpu.make_async_remote_copy(
                src_ref=ins[k].at[2 * px + py],
                dst_ref=outs[k].at[j],
                send_sem=send_sems.at[3 * k + j],
                recv_sem=recv_sems.at[3 * k + j],
                device_id=(px, py, c),
                device_id_type=MESH,
            )
            for k in range(n) for j, (px, py) in enumerate(chips)
        ]

    return _Comm(parts, [jax.ShapeDtypeStruct((3,) + p.shape[1:], p.dtype) for p in parts],
                 [pltpu.SemaphoreType.DMA((3 * n,)), pltpu.SemaphoreType.DMA((3 * n,))],
                 lambda *refs: _start_all(copies(*refs)), lambda *refs: _wait_all(copies(*refs)))


def _mm(name, pairs, extras, outs, epi, *, m, tm, n, tn, arbitrary=False, comm=()):
    nj = n // tn
    a_arrays, a_specs, b_arrays, b_specs, b_index = [], [], [], [], []
    for a, b, nt, cols in pairs:
        a_arrays.append(a)
        a_specs.append(pl.BlockSpec((tm, a.shape[1]), lambda i, j: (i, 0)))
        known = [k for k, other in enumerate(b_arrays) if other is b]
        if known:
            b_index.append(known[0])
            continue
        b_index.append(len(b_arrays))
        b_arrays.append(b)
        if cols is not None:
            assert nj == 1
            b_specs.append(pl.BlockSpec(b.shape, lambda i, j: (0, 0)))
        elif nt:
            b_specs.append(pl.BlockSpec((tn, b.shape[1]), lambda i, j: (j, 0)))
        else:
            b_specs.append(pl.BlockSpec((b.shape[0], tn), lambda i, j: (0, j)))
    comm_arrays, comm_outs, comm_sems = _comm_operands(comm)
    arrays = a_arrays + b_arrays + [arr for arr, _ in extras] + comm_arrays
    in_specs = a_specs + b_specs + [spec for _, spec in extras] + [_ANY] * len(comm_arrays)
    n_a, n_b, n_extras, n_ci, n_out, n_co = len(a_arrays), len(b_arrays), len(extras), len(comm_arrays), len(outs), len(comm_outs)
    ni = m // tm

    def body(*refs):
        a_refs = refs[:n_a]
        b_refs = refs[n_a:n_a + n_b]
        ex = refs[n_a + n_b:n_a + n_b + n_extras]
        n_in = n_a + n_b + n_extras + n_ci
        comm_refs = refs[n_in - n_ci:n_in], refs[n_in + n_out:n_in + n_out + n_co], refs[n_in + n_out + n_co:]
        out = refs[n_in:n_in + n_out]
        if comm:
            @pl.when(jnp.logical_and(pl.program_id(0) == 0, pl.program_id(1) == 0))
            def _():
                _comm_phase(comm, "start", *comm_refs)

        accs = []
        for p, (_, _, nt, cols) in enumerate(pairs):
            av = a_refs[p][...]
            if av.dtype != BF16:
                av = av.astype(BF16)
            b_ref = b_refs[b_index[p]]
            if cols is None:
                bv = b_ref[...]
            else:
                bv = b_ref[:, cols[0]:cols[1]] if nt else b_ref[cols[0]:cols[1], :]
            accs.append(_dot_nt(av, bv) if nt else _dot(av, bv))
        epi(accs, ex, out)
        if comm:
            mid_row = ni // 2 if ni >= 3 else ni - 1
            mid_col = 0 if ni >= 3 else nj - 1

            @pl.when(jnp.logical_and(pl.program_id(0) == mid_row, pl.program_id(1) == mid_col))
            def _():
                _comm_phase(comm, "mid", *comm_refs)

            @pl.when(jnp.logical_and(pl.program_id(0) == ni - 1, pl.program_id(1) == nj - 1))
            def _():
                _comm_phase(comm, "finish", *comm_refs)

    sem = ("arbitrary", "arbitrary") if arbitrary or comm else ("parallel", "parallel")
    res = pl.pallas_call(
        body,
        name=name,
        grid=(ni, nj),
        in_specs=in_specs,
        out_specs=[spec for _, spec in outs] + [_ANY] * n_co,
        out_shape=[shape for shape, _ in outs] + comm_outs,
        scratch_shapes=comm_sems,
        compiler_params=_params(sem),
    )(*arrays)
    return (res[:n_out], res[n_out:]) if comm else res


def _tile(tm, tn, off=0):
    return pl.BlockSpec((tm, tn), lambda i, j: (i, j + off))


def _row(tm, w, blk=0):
    return pl.BlockSpec((tm, w), lambda i, j: (i, blk))


def _whole(shape):
    zeros = (0,) * len(shape)
    return pl.BlockSpec(shape, lambda i, j: zeros)


def _sds(shape, dtype):
    return jax.ShapeDtypeStruct(shape, dtype)


def _tile_t(tm, tn):
    return pl.BlockSpec((tn, tm), lambda i, j: (j, i))


def _grad_w(name, a_t, g, *, tk, tn, ts, block_cols=None):
    ka, s_len = a_t.shape
    n = g.shape[1]
    width = tn if block_cols is None else block_cols

    def body(a_ref, g_ref, o_ref):
        first = pl.program_id(2) == 0
        gv = g_ref[...].astype(BF16)
        for b in range(tn // width):
            part = _dot(a_ref[...], gv[:, b * width:(b + 1) * width])
            dst = o_ref if block_cols is None else o_ref.at[b]

            @pl.when(first)
            def _():
                dst[...] = part

            @pl.when(jnp.logical_not(first))
            def _():
                dst[...] += part

    if block_cols is None:
        out_shape = _sds((ka, n), F32)
        out_spec = pl.BlockSpec((tk, tn), lambda i, j, s: (i, j))
    else:
        out_shape = _sds((n // width, ka, width), F32)
        out_spec = pl.BlockSpec((tn // width, tk, width), lambda i, j, s: (j, i, 0))
    return pl.pallas_call(
        body,
        name=name,
        grid=(ka // tk, n // tn, s_len // ts),
        in_specs=[pl.BlockSpec((tk, ts), lambda i, j, s: (i, s)), pl.BlockSpec((ts, tn), lambda i, j, s: (s, j))],
        out_specs=out_spec,
        out_shape=out_shape,
        compiler_params=_params(("parallel", "parallel", "arbitrary")),
    )(a_t, g)


def _rms_fwd(name, x, g, tm, comm=()):
    s_len, d = x.shape
    steps = s_len // tm
    comm_arrays, comm_outs, comm_sems = _comm_operands(comm)
    n_ci, n_co = len(comm_arrays), len(comm_outs)

    def body(x_ref, g_ref, *rest):
        comm_refs = rest[:n_ci], rest[n_ci + 3:n_ci + 3 + n_co], rest[n_ci + 3 + n_co:]
        h_ref, ht_ref, r_ref = rest[n_ci:n_ci + 3]
        for phase, at in (("start", 0), ("mid", steps // 2)):
            if comm:
                @pl.when(pl.program_id(0) == at)
                def _():
                    _comm_phase(comm, phase, *comm_refs)

        xv = x_ref[...]
        r = lax.rsqrt(jnp.mean(xv * xv, axis=-1, keepdims=True) + EPS)
        h = (xv * r * g_ref[...]).astype(BF16)
        h_ref[...] = h
        ht_ref[...] = jnp.transpose(h)
        r_ref[...] = r
        if comm:
            @pl.when(pl.program_id(0) == steps - 1)
            def _():
                _comm_phase(comm, "finish", *comm_refs)

    res = pl.pallas_call(
        body,
        name=name,
        grid=(steps,),
        in_specs=[pl.BlockSpec((tm, d), lambda i: (i, 0)), pl.BlockSpec((1, d), lambda i: (0, 0))] + [_ANY] * n_ci,
        out_specs=[pl.BlockSpec((tm, d), lambda i: (i, 0)), pl.BlockSpec((d, tm), lambda i: (0, i)),
                   pl.BlockSpec((tm, 1), lambda i: (i, 0))] + [_ANY] * n_co,
        out_shape=[_sds((s_len, d), BF16), _sds((d, s_len), BF16), _sds((s_len, 1), F32)] + comm_outs,
        scratch_shapes=comm_sems,
        compiler_params=_params(("arbitrary",) if comm else ("parallel",)),
    )(x, g, *comm_arrays)
    return res[0], res[1], res[2], res[3:]


def _forget_cumsum(prest, b_f_pad, tc):
    s_len = prest.shape[0]

    def body(f_ref, b_ref, c_ref, carry):
        @pl.when(pl.program_id(0) == 0)
        def _():
            carry[...] = jnp.zeros_like(carry)

        logf = _log_sigmoid(f_ref[...] + b_ref[...])
        row = lax.broadcasted_iota(jnp.int32, (tc, tc), 0)
        col = lax.broadcasted_iota(jnp.int32, (tc, tc), 1)
        tri = (row >= col).astype(F32)
        c = _dot_f32(tri, logf) + carry[...]
        c_ref[...] = c
        carry[...] = c[tc - 1:tc, :]

    return pl.pallas_call(
        body,
        name="forget_cumsum",
        grid=(s_len // tc,),
        in_specs=[pl.BlockSpec((tc, LANES), lambda i: (i, 0)), pl.BlockSpec((1, LANES), lambda i: (0, 0))],
        out_specs=pl.BlockSpec((tc, LANES), lambda i: (i, 0)),
        out_shape=_sds((s_len, LANES), F32),
        scratch_shapes=[pltpu.VMEM((1, LANES), F32)],
        compiler_params=_params(("arbitrary",)),
    )(prest, b_f_pad)


def _stack_heads(pair, lt64):
    zero = jnp.zeros_like(pair)
    return jnp.concatenate([jnp.where(lt64, pair, zero), jnp.where(lt64, zero, pair)], axis=0)


def _score_tiles(q_ref, k_ref, ck_ref, st_sc, tk):
    lt64 = _lane_lt64((tk, LANES))
    for p in range(N_HEADS // 2):
        lanes = slice(p * LANES, (p + 1) * LANES)
        q_pair = q_ref[:, lanes] * jnp.asarray(HEAD_DIM ** -0.5, BF16)
        st2 = _dot_nt(_stack_heads(k_ref[:, lanes], lt64), q_pair)
        for half in range(2):
            h = 2 * p + half
            st_sc[h] = st2[half * tk:(half + 1) * tk] - ck_ref[:, h:h + 1]


ROW_CHUNK = 64


def _row_chunks(tk):
    rc = min(ROW_CHUNK, tk)
    return [slice(r, r + rc) for r in range(0, tk, rc)]


def _by_sublane(x):
    return x.reshape(x.shape[0] // 8, 8, x.shape[1])


def _softmax_update(st_sc, p_sc, m_sc, l_sc, tk, tq):
    alphas = []
    for h in range(N_HEADS):
        top8 = jnp.full((8, tq), NEG, F32)
        for rows in _row_chunks(tk):
            top8 = jnp.maximum(top8, jnp.max(_by_sublane(st_sc[h, rows, :]), axis=0))
        m_old = m_sc[h]
        m_new = jnp.maximum(m_old, jnp.max(top8, axis=0, keepdims=True))
        sum8 = jnp.zeros((8, tq), F32)
        for rows in _row_chunks(tk):
            pt = jnp.exp(st_sc[h, rows, :] - m_new)
            p_sc[h, rows, :] = pt.astype(BF16)
            sum8 = sum8 + jnp.sum(_by_sublane(pt), axis=0)
        alpha = jnp.exp(m_old - m_new)
        l_sc[h] = alpha * l_sc[h] + jnp.sum(sum8, axis=0, keepdims=True)
        m_sc[h] = m_new
        alphas.append(alpha)
    return alphas


def _mask_diagonal(st_sc, i, j, tq, tk):
    @pl.when((j + 1) * tk - 1 > i * tq)
    def _():
        key = j * tk + lax.broadcasted_iota(jnp.int32, (tk, tq), 0)
        query = i * tq + lax.broadcasted_iota(jnp.int32, (tk, tq), 1)
        st_sc[...] = jnp.where((query >= key)[None], st_sc[...], NEG)


def _keep_mask(i, j, tq, tk):
    key = j * tk + lax.broadcasted_iota(jnp.int32, (tk, tq), 0)
    query = i * tq + lax.broadcasted_iota(jnp.int32, (tk, tq), 1)
    keep = jnp.logical_or(query >= key, (j + 1) * tk - 1 <= i * tq)
    return jnp.concatenate([keep, keep], axis=0)


def _pair_scores(q_ref, k_ref, ck_ref, p, lt64, keep2):
    lanes = slice(p * LANES, (p + 1) * LANES)
    q_pair = q_ref[:, lanes] * jnp.asarray(HEAD_DIM ** -0.5, BF16)
    ck2 = jnp.concatenate([ck_ref[:, 2 * p:2 * p + 1], ck_ref[:, 2 * p + 1:2 * p + 2]], axis=0)
    st2 = _dot_nt(_stack_heads(k_ref[:, lanes], lt64), q_pair) - ck2
    return jnp.where(keep2, st2, NEG)


def _attn_fwd(qkv, v_t, c_col, tq, tk, comm=()):
    s_len = qkv.shape[0]
    ratio = tq // tk
    steps = [(i, j) for i in range(s_len // tq) for j in range((i + 1) * ratio)]
    i_tab = jnp.asarray([i for i, _ in steps], jnp.int32)
    j_tab = jnp.asarray([j for _, j in steps], jnp.int32)

    comm_arrays, comm_outs, comm_sems = _comm_operands(comm)
    n_ci, n_co = len(comm_arrays), len(comm_outs)

    def body(i_ref, j_ref, q_ref, k_ref, vt_ref, ck_ref, *rest):
        comm_refs = rest[:n_ci], rest[n_ci + 3:n_ci + 3 + n_co], rest[n_ci + 3 + n_co + 5:]
        o_ref, ot_ref, lse_ref = rest[n_ci:n_ci + 3]
        acc_t, m_sc, l_sc, st_sc, p_sc = rest[n_ci + 3 + n_co:n_ci + 3 + n_co + 5]
        n = pl.program_id(0)
        i, j = i_ref[n], j_ref[n]
        for phase, at in (("start", 0), ("mid", (2 * len(steps)) // 3)):
            if comm:
                @pl.when(n == at)
                def _():
                    _comm_phase(comm, phase, *comm_refs)

        @pl.when(j == 0)
        def _():
            acc_t[...] = jnp.zeros_like(acc_t)
            m_sc[...] = jnp.full_like(m_sc, NEG)
            l_sc[...] = jnp.zeros_like(l_sc)

        _score_tiles(q_ref, k_ref, ck_ref, st_sc, tk)
        _mask_diagonal(st_sc, i, j, tq, tk)
        alpha = _softmax_update(st_sc, p_sc, m_sc, l_sc, tk, tq)
        top = lax.broadcasted_iota(jnp.int32, (LANES, tq), 0) < HEAD_DIM
        for p in range(N_HEADS // 2):
            lanes = slice(p * LANES, (p + 1) * LANES)
            vt_pair = vt_ref[lanes, :]
            pv = jnp.where(top, _dot(vt_pair, p_sc[2 * p]), _dot(vt_pair, p_sc[2 * p + 1]))
            acc_t[lanes, :] = acc_t[lanes, :] * jnp.where(top, alpha[2 * p], alpha[2 * p + 1]) + pv

        @pl.when(j == (i + 1) * ratio - 1)
        def _():
            for p in range(N_HEADS // 2):
                lanes = slice(p * LANES, (p + 1) * LANES)
                l_pair = jnp.where(top, l_sc[2 * p], l_sc[2 * p + 1])
                o_t = acc_t[lanes, :] / l_pair
                o_ref[:, lanes] = jnp.transpose(o_t)
                ot_ref[lanes, :] = o_t.astype(BF16)
            lse_ref[...] = m_sc[...] + jnp.log(l_sc[...])

        if comm:
            @pl.when(n == len(steps) - 1)
            def _():
                _comm_phase(comm, "finish", *comm_refs)

    stat = pltpu.VMEM((N_HEADS, 1, tq), F32)
    res = pl.pallas_call(
        body,
        name="attn_fwd",
        grid_spec=pltpu.PrefetchScalarGridSpec(
            num_scalar_prefetch=2,
            grid=(len(steps),),
            in_specs=[
                pl.BlockSpec((tq, FOX_W), lambda n, it, jt: (it[n], 0)),
                pl.BlockSpec((tk, FOX_W), lambda n, it, jt: (jt[n], 1)),
                pl.BlockSpec((FOX_W, tk), lambda n, it, jt: (0, jt[n])),
                pl.BlockSpec((tk, LANES), lambda n, it, jt: (jt[n], 0)),
            ] + [_ANY] * n_ci,
            out_specs=[
                pl.BlockSpec((tq, FOX_W), lambda n, it, jt: (it[n], 0)),
                pl.BlockSpec((FOX_W, tq), lambda n, it, jt: (0, it[n])),
                pl.BlockSpec((N_HEADS, 1, tq), lambda n, it, jt: (0, 0, it[n])),
            ] + [_ANY] * n_co,
            scratch_shapes=[pltpu.VMEM((FOX_W, tq), F32), stat, stat, pltpu.VMEM((N_HEADS, tk, tq), F32),
                            pltpu.VMEM((N_HEADS, tk, tq), BF16)] + comm_sems,
        ),
        out_shape=[_sds((s_len, FOX_W), F32), _sds((FOX_W, s_len), BF16), _sds((N_HEADS, 1, s_len), F32)] + comm_outs,
        compiler_params=_params(("arbitrary",)),
    )(i_tab, j_tab, qkv, qkv, v_t, c_col, *comm_arrays)
    return res[0], res[1], res[2], res[3:]


def _sgu_mix(vn, w_stack, lt64):
    outs = []
    for p in range(SGU_G // 2):
        r = _dot(w_stack[p], vn[:, p * LANES:(p + 1) * LANES])
        outs.append(jnp.where(lt64, r[:SGU_LEN], r[SGU_LEN:]))
    return jnp.concatenate(outs, axis=1)


def _sgu_norm(sv, ln_g, ln_b):
    svg = _gelu(sv)
    xc = svg - jnp.mean(svg, axis=-1, keepdims=True)
    rstd = lax.rsqrt(jnp.mean(xc * xc, axis=-1, keepdims=True) + EPS)
    xhat = xc * rstd
    return xhat, rstd, xhat * ln_g + ln_b


def _sgu_fwd(prest, ln_g, ln_b, w_stack, b_pair, tm):
    s_len = prest.shape[0]

    def body(u_ref, sv_ref, g_ref, b_ref, w_ref, bp_ref, sg_ref, sgt_ref):
        lt64 = _lane_lt64((SGU_LEN, LANES))
        _, _, vn = _sgu_norm(sv_ref[...].astype(F32), g_ref[...], b_ref[...])
        vn = vn.astype(BF16)
        w_stack_v = [w_ref[p] for p in range(SGU_G // 2)]
        for w in range(tm // SGU_LEN):
            win = slice(w * SGU_LEN, (w + 1) * SGU_LEN)
            mixed = _sgu_mix(vn[win], w_stack_v, lt64) + bp_ref[...]
            sg = (_gelu(u_ref[win, :].astype(F32)) * mixed).astype(BF16)
            sg_ref[win, :] = sg
            sgt_ref[:, win] = jnp.transpose(sg)

    return pl.pallas_call(
        body,
        name="sgu_fwd",
        grid=(s_len // tm,),
        in_specs=[
            pl.BlockSpec((tm, SGU_W), lambda i: (i, U_OFF // SGU_W)),
            pl.BlockSpec((tm, SGU_W), lambda i: (i, SV_OFF // SGU_W)),
            pl.BlockSpec((1, SGU_W), lambda i: (0, 0)),
            pl.BlockSpec((1, SGU_W), lambda i: (0, 0)),
            pl.BlockSpec((SGU_G // 2, 2 * SGU_LEN, SGU_LEN), lambda i: (0, 0, 0)),
            pl.BlockSpec((SGU_LEN, SGU_W), lambda i: (0, 0)),
        ],
        out_specs=[pl.BlockSpec((tm, SGU_W), lambda i: (i, 0)), pl.BlockSpec((SGU_W, tm), lambda i: (0, i))],
        out_shape=[_sds((s_len, SGU_W), BF16), _sds((SGU_W, s_len), BF16)],
        compiler_params=_params(("parallel",)),
    )(prest, prest, ln_g, ln_b, w_stack, b_pair)


def _sgu_bwd(prest, dsg, ln_g, ln_b, w_stack, wt_stack, b_pair, tm):
    s_len = prest.shape[0]
    n_pair = SGU_G // 2

    def body(u_ref, sv_ref, dsg_ref, g_ref, b_ref, w_ref, wt_ref, bp_ref,
             du_ref, dsv_ref, dw_ref, db_ref, dg_ref, dbeta_ref, dut_ref, dsvt_ref, dvn_sc):
        @pl.when(pl.program_id(0) == 0)
        def _():
            dw_ref[...] = jnp.zeros_like(dw_ref)
            db_ref[...] = jnp.zeros_like(db_ref)
            dg_ref[...] = jnp.zeros_like(dg_ref)
            dbeta_ref[...] = jnp.zeros_like(dbeta_ref)

        lt64 = _lane_lt64((SGU_LEN, LANES))
        sv = sv_ref[...].astype(F32)
        xhat, rstd, vn32 = _sgu_norm(sv, g_ref[...], b_ref[...])
        vn = vn32.astype(BF16)
        w_stack_v = [w_ref[p] for p in range(n_pair)]
        db = jnp.zeros((SGU_LEN, SGU_W), F32)
        for w in range(tm // SGU_LEN):
            win = slice(w * SGU_LEN, (w + 1) * SGU_LEN)
            u = u_ref[win, :].astype(F32)
            dsg_w = dsg_ref[win, :]
            mixed = _sgu_mix(vn[win], w_stack_v, lt64) + bp_ref[...]
            du = (dsg_w * mixed * _gelu_grad(u)).astype(BF16)
            du_ref[win, :] = du
            dut_ref[:, win] = jnp.transpose(du)
            dmixed = dsg_w * _gelu(u)
            db = db + dmixed
            dm16 = dmixed.astype(BF16)
            for p in range(n_pair):
                lanes = slice(p * LANES, (p + 1) * LANES)
                dmp = dm16[:, lanes]
                r = _dot(wt_ref[p], dmp)
                dvn_sc[win, lanes] = jnp.where(lt64, r[:SGU_LEN], r[SGU_LEN:])
                zero = jnp.zeros_like(dmp)
                dm_ab = jnp.concatenate([jnp.where(lt64, dmp, zero), jnp.where(lt64, zero, dmp)], axis=0)
                dw_ref[p] += _dot_nt(dm_ab, vn[win, lanes])
        db_ref[...] += db
        dvn = dvn_sc[...]
        dg_ref[...] += jnp.sum(dvn * xhat, axis=0, keepdims=True)
        dbeta_ref[...] += jnp.sum(dvn, axis=0, keepdims=True)
        dxh = dvn * g_ref[...]
        dsvg = rstd * (dxh - jnp.mean(dxh, axis=-1, keepdims=True) - xhat * jnp.mean(dxh * xhat, axis=-1, keepdims=True))
        dsv = (dsvg * _gelu_grad(sv)).astype(BF16)
        dsv_ref[...] = dsv
        dsvt_ref[...] = jnp.transpose(dsv)

    const2 = lambda i: (0, 0)
    const3 = lambda i: (0, 0, 0)
    return pl.pallas_call(
        body,
        name="sgu_bwd",
        grid=(s_len // tm,),
        in_specs=[
            pl.BlockSpec((tm, SGU_W), lambda i: (i, U_OFF // SGU_W)),
            pl.BlockSpec((tm, SGU_W), lambda i: (i, SV_OFF // SGU_W)),
            pl.BlockSpec((tm, SGU_W), lambda i: (i, 0)),
            pl.BlockSpec((1, SGU_W), const2),
            pl.BlockSpec((1, SGU_W), const2),
            pl.BlockSpec((n_pair, 2 * SGU_LEN, SGU_LEN), const3),
            pl.BlockSpec((n_pair, 2 * SGU_LEN, SGU_LEN), const3),
            pl.BlockSpec((SGU_LEN, SGU_W), const2),
        ],
        out_specs=[
            pl.BlockSpec((tm, SGU_W), lambda i: (i, 0)),
            pl.BlockSpec((tm, SGU_W), lambda i: (i, 0)),
            pl.BlockSpec((n_pair, 2 * SGU_LEN, SGU_LEN), const3),
            pl.BlockSpec((SGU_LEN, SGU_W), const2),
            pl.BlockSpec((1, SGU_W), const2),
            pl.BlockSpec((1, SGU_W), const2),
            pl.BlockSpec((SGU_W, tm), lambda i: (0, i)),
            pl.BlockSpec((SGU_W, tm), lambda i: (0, i)),
        ],
        out_shape=[
            _sds((s_len, SGU_W), BF16), _sds((s_len, SGU_W), BF16), _sds((n_pair, 2 * SGU_LEN, SGU_LEN), F32),
            _sds((SGU_LEN, SGU_W), F32), _sds((1, SGU_W), F32), _sds((1, SGU_W), F32),
            _sds((SGU_W, s_len), BF16), _sds((SGU_W, s_len), BF16),
        ],
        scratch_shapes=[pltpu.VMEM((tm, SGU_W), F32)],
        compiler_params=_params(("arbitrary",)),
    )(prest, prest, dsg, ln_g, ln_b, w_stack, wt_stack, b_pair)


def _attn_bwd(qkv, k_t, do, c_col, lse_row, delta_row, tq, tk, comm=()):
    s_len = qkv.shape[0]
    nq, nk = s_len // tq, s_len // tk
    ratio = tq // tk
    scale = HEAD_DIM ** -0.5
    steps = [(j, i) for j in range(nk) for i in range(j // ratio, nq)]
    j_tab = jnp.asarray([j for j, _ in steps], jnp.int32)
    i_tab = jnp.asarray([i for _, i in steps], jnp.int32)

    comm_arrays, comm_outs, comm_sems = _comm_operands(comm)
    n_ci, n_co = len(comm_arrays), len(comm_outs)

    def body(j_ref, i_ref, q_ref, k_ref, v_ref, kt_ref, do_ref, ck_ref, lse_ref, dl_ref, *rest):
        comm_refs = rest[:n_ci], rest[n_ci + 7:n_ci + 7 + n_co], rest[n_ci + 7 + n_co + 4:]
        dq_ref, dk_ref, dv_ref, dcr_ref, dcc_ref, dkt_ref, dvt_ref = rest[n_ci:n_ci + 7]
        dq_t, dk_acc, dv_acc, dcc_acc = rest[n_ci + 7 + n_co:n_ci + 7 + n_co + 4]
        n = pl.program_id(0)
        j, i = j_ref[n], i_ref[n]

        @pl.when(n == 0)
        def _():
            _comm_phase(comm, "start", *comm_refs)
            dq_t[...] = jnp.zeros_like(dq_t)
            dcr_ref[...] = jnp.zeros_like(dcr_ref)

        @pl.when(i == j // ratio)
        def _():
            dk_acc[...] = jnp.zeros_like(dk_acc)
            dv_acc[...] = jnp.zeros_like(dv_acc)
            dcc_acc[...] = jnp.zeros_like(dcc_acc)

        lt64 = _lane_lt64((tk, LANES))
        first_head = lax.broadcasted_iota(jnp.int32, (2 * tk, tq), 0) < tk
        keep2 = _keep_mask(i, j, tq, tk)
        lane = lax.broadcasted_iota(jnp.int32, (tk, LANES), 1)
        top = lax.broadcasted_iota(jnp.int32, (LANES, tq), 0) < HEAD_DIM
        dcc = jnp.zeros((tk, LANES), F32)
        for p in range(N_HEADS // 2):
            lanes = slice(p * LANES, (p + 1) * LANES)
            a, b = 2 * p, 2 * p + 1
            q_pair = q_ref[:, lanes] * jnp.asarray(scale, BF16)
            do_pair = do_ref[:, lanes].astype(BF16)
            st2 = _pair_scores(q_ref, k_ref, ck_ref, p, lt64, keep2)
            dpt2 = _dot_nt(_stack_heads(v_ref[:, lanes], lt64), do_pair)
            pt2 = jnp.exp(st2 - jnp.where(first_head, lse_ref[a], lse_ref[b]))
            dst2 = pt2 * (dpt2 - jnp.where(first_head, dl_ref[a], dl_ref[b]))
            p16, ds16 = pt2.astype(BF16), dst2.astype(BF16)
            dcr_ref[i, a] += jnp.sum(dst2[:tk], axis=0, keepdims=True)
            dcr_ref[i, b] += jnp.sum(dst2[tk:], axis=0, keepdims=True)
            col_sums = jnp.sum(dst2, axis=1, keepdims=True)
            dcc = jnp.where(lane == a, -col_sums[:tk], jnp.where(lane == b, -col_sums[tk:], dcc))
            dv2 = _dot(p16, do_pair)
            dv_acc[:, lanes] += jnp.where(lt64, dv2[:tk], dv2[tk:])
            dk2 = _dot(ds16, q_pair)
            dk_acc[:, lanes] += jnp.where(lt64, dk2[:tk], dk2[tk:])
            dq2 = _dot(kt_ref[lanes, :], jnp.concatenate([ds16[:tk], ds16[tk:]], axis=1))
            dq_t[i, lanes, :] += jnp.where(top, dq2[:, :tq], dq2[:, tq:])
        dcc_acc[...] += dcc

        @pl.when(j == (i + 1) * ratio - 1)
        def _():
            rows = pl.ds(pl.multiple_of(i * tq, tq), tq)
            for p in range(N_HEADS // 2):
                lanes = slice(p * LANES, (p + 1) * LANES)
                dq_ref[rows, lanes] = (jnp.transpose(dq_t[i, lanes, :]) * scale).astype(BF16)

        @pl.when(i == nq - 1)
        def _():
            dk16, dv16 = dk_acc[...].astype(BF16), dv_acc[...].astype(BF16)
            dk_ref[...] = dk16
            dv_ref[...] = dv16
            dkt_ref[...] = jnp.transpose(dk16)
            dvt_ref[...] = jnp.transpose(dv16)
            dcc_ref[...] = dcc_acc[...]

        if comm:
            @pl.when(n == len(steps) // 2)
            def _():
                _comm_phase(comm, "mid", *comm_refs)

            @pl.when(n == len(steps) - 1)
            def _():
                _comm_phase(comm, "finish", *comm_refs)

    q_map = lambda n, jt, it: (it[n], 0)
    q_stat = lambda n, jt, it: (0, 0, it[n])
    k_map = lambda n, jt, it: (jt[n], 0)
    res = pl.pallas_call(
        body,
        name="attn_bwd",
        grid_spec=pltpu.PrefetchScalarGridSpec(
            num_scalar_prefetch=2,
            grid=(len(steps),),
            in_specs=[
                pl.BlockSpec((tq, FOX_W), q_map),
                pl.BlockSpec((tk, FOX_W), lambda n, jt, it: (jt[n], 1)),
                pl.BlockSpec((tk, FOX_W), lambda n, jt, it: (jt[n], 2)),
                pl.BlockSpec((FOX_W, tk), lambda n, jt, it: (0, jt[n])),
                pl.BlockSpec((tq, FOX_W), q_map),
                pl.BlockSpec((tk, LANES), k_map),
                pl.BlockSpec((N_HEADS, 1, tq), q_stat),
                pl.BlockSpec((N_HEADS, 1, tq), q_stat),
            ] + [_ANY] * n_ci,
            out_specs=[
                pl.BlockSpec((s_len, FOX_W), lambda n, jt, it: (0, 0)),
                pl.BlockSpec((tk, FOX_W), k_map),
                pl.BlockSpec((tk, FOX_W), k_map),
                pl.BlockSpec((nq, N_HEADS, 1, tq), lambda n, jt, it: (0, 0, 0, 0)),
                pl.BlockSpec((tk, LANES), k_map),
                pl.BlockSpec((FOX_W, tk), lambda n, jt, it: (0, jt[n])),
                pl.BlockSpec((FOX_W, tk), lambda n, jt, it: (0, jt[n])),
            ] + [_ANY] * n_co,
            scratch_shapes=[pltpu.VMEM((nq, FOX_W, tq), F32), pltpu.VMEM((tk, FOX_W), F32), pltpu.VMEM((tk, FOX_W), F32),
                            pltpu.VMEM((tk, LANES), F32)] + comm_sems,
        ),
        out_shape=[_sds((s_len, FOX_W), BF16), _sds((s_len, FOX_W), BF16), _sds((s_len, FOX_W), BF16),
                   _sds((nq, N_HEADS, 1, tq), F32), _sds((s_len, LANES), F32),
                   _sds((FOX_W, s_len), BF16), _sds((FOX_W, s_len), BF16)] + comm_outs,
        compiler_params=_params(("arbitrary",)),
    )(j_tab, i_tab, qkv, qkv, qkv, k_t, do, c_col, lse_row, delta_row, *comm_arrays)
    return res[:7], res[7:]


def _forget_bwd(dc_rows, dc_cols, prest, b_f_pad, tc):
    s_len = dc_rows.shape[0]
    nb = s_len // tc

    def body(dcr_ref, dc_ref, f_ref, b_ref, df_ref, db_ref, dft_ref, carry):
        @pl.when(pl.program_id(0) == 0)
        def _():
            carry[...] = jnp.zeros_like(carry)
            db_ref[...] = jnp.zeros_like(db_ref)

        row = lax.broadcasted_iota(jnp.int32, (tc, tc), 0)
        col = lax.broadcasted_iota(jnp.int32, (tc, tc), 1)
        tri = (row <= col).astype(F32)
        dlogf = _dot_f32(tri, dcr_ref[...] + dc_ref[...]) + carry[...]
        carry[...] = dlogf[0:1, :]
        z = f_ref[...] + b_ref[...]
        lane = lax.broadcasted_iota(jnp.int32, (tc, LANES), 1)
        dz = jnp.where(lane < N_HEADS, dlogf * _sigmoid(-z), 0.0)
        df_ref[...] = dz.astype(BF16)
        dft_ref[...] = jnp.transpose(dz).astype(BF16)
        db_ref[...] += jnp.sum(dz, axis=0, keepdims=True)

    rev = lambda i: (nb - 1 - i, 0)
    return pl.pallas_call(
        body,
        name="forget_bwd",
        grid=(nb,),
        in_specs=[
            pl.BlockSpec((tc, LANES), rev),
            pl.BlockSpec((tc, LANES), rev),
            pl.BlockSpec((tc, LANES), rev),
            pl.BlockSpec((1, LANES), lambda i: (0, 0)),
        ],
        out_specs=[pl.BlockSpec((tc, LANES), rev), pl.BlockSpec((1, LANES), lambda i: (0, 0)),
                   pl.BlockSpec((LANES, tc), lambda i: (0, nb - 1 - i))],
        out_shape=[_sds((s_len, LANES), BF16), _sds((1, LANES), F32), _sds((LANES, s_len), BF16)],
        scratch_shapes=[pltpu.VMEM((1, LANES), F32)],
        compiler_params=_params(("arbitrary",)),
    )(dc_rows, dc_cols, prest, b_f_pad)


def _pair_sum(name, g4, recv, idx, tr):
    _, _, r, c = g4.shape

    def body(idx_ref, g_ref, r_ref, p16_ref, own_ref):
        k = pl.program_id(1)
        s = g_ref[...] + r_ref[...]
        p16_ref[...] = s.astype(BF16)

        @pl.when(k == idx_ref[1])
        def _():
            own_ref[...] = s

    return pl.pallas_call(
        body,
        name=name,
        grid_spec=pltpu.PrefetchScalarGridSpec(
            num_scalar_prefetch=1,
            grid=(r // tr, 4),
            in_specs=[
                pl.BlockSpec((None, None, tr, c), lambda i, k, idx: (k, idx[0], i, 0)),
                pl.BlockSpec((None, tr, c), lambda i, k, idx: (k, i, 0)),
            ],
            out_specs=[
                pl.BlockSpec((None, tr, c), lambda i, k, idx: (k, i, 0)),
                pl.BlockSpec((tr, c), lambda i, k, idx: (i, 0)),
            ],
        ),
        out_shape=[_sds((4, r, c), BF16), _sds((r, c), F32)],
        compiler_params=_params(("parallel", "arbitrary")),
    )(idx, g4, recv)


def _adamw_math(w, g, m, v):
    m2 = ADAM_B1 * m + (1.0 - ADAM_B1) * g
    v2 = ADAM_B2 * v + (1.0 - ADAM_B2) * (g * g)
    m_hat = m2 / (1.0 - ADAM_B1 ** ADAM_STEP)
    v_hat = v2 / (1.0 - ADAM_B2 ** ADAM_STEP)
    delta = -ADAM_LR * (m_hat / (jnp.sqrt(v_hat) + ADAM_EPS) + ADAM_WD * w)
    return delta, m2, v2


def _adamw_shard(name, own, recv, w, m, v, tr):
    r, c = own.shape

    def body(own_ref, recv_ref, w_ref, m_ref, v_ref, g_ref, d_ref, m2_ref, v2_ref):
        g = own_ref[...]
        for k in range(3):
            g = g + recv_ref[k].astype(F32)
        delta, m2, v2 = _adamw_math(w_ref[...], g, m_ref[...], v_ref[...])
        g_ref[...] = g
        d_ref[...] = delta
        m2_ref[...] = m2
        v2_ref[...] = v2

    spec = pl.BlockSpec((tr, c), lambda i: (i, 0))
    return pl.pallas_call(
        body,
        name=name,
        grid=(r // tr,),
        in_specs=[spec, pl.BlockSpec((3, tr, c), lambda i: (0, i, 0)), spec, spec, spec],
        out_specs=[spec] * 4,
        out_shape=[_sds((r, c), F32)] * 4,
        compiler_params=_params(("parallel",)),
    )(own, recv, w, m, v)


def _adamw_small(name, gathered, first_row, w, m, v):
    r = w.shape[0]
    assert first_row % r == 0

    def body(ga_ref, w_ref, m_ref, v_ref, g_ref, d_ref, m2_ref, v2_ref):
        g = ga_ref[0]
        for k in range(1, N_DEV):
            g = g + ga_ref[k]
        delta, m2, v2 = _adamw_math(w_ref[...], g, m_ref[...], v_ref[...])
        g_ref[...] = g
        d_ref[...] = delta
        m2_ref[...] = m2
        v2_ref[...] = v2

    spec = pl.BlockSpec((r, LANES), lambda i: (0, 0))
    return pl.pallas_call(
        body,
        name=name,
        grid=(1,),
        in_specs=[pl.BlockSpec((N_DEV, r, LANES), lambda i: (0, first_row // r, 0)), spec, spec, spec],
        out_specs=[spec] * 4,
        out_shape=[_sds((r, LANES), F32)] * 4,
        compiler_params=_params(("arbitrary",)),
    )(gathered, w, m, v)


_TINY_EARLY = (("b_sgu", (1, SGU_G, SGU_LEN)), ("norm2_g", (1, D_MODEL)), ("normf_g", (D_MODEL,)),
               ("ln_v_g", (1, SGU_W)), ("ln_v_b", (1, SGU_W)))
_TINY_LATE = (("b_f", (1, N_HEADS)), ("norm1_g", (1, D_MODEL)), ("loss", ()))


def _pack_rows(values):
    rows = []
    for val in values:
        flat = val.reshape(-1).astype(F32)
        pad = (-flat.shape[0]) % LANES
        rows.append(jnp.pad(flat, (0, pad)).reshape(-1, LANES))
    packed = jnp.concatenate(rows, axis=0)
    return jnp.pad(packed, ((0, (-packed.shape[0]) % 8), (0, 0)))


def _unpack_rows(packed, group):
    out, row = {}, 0
    for name, shape in group:
        size = math.prod(shape)
        n_rows = -(-size // LANES)
        out[name] = packed[row:row + n_rows].reshape(-1)[:size].reshape(shape)
        row += n_rows
    return out


def kernel(x, norm1_g, w_in, b_f, ln_v_g, ln_v_b, w_sgu, b_sgu, w_a, w_b, w_o, norm2_g, w_up, w_down, normf_g, loss_target, m_norm1_g, m_w_in, m_b_f, m_ln_v_g, m_ln_v_b, m_w_sgu, m_b_sgu, m_w_a, m_w_b, m_w_o, m_norm2_g, m_w_up, m_w_down, m_normf_g, v_norm1_g, v_w_in, v_b_f, v_ln_v_g, v_ln_v_b, v_w_sgu, v_b_sgu, v_w_a, v_w_b, v_w_o, v_norm2_g, v_w_up, v_w_down, v_normf_g):
    xs = x[0]
    target = loss_target[0]
    s_len, d = xs.shape
    tm = min(512, s_len)
    tl = min(1024, s_len)
    tr = min(512, s_len)
    ta = min(512, s_len)
    tc = min(512, s_len)

    w_in_t = jnp.transpose(w_in[0])
    lin = (IN_SHARD * d // LANES, LANES)
    big = (w_in_t.reshape(lin), w_a[0], w_b[0], w_o[0], w_up[0], w_down[0])
    h, h_t, r1, (w_in_g,) = _rms_fwd("rms1", xs, norm1_g, tm, comm=[_gather_plan(w_in_t.astype(BF16))])
    w_in_f = w_in_g.reshape(IN_COLS, d)
    later_shards = [jnp.transpose(w_a[0]), jnp.transpose(w_b[0]), w_o[0], jnp.transpose(w_up[0]), w_down[0]]
    later_plans = [_gather_plan(w.astype(BF16)) for w in later_shards]

    def unflatten(gathered):
        return [g.reshape(N_DEV * g.shape[1], g.shape[2]) for g in gathered]

    w_qkv = w_in_f[:QKV_W]
    f_lo = QKV_W
    u_lo = f_lo + N_HEADS
    w_rest = jnp.concatenate([w_in_f[u_lo:], jnp.pad(w_in_f[f_lo:u_lo], ((0, LANES - N_HEADS), (0, 0)))], axis=0)

    chunk_id = jnp.arange(SGU_LEN) // CHUNK
    sgu_mask = chunk_id[None, :] <= chunk_id[:, None]
    w_masked = jnp.where(sgu_mask[None], w_sgu[0], 0.0)
    w_stack = w_masked.reshape(SGU_G // 2, 2 * SGU_LEN, SGU_LEN).astype(BF16)
    wt_stack = jnp.transpose(w_masked, (0, 2, 1)).reshape(SGU_G // 2, 2 * SGU_LEN, SGU_LEN).astype(BF16)
    b_pair = jnp.transpose(jnp.repeat(b_sgu[0], SGU_W // SGU_G, axis=0))
    b_f_pad = jnp.pad(b_f, ((0, 0), (0, LANES - N_HEADS)))
    head_sel = (jnp.arange(FOX_W)[:, None] // HEAD_DIM == jnp.arange(LANES)[None, :]).astype(F32)

    def store(dtype):
        def epi(accs, ex, out):
            out[0][...] = accs[0].astype(dtype)
        return epi

    def qkv_epi(accs, ex, out):
        tile = accs[0].astype(BF16)
        out[0][...] = tile
        for col, ref in ((1, out[1]), (2, out[2])):
            @pl.when(pl.program_id(1) == col)
            def _():
                ref[...] = jnp.transpose(tile)

    t_spec = pl.BlockSpec((FOX_W, tl), lambda i, j: (0, i))
    qkv, k_t, v_t = _mm("proj_qkv", [(h, w_qkv, True, None)], [],
                        [(_sds((s_len, QKV_W), BF16), _tile(tl, FOX_W)), (_sds((FOX_W, s_len), BF16), t_spec),
                         (_sds((FOX_W, s_len), BF16), t_spec)],
                        qkv_epi, m=s_len, tm=tl, n=QKV_W, tn=FOX_W, arbitrary=True)
    rest_tn = 640
    f_tile, f_lane = F_OFF // rest_tn, F_OFF % rest_tn

    def rest_epi(accs, ex, out):
        out[0][...] = accs[0].astype(BF16)

        @pl.when(pl.program_id(1) == f_tile)
        def _():
            out[1][...] = accs[0][:, f_lane:f_lane + LANES]

    prest, f_logit = _mm("proj_rest", [(h, w_rest, True, None)], [],
                         [(_sds((s_len, REST_W), BF16), _tile(tl, rest_tn)), (_sds((s_len, LANES), F32), _row(tl, LANES))],
                         rest_epi, m=s_len, tm=tl, n=REST_W, tn=rest_tn, arbitrary=True)

    c_col = _forget_cumsum(f_logit, b_f_pad, tc)
    o, o_t, lse_row, later_g = _attn_fwd(qkv, v_t, c_col, ta, ta, comm=later_plans)
    w_a_t, w_b_t, w_o_f, w_up_t, w_down_f = unflatten(later_g)
    sg, sg_t = _sgu_fwd(prest, ln_v_g, ln_v_b, w_stack, b_pair, tm)

    def merge_epi(accs, ex, out):
        ya, yb = accs
        sa, sb = _sigmoid(ex[0][...].astype(F32)), _sigmoid(ex[1][...].astype(F32))
        merged = (sa * ya + sb * yb).astype(BF16)
        out[0][...] = merged
        out[1][...] = ya.astype(BF16)
        out[2][...] = yb.astype(BF16)
        out[3][...] = jnp.transpose(merged)

    merged, ya, yb, merged_t = _mm(
        "merge", [(o, w_a_t, True, None), (sg, w_b_t, True, None)],
        [(prest, _tile(tm, d, GA_OFF // d)), (prest, _tile(tm, d, GB_OFF // d))],
        [(_sds((s_len, d), BF16), _tile(tm, d))] * 3 + [(_sds((d, s_len), BF16), _tile_t(tm, d))],
        merge_epi, m=s_len, tm=tm, n=d, tn=d)

    def resid_epi(accs, ex, out):
        x1v = ex[0][...] + accs[0]
        out[0][...] = x1v
        r = lax.rsqrt(jnp.mean(x1v * x1v, axis=-1, keepdims=True) + EPS)
        h2v = (x1v * r * ex[1][...]).astype(BF16)
        out[1][...] = h2v
        out[2][...] = jnp.transpose(h2v)
        out[3][...] = r

    x1, h2, h2_t, r2 = _mm(
        "out_proj", [(merged, w_o_f, False, None)], [(xs, _tile(tm, d)), (norm2_g, _whole((1, d)))],
        [(_sds((s_len, d), F32), _tile(tm, d)), (_sds((s_len, d), BF16), _tile(tm, d)),
         (_sds((d, s_len), BF16), _tile_t(tm, d)), (_sds((s_len, 1), F32), _row(tm, 1))],
        resid_epi, m=s_len, tm=tm, n=d, tn=d)

    def up_epi(accs, ex, out):
        act = jnp.square(jnp.maximum(accs[0], 0.0)).astype(BF16)
        out[0][...] = act
        out[1][...] = jnp.transpose(act)

    act, act_t = _mm(
        "mlp_up", [(h2, w_up_t, True, None)], [],
        [(_sds((s_len, D_FF), BF16), _tile(tl, 512)), (_sds((D_FF, s_len), BF16), _tile_t(tl, 512))],
        up_epi, m=s_len, tm=tl, n=D_FF, tn=512)

    def first_step():
        return jnp.logical_and(pl.program_id(0) == 0, pl.program_id(1) == 0)

    def accumulate(ref, val):
        @pl.when(first_step())
        def _():
            ref[...] = val

        @pl.when(jnp.logical_not(first_step()))
        def _():
            ref[...] += val

    def final_epi(accs, ex, out):
        x1_ref, t_ref, g_ref = ex
        x2 = x1_ref[...] + accs[0]
        rf = lax.rsqrt(jnp.mean(x2 * x2, axis=-1, keepdims=True) + EPS)
        xh = x2 * rf
        gf = g_ref[...]
        err = xh * gf - t_ref[...]
        dy = err * (1.0 / d)
        dx2 = _rms_bwd(xh, rf, gf, dy)
        out[0][...] = dx2
        accumulate(out[1], jnp.sum(dy * xh, axis=0, keepdims=True))
        part = 0.5 * jnp.sum(jnp.sum(err * err, axis=-1, keepdims=True) * (1.0 / d), axis=0, keepdims=True)
        accumulate(out[2], jnp.broadcast_to(part, (1, LANES)))
        out[3][...] = dx2.astype(BF16)

    gf2 = normf_g.reshape(1, d)
    dx2, g_normf, loss_part, dx2_16 = _mm(
        "mlp_down_loss", [(act, w_down_f, False, None)],
        [(x1, _row(tr, d)), (target, _row(tr, d)), (gf2, _whole((1, d)))],
        [(_sds((s_len, d), F32), _row(tr, d)), (_sds((1, d), F32), _whole((1, d))), (_sds((1, LANES), F32), _whole((1, LANES))),
         (_sds((s_len, d), BF16), _row(tr, d))],
        final_epi, m=s_len, tm=tr, n=d, tn=d, arbitrary=True)

    def dact_epi(accs, ex, out):
        out[0][...] = (accs[0] * (2.0 * jnp.sqrt(ex[0][...].astype(F32)))).astype(BF16)

    (da,) = _mm("mlp_down_bwd", [(dx2_16, w_down_f, True, None)], [(act, _tile(tl, 512))],
                [(_sds((s_len, D_FF), BF16), _tile(tl, 512))], dact_epi, m=s_len, tm=tl, n=D_FF, tn=512)
    g_down = _grad_w("grad_w_down", act_t, dx2_16, tk=1024, tn=d, ts=tl)
    g_up = _grad_w("grad_w_up", h2_t, da, tk=d, tn=1024, ts=tl, block_cols=D_FF // N_DEV)

    def dh2_epi(accs, ex, out):
        x1_ref, r_ref, g_ref, dx2_ref = ex
        r = r_ref[...]
        xh = x1_ref[...] * r
        dh2 = accs[0]
        out[0][...] = dx2_ref[...] + _rms_bwd(xh, r, g_ref[...], dh2)
        accumulate(out[1], jnp.sum(dh2 * xh, axis=0, keepdims=True))

    my_c = lax.axis_index("c")
    my_chip = 2 * lax.axis_index("x") + lax.axis_index("y")
    idx = jnp.stack([my_c, my_chip]).astype(jnp.int32)
    parts16, owns = {}, {}

    def split_cores(g8):
        return g8.reshape((4, 2) + g8.shape[1:])

    def row_tile(r):
        return 512 if r % 512 == 0 else r

    def pair_sums(names, grads4, from_sibling):
        for name, g4, recv in zip(names, grads4, from_sibling):
            parts16[name], owns[name] = _pair_sum("grad_pair_sum_" + name, g4, recv, idx, row_tile(g4.shape[2]))

    grads4_mlp = [split_cores(g_up), split_cores(g_down.reshape(N_DEV, D_FF // N_DEV, d))]
    (dx1, g_norm2), from_sibling = _mm(
        "mlp_up_bwd", [(da, w_up_t, False, None)],
        [(x1, _row(tr, d)), (r2, _row(tr, 1)), (norm2_g, _whole((1, d))), (dx2, _row(tr, d))],
        [(_sds((s_len, d), F32), _row(tr, d)), (_sds((1, d), F32), _whole((1, d)))],
        dh2_epi, m=s_len, tm=tr, n=d, tn=d, arbitrary=True, comm=[_pair_exchange_plan(grads4_mlp)])
    pair_sums(("w_up", "w_down"), grads4_mlp, from_sibling)

    def dmerge_epi(accs, ex, out):
        dm = accs[0]
        sa, sb = _sigmoid(ex[0][...].astype(F32)), _sigmoid(ex[1][...].astype(F32))
        out[0][...] = (dm * sa).astype(BF16)
        out[1][...] = (dm * sb).astype(BF16)
        dga = (dm * ex[2][...] * sa * (1.0 - sa)).astype(BF16)
        dgb = (dm * ex[3][...] * sb * (1.0 - sb)).astype(BF16)
        out[2][...] = dga
        out[3][...] = dgb
        out[4][...] = jnp.transpose(dga)
        out[5][...] = jnp.transpose(dgb)

    dya, dyb, dga, dgb, dga_t, dgb_t = _mm(
        "out_proj_bwd", [(dx1, w_o_f, True, None)],
        [(prest, _tile(tm, d, GA_OFF // d)), (prest, _tile(tm, d, GB_OFF // d)), (ya, _tile(tm, d)), (yb, _tile(tm, d))],
        [(_sds((s_len, d), BF16), _tile(tm, d))] * 4 + [(_sds((d, s_len), BF16), _tile_t(tm, d))] * 2,
        dmerge_epi, m=s_len, tm=tm, n=d, tn=d)
    g_o = _grad_w("grad_w_o", merged_t, dx1, tk=d, tn=d, ts=tl).reshape(N_DEV, d // N_DEV, d)
    def col_blocks(g):
        return jnp.transpose(g.reshape(g.shape[0], N_DEV, g.shape[1] // N_DEV), (1, 0, 2))

    g_a = col_blocks(_grad_w("grad_w_a", o_t, dya, tk=FOX_W, tn=d, ts=tl))
    g_b = col_blocks(_grad_w("grad_w_b", sg_t, dyb, tk=SGU_W, tn=d, ts=tl))

    def do_epi(accs, ex, out):
        do = accs[0]
        out[0][...] = do
        out[1][...] = _dot_f32(do * ex[0][...], ex[1][...])

    grads4_mix = [split_cores(g) for g in (g_a, g_b, g_o)]
    (do, delta), from_sibling = _mm(
        "attn_out_bwd", [(dya, w_a_t, False, None)], [(o, _row(tm, FOX_W)), (head_sel, _whole((FOX_W, LANES)))],
        [(_sds((s_len, FOX_W), F32), _row(tm, FOX_W)), (_sds((s_len, LANES), F32), _row(tm, LANES))],
        do_epi, m=s_len, tm=tm, n=FOX_W, tn=FOX_W, comm=[_pair_exchange_plan(grads4_mix)])
    pair_sums(("w_a", "w_b", "w_o"), grads4_mix, from_sibling)
    (dsg,) = _mm("sgu_out_bwd", [(dyb, w_b_t, False, None)], [], [(_sds((s_len, SGU_W), F32), _tile(tm, SGU_W))],
                 store(F32), m=s_len, tm=tm, n=SGU_W, tn=SGU_W)

    du, dsv, dw_pairs, db_pos, g_ln_g, g_ln_b, du_t, dsv_t = _sgu_bwd(
        prest, dsg, ln_v_g, ln_v_b, w_stack, wt_stack, b_pair, tm)
    g_w_sgu = jnp.where(sgu_mask[None], dw_pairs.reshape(SGU_G, SGU_LEN, SGU_LEN), 0.0)
    g_b_sgu = jnp.transpose(jnp.sum(db_pos.reshape(SGU_LEN, SGU_G, SGU_W // SGU_G), axis=-1))

    delta_row = jnp.transpose(delta[:, :N_HEADS]).reshape(N_HEADS, 1, s_len)
    early = ("w_a", "w_b", "w_o", "w_up", "w_down")
    small_early = _pack_rows((g_w_sgu, g_b_sgu, g_norm2, g_normf, g_ln_g, g_ln_b))
    (dq, dk, dv, dc_rows_blk, dc_cols, dk_t, dv_t), (small_early_all, *from_chips_early) = _attn_bwd(
        qkv, k_t, do, c_col, lse_row, delta_row, ta, ta,
        comm=[_gather_plan(small_early), _chip_exchange_plan([parts16[n] for n in early])])
    dc_rows = jnp.transpose(dc_rows_blk.reshape(s_len // ta, N_HEADS, ta), (0, 2, 1)).reshape(s_len, N_HEADS)
    dc_rows = jnp.pad(dc_rows, ((0, 0), (0, LANES - N_HEADS)))
    dfl, g_bf, dfl_t = _forget_bwd(dc_rows, dc_cols, f_logit, b_f_pad, tc)

    dp_t = (jnp.transpose(dq), dk_t, dv_t, dfl_t, du_t, dsv_t, dga_t, dgb_t)
    g_in_rows = [_grad_w("grad_w_in_%d" % k, seg_t, h, tk=seg_t.shape[0], tn=d, ts=tl) for k, seg_t in enumerate(dp_t)]
    g_in_rows[3] = g_in_rows[3][:N_HEADS]
    g_in = jnp.concatenate(g_in_rows, axis=0).reshape((N_DEV,) + lin)

    def dx_epi(accs, ex, out):
        x_ref, r_ref, g_ref, dx1_ref = ex
        dh = accs[0]
        for extra in accs[1:]:
            dh = dh + extra
        r = r_ref[...]
        xh = x_ref[...] * r
        out[0][...] = dx1_ref[...] + _rms_bwd(xh, r, g_ref[...], dh)
        accumulate(out[1], jnp.sum(dh * xh, axis=0, keepdims=True))

    rest_cols = ((du, U_OFF, 512), (dsv, SV_OFF, 512), (dga, GA_OFF, 1024), (dgb, GB_OFF, 1024), (dfl, F_OFF, LANES))
    dx_pairs = [(seg, w_qkv, False, (512 * k, 512 * (k + 1))) for k, seg in enumerate((dq, dk, dv))]
    dx_pairs += [(seg, w_rest, False, (lo, lo + width)) for seg, lo, width in rest_cols]
    grads4_in = [split_cores(g_in)]
    pair_sums(("w_in",), grads4_in, _run_comm("grad_pair_exchange_w_in", [_pair_exchange_plan(grads4_in)]))
    (grad_x, g_norm1), (from_chips_in,) = _mm(
        "proj_bwd", dx_pairs,
        [(xs, _row(tr, d)), (r1, _row(tr, 1)), (norm1_g, _whole((1, d))), (dx1, _row(tr, d))],
        [(_sds((s_len, d), F32), _row(tr, d)), (_sds((1, d), F32), _whole((1, d)))],
        dx_epi, m=s_len, tm=tr, n=d, tn=d, arbitrary=True, comm=[_chip_exchange_plan([parts16["w_in"]])])
    small_late = _pack_rows((g_bf[:, :N_HEADS], g_norm1, loss_part[0, 0]))
    (small_late_all,) = _run_comm("gather_last_grads", [_gather_plan(small_late)])
    from_chips = dict(zip(early, from_chips_early), w_in=from_chips_in)

    names = ("w_in", "w_a", "w_b", "w_o", "w_up", "w_down")
    moments_m = (m_w_in, m_w_a, m_w_b, m_w_o, m_w_up, m_w_down)
    moments_v = (v_w_in, v_w_a, v_w_b, v_w_o, v_w_up, v_w_down)
    big_out = {}
    for name, w, m, v in zip(names, big, moments_m, moments_v):
        own = owns[name]
        transposed = name == "w_in"
        m0, v0 = (jnp.transpose(m[0]).reshape(lin), jnp.transpose(v[0]).reshape(lin)) if transposed else (m[0], v[0])
        res = _adamw_shard("adamw_" + name, own, from_chips[name], w, m0, v0, row_tile(own.shape[0]))
        big_out[name] = [(jnp.transpose(t.reshape(IN_SHARD, d)) if transposed else t)[None] for t in res]

    zero = jnp.zeros((), F32)
    sgu_rows = (SGU_G * SGU_LEN, LANES)
    res_sgu = _adamw_small("adamw_w_sgu", small_early_all, 0, w_sgu.reshape(sgu_rows), m_w_sgu.reshape(sgu_rows),
                           v_w_sgu.reshape(sgu_rows))
    small_out = {"w_sgu": [t.reshape(w_sgu.shape) for t in res_sgu]}
    res_early = _adamw_small(
        "adamw_tiny_early", small_early_all, sgu_rows[0], _pack_rows((b_sgu, norm2_g, normf_g, ln_v_g, ln_v_b)),
        _pack_rows((m_b_sgu, m_norm2_g, m_normf_g, m_ln_v_g, m_ln_v_b)),
        _pack_rows((v_b_sgu, v_norm2_g, v_normf_g, v_ln_v_g, v_ln_v_b)))
    res_late = _adamw_small(
        "adamw_tiny_late", small_late_all, 0, _pack_rows((b_f, norm1_g, zero)), _pack_rows((m_b_f, m_norm1_g, zero)),
        _pack_rows((v_b_f, v_norm1_g, zero)))
    for res, group in ((res_early, _TINY_EARLY), (res_late, _TINY_LATE)):
        unpacked = [_unpack_rows(t, group) for t in res]
        small_out.update({name: [u[name] for u in unpacked] for name, _ in group})
    loss = small_out["loss"][0]

    order = ("norm1_g", "w_in", "b_f", "ln_v_g", "ln_v_b", "w_sgu", "b_sgu", "w_a", "w_b", "w_o", "norm2_g", "w_up",
             "w_down", "normf_g")
    table = {**big_out, **small_out}
    outs = [loss, grad_x[None]]
    for kind in range(4):
        outs += [table[n][kind] for n in order]
    return tuple(outs)
```

```python
import math

import jax
import jax.numpy as jnp
from jax import lax
from jax.experimental import pallas as pl
from jax.experimental.pallas import tpu as pltpu

F32 = jnp.float32
BF16 = jnp.bfloat16

N_DEV = 8
D_MODEL = 1024
N_HEADS = 8
HEAD_DIM = 64
FOX_W = N_HEADS * HEAD_DIM
SGU_G = 8
SGU_W = 512
SGU_LEN = 128
CHUNK = 64
D_FF = 4 * D_MODEL
IN_COLS = 3 * FOX_W + N_HEADS + 2 * SGU_W + 2 * D_MODEL
IN_SHARD = IN_COLS // N_DEV
LANES = 128
QKV_W = 3 * FOX_W
U_OFF, SV_OFF, GA_OFF, GB_OFF, F_OFF = 0, 512, 1024, 2048, 3072
REST_W = F_OFF + LANES
EPS = 1e-6
NEG = -1e30

ADAM_LR = 0.001
ADAM_B1 = 0.9
ADAM_B2 = 0.999
ADAM_EPS = 1e-08
ADAM_WD = 0.01
ADAM_STEP = 10

VMEM_LIMIT = 56 * 1024 * 1024
MESH = pl.DeviceIdType.MESH


def _params(sem=None):
    return pltpu.CompilerParams(dimension_semantics=sem, vmem_limit_bytes=VMEM_LIMIT)


def _dot(a, b):
    return jnp.dot(a, b, preferred_element_type=F32)


def _dot_nt(a, b):
    return lax.dot_general(a, b, (((1,), (1,)), ((), ())), preferred_element_type=F32)


def _dot_f32(a, b):
    return jnp.dot(a, b, preferred_element_type=F32, precision=lax.Precision.HIGHEST)


def _sigmoid(x):
    return 1.0 / (1.0 + jnp.exp(-x))


def _log_sigmoid(z):
    return jnp.minimum(z, 0.0) - jnp.log(1.0 + jnp.exp(-jnp.abs(z)))


_GELU_K = math.sqrt(2.0 / math.pi)
_GELU_C = 0.044715


def _gelu(x):
    t = jnp.tanh(_GELU_K * (x + _GELU_C * (x * x * x)))
    return 0.5 * x * (1.0 + t)


def _gelu_grad(x):
    x2 = x * x
    t = jnp.tanh(_GELU_K * (x + _GELU_C * (x2 * x)))
    return 0.5 * (1.0 + t) + 0.5 * x * (1.0 - t * t) * (_GELU_K * (1.0 + 3.0 * _GELU_C * x2))


def _rms_bwd(xh, r, g, dy):
    gy = dy * g
    return r * (gy - xh * jnp.mean(xh * gy, axis=-1, keepdims=True))


def _lane_lt64(shape):
    return lax.broadcasted_iota(jnp.int32, shape, len(shape) - 1) < HEAD_DIM


class _Comm:
    def __init__(self, arrays, out_shapes, sems, start, finish, mid=None):
        self.arrays, self.out_shapes, self.sems = list(arrays), list(out_shapes), list(sems)
        self.start, self.mid, self.finish = start, mid, finish


def _comm_phase(plans, phase, in_refs, out_refs, sem_refs):
    ia = io = ks = 0
    for plan in plans:
        na, no, ns = len(plan.arrays), len(plan.out_shapes), len(plan.sems)
        fn = getattr(plan, phase)
        if fn is not None:
            fn(in_refs[ia:ia + na], out_refs[io:io + no], sem_refs[ks:ks + ns])
        ia, io, ks = ia + na, io + no, ks + ns


def _comm_operands(plans):
    arrays = [a for plan in plans for a in plan.arrays]
    out_shapes = [o for plan in plans for o in plan.out_shapes]
    sems = [s for plan in plans for s in plan.sems]
    return arrays, out_shapes, sems


_ANY = pl.BlockSpec(memory_space=pl.ANY)


def _run_comm(name, plans):
    arrays, out_shapes, sems = _comm_operands(plans)
    n_in, n_out = len(arrays), len(out_shapes)

    def body(*refs):
        parts = refs[:n_in], refs[n_in:n_in + n_out], refs[n_in + n_out:]
        for phase in ("start", "mid", "finish"):
            _comm_phase(plans, phase, *parts)

    return pl.pallas_call(
        body, name=name, out_shape=out_shapes, in_specs=[_ANY] * n_in, out_specs=[_ANY] * n_out, scratch_shapes=sems,
    )(*arrays)


def _gather_plan(shard):
    def setup(ins, outs, sems):
        (x_ref,), (out_ref,), (send_sems, recv_sems, local_sem) = ins, outs, sems
        x, y, c = lax.axis_index("x"), lax.axis_index("y"), lax.axis_index("c")
        me, sibling = (x, y, c), (x, y, 1 - c)
        chips = [(1 - x, y), (x, 1 - y), (1 - x, 1 - y)]

        def rows(px, py, pc):
            return out_ref.at[4 * px + 2 * py + pc]

        def copy(k, block, to, src=None):
            return pltpu.make_async_remote_copy(
                src_ref=rows(*block) if src is None else src,
                dst_ref=rows(*block),
                send_sem=send_sems.at[k],
                recv_sem=recv_sems.at[k],
                device_id=to,
                device_id_type=MESH,
            )

        mine = pltpu.make_async_copy(x_ref, rows(*me), local_sem)
        first = [copy(0, me, sibling, src=x_ref)]
        first += [copy(1 + j, me, (*chip, c), src=x_ref) for j, chip in enumerate(chips)]
        passed = [copy(4 + j, (*chip, c), sibling) for j, chip in enumerate(chips)]
        landed = [copy(1 + j, (*chip, c), me) for j, chip in enumerate(chips)]
        from_sibling = [copy(0, sibling, me)] + [copy(4 + j, (*chip, 1 - c), me) for j, chip in enumerate(chips)]
        return mine, first, passed, landed, from_sibling

    def start(ins, outs, sems):
        mine, first, _, _, _ = setup(ins, outs, sems)
        mine.start()
        for cp in first:
            cp.start()

    def mid(ins, outs, sems):
        _, _, passed, landed, _ = setup(ins, outs, sems)
        for arrived, onward in zip(landed, passed):
            arrived.wait_recv()
            onward.start()

    def finish(ins, outs, sems):
        mine, first, passed, _, from_sibling = setup(ins, outs, sems)
        for cp in from_sibling:
            cp.wait_recv()
        for cp in first + passed:
            cp.wait_send()
        mine.wait()

    return _Comm([shard], [jax.ShapeDtypeStruct((N_DEV,) + shard.shape, shard.dtype)],
                 [pltpu.SemaphoreType.DMA((7,)), pltpu.SemaphoreType.DMA((7,)), pltpu.SemaphoreType.DMA],
                 start, finish, mid)


def _start_all(copies):
    for cp in copies:
        cp.start()


def _wait_all(copies):
    for cp in copies:
        cp.wait_recv()
    for cp in copies:
        cp.wait_send()


def _pair_exchange_plan(grads):
    n = len(grads)

    def copies(ins, outs, sems):
        send_sems, recv_sems = sems
        x, y, c = lax.axis_index("x"), lax.axis_index("y"), lax.axis_index("c")
        return [
            pltpu.make_async_remote_copy(
                src_ref=ins[k].at[:, 1 - c],
                dst_ref=outs[k],
                send_sem=send_sems.at[k],
                recv_sem=recv_sems.at[k],
                device_id=(x, y, 1 - c),
                device_id_type=MESH,
            )
            for k in range(n)
        ]

    return _Comm(grads, [jax.ShapeDtypeStruct((4,) + g.shape[2:], g.dtype) for g in grads],
                 [pltpu.SemaphoreType.DMA((n,)), pltpu.SemaphoreType.DMA((n,))],
                 lambda *refs: _start_all(copies(*refs)), lambda *refs: _wait_all(copies(*refs)))


def _chip_exchange_plan(parts):
    n = len(parts)

    def copies(ins, outs, sems):
        send_sems, recv_sems = sems
        x, y, c = lax.axis_index("x"), lax.axis_index("y"), lax.axis_index("c")
        chips = [(1 - x, y), (x, 1 - y), (1 - x, 1 - y)]
        return [
            pltpu.make_async_remote_copy(
                src_ref=ins[k].at[2 * px + py],
                dst_ref=outs[k].at[j],
                send_sem=send_sems.at[3 * k + j],
                recv_sem=recv_sems.at[3 * k + j],
                device_id=(px, py, c),
                device_id_type=MESH,
            )
            for k in range(n) for j, (px, py) in enumerate(chips)
        ]

    return _Comm(parts, [jax.ShapeDtypeStruct((3,) + p.shape[1:], p.dtype) for p in parts],
                 [pltpu.SemaphoreType.DMA((3 * n,)), pltpu.SemaphoreType.DMA((3 * n,))],
                 lambda *refs: _start_all(copies(*refs)), lambda *refs: _wait_all(copies(*refs)))


def _mm(name, pairs, extras, outs, epi, *, m, tm, n, tn, arbitrary=False, comm=()):
    nj = n // tn
    a_arrays, a_specs, b_arrays, b_specs, b_index = [], [], [], [], []
    for a, b, nt, cols in pairs:
        a_arrays.append(a)
        a_specs.append(pl.BlockSpec((tm, a.shape[1]), lambda i, j: (i, 0)))
        known = [k for k, other in enumerate(b_arrays) if other is b]
        if known:
            b_index.append(known[0])
            continue
        b_index.append(len(b_arrays))
        b_arrays.append(b)
        if cols is not None:
            assert nj == 1
            b_specs.append(pl.BlockSpec(b.shape, lambda i, j: (0, 0)))
        elif nt:
            b_specs.append(pl.BlockSpec((tn, b.shape[1]), lambda i, j: (j, 0)))
        else:
            b_specs.append(pl.BlockSpec((b.shape[0], tn), lambda i, j: (0, j)))
    comm_arrays, comm_outs, comm_sems = _comm_operands(comm)
    arrays = a_arrays + b_arrays + [arr for arr, _ in extras] + comm_arrays
    in_specs = a_specs + b_specs + [spec for _, spec in extras] + [_ANY] * len(comm_arrays)
    n_a, n_b, n_extras, n_ci, n_out, n_co = len(a_arrays), len(b_arrays), len(extras), len(comm_arrays), len(outs), len(comm_outs)
    ni = m // tm

    def body(*refs):
        a_refs = refs[:n_a]
        b_refs = refs[n_a:n_a + n_b]
        ex = refs[n_a + n_b:n_a + n_b + n_extras]
        n_in = n_a + n_b + n_extras + n_ci
        comm_refs = refs[n_in - n_ci:n_in], refs[n_in + n_out:n_in + n_out + n_co], refs[n_in + n_out + n_co:]
        out = refs[n_in:n_in + n_out]
        if comm:
            @pl.when(jnp.logical_and(pl.program_id(0) == 0, pl.program_id(1) == 0))
            def _():
                _comm_phase(comm, "start", *comm_refs)

        accs = []
        for p, (_, _, nt, cols) in enumerate(pairs):
            av = a_refs[p][...]
            if av.dtype != BF16:
                av = av.astype(BF16)
            b_ref = b_refs[b_index[p]]
            if cols is None:
                bv = b_ref[...]
            else:
                bv = b_ref[:, cols[0]:cols[1]] if nt else b_ref[cols[0]:cols[1], :]
            accs.append(_dot_nt(av, bv) if nt else _dot(av, bv))
        epi(accs, ex, out)
        if comm:
            mid_row = ni // 2 if ni >= 3 else ni - 1
            mid_col = 0 if ni >= 3 else nj - 1

            @pl.when(jnp.logical_and(pl.program_id(0) == mid_row, pl.program_id(1) == mid_col))
            def _():
                _comm_phase(comm, "mid", *comm_refs)

            @pl.when(jnp.logical_and(pl.program_id(0) == ni - 1, pl.program_id(1) == nj - 1))
            def _():
                _comm_phase(comm, "finish", *comm_refs)

    sem = ("arbitrary", "arbitrary") if arbitrary or comm else ("parallel", "parallel")
    res = pl.pallas_call(
        body,
        name=name,
        grid=(ni, nj),
        in_specs=in_specs,
        out_specs=[spec for _, spec in outs] + [_ANY] * n_co,
        out_shape=[shape for shape, _ in outs] + comm_outs,
        scratch_shapes=comm_sems,
        compiler_params=_params(sem),
    )(*arrays)
    return (res[:n_out], res[n_out:]) if comm else res


def _tile(tm, tn, off=0):
    return pl.BlockSpec((tm, tn), lambda i, j: (i, j + off))


def _row(tm, w, blk=0):
    return pl.BlockSpec((tm, w), lambda i, j: (i, blk))


def _whole(shape):
    zeros = (0,) * len(shape)
    return pl.BlockSpec(shape, lambda i, j: zeros)


def _sds(shape, dtype):
    return jax.ShapeDtypeStruct(shape, dtype)


def _tile_t(tm, tn):
    return pl.BlockSpec((tn, tm), lambda i, j: (j, i))


def _grad_w(name, a_t, g, *, tk, tn, ts, block_cols=None):
    ka, s_len = a_t.shape
    n = g.shape[1]
    width = tn if block_cols is None else block_cols

    def body(a_ref, g_ref, o_ref):
        first = pl.program_id(2) == 0
        gv = g_ref[...].astype(BF16)
        for b in range(tn // width):
            part = _dot(a_ref[...], gv[:, b * width:(b + 1) * width])
            dst = o_ref if block_cols is None else o_ref.at[b]

            @pl.when(first)
            def _():
                dst[...] = part

            @pl.when(jnp.logical_not(first))
            def _():
                dst[...] += part

    if block_cols is None:
        out_shape = _sds((ka, n), F32)
        out_spec = pl.BlockSpec((tk, tn), lambda i, j, s: (i, j))
    else:
        out_shape = _sds((n // width, ka, width), F32)
        out_spec = pl.BlockSpec((tn // width, tk, width), lambda i, j, s: (j, i, 0))
    return pl.pallas_call(
        body,
        name=name,
        grid=(ka // tk, n // tn, s_len // ts),
        in_specs=[pl.BlockSpec((tk, ts), lambda i, j, s: (i, s)), pl.BlockSpec((ts, tn), lambda i, j, s: (s, j))],
        out_specs=out_spec,
        out_shape=out_shape,
        compiler_params=_params(("parallel", "parallel", "arbitrary")),
    )(a_t, g)


def _rms_fwd(name, x, g, tm, comm=()):
    s_len, d = x.shape
    steps = s_len // tm
    comm_arrays, comm_outs, comm_sems = _comm_operands(comm)
    n_ci, n_co = len(comm_arrays), len(comm_outs)

    def body(x_ref, g_ref, *rest):
        comm_refs = rest[:n_ci], rest[n_ci + 3:n_ci + 3 + n_co], rest[n_ci + 3 + n_co:]
        h_ref, ht_ref, r_ref = rest[n_ci:n_ci + 3]
        for phase, at in (("start", 0), ("mid", steps // 2)):
            if comm:
                @pl.when(pl.program_id(0) == at)
                def _():
                    _comm_phase(comm, phase, *comm_refs)

        xv = x_ref[...]
        r = lax.rsqrt(jnp.mean(xv * xv, axis=-1, keepdims=True) + EPS)
        h = (xv * r * g_ref[...]).astype(BF16)
        h_ref[...] = h
        ht_ref[...] = jnp.transpose(h)
        r_ref[...] = r
        if comm:
            @pl.when(pl.program_id(0) == steps - 1)
            def _():
                _comm_phase(comm, "finish", *comm_refs)

    res = pl.pallas_call(
        body,
        name=name,
        grid=(steps,),
        in_specs=[pl.BlockSpec((tm, d), lambda i: (i, 0)), pl.BlockSpec((1, d), lambda i: (0, 0))] + [_ANY] * n_ci,
        out_specs=[pl.BlockSpec((tm, d), lambda i: (i, 0)), pl.BlockSpec((d, tm), lambda i: (0, i)),
                   pl.BlockSpec((tm, 1), lambda i: (i, 0))] + [_ANY] * n_co,
        out_shape=[_sds((s_len, d), BF16), _sds((d, s_len), BF16), _sds((s_len, 1), F32)] + comm_outs,
        scratch_shapes=comm_sems,
        compiler_params=_params(("arbitrary",) if comm else ("parallel",)),
    )(x, g, *comm_arrays)
    return res[0], res[1], res[2], res[3:]


def _forget_cumsum(prest, b_f_pad, tc):
    s_len = prest.shape[0]

    def body(f_ref, b_ref, c_ref, carry):
        @pl.when(pl.program_id(0) == 0)
        def _():
            carry[...] = jnp.zeros_like(carry)

        logf = _log_sigmoid(f_ref[...] + b_ref[...])
        row = lax.broadcasted_iota(jnp.int32, (tc, tc), 0)
        col = lax.broadcasted_iota(jnp.int32, (tc, tc), 1)
        tri = (row >= col).astype(F32)
        c = _dot_f32(tri, logf) + carry[...]
        c_ref[...] = c
        carry[...] = c[tc - 1:tc, :]

    return pl.pallas_call(
        body,
        name="forget_cumsum",
        grid=(s_len // tc,),
        in_specs=[pl.BlockSpec((tc, LANES), lambda i: (i, 0)), pl.BlockSpec((1, LANES), lambda i: (0, 0))],
        out_specs=pl.BlockSpec((tc, LANES), lambda i: (i, 0)),
        out_shape=_sds((s_len, LANES), F32),
        scratch_shapes=[pltpu.VMEM((1, LANES), F32)],
        compiler_params=_params(("arbitrary",)),
    )(prest, b_f_pad)


def _stack_heads(pair, lt64):
    zero = jnp.zeros_like(pair)
    return jnp.concatenate([jnp.where(lt64, pair, zero), jnp.where(lt64, zero, pair)], axis=0)


def _score_tiles(q_ref, k_ref, ck_ref, st_sc, tk):
    lt64 = _lane_lt64((tk, LANES))
    for p in range(N_HEADS // 2):
        lanes = slice(p * LANES, (p + 1) * LANES)
        q_pair = q_ref[:, lanes] * jnp.asarray(HEAD_DIM ** -0.5, BF16)
        st2 = _dot_nt(_stack_heads(k_ref[:, lanes], lt64), q_pair)
        for half in range(2):
            h = 2 * p + half
            st_sc[h] = st2[half * tk:(half + 1) * tk] - ck_ref[:, h:h + 1]


ROW_CHUNK = 64


def _row_chunks(tk):
    rc = min(ROW_CHUNK, tk)
    return [slice(r, r + rc) for r in range(0, tk, rc)]


def _by_sublane(x):
    return x.reshape(x.shape[0] // 8, 8, x.shape[1])


def _softmax_update(st_sc, p_sc, m_sc, l_sc, tk, tq):
    alphas = []
    for h in range(N_HEADS):
        top8 = jnp.full((8, tq), NEG, F32)
        for rows in _row_chunks(tk):
            top8 = jnp.maximum(top8, jnp.max(_by_sublane(st_sc[h, rows, :]), axis=0))
        m_old = m_sc[h]
        m_new = jnp.maximum(m_old, jnp.max(top8, axis=0, keepdims=True))
        sum8 = jnp.zeros((8, tq), F32)
        for rows in _row_chunks(tk):
            pt = jnp.exp(st_sc[h, rows, :] - m_new)
            p_sc[h, rows, :] = pt.astype(BF16)
            sum8 = sum8 + jnp.sum(_by_sublane(pt), axis=0)
        alpha = jnp.exp(m_old - m_new)
        l_sc[h] = alpha * l_sc[h] + jnp.sum(sum8, axis=0, keepdims=True)
        m_sc[h] = m_new
        alphas.append(alpha)
    return alphas


def _mask_diagonal(st_sc, i, j, tq, tk):
    @pl.when((j + 1) * tk - 1 > i * tq)
    def _():
        key = j * tk + lax.broadcasted_iota(jnp.int32, (tk, tq), 0)
        query = i * tq + lax.broadcasted_iota(jnp.int32, (tk, tq), 1)
        st_sc[...] = jnp.where((query >= key)[None], st_sc[...], NEG)


def _attn_fwd(qkv, v_t, c_col, tq, tk, comm=()):
    s_len = qkv.shape[0]
    ratio = tq // tk
    steps = [(i, j) for i in range(s_len // tq) for j in range((i + 1) * ratio)]
    i_tab = jnp.asarray([i for i, _ in steps], jnp.int32)
    j_tab = jnp.asarray([j for _, j in steps], jnp.int32)

    comm_arrays, comm_outs, comm_sems = _comm_operands(comm)
    n_ci, n_co = len(comm_arrays), len(comm_outs)

    def body(i_ref, j_ref, q_ref, k_ref, vt_ref, ck_ref, *rest):
        comm_refs = rest[:n_ci], rest[n_ci + 3:n_ci + 3 + n_co], rest[n_ci + 3 + n_co + 5:]
        o_ref, ot_ref, lse_ref = rest[n_ci:n_ci + 3]
        acc_t, m_sc, l_sc, st_sc, p_sc = rest[n_ci + 3 + n_co:n_ci + 3 + n_co + 5]
        n = pl.program_id(0)
        i, j = i_ref[n], j_ref[n]
        for phase, at in (("start", 0), ("mid", (2 * len(steps)) // 3)):
            if comm:
                @pl.when(n == at)
                def _():
                    _comm_phase(comm, phase, *comm_refs)

        @pl.when(j == 0)
        def _():
            acc_t[...] = jnp.zeros_like(acc_t)
            m_sc[...] = jnp.full_like(m_sc, NEG)
            l_sc[...] = jnp.zeros_like(l_sc)

        _score_tiles(q_ref, k_ref, ck_ref, st_sc, tk)
        _mask_diagonal(st_sc, i, j, tq, tk)
        alpha = _softmax_update(st_sc, p_sc, m_sc, l_sc, tk, tq)
        top = lax.broadcasted_iota(jnp.int32, (LANES, tq), 0) < HEAD_DIM
        for p in range(N_HEADS // 2):
            lanes = slice(p * LANES, (p + 1) * LANES)
            vt_pair = vt_ref[lanes, :]
            pv = jnp.where(top, _dot(vt_pair, p_sc[2 * p]), _dot(vt_pair, p_sc[2 * p + 1]))
            acc_t[lanes, :] = acc_t[lanes, :] * jnp.where(top, alpha[2 * p], alpha[2 * p + 1]) + pv

        @pl.when(j == (i + 1) * ratio - 1)
        def _():
            for p in range(N_HEADS // 2):
                lanes = slice(p * LANES, (p + 1) * LANES)
                l_pair = jnp.where(top, l_sc[2 * p], l_sc[2 * p + 1])
                o_t = acc_t[lanes, :] / l_pair
                o_ref[:, lanes] = jnp.transpose(o_t)
                ot_ref[lanes, :] = o_t.astype(BF16)
            lse_ref[...] = m_sc[...] + jnp.log(l_sc[...])

        if comm:
            @pl.when(n == len(steps) - 1)
            def _():
                _comm_phase(comm, "finish", *comm_refs)

    stat = pltpu.VMEM((N_HEADS, 1, tq), F32)
    res = pl.pallas_call(
        body,
        name="attn_fwd",
        grid_spec=pltpu.PrefetchScalarGridSpec(
            num_scalar_prefetch=2,
            grid=(len(steps),),
            in_specs=[
                pl.BlockSpec((tq, FOX_W), lambda n, it, jt: (it[n], 0)),
                pl.BlockSpec((tk, FOX_W), lambda n, it, jt: (jt[n], 1)),
                pl.BlockSpec((FOX_W, tk), lambda n, it, jt: (0, jt[n])),
                pl.BlockSpec((tk, LANES), lambda n, it, jt: (jt[n], 0)),
            ] + [_ANY] * n_ci,
            out_specs=[
                pl.BlockSpec((tq, FOX_W), lambda n, it, jt: (it[n], 0)),
                pl.BlockSpec((FOX_W, tq), lambda n, it, jt: (0, it[n])),
                pl.BlockSpec((N_HEADS, 1, tq), lambda n, it, jt: (0, 0, it[n])),
            ] + [_ANY] * n_co,
            scratch_shapes=[pltpu.VMEM((FOX_W, tq), F32), stat, stat, pltpu.VMEM((N_HEADS, tk, tq), F32),
                            pltpu.VMEM((N_HEADS, tk, tq), BF16)] + comm_sems,
        ),
        out_shape=[_sds((s_len, FOX_W), F32), _sds((FOX_W, s_len), BF16), _sds((N_HEADS, 1, s_len), F32)] + comm_outs,
        compiler_params=_params(("arbitrary",)),
    )(i_tab, j_tab, qkv, qkv, v_t, c_col, *comm_arrays)
    return res[0], res[1], res[2], res[3:]


def _sgu_mix(vn, w_stack, lt64):
    outs = []
    for p in range(SGU_G // 2):
        r = _dot(w_stack[p], vn[:, p * LANES:(p + 1) * LANES])
        outs.append(jnp.where(lt64, r[:SGU_LEN], r[SGU_LEN:]))
    return jnp.concatenate(outs, axis=1)


def _sgu_norm(sv, ln_g, ln_b):
    svg = _gelu(sv)
    xc = svg - jnp.mean(svg, axis=-1, keepdims=True)
    rstd = lax.rsqrt(jnp.mean(xc * xc, axis=-1, keepdims=True) + EPS)
    xhat = xc * rstd
    return xhat, rstd, xhat * ln_g + ln_b


def _sgu_fwd(prest, ln_g, ln_b, w_stack, b_pair, tm):
    s_len = prest.shape[0]

    def body(u_ref, sv_ref, g_ref, b_ref, w_ref, bp_ref, sg_ref, sgt_ref):
        lt64 = _lane_lt64((SGU_LEN, LANES))
        _, _, vn = _sgu_norm(sv_ref[...].astype(F32), g_ref[...], b_ref[...])
        vn = vn.astype(BF16)
        w_stack_v = [w_ref[p] for p in range(SGU_G // 2)]
        for w in range(tm // SGU_LEN):
            win = slice(w * SGU_LEN, (w + 1) * SGU_LEN)
            mixed = _sgu_mix(vn[win], w_stack_v, lt64) + bp_ref[...]
            sg = (_gelu(u_ref[win, :].astype(F32)) * mixed).astype(BF16)
            sg_ref[win, :] = sg
            sgt_ref[:, win] = jnp.transpose(sg)

    return pl.pallas_call(
        body,
        name="sgu_fwd",
        grid=(s_len // tm,),
        in_specs=[
            pl.BlockSpec((tm, SGU_W), lambda i: (i, U_OFF // SGU_W)),
            pl.BlockSpec((tm, SGU_W), lambda i: (i, SV_OFF // SGU_W)),
            pl.BlockSpec((1, SGU_W), lambda i: (0, 0)),
            pl.BlockSpec((1, SGU_W), lambda i: (0, 0)),
            pl.BlockSpec((SGU_G // 2, 2 * SGU_LEN, SGU_LEN), lambda i: (0, 0, 0)),
            pl.BlockSpec((SGU_LEN, SGU_W), lambda i: (0, 0)),
        ],
        out_specs=[pl.BlockSpec((tm, SGU_W), lambda i: (i, 0)), pl.BlockSpec((SGU_W, tm), lambda i: (0, i))],
        out_shape=[_sds((s_len, SGU_W), BF16), _sds((SGU_W, s_len), BF16)],
        compiler_params=_params(("parallel",)),
    )(prest, prest, ln_g, ln_b, w_stack, b_pair)


def _sgu_bwd(prest, dsg, ln_g, ln_b, w_stack, wt_stack, b_pair, tm):
    s_len = prest.shape[0]
    n_pair = SGU_G // 2

    def body(u_ref, sv_ref, dsg_ref, g_ref, b_ref, w_ref, wt_ref, bp_ref,
             du_ref, dsv_ref, dw_ref, db_ref, dg_ref, dbeta_ref, dusvt_ref, dvn_sc):
        @pl.when(pl.program_id(0) == 0)
        def _():
            dw_ref[...] = jnp.zeros_like(dw_ref)
            db_ref[...] = jnp.zeros_like(db_ref)
            dg_ref[...] = jnp.zeros_like(dg_ref)
            dbeta_ref[...] = jnp.zeros_like(dbeta_ref)

        lt64 = _lane_lt64((SGU_LEN, LANES))
        sv = sv_ref[...].astype(F32)
        xhat, rstd, vn32 = _sgu_norm(sv, g_ref[...], b_ref[...])
        vn = vn32.astype(BF16)
        w_stack_v = [w_ref[p] for p in range(n_pair)]
        db = jnp.zeros((SGU_LEN, SGU_W), F32)
        for w in range(tm // SGU_LEN):
            win = slice(w * SGU_LEN, (w + 1) * SGU_LEN)
            u = u_ref[win, :].astype(F32)
            dsg_w = dsg_ref[win, :]
            mixed = _sgu_mix(vn[win], w_stack_v, lt64) + bp_ref[...]
            du = (dsg_w * mixed * _gelu_grad(u)).astype(BF16)
            du_ref[win, :] = du
            dusvt_ref[:SGU_W, win] = jnp.transpose(du)
            dmixed = dsg_w * _gelu(u)
            db = db + dmixed
            dm16 = dmixed.astype(BF16)
            for p in range(n_pair):
                lanes = slice(p * LANES, (p + 1) * LANES)
                dmp = dm16[:, lanes]
                r = _dot(wt_ref[p], dmp)
                dvn_sc[win, lanes] = jnp.where(lt64, r[:SGU_LEN], r[SGU_LEN:])
                zero = jnp.zeros_like(dmp)
                dm_ab = jnp.concatenate([jnp.where(lt64, dmp, zero), jnp.where(lt64, zero, dmp)], axis=0)
                dw_ref[p] += _dot_nt(dm_ab, vn[win, lanes])
        db_ref[...] += db
        dvn = dvn_sc[...]
        dg_ref[...] += jnp.sum(dvn * xhat, axis=0, keepdims=True)
        dbeta_ref[...] += jnp.sum(dvn, axis=0, keepdims=True)
        dxh = dvn * g_ref[...]
        dsvg = rstd * (dxh - jnp.mean(dxh, axis=-1, keepdims=True) - xhat * jnp.mean(dxh * xhat, axis=-1, keepdims=True))
        dsv = (dsvg * _gelu_grad(sv)).astype(BF16)
        dsv_ref[...] = dsv
        dusvt_ref[SGU_W:, :] = jnp.transpose(dsv)

    const2 = lambda i: (0, 0)
    const3 = lambda i: (0, 0, 0)
    return pl.pallas_call(
        body,
        name="sgu_bwd",
        grid=(s_len // tm,),
        in_specs=[
            pl.BlockSpec((tm, SGU_W), lambda i: (i, U_OFF // SGU_W)),
            pl.BlockSpec((tm, SGU_W), lambda i: (i, SV_OFF // SGU_W)),
            pl.BlockSpec((tm, SGU_W), lambda i: (i, 0)),
            pl.BlockSpec((1, SGU_W), const2),
            pl.BlockSpec((1, SGU_W), const2),
            pl.BlockSpec((n_pair, 2 * SGU_LEN, SGU_LEN), const3),
            pl.BlockSpec((n_pair, 2 * SGU_LEN, SGU_LEN), const3),
            pl.BlockSpec((SGU_LEN, SGU_W), const2),
        ],
        out_specs=[
            pl.BlockSpec((tm, SGU_W), lambda i: (i, 0)),
            pl.BlockSpec((tm, SGU_W), lambda i: (i, 0)),
            pl.BlockSpec((n_pair, 2 * SGU_LEN, SGU_LEN), const3),
            pl.BlockSpec((SGU_LEN, SGU_W), const2),
            pl.BlockSpec((1, SGU_W), const2),
            pl.BlockSpec((1, SGU_W), const2),
            pl.BlockSpec((2 * SGU_W, tm), lambda i: (0, i)),
        ],
        out_shape=[
            _sds((s_len, SGU_W), BF16), _sds((s_len, SGU_W), BF16), _sds((n_pair, 2 * SGU_LEN, SGU_LEN), F32),
            _sds((SGU_LEN, SGU_W), F32), _sds((1, SGU_W), F32), _sds((1, SGU_W), F32),
            _sds((2 * SGU_W, s_len), BF16),
        ],
        scratch_shapes=[pltpu.VMEM((tm, SGU_W), F32)],
        compiler_params=_params(("arbitrary",)),
    )(prest, prest, dsg, ln_g, ln_b, w_stack, wt_stack, b_pair)


def _attn_bwd(qkv, k_t, do, c_col, lse_row, delta_row, tq, tk, comm=()):
    s_len = qkv.shape[0]
    nq, nk = s_len // tq, s_len // tk
    ratio = tq // tk
    scale = HEAD_DIM ** -0.5
    steps = [(j, i) for j in range(nk) for i in range(j // ratio, nq)]
    j_tab = jnp.asarray([j for j, _ in steps], jnp.int32)
    i_tab = jnp.asarray([i for _, i in steps], jnp.int32)

    comm_arrays, comm_outs, comm_sems = _comm_operands(comm)
    n_ci, n_co = len(comm_arrays), len(comm_outs)

    def body(j_ref, i_ref, q_ref, k_ref, v_ref, kt_ref, do_ref, ck_ref, lse_ref, dl_ref, *rest):
        comm_refs = rest[:n_ci], rest[n_ci + 6:n_ci + 6 + n_co], rest[n_ci + 6 + n_co + 8:]
        dq_ref, dk_ref, dv_ref, dcr_ref, dcc_ref, dkvt_ref = rest[n_ci:n_ci + 6]
        dq_t, dk_acc, dv_acc, dcc_acc, st_sc, dpt_sc, p_sc, ds_sc = rest[n_ci + 6 + n_co:n_ci + 6 + n_co + 8]
        n = pl.program_id(0)
        j, i = j_ref[n], i_ref[n]

        @pl.when(n == 0)
        def _():
            _comm_phase(comm, "start", *comm_refs)
            dq_t[...] = jnp.zeros_like(dq_t)
            dcr_ref[...] = jnp.zeros_like(dcr_ref)

        @pl.when(i == j // ratio)
        def _():
            dk_acc[...] = jnp.zeros_like(dk_acc)
            dv_acc[...] = jnp.zeros_like(dv_acc)
            dcc_acc[...] = jnp.zeros_like(dcc_acc)

        lt64 = _lane_lt64((tk, LANES))
        _score_tiles(q_ref, k_ref, ck_ref, st_sc, tk)
        for p in range(N_HEADS // 2):
            lanes = slice(p * LANES, (p + 1) * LANES)
            dpt2 = _dot_nt(_stack_heads(v_ref[:, lanes], lt64), do_ref[:, lanes].astype(BF16))
            dpt_sc[2 * p] = dpt2[:tk]
            dpt_sc[2 * p + 1] = dpt2[tk:]
        _mask_diagonal(st_sc, i, j, tq, tk)

        pt = jnp.exp(st_sc[...] - lse_ref[...])
        dst = pt * (dpt_sc[...] - dl_ref[...])
        p_sc[...] = pt.astype(BF16)
        ds_sc[...] = dst.astype(BF16)
        dcr_ref[i] += jnp.sum(dst, axis=1, keepdims=True)
        col_sums = jnp.sum(dst, axis=2, keepdims=True)
        lane = lax.broadcasted_iota(jnp.int32, (tk, LANES), 1)
        dcc = jnp.zeros((tk, LANES), F32)
        for h in range(N_HEADS):
            dcc = jnp.where(lane == h, -col_sums[h], dcc)
        dcc_acc[...] += dcc

        for p in range(N_HEADS // 2):
            lanes = slice(p * LANES, (p + 1) * LANES)
            q_pair = q_ref[:, lanes] * jnp.asarray(scale, BF16)
            dv2 = _dot(p_sc[2 * p:2 * p + 2].reshape(2 * tk, tq), do_ref[:, lanes].astype(BF16))
            dv_acc[:, lanes] += jnp.where(lt64, dv2[:tk], dv2[tk:])
            dk2 = _dot(ds_sc[2 * p:2 * p + 2].reshape(2 * tk, tq), q_pair)
            dk_acc[:, lanes] += jnp.where(lt64, dk2[:tk], dk2[tk:])
            dq2 = _dot(kt_ref[lanes, :], jnp.concatenate([ds_sc[2 * p], ds_sc[2 * p + 1]], axis=1))
            top = lax.broadcasted_iota(jnp.int32, (LANES, tq), 0) < HEAD_DIM
            dq_t[i, lanes, :] += jnp.where(top, dq2[:, :tq], dq2[:, tq:])

        @pl.when(j == (i + 1) * ratio - 1)
        def _():
            rows = pl.ds(pl.multiple_of(i * tq, tq), tq)
            for p in range(N_HEADS // 2):
                lanes = slice(p * LANES, (p + 1) * LANES)
                dq_ref[rows, lanes] = (jnp.transpose(dq_t[i, lanes, :]) * scale).astype(BF16)

        @pl.when(i == nq - 1)
        def _():
            dk16, dv16 = dk_acc[...].astype(BF16), dv_acc[...].astype(BF16)
            dk_ref[...] = dk16
            dv_ref[...] = dv16
            dkvt_ref[:FOX_W, :] = jnp.transpose(dk16)
            dkvt_ref[FOX_W:, :] = jnp.transpose(dv16)
            dcc_ref[...] = dcc_acc[...]

        if comm:
            @pl.when(n == len(steps) // 2)
            def _():
                _comm_phase(comm, "mid", *comm_refs)

            @pl.when(n == len(steps) - 1)
            def _():
                _comm_phase(comm, "finish", *comm_refs)

    q_map = lambda n, jt, it: (it[n], 0)
    q_stat = lambda n, jt, it: (0, 0, it[n])
    k_map = lambda n, jt, it: (jt[n], 0)
    tile = (N_HEADS, tk, tq)
    res = pl.pallas_call(
        body,
        name="attn_bwd",
        grid_spec=pltpu.PrefetchScalarGridSpec(
            num_scalar_prefetch=2,
            grid=(len(steps),),
            in_specs=[
                pl.BlockSpec((tq, FOX_W), q_map),
                pl.BlockSpec((tk, FOX_W), lambda n, jt, it: (jt[n], 1)),
                pl.BlockSpec((tk, FOX_W), lambda n, jt, it: (jt[n], 2)),
                pl.BlockSpec((FOX_W, tk), lambda n, jt, it: (0, jt[n])),
                pl.BlockSpec((tq, FOX_W), q_map),
                pl.BlockSpec((tk, LANES), k_map),
                pl.BlockSpec((N_HEADS, 1, tq), q_stat),
                pl.BlockSpec((N_HEADS, 1, tq), q_stat),
            ] + [_ANY] * n_ci,
            out_specs=[
                pl.BlockSpec((s_len, FOX_W), lambda n, jt, it: (0, 0)),
                pl.BlockSpec((tk, FOX_W), k_map),
                pl.BlockSpec((tk, FOX_W), k_map),
                pl.BlockSpec((nq, N_HEADS, 1, tq), lambda n, jt, it: (0, 0, 0, 0)),
                pl.BlockSpec((tk, LANES), k_map),
                pl.BlockSpec((2 * FOX_W, tk), lambda n, jt, it: (0, jt[n])),
            ] + [_ANY] * n_co,
            scratch_shapes=[pltpu.VMEM((nq, FOX_W, tq), F32), pltpu.VMEM((tk, FOX_W), F32), pltpu.VMEM((tk, FOX_W), F32),
                            pltpu.VMEM((tk, LANES), F32), pltpu.VMEM(tile, F32), pltpu.VMEM(tile, F32),
                            pltpu.VMEM(tile, BF16), pltpu.VMEM(tile, BF16)] + comm_sems,
        ),
        out_shape=[_sds((s_len, FOX_W), BF16), _sds((s_len, FOX_W), BF16), _sds((s_len, FOX_W), BF16),
                   _sds((nq, N_HEADS, 1, tq), F32), _sds((s_len, LANES), F32),
                   _sds((2 * FOX_W, s_len), BF16)] + comm_outs,
        compiler_params=_params(("arbitrary",)),
    )(j_tab, i_tab, qkv, qkv, qkv, k_t, do, c_col, lse_row, delta_row, *comm_arrays)
    return res[:6], res[6:]


def _forget_bwd(dc_rows, dc_cols, prest, b_f_pad, tc):
    s_len = dc_rows.shape[0]
    nb = s_len // tc

    def body(dcr_ref, dc_ref, f_ref, b_ref, df_ref, db_ref, dft_ref, carry):
        @pl.when(pl.program_id(0) == 0)
        def _():
            carry[...] = jnp.zeros_like(carry)
            db_ref[...] = jnp.zeros_like(db_ref)

        row = lax.broadcasted_iota(jnp.int32, (tc, tc), 0)
        col = lax.broadcasted_iota(jnp.int32, (tc, tc), 1)
        tri = (row <= col).astype(F32)
        dlogf = _dot_f32(tri, dcr_ref[...] + dc_ref[...]) + carry[...]
        carry[...] = dlogf[0:1, :]
        z = f_ref[...] + b_ref[...]
        lane = lax.broadcasted_iota(jnp.int32, (tc, LANES), 1)
        dz = jnp.where(lane < N_HEADS, dlogf * _sigmoid(-z), 0.0)
        df_ref[...] = dz.astype(BF16)
        dft_ref[...] = jnp.transpose(dz).astype(BF16)
        db_ref[...] += jnp.sum(dz, axis=0, keepdims=True)

    rev = lambda i: (nb - 1 - i, 0)
    return pl.pallas_call(
        body,
        name="forget_bwd",
        grid=(nb,),
        in_specs=[
            pl.BlockSpec((tc, LANES), rev),
            pl.BlockSpec((tc, LANES), rev),
            pl.BlockSpec((tc, LANES), rev),
            pl.BlockSpec((1, LANES), lambda i: (0, 0)),
        ],
        out_specs=[pl.BlockSpec((tc, LANES), rev), pl.BlockSpec((1, LANES), lambda i: (0, 0)),
                   pl.BlockSpec((LANES, tc), lambda i: (0, nb - 1 - i))],
        out_shape=[_sds((s_len, LANES), BF16), _sds((1, LANES), F32), _sds((LANES, s_len), BF16)],
        scratch_shapes=[pltpu.VMEM((1, LANES), F32)],
        compiler_params=_params(("arbitrary",)),
    )(dc_rows, dc_cols, prest, b_f_pad)


def _pair_sum(name, g4, recv, idx, tr):
    _, _, r, c = g4.shape

    def body(idx_ref, g_ref, r_ref, p16_ref, own_ref):
        k = pl.program_id(1)
        s = g_ref[...] + r_ref[...]
        p16_ref[...] = s.astype(BF16)

        @pl.when(k == idx_ref[1])
        def _():
            own_ref[...] = s

    return pl.pallas_call(
        body,
        name=name,
        grid_spec=pltpu.PrefetchScalarGridSpec(
            num_scalar_prefetch=1,
            grid=(r // tr, 4),
            in_specs=[
                pl.BlockSpec((None, None, tr, c), lambda i, k, idx: (k, idx[0], i, 0)),
                pl.BlockSpec((None, tr, c), lambda i, k, idx: (k, i, 0)),
            ],
            out_specs=[
                pl.BlockSpec((None, tr, c), lambda i, k, idx: (k, i, 0)),
                pl.BlockSpec((tr, c), lambda i, k, idx: (i, 0)),
            ],
        ),
        out_shape=[_sds((4, r, c), BF16), _sds((r, c), F32)],
        compiler_params=_params(("parallel", "arbitrary")),
    )(idx, g4, recv)


def _adamw_math(w, g, m, v):
    m2 = ADAM_B1 * m + (1.0 - ADAM_B1) * g
    v2 = ADAM_B2 * v + (1.0 - ADAM_B2) * (g * g)
    m_hat = m2 / (1.0 - ADAM_B1 ** ADAM_STEP)
    v_hat = v2 / (1.0 - ADAM_B2 ** ADAM_STEP)
    delta = -ADAM_LR * (m_hat / (jnp.sqrt(v_hat) + ADAM_EPS) + ADAM_WD * w)
    return delta, m2, v2


def _adamw_shard(name, own, recv, w, m, v, tr):
    r, c = own.shape

    def body(own_ref, recv_ref, w_ref, m_ref, v_ref, g_ref, d_ref, m2_ref, v2_ref):
        g = own_ref[...]
        for k in range(3):
            g = g + recv_ref[k].astype(F32)
        delta, m2, v2 = _adamw_math(w_ref[...], g, m_ref[...], v_ref[...])
        g_ref[...] = g
        d_ref[...] = delta
        m2_ref[...] = m2
        v2_ref[...] = v2

    spec = pl.BlockSpec((tr, c), lambda i: (i, 0))
    return pl.pallas_call(
        body,
        name=name,
        grid=(r // tr,),
        in_specs=[spec, pl.BlockSpec((3, tr, c), lambda i: (0, i, 0)), spec, spec, spec],
        out_specs=[spec] * 4,
        out_shape=[_sds((r, c), F32)] * 4,
        compiler_params=_params(("parallel",)),
    )(own, recv, w, m, v)


def _adamw_small(name, gathered, first_row, w, m, v):
    r = w.shape[0]
    assert first_row % r == 0

    def body(ga_ref, w_ref, m_ref, v_ref, g_ref, d_ref, m2_ref, v2_ref):
        g = ga_ref[0]
        for k in range(1, N_DEV):
            g = g + ga_ref[k]
        delta, m2, v2 = _adamw_math(w_ref[...], g, m_ref[...], v_ref[...])
        g_ref[...] = g
        d_ref[...] = delta
        m2_ref[...] = m2
        v2_ref[...] = v2

    spec = pl.BlockSpec((r, LANES), lambda i: (0, 0))
    return pl.pallas_call(
        body,
        name=name,
        grid=(1,),
        in_specs=[pl.BlockSpec((N_DEV, r, LANES), lambda i: (0, first_row // r, 0)), spec, spec, spec],
        out_specs=[spec] * 4,
        out_shape=[_sds((r, LANES), F32)] * 4,
        compiler_params=_params(("arbitrary",)),
    )(gathered, w, m, v)


_TINY_EARLY = (("b_sgu", (1, SGU_G, SGU_LEN)), ("norm2_g", (1, D_MODEL)), ("normf_g", (D_MODEL,)),
               ("ln_v_g", (1, SGU_W)), ("ln_v_b", (1, SGU_W)))
_TINY_LATE = (("b_f", (1, N_HEADS)), ("norm1_g", (1, D_MODEL)), ("loss", ()))


def _pack_rows(values):
    rows = []
    for val in values:
        flat = val.reshape(-1).astype(F32)
        pad = (-flat.shape[0]) % LANES
        rows.append(jnp.pad(flat, (0, pad)).reshape(-1, LANES))
    packed = jnp.concatenate(rows, axis=0)
    return jnp.pad(packed, ((0, (-packed.shape[0]) % 8), (0, 0)))


def _unpack_rows(packed, group):
    out, row = {}, 0
    for name, shape in group:
        size = math.prod(shape)
        n_rows = -(-size // LANES)
        out[name] = packed[row:row + n_rows].reshape(-1)[:size].reshape(shape)
        row += n_rows
    return out


def kernel(x, norm1_g, w_in, b_f, ln_v_g, ln_v_b, w_sgu, b_sgu, w_a, w_b, w_o, norm2_g, w_up, w_down, normf_g, loss_target, m_norm1_g, m_w_in, m_b_f, m_ln_v_g, m_ln_v_b, m_w_sgu, m_b_sgu, m_w_a, m_w_b, m_w_o, m_norm2_g, m_w_up, m_w_down, m_normf_g, v_norm1_g, v_w_in, v_b_f, v_ln_v_g, v_ln_v_b, v_w_sgu, v_b_sgu, v_w_a, v_w_b, v_w_o, v_norm2_g, v_w_up, v_w_down, v_normf_g):
    xs = x[0]
    target = loss_target[0]
    s_len, d = xs.shape
    tm = min(512, s_len)
    tl = min(1024, s_len)
    tr = min(512, s_len)
    ta = min(512, s_len)
    tc = min(512, s_len)

    w_in_t = jnp.transpose(w_in[0])
    lin = (IN_SHARD * d // LANES, LANES)
    big = (w_in_t.reshape(lin), w_a[0], w_b[0], w_o[0], w_up[0], w_down[0])
    h, h_t, r1, (w_in_g,) = _rms_fwd("rms1", xs, norm1_g, tm, comm=[_gather_plan(w_in_t.astype(BF16))])
    w_in_f = w_in_g.reshape(IN_COLS, d)
    later_shards = [jnp.transpose(w_a[0]), jnp.transpose(w_b[0]), w_o[0], jnp.transpose(w_up[0]), w_down[0]]
    later_plans = [_gather_plan(w.astype(BF16)) for w in later_shards]

    def unflatten(gathered):
        return [g.reshape(N_DEV * g.shape[1], g.shape[2]) for g in gathered]

    w_qkv = w_in_f[:QKV_W]
    f_lo = QKV_W
    u_lo = f_lo + N_HEADS
    w_rest = jnp.concatenate([w_in_f[u_lo:], jnp.pad(w_in_f[f_lo:u_lo], ((0, LANES - N_HEADS), (0, 0)))], axis=0)

    chunk_id = jnp.arange(SGU_LEN) // CHUNK
    sgu_mask = chunk_id[None, :] <= chunk_id[:, None]
    w_masked = jnp.where(sgu_mask[None], w_sgu[0], 0.0)
    w_stack = w_masked.reshape(SGU_G // 2, 2 * SGU_LEN, SGU_LEN).astype(BF16)
    wt_stack = jnp.transpose(w_masked, (0, 2, 1)).reshape(SGU_G // 2, 2 * SGU_LEN, SGU_LEN).astype(BF16)
    b_pair = jnp.transpose(jnp.repeat(b_sgu[0], SGU_W // SGU_G, axis=0))
    b_f_pad = jnp.pad(b_f, ((0, 0), (0, LANES - N_HEADS)))
    head_sel = (jnp.arange(FOX_W)[:, None] // HEAD_DIM == jnp.arange(LANES)[None, :]).astype(F32)

    def store(dtype):
        def epi(accs, ex, out):
            out[0][...] = accs[0].astype(dtype)
        return epi

    def qkv_epi(accs, ex, out):
        tile = accs[0].astype(BF16)
        out[0][...] = tile
        for col, ref in ((1, out[1]), (2, out[2])):
            @pl.when(pl.program_id(1) == col)
            def _():
                ref[...] = jnp.transpose(tile)

    t_spec = pl.BlockSpec((FOX_W, tl), lambda i, j: (0, i))
    qkv, k_t, v_t = _mm("proj_qkv", [(h, w_qkv, True, None)], [],
                        [(_sds((s_len, QKV_W), BF16), _tile(tl, FOX_W)), (_sds((FOX_W, s_len), BF16), t_spec),
                         (_sds((FOX_W, s_len), BF16), t_spec)],
                        qkv_epi, m=s_len, tm=tl, n=QKV_W, tn=FOX_W, arbitrary=True)
    rest_tn = 640
    f_tile, f_lane = F_OFF // rest_tn, F_OFF % rest_tn

    def rest_epi(accs, ex, out):
        out[0][...] = accs[0].astype(BF16)

        @pl.when(pl.program_id(1) == f_tile)
        def _():
            out[1][...] = accs[0][:, f_lane:f_lane + LANES]

    prest, f_logit = _mm("proj_rest", [(h, w_rest, True, None)], [],
                         [(_sds((s_len, REST_W), BF16), _tile(tl, rest_tn)), (_sds((s_len, LANES), F32), _row(tl, LANES))],
                         rest_epi, m=s_len, tm=tl, n=REST_W, tn=rest_tn, arbitrary=True)

    c_col = _forget_cumsum(f_logit, b_f_pad, tc)
    o, o_t, lse_row, later_g = _attn_fwd(qkv, v_t, c_col, ta, ta, comm=later_plans)
    w_a_t, w_b_t, w_o_f, w_up_t, w_down_f = unflatten(later_g)
    sg, sg_t = _sgu_fwd(prest, ln_v_g, ln_v_b, w_stack, b_pair, tm)

    def merge_epi(accs, ex, out):
        ya, yb = accs
        sa, sb = _sigmoid(ex[0][...].astype(F32)), _sigmoid(ex[1][...].astype(F32))
        merged = (sa * ya + sb * yb).astype(BF16)
        out[0][...] = merged
        out[1][...] = ya.astype(BF16)
        out[2][...] = yb.astype(BF16)
        out[3][...] = jnp.transpose(merged)

    merged, ya, yb, merged_t = _mm(
        "merge", [(o, w_a_t, True, None), (sg, w_b_t, True, None)],
        [(prest, _tile(tm, d, GA_OFF // d)), (prest, _tile(tm, d, GB_OFF // d))],
        [(_sds((s_len, d), BF16), _tile(tm, d))] * 3 + [(_sds((d, s_len), BF16), _tile_t(tm, d))],
        merge_epi, m=s_len, tm=tm, n=d, tn=d)

    def resid_epi(accs, ex, out):
        x1v = ex[0][...] + accs[0]
        out[0][...] = x1v
        r = lax.rsqrt(jnp.mean(x1v * x1v, axis=-1, keepdims=True) + EPS)
        h2v = (x1v * r * ex[1][...]).astype(BF16)
        out[1][...] = h2v
        out[2][...] = jnp.transpose(h2v)
        out[3][...] = r

    x1, h2, h2_t, r2 = _mm(
        "out_proj", [(merged, w_o_f, False, None)], [(xs, _tile(tm, d)), (norm2_g, _whole((1, d)))],
        [(_sds((s_len, d), F32), _tile(tm, d)), (_sds((s_len, d), BF16), _tile(tm, d)),
         (_sds((d, s_len), BF16), _tile_t(tm, d)), (_sds((s_len, 1), F32), _row(tm, 1))],
        resid_epi, m=s_len, tm=tm, n=d, tn=d)

    def up_epi(accs, ex, out):
        act = jnp.square(jnp.maximum(accs[0], 0.0)).astype(BF16)
        out[0][...] = act
        out[1][...] = jnp.transpose(act)

    act, act_t = _mm(
        "mlp_up", [(h2, w_up_t, True, None)], [],
        [(_sds((s_len, D_FF), BF16), _tile(tl, 1024)), (_sds((D_FF, s_len), BF16), _tile_t(tl, 1024))],
        up_epi, m=s_len, tm=tl, n=D_FF, tn=1024)

    def first_step():
        return jnp.logical_and(pl.program_id(0) == 0, pl.program_id(1) == 0)

    def accumulate(ref, val):
        @pl.when(first_step())
        def _():
            ref[...] = val

        @pl.when(jnp.logical_not(first_step()))
        def _():
            ref[...] += val

    def final_epi(accs, ex, out):
        x1_ref, t_ref, g_ref = ex
        x2 = x1_ref[...] + accs[0]
        rf = lax.rsqrt(jnp.mean(x2 * x2, axis=-1, keepdims=True) + EPS)
        xh = x2 * rf
        gf = g_ref[...]
        err = xh * gf - t_ref[...]
        dy = err * (1.0 / d)
        dx2 = _rms_bwd(xh, rf, gf, dy)
        out[0][...] = dx2
        accumulate(out[1], jnp.sum(dy * xh, axis=0, keepdims=True))
        part = 0.5 * jnp.sum(jnp.sum(err * err, axis=-1, keepdims=True) * (1.0 / d), axis=0, keepdims=True)
        accumulate(out[2], jnp.broadcast_to(part, (1, LANES)))
        out[3][...] = dx2.astype(BF16)

    gf2 = normf_g.reshape(1, d)
    dx2, g_normf, loss_part, dx2_16 = _mm(
        "mlp_down_loss", [(act, w_down_f, False, None)],
        [(x1, _row(tr, d)), (target, _row(tr, d)), (gf2, _whole((1, d)))],
        [(_sds((s_len, d), F32), _row(tr, d)), (_sds((1, d), F32), _whole((1, d))), (_sds((1, LANES), F32), _whole((1, LANES))),
         (_sds((s_len, d), BF16), _row(tr, d))],
        final_epi, m=s_len, tm=tr, n=d, tn=d, arbitrary=True)

    def dact_epi(accs, ex, out):
        out[0][...] = (accs[0] * (2.0 * jnp.sqrt(ex[0][...].astype(F32)))).astype(BF16)

    (da,) = _mm("mlp_down_bwd", [(dx2_16, w_down_f, True, None)], [(act, _tile(tl, 1024))],
                [(_sds((s_len, D_FF), BF16), _tile(tl, 1024))], dact_epi, m=s_len, tm=tl, n=D_FF, tn=1024)
    g_down = _grad_w("grad_w_down", act_t, dx2_16, tk=1024, tn=d, ts=tl)
    g_up = _grad_w("grad_w_up", h2_t, da, tk=d, tn=1024, ts=tl, block_cols=D_FF // N_DEV)

    def dh2_epi(accs, ex, out):
        x1_ref, r_ref, g_ref, dx2_ref = ex
        r = r_ref[...]
        xh = x1_ref[...] * r
        dh2 = accs[0]
        out[0][...] = dx2_ref[...] + _rms_bwd(xh, r, g_ref[...], dh2)
        accumulate(out[1], jnp.sum(dh2 * xh, axis=0, keepdims=True))

    my_c = lax.axis_index("c")
    my_chip = 2 * lax.axis_index("x") + lax.axis_index("y")
    idx = jnp.stack([my_c, my_chip]).astype(jnp.int32)
    parts16, owns = {}, {}

    def split_cores(g8):
        return g8.reshape((4, 2) + g8.shape[1:])

    def row_tile(r):
        return 512 if r % 512 == 0 else r

    def pair_sums(names, grads4, from_sibling):
        for name, g4, recv in zip(names, grads4, from_sibling):
            parts16[name], owns[name] = _pair_sum("grad_pair_sum_" + name, g4, recv, idx, row_tile(g4.shape[2]))

    grads4_mlp = [split_cores(g_up), split_cores(g_down.reshape(N_DEV, D_FF // N_DEV, d))]
    (dx1, g_norm2), from_sibling = _mm(
        "mlp_up_bwd", [(da, w_up_t, False, None)],
        [(x1, _row(tr, d)), (r2, _row(tr, 1)), (norm2_g, _whole((1, d))), (dx2, _row(tr, d))],
        [(_sds((s_len, d), F32), _row(tr, d)), (_sds((1, d), F32), _whole((1, d)))],
        dh2_epi, m=s_len, tm=tr, n=d, tn=d, arbitrary=True, comm=[_pair_exchange_plan(grads4_mlp)])
    pair_sums(("w_up", "w_down"), grads4_mlp, from_sibling)

    def dmerge_epi(accs, ex, out):
        dm = accs[0]
        sa, sb = _sigmoid(ex[0][...].astype(F32)), _sigmoid(ex[1][...].astype(F32))
        out[0][...] = (dm * sa).astype(BF16)
        out[1][...] = (dm * sb).astype(BF16)
        dga = (dm * ex[2][...] * sa * (1.0 - sa)).astype(BF16)
        dgb = (dm * ex[3][...] * sb * (1.0 - sb)).astype(BF16)
        out[2][...] = dga
        out[3][...] = dgb
        out[4][:d, :] = jnp.transpose(dga)
        out[4][d:, :] = jnp.transpose(dgb)

    dya, dyb, dga, dgb, dg_t = _mm(
        "out_proj_bwd", [(dx1, w_o_f, True, None)],
        [(prest, _tile(tm, d, GA_OFF // d)), (prest, _tile(tm, d, GB_OFF // d)), (ya, _tile(tm, d)), (yb, _tile(tm, d))],
        [(_sds((s_len, d), BF16), _tile(tm, d))] * 4
        + [(_sds((2 * d, s_len), BF16), pl.BlockSpec((2 * d, tm), lambda i, j: (0, i)))],
        dmerge_epi, m=s_len, tm=tm, n=d, tn=d)
    g_o = _grad_w("grad_w_o", merged_t, dx1, tk=d, tn=d, ts=tl).reshape(N_DEV, d // N_DEV, d)
    def col_blocks(g):
        return jnp.transpose(g.reshape(g.shape[0], N_DEV, g.shape[1] // N_DEV), (1, 0, 2))

    g_a = col_blocks(_grad_w("grad_w_a", o_t, dya, tk=FOX_W, tn=d, ts=tl))
    g_b = col_blocks(_grad_w("grad_w_b", sg_t, dyb, tk=SGU_W, tn=d, ts=tl))

    def do_epi(accs, ex, out):
        do = accs[0]
        out[0][...] = do
        out[1][...] = _dot_f32(do * ex[0][...], ex[1][...])

    grads4_mix = [split_cores(g) for g in (g_a, g_b, g_o)]
    (do, delta), from_sibling = _mm(
        "attn_out_bwd", [(dya, w_a_t, False, None)], [(o, _row(tm, FOX_W)), (head_sel, _whole((FOX_W, LANES)))],
        [(_sds((s_len, FOX_W), F32), _row(tm, FOX_W)), (_sds((s_len, LANES), F32), _row(tm, LANES))],
        do_epi, m=s_len, tm=tm, n=FOX_W, tn=FOX_W, comm=[_pair_exchange_plan(grads4_mix)])
    pair_sums(("w_a", "w_b", "w_o"), grads4_mix, from_sibling)
    (dsg,) = _mm("sgu_out_bwd", [(dyb, w_b_t, False, None)], [], [(_sds((s_len, SGU_W), F32), _tile(tm, SGU_W))],
                 store(F32), m=s_len, tm=tm, n=SGU_W, tn=SGU_W)

    du, dsv, dw_pairs, db_pos, g_ln_g, g_ln_b, dusv_t = _sgu_bwd(
        prest, dsg, ln_v_g, ln_v_b, w_stack, wt_stack, b_pair, tm)
    g_w_sgu = jnp.where(sgu_mask[None], dw_pairs.reshape(SGU_G, SGU_LEN, SGU_LEN), 0.0)
    g_b_sgu = jnp.transpose(jnp.sum(db_pos.reshape(SGU_LEN, SGU_G, SGU_W // SGU_G), axis=-1))

    delta_row = jnp.transpose(delta[:, :N_HEADS]).reshape(N_HEADS, 1, s_len)
    early = ("w_a", "w_b", "w_o", "w_up", "w_down")
    small_early = _pack_rows((g_w_sgu, g_b_sgu, g_norm2, g_normf, g_ln_g, g_ln_b))
    (dq, dk, dv, dc_rows_blk, dc_cols, dkv_t), (small_early_all, *from_chips_early) = _attn_bwd(
        qkv, k_t, do, c_col, lse_row, delta_row, ta, ta,
        comm=[_gather_plan(small_early), _chip_exchange_plan([parts16[n] for n in early])])
    dc_rows = jnp.transpose(dc_rows_blk.reshape(s_len // ta, N_HEADS, ta), (0, 2, 1)).reshape(s_len, N_HEADS)
    dc_rows = jnp.pad(dc_rows, ((0, 0), (0, LANES - N_HEADS)))
    dfl, g_bf, dfl_t = _forget_bwd(dc_rows, dc_cols, f_logit, b_f_pad, tc)

    dp_t = (jnp.transpose(dq), dkv_t, dfl_t, dusv_t, dg_t)
    g_in_rows = [_grad_w("grad_w_in_%d" % k, seg_t, h, tk=min(seg_t.shape[0], 1024), tn=d, ts=tl)
                 for k, seg_t in enumerate(dp_t)]
    g_in_rows[2] = g_in_rows[2][:N_HEADS]
    g_in = jnp.concatenate(g_in_rows, axis=0).reshape((N_DEV,) + lin)

    def dx_epi(accs, ex, out):
        x_ref, r_ref, g_ref, dx1_ref = ex
        dh = accs[0]
        for extra in accs[1:]:
            dh = dh + extra
        r = r_ref[...]
        xh = x_ref[...] * r
        out[0][...] = dx1_ref[...] + _rms_bwd(xh, r, g_ref[...], dh)
        accumulate(out[1], jnp.sum(dh * xh, axis=0, keepdims=True))

    rest_cols = ((du, U_OFF, 512), (dsv, SV_OFF, 512), (dga, GA_OFF, 1024), (dgb, GB_OFF, 1024), (dfl, F_OFF, LANES))
    dx_pairs = [(seg, w_qkv, False, (512 * k, 512 * (k + 1))) for k, seg in enumerate((dq, dk, dv))]
    dx_pairs += [(seg, w_rest, False, (lo, lo + width)) for seg, lo, width in rest_cols]
    grads4_in = [split_cores(g_in)]
    pair_sums(("w_in",), grads4_in, _run_comm("grad_pair_exchange_w_in", [_pair_exchange_plan(grads4_in)]))
    (grad_x, g_norm1), (from_chips_in,) = _mm(
        "proj_bwd", dx_pairs,
        [(xs, _row(tr, d)), (r1, _row(tr, 1)), (norm1_g, _whole((1, d))), (dx1, _row(tr, d))],
        [(_sds((s_len, d), F32), _row(tr, d)), (_sds((1, d), F32), _whole((1, d)))],
        dx_epi, m=s_len, tm=tr, n=d, tn=d, arbitrary=True, comm=[_chip_exchange_plan([parts16["w_in"]])])
    small_late = _pack_rows((g_bf[:, :N_HEADS], g_norm1, loss_part[0, 0]))
    (small_late_all,) = _run_comm("gather_last_grads", [_gather_plan(small_late)])
    from_chips = dict(zip(early, from_chips_early), w_in=from_chips_in)

    names = ("w_in", "w_a", "w_b", "w_o", "w_up", "w_down")
    moments_m = (m_w_in, m_w_a, m_w_b, m_w_o, m_w_up, m_w_down)
    moments_v = (v_w_in, v_w_a, v_w_b, v_w_o, v_w_up, v_w_down)
    big_out = {}
    for name, w, m, v in zip(names, big, moments_m, moments_v):
        own = owns[name]
        transposed = name == "w_in"
        m0, v0 = (jnp.transpose(m[0]).reshape(lin), jnp.transpose(v[0]).reshape(lin)) if transposed else (m[0], v[0])
        res = _adamw_shard("adamw_" + name, own, from_chips[name], w, m0, v0, row_tile(own.shape[0]))
        big_out[name] = [(jnp.transpose(t.reshape(IN_SHARD, d)) if transposed else t)[None] for t in res]

    zero = jnp.zeros((), F32)
    sgu_rows = (SGU_G * SGU_LEN, LANES)
    res_sgu = _adamw_small("adamw_w_sgu", small_early_all, 0, w_sgu.reshape(sgu_rows), m_w_sgu.reshape(sgu_rows),
                           v_w_sgu.reshape(sgu_rows))
    small_out = {"w_sgu": [t.reshape(w_sgu.shape) for t in res_sgu]}
    res_early = _adamw_small(
        "adamw_tiny_early", small_early_all, sgu_rows[0], _pack_rows((b_sgu, norm2_g, normf_g, ln_v_g, ln_v_b)),
        _pack_rows((m_b_sgu, m_norm2_g, m_normf_g, m_ln_v_g, m_ln_v_b)),
        _pack_rows((v_b_sgu, v_norm2_g, v_normf_g, v_ln_v_g, v_ln_v_b)))
    res_late = _adamw_small(
        "adamw_tiny_late", small_late_all, 0, _pack_rows((b_f, norm1_g, zero)), _pack_rows((m_b_f, m_norm1_g, zero)),
        _pack_rows((v_b_f, v_norm1_g, zero)))
    for res, group in ((res_early, _TINY_EARLY), (res_late, _TINY_LATE)):
        unpacked = [_unpack_rows(t, group) for t in res]
        small_out.update({name: [u[name] for u in unpacked] for name, _ in group})
    loss = small_out["loss"][0]

    order = ("norm1_g", "w_in", "b_f", "ln_v_g", "ln_v_b", "w_sgu", "b_sgu", "w_a", "w_b", "w_o", "norm2_g", "w_up",
             "w_down", "normf_g")
    table = {**big_out, **small_out}
    outs = [loss, grad_x[None]]
    for kind in range(4):
        outs += [table[n][kind] for n in order]
    return tuple(outs)
```

```python
import math

import jax
import jax.numpy as jnp
from jax import lax
from jax.experimental import pallas as pl
from jax.experimental.pallas import tpu as pltpu

F32 = jnp.float32
BF16 = jnp.bfloat16

N_DEV = 8
D_MODEL = 1024
N_HEADS = 8
HEAD_DIM = 64
FOX_W = N_HEADS * HEAD_DIM
SGU_G = 8
SGU_W = 512
SGU_LEN = 128
CHUNK = 64
D_FF = 4 * D_MODEL
IN_COLS = 3 * FOX_W + N_HEADS + 2 * SGU_W + 2 * D_MODEL
IN_SHARD = IN_COLS // N_DEV
LANES = 128
QKV_W = 3 * FOX_W
U_OFF, SV_OFF, GA_OFF, GB_OFF, F_OFF = 0, 512, 1024, 2048, 3072
REST_W = F_OFF + LANES
EPS = 1e-6
NEG = -1e30

ADAM_LR = 0.001
ADAM_B1 = 0.9
ADAM_B2 = 0.999
ADAM_EPS = 1e-08
ADAM_WD = 0.01
ADAM_STEP = 10

VMEM_LIMIT = 56 * 1024 * 1024
MESH = pl.DeviceIdType.MESH


def _params(sem=None):
    return pltpu.CompilerParams(dimension_semantics=sem, vmem_limit_bytes=VMEM_LIMIT)


def _dot(a, b):
    return jnp.dot(a, b, preferred_element_type=F32)


def _dot_nt(a, b):
    return lax.dot_general(a, b, (((1,), (1,)), ((), ())), preferred_element_type=F32)


def _split3(x):
    hi = x.astype(BF16)
    rest = x - hi.astype(F32)
    mid = rest.astype(BF16)
    return hi, mid, (rest - mid.astype(F32)).astype(BF16)


def _dot_f32(a, b, exact):
    if exact == "a":
        a16 = a.astype(BF16)
        return sum(_dot(a16, part) for part in _split3(b))
    b16 = b.astype(BF16)
    return sum(_dot(part, b16) for part in _split3(a))


def _sigmoid(x):
    return 1.0 / (1.0 + jnp.exp(-x))


def _log_sigmoid(z):
    return jnp.minimum(z, 0.0) - jnp.log(1.0 + jnp.exp(-jnp.abs(z)))


_GELU_K = math.sqrt(2.0 / math.pi)
_GELU_C = 0.044715


def _gelu(x):
    t = jnp.tanh(_GELU_K * (x + _GELU_C * (x * x * x)))
    return 0.5 * x * (1.0 + t)


def _gelu_grad(x):
    x2 = x * x
    t = jnp.tanh(_GELU_K * (x + _GELU_C * (x2 * x)))
    return 0.5 * (1.0 + t) + 0.5 * x * (1.0 - t * t) * (_GELU_K * (1.0 + 3.0 * _GELU_C * x2))


def _rms_bwd(xh, r, g, dy):
    gy = dy * g
    return r * (gy - xh * jnp.mean(xh * gy, axis=-1, keepdims=True))


def _lane_lt64(shape):
    return lax.broadcasted_iota(jnp.int32, shape, len(shape) - 1) < HEAD_DIM


class _Comm:
    def __init__(self, arrays, out_shapes, sems, start, finish, mid=None):
        self.arrays, self.out_shapes, self.sems = list(arrays), list(out_shapes), list(sems)
        self.start, self.mid, self.finish = start, mid, finish


def _comm_phase(plans, phase, in_refs, out_refs, sem_refs):
    ia = io = ks = 0
    for plan in plans:
        na, no, ns = len(plan.arrays), len(plan.out_shapes), len(plan.sems)
        fn = getattr(plan, phase)
        if fn is not None:
            fn(in_refs[ia:ia + na], out_refs[io:io + no], sem_refs[ks:ks + ns])
        ia, io, ks = ia + na, io + no, ks + ns


def _comm_operands(plans):
    arrays = [a for plan in plans for a in plan.arrays]
    out_shapes = [o for plan in plans for o in plan.out_shapes]
    sems = [s for plan in plans for s in plan.sems]
    return arrays, out_shapes, sems


_ANY = pl.BlockSpec(memory_space=pl.ANY)


def _run_comm(name, plans):
    arrays, out_shapes, sems = _comm_operands(plans)
    n_in, n_out = len(arrays), len(out_shapes)

    def body(*refs):
        parts = refs[:n_in], refs[n_in:n_in + n_out], refs[n_in + n_out:]
        for phase in ("start", "mid", "finish"):
            _comm_phase(plans, phase, *parts)

    return pl.pallas_call(
        body, name=name, out_shape=out_shapes, in_specs=[_ANY] * n_in, out_specs=[_ANY] * n_out, scratch_shapes=sems,
    )(*arrays)


def _gather_plan(shard):
    def setup(ins, outs, sems):
        (x_ref,), (out_ref,), (send_sems, recv_sems, local_sem) = ins, outs, sems
        x, y, c = lax.axis_index("x"), lax.axis_index("y"), lax.axis_index("c")
        me, sibling = (x, y, c), (x, y, 1 - c)
        chips = [(1 - x, y), (x, 1 - y), (1 - x, 1 - y)]

        def rows(px, py, pc):
            return out_ref.at[4 * px + 2 * py + pc]

        def copy(k, block, to, src=None):
            return pltpu.make_async_remote_copy(
                src_ref=rows(*block) if src is None else src,
                dst_ref=rows(*block),
                send_sem=send_sems.at[k],
                recv_sem=recv_sems.at[k],
                device_id=to,
                device_id_type=MESH,
            )

        mine = pltpu.make_async_copy(x_ref, rows(*me), local_sem)
        first = [copy(0, me, sibling, src=x_ref)]
        first += [copy(1 + j, me, (*chip, c), src=x_ref) for j, chip in enumerate(chips)]
        passed = [copy(4 + j, (*chip, c), sibling) for j, chip in enumerate(chips)]
        landed = [copy(1 + j, (*chip, c), me) for j, chip in enumerate(chips)]
        from_sibling = [copy(0, sibling, me)] + [copy(4 + j, (*chip, 1 - c), me) for j, chip in enumerate(chips)]
        return mine, first, passed, landed, from_sibling

    def start(ins, outs, sems):
        mine, first, _, _, _ = setup(ins, outs, sems)
        mine.start()
        for cp in first:
            cp.start()

    def mid(ins, outs, sems):
        _, _, passed, landed, _ = setup(ins, outs, sems)
        for arrived, onward in zip(landed, passed):
            arrived.wait_recv()
            onward.start()

    def finish(ins, outs, sems):
        mine, first, passed, _, from_sibling = setup(ins, outs, sems)
        for cp in from_sibling:
            cp.wait_recv()
        for cp in first + passed:
            cp.wait_send()
        mine.wait()

    return _Comm([shard], [jax.ShapeDtypeStruct((N_DEV,) + shard.shape, shard.dtype)],
                 [pltpu.SemaphoreType.DMA((7,)), pltpu.SemaphoreType.DMA((7,)), pltpu.SemaphoreType.DMA],
                 start, finish, mid)


def _start_all(copies):
    for cp in copies:
        cp.start()


def _wait_all(copies):
    for cp in copies:
        cp.wait_recv()
    for cp in copies:
        cp.wait_send()


def _pair_exchange_plan(grads):
    n = len(grads)

    def copies(ins, outs, sems):
        send_sems, recv_sems = sems
        x, y, c = lax.axis_index("x"), lax.axis_index("y"), lax.axis_index("c")
        return [
            pltpu.make_async_remote_copy(
                src_ref=ins[k].at[:, 1 - c],
                dst_ref=outs[k],
                send_sem=send_sems.at[k],
                recv_sem=recv_sems.at[k],
                device_id=(x, y, 1 - c),
                device_id_type=MESH,
            )
            for k in range(n)
        ]

    return _Comm(grads, [jax.ShapeDtypeStruct((4,) + g.shape[2:], g.dtype) for g in grads],
                 [pltpu.SemaphoreType.DMA((n,)), pltpu.SemaphoreType.DMA((n,))],
                 lambda *refs: _start_all(copies(*refs)), lambda *refs: _wait_all(copies(*refs)))


def _chip_exchange_plan(parts):
    n = len(parts)

    def copies(ins, outs, sems):
        send_sems, recv_sems = sems
        x, y, c = lax.axis_index("x"), lax.axis_index("y"), lax.axis_index("c")
        chips = [(1 - x, y), (x, 1 - y), (1 - x, 1 - y)]
        return [
            pltpu.make_async_remote_copy(
                src_ref=ins[k].at[2 * px + py],
                dst_ref=outs[k].at[j],
                send_sem=send_sems.at[3 * k + j],
                recv_sem=recv_sems.at[3 * k + j],
                device_id=(px, py, c),
                device_id_type=MESH,
            )
            for k in range(n) for j, (px, py) in enumerate(chips)
        ]

    return _Comm(parts, [jax.ShapeDtypeStruct((3,) + p.shape[1:], p.dtype) for p in parts],
                 [pltpu.SemaphoreType.DMA((3 * n,)), pltpu.SemaphoreType.DMA((3 * n,))],
                 lambda *refs: _start_all(copies(*refs)), lambda *refs: _wait_all(copies(*refs)))


def _mm(name, pairs, extras, outs, epi, *, m, tm, n, tn, arbitrary=False, comm=()):
    nj = n // tn
    a_arrays, a_specs, b_arrays, b_specs, b_index = [], [], [], [], []
    for a, b, nt, cols in pairs:
        a_arrays.append(a)
        a_specs.append(pl.BlockSpec((tm, a.shape[1]), lambda i, j: (i, 0)))
        known = [k for k, other in enumerate(b_arrays) if other is b]
        if known:
            b_index.append(known[0])
            continue
        b_index.append(len(b_arrays))
        b_arrays.append(b)
        if cols is not None:
            assert nj == 1
            b_specs.append(pl.BlockSpec(b.shape, lambda i, j: (0, 0)))
        elif nt:
            b_specs.append(pl.BlockSpec((tn, b.shape[1]), lambda i, j: (j, 0)))
        else:
            b_specs.append(pl.BlockSpec((b.shape[0], tn), lambda i, j: (0, j)))
    comm_arrays, comm_outs, comm_sems = _comm_operands(comm)
    arrays = a_arrays + b_arrays + [arr for arr, _ in extras] + comm_arrays
    in_specs = a_specs + b_specs + [spec for _, spec in extras] + [_ANY] * len(comm_arrays)
    n_a, n_b, n_extras, n_ci, n_out, n_co = len(a_arrays), len(b_arrays), len(extras), len(comm_arrays), len(outs), len(comm_outs)
    ni = m // tm

    def body(*refs):
        a_refs = refs[:n_a]
        b_refs = refs[n_a:n_a + n_b]
        ex = refs[n_a + n_b:n_a + n_b + n_extras]
        n_in = n_a + n_b + n_extras + n_ci
        comm_refs = refs[n_in - n_ci:n_in], refs[n_in + n_out:n_in + n_out + n_co], refs[n_in + n_out + n_co:]
        out = refs[n_in:n_in + n_out]
        if comm:
            @pl.when(jnp.logical_and(pl.program_id(0) == 0, pl.program_id(1) == 0))
            def _():
                _comm_phase(comm, "start", *comm_refs)

        accs = []
        for p, (_, _, nt, cols) in enumerate(pairs):
            av = a_refs[p][...]
            if av.dtype != BF16:
                av = av.astype(BF16)
            b_ref = b_refs[b_index[p]]
            if cols is None:
                bv = b_ref[...]
            else:
                bv = b_ref[:, cols[0]:cols[1]] if nt else b_ref[cols[0]:cols[1], :]
            accs.append(_dot_nt(av, bv) if nt else _dot(av, bv))
        epi(accs, ex, out)
        if comm:
            mid_row = ni // 2 if ni >= 3 else ni - 1
            mid_col = 0 if ni >= 3 else nj - 1

            @pl.when(jnp.logical_and(pl.program_id(0) == mid_row, pl.program_id(1) == mid_col))
            def _():
                _comm_phase(comm, "mid", *comm_refs)

            @pl.when(jnp.logical_and(pl.program_id(0) == ni - 1, pl.program_id(1) == nj - 1))
            def _():
                _comm_phase(comm, "finish", *comm_refs)

    sem = ("arbitrary", "arbitrary") if arbitrary or comm else ("parallel", "parallel")
    res = pl.pallas_call(
        body,
        name=name,
        grid=(ni, nj),
        in_specs=in_specs,
        out_specs=[spec for _, spec in outs] + [_ANY] * n_co,
        out_shape=[shape for shape, _ in outs] + comm_outs,
        scratch_shapes=comm_sems,
        compiler_params=_params(sem),
    )(*arrays)
    return (res[:n_out], res[n_out:]) if comm else res


def _tile(tm, tn, off=0):
    return pl.BlockSpec((tm, tn), lambda i, j: (i, j + off))


def _row(tm, w, blk=0):
    return pl.BlockSpec((tm, w), lambda i, j: (i, blk))


def _whole(shape):
    zeros = (0,) * len(shape)
    return pl.BlockSpec(shape, lambda i, j: zeros)


def _sds(shape, dtype):
    return jax.ShapeDtypeStruct(shape, dtype)


def _tile_t(tm, tn):
    return pl.BlockSpec((tn, tm), lambda i, j: (j, i))


def _grad_w(name, a_t, g, *, tk, tn, ts, block_cols=None):
    ka, s_len = a_t.shape
    n = g.shape[1]
    width = tn if block_cols is None else block_cols

    def body(a_ref, g_ref, o_ref):
        first = pl.program_id(2) == 0
        gv = g_ref[...].astype(BF16)
        for b in range(tn // width):
            part = _dot(a_ref[...], gv[:, b * width:(b + 1) * width])
            dst = o_ref if block_cols is None else o_ref.at[b]

            @pl.when(first)
            def _():
                dst[...] = part

            @pl.when(jnp.logical_not(first))
            def _():
                dst[...] += part

    if block_cols is None:
        out_shape = _sds((ka, n), F32)
        out_spec = pl.BlockSpec((tk, tn), lambda i, j, s: (i, j))
    else:
        out_shape = _sds((n // width, ka, width), F32)
        out_spec = pl.BlockSpec((tn // width, tk, width), lambda i, j, s: (j, i, 0))
    return pl.pallas_call(
        body,
        name=name,
        grid=(ka // tk, n // tn, s_len // ts),
        in_specs=[pl.BlockSpec((tk, ts), lambda i, j, s: (i, s)), pl.BlockSpec((ts, tn), lambda i, j, s: (s, j))],
        out_specs=out_spec,
        out_shape=out_shape,
        compiler_params=_params(("parallel", "parallel", "arbitrary")),
    )(a_t, g)


def _rms_fwd(name, x, g, tm, comm=()):
    s_len, d = x.shape
    steps = s_len // tm
    comm_arrays, comm_outs, comm_sems = _comm_operands(comm)
    n_ci, n_co = len(comm_arrays), len(comm_outs)

    def body(x_ref, g_ref, *rest):
        comm_refs = rest[:n_ci], rest[n_ci + 3:n_ci + 3 + n_co], rest[n_ci + 3 + n_co:]
        h_ref, ht_ref, r_ref = rest[n_ci:n_ci + 3]
        for phase, at in (("start", 0), ("mid", steps // 2)):
            if comm:
                @pl.when(pl.program_id(0) == at)
                def _():
                    _comm_phase(comm, phase, *comm_refs)

        xv = x_ref[...]
        r = lax.rsqrt(jnp.mean(xv * xv, axis=-1, keepdims=True) + EPS)
        h = (xv * r * g_ref[...]).astype(BF16)
        h_ref[...] = h
        ht_ref[...] = jnp.transpose(h)
        r_ref[...] = r
        if comm:
            @pl.when(pl.program_id(0) == steps - 1)
            def _():
                _comm_phase(comm, "finish", *comm_refs)

    res = pl.pallas_call(
        body,
        name=name,
        grid=(steps,),
        in_specs=[pl.BlockSpec((tm, d), lambda i: (i, 0)), pl.BlockSpec((1, d), lambda i: (0, 0))] + [_ANY] * n_ci,
        out_specs=[pl.BlockSpec((tm, d), lambda i: (i, 0)), pl.BlockSpec((d, tm), lambda i: (0, i)),
                   pl.BlockSpec((tm, 1), lambda i: (i, 0))] + [_ANY] * n_co,
        out_shape=[_sds((s_len, d), BF16), _sds((d, s_len), BF16), _sds((s_len, 1), F32)] + comm_outs,
        scratch_shapes=comm_sems,
        compiler_params=_params(("arbitrary",) if comm else ("parallel",)),
    )(x, g, *comm_arrays)
    return res[0], res[1], res[2], res[3:]


def _forget_cumsum(prest, b_f_pad, tc):
    s_len = prest.shape[0]

    def body(f_ref, b_ref, c_ref, carry):
        @pl.when(pl.program_id(0) == 0)
        def _():
            carry[...] = jnp.zeros_like(carry)

        logf = _log_sigmoid(f_ref[...] + b_ref[...])
        row = lax.broadcasted_iota(jnp.int32, (tc, tc), 0)
        col = lax.broadcasted_iota(jnp.int32, (tc, tc), 1)
        tri = (row >= col).astype(F32)
        c = _dot_f32(tri, logf, "a") + carry[...]
        c_ref[...] = c
        carry[...] = c[tc - 1:tc, :]

    return pl.pallas_call(
        body,
        name="forget_cumsum",
        grid=(s_len // tc,),
        in_specs=[pl.BlockSpec((tc, LANES), lambda i: (i, 0)), pl.BlockSpec((1, LANES), lambda i: (0, 0))],
        out_specs=pl.BlockSpec((tc, LANES), lambda i: (i, 0)),
        out_shape=_sds((s_len, LANES), F32),
        scratch_shapes=[pltpu.VMEM((1, LANES), F32)],
        compiler_params=_params(("arbitrary",)),
    )(prest, b_f_pad)


def _stack_heads(pair, lt64):
    zero = jnp.zeros_like(pair)
    return jnp.concatenate([jnp.where(lt64, pair, zero), jnp.where(lt64, zero, pair)], axis=0)


def _score_tiles(q_ref, k_ref, ck_ref, st_sc, tk):
    lt64 = _lane_lt64((tk, LANES))
    for p in range(N_HEADS // 2):
        lanes = slice(p * LANES, (p + 1) * LANES)
        q_pair = q_ref[:, lanes] * jnp.asarray(HEAD_DIM ** -0.5, BF16)
        st2 = _dot_nt(_stack_heads(k_ref[:, lanes], lt64), q_pair)
        for half in range(2):
            h = 2 * p + half
            st_sc[h] = st2[half * tk:(half + 1) * tk] - ck_ref[:, h:h + 1]


ROW_CHUNK = 64


def _row_chunks(tk):
    rc = min(ROW_CHUNK, tk)
    return [slice(r, r + rc) for r in range(0, tk, rc)]


def _by_sublane(x):
    return x.reshape(x.shape[0] // 8, 8, x.shape[1])


def _softmax_update(st_sc, p_sc, m_sc, l_sc, tk, tq):
    alphas = []
    for h in range(N_HEADS):
        top8 = jnp.full((8, tq), NEG, F32)
        for rows in _row_chunks(tk):
            top8 = jnp.maximum(top8, jnp.max(_by_sublane(st_sc[h, rows, :]), axis=0))
        m_old = m_sc[h]
        m_new = jnp.maximum(m_old, jnp.max(top8, axis=0, keepdims=True))
        sum8 = jnp.zeros((8, tq), F32)
        for rows in _row_chunks(tk):
            pt = jnp.exp(st_sc[h, rows, :] - m_new)
            p_sc[h, rows, :] = pt.astype(BF16)
            sum8 = sum8 + jnp.sum(_by_sublane(pt), axis=0)
        alpha = jnp.exp(m_old - m_new)
        l_sc[h] = alpha * l_sc[h] + jnp.sum(sum8, axis=0, keepdims=True)
        m_sc[h] = m_new
        alphas.append(alpha)
    return alphas


def _mask_diagonal(st_sc, i, j, tq, tk):
    @pl.when((j + 1) * tk - 1 > i * tq)
    def _():
        key = j * tk + lax.broadcasted_iota(jnp.int32, (tk, tq), 0)
        query = i * tq + lax.broadcasted_iota(jnp.int32, (tk, tq), 1)
        st_sc[...] = jnp.where((query >= key)[None], st_sc[...], NEG)


def _attn_fwd(qkv, v_t, c_col, tq, tk, comm=()):
    s_len = qkv.shape[0]
    ratio = tq // tk
    steps = [(i, j) for i in range(s_len // tq) for j in range((i + 1) * ratio)]
    i_tab = jnp.asarray([i for i, _ in steps], jnp.int32)
    j_tab = jnp.asarray([j for _, j in steps], jnp.int32)

    comm_arrays, comm_outs, comm_sems = _comm_operands(comm)
    n_ci, n_co = len(comm_arrays), len(comm_outs)

    def body(i_ref, j_ref, q_ref, k_ref, vt_ref, ck_ref, *rest):
        comm_refs = rest[:n_ci], rest[n_ci + 3:n_ci + 3 + n_co], rest[n_ci + 3 + n_co + 5:]
        o_ref, ot_ref, lse_ref = rest[n_ci:n_ci + 3]
        acc_t, m_sc, l_sc, st_sc, p_sc = rest[n_ci + 3 + n_co:n_ci + 3 + n_co + 5]
        n = pl.program_id(0)
        i, j = i_ref[n], j_ref[n]
        for phase, at in (("start", 0), ("mid", (2 * len(steps)) // 3)):
            if comm:
                @pl.when(n == at)
                def _():
                    _comm_phase(comm, phase, *comm_refs)

        @pl.when(j == 0)
        def _():
            acc_t[...] = jnp.zeros_like(acc_t)
            m_sc[...] = jnp.full_like(m_sc, NEG)
            l_sc[...] = jnp.zeros_like(l_sc)

        _score_tiles(q_ref, k_ref, ck_ref, st_sc, tk)
        _mask_diagonal(st_sc, i, j, tq, tk)
        alpha = _softmax_update(st_sc, p_sc, m_sc, l_sc, tk, tq)
        top = lax.broadcasted_iota(jnp.int32, (LANES, tq), 0) < HEAD_DIM
        for p in range(N_HEADS // 2):
            lanes = slice(p * LANES, (p + 1) * LANES)
            vt_pair = vt_ref[lanes, :]
            pv = jnp.where(top, _dot(vt_pair, p_sc[2 * p]), _dot(vt_pair, p_sc[2 * p + 1]))
            acc_t[lanes, :] = acc_t[lanes, :] * jnp.where(top, alpha[2 * p], alpha[2 * p + 1]) + pv

        @pl.when(j == (i + 1) * ratio - 1)
        def _():
            for p in range(N_HEADS // 2):
                lanes = slice(p * LANES, (p + 1) * LANES)
                l_pair = jnp.where(top, l_sc[2 * p], l_sc[2 * p + 1])
                o_t = acc_t[lanes, :] / l_pair
                o_ref[:, lanes] = jnp.transpose(o_t)
                ot_ref[lanes, :] = o_t.astype(BF16)
            lse_ref[...] = m_sc[...] + jnp.log(l_sc[...])

        if comm:
            @pl.when(n == len(steps) - 1)
            def _():
                _comm_phase(comm, "finish", *comm_refs)

    stat = pltpu.VMEM((N_HEADS, 1, tq), F32)
    res = pl.pallas_call(
        body,
        name="attn_fwd",
        grid_spec=pltpu.PrefetchScalarGridSpec(
            num_scalar_prefetch=2,
            grid=(len(steps),),
            in_specs=[
                pl.BlockSpec((tq, FOX_W), lambda n, it, jt: (it[n], 0)),
                pl.BlockSpec((tk, FOX_W), lambda n, it, jt: (jt[n], 1)),
                pl.BlockSpec((FOX_W, tk), lambda n, it, jt: (0, jt[n])),
                pl.BlockSpec((tk, LANES), lambda n, it, jt: (jt[n], 0)),
            ] + [_ANY] * n_ci,
            out_specs=[
                pl.BlockSpec((tq, FOX_W), lambda n, it, jt: (it[n], 0)),
                pl.BlockSpec((FOX_W, tq), lambda n, it, jt: (0, it[n])),
                pl.BlockSpec((N_HEADS, 1, tq), lambda n, it, jt: (0, 0, it[n])),
            ] + [_ANY] * n_co,
            scratch_shapes=[pltpu.VMEM((FOX_W, tq), F32), stat, stat, pltpu.VMEM((N_HEADS, tk, tq), F32),
                            pltpu.VMEM((N_HEADS, tk, tq), BF16)] + comm_sems,
        ),
        out_shape=[_sds((s_len, FOX_W), F32), _sds((FOX_W, s_len), BF16), _sds((N_HEADS, 1, s_len), F32)] + comm_outs,
        compiler_params=_params(("arbitrary",)),
    )(i_tab, j_tab, qkv, qkv, v_t, c_col, *comm_arrays)
    return res[0], res[1], res[2], res[3:]


def _sgu_mix(vn, w_stack, lt64):
    outs = []
    for p in range(SGU_G // 2):
        r = _dot(w_stack[p], vn[:, p * LANES:(p + 1) * LANES])
        outs.append(jnp.where(lt64, r[:SGU_LEN], r[SGU_LEN:]))
    return jnp.concatenate(outs, axis=1)


def _sgu_norm(sv, ln_g, ln_b):
    svg = _gelu(sv)
    xc = svg - jnp.mean(svg, axis=-1, keepdims=True)
    rstd = lax.rsqrt(jnp.mean(xc * xc, axis=-1, keepdims=True) + EPS)
    xhat = xc * rstd
    return xhat, rstd, xhat * ln_g + ln_b


def _sgu_fwd(prest, ln_g, ln_b, w_stack, b_pair, tm):
    s_len = prest.shape[0]

    def body(u_ref, sv_ref, g_ref, b_ref, w_ref, bp_ref, sg_ref, sgt_ref):
        lt64 = _lane_lt64((SGU_LEN, LANES))
        _, _, vn = _sgu_norm(sv_ref[...].astype(F32), g_ref[...], b_ref[...])
        vn = vn.astype(BF16)
        w_stack_v = [w_ref[p] for p in range(SGU_G // 2)]
        for w in range(tm // SGU_LEN):
            win = slice(w * SGU_LEN, (w + 1) * SGU_LEN)
            mixed = _sgu_mix(vn[win], w_stack_v, lt64) + bp_ref[...]
            sg = (_gelu(u_ref[win, :].astype(F32)) * mixed).astype(BF16)
            sg_ref[win, :] = sg
            sgt_ref[:, win] = jnp.transpose(sg)

    return pl.pallas_call(
        body,
        name="sgu_fwd",
        grid=(s_len // tm,),
        in_specs=[
            pl.BlockSpec((tm, SGU_W), lambda i: (i, U_OFF // SGU_W)),
            pl.BlockSpec((tm, SGU_W), lambda i: (i, SV_OFF // SGU_W)),
            pl.BlockSpec((1, SGU_W), lambda i: (0, 0)),
            pl.BlockSpec((1, SGU_W), lambda i: (0, 0)),
            pl.BlockSpec((SGU_G // 2, 2 * SGU_LEN, SGU_LEN), lambda i: (0, 0, 0)),
            pl.BlockSpec((SGU_LEN, SGU_W), lambda i: (0, 0)),
        ],
        out_specs=[pl.BlockSpec((tm, SGU_W), lambda i: (i, 0)), pl.BlockSpec((SGU_W, tm), lambda i: (0, i))],
        out_shape=[_sds((s_len, SGU_W), BF16), _sds((SGU_W, s_len), BF16)],
        compiler_params=_params(("parallel",)),
    )(prest, prest, ln_g, ln_b, w_stack, b_pair)


def _sgu_bwd(prest, dsg, ln_g, ln_b, w_stack, wt_stack, b_pair, tm):
    s_len = prest.shape[0]
    n_pair = SGU_G // 2

    def body(u_ref, sv_ref, dsg_ref, g_ref, b_ref, w_ref, wt_ref, bp_ref,
             du_ref, dsv_ref, dw_ref, db_ref, dg_ref, dbeta_ref, dusvt_ref, dvn_sc):
        @pl.when(pl.program_id(0) == 0)
        def _():
            dw_ref[...] = jnp.zeros_like(dw_ref)
            db_ref[...] = jnp.zeros_like(db_ref)
            dg_ref[...] = jnp.zeros_like(dg_ref)
            dbeta_ref[...] = jnp.zeros_like(dbeta_ref)

        lt64 = _lane_lt64((SGU_LEN, LANES))
        sv = sv_ref[...].astype(F32)
        xhat, rstd, vn32 = _sgu_norm(sv, g_ref[...], b_ref[...])
        vn = vn32.astype(BF16)
        w_stack_v = [w_ref[p] for p in range(n_pair)]
        db = jnp.zeros((SGU_LEN, SGU_W), F32)
        for w in range(tm // SGU_LEN):
            win = slice(w * SGU_LEN, (w + 1) * SGU_LEN)
            u = u_ref[win, :].astype(F32)
            dsg_w = dsg_ref[win, :]
            mixed = _sgu_mix(vn[win], w_stack_v, lt64) + bp_ref[...]
            du = (dsg_w * mixed * _gelu_grad(u)).astype(BF16)
            du_ref[win, :] = du
            dusvt_ref[:SGU_W, win] = jnp.transpose(du)
            dmixed = dsg_w * _gelu(u)
            db = db + dmixed
            dm16 = dmixed.astype(BF16)
            for p in range(n_pair):
                lanes = slice(p * LANES, (p + 1) * LANES)
                dmp = dm16[:, lanes]
                r = _dot(wt_ref[p], dmp)
                dvn_sc[win, lanes] = jnp.where(lt64, r[:SGU_LEN], r[SGU_LEN:])
                zero = jnp.zeros_like(dmp)
                dm_ab = jnp.concatenate([jnp.where(lt64, dmp, zero), jnp.where(lt64, zero, dmp)], axis=0)
                dw_ref[p] += _dot_nt(dm_ab, vn[win, lanes])
        db_ref[...] += db
        dvn = dvn_sc[...]
        dg_ref[...] += jnp.sum(dvn * xhat, axis=0, keepdims=True)
        dbeta_ref[...] += jnp.sum(dvn, axis=0, keepdims=True)
        dxh = dvn * g_ref[...]
        dsvg = rstd * (dxh - jnp.mean(dxh, axis=-1, keepdims=True) - xhat * jnp.mean(dxh * xhat, axis=-1, keepdims=True))
        dsv = (dsvg * _gelu_grad(sv)).astype(BF16)
        dsv_ref[...] = dsv
        dusvt_ref[SGU_W:, :] = jnp.transpose(dsv)

    const2 = lambda i: (0, 0)
    const3 = lambda i: (0, 0, 0)
    return pl.pallas_call(
        body,
        name="sgu_bwd",
        grid=(s_len // tm,),
        in_specs=[
            pl.BlockSpec((tm, SGU_W), lambda i: (i, U_OFF // SGU_W)),
            pl.BlockSpec((tm, SGU_W), lambda i: (i, SV_OFF // SGU_W)),
            pl.BlockSpec((tm, SGU_W), lambda i: (i, 0)),
            pl.BlockSpec((1, SGU_W), const2),
            pl.BlockSpec((1, SGU_W), const2),
            pl.BlockSpec((n_pair, 2 * SGU_LEN, SGU_LEN), const3),
            pl.BlockSpec((n_pair, 2 * SGU_LEN, SGU_LEN), const3),
            pl.BlockSpec((SGU_LEN, SGU_W), const2),
        ],
        out_specs=[
            pl.BlockSpec((tm, SGU_W), lambda i: (i, 0)),
            pl.BlockSpec((tm, SGU_W), lambda i: (i, 0)),
            pl.BlockSpec((n_pair, 2 * SGU_LEN, SGU_LEN), const3),
            pl.BlockSpec((SGU_LEN, SGU_W), const2),
            pl.BlockSpec((1, SGU_W), const2),
            pl.BlockSpec((1, SGU_W), const2),
            pl.BlockSpec((2 * SGU_W, tm), lambda i: (0, i)),
        ],
        out_shape=[
            _sds((s_len, SGU_W), BF16), _sds((s_len, SGU_W), BF16), _sds((n_pair, 2 * SGU_LEN, SGU_LEN), F32),
            _sds((SGU_LEN, SGU_W), F32), _sds((1, SGU_W), F32), _sds((1, SGU_W), F32),
            _sds((2 * SGU_W, s_len), BF16),
        ],
        scratch_shapes=[pltpu.VMEM((tm, SGU_W), F32)],
        compiler_params=_params(("arbitrary",)),
    )(prest, prest, dsg, ln_g, ln_b, w_stack, wt_stack, b_pair)


def _attn_bwd(qkv, k_t, do, c_col, lse_row, delta_row, tq, tk, comm=()):
    s_len = qkv.shape[0]
    nq, nk = s_len // tq, s_len // tk
    ratio = tq // tk
    scale = HEAD_DIM ** -0.5
    steps = [(j, i) for j in range(nk) for i in range(j // ratio, nq)]
    j_tab = jnp.asarray([j for j, _ in steps], jnp.int32)
    i_tab = jnp.asarray([i for _, i in steps], jnp.int32)

    comm_arrays, comm_outs, comm_sems = _comm_operands(comm)
    n_ci, n_co = len(comm_arrays), len(comm_outs)

    def body(j_ref, i_ref, q_ref, k_ref, v_ref, kt_ref, do_ref, ck_ref, lse_ref, dl_ref, *rest):
        comm_refs = rest[:n_ci], rest[n_ci + 6:n_ci + 6 + n_co], rest[n_ci + 6 + n_co + 8:]
        dq_ref, dk_ref, dv_ref, dcr_ref, dcc_ref, dkvt_ref = rest[n_ci:n_ci + 6]
        dq_t, dk_acc, dv_acc, dcc_acc, st_sc, dpt_sc, p_sc, ds_sc = rest[n_ci + 6 + n_co:n_ci + 6 + n_co + 8]
        n = pl.program_id(0)
        j, i = j_ref[n], i_ref[n]

        @pl.when(n == 0)
        def _():
            _comm_phase(comm, "start", *comm_refs)
            dq_t[...] = jnp.zeros_like(dq_t)
            dcr_ref[...] = jnp.zeros_like(dcr_ref)

        @pl.when(i == j // ratio)
        def _():
            dk_acc[...] = jnp.zeros_like(dk_acc)
            dv_acc[...] = jnp.zeros_like(dv_acc)
            dcc_acc[...] = jnp.zeros_like(dcc_acc)

        lt64 = _lane_lt64((tk, LANES))
        _score_tiles(q_ref, k_ref, ck_ref, st_sc, tk)
        for p in range(N_HEADS // 2):
            lanes = slice(p * LANES, (p + 1) * LANES)
            dpt2 = _dot_nt(_stack_heads(v_ref[:, lanes], lt64), do_ref[:, lanes].astype(BF16))
            dpt_sc[2 * p] = dpt2[:tk]
            dpt_sc[2 * p + 1] = dpt2[tk:]
        _mask_diagonal(st_sc, i, j, tq, tk)

        pt = jnp.exp(st_sc[...] - lse_ref[...])
        dst = pt * (dpt_sc[...] - dl_ref[...])
        p_sc[...] = pt.astype(BF16)
        ds_sc[...] = dst.astype(BF16)
        dcr_ref[i] += jnp.sum(dst, axis=1, keepdims=True)
        col_sums = jnp.sum(dst, axis=2, keepdims=True)
        lane = lax.broadcasted_iota(jnp.int32, (tk, LANES), 1)
        dcc = jnp.zeros((tk, LANES), F32)
        for h in range(N_HEADS):
            dcc = jnp.where(lane == h, -col_sums[h], dcc)
        dcc_acc[...] += dcc

        for p in range(N_HEADS // 2):
            lanes = slice(p * LANES, (p + 1) * LANES)
            q_pair = q_ref[:, lanes] * jnp.asarray(scale, BF16)
            dv2 = _dot(p_sc[2 * p:2 * p + 2].reshape(2 * tk, tq), do_ref[:, lanes].astype(BF16))
            dv_acc[:, lanes] += jnp.where(lt64, dv2[:tk], dv2[tk:])
            dk2 = _dot(ds_sc[2 * p:2 * p + 2].reshape(2 * tk, tq), q_pair)
            dk_acc[:, lanes] += jnp.where(lt64, dk2[:tk], dk2[tk:])
            dq2 = _dot(kt_ref[lanes, :], jnp.concatenate([ds_sc[2 * p], ds_sc[2 * p + 1]], axis=1))
            top = lax.broadcasted_iota(jnp.int32, (LANES, tq), 0) < HEAD_DIM
            dq_t[i, lanes, :] += jnp.where(top, dq2[:, :tq], dq2[:, tq:])

        @pl.when(j == (i + 1) * ratio - 1)
        def _():
            rows = pl.ds(pl.multiple_of(i * tq, tq), tq)
            for p in range(N_HEADS // 2):
                lanes = slice(p * LANES, (p + 1) * LANES)
                dq_ref[rows, lanes] = (jnp.transpose(dq_t[i, lanes, :]) * scale).astype(BF16)

        @pl.when(i == nq - 1)
        def _():
            dk16, dv16 = dk_acc[...].astype(BF16), dv_acc[...].astype(BF16)
            dk_ref[...] = dk16
            dv_ref[...] = dv16
            dkvt_ref[:FOX_W, :] = jnp.transpose(dk16)
            dkvt_ref[FOX_W:, :] = jnp.transpose(dv16)
            dcc_ref[...] = dcc_acc[...]

        if comm:
            @pl.when(n == len(steps) // 2)
            def _():
                _comm_phase(comm, "mid", *comm_refs)

            @pl.when(n == len(steps) - 1)
            def _():
                _comm_phase(comm, "finish", *comm_refs)

    q_map = lambda n, jt, it: (it[n], 0)
    q_stat = lambda n, jt, it: (0, 0, it[n])
    k_map = lambda n, jt, it: (jt[n], 0)
    tile = (N_HEADS, tk, tq)
    res = pl.pallas_call(
        body,
        name="attn_bwd",
        grid_spec=pltpu.PrefetchScalarGridSpec(
            num_scalar_prefetch=2,
            grid=(len(steps),),
            in_specs=[
                pl.BlockSpec((tq, FOX_W), q_map),
                pl.BlockSpec((tk, FOX_W), lambda n, jt, it: (jt[n], 1)),
                pl.BlockSpec((tk, FOX_W), lambda n, jt, it: (jt[n], 2)),
                pl.BlockSpec((FOX_W, tk), lambda n, jt, it: (0, jt[n])),
                pl.BlockSpec((tq, FOX_W), q_map),
                pl.BlockSpec((tk, LANES), k_map),
                pl.BlockSpec((N_HEADS, 1, tq), q_stat),
                pl.BlockSpec((N_HEADS, 1, tq), q_stat),
            ] + [_ANY] * n_ci,
            out_specs=[
                pl.BlockSpec((s_len, FOX_W), lambda n, jt, it: (0, 0)),
                pl.BlockSpec((tk, FOX_W), k_map),
                pl.BlockSpec((tk, FOX_W), k_map),
                pl.BlockSpec((nq, N_HEADS, 1, tq), lambda n, jt, it: (0, 0, 0, 0)),
                pl.BlockSpec((tk, LANES), k_map),
                pl.BlockSpec((2 * FOX_W, tk), lambda n, jt, it: (0, jt[n])),
            ] + [_ANY] * n_co,
            scratch_shapes=[pltpu.VMEM((nq, FOX_W, tq), F32), pltpu.VMEM((tk, FOX_W), F32), pltpu.VMEM((tk, FOX_W), F32),
                            pltpu.VMEM((tk, LANES), F32), pltpu.VMEM(tile, F32), pltpu.VMEM(tile, F32),
                            pltpu.VMEM(tile, BF16), pltpu.VMEM(tile, BF16)] + comm_sems,
        ),
        out_shape=[_sds((s_len, FOX_W), BF16), _sds((s_len, FOX_W), BF16), _sds((s_len, FOX_W), BF16),
                   _sds((nq, N_HEADS, 1, tq), F32), _sds((s_len, LANES), F32),
                   _sds((2 * FOX_W, s_len), BF16)] + comm_outs,
        compiler_params=_params(("arbitrary",)),
    )(j_tab, i_tab, qkv, qkv, qkv, k_t, do, c_col, lse_row, delta_row, *comm_arrays)
    return res[:6], res[6:]


def _forget_bwd(dc_rows, dc_cols, prest, b_f_pad, tc):
    s_len = dc_rows.shape[0]
    nb = s_len // tc

    def body(dcr_ref, dc_ref, f_ref, b_ref, df_ref, db_ref, dft_ref, carry):
        @pl.when(pl.program_id(0) == 0)
        def _():
            carry[...] = jnp.zeros_like(carry)
            db_ref[...] = jnp.zeros_like(db_ref)

        row = lax.broadcasted_iota(jnp.int32, (tc, tc), 0)
        col = lax.broadcasted_iota(jnp.int32, (tc, tc), 1)
        tri = (row <= col).astype(F32)
        dlogf = _dot_f32(tri, dcr_ref[...] + dc_ref[...], "a") + carry[...]
        carry[...] = dlogf[0:1, :]
        z = f_ref[...] + b_ref[...]
        lane = lax.broadcasted_iota(jnp.int32, (tc, LANES), 1)
        dz = jnp.where(lane < N_HEADS, dlogf * _sigmoid(-z), 0.0)
        df_ref[...] = dz.astype(BF16)
        dft_ref[...] = jnp.transpose(dz).astype(BF16)
        db_ref[...] += jnp.sum(dz, axis=0, keepdims=True)

    rev = lambda i: (nb - 1 - i, 0)
    return pl.pallas_call(
        body,
        name="forget_bwd",
        grid=(nb,),
        in_specs=[
            pl.BlockSpec((tc, LANES), rev),
            pl.BlockSpec((tc, LANES), rev),
            pl.BlockSpec((tc, LANES), rev),
            pl.BlockSpec((1, LANES), lambda i: (0, 0)),
        ],
        out_specs=[pl.BlockSpec((tc, LANES), rev), pl.BlockSpec((1, LANES), lambda i: (0, 0)),
                   pl.BlockSpec((LANES, tc), lambda i: (0, nb - 1 - i))],
        out_shape=[_sds((s_len, LANES), BF16), _sds((1, LANES), F32), _sds((LANES, s_len), BF16)],
        scratch_shapes=[pltpu.VMEM((1, LANES), F32)],
        compiler_params=_params(("arbitrary",)),
    )(dc_rows, dc_cols, prest, b_f_pad)


def _pair_sum(name, g4, recv, idx, tr):
    _, _, r, c = g4.shape

    def body(idx_ref, g_ref, r_ref, p16_ref, own_ref):
        k = pl.program_id(1)
        s = g_ref[...] + r_ref[...]
        p16_ref[...] = s.astype(BF16)

        @pl.when(k == idx_ref[1])
        def _():
            own_ref[...] = s

    return pl.pallas_call(
        body,
        name=name,
        grid_spec=pltpu.PrefetchScalarGridSpec(
            num_scalar_prefetch=1,
            grid=(r // tr, 4),
            in_specs=[
                pl.BlockSpec((None, None, tr, c), lambda i, k, idx: (k, idx[0], i, 0)),
                pl.BlockSpec((None, tr, c), lambda i, k, idx: (k, i, 0)),
            ],
            out_specs=[
                pl.BlockSpec((None, tr, c), lambda i, k, idx: (k, i, 0)),
                pl.BlockSpec((tr, c), lambda i, k, idx: (i, 0)),
            ],
        ),
        out_shape=[_sds((4, r, c), BF16), _sds((r, c), F32)],
        compiler_params=_params(("parallel", "arbitrary")),
    )(idx, g4, recv)


def _adamw_math(w, g, m, v):
    m2 = ADAM_B1 * m + (1.0 - ADAM_B1) * g
    v2 = ADAM_B2 * v + (1.0 - ADAM_B2) * (g * g)
    m_hat = m2 / (1.0 - ADAM_B1 ** ADAM_STEP)
    v_hat = v2 / (1.0 - ADAM_B2 ** ADAM_STEP)
    delta = -ADAM_LR * (m_hat / (jnp.sqrt(v_hat) + ADAM_EPS) + ADAM_WD * w)
    return delta, m2, v2


def _adamw_shard(name, own, recv, w, m, v, tr):
    r, c = own.shape

    def body(own_ref, recv_ref, w_ref, m_ref, v_ref, g_ref, d_ref, m2_ref, v2_ref):
        g = own_ref[...]
        for k in range(3):
            g = g + recv_ref[k].astype(F32)
        delta, m2, v2 = _adamw_math(w_ref[...], g, m_ref[...], v_ref[...])
        g_ref[...] = g
        d_ref[...] = delta
        m2_ref[...] = m2
        v2_ref[...] = v2

    spec = pl.BlockSpec((tr, c), lambda i: (i, 0))
    return pl.pallas_call(
        body,
        name=name,
        grid=(r // tr,),
        in_specs=[spec, pl.BlockSpec((3, tr, c), lambda i: (0, i, 0)), spec, spec, spec],
        out_specs=[spec] * 4,
        out_shape=[_sds((r, c), F32)] * 4,
        compiler_params=_params(("parallel",)),
    )(own, recv, w, m, v)


def _adamw_small(name, gathered, first_row, w, m, v):
    r = w.shape[0]
    assert first_row % r == 0

    def body(ga_ref, w_ref, m_ref, v_ref, g_ref, d_ref, m2_ref, v2_ref):
        g = ga_ref[0]
        for k in range(1, N_DEV):
            g = g + ga_ref[k]
        delta, m2, v2 = _adamw_math(w_ref[...], g, m_ref[...], v_ref[...])
        g_ref[...] = g
        d_ref[...] = delta
        m2_ref[...] = m2
        v2_ref[...] = v2

    spec = pl.BlockSpec((r, LANES), lambda i: (0, 0))
    return pl.pallas_call(
        body,
        name=name,
        grid=(1,),
        in_specs=[pl.BlockSpec((N_DEV, r, LANES), lambda i: (0, first_row // r, 0)), spec, spec, spec],
        out_specs=[spec] * 4,
        out_shape=[_sds((r, LANES), F32)] * 4,
        compiler_params=_params(("arbitrary",)),
    )(gathered, w, m, v)


_TINY_EARLY = (("b_sgu", (1, SGU_G, SGU_LEN)), ("norm2_g", (1, D_MODEL)), ("normf_g", (D_MODEL,)),
               ("ln_v_g", (1, SGU_W)), ("ln_v_b", (1, SGU_W)))
_TINY_LATE = (("b_f", (1, N_HEADS)), ("norm1_g", (1, D_MODEL)), ("loss", ()))


def _pack_rows(values):
    rows = []
    for val in values:
        flat = val.reshape(-1).astype(F32)
        pad = (-flat.shape[0]) % LANES
        rows.append(jnp.pad(flat, (0, pad)).reshape(-1, LANES))
    packed = jnp.concatenate(rows, axis=0)
    return jnp.pad(packed, ((0, (-packed.shape[0]) % 8), (0, 0)))


def _unpack_rows(packed, group):
    out, row = {}, 0
    for name, shape in group:
        size = math.prod(shape)
        n_rows = -(-size // LANES)
        out[name] = packed[row:row + n_rows].reshape(-1)[:size].reshape(shape)
        row += n_rows
    return out


def kernel(x, norm1_g, w_in, b_f, ln_v_g, ln_v_b, w_sgu, b_sgu, w_a, w_b, w_o, norm2_g, w_up, w_down, normf_g, loss_target, m_norm1_g, m_w_in, m_b_f, m_ln_v_g, m_ln_v_b, m_w_sgu, m_b_sgu, m_w_a, m_w_b, m_w_o, m_norm2_g, m_w_up, m_w_down, m_normf_g, v_norm1_g, v_w_in, v_b_f, v_ln_v_g, v_ln_v_b, v_w_sgu, v_b_sgu, v_w_a, v_w_b, v_w_o, v_norm2_g, v_w_up, v_w_down, v_normf_g):
    xs = x[0]
    target = loss_target[0]
    s_len, d = xs.shape
    tm = min(512, s_len)
    tl = min(1024, s_len)
    tr = min(512, s_len)
    ta = min(512, s_len)
    tc = min(512, s_len)

    w_in_t = jnp.transpose(w_in[0])
    lin = (IN_SHARD * d // LANES, LANES)
    big = (w_in_t.reshape(lin), w_a[0], w_b[0], w_o[0], w_up[0], w_down[0])
    h, h_t, r1, (w_in_g,) = _rms_fwd("rms1", xs, norm1_g, tm, comm=[_gather_plan(w_in_t.astype(BF16))])
    w_in_f = w_in_g.reshape(IN_COLS, d)
    later_shards = [jnp.transpose(w_a[0]), jnp.transpose(w_b[0]), w_o[0], jnp.transpose(w_up[0]), w_down[0]]
    later_plans = [_gather_plan(w.astype(BF16)) for w in later_shards]

    def unflatten(gathered):
        return [g.reshape(N_DEV * g.shape[1], g.shape[2]) for g in gathered]

    w_qkv = w_in_f[:QKV_W]
    f_lo = QKV_W
    u_lo = f_lo + N_HEADS
    w_rest = jnp.concatenate([w_in_f[u_lo:], jnp.pad(w_in_f[f_lo:u_lo], ((0, LANES - N_HEADS), (0, 0)))], axis=0)

    chunk_id = jnp.arange(SGU_LEN) // CHUNK
    sgu_mask = chunk_id[None, :] <= chunk_id[:, None]
    w_masked = jnp.where(sgu_mask[None], w_sgu[0], 0.0)
    w_stack = w_masked.reshape(SGU_G // 2, 2 * SGU_LEN, SGU_LEN).astype(BF16)
    wt_stack = jnp.transpose(w_masked, (0, 2, 1)).reshape(SGU_G // 2, 2 * SGU_LEN, SGU_LEN).astype(BF16)
    b_pair = jnp.transpose(jnp.repeat(b_sgu[0], SGU_W // SGU_G, axis=0))
    b_f_pad = jnp.pad(b_f, ((0, 0), (0, LANES - N_HEADS)))
    head_sel = (jnp.arange(FOX_W)[:, None] // HEAD_DIM == jnp.arange(LANES)[None, :]).astype(F32)

    def store(dtype):
        def epi(accs, ex, out):
            out[0][...] = accs[0].astype(dtype)
        return epi

    def qkv_epi(accs, ex, out):
        tile = accs[0].astype(BF16)
        out[0][...] = tile
        for col, ref in ((1, out[1]), (2, out[2])):
            @pl.when(pl.program_id(1) == col)
            def _():
                ref[...] = jnp.transpose(tile)

    t_spec = pl.BlockSpec((FOX_W, tl), lambda i, j: (0, i))
    qkv, k_t, v_t = _mm("proj_qkv", [(h, w_qkv, True, None)], [],
                        [(_sds((s_len, QKV_W), BF16), _tile(tl, FOX_W)), (_sds((FOX_W, s_len), BF16), t_spec),
                         (_sds((FOX_W, s_len), BF16), t_spec)],
                        qkv_epi, m=s_len, tm=tl, n=QKV_W, tn=FOX_W, arbitrary=True)
    rest_tn = 640
    f_tile, f_lane = F_OFF // rest_tn, F_OFF % rest_tn

    def rest_epi(accs, ex, out):
        out[0][...] = accs[0].astype(BF16)

        @pl.when(pl.program_id(1) == f_tile)
        def _():
            out[1][...] = accs[0][:, f_lane:f_lane + LANES]

    prest, f_logit = _mm("proj_rest", [(h, w_rest, True, None)], [],
                         [(_sds((s_len, REST_W), BF16), _tile(tl, rest_tn)), (_sds((s_len, LANES), F32), _row(tl, LANES))],
                         rest_epi, m=s_len, tm=tl, n=REST_W, tn=rest_tn, arbitrary=True)

    c_col = _forget_cumsum(f_logit, b_f_pad, tc)
    o, o_t, lse_row, later_g = _attn_fwd(qkv, v_t, c_col, ta, ta, comm=later_plans)
    w_a_t, w_b_t, w_o_f, w_up_t, w_down_f = unflatten(later_g)
    sg, sg_t = _sgu_fwd(prest, ln_v_g, ln_v_b, w_stack, b_pair, tm)

    def merge_epi(accs, ex, out):
        ya, yb = accs
        sa, sb = _sigmoid(ex[0][...].astype(F32)), _sigmoid(ex[1][...].astype(F32))
        merged = (sa * ya + sb * yb).astype(BF16)
        out[0][...] = merged
        out[1][...] = ya.astype(BF16)
        out[2][...] = yb.astype(BF16)
        out[3][...] = jnp.transpose(merged)

    merged, ya, yb, merged_t = _mm(
        "merge", [(o, w_a_t, True, None), (sg, w_b_t, True, None)],
        [(prest, _tile(tm, d, GA_OFF // d)), (prest, _tile(tm, d, GB_OFF // d))],
        [(_sds((s_len, d), BF16), _tile(tm, d))] * 3 + [(_sds((d, s_len), BF16), _tile_t(tm, d))],
        merge_epi, m=s_len, tm=tm, n=d, tn=d)

    def resid_epi(accs, ex, out):
        x1v = ex[0][...] + accs[0]
        out[0][...] = x1v
        r = lax.rsqrt(jnp.mean(x1v * x1v, axis=-1, keepdims=True) + EPS)
        h2v = (x1v * r * ex[1][...]).astype(BF16)
        out[1][...] = h2v
        out[2][...] = jnp.transpose(h2v)
        out[3][...] = r

    x1, h2, h2_t, r2 = _mm(
        "out_proj", [(merged, w_o_f, False, None)], [(xs, _tile(tm, d)), (norm2_g, _whole((1, d)))],
        [(_sds((s_len, d), F32), _tile(tm, d)), (_sds((s_len, d), BF16), _tile(tm, d)),
         (_sds((d, s_len), BF16), _tile_t(tm, d)), (_sds((s_len, 1), F32), _row(tm, 1))],
        resid_epi, m=s_len, tm=tm, n=d, tn=d)

    def up_epi(accs, ex, out):
        act = jnp.square(jnp.maximum(accs[0], 0.0)).astype(BF16)
        out[0][...] = act
        out[1][...] = jnp.transpose(act)

    act, act_t = _mm(
        "mlp_up", [(h2, w_up_t, True, None)], [],
        [(_sds((s_len, D_FF), BF16), _tile(tl, 1024)), (_sds((D_FF, s_len), BF16), _tile_t(tl, 1024))],
        up_epi, m=s_len, tm=tl, n=D_FF, tn=1024)

    def first_step():
        return jnp.logical_and(pl.program_id(0) == 0, pl.program_id(1) == 0)

    def accumulate(ref, val):
        @pl.when(first_step())
        def _():
            ref[...] = val

        @pl.when(jnp.logical_not(first_step()))
        def _():
            ref[...] += val

    def final_epi(accs, ex, out):
        x1_ref, t_ref, g_ref = ex
        x2 = x1_ref[...] + accs[0]
        rf = lax.rsqrt(jnp.mean(x2 * x2, axis=-1, keepdims=True) + EPS)
        xh = x2 * rf
        gf = g_ref[...]
        err = xh * gf - t_ref[...]
        dy = err * (1.0 / d)
        dx2 = _rms_bwd(xh, rf, gf, dy)
        out[0][...] = dx2
        accumulate(out[1], jnp.sum(dy * xh, axis=0, keepdims=True))
        part = 0.5 * jnp.sum(jnp.sum(err * err, axis=-1, keepdims=True) * (1.0 / d), axis=0, keepdims=True)
        accumulate(out[2], jnp.broadcast_to(part, (1, LANES)))
        out[3][...] = dx2.astype(BF16)

    gf2 = normf_g.reshape(1, d)
    dx2, g_normf, loss_part, dx2_16 = _mm(
        "mlp_down_loss", [(act, w_down_f, False, None)],
        [(x1, _row(tr, d)), (target, _row(tr, d)), (gf2, _whole((1, d)))],
        [(_sds((s_len, d), F32), _row(tr, d)), (_sds((1, d), F32), _whole((1, d))), (_sds((1, LANES), F32), _whole((1, LANES))),
         (_sds((s_len, d), BF16), _row(tr, d))],
        final_epi, m=s_len, tm=tr, n=d, tn=d, arbitrary=True)

    def dact_epi(accs, ex, out):
        out[0][...] = (accs[0] * (2.0 * jnp.sqrt(ex[0][...].astype(F32)))).astype(BF16)

    (da,) = _mm("mlp_down_bwd", [(dx2_16, w_down_f, True, None)], [(act, _tile(tl, 1024))],
                [(_sds((s_len, D_FF), BF16), _tile(tl, 1024))], dact_epi, m=s_len, tm=tl, n=D_FF, tn=1024)
    g_down = _grad_w("grad_w_down", act_t, dx2_16, tk=1024, tn=d, ts=tl)
    g_up = _grad_w("grad_w_up", h2_t, da, tk=d, tn=1024, ts=tl, block_cols=D_FF // N_DEV)

    def dh2_epi(accs, ex, out):
        x1_ref, r_ref, g_ref, dx2_ref = ex
        r = r_ref[...]
        xh = x1_ref[...] * r
        dh2 = accs[0]
        out[0][...] = dx2_ref[...] + _rms_bwd(xh, r, g_ref[...], dh2)
        accumulate(out[1], jnp.sum(dh2 * xh, axis=0, keepdims=True))

    my_c = lax.axis_index("c")
    my_chip = 2 * lax.axis_index("x") + lax.axis_index("y")
    idx = jnp.stack([my_c, my_chip]).astype(jnp.int32)
    parts16, owns = {}, {}

    def split_cores(g8):
        return g8.reshape((4, 2) + g8.shape[1:])

    def row_tile(r):
        return 512 if r % 512 == 0 else r

    def pair_sums(names, grads4, from_sibling):
        for name, g4, recv in zip(names, grads4, from_sibling):
            parts16[name], owns[name] = _pair_sum("grad_pair_sum_" + name, g4, recv, idx, row_tile(g4.shape[2]))

    grads4_mlp = [split_cores(g_up), split_cores(g_down.reshape(N_DEV, D_FF // N_DEV, d))]
    (dx1, g_norm2), from_sibling = _mm(
        "mlp_up_bwd", [(da, w_up_t, False, None)],
        [(x1, _row(tr, d)), (r2, _row(tr, 1)), (norm2_g, _whole((1, d))), (dx2, _row(tr, d))],
        [(_sds((s_len, d), F32), _row(tr, d)), (_sds((1, d), F32), _whole((1, d)))],
        dh2_epi, m=s_len, tm=tr, n=d, tn=d, arbitrary=True, comm=[_pair_exchange_plan(grads4_mlp)])
    pair_sums(("w_up", "w_down"), grads4_mlp, from_sibling)

    def dmerge_epi(accs, ex, out):
        dm = accs[0]
        sa, sb = _sigmoid(ex[0][...].astype(F32)), _sigmoid(ex[1][...].astype(F32))
        out[0][...] = (dm * sa).astype(BF16)
        out[1][...] = (dm * sb).astype(BF16)
        dga = (dm * ex[2][...] * sa * (1.0 - sa)).astype(BF16)
        dgb = (dm * ex[3][...] * sb * (1.0 - sb)).astype(BF16)
        out[2][...] = dga
        out[3][...] = dgb
        out[4][:d, :] = jnp.transpose(dga)
        out[4][d:, :] = jnp.transpose(dgb)

    dya, dyb, dga, dgb, dg_t = _mm(
        "out_proj_bwd", [(dx1, w_o_f, True, None)],
        [(prest, _tile(tm, d, GA_OFF // d)), (prest, _tile(tm, d, GB_OFF // d)), (ya, _tile(tm, d)), (yb, _tile(tm, d))],
        [(_sds((s_len, d), BF16), _tile(tm, d))] * 4
        + [(_sds((2 * d, s_len), BF16), pl.BlockSpec((2 * d, tm), lambda i, j: (0, i)))],
        dmerge_epi, m=s_len, tm=tm, n=d, tn=d)
    g_o = _grad_w("grad_w_o", merged_t, dx1, tk=d, tn=d, ts=tl).reshape(N_DEV, d // N_DEV, d)
    def col_blocks(g):
        return jnp.transpose(g.reshape(g.shape[0], N_DEV, g.shape[1] // N_DEV), (1, 0, 2))

    g_a = col_blocks(_grad_w("grad_w_a", o_t, dya, tk=FOX_W, tn=d, ts=tl))
    g_b = col_blocks(_grad_w("grad_w_b", sg_t, dyb, tk=SGU_W, tn=d, ts=tl))

    def do_epi(accs, ex, out):
        do = accs[0]
        out[0][...] = do
        out[1][...] = _dot_f32(do * ex[0][...], ex[1][...], "b")

    grads4_mix = [split_cores(g) for g in (g_a, g_b, g_o)]
    (do, delta), from_sibling = _mm(
        "attn_out_bwd", [(dya, w_a_t, False, None)], [(o, _row(tm, FOX_W)), (head_sel, _whole((FOX_W, LANES)))],
        [(_sds((s_len, FOX_W), F32), _row(tm, FOX_W)), (_sds((s_len, LANES), F32), _row(tm, LANES))],
        do_epi, m=s_len, tm=tm, n=FOX_W, tn=FOX_W, comm=[_pair_exchange_plan(grads4_mix)])
    pair_sums(("w_a", "w_b", "w_o"), grads4_mix, from_sibling)
    (dsg,) = _mm("sgu_out_bwd", [(dyb, w_b_t, False, None)], [], [(_sds((s_len, SGU_W), F32), _tile(tm, SGU_W))],
                 store(F32), m=s_len, tm=tm, n=SGU_W, tn=SGU_W)

    du, dsv, dw_pairs, db_pos, g_ln_g, g_ln_b, dusv_t = _sgu_bwd(
        prest, dsg, ln_v_g, ln_v_b, w_stack, wt_stack, b_pair, tm)
    g_w_sgu = jnp.where(sgu_mask[None], dw_pairs.reshape(SGU_G, SGU_LEN, SGU_LEN), 0.0)
    g_b_sgu = jnp.transpose(jnp.sum(db_pos.reshape(SGU_LEN, SGU_G, SGU_W // SGU_G), axis=-1))

    delta_row = jnp.transpose(delta[:, :N_HEADS]).reshape(N_HEADS, 1, s_len)
    early = ("w_a", "w_b", "w_o", "w_up", "w_down")
    small_early = _pack_rows((g_w_sgu, g_b_sgu, g_norm2, g_normf, g_ln_g, g_ln_b))
    (dq, dk, dv, dc_rows_blk, dc_cols, dkv_t), (small_early_all, *from_chips_early) = _attn_bwd(
        qkv, k_t, do, c_col, lse_row, delta_row, ta, ta,
        comm=[_gather_plan(small_early), _chip_exchange_plan([parts16[n] for n in early])])
    dc_rows = jnp.transpose(dc_rows_blk.reshape(s_len // ta, N_HEADS, ta), (0, 2, 1)).reshape(s_len, N_HEADS)
    dc_rows = jnp.pad(dc_rows, ((0, 0), (0, LANES - N_HEADS)))
    dfl, g_bf, dfl_t = _forget_bwd(dc_rows, dc_cols, f_logit, b_f_pad, tc)

    dp_t = (jnp.transpose(dq), dkv_t, dfl_t, dusv_t, dg_t)
    g_in_rows = [_grad_w("grad_w_in_%d" % k, seg_t, h, tk=min(seg_t.shape[0], 1024), tn=d, ts=tl)
                 for k, seg_t in enumerate(dp_t)]
    g_in_rows[2] = g_in_rows[2][:N_HEADS]
    g_in = jnp.concatenate(g_in_rows, axis=0).reshape((N_DEV,) + lin)

    def dx_epi(accs, ex, out):
        x_ref, r_ref, g_ref, dx1_ref = ex
        dh = accs[0]
        for extra in accs[1:]:
            dh = dh + extra
        r = r_ref[...]
        xh = x_ref[...] * r
        out[0][...] = dx1_ref[...] + _rms_bwd(xh, r, g_ref[...], dh)
        accumulate(out[1], jnp.sum(dh * xh, axis=0, keepdims=True))

    rest_cols = ((du, U_OFF, 512), (dsv, SV_OFF, 512), (dga, GA_OFF, 1024), (dgb, GB_OFF, 1024), (dfl, F_OFF, LANES))
    dx_pairs = [(seg, w_qkv, False, (512 * k, 512 * (k + 1))) for k, seg in enumerate((dq, dk, dv))]
    dx_pairs += [(seg, w_rest, False, (lo, lo + width)) for seg, lo, width in rest_cols]
    grads4_in = [split_cores(g_in)]
    pair_sums(("w_in",), grads4_in, _run_comm("grad_pair_exchange_w_in", [_pair_exchange_plan(grads4_in)]))
    (grad_x, g_norm1), (from_chips_in,) = _mm(
        "proj_bwd", dx_pairs,
        [(xs, _row(tr, d)), (r1, _row(tr, 1)), (norm1_g, _whole((1, d))), (dx1, _row(tr, d))],
        [(_sds((s_len, d), F32), _row(tr, d)), (_sds((1, d), F32), _whole((1, d)))],
        dx_epi, m=s_len, tm=tr, n=d, tn=d, arbitrary=True, comm=[_chip_exchange_plan([parts16["w_in"]])])
    small_late = _pack_rows((g_bf[:, :N_HEADS], g_norm1, loss_part[0, 0]))
    (small_late_all,) = _run_comm("gather_last_grads", [_gather_plan(small_late)])
    from_chips = dict(zip(early, from_chips_early), w_in=from_chips_in)

    names = ("w_in", "w_a", "w_b", "w_o", "w_up", "w_down")
    moments_m = (m_w_in, m_w_a, m_w_b, m_w_o, m_w_up, m_w_down)
    moments_v = (v_w_in, v_w_a, v_w_b, v_w_o, v_w_up, v_w_down)
    big_out = {}
    for name, w, m, v in zip(names, big, moments_m, moments_v):
        own = owns[name]
        transposed = name == "w_in"
        m0, v0 = (jnp.transpose(m[0]).reshape(lin), jnp.transpose(v[0]).reshape(lin)) if transposed else (m[0], v[0])
        res = _adamw_shard("adamw_" + name, own, from_chips[name], w, m0, v0, row_tile(own.shape[0]))
        big_out[name] = [(jnp.transpose(t.reshape(IN_SHARD, d)) if transposed else t)[None] for t in res]

    zero = jnp.zeros((), F32)
    sgu_rows = (SGU_G * SGU_LEN, LANES)
    res_sgu = _adamw_small("adamw_w_sgu", small_early_all, 0, w_sgu.reshape(sgu_rows), m_w_sgu.reshape(sgu_rows),
                           v_w_sgu.reshape(sgu_rows))
    small_out = {"w_sgu": [t.reshape(w_sgu.shape) for t in res_sgu]}
    res_early = _adamw_small(
        "adamw_tiny_early", small_early_all, sgu_rows[0], _pack_rows((b_sgu, norm2_g, normf_g, ln_v_g, ln_v_b)),
        _pack_rows((m_b_sgu, m_norm2_g, m_normf_g, m_ln_v_g, m_ln_v_b)),
        _pack_rows((v_b_sgu, v_norm2_g, v_normf_g, v_ln_v_g, v_ln_v_b)))
    res_late = _adamw_small(
        "adamw_tiny_late", small_late_all, 0, _pack_rows((b_f, norm1_g, zero)), _pack_rows((m_b_f, m_norm1_g, zero)),
        _pack_rows((v_b_f, v_norm1_g, zero)))
    for res, group in ((res_early, _TINY_EARLY), (res_late, _TINY_LATE)):
        unpacked = [_unpack_rows(t, group) for t in res]
        small_out.update({name: [u[name] for u in unpacked] for name, _ in group})
    loss = small_out["loss"][0]

    order = ("norm1_g", "w_in", "b_f", "ln_v_g", "ln_v_b", "w_sgu", "b_sgu", "w_a", "w_b", "w_o", "norm2_g", "w_up",
             "w_down", "normf_g")
    table = {**big_out, **small_out}
    outs = [loss, grad_x[None]]
    for kind in range(4):
        outs += [table[n][kind] for n in order]
    return tuple(outs)
```

```python
import math

import jax
import jax.numpy as jnp
from jax import lax
from jax.experimental import pallas as pl
from jax.experimental.pallas import tpu as pltpu

F32 = jnp.float32
BF16 = jnp.bfloat16

N_DEV = 8
D_MODEL = 1024
N_HEADS = 8
HEAD_DIM = 64
FOX_W = N_HEADS * HEAD_DIM
SGU_G = 8
SGU_W = 512
SGU_LEN = 128
CHUNK = 64
D_FF = 4 * D_MODEL
IN_COLS = 3 * FOX_W + N_HEADS + 2 * SGU_W + 2 * D_MODEL
IN_SHARD = IN_COLS // N_DEV
LANES = 128
QKV_W = 3 * FOX_W
U_OFF, SV_OFF, GA_OFF, GB_OFF, F_OFF = 0, 512, 1024, 2048, 3072
REST_W = F_OFF + LANES
EPS = 1e-6
NEG = -1e30

ADAM_LR = 0.001
ADAM_B1 = 0.9
ADAM_B2 = 0.999
ADAM_EPS = 1e-08
ADAM_WD = 0.01
ADAM_STEP = 10

VMEM_LIMIT = 56 * 1024 * 1024
MESH = pl.DeviceIdType.MESH


def _params(sem=None):
    return pltpu.CompilerParams(dimension_semantics=sem, vmem_limit_bytes=VMEM_LIMIT)


def _dot(a, b):
    return jnp.dot(a, b, preferred_element_type=F32)


def _dot_nt(a, b):
    return lax.dot_general(a, b, (((1,), (1,)), ((), ())), preferred_element_type=F32)


def _split3(x):
    hi = x.astype(BF16)
    rest = x - hi.astype(F32)
    mid = rest.astype(BF16)
    return hi, mid, (rest - mid.astype(F32)).astype(BF16)


def _dot_f32(a, b, exact):
    if exact == "a":
        a16 = a.astype(BF16)
        return sum(_dot(a16, part) for part in _split3(b))
    b16 = b.astype(BF16)
    return sum(_dot(part, b16) for part in _split3(a))


def _sigmoid(x):
    return 1.0 / (1.0 + jnp.exp(-x))


def _log_sigmoid(z):
    return jnp.minimum(z, 0.0) - jnp.log(1.0 + jnp.exp(-jnp.abs(z)))


_GELU_K = math.sqrt(2.0 / math.pi)
_GELU_C = 0.044715


def _gelu(x):
    t = jnp.tanh(_GELU_K * (x + _GELU_C * (x * x * x)))
    return 0.5 * x * (1.0 + t)


def _gelu_grad(x):
    x2 = x * x
    t = jnp.tanh(_GELU_K * (x + _GELU_C * (x2 * x)))
    return 0.5 * (1.0 + t) + 0.5 * x * (1.0 - t * t) * (_GELU_K * (1.0 + 3.0 * _GELU_C * x2))


def _rms_bwd(xh, r, g, dy):
    gy = dy * g
    return r * (gy - xh * jnp.mean(xh * gy, axis=-1, keepdims=True))


def _lane_lt64(shape):
    return lax.broadcasted_iota(jnp.int32, shape, len(shape) - 1) < HEAD_DIM


class _Comm:
    def __init__(self, arrays, out_shapes, sems, start, finish, mid=None):
        self.arrays, self.out_shapes, self.sems = list(arrays), list(out_shapes), list(sems)
        self.start, self.mid, self.finish = start, mid, finish


def _comm_phase(plans, phase, in_refs, out_refs, sem_refs):
    ia = io = ks = 0
    for plan in plans:
        na, no, ns = len(plan.arrays), len(plan.out_shapes), len(plan.sems)
        fn = getattr(plan, phase)
        if fn is not None:
            fn(in_refs[ia:ia + na], out_refs[io:io + no], sem_refs[ks:ks + ns])
        ia, io, ks = ia + na, io + no, ks + ns


def _comm_operands(plans):
    arrays = [a for plan in plans for a in plan.arrays]
    out_shapes = [o for plan in plans for o in plan.out_shapes]
    sems = [s for plan in plans for s in plan.sems]
    return arrays, out_shapes, sems


_ANY = pl.BlockSpec(memory_space=pl.ANY)


def _run_comm(name, plans):
    arrays, out_shapes, sems = _comm_operands(plans)
    n_in, n_out = len(arrays), len(out_shapes)

    def body(*refs):
        parts = refs[:n_in], refs[n_in:n_in + n_out], refs[n_in + n_out:]
        for phase in ("start", "mid", "finish"):
            _comm_phase(plans, phase, *parts)

    return pl.pallas_call(
        body, name=name, out_shape=out_shapes, in_specs=[_ANY] * n_in, out_specs=[_ANY] * n_out, scratch_shapes=sems,
    )(*arrays)


def _gather_plan(shard):
    def setup(ins, outs, sems):
        (x_ref,), (out_ref,), (send_sems, recv_sems, local_sem) = ins, outs, sems
        x, y, c = lax.axis_index("x"), lax.axis_index("y"), lax.axis_index("c")
        me, sibling = (x, y, c), (x, y, 1 - c)
        chips = [(1 - x, y), (x, 1 - y), (1 - x, 1 - y)]

        def rows(px, py, pc):
            return out_ref.at[4 * px + 2 * py + pc]

        def copy(k, block, to, src=None):
            return pltpu.make_async_remote_copy(
                src_ref=rows(*block) if src is None else src,
                dst_ref=rows(*block),
                send_sem=send_sems.at[k],
                recv_sem=recv_sems.at[k],
                device_id=to,
                device_id_type=MESH,
            )

        mine = pltpu.make_async_copy(x_ref, rows(*me), local_sem)
        first = [copy(0, me, sibling, src=x_ref)]
        first += [copy(1 + j, me, (*chip, c), src=x_ref) for j, chip in enumerate(chips)]
        passed = [copy(4 + j, (*chip, c), sibling) for j, chip in enumerate(chips)]
        landed = [copy(1 + j, (*chip, c), me) for j, chip in enumerate(chips)]
        from_sibling = [copy(0, sibling, me)] + [copy(4 + j, (*chip, 1 - c), me) for j, chip in enumerate(chips)]
        return mine, first, passed, landed, from_sibling

    def start(ins, outs, sems):
        mine, first, _, _, _ = setup(ins, outs, sems)
        mine.start()
        for cp in first:
            cp.start()

    def mid(ins, outs, sems):
        _, _, passed, landed, _ = setup(ins, outs, sems)
        for arrived, onward in zip(landed, passed):
            arrived.wait_recv()
            onward.start()

    def finish(ins, outs, sems):
        mine, first, passed, _, from_sibling = setup(ins, outs, sems)
        for cp in from_sibling:
            cp.wait_recv()
        for cp in first + passed:
            cp.wait_send()
        mine.wait()

    return _Comm([shard], [jax.ShapeDtypeStruct((N_DEV,) + shard.shape, shard.dtype)],
                 [pltpu.SemaphoreType.DMA((7,)), pltpu.SemaphoreType.DMA((7,)), pltpu.SemaphoreType.DMA],
                 start, finish, mid)


def _start_all(copies):
    for cp in copies:
        cp.start()


def _wait_all(copies):
    for cp in copies:
        cp.wait_recv()
    for cp in copies:
        cp.wait_send()


def _pair_exchange_plan(grads):
    n = len(grads)

    def copies(ins, outs, sems):
        send_sems, recv_sems = sems
        x, y, c = lax.axis_index("x"), lax.axis_index("y"), lax.axis_index("c")
        return [
            pltpu.make_async_remote_copy(
                src_ref=ins[k].at[:, 1 - c],
                dst_ref=outs[k],
                send_sem=send_sems.at[k],
                recv_sem=recv_sems.at[k],
                device_id=(x, y, 1 - c),
                device_id_type=MESH,
            )
            for k in range(n)
        ]

    return _Comm(grads, [jax.ShapeDtypeStruct((4,) + g.shape[2:], g.dtype) for g in grads],
                 [pltpu.SemaphoreType.DMA((n,)), pltpu.SemaphoreType.DMA((n,))],
                 lambda *refs: _start_all(copies(*refs)), lambda *refs: _wait_all(copies(*refs)))


def _chip_exchange_plan(parts):
    n = len(parts)

    def copies(ins, outs, sems):
        send_sems, recv_sems = sems
        x, y, c = lax.axis_index("x"), lax.axis_index("y"), lax.axis_index("c")
        chips = [(1 - x, y), (x, 1 - y), (1 - x, 1 - y)]
        return [
            pltpu.make_async_remote_copy(
                src_ref=ins[k].at[2 * px + py],
                dst_ref=outs[k].at[j],
                send_sem=send_sems.at[3 * k + j],
                recv_sem=recv_sems.at[3 * k + j],
                device_id=(px, py, c),
                device_id_type=MESH,
            )
            for k in range(n) for j, (px, py) in enumerate(chips)
        ]

    return _Comm(parts, [jax.ShapeDtypeStruct((3,) + p.shape[1:], p.dtype) for p in parts],
                 [pltpu.SemaphoreType.DMA((3 * n,)), pltpu.SemaphoreType.DMA((3 * n,))],
                 lambda *refs: _start_all(copies(*refs)), lambda *refs: _wait_all(copies(*refs)))


def _mm(name, pairs, extras, outs, epi, *, m, tm, n, tn, arbitrary=False, comm=()):
    nj = n // tn
    a_arrays, a_specs, b_arrays, b_specs, b_index = [], [], [], [], []
    for a, b, nt, cols in pairs:
        a_arrays.append(a)
        a_specs.append(pl.BlockSpec((tm, a.shape[1]), lambda i, j: (i, 0)))
        known = [k for k, other in enumerate(b_arrays) if other is b]
        if known:
            b_index.append(known[0])
            continue
        b_index.append(len(b_arrays))
        b_arrays.append(b)
        if cols is not None:
            assert nj == 1
            b_specs.append(pl.BlockSpec(b.shape, lambda i, j: (0, 0)))
        elif nt:
            b_specs.append(pl.BlockSpec((tn, b.shape[1]), lambda i, j: (j, 0)))
        else:
            b_specs.append(pl.BlockSpec((b.shape[0], tn), lambda i, j: (0, j)))
    comm_arrays, comm_outs, comm_sems = _comm_operands(comm)
    arrays = a_arrays + b_arrays + [arr for arr, _ in extras] + comm_arrays
    in_specs = a_specs + b_specs + [spec for _, spec in extras] + [_ANY] * len(comm_arrays)
    n_a, n_b, n_extras, n_ci, n_out, n_co = len(a_arrays), len(b_arrays), len(extras), len(comm_arrays), len(outs), len(comm_outs)
    ni = m // tm

    def body(*refs):
        a_refs = refs[:n_a]
        b_refs = refs[n_a:n_a + n_b]
        ex = refs[n_a + n_b:n_a + n_b + n_extras]
        n_in = n_a + n_b + n_extras + n_ci
        comm_refs = refs[n_in - n_ci:n_in], refs[n_in + n_out:n_in + n_out + n_co], refs[n_in + n_out + n_co:]
        out = refs[n_in:n_in + n_out]
        if comm:
            @pl.when(jnp.logical_and(pl.program_id(0) == 0, pl.program_id(1) == 0))
            def _():
                _comm_phase(comm, "start", *comm_refs)

        accs = []
        for p, (_, _, nt, cols) in enumerate(pairs):
            av = a_refs[p][...]
            if av.dtype != BF16:
                av = av.astype(BF16)
            b_ref = b_refs[b_index[p]]
            if cols is None:
                bv = b_ref[...]
            else:
                bv = b_ref[:, cols[0]:cols[1]] if nt else b_ref[cols[0]:cols[1], :]
            accs.append(_dot_nt(av, bv) if nt else _dot(av, bv))
        epi(accs, ex, out)
        if comm:
            mid_row = ni // 2 if ni >= 3 else ni - 1
            mid_col = 0 if ni >= 3 else nj - 1

            @pl.when(jnp.logical_and(pl.program_id(0) == mid_row, pl.program_id(1) == mid_col))
            def _():
                _comm_phase(comm, "mid", *comm_refs)

            @pl.when(jnp.logical_and(pl.program_id(0) == ni - 1, pl.program_id(1) == nj - 1))
            def _():
                _comm_phase(comm, "finish", *comm_refs)

    sem = ("arbitrary", "arbitrary") if arbitrary or comm else ("parallel", "parallel")
    res = pl.pallas_call(
        body,
        name=name,
        grid=(ni, nj),
        in_specs=in_specs,
        out_specs=[spec for _, spec in outs] + [_ANY] * n_co,
        out_shape=[shape for shape, _ in outs] + comm_outs,
        scratch_shapes=comm_sems,
        compiler_params=_params(sem),
    )(*arrays)
    return (res[:n_out], res[n_out:]) if comm else res


def _tile(tm, tn, off=0):
    return pl.BlockSpec((tm, tn), lambda i, j: (i, j + off))


def _row(tm, w, blk=0):
    return pl.BlockSpec((tm, w), lambda i, j: (i, blk))


def _whole(shape):
    zeros = (0,) * len(shape)
    return pl.BlockSpec(shape, lambda i, j: zeros)


def _sds(shape, dtype):
    return jax.ShapeDtypeStruct(shape, dtype)


def _tile_t(tm, tn):
    return pl.BlockSpec((tn, tm), lambda i, j: (j, i))


def _grad_w(name, a_t, g, *, tk, tn, ts, block_cols=None):
    ka, s_len = a_t.shape
    n = g.shape[1]
    width = tn if block_cols is None else block_cols

    def body(a_ref, g_ref, o_ref):
        first = pl.program_id(2) == 0
        gv = g_ref[...].astype(BF16)
        for b in range(tn // width):
            part = _dot(a_ref[...], gv[:, b * width:(b + 1) * width])
            dst = o_ref if block_cols is None else o_ref.at[b]

            @pl.when(first)
            def _():
                dst[...] = part

            @pl.when(jnp.logical_not(first))
            def _():
                dst[...] += part

    if block_cols is None:
        out_shape = _sds((ka, n), F32)
        out_spec = pl.BlockSpec((tk, tn), lambda i, j, s: (i, j))
    else:
        out_shape = _sds((n // width, ka, width), F32)
        out_spec = pl.BlockSpec((tn // width, tk, width), lambda i, j, s: (j, i, 0))
    return pl.pallas_call(
        body,
        name=name,
        grid=(ka // tk, n // tn, s_len // ts),
        in_specs=[pl.BlockSpec((tk, ts), lambda i, j, s: (i, s)), pl.BlockSpec((ts, tn), lambda i, j, s: (s, j))],
        out_specs=out_spec,
        out_shape=out_shape,
        compiler_params=_params(("parallel", "parallel", "arbitrary")),
    )(a_t, g)


def _rms_fwd(name, x, g, tm, comm=()):
    s_len, d = x.shape
    steps = s_len // tm
    comm_arrays, comm_outs, comm_sems = _comm_operands(comm)
    n_ci, n_co = len(comm_arrays), len(comm_outs)

    def body(x_ref, g_ref, *rest):
        comm_refs = rest[:n_ci], rest[n_ci + 3:n_ci + 3 + n_co], rest[n_ci + 3 + n_co:]
        h_ref, ht_ref, r_ref = rest[n_ci:n_ci + 3]
        for phase, at in (("start", 0), ("mid", steps // 2)):
            if comm:
                @pl.when(pl.program_id(0) == at)
                def _():
                    _comm_phase(comm, phase, *comm_refs)

        xv = x_ref[...]
        r = lax.rsqrt(jnp.mean(xv * xv, axis=-1, keepdims=True) + EPS)
        h = (xv * r * g_ref[...]).astype(BF16)
        h_ref[...] = h
        ht_ref[...] = jnp.transpose(h)
        r_ref[...] = r
        if comm:
            @pl.when(pl.program_id(0) == steps - 1)
            def _():
                _comm_phase(comm, "finish", *comm_refs)

    res = pl.pallas_call(
        body,
        name=name,
        grid=(steps,),
        in_specs=[pl.BlockSpec((tm, d), lambda i: (i, 0)), pl.BlockSpec((1, d), lambda i: (0, 0))] + [_ANY] * n_ci,
        out_specs=[pl.BlockSpec((tm, d), lambda i: (i, 0)), pl.BlockSpec((d, tm), lambda i: (0, i)),
                   pl.BlockSpec((tm, 1), lambda i: (i, 0))] + [_ANY] * n_co,
        out_shape=[_sds((s_len, d), BF16), _sds((d, s_len), BF16), _sds((s_len, 1), F32)] + comm_outs,
        scratch_shapes=comm_sems,
        compiler_params=_params(("arbitrary",) if comm else ("parallel",)),
    )(x, g, *comm_arrays)
    return res[0], res[1], res[2], res[3:]


def _forget_cumsum(prest, b_f_pad, tc):
    s_len = prest.shape[0]

    def body(f_ref, b_ref, c_ref, carry):
        @pl.when(pl.program_id(0) == 0)
        def _():
            carry[...] = jnp.zeros_like(carry)

        logf = _log_sigmoid(f_ref[...] + b_ref[...])
        row = lax.broadcasted_iota(jnp.int32, (tc, tc), 0)
        col = lax.broadcasted_iota(jnp.int32, (tc, tc), 1)
        tri = (row >= col).astype(F32)
        c = _dot_f32(tri, logf, "a") + carry[...]
        c_ref[...] = c
        carry[...] = c[tc - 1:tc, :]

    return pl.pallas_call(
        body,
        name="forget_cumsum",
        grid=(s_len // tc,),
        in_specs=[pl.BlockSpec((tc, LANES), lambda i: (i, 0)), pl.BlockSpec((1, LANES), lambda i: (0, 0))],
        out_specs=pl.BlockSpec((tc, LANES), lambda i: (i, 0)),
        out_shape=_sds((s_len, LANES), F32),
        scratch_shapes=[pltpu.VMEM((1, LANES), F32)],
        compiler_params=_params(("arbitrary",)),
    )(prest, b_f_pad)


def _stack_heads(pair, lt64):
    zero = jnp.zeros_like(pair)
    return jnp.concatenate([jnp.where(lt64, pair, zero), jnp.where(lt64, zero, pair)], axis=0)


def _score_tiles(q_ref, k_ref, ck_ref, st_sc, tk):
    lt64 = _lane_lt64((tk, LANES))
    for p in range(N_HEADS // 2):
        lanes = slice(p * LANES, (p + 1) * LANES)
        q_pair = q_ref[:, lanes] * jnp.asarray(HEAD_DIM ** -0.5, BF16)
        st2 = _dot_nt(_stack_heads(k_ref[:, lanes], lt64), q_pair)
        for half in range(2):
            h = 2 * p + half
            st_sc[h] = st2[half * tk:(half + 1) * tk] - ck_ref[:, h:h + 1]


ROW_CHUNK = 64


def _row_chunks(tk):
    rc = min(ROW_CHUNK, tk)
    return [slice(r, r + rc) for r in range(0, tk, rc)]


def _by_sublane(x):
    return x.reshape(x.shape[0] // 8, 8, x.shape[1])


def _softmax_update(st_sc, p_sc, m_sc, l_sc, tk, tq):
    alphas = []
    for h in range(N_HEADS):
        top8 = jnp.full((8, tq), NEG, F32)
        for rows in _row_chunks(tk):
            top8 = jnp.maximum(top8, jnp.max(_by_sublane(st_sc[h, rows, :]), axis=0))
        m_old = m_sc[h]
        m_new = jnp.maximum(m_old, jnp.max(top8, axis=0, keepdims=True))
        sum8 = jnp.zeros((8, tq), F32)
        for rows in _row_chunks(tk):
            pt = jnp.exp(st_sc[h, rows, :] - m_new)
            p_sc[h, rows, :] = pt.astype(BF16)
            sum8 = sum8 + jnp.sum(_by_sublane(pt), axis=0)
        alpha = jnp.exp(m_old - m_new)
        l_sc[h] = alpha * l_sc[h] + jnp.sum(sum8, axis=0, keepdims=True)
        m_sc[h] = m_new
        alphas.append(alpha)
    return alphas


def _mask_diagonal(st_sc, i, j, tq, tk):
    @pl.when((j + 1) * tk - 1 > i * tq)
    def _():
        key = j * tk + lax.broadcasted_iota(jnp.int32, (tk, tq), 0)
        query = i * tq + lax.broadcasted_iota(jnp.int32, (tk, tq), 1)
        st_sc[...] = jnp.where((query >= key)[None], st_sc[...], NEG)


def _attn_fwd(qkv, v_t, c_col, tq, tk, comm=()):
    s_len = qkv.shape[0]
    ratio = tq // tk
    steps = [(i, j) for i in range(s_len // tq) for j in range((i + 1) * ratio)]
    i_tab = jnp.asarray([i for i, _ in steps], jnp.int32)
    j_tab = jnp.asarray([j for _, j in steps], jnp.int32)

    comm_arrays, comm_outs, comm_sems = _comm_operands(comm)
    n_ci, n_co = len(comm_arrays), len(comm_outs)

    def body(i_ref, j_ref, q_ref, k_ref, vt_ref, ck_ref, *rest):
        comm_refs = rest[:n_ci], rest[n_ci + 3:n_ci + 3 + n_co], rest[n_ci + 3 + n_co + 5:]
        o_ref, ot_ref, lse_ref = rest[n_ci:n_ci + 3]
        acc_t, m_sc, l_sc, st_sc, p_sc = rest[n_ci + 3 + n_co:n_ci + 3 + n_co + 5]
        n = pl.program_id(0)
        i, j = i_ref[n], j_ref[n]
        for phase, at in (("start", 0), ("mid", (2 * len(steps)) // 3)):
            if comm:
                @pl.when(n == at)
                def _():
                    _comm_phase(comm, phase, *comm_refs)

        @pl.when(j == 0)
        def _():
            acc_t[...] = jnp.zeros_like(acc_t)
            m_sc[...] = jnp.full_like(m_sc, NEG)
            l_sc[...] = jnp.zeros_like(l_sc)

        _score_tiles(q_ref, k_ref, ck_ref, st_sc, tk)
        _mask_diagonal(st_sc, i, j, tq, tk)
        alpha = _softmax_update(st_sc, p_sc, m_sc, l_sc, tk, tq)
        top = lax.broadcasted_iota(jnp.int32, (LANES, tq), 0) < HEAD_DIM
        for p in range(N_HEADS // 2):
            lanes = slice(p * LANES, (p + 1) * LANES)
            vt_pair = vt_ref[lanes, :]
            pv = jnp.where(top, _dot(vt_pair, p_sc[2 * p]), _dot(vt_pair, p_sc[2 * p + 1]))
            acc_t[lanes, :] = acc_t[lanes, :] * jnp.where(top, alpha[2 * p], alpha[2 * p + 1]) + pv

        @pl.when(j == (i + 1) * ratio - 1)
        def _():
            for p in range(N_HEADS // 2):
                lanes = slice(p * LANES, (p + 1) * LANES)
                l_pair = jnp.where(top, l_sc[2 * p], l_sc[2 * p + 1])
                o_t = acc_t[lanes, :] / l_pair
                o_ref[:, lanes] = jnp.transpose(o_t)
                ot_ref[lanes, :] = o_t.astype(BF16)
            lse_ref[...] = m_sc[...] + jnp.log(l_sc[...])

        if comm:
            @pl.when(n == len(steps) - 1)
            def _():
                _comm_phase(comm, "finish", *comm_refs)

    stat = pltpu.VMEM((N_HEADS, 1, tq), F32)
    res = pl.pallas_call(
        body,
        name="attn_fwd",
        grid_spec=pltpu.PrefetchScalarGridSpec(
            num_scalar_prefetch=2,
            grid=(len(steps),),
            in_specs=[
                pl.BlockSpec((tq, FOX_W), lambda n, it, jt: (it[n], 0)),
                pl.BlockSpec((tk, FOX_W), lambda n, it, jt: (jt[n], 1)),
                pl.BlockSpec((FOX_W, tk), lambda n, it, jt: (0, jt[n])),
                pl.BlockSpec((tk, LANES), lambda n, it, jt: (jt[n], 0)),
            ] + [_ANY] * n_ci,
            out_specs=[
                pl.BlockSpec((tq, FOX_W), lambda n, it, jt: (it[n], 0)),
                pl.BlockSpec((FOX_W, tq), lambda n, it, jt: (0, it[n])),
                pl.BlockSpec((N_HEADS, 1, tq), lambda n, it, jt: (0, 0, it[n])),
            ] + [_ANY] * n_co,
            scratch_shapes=[pltpu.VMEM((FOX_W, tq), F32), stat, stat, pltpu.VMEM((N_HEADS, tk, tq), F32),
                            pltpu.VMEM((N_HEADS, tk, tq), BF16)] + comm_sems,
        ),
        out_shape=[_sds((s_len, FOX_W), F32), _sds((FOX_W, s_len), BF16), _sds((N_HEADS, 1, s_len), F32)] + comm_outs,
        compiler_params=_params(("arbitrary",)),
    )(i_tab, j_tab, qkv, qkv, v_t, c_col, *comm_arrays)
    return res[0], res[1], res[2], res[3:]


def _sgu_mix(vn, w_stack, lt64):
    outs = []
    for p in range(SGU_G // 2):
        r = _dot(w_stack[p], vn[:, p * LANES:(p + 1) * LANES])
        outs.append(jnp.where(lt64, r[:SGU_LEN], r[SGU_LEN:]))
    return jnp.concatenate(outs, axis=1)


def _sgu_norm(sv, ln_g, ln_b):
    svg = _gelu(sv)
    xc = svg - jnp.mean(svg, axis=-1, keepdims=True)
    rstd = lax.rsqrt(jnp.mean(xc * xc, axis=-1, keepdims=True) + EPS)
    xhat = xc * rstd
    return xhat, rstd, xhat * ln_g + ln_b


def _sgu_fwd(prest, ln_g, ln_b, w_stack, b_pair, tm):
    s_len = prest.shape[0]

    def body(u_ref, sv_ref, g_ref, b_ref, w_ref, bp_ref, sg_ref, sgt_ref):
        lt64 = _lane_lt64((SGU_LEN, LANES))
        _, _, vn = _sgu_norm(sv_ref[...].astype(F32), g_ref[...], b_ref[...])
        vn = vn.astype(BF16)
        w_stack_v = [w_ref[p] for p in range(SGU_G // 2)]
        for w in range(tm // SGU_LEN):
            win = slice(w * SGU_LEN, (w + 1) * SGU_LEN)
            mixed = _sgu_mix(vn[win], w_stack_v, lt64) + bp_ref[...]
            sg = (_gelu(u_ref[win, :].astype(F32)) * mixed).astype(BF16)
            sg_ref[win, :] = sg
            sgt_ref[:, win] = jnp.transpose(sg)

    return pl.pallas_call(
        body,
        name="sgu_fwd",
        grid=(s_len // tm,),
        in_specs=[
            pl.BlockSpec((tm, SGU_W), lambda i: (i, U_OFF // SGU_W)),
            pl.BlockSpec((tm, SGU_W), lambda i: (i, SV_OFF // SGU_W)),
            pl.BlockSpec((1, SGU_W), lambda i: (0, 0)),
            pl.BlockSpec((1, SGU_W), lambda i: (0, 0)),
            pl.BlockSpec((SGU_G // 2, 2 * SGU_LEN, SGU_LEN), lambda i: (0, 0, 0)),
            pl.BlockSpec((SGU_LEN, SGU_W), lambda i: (0, 0)),
        ],
        out_specs=[pl.BlockSpec((tm, SGU_W), lambda i: (i, 0)), pl.BlockSpec((SGU_W, tm), lambda i: (0, i))],
        out_shape=[_sds((s_len, SGU_W), BF16), _sds((SGU_W, s_len), BF16)],
        compiler_params=_params(("parallel",)),
    )(prest, prest, ln_g, ln_b, w_stack, b_pair)


def _sgu_bwd(prest, dsg, ln_g, ln_b, w_stack, wt_stack, b_pair, tm):
    s_len = prest.shape[0]
    n_pair = SGU_G // 2

    def body(u_ref, sv_ref, dsg_ref, g_ref, b_ref, w_ref, wt_ref, bp_ref,
             du_ref, dsv_ref, dw_ref, db_ref, dg_ref, dbeta_ref, dusvt_ref, dvn_sc):
        @pl.when(pl.program_id(0) == 0)
        def _():
            dw_ref[...] = jnp.zeros_like(dw_ref)
            db_ref[...] = jnp.zeros_like(db_ref)
            dg_ref[...] = jnp.zeros_like(dg_ref)
            dbeta_ref[...] = jnp.zeros_like(dbeta_ref)

        lt64 = _lane_lt64((SGU_LEN, LANES))
        sv = sv_ref[...].astype(F32)
        xhat, rstd, vn32 = _sgu_norm(sv, g_ref[...], b_ref[...])
        vn = vn32.astype(BF16)
        w_stack_v = [w_ref[p] for p in range(n_pair)]
        db = jnp.zeros((SGU_LEN, SGU_W), F32)
        for w in range(tm // SGU_LEN):
            win = slice(w * SGU_LEN, (w + 1) * SGU_LEN)
            u = u_ref[win, :].astype(F32)
            dsg_w = dsg_ref[win, :]
            mixed = _sgu_mix(vn[win], w_stack_v, lt64) + bp_ref[...]
            du = (dsg_w * mixed * _gelu_grad(u)).astype(BF16)
            du_ref[win, :] = du
            dusvt_ref[:SGU_W, win] = jnp.transpose(du)
            dmixed = dsg_w * _gelu(u)
            db = db + dmixed
            dm16 = dmixed.astype(BF16)
            for p in range(n_pair):
                lanes = slice(p * LANES, (p + 1) * LANES)
                dmp = dm16[:, lanes]
                r = _dot(wt_ref[p], dmp)
                dvn_sc[win, lanes] = jnp.where(lt64, r[:SGU_LEN], r[SGU_LEN:])
                zero = jnp.zeros_like(dmp)
                dm_ab = jnp.concatenate([jnp.where(lt64, dmp, zero), jnp.where(lt64, zero, dmp)], axis=0)
                dw_ref[p] += _dot_nt(dm_ab, vn[win, lanes])
        db_ref[...] += db
        dvn = dvn_sc[...]
        dg_ref[...] += jnp.sum(dvn * xhat, axis=0, keepdims=True)
        dbeta_ref[...] += jnp.sum(dvn, axis=0, keepdims=True)
        dxh = dvn * g_ref[...]
        dsvg = rstd * (dxh - jnp.mean(dxh, axis=-1, keepdims=True) - xhat * jnp.mean(dxh * xhat, axis=-1, keepdims=True))
        dsv = (dsvg * _gelu_grad(sv)).astype(BF16)
        dsv_ref[...] = dsv
        dusvt_ref[SGU_W:, :] = jnp.transpose(dsv)

    const2 = lambda i: (0, 0)
    const3 = lambda i: (0, 0, 0)
    return pl.pallas_call(
        body,
        name="sgu_bwd",
        grid=(s_len // tm,),
        in_specs=[
            pl.BlockSpec((tm, SGU_W), lambda i: (i, U_OFF // SGU_W)),
            pl.BlockSpec((tm, SGU_W), lambda i: (i, SV_OFF // SGU_W)),
            pl.BlockSpec((tm, SGU_W), lambda i: (i, 0)),
            pl.BlockSpec((1, SGU_W), const2),
            pl.BlockSpec((1, SGU_W), const2),
            pl.BlockSpec((n_pair, 2 * SGU_LEN, SGU_LEN), const3),
            pl.BlockSpec((n_pair, 2 * SGU_LEN, SGU_LEN), const3),
            pl.BlockSpec((SGU_LEN, SGU_W), const2),
        ],
        out_specs=[
            pl.BlockSpec((tm, SGU_W), lambda i: (i, 0)),
            pl.BlockSpec((tm, SGU_W), lambda i: (i, 0)),
            pl.BlockSpec((n_pair, 2 * SGU_LEN, SGU_LEN), const3),
            pl.BlockSpec((SGU_LEN, SGU_W), const2),
            pl.BlockSpec((1, SGU_W), const2),
            pl.BlockSpec((1, SGU_W), const2),
            pl.BlockSpec((2 * SGU_W, tm), lambda i: (0, i)),
        ],
        out_shape=[
            _sds((s_len, SGU_W), BF16), _sds((s_len, SGU_W), BF16), _sds((n_pair, 2 * SGU_LEN, SGU_LEN), F32),
            _sds((SGU_LEN, SGU_W), F32), _sds((1, SGU_W), F32), _sds((1, SGU_W), F32),
            _sds((2 * SGU_W, s_len), BF16),
        ],
        scratch_shapes=[pltpu.VMEM((tm, SGU_W), F32)],
        compiler_params=_params(("arbitrary",)),
    )(prest, prest, dsg, ln_g, ln_b, w_stack, wt_stack, b_pair)


def _attn_bwd(qkv, k_t, do, c_col, lse_row, delta_row, tq, tk, comm=()):
    s_len = qkv.shape[0]
    nq, nk = s_len // tq, s_len // tk
    ratio = tq // tk
    scale = HEAD_DIM ** -0.5
    steps = [(j, i) for j in range(nk) for i in range(j // ratio, nq)]
    j_tab = jnp.asarray([j for j, _ in steps], jnp.int32)
    i_tab = jnp.asarray([i for _, i in steps], jnp.int32)

    comm_arrays, comm_outs, comm_sems = _comm_operands(comm)
    n_ci, n_co = len(comm_arrays), len(comm_outs)

    def body(j_ref, i_ref, q_ref, k_ref, v_ref, kt_ref, do_ref, ck_ref, lse_ref, dl_ref, *rest):
        comm_refs = rest[:n_ci], rest[n_ci + 6:n_ci + 6 + n_co], rest[n_ci + 6 + n_co + 8:]
        dq_ref, dk_ref, dv_ref, dcr_ref, dcc_ref, dkvt_ref = rest[n_ci:n_ci + 6]
        dq_t, dk_acc, dv_acc, dcc_acc, st_sc, dpt_sc, p_sc, ds_sc = rest[n_ci + 6 + n_co:n_ci + 6 + n_co + 8]
        n = pl.program_id(0)
        j, i = j_ref[n], i_ref[n]

        @pl.when(n == 0)
        def _():
            _comm_phase(comm, "start", *comm_refs)
            dq_t[...] = jnp.zeros_like(dq_t)
            dcr_ref[...] = jnp.zeros_like(dcr_ref)

        @pl.when(i == j // ratio)
        def _():
            dk_acc[...] = jnp.zeros_like(dk_acc)
            dv_acc[...] = jnp.zeros_like(dv_acc)
            dcc_acc[...] = jnp.zeros_like(dcc_acc)

        lt64 = _lane_lt64((tk, LANES))
        _score_tiles(q_ref, k_ref, ck_ref, st_sc, tk)
        for p in range(N_HEADS // 2):
            lanes = slice(p * LANES, (p + 1) * LANES)
            dpt2 = _dot_nt(_stack_heads(v_ref[:, lanes], lt64), do_ref[:, lanes].astype(BF16))
            dpt_sc[2 * p] = dpt2[:tk]
            dpt_sc[2 * p + 1] = dpt2[tk:]
        _mask_diagonal(st_sc, i, j, tq, tk)

        pt = jnp.exp(st_sc[...] - lse_ref[...])
        dst = pt * (dpt_sc[...] - dl_ref[...])
        p_sc[...] = pt.astype(BF16)
        ds_sc[...] = dst.astype(BF16)
        dcr_ref[i] += jnp.sum(dst, axis=1, keepdims=True)
        col_sums = jnp.sum(dst, axis=2, keepdims=True)
        lane = lax.broadcasted_iota(jnp.int32, (tk, LANES), 1)
        dcc = jnp.zeros((tk, LANES), F32)
        for h in range(N_HEADS):
            dcc = jnp.where(lane == h, -col_sums[h], dcc)
        dcc_acc[...] += dcc

        for p in range(N_HEADS // 2):
            lanes = slice(p * LANES, (p + 1) * LANES)
            q_pair = q_ref[:, lanes] * jnp.asarray(scale, BF16)
            dv2 = _dot(p_sc[2 * p:2 * p + 2].reshape(2 * tk, tq), do_ref[:, lanes].astype(BF16))
            dv_acc[:, lanes] += jnp.where(lt64, dv2[:tk], dv2[tk:])
            dk2 = _dot(ds_sc[2 * p:2 * p + 2].reshape(2 * tk, tq), q_pair)
            dk_acc[:, lanes] += jnp.where(lt64, dk2[:tk], dk2[tk:])
            dq2 = _dot(kt_ref[lanes, :], jnp.concatenate([ds_sc[2 * p], ds_sc[2 * p + 1]], axis=1))
            top = lax.broadcasted_iota(jnp.int32, (LANES, tq), 0) < HEAD_DIM
            dq_t[i, lanes, :] += jnp.where(top, dq2[:, :tq], dq2[:, tq:])

        @pl.when(j == (i + 1) * ratio - 1)
        def _():
            rows = pl.ds(pl.multiple_of(i * tq, tq), tq)
            for p in range(N_HEADS // 2):
                lanes = slice(p * LANES, (p + 1) * LANES)
                dq_ref[rows, lanes] = (jnp.transpose(dq_t[i, lanes, :]) * scale).astype(BF16)

        @pl.when(i == nq - 1)
        def _():
            dk16, dv16 = dk_acc[...].astype(BF16), dv_acc[...].astype(BF16)
            dk_ref[...] = dk16
            dv_ref[...] = dv16
            dkvt_ref[:FOX_W, :] = jnp.transpose(dk16)
            dkvt_ref[FOX_W:, :] = jnp.transpose(dv16)
            dcc_ref[...] = dcc_acc[...]

        if comm:
            @pl.when(n == len(steps) // 2)
            def _():
                _comm_phase(comm, "mid", *comm_refs)

            @pl.when(n == len(steps) - 1)
            def _():
                _comm_phase(comm, "finish", *comm_refs)

    q_map = lambda n, jt, it: (it[n], 0)
    q_stat = lambda n, jt, it: (0, 0, it[n])
    k_map = lambda n, jt, it: (jt[n], 0)
    tile = (N_HEADS, tk, tq)
    res = pl.pallas_call(
        body,
        name="attn_bwd",
        grid_spec=pltpu.PrefetchScalarGridSpec(
            num_scalar_prefetch=2,
            grid=(len(steps),),
            in_specs=[
                pl.BlockSpec((tq, FOX_W), q_map),
                pl.BlockSpec((tk, FOX_W), lambda n, jt, it: (jt[n], 1)),
                pl.BlockSpec((tk, FOX_W), lambda n, jt, it: (jt[n], 2)),
                pl.BlockSpec((FOX_W, tk), lambda n, jt, it: (0, jt[n])),
                pl.BlockSpec((tq, FOX_W), q_map),
                pl.BlockSpec((tk, LANES), k_map),
                pl.BlockSpec((N_HEADS, 1, tq), q_stat),
                pl.BlockSpec((N_HEADS, 1, tq), q_stat),
            ] + [_ANY] * n_ci,
            out_specs=[
                pl.BlockSpec((s_len, FOX_W), lambda n, jt, it: (0, 0)),
                pl.BlockSpec((tk, FOX_W), k_map),
                pl.BlockSpec((tk, FOX_W), k_map),
                pl.BlockSpec((nq, N_HEADS, 1, tq), lambda n, jt, it: (0, 0, 0, 0)),
                pl.BlockSpec((tk, LANES), k_map),
                pl.BlockSpec((2 * FOX_W, tk), lambda n, jt, it: (0, jt[n])),
            ] + [_ANY] * n_co,
            scratch_shapes=[pltpu.VMEM((nq, FOX_W, tq), F32), pltpu.VMEM((tk, FOX_W), F32), pltpu.VMEM((tk, FOX_W), F32),
                            pltpu.VMEM((tk, LANES), F32), pltpu.VMEM(tile, F32), pltpu.VMEM(tile, F32),
                            pltpu.VMEM(tile, BF16), pltpu.VMEM(tile, BF16)] + comm_sems,
        ),
        out_shape=[_sds((s_len, FOX_W), BF16), _sds((s_len, FOX_W), BF16), _sds((s_len, FOX_W), BF16),
                   _sds((nq, N_HEADS, 1, tq), F32), _sds((s_len, LANES), F32),
                   _sds((2 * FOX_W, s_len), BF16)] + comm_outs,
        compiler_params=_params(("arbitrary",)),
    )(j_tab, i_tab, qkv, qkv, qkv, k_t, do, c_col, lse_row, delta_row, *comm_arrays)
    return res[:6], res[6:]


def _forget_bwd(dc_rows, dc_cols, prest, b_f_pad, tc):
    s_len = dc_rows.shape[0]
    nb = s_len // tc

    def body(dcr_ref, dc_ref, f_ref, b_ref, df_ref, db_ref, dft_ref, carry):
        @pl.when(pl.program_id(0) == 0)
        def _():
            carry[...] = jnp.zeros_like(carry)
            db_ref[...] = jnp.zeros_like(db_ref)

        row = lax.broadcasted_iota(jnp.int32, (tc, tc), 0)
        col = lax.broadcasted_iota(jnp.int32, (tc, tc), 1)
        tri = (row <= col).astype(F32)
        dlogf = _dot_f32(tri, dcr_ref[...] + dc_ref[...], "a") + carry[...]
        carry[...] = dlogf[0:1, :]
        z = f_ref[...] + b_ref[...]
        lane = lax.broadcasted_iota(jnp.int32, (tc, LANES), 1)
        dz = jnp.where(lane < N_HEADS, dlogf * _sigmoid(-z), 0.0)
        df_ref[...] = dz.astype(BF16)
        dft_ref[...] = jnp.transpose(dz).astype(BF16)
        db_ref[...] += jnp.sum(dz, axis=0, keepdims=True)

    rev = lambda i: (nb - 1 - i, 0)
    return pl.pallas_call(
        body,
        name="forget_bwd",
        grid=(nb,),
        in_specs=[
            pl.BlockSpec((tc, LANES), rev),
            pl.BlockSpec((tc, LANES), rev),
            pl.BlockSpec((tc, LANES), rev),
            pl.BlockSpec((1, LANES), lambda i: (0, 0)),
        ],
        out_specs=[pl.BlockSpec((tc, LANES), rev), pl.BlockSpec((1, LANES), lambda i: (0, 0)),
                   pl.BlockSpec((LANES, tc), lambda i: (0, nb - 1 - i))],
        out_shape=[_sds((s_len, LANES), BF16), _sds((1, LANES), F32), _sds((LANES, s_len), BF16)],
        scratch_shapes=[pltpu.VMEM((1, LANES), F32)],
        compiler_params=_params(("arbitrary",)),
    )(dc_rows, dc_cols, prest, b_f_pad)


def _pair_sum(name, g4, recv, idx, tr):
    _, _, r, c = g4.shape

    def body(idx_ref, g_ref, r_ref, p16_ref, own_ref):
        k = pl.program_id(1)
        s = g_ref[...] + r_ref[...]
        p16_ref[...] = s.astype(BF16)

        @pl.when(k == idx_ref[1])
        def _():
            own_ref[...] = s

    return pl.pallas_call(
        body,
        name=name,
        grid_spec=pltpu.PrefetchScalarGridSpec(
            num_scalar_prefetch=1,
            grid=(r // tr, 4),
            in_specs=[
                pl.BlockSpec((None, None, tr, c), lambda i, k, idx: (k, idx[0], i, 0)),
                pl.BlockSpec((None, tr, c), lambda i, k, idx: (k, i, 0)),
            ],
            out_specs=[
                pl.BlockSpec((None, tr, c), lambda i, k, idx: (k, i, 0)),
                pl.BlockSpec((tr, c), lambda i, k, idx: (i, 0)),
            ],
        ),
        out_shape=[_sds((4, r, c), BF16), _sds((r, c), F32)],
        compiler_params=_params(("parallel", "arbitrary")),
    )(idx, g4, recv)


def _adamw_math(w, g, m, v):
    m2 = ADAM_B1 * m + (1.0 - ADAM_B1) * g
    v2 = ADAM_B2 * v + (1.0 - ADAM_B2) * (g * g)
    m_hat = m2 / (1.0 - ADAM_B1 ** ADAM_STEP)
    v_hat = v2 / (1.0 - ADAM_B2 ** ADAM_STEP)
    delta = -ADAM_LR * (m_hat / (jnp.sqrt(v_hat) + ADAM_EPS) + ADAM_WD * w)
    return delta, m2, v2


def _adamw_shard(name, own, recv, w, m, v, tr):
    r, c = own.shape

    def body(own_ref, recv_ref, w_ref, m_ref, v_ref, g_ref, d_ref, m2_ref, v2_ref):
        g = own_ref[...]
        for k in range(3):
            g = g + recv_ref[k].astype(F32)
        delta, m2, v2 = _adamw_math(w_ref[...], g, m_ref[...], v_ref[...])
        g_ref[...] = g
        d_ref[...] = delta
        m2_ref[...] = m2
        v2_ref[...] = v2

    spec = pl.BlockSpec((tr, c), lambda i: (i, 0))
    return pl.pallas_call(
        body,
        name=name,
        grid=(r // tr,),
        in_specs=[spec, pl.BlockSpec((3, tr, c), lambda i: (0, i, 0)), spec, spec, spec],
        out_specs=[spec] * 4,
        out_shape=[_sds((r, c), F32)] * 4,
        compiler_params=_params(("parallel",)),
    )(own, recv, w, m, v)


def _adamw_small(name, gathered, first_row, w, m, v):
    r = w.shape[0]
    assert first_row % r == 0

    def body(ga_ref, w_ref, m_ref, v_ref, g_ref, d_ref, m2_ref, v2_ref):
        g = ga_ref[0]
        for k in range(1, N_DEV):
            g = g + ga_ref[k]
        delta, m2, v2 = _adamw_math(w_ref[...], g, m_ref[...], v_ref[...])
        g_ref[...] = g
        d_ref[...] = delta
        m2_ref[...] = m2
        v2_ref[...] = v2

    spec = pl.BlockSpec((r, LANES), lambda i: (0, 0))
    return pl.pallas_call(
        body,
        name=name,
        grid=(1,),
        in_specs=[pl.BlockSpec((N_DEV, r, LANES), lambda i: (0, first_row // r, 0)), spec, spec, spec],
        out_specs=[spec] * 4,
        out_shape=[_sds((r, LANES), F32)] * 4,
        compiler_params=_params(("arbitrary",)),
    )(gathered, w, m, v)


_TINY_EARLY = (("b_sgu", (1, SGU_G, SGU_LEN)), ("norm2_g", (1, D_MODEL)), ("normf_g", (D_MODEL,)),
               ("ln_v_g", (1, SGU_W)), ("ln_v_b", (1, SGU_W)))
_TINY_LATE = (("b_f", (1, N_HEADS)), ("norm1_g", (1, D_MODEL)), ("loss", ()))


def _pack_rows(values):
    rows = []
    for val in values:
        flat = val.reshape(-1).astype(F32)
        pad = (-flat.shape[0]) % LANES
        rows.append(jnp.pad(flat, (0, pad)).reshape(-1, LANES))
    packed = jnp.concatenate(rows, axis=0)
    return jnp.pad(packed, ((0, (-packed.shape[0]) % 8), (0, 0)))


def _unpack_rows(packed, group):
    out, row = {}, 0
    for name, shape in group:
        size = math.prod(shape)
        n_rows = -(-size // LANES)
        out[name] = packed[row:row + n_rows].reshape(-1)[:size].reshape(shape)
        row += n_rows
    return out


def kernel(x, norm1_g, w_in, b_f, ln_v_g, ln_v_b, w_sgu, b_sgu, w_a, w_b, w_o, norm2_g, w_up, w_down, normf_g, loss_target, m_norm1_g, m_w_in, m_b_f, m_ln_v_g, m_ln_v_b, m_w_sgu, m_b_sgu, m_w_a, m_w_b, m_w_o, m_norm2_g, m_w_up, m_w_down, m_normf_g, v_norm1_g, v_w_in, v_b_f, v_ln_v_g, v_ln_v_b, v_w_sgu, v_b_sgu, v_w_a, v_w_b, v_w_o, v_norm2_g, v_w_up, v_w_down, v_normf_g):
    xs = x[0]
    target = loss_target[0]
    s_len, d = xs.shape
    tm = min(512, s_len)
    tl = min(1024, s_len)
    tr = min(512, s_len)
    ta = min(512, s_len)
    tc = min(512, s_len)

    w_in_t = jnp.transpose(w_in[0])
    lin = (IN_SHARD * d // LANES, LANES)
    big = (w_in_t.reshape(lin), w_a[0], w_b[0], w_o[0], w_up[0], w_down[0])
    h, h_t, r1, (w_in_g,) = _rms_fwd("rms1", xs, norm1_g, tm, comm=[_gather_plan(w_in_t.astype(BF16))])
    w_in_f = w_in_g.reshape(IN_COLS, d)
    later_shards = [jnp.transpose(w_a[0]), jnp.transpose(w_b[0]), w_o[0], jnp.transpose(w_up[0]), w_down[0]]
    later_plans = [_gather_plan(w.astype(BF16)) for w in later_shards]

    def unflatten(gathered):
        return [g.reshape(N_DEV * g.shape[1], g.shape[2]) for g in gathered]

    w_qkv = w_in_f[:QKV_W]
    f_lo = QKV_W
    u_lo = f_lo + N_HEADS
    w_rest = jnp.concatenate([w_in_f[u_lo:], jnp.pad(w_in_f[f_lo:u_lo], ((0, LANES - N_HEADS), (0, 0)))], axis=0)

    chunk_id = jnp.arange(SGU_LEN) // CHUNK
    sgu_mask = chunk_id[None, :] <= chunk_id[:, None]
    w_masked = jnp.where(sgu_mask[None], w_sgu[0], 0.0)
    w_stack = w_masked.reshape(SGU_G // 2, 2 * SGU_LEN, SGU_LEN).astype(BF16)
    wt_stack = jnp.transpose(w_masked, (0, 2, 1)).reshape(SGU_G // 2, 2 * SGU_LEN, SGU_LEN).astype(BF16)
    b_pair = jnp.transpose(jnp.repeat(b_sgu[0], SGU_W // SGU_G, axis=0))
    b_f_pad = jnp.pad(b_f, ((0, 0), (0, LANES - N_HEADS)))
    head_sel = (jnp.arange(FOX_W)[:, None] // HEAD_DIM == jnp.arange(LANES)[None, :]).astype(F32)

    def store(dtype):
        def epi(accs, ex, out):
            out[0][...] = accs[0].astype(dtype)
        return epi

    def qkv_epi(accs, ex, out):
        tile = accs[0].astype(BF16)
        out[0][...] = tile
        out[1][...] = jnp.transpose(tile[:, FOX_W:2 * FOX_W])
        out[2][...] = jnp.transpose(tile[:, 2 * FOX_W:])

    t_spec = pl.BlockSpec((FOX_W, tm), lambda i, j: (0, i))
    qkv, k_t, v_t = _mm("proj_qkv", [(h, w_qkv, True, None)], [],
                        [(_sds((s_len, QKV_W), BF16), _tile(tm, QKV_W)), (_sds((FOX_W, s_len), BF16), t_spec),
                         (_sds((FOX_W, s_len), BF16), t_spec)],
                        qkv_epi, m=s_len, tm=tm, n=QKV_W, tn=QKV_W)
    def rest_epi(accs, ex, out):
        out[0][...] = accs[0].astype(BF16)
        out[1][...] = accs[0][:, F_OFF:F_OFF + LANES]

    prest, f_logit = _mm("proj_rest", [(h, w_rest, True, None)], [],
                         [(_sds((s_len, REST_W), BF16), _tile(tm, REST_W)), (_sds((s_len, LANES), F32), _row(tm, LANES))],
                         rest_epi, m=s_len, tm=tm, n=REST_W, tn=REST_W)

    c_col = _forget_cumsum(f_logit, b_f_pad, tc)
    o, o_t, lse_row, later_g = _attn_fwd(qkv, v_t, c_col, ta, ta, comm=later_plans)
    w_a_t, w_b_t, w_o_f, w_up_t, w_down_f = unflatten(later_g)
    sg, sg_t = _sgu_fwd(prest, ln_v_g, ln_v_b, w_stack, b_pair, tm)

    def merge_epi(accs, ex, out):
        ya, yb = accs
        sa, sb = _sigmoid(ex[0][...].astype(F32)), _sigmoid(ex[1][...].astype(F32))
        merged = (sa * ya + sb * yb).astype(BF16)
        out[0][...] = merged
        out[1][...] = ya.astype(BF16)
        out[2][...] = yb.astype(BF16)
        out[3][...] = jnp.transpose(merged)

    merged, ya, yb, merged_t = _mm(
        "merge", [(o, w_a_t, True, None), (sg, w_b_t, True, None)],
        [(prest, _tile(tm, d, GA_OFF // d)), (prest, _tile(tm, d, GB_OFF // d))],
        [(_sds((s_len, d), BF16), _tile(tm, d))] * 3 + [(_sds((d, s_len), BF16), _tile_t(tm, d))],
        merge_epi, m=s_len, tm=tm, n=d, tn=d)

    def resid_epi(accs, ex, out):
        x1v = ex[0][...] + accs[0]
        out[0][...] = x1v
        r = lax.rsqrt(jnp.mean(x1v * x1v, axis=-1, keepdims=True) + EPS)
        h2v = (x1v * r * ex[1][...]).astype(BF16)
        out[1][...] = h2v
        out[2][...] = jnp.transpose(h2v)
        out[3][...] = r

    x1, h2, h2_t, r2 = _mm(
        "out_proj", [(merged, w_o_f, False, None)], [(xs, _tile(tm, d)), (norm2_g, _whole((1, d)))],
        [(_sds((s_len, d), F32), _tile(tm, d)), (_sds((s_len, d), BF16), _tile(tm, d)),
         (_sds((d, s_len), BF16), _tile_t(tm, d)), (_sds((s_len, 1), F32), _row(tm, 1))],
        resid_epi, m=s_len, tm=tm, n=d, tn=d)

    def up_epi(accs, ex, out):
        act = jnp.square(jnp.maximum(accs[0], 0.0)).astype(BF16)
        out[0][...] = act
        out[1][...] = jnp.transpose(act)

    act, act_t = _mm(
        "mlp_up", [(h2, w_up_t, True, None)], [],
        [(_sds((s_len, D_FF), BF16), _tile(tl, 1024)), (_sds((D_FF, s_len), BF16), _tile_t(tl, 1024))],
        up_epi, m=s_len, tm=tl, n=D_FF, tn=1024)

    def first_step():
        return jnp.logical_and(pl.program_id(0) == 0, pl.program_id(1) == 0)

    def accumulate(ref, val):
        @pl.when(first_step())
        def _():
            ref[...] = val

        @pl.when(jnp.logical_not(first_step()))
        def _():
            ref[...] += val

    def final_epi(accs, ex, out):
        x1_ref, t_ref, g_ref = ex
        x2 = x1_ref[...] + accs[0]
        rf = lax.rsqrt(jnp.mean(x2 * x2, axis=-1, keepdims=True) + EPS)
        xh = x2 * rf
        gf = g_ref[...]
        err = xh * gf - t_ref[...]
        dy = err * (1.0 / d)
        dx2 = _rms_bwd(xh, rf, gf, dy)
        out[0][...] = dx2
        accumulate(out[1], jnp.sum(dy * xh, axis=0, keepdims=True))
        part = 0.5 * jnp.sum(jnp.sum(err * err, axis=-1, keepdims=True) * (1.0 / d), axis=0, keepdims=True)
        accumulate(out[2], jnp.broadcast_to(part, (1, LANES)))
        out[3][...] = dx2.astype(BF16)

    gf2 = normf_g.reshape(1, d)
    dx2, g_normf, loss_part, dx2_16 = _mm(
        "mlp_down_loss", [(act, w_down_f, False, None)],
        [(x1, _row(tr, d)), (target, _row(tr, d)), (gf2, _whole((1, d)))],
        [(_sds((s_len, d), F32), _row(tr, d)), (_sds((1, d), F32), _whole((1, d))), (_sds((1, LANES), F32), _whole((1, LANES))),
         (_sds((s_len, d), BF16), _row(tr, d))],
        final_epi, m=s_len, tm=tr, n=d, tn=d, arbitrary=True)

    def dact_epi(accs, ex, out):
        out[0][...] = (accs[0] * (2.0 * jnp.sqrt(ex[0][...].astype(F32)))).astype(BF16)

    (da,) = _mm("mlp_down_bwd", [(dx2_16, w_down_f, True, None)], [(act, _tile(tl, 1024))],
                [(_sds((s_len, D_FF), BF16), _tile(tl, 1024))], dact_epi, m=s_len, tm=tl, n=D_FF, tn=1024)
    g_down = _grad_w("grad_w_down", act_t, dx2_16, tk=1024, tn=d, ts=tl)
    g_up = _grad_w("grad_w_up", h2_t, da, tk=d, tn=1024, ts=tl, block_cols=D_FF // N_DEV)

    def dh2_epi(accs, ex, out):
        x1_ref, r_ref, g_ref, dx2_ref = ex
        r = r_ref[...]
        xh = x1_ref[...] * r
        dh2 = accs[0]
        out[0][...] = dx2_ref[...] + _rms_bwd(xh, r, g_ref[...], dh2)
        accumulate(out[1], jnp.sum(dh2 * xh, axis=0, keepdims=True))

    my_c = lax.axis_index("c")
    my_chip = 2 * lax.axis_index("x") + lax.axis_index("y")
    idx = jnp.stack([my_c, my_chip]).astype(jnp.int32)
    parts16, owns = {}, {}

    def split_cores(g8):
        return g8.reshape((4, 2) + g8.shape[1:])

    def row_tile(r):
        return 512 if r % 512 == 0 else r

    def pair_sums(names, grads4, from_sibling):
        for name, g4, recv in zip(names, grads4, from_sibling):
            parts16[name], owns[name] = _pair_sum("grad_pair_sum_" + name, g4, recv, idx, row_tile(g4.shape[2]))

    grads4_mlp = [split_cores(g_up), split_cores(g_down.reshape(N_DEV, D_FF // N_DEV, d))]
    (dx1, g_norm2), from_sibling = _mm(
        "mlp_up_bwd", [(da, w_up_t, False, None)],
        [(x1, _row(tr, d)), (r2, _row(tr, 1)), (norm2_g, _whole((1, d))), (dx2, _row(tr, d))],
        [(_sds((s_len, d), F32), _row(tr, d)), (_sds((1, d), F32), _whole((1, d)))],
        dh2_epi, m=s_len, tm=tr, n=d, tn=d, arbitrary=True, comm=[_pair_exchange_plan(grads4_mlp)])
    pair_sums(("w_up", "w_down"), grads4_mlp, from_sibling)

    def dmerge_epi(accs, ex, out):
        dm = accs[0]
        sa, sb = _sigmoid(ex[0][...].astype(F32)), _sigmoid(ex[1][...].astype(F32))
        out[0][...] = (dm * sa).astype(BF16)
        out[1][...] = (dm * sb).astype(BF16)
        dga = (dm * ex[2][...] * sa * (1.0 - sa)).astype(BF16)
        dgb = (dm * ex[3][...] * sb * (1.0 - sb)).astype(BF16)
        out[2][...] = dga
        out[3][...] = dgb
        out[4][:d, :] = jnp.transpose(dga)
        out[4][d:, :] = jnp.transpose(dgb)

    dya, dyb, dga, dgb, dg_t = _mm(
        "out_proj_bwd", [(dx1, w_o_f, True, None)],
        [(prest, _tile(tm, d, GA_OFF // d)), (prest, _tile(tm, d, GB_OFF // d)), (ya, _tile(tm, d)), (yb, _tile(tm, d))],
        [(_sds((s_len, d), BF16), _tile(tm, d))] * 4
        + [(_sds((2 * d, s_len), BF16), pl.BlockSpec((2 * d, tm), lambda i, j: (0, i)))],
        dmerge_epi, m=s_len, tm=tm, n=d, tn=d)
    g_o = _grad_w("grad_w_o", merged_t, dx1, tk=d, tn=d, ts=tl).reshape(N_DEV, d // N_DEV, d)
    def col_blocks(g):
        return jnp.transpose(g.reshape(g.shape[0], N_DEV, g.shape[1] // N_DEV), (1, 0, 2))

    g_a = col_blocks(_grad_w("grad_w_a", o_t, dya, tk=FOX_W, tn=d, ts=tl))
    g_b = col_blocks(_grad_w("grad_w_b", sg_t, dyb, tk=SGU_W, tn=d, ts=tl))

    def do_epi(accs, ex, out):
        do = accs[0]
        out[0][...] = do
        out[1][...] = _dot_f32(do * ex[0][...], ex[1][...], "b")

    grads4_mix = [split_cores(g) for g in (g_a, g_b, g_o)]
    (do, delta), from_sibling = _mm(
        "attn_out_bwd", [(dya, w_a_t, False, None)], [(o, _row(tm, FOX_W)), (head_sel, _whole((FOX_W, LANES)))],
        [(_sds((s_len, FOX_W), F32), _row(tm, FOX_W)), (_sds((s_len, LANES), F32), _row(tm, LANES))],
        do_epi, m=s_len, tm=tm, n=FOX_W, tn=FOX_W, comm=[_pair_exchange_plan(grads4_mix)])
    pair_sums(("w_a", "w_b", "w_o"), grads4_mix, from_sibling)
    (dsg,) = _mm("sgu_out_bwd", [(dyb, w_b_t, False, None)], [], [(_sds((s_len, SGU_W), F32), _tile(tm, SGU_W))],
                 store(F32), m=s_len, tm=tm, n=SGU_W, tn=SGU_W)

    du, dsv, dw_pairs, db_pos, g_ln_g, g_ln_b, dusv_t = _sgu_bwd(
        prest, dsg, ln_v_g, ln_v_b, w_stack, wt_stack, b_pair, tm)
    g_w_sgu = jnp.where(sgu_mask[None], dw_pairs.reshape(SGU_G, SGU_LEN, SGU_LEN), 0.0)
    g_b_sgu = jnp.transpose(jnp.sum(db_pos.reshape(SGU_LEN, SGU_G, SGU_W // SGU_G), axis=-1))

    delta_row = jnp.transpose(delta[:, :N_HEADS]).reshape(N_HEADS, 1, s_len)
    early = ("w_a", "w_b", "w_o", "w_up", "w_down")
    small_early = _pack_rows((g_w_sgu, g_b_sgu, g_norm2, g_normf, g_ln_g, g_ln_b))
    (dq, dk, dv, dc_rows_blk, dc_cols, dkv_t), (small_early_all, *from_chips_early) = _attn_bwd(
        qkv, k_t, do, c_col, lse_row, delta_row, ta, ta,
        comm=[_gather_plan(small_early), _chip_exchange_plan([parts16[n] for n in early])])
    dc_rows = jnp.transpose(dc_rows_blk.reshape(s_len // ta, N_HEADS, ta), (0, 2, 1)).reshape(s_len, N_HEADS)
    dc_rows = jnp.pad(dc_rows, ((0, 0), (0, LANES - N_HEADS)))
    dfl, g_bf, dfl_t = _forget_bwd(dc_rows, dc_cols, f_logit, b_f_pad, tc)

    dp_t = (jnp.transpose(dq), dkv_t, dfl_t, dusv_t, dg_t)
    g_in_rows = [_grad_w("grad_w_in_%d" % k, seg_t, h, tk=min(seg_t.shape[0], 1024), tn=d, ts=tl)
                 for k, seg_t in enumerate(dp_t)]
    g_in_rows[2] = g_in_rows[2][:N_HEADS]
    g_in = jnp.concatenate(g_in_rows, axis=0).reshape((N_DEV,) + lin)

    def dx_epi(accs, ex, out):
        x_ref, r_ref, g_ref, dx1_ref = ex
        dh = accs[0]
        for extra in accs[1:]:
            dh = dh + extra
        r = r_ref[...]
        xh = x_ref[...] * r
        out[0][...] = dx1_ref[...] + _rms_bwd(xh, r, g_ref[...], dh)
        accumulate(out[1], jnp.sum(dh * xh, axis=0, keepdims=True))

    rest_cols = ((du, U_OFF, 512), (dsv, SV_OFF, 512), (dga, GA_OFF, 1024), (dgb, GB_OFF, 1024), (dfl, F_OFF, LANES))
    dx_pairs = [(seg, w_qkv, False, (512 * k, 512 * (k + 1))) for k, seg in enumerate((dq, dk, dv))]
    dx_pairs += [(seg, w_rest, False, (lo, lo + width)) for seg, lo, width in rest_cols]
    grads4_in = [split_cores(g_in)]
    pair_sums(("w_in",), grads4_in, _run_comm("grad_pair_exchange_w_in", [_pair_exchange_plan(grads4_in)]))
    (grad_x, g_norm1), (from_chips_in,) = _mm(
        "proj_bwd", dx_pairs,
        [(xs, _row(tr, d)), (r1, _row(tr, 1)), (norm1_g, _whole((1, d))), (dx1, _row(tr, d))],
        [(_sds((s_len, d), F32), _row(tr, d)), (_sds((1, d), F32), _whole((1, d)))],
        dx_epi, m=s_len, tm=tr, n=d, tn=d, arbitrary=True, comm=[_chip_exchange_plan([parts16["w_in"]])])
    small_late = _pack_rows((g_bf[:, :N_HEADS], g_norm1, loss_part[0, 0]))
    (small_late_all,) = _run_comm("gather_last_grads", [_gather_plan(small_late)])
    from_chips = dict(zip(early, from_chips_early), w_in=from_chips_in)

    names = ("w_in", "w_a", "w_b", "w_o", "w_up", "w_down")
    moments_m = (m_w_in, m_w_a, m_w_b, m_w_o, m_w_up, m_w_down)
    moments_v = (v_w_in, v_w_a, v_w_b, v_w_o, v_w_up, v_w_down)
    big_out = {}
    for name, w, m, v in zip(names, big, moments_m, moments_v):
        own = owns[name]
        transposed = name == "w_in"
        m0, v0 = (jnp.transpose(m[0]).reshape(lin), jnp.transpose(v[0]).reshape(lin)) if transposed else (m[0], v[0])
        res = _adamw_shard("adamw_" + name, own, from_chips[name], w, m0, v0, row_tile(own.shape[0]))
        big_out[name] = [(jnp.transpose(t.reshape(IN_SHARD, d)) if transposed else t)[None] for t in res]

    zero = jnp.zeros((), F32)
    sgu_rows = (SGU_G * SGU_LEN, LANES)
    res_sgu = _adamw_small("adamw_w_sgu", small_early_all, 0, w_sgu.reshape(sgu_rows), m_w_sgu.reshape(sgu_rows),
                           v_w_sgu.reshape(sgu_rows))
    small_out = {"w_sgu": [t.reshape(w_sgu.shape) for t in res_sgu]}
    res_early = _adamw_small(
        "adamw_tiny_early", small_early_all, sgu_rows[0], _pack_rows((b_sgu, norm2_g, normf_g, ln_v_g, ln_v_b)),
        _pack_rows((m_b_sgu, m_norm2_g, m_normf_g, m_ln_v_g, m_ln_v_b)),
        _pack_rows((v_b_sgu, v_norm2_g, v_normf_g, v_ln_v_g, v_ln_v_b)))
    res_late = _adamw_small(
        "adamw_tiny_late", small_late_all, 0, _pack_rows((b_f, norm1_g, zero)), _pack_rows((m_b_f, m_norm1_g, zero)),
        _pack_rows((v_b_f, v_norm1_g, zero)))
    for res, group in ((res_early, _TINY_EARLY), (res_late, _TINY_LATE)):
        unpacked = [_unpack_rows(t, group) for t in res]
        small_out.update({name: [u[name] for u in unpacked] for name, _ in group})
    loss = small_out["loss"][0]

    order = ("norm1_g", "w_in", "b_f", "ln_v_g", "ln_v_b", "w_sgu", "b_sgu", "w_a", "w_b", "w_o", "norm2_g", "w_up",
             "w_down", "normf_g")
    table = {**big_out, **small_out}
    outs = [loss, grad_x[None]]
    for kind in range(4):
        outs += [table[n][kind] for n in order]
    return tuple(outs)
```

```python
import math

import jax
import jax.numpy as jnp
from jax import lax
from jax.experimental import pallas as pl
from jax.experimental.pallas import tpu as pltpu

F32 = jnp.float32
BF16 = jnp.bfloat16

N_DEV = 8
D_MODEL = 1024
N_HEADS = 8
HEAD_DIM = 64
FOX_W = N_HEADS * HEAD_DIM
SGU_G = 8
SGU_W = 512
SGU_LEN = 128
CHUNK = 64
D_FF = 4 * D_MODEL
IN_COLS = 3 * FOX_W + N_HEADS + 2 * SGU_W + 2 * D_MODEL
IN_SHARD = IN_COLS // N_DEV
LANES = 128
QKV_W = 3 * FOX_W
U_OFF, SV_OFF, GA_OFF, GB_OFF, F_OFF = 0, 512, 1024, 2048, 3072
REST_W = F_OFF + LANES
EPS = 1e-6
NEG = -1e30

ADAM_LR = 0.001
ADAM_B1 = 0.9
ADAM_B2 = 0.999
ADAM_EPS = 1e-08
ADAM_WD = 0.01
ADAM_STEP = 10

VMEM_LIMIT = 56 * 1024 * 1024
MESH = pl.DeviceIdType.MESH


def _params(sem=None):
    return pltpu.CompilerParams(dimension_semantics=sem, vmem_limit_bytes=VMEM_LIMIT)


def _dot(a, b):
    return jnp.dot(a, b, preferred_element_type=F32)


def _dot_nt(a, b):
    return lax.dot_general(a, b, (((1,), (1,)), ((), ())), preferred_element_type=F32)


def _split3(x):
    hi = x.astype(BF16)
    rest = x - hi.astype(F32)
    mid = rest.astype(BF16)
    return hi, mid, (rest - mid.astype(F32)).astype(BF16)


def _dot_f32(a, b, exact):
    if exact == "a":
        a16 = a.astype(BF16)
        return sum(_dot(a16, part) for part in _split3(b))
    b16 = b.astype(BF16)
    return sum(_dot(part, b16) for part in _split3(a))


def _sigmoid(x):
    return 1.0 / (1.0 + jnp.exp(-x))


def _log_sigmoid(z):
    return jnp.minimum(z, 0.0) - jnp.log(1.0 + jnp.exp(-jnp.abs(z)))


_GELU_K = math.sqrt(2.0 / math.pi)
_GELU_C = 0.044715


def _gelu(x):
    t = jnp.tanh(_GELU_K * (x + _GELU_C * (x * x * x)))
    return 0.5 * x * (1.0 + t)


def _gelu_grad(x):
    x2 = x * x
    t = jnp.tanh(_GELU_K * (x + _GELU_C * (x2 * x)))
    return 0.5 * (1.0 + t) + 0.5 * x * (1.0 - t * t) * (_GELU_K * (1.0 + 3.0 * _GELU_C * x2))


def _rms_bwd(xh, r, g, dy):
    gy = dy * g
    return r * (gy - xh * jnp.mean(xh * gy, axis=-1, keepdims=True))


def _lane_lt64(shape):
    return lax.broadcasted_iota(jnp.int32, shape, len(shape) - 1) < HEAD_DIM


class _Comm:
    def __init__(self, arrays, out_shapes, sems, start, finish, mid=None):
        self.arrays, self.out_shapes, self.sems = list(arrays), list(out_shapes), list(sems)
        self.start, self.mid, self.finish = start, mid, finish


def _comm_phase(plans, phase, in_refs, out_refs, sem_refs):
    ia = io = ks = 0
    for plan in plans:
        na, no, ns = len(plan.arrays), len(plan.out_shapes), len(plan.sems)
        fn = getattr(plan, phase)
        if fn is not None:
            fn(in_refs[ia:ia + na], out_refs[io:io + no], sem_refs[ks:ks + ns])
        ia, io, ks = ia + na, io + no, ks + ns


def _comm_operands(plans):
    arrays = [a for plan in plans for a in plan.arrays]
    out_shapes = [o for plan in plans for o in plan.out_shapes]
    sems = [s for plan in plans for s in plan.sems]
    return arrays, out_shapes, sems


_ANY = pl.BlockSpec(memory_space=pl.ANY)


def _run_comm(name, plans):
    arrays, out_shapes, sems = _comm_operands(plans)
    n_in, n_out = len(arrays), len(out_shapes)

    def body(*refs):
        parts = refs[:n_in], refs[n_in:n_in + n_out], refs[n_in + n_out:]
        for phase in ("start", "mid", "finish"):
            _comm_phase(plans, phase, *parts)

    return pl.pallas_call(
        body, name=name, out_shape=out_shapes, in_specs=[_ANY] * n_in, out_specs=[_ANY] * n_out, scratch_shapes=sems,
    )(*arrays)


def _gather_plan(shard):
    def setup(ins, outs, sems):
        (x_ref,), (out_ref,), (send_sems, recv_sems, local_sem) = ins, outs, sems
        x, y, c = lax.axis_index("x"), lax.axis_index("y"), lax.axis_index("c")
        me, sibling = (x, y, c), (x, y, 1 - c)
        chips = [(1 - x, y), (x, 1 - y), (1 - x, 1 - y)]

        def rows(px, py, pc):
            return out_ref.at[4 * px + 2 * py + pc]

        def copy(k, block, to, src=None):
            return pltpu.make_async_remote_copy(
                src_ref=rows(*block) if src is None else src,
                dst_ref=rows(*block),
                send_sem=send_sems.at[k],
                recv_sem=recv_sems.at[k],
                device_id=to,
                device_id_type=MESH,
            )

        mine = pltpu.make_async_copy(x_ref, rows(*me), local_sem)
        first = [copy(0, me, sibling, src=x_ref)]
        first += [copy(1 + j, me, (*chip, c), src=x_ref) for j, chip in enumerate(chips)]
        passed = [copy(4 + j, (*chip, c), sibling) for j, chip in enumerate(chips)]
        landed = [copy(1 + j, (*chip, c), me) for j, chip in enumerate(chips)]
        from_sibling = [copy(0, sibling, me)] + [copy(4 + j, (*chip, 1 - c), me) for j, chip in enumerate(chips)]
        return mine, first, passed, landed, from_sibling

    def start(ins, outs, sems):
        mine, first, _, _, _ = setup(ins, outs, sems)
        mine.start()
        for cp in first:
            cp.start()

    def mid(ins, outs, sems):
        _, _, passed, landed, _ = setup(ins, outs, sems)
        for arrived, onward in zip(landed, passed):
            arrived.wait_recv()
            onward.start()

    def finish(ins, outs, sems):
        mine, first, passed, _, from_sibling = setup(ins, outs, sems)
        for cp in from_sibling:
            cp.wait_recv()
        for cp in first + passed:
            cp.wait_send()
        mine.wait()

    return _Comm([shard], [jax.ShapeDtypeStruct((N_DEV,) + shard.shape, shard.dtype)],
                 [pltpu.SemaphoreType.DMA((7,)), pltpu.SemaphoreType.DMA((7,)), pltpu.SemaphoreType.DMA],
                 start, finish, mid)


def _start_all(copies):
    for cp in copies:
        cp.start()


def _wait_all(copies):
    for cp in copies:
        cp.wait_recv()
    for cp in copies:
        cp.wait_send()


def _pair_exchange_plan(grads):
    n = len(grads)

    def copies(ins, outs, sems):
        send_sems, recv_sems = sems
        x, y, c = lax.axis_index("x"), lax.axis_index("y"), lax.axis_index("c")
        return [
            pltpu.make_async_remote_copy(
                src_ref=ins[k].at[:, 1 - c],
                dst_ref=outs[k],
                send_sem=send_sems.at[k],
                recv_sem=recv_sems.at[k],
                device_id=(x, y, 1 - c),
                device_id_type=MESH,
            )
            for k in range(n)
        ]

    return _Comm(grads, [jax.ShapeDtypeStruct((4,) + g.shape[2:], g.dtype) for g in grads],
                 [pltpu.SemaphoreType.DMA((n,)), pltpu.SemaphoreType.DMA((n,))],
                 lambda *refs: _start_all(copies(*refs)), lambda *refs: _wait_all(copies(*refs)))


def _chip_exchange_plan(parts):
    n = len(parts)

    def copies(ins, outs, sems):
        send_sems, recv_sems = sems
        x, y, c = lax.axis_index("x"), lax.axis_index("y"), lax.axis_index("c")
        chips = [(1 - x, y), (x, 1 - y), (1 - x, 1 - y)]
        return [
            pltpu.make_async_remote_copy(
                src_ref=ins[k].at[2 * px + py],
                dst_ref=outs[k].at[j],
                send_sem=send_sems.at[3 * k + j],
                recv_sem=recv_sems.at[3 * k + j],
                device_id=(px, py, c),
                device_id_type=MESH,
            )
            for k in range(n) for j, (px, py) in enumerate(chips)
        ]

    return _Comm(parts, [jax.ShapeDtypeStruct((3,) + p.shape[1:], p.dtype) for p in parts],
                 [pltpu.SemaphoreType.DMA((3 * n,)), pltpu.SemaphoreType.DMA((3 * n,))],
                 lambda *refs: _start_all(copies(*refs)), lambda *refs: _wait_all(copies(*refs)))


def _mm(name, pairs, extras, outs, epi, *, m, tm, n, tn, arbitrary=False, comm=()):
    nj = n // tn
    a_arrays, a_specs, b_arrays, b_specs, b_index = [], [], [], [], []
    for a, b, nt, cols in pairs:
        a_arrays.append(a)
        a_specs.append(pl.BlockSpec((tm, a.shape[1]), lambda i, j: (i, 0)))
        known = [k for k, other in enumerate(b_arrays) if other is b]
        if known:
            b_index.append(known[0])
            continue
        b_index.append(len(b_arrays))
        b_arrays.append(b)
        if cols is not None:
            assert nj == 1
            b_specs.append(pl.BlockSpec(b.shape, lambda i, j: (0, 0)))
        elif nt:
            b_specs.append(pl.BlockSpec((tn, b.shape[1]), lambda i, j: (j, 0)))
        else:
            b_specs.append(pl.BlockSpec((b.shape[0], tn), lambda i, j: (0, j)))
    comm_arrays, comm_outs, comm_sems = _comm_operands(comm)
    arrays = a_arrays + b_arrays + [arr for arr, _ in extras] + comm_arrays
    in_specs = a_specs + b_specs + [spec for _, spec in extras] + [_ANY] * len(comm_arrays)
    n_a, n_b, n_extras, n_ci, n_out, n_co = len(a_arrays), len(b_arrays), len(extras), len(comm_arrays), len(outs), len(comm_outs)
    ni = m // tm

    def body(*refs):
        a_refs = refs[:n_a]
        b_refs = refs[n_a:n_a + n_b]
        ex = refs[n_a + n_b:n_a + n_b + n_extras]
        n_in = n_a + n_b + n_extras + n_ci
        comm_refs = refs[n_in - n_ci:n_in], refs[n_in + n_out:n_in + n_out + n_co], refs[n_in + n_out + n_co:]
        out = refs[n_in:n_in + n_out]
        if comm:
            @pl.when(jnp.logical_and(pl.program_id(0) == 0, pl.program_id(1) == 0))
            def _():
                _comm_phase(comm, "start", *comm_refs)

        accs = []
        for p, (_, _, nt, cols) in enumerate(pairs):
            av = a_refs[p][...]
            if av.dtype != BF16:
                av = av.astype(BF16)
            b_ref = b_refs[b_index[p]]
            if cols is None:
                bv = b_ref[...]
            else:
                bv = b_ref[:, cols[0]:cols[1]] if nt else b_ref[cols[0]:cols[1], :]
            accs.append(_dot_nt(av, bv) if nt else _dot(av, bv))
        epi(accs, ex, out)
        if comm:
            mid_row = ni // 2 if ni >= 3 else ni - 1
            mid_col = 0 if ni >= 3 else nj - 1

            @pl.when(jnp.logical_and(pl.program_id(0) == mid_row, pl.program_id(1) == mid_col))
            def _():
                _comm_phase(comm, "mid", *comm_refs)

            @pl.when(jnp.logical_and(pl.program_id(0) == ni - 1, pl.program_id(1) == nj - 1))
            def _():
                _comm_phase(comm, "finish", *comm_refs)

    sem = ("arbitrary", "arbitrary") if arbitrary or comm else ("parallel", "parallel")
    res = pl.pallas_call(
        body,
        name=name,
        grid=(ni, nj),
        in_specs=in_specs,
        out_specs=[spec for _, spec in outs] + [_ANY] * n_co,
        out_shape=[shape for shape, _ in outs] + comm_outs,
        scratch_shapes=comm_sems,
        compiler_params=_params(sem),
    )(*arrays)
    return (res[:n_out], res[n_out:]) if comm else res


def _tile(tm, tn, off=0):
    return pl.BlockSpec((tm, tn), lambda i, j: (i, j + off))


def _row(tm, w, blk=0):
    return pl.BlockSpec((tm, w), lambda i, j: (i, blk))


def _whole(shape):
    zeros = (0,) * len(shape)
    return pl.BlockSpec(shape, lambda i, j: zeros)


def _sds(shape, dtype):
    return jax.ShapeDtypeStruct(shape, dtype)


def _tile_t(tm, tn):
    return pl.BlockSpec((tn, tm), lambda i, j: (j, i))


def _grad_w(name, a_t, g, *, tk, tn, ts, block_cols=None):
    ka, s_len = a_t.shape
    n = g.shape[1]
    width = tn if block_cols is None else block_cols

    def body(a_ref, g_ref, o_ref):
        first = pl.program_id(2) == 0
        gv = g_ref[...].astype(BF16)
        for b in range(tn // width):
            part = _dot(a_ref[...], gv[:, b * width:(b + 1) * width])
            dst = o_ref if block_cols is None else o_ref.at[b]

            @pl.when(first)
            def _():
                dst[...] = part

            @pl.when(jnp.logical_not(first))
            def _():
                dst[...] += part

    if block_cols is None:
        out_shape = _sds((ka, n), F32)
        out_spec = pl.BlockSpec((tk, tn), lambda i, j, s: (i, j))
    else:
        out_shape = _sds((n // width, ka, width), F32)
        out_spec = pl.BlockSpec((tn // width, tk, width), lambda i, j, s: (j, i, 0))
    return pl.pallas_call(
        body,
        name=name,
        grid=(ka // tk, n // tn, s_len // ts),
        in_specs=[pl.BlockSpec((tk, ts), lambda i, j, s: (i, s)), pl.BlockSpec((ts, tn), lambda i, j, s: (s, j))],
        out_specs=out_spec,
        out_shape=out_shape,
        compiler_params=_params(("parallel", "parallel", "arbitrary")),
    )(a_t, g)


def _rms_fwd(name, x, g, tm, comm=()):
    s_len, d = x.shape
    steps = s_len // tm
    comm_arrays, comm_outs, comm_sems = _comm_operands(comm)
    n_ci, n_co = len(comm_arrays), len(comm_outs)

    def body(x_ref, g_ref, *rest):
        comm_refs = rest[:n_ci], rest[n_ci + 3:n_ci + 3 + n_co], rest[n_ci + 3 + n_co:]
        h_ref, ht_ref, r_ref = rest[n_ci:n_ci + 3]
        for phase, at in (("start", 0), ("mid", steps // 2)):
            if comm:
                @pl.when(pl.program_id(0) == at)
                def _():
                    _comm_phase(comm, phase, *comm_refs)

        xv = x_ref[...]
        r = lax.rsqrt(jnp.mean(xv * xv, axis=-1, keepdims=True) + EPS)
        h = (xv * r * g_ref[...]).astype(BF16)
        h_ref[...] = h
        ht_ref[...] = jnp.transpose(h)
        r_ref[...] = r
        if comm:
            @pl.when(pl.program_id(0) == steps - 1)
            def _():
                _comm_phase(comm, "finish", *comm_refs)

    res = pl.pallas_call(
        body,
        name=name,
        grid=(steps,),
        in_specs=[pl.BlockSpec((tm, d), lambda i: (i, 0)), pl.BlockSpec((1, d), lambda i: (0, 0))] + [_ANY] * n_ci,
        out_specs=[pl.BlockSpec((tm, d), lambda i: (i, 0)), pl.BlockSpec((d, tm), lambda i: (0, i)),
                   pl.BlockSpec((tm, 1), lambda i: (i, 0))] + [_ANY] * n_co,
        out_shape=[_sds((s_len, d), BF16), _sds((d, s_len), BF16), _sds((s_len, 1), F32)] + comm_outs,
        scratch_shapes=comm_sems,
        compiler_params=_params(("arbitrary",) if comm else ("parallel",)),
    )(x, g, *comm_arrays)
    return res[0], res[1], res[2], res[3:]


def _forget_cumsum(prest, b_f_pad, tc):
    s_len = prest.shape[0]

    def body(f_ref, b_ref, c_ref, carry):
        @pl.when(pl.program_id(0) == 0)
        def _():
            carry[...] = jnp.zeros_like(carry)

        logf = _log_sigmoid(f_ref[...] + b_ref[...])
        row = lax.broadcasted_iota(jnp.int32, (tc, tc), 0)
        col = lax.broadcasted_iota(jnp.int32, (tc, tc), 1)
        tri = (row >= col).astype(F32)
        c = _dot_f32(tri, logf, "a") + carry[...]
        c_ref[...] = c
        carry[...] = c[tc - 1:tc, :]

    return pl.pallas_call(
        body,
        name="forget_cumsum",
        grid=(s_len // tc,),
        in_specs=[pl.BlockSpec((tc, LANES), lambda i: (i, 0)), pl.BlockSpec((1, LANES), lambda i: (0, 0))],
        out_specs=pl.BlockSpec((tc, LANES), lambda i: (i, 0)),
        out_shape=_sds((s_len, LANES), F32),
        scratch_shapes=[pltpu.VMEM((1, LANES), F32)],
        compiler_params=_params(("arbitrary",)),
    )(prest, b_f_pad)


def _stack_heads(pair, lt64):
    zero = jnp.zeros_like(pair)
    return jnp.concatenate([jnp.where(lt64, pair, zero), jnp.where(lt64, zero, pair)], axis=0)


def _score_tiles(q_ref, k_ref, ck_ref, st_sc, tk):
    lt64 = _lane_lt64((tk, LANES))
    for p in range(N_HEADS // 2):
        lanes = slice(p * LANES, (p + 1) * LANES)
        q_pair = q_ref[:, lanes] * jnp.asarray(HEAD_DIM ** -0.5, BF16)
        st2 = _dot_nt(_stack_heads(k_ref[:, lanes], lt64), q_pair)
        for half in range(2):
            h = 2 * p + half
            st_sc[h] = st2[half * tk:(half + 1) * tk] - ck_ref[:, h:h + 1]


ROW_CHUNK = 64


def _row_chunks(tk):
    rc = min(ROW_CHUNK, tk)
    return [slice(r, r + rc) for r in range(0, tk, rc)]


def _by_sublane(x):
    return x.reshape(x.shape[0] // 8, 8, x.shape[1])


def _softmax_update(st_sc, p_sc, m_sc, l_sc, tk, tq):
    alphas = []
    for h in range(N_HEADS):
        top8 = jnp.full((8, tq), NEG, F32)
        for rows in _row_chunks(tk):
            top8 = jnp.maximum(top8, jnp.max(_by_sublane(st_sc[h, rows, :]), axis=0))
        m_old = m_sc[h]
        m_new = jnp.maximum(m_old, jnp.max(top8, axis=0, keepdims=True))
        sum8 = jnp.zeros((8, tq), F32)
        for rows in _row_chunks(tk):
            pt = jnp.exp(st_sc[h, rows, :] - m_new)
            p_sc[h, rows, :] = pt.astype(BF16)
            sum8 = sum8 + jnp.sum(_by_sublane(pt), axis=0)
        alpha = jnp.exp(m_old - m_new)
        l_sc[h] = alpha * l_sc[h] + jnp.sum(sum8, axis=0, keepdims=True)
        m_sc[h] = m_new
        alphas.append(alpha)
    return alphas


def _mask_diagonal(st_sc, i, j, tq, tk):
    @pl.when((j + 1) * tk - 1 > i * tq)
    def _():
        key = j * tk + lax.broadcasted_iota(jnp.int32, (tk, tq), 0)
        query = i * tq + lax.broadcasted_iota(jnp.int32, (tk, tq), 1)
        st_sc[...] = jnp.where((query >= key)[None], st_sc[...], NEG)


def _attn_fwd(qkv, v_t, c_col, tq, tk, comm=()):
    s_len = qkv.shape[0]
    ratio = tq // tk
    steps = [(i, j) for i in range(s_len // tq) for j in range((i + 1) * ratio)]
    i_tab = jnp.asarray([i for i, _ in steps], jnp.int32)
    j_tab = jnp.asarray([j for _, j in steps], jnp.int32)

    comm_arrays, comm_outs, comm_sems = _comm_operands(comm)
    n_ci, n_co = len(comm_arrays), len(comm_outs)

    def body(i_ref, j_ref, q_ref, k_ref, vt_ref, ck_ref, *rest):
        comm_refs = rest[:n_ci], rest[n_ci + 3:n_ci + 3 + n_co], rest[n_ci + 3 + n_co + 5:]
        o_ref, ot_ref, lse_ref = rest[n_ci:n_ci + 3]
        acc_t, m_sc, l_sc, st_sc, p_sc = rest[n_ci + 3 + n_co:n_ci + 3 + n_co + 5]
        n = pl.program_id(0)
        i, j = i_ref[n], j_ref[n]
        for phase, at in (("start", 0), ("mid", (2 * len(steps)) // 3)):
            if comm:
                @pl.when(n == at)
                def _():
                    _comm_phase(comm, phase, *comm_refs)

        @pl.when(j == 0)
        def _():
            acc_t[...] = jnp.zeros_like(acc_t)
            m_sc[...] = jnp.full_like(m_sc, NEG)
            l_sc[...] = jnp.zeros_like(l_sc)

        _score_tiles(q_ref, k_ref, ck_ref, st_sc, tk)
        _mask_diagonal(st_sc, i, j, tq, tk)
        alpha = _softmax_update(st_sc, p_sc, m_sc, l_sc, tk, tq)
        top = lax.broadcasted_iota(jnp.int32, (LANES, tq), 0) < HEAD_DIM
        for p in range(N_HEADS // 2):
            lanes = slice(p * LANES, (p + 1) * LANES)
            vt_pair = vt_ref[lanes, :]
            pv = jnp.where(top, _dot(vt_pair, p_sc[2 * p]), _dot(vt_pair, p_sc[2 * p + 1]))
            acc_t[lanes, :] = acc_t[lanes, :] * jnp.where(top, alpha[2 * p], alpha[2 * p + 1]) + pv

        @pl.when(j == (i + 1) * ratio - 1)
        def _():
            for p in range(N_HEADS // 2):
                lanes = slice(p * LANES, (p + 1) * LANES)
                l_pair = jnp.where(top, l_sc[2 * p], l_sc[2 * p + 1])
                o_t = acc_t[lanes, :] / l_pair
                o_ref[:, lanes] = jnp.transpose(o_t)
                ot_ref[lanes, :] = o_t.astype(BF16)
            lse_ref[...] = m_sc[...] + jnp.log(l_sc[...])

        if comm:
            @pl.when(n == len(steps) - 1)
            def _():
                _comm_phase(comm, "finish", *comm_refs)

    stat = pltpu.VMEM((N_HEADS, 1, tq), F32)
    res = pl.pallas_call(
        body,
        name="attn_fwd",
        grid_spec=pltpu.PrefetchScalarGridSpec(
            num_scalar_prefetch=2,
            grid=(len(steps),),
            in_specs=[
                pl.BlockSpec((tq, FOX_W), lambda n, it, jt: (it[n], 0)),
                pl.BlockSpec((tk, FOX_W), lambda n, it, jt: (jt[n], 1)),
                pl.BlockSpec((FOX_W, tk), lambda n, it, jt: (0, jt[n])),
                pl.BlockSpec((tk, LANES), lambda n, it, jt: (jt[n], 0)),
            ] + [_ANY] * n_ci,
            out_specs=[
                pl.BlockSpec((tq, FOX_W), lambda n, it, jt: (it[n], 0)),
                pl.BlockSpec((FOX_W, tq), lambda n, it, jt: (0, it[n])),
                pl.BlockSpec((N_HEADS, 1, tq), lambda n, it, jt: (0, 0, it[n])),
            ] + [_ANY] * n_co,
            scratch_shapes=[pltpu.VMEM((FOX_W, tq), F32), stat, stat, pltpu.VMEM((N_HEADS, tk, tq), F32),
                            pltpu.VMEM((N_HEADS, tk, tq), BF16)] + comm_sems,
        ),
        out_shape=[_sds((s_len, FOX_W), F32), _sds((FOX_W, s_len), BF16), _sds((N_HEADS, 1, s_len), F32)] + comm_outs,
        compiler_params=_params(("arbitrary",)),
    )(i_tab, j_tab, qkv, qkv, v_t, c_col, *comm_arrays)
    return res[0], res[1], res[2], res[3:]


def _sgu_mix(vn, w_stack, lt64):
    outs = []
    for p in range(SGU_G // 2):
        r = _dot(w_stack[p], vn[:, p * LANES:(p + 1) * LANES])
        outs.append(jnp.where(lt64, r[:SGU_LEN], r[SGU_LEN:]))
    return jnp.concatenate(outs, axis=1)


def _sgu_norm(sv, ln_g, ln_b):
    svg = _gelu(sv)
    xc = svg - jnp.mean(svg, axis=-1, keepdims=True)
    rstd = lax.rsqrt(jnp.mean(xc * xc, axis=-1, keepdims=True) + EPS)
    xhat = xc * rstd
    return xhat, rstd, xhat * ln_g + ln_b


def _sgu_fwd(prest, ln_g, ln_b, w_stack, b_pair, tm):
    s_len = prest.shape[0]

    def body(u_ref, sv_ref, g_ref, b_ref, w_ref, bp_ref, sg_ref, sgt_ref):
        lt64 = _lane_lt64((SGU_LEN, LANES))
        _, _, vn = _sgu_norm(sv_ref[...].astype(F32), g_ref[...], b_ref[...])
        vn = vn.astype(BF16)
        w_stack_v = [w_ref[p] for p in range(SGU_G // 2)]
        for w in range(tm // SGU_LEN):
            win = slice(w * SGU_LEN, (w + 1) * SGU_LEN)
            mixed = _sgu_mix(vn[win], w_stack_v, lt64) + bp_ref[...]
            sg = (_gelu(u_ref[win, :].astype(F32)) * mixed).astype(BF16)
            sg_ref[win, :] = sg
            sgt_ref[:, win] = jnp.transpose(sg)

    return pl.pallas_call(
        body,
        name="sgu_fwd",
        grid=(s_len // tm,),
        in_specs=[
            pl.BlockSpec((tm, SGU_W), lambda i: (i, U_OFF // SGU_W)),
            pl.BlockSpec((tm, SGU_W), lambda i: (i, SV_OFF // SGU_W)),
            pl.BlockSpec((1, SGU_W), lambda i: (0, 0)),
            pl.BlockSpec((1, SGU_W), lambda i: (0, 0)),
            pl.BlockSpec((SGU_G // 2, 2 * SGU_LEN, SGU_LEN), lambda i: (0, 0, 0)),
            pl.BlockSpec((SGU_LEN, SGU_W), lambda i: (0, 0)),
        ],
        out_specs=[pl.BlockSpec((tm, SGU_W), lambda i: (i, 0)), pl.BlockSpec((SGU_W, tm), lambda i: (0, i))],
        out_shape=[_sds((s_len, SGU_W), BF16), _sds((SGU_W, s_len), BF16)],
        compiler_params=_params(("parallel",)),
    )(prest, prest, ln_g, ln_b, w_stack, b_pair)


def _sgu_bwd(prest, dyb, w_b_t, ln_g, ln_b, w_stack, wt_stack, b_pair, tm):
    s_len = prest.shape[0]
    n_pair = SGU_G // 2

    def body(u_ref, sv_ref, dyb_ref, wb_ref, g_ref, b_ref, w_ref, wt_ref, bp_ref,
             du_ref, dsv_ref, dw_ref, db_ref, dg_ref, dbeta_ref, dusvt_ref, dvn_sc):
        @pl.when(pl.program_id(0) == 0)
        def _():
            dw_ref[...] = jnp.zeros_like(dw_ref)
            db_ref[...] = jnp.zeros_like(db_ref)
            dg_ref[...] = jnp.zeros_like(dg_ref)
            dbeta_ref[...] = jnp.zeros_like(dbeta_ref)

        lt64 = _lane_lt64((SGU_LEN, LANES))
        sv = sv_ref[...].astype(F32)
        xhat, rstd, vn32 = _sgu_norm(sv, g_ref[...], b_ref[...])
        vn = vn32.astype(BF16)
        w_stack_v = [w_ref[p] for p in range(n_pair)]
        dsg = _dot(dyb_ref[...], wb_ref[...])
        db = jnp.zeros((SGU_LEN, SGU_W), F32)
        for w in range(tm // SGU_LEN):
            win = slice(w * SGU_LEN, (w + 1) * SGU_LEN)
            u = u_ref[win, :].astype(F32)
            dsg_w = dsg[win]
            mixed = _sgu_mix(vn[win], w_stack_v, lt64) + bp_ref[...]
            du = (dsg_w * mixed * _gelu_grad(u)).astype(BF16)
            du_ref[win, :] = du
            dusvt_ref[:SGU_W, win] = jnp.transpose(du)
            dmixed = dsg_w * _gelu(u)
            db = db + dmixed
            dm16 = dmixed.astype(BF16)
            for p in range(n_pair):
                lanes = slice(p * LANES, (p + 1) * LANES)
                dmp = dm16[:, lanes]
                r = _dot(wt_ref[p], dmp)
                dvn_sc[win, lanes] = jnp.where(lt64, r[:SGU_LEN], r[SGU_LEN:])
                zero = jnp.zeros_like(dmp)
                dm_ab = jnp.concatenate([jnp.where(lt64, dmp, zero), jnp.where(lt64, zero, dmp)], axis=0)
                dw_ref[p] += _dot_nt(dm_ab, vn[win, lanes])
        db_ref[...] += db
        dvn = dvn_sc[...]
        dg_ref[...] += jnp.sum(dvn * xhat, axis=0, keepdims=True)
        dbeta_ref[...] += jnp.sum(dvn, axis=0, keepdims=True)
        dxh = dvn * g_ref[...]
        dsvg = rstd * (dxh - jnp.mean(dxh, axis=-1, keepdims=True) - xhat * jnp.mean(dxh * xhat, axis=-1, keepdims=True))
        dsv = (dsvg * _gelu_grad(sv)).astype(BF16)
        dsv_ref[...] = dsv
        dusvt_ref[SGU_W:, :] = jnp.transpose(dsv)

    const2 = lambda i: (0, 0)
    const3 = lambda i: (0, 0, 0)
    return pl.pallas_call(
        body,
        name="sgu_bwd",
        grid=(s_len // tm,),
        in_specs=[
            pl.BlockSpec((tm, SGU_W), lambda i: (i, U_OFF // SGU_W)),
            pl.BlockSpec((tm, SGU_W), lambda i: (i, SV_OFF // SGU_W)),
            pl.BlockSpec((tm, dyb.shape[1]), lambda i: (i, 0)),
            pl.BlockSpec(w_b_t.shape, const2),
            pl.BlockSpec((1, SGU_W), const2),
            pl.BlockSpec((1, SGU_W), const2),
            pl.BlockSpec((n_pair, 2 * SGU_LEN, SGU_LEN), const3),
            pl.BlockSpec((n_pair, 2 * SGU_LEN, SGU_LEN), const3),
            pl.BlockSpec((SGU_LEN, SGU_W), const2),
        ],
        out_specs=[
            pl.BlockSpec((tm, SGU_W), lambda i: (i, 0)),
            pl.BlockSpec((tm, SGU_W), lambda i: (i, 0)),
            pl.BlockSpec((n_pair, 2 * SGU_LEN, SGU_LEN), const3),
            pl.BlockSpec((SGU_LEN, SGU_W), const2),
            pl.BlockSpec((1, SGU_W), const2),
            pl.BlockSpec((1, SGU_W), const2),
            pl.BlockSpec((2 * SGU_W, tm), lambda i: (0, i)),
        ],
        out_shape=[
            _sds((s_len, SGU_W), BF16), _sds((s_len, SGU_W), BF16), _sds((n_pair, 2 * SGU_LEN, SGU_LEN), F32),
            _sds((SGU_LEN, SGU_W), F32), _sds((1, SGU_W), F32), _sds((1, SGU_W), F32),
            _sds((2 * SGU_W, s_len), BF16),
        ],
        scratch_shapes=[pltpu.VMEM((tm, SGU_W), F32)],
        compiler_params=_params(("arbitrary",)),
    )(prest, prest, dyb, w_b_t, ln_g, ln_b, w_stack, wt_stack, b_pair)


def _attn_bwd(qkv, k_t, do, c_col, lse_row, delta_row, tq, tk, comm=()):
    s_len = qkv.shape[0]
    nq, nk = s_len // tq, s_len // tk
    ratio = tq // tk
    scale = HEAD_DIM ** -0.5
    steps = [(j, i) for j in range(nk) for i in range(j // ratio, nq)]
    j_tab = jnp.asarray([j for j, _ in steps], jnp.int32)
    i_tab = jnp.asarray([i for _, i in steps], jnp.int32)

    comm_arrays, comm_outs, comm_sems = _comm_operands(comm)
    n_ci, n_co = len(comm_arrays), len(comm_outs)

    def body(j_ref, i_ref, q_ref, k_ref, v_ref, kt_ref, do_ref, ck_ref, lse_ref, dl_ref, *rest):
        comm_refs = rest[:n_ci], rest[n_ci + 6:n_ci + 6 + n_co], rest[n_ci + 6 + n_co + 8:]
        dq_ref, dk_ref, dv_ref, dcr_ref, dcc_ref, dkvt_ref = rest[n_ci:n_ci + 6]
        dq_t, dk_acc, dv_acc, dcc_acc, st_sc, dpt_sc, p_sc, ds_sc = rest[n_ci + 6 + n_co:n_ci + 6 + n_co + 8]
        n = pl.program_id(0)
        j, i = j_ref[n], i_ref[n]

        @pl.when(n == 0)
        def _():
            _comm_phase(comm, "start", *comm_refs)
            dq_t[...] = jnp.zeros_like(dq_t)
            dcr_ref[...] = jnp.zeros_like(dcr_ref)

        @pl.when(i == j // ratio)
        def _():
            dk_acc[...] = jnp.zeros_like(dk_acc)
            dv_acc[...] = jnp.zeros_like(dv_acc)
            dcc_acc[...] = jnp.zeros_like(dcc_acc)

        lt64 = _lane_lt64((tk, LANES))
        _score_tiles(q_ref, k_ref, ck_ref, st_sc, tk)
        for p in range(N_HEADS // 2):
            lanes = slice(p * LANES, (p + 1) * LANES)
            dpt2 = _dot_nt(_stack_heads(v_ref[:, lanes], lt64), do_ref[:, lanes].astype(BF16))
            dpt_sc[2 * p] = dpt2[:tk]
            dpt_sc[2 * p + 1] = dpt2[tk:]
        _mask_diagonal(st_sc, i, j, tq, tk)

        pt = jnp.exp(st_sc[...] - lse_ref[...])
        dst = pt * (dpt_sc[...] - dl_ref[...])
        p_sc[...] = pt.astype(BF16)
        ds_sc[...] = dst.astype(BF16)
        dcr_ref[i] += jnp.sum(dst, axis=1, keepdims=True)
        col_sums = jnp.sum(dst, axis=2, keepdims=True)
        lane = lax.broadcasted_iota(jnp.int32, (tk, LANES), 1)
        dcc = jnp.zeros((tk, LANES), F32)
        for h in range(N_HEADS):
            dcc = jnp.where(lane == h, -col_sums[h], dcc)
        dcc_acc[...] += dcc

        for p in range(N_HEADS // 2):
            lanes = slice(p * LANES, (p + 1) * LANES)
            q_pair = q_ref[:, lanes] * jnp.asarray(scale, BF16)
            dv2 = _dot(p_sc[2 * p:2 * p + 2].reshape(2 * tk, tq), do_ref[:, lanes].astype(BF16))
            dv_acc[:, lanes] += jnp.where(lt64, dv2[:tk], dv2[tk:])
            dk2 = _dot(ds_sc[2 * p:2 * p + 2].reshape(2 * tk, tq), q_pair)
            dk_acc[:, lanes] += jnp.where(lt64, dk2[:tk], dk2[tk:])
            dq2 = _dot(kt_ref[lanes, :], jnp.concatenate([ds_sc[2 * p], ds_sc[2 * p + 1]], axis=1))
            top = lax.broadcasted_iota(jnp.int32, (LANES, tq), 0) < HEAD_DIM
            dq_t[i, lanes, :] += jnp.where(top, dq2[:, :tq], dq2[:, tq:])

        @pl.when(j == (i + 1) * ratio - 1)
        def _():
            rows = pl.ds(pl.multiple_of(i * tq, tq), tq)
            for p in range(N_HEADS // 2):
                lanes = slice(p * LANES, (p + 1) * LANES)
                dq_ref[rows, lanes] = (jnp.transpose(dq_t[i, lanes, :]) * scale).astype(BF16)

        @pl.when(i == nq - 1)
        def _():
            dk16, dv16 = dk_acc[...].astype(BF16), dv_acc[...].astype(BF16)
            dk_ref[...] = dk16
            dv_ref[...] = dv16
            dkvt_ref[:FOX_W, :] = jnp.transpose(dk16)
            dkvt_ref[FOX_W:, :] = jnp.transpose(dv16)
            dcc_ref[...] = dcc_acc[...]

        if comm:
            @pl.when(n == len(steps) // 2)
            def _():
                _comm_phase(comm, "mid", *comm_refs)

            @pl.when(n == len(steps) - 1)
            def _():
                _comm_phase(comm, "finish", *comm_refs)

    q_map = lambda n, jt, it: (it[n], 0)
    q_stat = lambda n, jt, it: (0, 0, it[n])
    k_map = lambda n, jt, it: (jt[n], 0)
    tile = (N_HEADS, tk, tq)
    res = pl.pallas_call(
        body,
        name="attn_bwd",
        grid_spec=pltpu.PrefetchScalarGridSpec(
            num_scalar_prefetch=2,
            grid=(len(steps),),
            in_specs=[
                pl.BlockSpec((tq, FOX_W), q_map),
                pl.BlockSpec((tk, FOX_W), lambda n, jt, it: (jt[n], 1)),
                pl.BlockSpec((tk, FOX_W), lambda n, jt, it: (jt[n], 2)),
                pl.BlockSpec((FOX_W, tk), lambda n, jt, it: (0, jt[n])),
                pl.BlockSpec((tq, FOX_W), q_map),
                pl.BlockSpec((tk, LANES), k_map),
                pl.BlockSpec((N_HEADS, 1, tq), q_stat),
                pl.BlockSpec((N_HEADS, 1, tq), q_stat),
            ] + [_ANY] * n_ci,
            out_specs=[
                pl.BlockSpec((s_len, FOX_W), lambda n, jt, it: (0, 0)),
                pl.BlockSpec((tk, FOX_W), k_map),
                pl.BlockSpec((tk, FOX_W), k_map),
                pl.BlockSpec((nq, N_HEADS, 1, tq), lambda n, jt, it: (0, 0, 0, 0)),
                pl.BlockSpec((tk, LANES), k_map),
                pl.BlockSpec((2 * FOX_W, tk), lambda n, jt, it: (0, jt[n])),
            ] + [_ANY] * n_co,
            scratch_shapes=[pltpu.VMEM((nq, FOX_W, tq), F32), pltpu.VMEM((tk, FOX_W), F32), pltpu.VMEM((tk, FOX_W), F32),
                            pltpu.VMEM((tk, LANES), F32), pltpu.VMEM(tile, F32), pltpu.VMEM(tile, F32),
                            pltpu.VMEM(tile, BF16), pltpu.VMEM(tile, BF16)] + comm_sems,
        ),
        out_shape=[_sds((s_len, FOX_W), BF16), _sds((s_len, FOX_W), BF16), _sds((s_len, FOX_W), BF16),
                   _sds((nq, N_HEADS, 1, tq), F32), _sds((s_len, LANES), F32),
                   _sds((2 * FOX_W, s_len), BF16)] + comm_outs,
        compiler_params=_params(("arbitrary",)),
    )(j_tab, i_tab, qkv, qkv, qkv, k_t, do, c_col, lse_row, delta_row, *comm_arrays)
    return res[:6], res[6:]


def _forget_bwd(dc_rows, dc_cols, prest, b_f_pad, tc):
    s_len = dc_rows.shape[0]
    nb = s_len // tc

    def body(dcr_ref, dc_ref, f_ref, b_ref, df_ref, db_ref, dft_ref, carry):
        @pl.when(pl.program_id(0) == 0)
        def _():
            carry[...] = jnp.zeros_like(carry)
            db_ref[...] = jnp.zeros_like(db_ref)

        row = lax.broadcasted_iota(jnp.int32, (tc, tc), 0)
        col = lax.broadcasted_iota(jnp.int32, (tc, tc), 1)
        tri = (row <= col).astype(F32)
        dlogf = _dot_f32(tri, dcr_ref[...] + dc_ref[...], "a") + carry[...]
        carry[...] = dlogf[0:1, :]
        z = f_ref[...] + b_ref[...]
        lane = lax.broadcasted_iota(jnp.int32, (tc, LANES), 1)
        dz = jnp.where(lane < N_HEADS, dlogf * _sigmoid(-z), 0.0)
        df_ref[...] = dz.astype(BF16)
        dft_ref[...] = jnp.transpose(dz).astype(BF16)
        db_ref[...] += jnp.sum(dz, axis=0, keepdims=True)

    rev = lambda i: (nb - 1 - i, 0)
    return pl.pallas_call(
        body,
        name="forget_bwd",
        grid=(nb,),
        in_specs=[
            pl.BlockSpec((tc, LANES), rev),
            pl.BlockSpec((tc, LANES), rev),
            pl.BlockSpec((tc, LANES), rev),
            pl.BlockSpec((1, LANES), lambda i: (0, 0)),
        ],
        out_specs=[pl.BlockSpec((tc, LANES), rev), pl.BlockSpec((1, LANES), lambda i: (0, 0)),
                   pl.BlockSpec((LANES, tc), lambda i: (0, nb - 1 - i))],
        out_shape=[_sds((s_len, LANES), BF16), _sds((1, LANES), F32), _sds((LANES, s_len), BF16)],
        scratch_shapes=[pltpu.VMEM((1, LANES), F32)],
        compiler_params=_params(("arbitrary",)),
    )(dc_rows, dc_cols, prest, b_f_pad)


def _pair_sum(name, g4, recv, idx, tr):
    _, _, r, c = g4.shape

    def body(idx_ref, g_ref, r_ref, p16_ref, own_ref):
        k = pl.program_id(1)
        s = g_ref[...] + r_ref[...]
        p16_ref[...] = s.astype(BF16)

        @pl.when(k == idx_ref[1])
        def _():
            own_ref[...] = s

    return pl.pallas_call(
        body,
        name=name,
        grid_spec=pltpu.PrefetchScalarGridSpec(
            num_scalar_prefetch=1,
            grid=(r // tr, 4),
            in_specs=[
                pl.BlockSpec((None, None, tr, c), lambda i, k, idx: (k, idx[0], i, 0)),
                pl.BlockSpec((None, tr, c), lambda i, k, idx: (k, i, 0)),
            ],
            out_specs=[
                pl.BlockSpec((None, tr, c), lambda i, k, idx: (k, i, 0)),
                pl.BlockSpec((tr, c), lambda i, k, idx: (i, 0)),
            ],
        ),
        out_shape=[_sds((4, r, c), BF16), _sds((r, c), F32)],
        compiler_params=_params(("parallel", "arbitrary")),
    )(idx, g4, recv)


def _adamw_math(w, g, m, v):
    m2 = ADAM_B1 * m + (1.0 - ADAM_B1) * g
    v2 = ADAM_B2 * v + (1.0 - ADAM_B2) * (g * g)
    m_hat = m2 / (1.0 - ADAM_B1 ** ADAM_STEP)
    v_hat = v2 / (1.0 - ADAM_B2 ** ADAM_STEP)
    delta = -ADAM_LR * (m_hat / (jnp.sqrt(v_hat) + ADAM_EPS) + ADAM_WD * w)
    return delta, m2, v2


def _adamw_shard(name, own, recv, w, m, v, tr):
    r, c = own.shape

    def body(own_ref, recv_ref, w_ref, m_ref, v_ref, g_ref, d_ref, m2_ref, v2_ref):
        g = own_ref[...]
        for k in range(3):
            g = g + recv_ref[k].astype(F32)
        delta, m2, v2 = _adamw_math(w_ref[...], g, m_ref[...], v_ref[...])
        g_ref[...] = g
        d_ref[...] = delta
        m2_ref[...] = m2
        v2_ref[...] = v2

    spec = pl.BlockSpec((tr, c), lambda i: (i, 0))
    return pl.pallas_call(
        body,
        name=name,
        grid=(r // tr,),
        in_specs=[spec, pl.BlockSpec((3, tr, c), lambda i: (0, i, 0)), spec, spec, spec],
        out_specs=[spec] * 4,
        out_shape=[_sds((r, c), F32)] * 4,
        compiler_params=_params(("parallel",)),
    )(own, recv, w, m, v)


def _adamw_small(name, gathered, first_row, w, m, v):
    r = w.shape[0]
    assert first_row % r == 0

    def body(ga_ref, w_ref, m_ref, v_ref, g_ref, d_ref, m2_ref, v2_ref):
        g = ga_ref[0]
        for k in range(1, N_DEV):
            g = g + ga_ref[k]
        delta, m2, v2 = _adamw_math(w_ref[...], g, m_ref[...], v_ref[...])
        g_ref[...] = g
        d_ref[...] = delta
        m2_ref[...] = m2
        v2_ref[...] = v2

    spec = pl.BlockSpec((r, LANES), lambda i: (0, 0))
    return pl.pallas_call(
        body,
        name=name,
        grid=(1,),
        in_specs=[pl.BlockSpec((N_DEV, r, LANES), lambda i: (0, first_row // r, 0)), spec, spec, spec],
        out_specs=[spec] * 4,
        out_shape=[_sds((r, LANES), F32)] * 4,
        compiler_params=_params(("arbitrary",)),
    )(gathered, w, m, v)


_TINY_EARLY = (("b_sgu", (1, SGU_G, SGU_LEN)), ("norm2_g", (1, D_MODEL)), ("normf_g", (D_MODEL,)),
               ("ln_v_g", (1, SGU_W)), ("ln_v_b", (1, SGU_W)))
_TINY_LATE = (("b_f", (1, N_HEADS)), ("norm1_g", (1, D_MODEL)), ("loss", ()))


def _pack_rows(values):
    rows = []
    for val in values:
        flat = val.reshape(-1).astype(F32)
        pad = (-flat.shape[0]) % LANES
        rows.append(jnp.pad(flat, (0, pad)).reshape(-1, LANES))
    packed = jnp.concatenate(rows, axis=0)
    return jnp.pad(packed, ((0, (-packed.shape[0]) % 8), (0, 0)))


def _unpack_rows(packed, group):
    out, row = {}, 0
    for name, shape in group:
        size = math.prod(shape)
        n_rows = -(-size // LANES)
        out[name] = packed[row:row + n_rows].reshape(-1)[:size].reshape(shape)
        row += n_rows
    return out


def kernel(x, norm1_g, w_in, b_f, ln_v_g, ln_v_b, w_sgu, b_sgu, w_a, w_b, w_o, norm2_g, w_up, w_down, normf_g, loss_target, m_norm1_g, m_w_in, m_b_f, m_ln_v_g, m_ln_v_b, m_w_sgu, m_b_sgu, m_w_a, m_w_b, m_w_o, m_norm2_g, m_w_up, m_w_down, m_normf_g, v_norm1_g, v_w_in, v_b_f, v_ln_v_g, v_ln_v_b, v_w_sgu, v_b_sgu, v_w_a, v_w_b, v_w_o, v_norm2_g, v_w_up, v_w_down, v_normf_g):
    xs = x[0]
    target = loss_target[0]
    s_len, d = xs.shape
    tm = min(512, s_len)
    tl = min(1024, s_len)
    tr = min(512, s_len)
    ta = min(512, s_len)
    tc = min(512, s_len)

    w_in_t = jnp.transpose(w_in[0])
    lin = (IN_SHARD * d // LANES, LANES)
    big = (w_in_t.reshape(lin), w_a[0], w_b[0], w_o[0], w_up[0], w_down[0])
    h, h_t, r1, (w_in_g,) = _rms_fwd("rms1", xs, norm1_g, tm, comm=[_gather_plan(w_in_t.astype(BF16))])
    w_in_f = w_in_g.reshape(IN_COLS, d)
    later_shards = [jnp.transpose(w_a[0]), jnp.transpose(w_b[0]), w_o[0], jnp.transpose(w_up[0]), w_down[0]]
    later_plans = [_gather_plan(w.astype(BF16)) for w in later_shards]

    def unflatten(gathered):
        return [g.reshape(N_DEV * g.shape[1], g.shape[2]) for g in gathered]

    w_qkv = w_in_f[:QKV_W]
    f_lo = QKV_W
    u_lo = f_lo + N_HEADS
    w_rest = jnp.concatenate([w_in_f[u_lo:], jnp.pad(w_in_f[f_lo:u_lo], ((0, LANES - N_HEADS), (0, 0)))], axis=0)

    chunk_id = jnp.arange(SGU_LEN) // CHUNK
    sgu_mask = chunk_id[None, :] <= chunk_id[:, None]
    w_masked = jnp.where(sgu_mask[None], w_sgu[0], 0.0)
    w_stack = w_masked.reshape(SGU_G // 2, 2 * SGU_LEN, SGU_LEN).astype(BF16)
    wt_stack = jnp.transpose(w_masked, (0, 2, 1)).reshape(SGU_G // 2, 2 * SGU_LEN, SGU_LEN).astype(BF16)
    b_pair = jnp.transpose(jnp.repeat(b_sgu[0], SGU_W // SGU_G, axis=0))
    b_f_pad = jnp.pad(b_f, ((0, 0), (0, LANES - N_HEADS)))
    head_sel = (jnp.arange(FOX_W)[:, None] // HEAD_DIM == jnp.arange(LANES)[None, :]).astype(F32)

    def qkv_epi(accs, ex, out):
        tile = accs[0].astype(BF16)
        out[0][...] = tile
        out[1][...] = jnp.transpose(tile[:, FOX_W:2 * FOX_W])
        out[2][...] = jnp.transpose(tile[:, 2 * FOX_W:])

    t_spec = pl.BlockSpec((FOX_W, tm), lambda i, j: (0, i))
    qkv, k_t, v_t = _mm("proj_qkv", [(h, w_qkv, True, None)], [],
                        [(_sds((s_len, QKV_W), BF16), _tile(tm, QKV_W)), (_sds((FOX_W, s_len), BF16), t_spec),
                         (_sds((FOX_W, s_len), BF16), t_spec)],
                        qkv_epi, m=s_len, tm=tm, n=QKV_W, tn=QKV_W)
    def rest_epi(accs, ex, out):
        out[0][...] = accs[0].astype(BF16)
        out[1][...] = accs[0][:, F_OFF:F_OFF + LANES]

    prest, f_logit = _mm("proj_rest", [(h, w_rest, True, None)], [],
                         [(_sds((s_len, REST_W), BF16), _tile(tm, REST_W)), (_sds((s_len, LANES), F32), _row(tm, LANES))],
                         rest_epi, m=s_len, tm=tm, n=REST_W, tn=REST_W)

    c_col = _forget_cumsum(f_logit, b_f_pad, tc)
    o, o_t, lse_row, later_g = _attn_fwd(qkv, v_t, c_col, ta, ta, comm=later_plans)
    w_a_t, w_b_t, w_o_f, w_up_t, w_down_f = unflatten(later_g)
    sg, sg_t = _sgu_fwd(prest, ln_v_g, ln_v_b, w_stack, b_pair, tm)

    def merge_epi(accs, ex, out):
        ya, yb = accs
        sa, sb = _sigmoid(ex[0][...].astype(F32)), _sigmoid(ex[1][...].astype(F32))
        merged = (sa * ya + sb * yb).astype(BF16)
        out[0][...] = merged
        out[1][...] = ya.astype(BF16)
        out[2][...] = yb.astype(BF16)
        out[3][...] = jnp.transpose(merged)

    merged, ya, yb, merged_t = _mm(
        "merge", [(o, w_a_t, True, None), (sg, w_b_t, True, None)],
        [(prest, _tile(tm, d, GA_OFF // d)), (prest, _tile(tm, d, GB_OFF // d))],
        [(_sds((s_len, d), BF16), _tile(tm, d))] * 3 + [(_sds((d, s_len), BF16), _tile_t(tm, d))],
        merge_epi, m=s_len, tm=tm, n=d, tn=d)

    def resid_epi(accs, ex, out):
        x1v = ex[0][...] + accs[0]
        out[0][...] = x1v
        r = lax.rsqrt(jnp.mean(x1v * x1v, axis=-1, keepdims=True) + EPS)
        h2v = (x1v * r * ex[1][...]).astype(BF16)
        out[1][...] = h2v
        out[2][...] = jnp.transpose(h2v)
        out[3][...] = r

    x1, h2, h2_t, r2 = _mm(
        "out_proj", [(merged, w_o_f, False, None)], [(xs, _tile(tm, d)), (norm2_g, _whole((1, d)))],
        [(_sds((s_len, d), F32), _tile(tm, d)), (_sds((s_len, d), BF16), _tile(tm, d)),
         (_sds((d, s_len), BF16), _tile_t(tm, d)), (_sds((s_len, 1), F32), _row(tm, 1))],
        resid_epi, m=s_len, tm=tm, n=d, tn=d)

    def up_epi(accs, ex, out):
        act = jnp.square(jnp.maximum(accs[0], 0.0)).astype(BF16)
        out[0][...] = act
        out[1][...] = jnp.transpose(act)

    act, act_t = _mm(
        "mlp_up", [(h2, w_up_t, True, None)], [],
        [(_sds((s_len, D_FF), BF16), _tile(tl, 1024)), (_sds((D_FF, s_len), BF16), _tile_t(tl, 1024))],
        up_epi, m=s_len, tm=tl, n=D_FF, tn=1024)

    def first_step():
        return jnp.logical_and(pl.program_id(0) == 0, pl.program_id(1) == 0)

    def accumulate(ref, val):
        @pl.when(first_step())
        def _():
            ref[...] = val

        @pl.when(jnp.logical_not(first_step()))
        def _():
            ref[...] += val

    def final_epi(accs, ex, out):
        x1_ref, t_ref, g_ref = ex
        x2 = x1_ref[...] + accs[0]
        rf = lax.rsqrt(jnp.mean(x2 * x2, axis=-1, keepdims=True) + EPS)
        xh = x2 * rf
        gf = g_ref[...]
        err = xh * gf - t_ref[...]
        dy = err * (1.0 / d)
        dx2 = _rms_bwd(xh, rf, gf, dy)
        out[0][...] = dx2
        accumulate(out[1], jnp.sum(dy * xh, axis=0, keepdims=True))
        part = 0.5 * jnp.sum(jnp.sum(err * err, axis=-1, keepdims=True) * (1.0 / d), axis=0, keepdims=True)
        accumulate(out[2], jnp.broadcast_to(part, (1, LANES)))
        out[3][...] = dx2.astype(BF16)

    gf2 = normf_g.reshape(1, d)
    dx2, g_normf, loss_part, dx2_16 = _mm(
        "mlp_down_loss", [(act, w_down_f, False, None)],
        [(x1, _row(tr, d)), (target, _row(tr, d)), (gf2, _whole((1, d)))],
        [(_sds((s_len, d), F32), _row(tr, d)), (_sds((1, d), F32), _whole((1, d))), (_sds((1, LANES), F32), _whole((1, LANES))),
         (_sds((s_len, d), BF16), _row(tr, d))],
        final_epi, m=s_len, tm=tr, n=d, tn=d, arbitrary=True)

    def dact_epi(accs, ex, out):
        out[0][...] = (accs[0] * (2.0 * jnp.sqrt(ex[0][...].astype(F32)))).astype(BF16)

    (da,) = _mm("mlp_down_bwd", [(dx2_16, w_down_f, True, None)], [(act, _tile(tl, 1024))],
                [(_sds((s_len, D_FF), BF16), _tile(tl, 1024))], dact_epi, m=s_len, tm=tl, n=D_FF, tn=1024)
    g_down = _grad_w("grad_w_down", act_t, dx2_16, tk=1024, tn=d, ts=tl)
    g_up = _grad_w("grad_w_up", h2_t, da, tk=d, tn=1024, ts=tl, block_cols=D_FF // N_DEV)

    def dh2_epi(accs, ex, out):
        x1_ref, r_ref, g_ref, dx2_ref = ex
        r = r_ref[...]
        xh = x1_ref[...] * r
        dh2 = accs[0]
        out[0][...] = dx2_ref[...] + _rms_bwd(xh, r, g_ref[...], dh2)
        accumulate(out[1], jnp.sum(dh2 * xh, axis=0, keepdims=True))

    my_c = lax.axis_index("c")
    my_chip = 2 * lax.axis_index("x") + lax.axis_index("y")
    idx = jnp.stack([my_c, my_chip]).astype(jnp.int32)
    parts16, owns = {}, {}

    def split_cores(g8):
        return g8.reshape((4, 2) + g8.shape[1:])

    def row_tile(r):
        return 512 if r % 512 == 0 else r

    def pair_sums(names, grads4, from_sibling):
        for name, g4, recv in zip(names, grads4, from_sibling):
            parts16[name], owns[name] = _pair_sum("grad_pair_sum_" + name, g4, recv, idx, row_tile(g4.shape[2]))

    grads4_mlp = [split_cores(g_up), split_cores(g_down.reshape(N_DEV, D_FF // N_DEV, d))]
    (dx1, g_norm2), from_sibling = _mm(
        "mlp_up_bwd", [(da, w_up_t, False, None)],
        [(x1, _row(tr, d)), (r2, _row(tr, 1)), (norm2_g, _whole((1, d))), (dx2, _row(tr, d))],
        [(_sds((s_len, d), F32), _row(tr, d)), (_sds((1, d), F32), _whole((1, d)))],
        dh2_epi, m=s_len, tm=tr, n=d, tn=d, arbitrary=True, comm=[_pair_exchange_plan(grads4_mlp)])
    pair_sums(("w_up", "w_down"), grads4_mlp, from_sibling)

    def dmerge_epi(accs, ex, out):
        dm = accs[0]
        sa, sb = _sigmoid(ex[0][...].astype(F32)), _sigmoid(ex[1][...].astype(F32))
        out[0][...] = (dm * sa).astype(BF16)
        out[1][...] = (dm * sb).astype(BF16)
        dga = (dm * ex[2][...] * sa * (1.0 - sa)).astype(BF16)
        dgb = (dm * ex[3][...] * sb * (1.0 - sb)).astype(BF16)
        out[2][...] = dga
        out[3][...] = dgb
        out[4][:d, :] = jnp.transpose(dga)
        out[4][d:, :] = jnp.transpose(dgb)

    dya, dyb, dga, dgb, dg_t = _mm(
        "out_proj_bwd", [(dx1, w_o_f, True, None)],
        [(prest, _tile(tm, d, GA_OFF // d)), (prest, _tile(tm, d, GB_OFF // d)), (ya, _tile(tm, d)), (yb, _tile(tm, d))],
        [(_sds((s_len, d), BF16), _tile(tm, d))] * 4
        + [(_sds((2 * d, s_len), BF16), pl.BlockSpec((2 * d, tm), lambda i, j: (0, i)))],
        dmerge_epi, m=s_len, tm=tm, n=d, tn=d)
    g_o = _grad_w("grad_w_o", merged_t, dx1, tk=d, tn=d, ts=tl).reshape(N_DEV, d // N_DEV, d)
    def col_blocks(g):
        return jnp.transpose(g.reshape(g.shape[0], N_DEV, g.shape[1] // N_DEV), (1, 0, 2))

    g_a = col_blocks(_grad_w("grad_w_a", o_t, dya, tk=FOX_W, tn=d, ts=tl))
    g_b = col_blocks(_grad_w("grad_w_b", sg_t, dyb, tk=SGU_W, tn=d, ts=tl))

    def do_epi(accs, ex, out):
        do = accs[0]
        out[0][...] = do
        out[1][...] = _dot_f32(do * ex[0][...], ex[1][...], "b")

    grads4_mix = [split_cores(g) for g in (g_a, g_b, g_o)]
    (do, delta), from_sibling = _mm(
        "attn_out_bwd", [(dya, w_a_t, False, None)], [(o, _row(tm, FOX_W)), (head_sel, _whole((FOX_W, LANES)))],
        [(_sds((s_len, FOX_W), F32), _row(tm, FOX_W)), (_sds((s_len, LANES), F32), _row(tm, LANES))],
        do_epi, m=s_len, tm=tm, n=FOX_W, tn=FOX_W, comm=[_pair_exchange_plan(grads4_mix)])
    pair_sums(("w_a", "w_b", "w_o"), grads4_mix, from_sibling)

    du, dsv, dw_pairs, db_pos, g_ln_g, g_ln_b, dusv_t = _sgu_bwd(
        prest, dyb, w_b_t, ln_v_g, ln_v_b, w_stack, wt_stack, b_pair, tm)
    g_w_sgu = jnp.where(sgu_mask[None], dw_pairs.reshape(SGU_G, SGU_LEN, SGU_LEN), 0.0)
    g_b_sgu = jnp.transpose(jnp.sum(db_pos.reshape(SGU_LEN, SGU_G, SGU_W // SGU_G), axis=-1))

    delta_row = jnp.transpose(delta[:, :N_HEADS]).reshape(N_HEADS, 1, s_len)
    early = ("w_a", "w_b", "w_o", "w_up", "w_down")
    small_early = _pack_rows((g_w_sgu, g_b_sgu, g_norm2, g_normf, g_ln_g, g_ln_b))
    (dq, dk, dv, dc_rows_blk, dc_cols, dkv_t), (small_early_all, *from_chips_early) = _attn_bwd(
        qkv, k_t, do, c_col, lse_row, delta_row, ta, ta,
        comm=[_gather_plan(small_early), _chip_exchange_plan([parts16[n] for n in early])])
    dc_rows = jnp.transpose(dc_rows_blk.reshape(s_len // ta, N_HEADS, ta), (0, 2, 1)).reshape(s_len, N_HEADS)
    dc_rows = jnp.pad(dc_rows, ((0, 0), (0, LANES - N_HEADS)))
    dfl, g_bf, dfl_t = _forget_bwd(dc_rows, dc_cols, f_logit, b_f_pad, tc)

    dp_t = (jnp.transpose(dq), dkv_t, dfl_t, dusv_t, dg_t)
    g_in_rows = [_grad_w("grad_w_in_%d" % k, seg_t, h, tk=min(seg_t.shape[0], 1024), tn=d, ts=tl)
                 for k, seg_t in enumerate(dp_t)]
    g_in_rows[2] = g_in_rows[2][:N_HEADS]
    g_in = jnp.concatenate(g_in_rows, axis=0).reshape((N_DEV,) + lin)

    def dx_epi(accs, ex, out):
        x_ref, r_ref, g_ref, dx1_ref = ex
        dh = accs[0]
        for extra in accs[1:]:
            dh = dh + extra
        r = r_ref[...]
        xh = x_ref[...] * r
        out[0][...] = dx1_ref[...] + _rms_bwd(xh, r, g_ref[...], dh)
        accumulate(out[1], jnp.sum(dh * xh, axis=0, keepdims=True))

    rest_cols = ((du, U_OFF, 512), (dsv, SV_OFF, 512), (dga, GA_OFF, 1024), (dgb, GB_OFF, 1024), (dfl, F_OFF, LANES))
    dx_pairs = [(seg, w_qkv, False, (512 * k, 512 * (k + 1))) for k, seg in enumerate((dq, dk, dv))]
    dx_pairs += [(seg, w_rest, False, (lo, lo + width)) for seg, lo, width in rest_cols]
    grads4_in = [split_cores(g_in)]
    pair_sums(("w_in",), grads4_in, _run_comm("grad_pair_exchange_w_in", [_pair_exchange_plan(grads4_in)]))
    (grad_x, g_norm1), (from_chips_in,) = _mm(
        "proj_bwd", dx_pairs,
        [(xs, _row(tr, d)), (r1, _row(tr, 1)), (norm1_g, _whole((1, d))), (dx1, _row(tr, d))],
        [(_sds((s_len, d), F32), _row(tr, d)), (_sds((1, d), F32), _whole((1, d)))],
        dx_epi, m=s_len, tm=tr, n=d, tn=d, arbitrary=True, comm=[_chip_exchange_plan([parts16["w_in"]])])
    small_late = _pack_rows((g_bf[:, :N_HEADS], g_norm1, loss_part[0, 0]))
    (small_late_all,) = _run_comm("gather_last_grads", [_gather_plan(small_late)])
    from_chips = dict(zip(early, from_chips_early), w_in=from_chips_in)

    names = ("w_in", "w_a", "w_b", "w_o", "w_up", "w_down")
    moments_m = (m_w_in, m_w_a, m_w_b, m_w_o, m_w_up, m_w_down)
    moments_v = (v_w_in, v_w_a, v_w_b, v_w_o, v_w_up, v_w_down)
    big_out = {}
    for name, w, m, v in zip(names, big, moments_m, moments_v):
        own = owns[name]
        transposed = name == "w_in"
        m0, v0 = (jnp.transpose(m[0]).reshape(lin), jnp.transpose(v[0]).reshape(lin)) if transposed else (m[0], v[0])
        res = _adamw_shard("adamw_" + name, own, from_chips[name], w, m0, v0, row_tile(own.shape[0]))
        big_out[name] = [(jnp.transpose(t.reshape(IN_SHARD, d)) if transposed else t)[None] for t in res]

    zero = jnp.zeros((), F32)
    sgu_rows = (SGU_G * SGU_LEN, LANES)
    res_sgu = _adamw_small("adamw_w_sgu", small_early_all, 0, w_sgu.reshape(sgu_rows), m_w_sgu.reshape(sgu_rows),
                           v_w_sgu.reshape(sgu_rows))
    small_out = {"w_sgu": [t.reshape(w_sgu.shape) for t in res_sgu]}
    res_early = _adamw_small(
        "adamw_tiny_early", small_early_all, sgu_rows[0], _pack_rows((b_sgu, norm2_g, normf_g, ln_v_g, ln_v_b)),
        _pack_rows((m_b_sgu, m_norm2_g, m_normf_g, m_ln_v_g, m_ln_v_b)),
        _pack_rows((v_b_sgu, v_norm2_g, v_normf_g, v_ln_v_g, v_ln_v_b)))
    res_late = _adamw_small(
        "adamw_tiny_late", small_late_all, 0, _pack_rows((b_f, norm1_g, zero)), _pack_rows((m_b_f, m_norm1_g, zero)),
        _pack_rows((v_b_f, v_norm1_g, zero)))
    for res, group in ((res_early, _TINY_EARLY), (res_late, _TINY_LATE)):
        unpacked = [_unpack_rows(t, group) for t in res]
        small_out.update({name: [u[name] for u in unpacked] for name, _ in group})
    loss = small_out["loss"][0]

    order = ("norm1_g", "w_in", "b_f", "ln_v_g", "ln_v_b", "w_sgu", "b_sgu", "w_a", "w_b", "w_o", "norm2_g", "w_up",
             "w_down", "normf_g")
    table = {**big_out, **small_out}
    outs = [loss, grad_x[None]]
    for kind in range(4):
        outs += [table[n][kind] for n in order]
    return tuple(outs)
```

```python
import math

import jax
import jax.numpy as jnp
from jax import lax
from jax.experimental import pallas as pl
from jax.experimental.pallas import tpu as pltpu

F32 = jnp.float32
BF16 = jnp.bfloat16

N_DEV = 8
D_MODEL = 1024
N_HEADS = 8
HEAD_DIM = 64
FOX_W = N_HEADS * HEAD_DIM
SGU_G = 8
SGU_W = 512
SGU_LEN = 128
CHUNK = 64
D_FF = 4 * D_MODEL
IN_COLS = 3 * FOX_W + N_HEADS + 2 * SGU_W + 2 * D_MODEL
IN_SHARD = IN_COLS // N_DEV
LANES = 128
QKV_W = 3 * FOX_W
U_OFF, SV_OFF, GA_OFF, GB_OFF, F_OFF = 0, 512, 1024, 2048, 3072
REST_W = F_OFF + LANES
EPS = 1e-6
NEG = -1e30

ADAM_LR = 0.001
ADAM_B1 = 0.9
ADAM_B2 = 0.999
ADAM_EPS = 1e-08
ADAM_WD = 0.01
ADAM_STEP = 10

VMEM_LIMIT = 56 * 1024 * 1024
MESH = pl.DeviceIdType.MESH


def _params(sem=None):
    return pltpu.CompilerParams(dimension_semantics=sem, vmem_limit_bytes=VMEM_LIMIT)


def _dot(a, b):
    return jnp.dot(a, b, preferred_element_type=F32)


def _dot_nt(a, b):
    return lax.dot_general(a, b, (((1,), (1,)), ((), ())), preferred_element_type=F32)


def _split3(x):
    hi = x.astype(BF16)
    rest = x - hi.astype(F32)
    mid = rest.astype(BF16)
    return hi, mid, (rest - mid.astype(F32)).astype(BF16)


def _dot_f32(a, b, exact):
    if exact == "a":
        a16 = a.astype(BF16)
        return sum(_dot(a16, part) for part in _split3(b))
    b16 = b.astype(BF16)
    return sum(_dot(part, b16) for part in _split3(a))


def _sigmoid(x):
    return 1.0 / (1.0 + jnp.exp(-x))


def _log_sigmoid(z):
    return jnp.minimum(z, 0.0) - jnp.log(1.0 + jnp.exp(-jnp.abs(z)))


_GELU_K = math.sqrt(2.0 / math.pi)
_GELU_C = 0.044715


def _gelu(x):
    t = jnp.tanh(_GELU_K * (x + _GELU_C * (x * x * x)))
    return 0.5 * x * (1.0 + t)


def _gelu_grad(x):
    x2 = x * x
    t = jnp.tanh(_GELU_K * (x + _GELU_C * (x2 * x)))
    return 0.5 * (1.0 + t) + 0.5 * x * (1.0 - t * t) * (_GELU_K * (1.0 + 3.0 * _GELU_C * x2))


def _rms_bwd(xh, r, g, dy):
    gy = dy * g
    return r * (gy - xh * jnp.mean(xh * gy, axis=-1, keepdims=True))


def _lane_lt64(shape):
    return lax.broadcasted_iota(jnp.int32, shape, len(shape) - 1) < HEAD_DIM


class _Comm:
    def __init__(self, arrays, out_shapes, sems, start, finish, mid=None):
        self.arrays, self.out_shapes, self.sems = list(arrays), list(out_shapes), list(sems)
        self.start, self.mid, self.finish = start, mid, finish


def _comm_phase(plans, phase, in_refs, out_refs, sem_refs):
    ia = io = ks = 0
    for plan in plans:
        na, no, ns = len(plan.arrays), len(plan.out_shapes), len(plan.sems)
        fn = getattr(plan, phase)
        if fn is not None:
            fn(in_refs[ia:ia + na], out_refs[io:io + no], sem_refs[ks:ks + ns])
        ia, io, ks = ia + na, io + no, ks + ns


def _comm_operands(plans):
    arrays = [a for plan in plans for a in plan.arrays]
    out_shapes = [o for plan in plans for o in plan.out_shapes]
    sems = [s for plan in plans for s in plan.sems]
    return arrays, out_shapes, sems


_ANY = pl.BlockSpec(memory_space=pl.ANY)


def _run_comm(name, plans):
    arrays, out_shapes, sems = _comm_operands(plans)
    n_in, n_out = len(arrays), len(out_shapes)

    def body(*refs):
        parts = refs[:n_in], refs[n_in:n_in + n_out], refs[n_in + n_out:]
        for phase in ("start", "mid", "finish"):
            _comm_phase(plans, phase, *parts)

    return pl.pallas_call(
        body, name=name, out_shape=out_shapes, in_specs=[_ANY] * n_in, out_specs=[_ANY] * n_out, scratch_shapes=sems,
    )(*arrays)


def _gather_plan(shard):
    def setup(ins, outs, sems):
        (x_ref,), (out_ref,), (send_sems, recv_sems, local_sem) = ins, outs, sems
        x, y, c = lax.axis_index("x"), lax.axis_index("y"), lax.axis_index("c")
        me, sibling = (x, y, c), (x, y, 1 - c)
        chips = [(1 - x, y), (x, 1 - y), (1 - x, 1 - y)]

        def rows(px, py, pc):
            return out_ref.at[4 * px + 2 * py + pc]

        def copy(k, block, to, src=None):
            return pltpu.make_async_remote_copy(
                src_ref=rows(*block) if src is None else src,
                dst_ref=rows(*block),
                send_sem=send_sems.at[k],
                recv_sem=recv_sems.at[k],
                device_id=to,
                device_id_type=MESH,
            )

        mine = pltpu.make_async_copy(x_ref, rows(*me), local_sem)
        first = [copy(0, me, sibling, src=x_ref)]
        first += [copy(1 + j, me, (*chip, c), src=x_ref) for j, chip in enumerate(chips)]
        passed = [copy(4 + j, (*chip, c), sibling) for j, chip in enumerate(chips)]
        landed = [copy(1 + j, (*chip, c), me) for j, chip in enumerate(chips)]
        from_sibling = [copy(0, sibling, me)] + [copy(4 + j, (*chip, 1 - c), me) for j, chip in enumerate(chips)]
        return mine, first, passed, landed, from_sibling

    def start(ins, outs, sems):
        mine, first, _, _, _ = setup(ins, outs, sems)
        mine.start()
        for cp in first:
            cp.start()

    def mid(ins, outs, sems):
        _, _, passed, landed, _ = setup(ins, outs, sems)
        for arrived, onward in zip(landed, passed):
            arrived.wait_recv()
            onward.start()

    def finish(ins, outs, sems):
        mine, first, passed, _, from_sibling = setup(ins, outs, sems)
        for cp in from_sibling:
            cp.wait_recv()
        for cp in first + passed:
            cp.wait_send()
        mine.wait()

    return _Comm([shard], [jax.ShapeDtypeStruct((N_DEV,) + shard.shape, shard.dtype)],
                 [pltpu.SemaphoreType.DMA((7,)), pltpu.SemaphoreType.DMA((7,)), pltpu.SemaphoreType.DMA],
                 start, finish, mid)


def _start_all(copies):
    for cp in copies:
        cp.start()


def _wait_all(copies):
    for cp in copies:
        cp.wait_recv()
    for cp in copies:
        cp.wait_send()


def _pair_exchange_plan(grads):
    n = len(grads)

    def copies(ins, outs, sems):
        send_sems, recv_sems = sems
        x, y, c = lax.axis_index("x"), lax.axis_index("y"), lax.axis_index("c")
        return [
            pltpu.make_async_remote_copy(
                src_ref=ins[k].at[:, 1 - c],
                dst_ref=outs[k],
                send_sem=send_sems.at[k],
                recv_sem=recv_sems.at[k],
                device_id=(x, y, 1 - c),
                device_id_type=MESH,
            )
            for k in range(n)
        ]

    return _Comm(grads, [jax.ShapeDtypeStruct((4,) + g.shape[2:], g.dtype) for g in grads],
                 [pltpu.SemaphoreType.DMA((n,)), pltpu.SemaphoreType.DMA((n,))],
                 lambda *refs: _start_all(copies(*refs)), lambda *refs: _wait_all(copies(*refs)))


def _chip_exchange_plan(parts):
    n = len(parts)

    def copies(ins, outs, sems):
        send_sems, recv_sems = sems
        x, y, c = lax.axis_index("x"), lax.axis_index("y"), lax.axis_index("c")
        chips = [(1 - x, y), (x, 1 - y), (1 - x, 1 - y)]
        return [
            pltpu.make_async_remote_copy(
                src_ref=ins[k].at[2 * px + py],
                dst_ref=outs[k].at[j],
                send_sem=send_sems.at[3 * k + j],
                recv_sem=recv_sems.at[3 * k + j],
                device_id=(px, py, c),
                device_id_type=MESH,
            )
            for k in range(n) for j, (px, py) in enumerate(chips)
        ]

    return _Comm(parts, [jax.ShapeDtypeStruct((3,) + p.shape[1:], p.dtype) for p in parts],
                 [pltpu.SemaphoreType.DMA((3 * n,)), pltpu.SemaphoreType.DMA((3 * n,))],
                 lambda *refs: _start_all(copies(*refs)), lambda *refs: _wait_all(copies(*refs)))


def _mm(name, pairs, extras, outs, epi, *, m, tm, n, tn, arbitrary=False, comm=()):
    nj = n // tn
    a_arrays, a_specs, b_arrays, b_specs, b_index = [], [], [], [], []
    for a, b, nt, cols in pairs:
        a_arrays.append(a)
        a_specs.append(pl.BlockSpec((tm, a.shape[1]), lambda i, j: (i, 0)))
        known = [k for k, other in enumerate(b_arrays) if other is b]
        if known:
            b_index.append(known[0])
            continue
        b_index.append(len(b_arrays))
        b_arrays.append(b)
        if cols is not None:
            assert nj == 1
            b_specs.append(pl.BlockSpec(b.shape, lambda i, j: (0, 0)))
        elif nt:
            b_specs.append(pl.BlockSpec((tn, b.shape[1]), lambda i, j: (j, 0)))
        else:
            b_specs.append(pl.BlockSpec((b.shape[0], tn), lambda i, j: (0, j)))
    comm_arrays, comm_outs, comm_sems = _comm_operands(comm)
    arrays = a_arrays + b_arrays + [arr for arr, _ in extras] + comm_arrays
    in_specs = a_specs + b_specs + [spec for _, spec in extras] + [_ANY] * len(comm_arrays)
    n_a, n_b, n_extras, n_ci, n_out, n_co = len(a_arrays), len(b_arrays), len(extras), len(comm_arrays), len(outs), len(comm_outs)
    ni = m // tm

    def body(*refs):
        a_refs = refs[:n_a]
        b_refs = refs[n_a:n_a + n_b]
        ex = refs[n_a + n_b:n_a + n_b + n_extras]
        n_in = n_a + n_b + n_extras + n_ci
        comm_refs = refs[n_in - n_ci:n_in], refs[n_in + n_out:n_in + n_out + n_co], refs[n_in + n_out + n_co:]
        out = refs[n_in:n_in + n_out]
        if comm:
            @pl.when(jnp.logical_and(pl.program_id(0) == 0, pl.program_id(1) == 0))
            def _():
                _comm_phase(comm, "start", *comm_refs)

        accs = []
        for p, (_, _, nt, cols) in enumerate(pairs):
            av = a_refs[p][...]
            if av.dtype != BF16:
                av = av.astype(BF16)
            b_ref = b_refs[b_index[p]]
            if cols is None:
                bv = b_ref[...]
            else:
                bv = b_ref[:, cols[0]:cols[1]] if nt else b_ref[cols[0]:cols[1], :]
            accs.append(_dot_nt(av, bv) if nt else _dot(av, bv))
        epi(accs, ex, out)
        if comm:
            mid_row = ni // 2 if ni >= 3 else ni - 1
            mid_col = 0 if ni >= 3 else nj - 1

            @pl.when(jnp.logical_and(pl.program_id(0) == mid_row, pl.program_id(1) == mid_col))
            def _():
                _comm_phase(comm, "mid", *comm_refs)

            @pl.when(jnp.logical_and(pl.program_id(0) == ni - 1, pl.program_id(1) == nj - 1))
            def _():
                _comm_phase(comm, "finish", *comm_refs)

    sem = ("arbitrary", "arbitrary") if arbitrary or comm else ("parallel", "parallel")
    res = pl.pallas_call(
        body,
        name=name,
        grid=(ni, nj),
        in_specs=in_specs,
        out_specs=[spec for _, spec in outs] + [_ANY] * n_co,
        out_shape=[shape for shape, _ in outs] + comm_outs,
        scratch_shapes=comm_sems,
        compiler_params=_params(sem),
    )(*arrays)
    return (res[:n_out], res[n_out:]) if comm else res


def _tile(tm, tn, off=0):
    return pl.BlockSpec((tm, tn), lambda i, j: (i, j + off))


def _row(tm, w, blk=0):
    return pl.BlockSpec((tm, w), lambda i, j: (i, blk))


def _whole(shape):
    zeros = (0,) * len(shape)
    return pl.BlockSpec(shape, lambda i, j: zeros)


def _sds(shape, dtype):
    return jax.ShapeDtypeStruct(shape, dtype)


def _tile_t(tm, tn):
    return pl.BlockSpec((tn, tm), lambda i, j: (j, i))


def _grad_w(name, a_t, g, *, tk, tn, ts, block_cols=None):
    ka, s_len = a_t.shape
    n = g.shape[1]
    width = tn if block_cols is None else block_cols

    def body(a_ref, g_ref, o_ref):
        first = pl.program_id(2) == 0
        gv = g_ref[...].astype(BF16)
        for b in range(tn // width):
            part = _dot(a_ref[...], gv[:, b * width:(b + 1) * width])
            dst = o_ref if block_cols is None else o_ref.at[b]

            @pl.when(first)
            def _():
                dst[...] = part

            @pl.when(jnp.logical_not(first))
            def _():
                dst[...] += part

    if block_cols is None:
        out_shape = _sds((ka, n), F32)
        out_spec = pl.BlockSpec((tk, tn), lambda i, j, s: (i, j))
    else:
        out_shape = _sds((n // width, ka, width), F32)
        out_spec = pl.BlockSpec((tn // width, tk, width), lambda i, j, s: (j, i, 0))
    return pl.pallas_call(
        body,
        name=name,
        grid=(ka // tk, n // tn, s_len // ts),
        in_specs=[pl.BlockSpec((tk, ts), lambda i, j, s: (i, s)), pl.BlockSpec((ts, tn), lambda i, j, s: (s, j))],
        out_specs=out_spec,
        out_shape=out_shape,
        compiler_params=_params(("parallel", "parallel", "arbitrary")),
    )(a_t, g)


def _rms_fwd(name, x, g, tm, comm=()):
    s_len, d = x.shape
    steps = s_len // tm
    comm_arrays, comm_outs, comm_sems = _comm_operands(comm)
    n_ci, n_co = len(comm_arrays), len(comm_outs)

    def body(x_ref, g_ref, *rest):
        comm_refs = rest[:n_ci], rest[n_ci + 3:n_ci + 3 + n_co], rest[n_ci + 3 + n_co:]
        h_ref, ht_ref, r_ref = rest[n_ci:n_ci + 3]
        for phase, at in (("start", 0), ("mid", steps // 2)):
            if comm:
                @pl.when(pl.program_id(0) == at)
                def _():
                    _comm_phase(comm, phase, *comm_refs)

        xv = x_ref[...]
        r = lax.rsqrt(jnp.mean(xv * xv, axis=-1, keepdims=True) + EPS)
        h = (xv * r * g_ref[...]).astype(BF16)
        h_ref[...] = h
        ht_ref[...] = jnp.transpose(h)
        r_ref[...] = r
        if comm:
            @pl.when(pl.program_id(0) == steps - 1)
            def _():
                _comm_phase(comm, "finish", *comm_refs)

    res = pl.pallas_call(
        body,
        name=name,
        grid=(steps,),
        in_specs=[pl.BlockSpec((tm, d), lambda i: (i, 0)), pl.BlockSpec((1, d), lambda i: (0, 0))] + [_ANY] * n_ci,
        out_specs=[pl.BlockSpec((tm, d), lambda i: (i, 0)), pl.BlockSpec((d, tm), lambda i: (0, i)),
                   pl.BlockSpec((tm, 1), lambda i: (i, 0))] + [_ANY] * n_co,
        out_shape=[_sds((s_len, d), BF16), _sds((d, s_len), BF16), _sds((s_len, 1), F32)] + comm_outs,
        scratch_shapes=comm_sems,
        compiler_params=_params(("arbitrary",) if comm else ("parallel",)),
    )(x, g, *comm_arrays)
    return res[0], res[1], res[2], res[3:]


def _forget_cumsum(prest, b_f_pad, tc):
    s_len = prest.shape[0]

    def body(f_ref, b_ref, c_ref, carry):
        @pl.when(pl.program_id(0) == 0)
        def _():
            carry[...] = jnp.zeros_like(carry)

        logf = _log_sigmoid(f_ref[...] + b_ref[...])
        row = lax.broadcasted_iota(jnp.int32, (tc, tc), 0)
        col = lax.broadcasted_iota(jnp.int32, (tc, tc), 1)
        tri = (row >= col).astype(F32)
        c = _dot_f32(tri, logf, "a") + carry[...]
        c_ref[...] = c
        carry[...] = c[tc - 1:tc, :]

    return pl.pallas_call(
        body,
        name="forget_cumsum",
        grid=(s_len // tc,),
        in_specs=[pl.BlockSpec((tc, LANES), lambda i: (i, 0)), pl.BlockSpec((1, LANES), lambda i: (0, 0))],
        out_specs=pl.BlockSpec((tc, LANES), lambda i: (i, 0)),
        out_shape=_sds((s_len, LANES), F32),
        scratch_shapes=[pltpu.VMEM((1, LANES), F32)],
        compiler_params=_params(("arbitrary",)),
    )(prest, b_f_pad)


def _stack_heads(pair, lt64):
    zero = jnp.zeros_like(pair)
    return jnp.concatenate([jnp.where(lt64, pair, zero), jnp.where(lt64, zero, pair)], axis=0)


def _score_tiles(q_ref, k_ref, ck_ref, st_sc, tk):
    lt64 = _lane_lt64((tk, LANES))
    for p in range(N_HEADS // 2):
        lanes = slice(p * LANES, (p + 1) * LANES)
        q_pair = q_ref[:, lanes] * jnp.asarray(HEAD_DIM ** -0.5, BF16)
        st2 = _dot_nt(_stack_heads(k_ref[:, lanes], lt64), q_pair)
        for half in range(2):
            h = 2 * p + half
            st_sc[h] = st2[half * tk:(half + 1) * tk] - ck_ref[:, h:h + 1]


ROW_CHUNK = 64


def _row_chunks(tk):
    rc = min(ROW_CHUNK, tk)
    return [slice(r, r + rc) for r in range(0, tk, rc)]


def _by_sublane(x):
    return x.reshape(x.shape[0] // 8, 8, x.shape[1])


def _softmax_update(st_sc, p_sc, m_sc, l_sc, tk, tq):
    alphas = []
    for h in range(N_HEADS):
        top8 = jnp.full((8, tq), NEG, F32)
        for rows in _row_chunks(tk):
            top8 = jnp.maximum(top8, jnp.max(_by_sublane(st_sc[h, rows, :]), axis=0))
        m_old = m_sc[h]
        m_new = jnp.maximum(m_old, jnp.max(top8, axis=0, keepdims=True))
        sum8 = jnp.zeros((8, tq), F32)
        for rows in _row_chunks(tk):
            pt = jnp.exp(st_sc[h, rows, :] - m_new)
            p_sc[h, rows, :] = pt.astype(BF16)
            sum8 = sum8 + jnp.sum(_by_sublane(pt), axis=0)
        alpha = jnp.exp(m_old - m_new)
        l_sc[h] = alpha * l_sc[h] + jnp.sum(sum8, axis=0, keepdims=True)
        m_sc[h] = m_new
        alphas.append(alpha)
    return alphas


def _mask_diagonal(st_sc, i, j, tq, tk):
    @pl.when((j + 1) * tk - 1 > i * tq)
    def _():
        key = j * tk + lax.broadcasted_iota(jnp.int32, (tk, tq), 0)
        query = i * tq + lax.broadcasted_iota(jnp.int32, (tk, tq), 1)
        st_sc[...] = jnp.where((query >= key)[None], st_sc[...], NEG)


def _attn_fwd(qkv, v_t, c_col, tq, tk, comm=()):
    s_len = qkv.shape[0]
    ratio = tq // tk
    steps = [(i, j) for i in range(s_len // tq) for j in range((i + 1) * ratio)]
    i_tab = jnp.asarray([i for i, _ in steps], jnp.int32)
    j_tab = jnp.asarray([j for _, j in steps], jnp.int32)

    comm_arrays, comm_outs, comm_sems = _comm_operands(comm)
    n_ci, n_co = len(comm_arrays), len(comm_outs)

    def body(i_ref, j_ref, q_ref, k_ref, vt_ref, ck_ref, *rest):
        comm_refs = rest[:n_ci], rest[n_ci + 3:n_ci + 3 + n_co], rest[n_ci + 3 + n_co + 5:]
        o_ref, ot_ref, lse_ref = rest[n_ci:n_ci + 3]
        acc_t, m_sc, l_sc, st_sc, p_sc = rest[n_ci + 3 + n_co:n_ci + 3 + n_co + 5]
        n = pl.program_id(0)
        i, j = i_ref[n], j_ref[n]
        for phase, at in (("start", 0), ("mid", (2 * len(steps)) // 3)):
            if comm:
                @pl.when(n == at)
                def _():
                    _comm_phase(comm, phase, *comm_refs)

        @pl.when(j == 0)
        def _():
            acc_t[...] = jnp.zeros_like(acc_t)
            m_sc[...] = jnp.full_like(m_sc, NEG)
            l_sc[...] = jnp.zeros_like(l_sc)

        _score_tiles(q_ref, k_ref, ck_ref, st_sc, tk)
        _mask_diagonal(st_sc, i, j, tq, tk)
        alpha = _softmax_update(st_sc, p_sc, m_sc, l_sc, tk, tq)
        top = lax.broadcasted_iota(jnp.int32, (LANES, tq), 0) < HEAD_DIM
        for p in range(N_HEADS // 2):
            lanes = slice(p * LANES, (p + 1) * LANES)
            vt_pair = vt_ref[lanes, :]
            pv = jnp.where(top, _dot(vt_pair, p_sc[2 * p]), _dot(vt_pair, p_sc[2 * p + 1]))
            acc_t[lanes, :] = acc_t[lanes, :] * jnp.where(top, alpha[2 * p], alpha[2 * p + 1]) + pv

        @pl.when(j == (i + 1) * ratio - 1)
        def _():
            for p in range(N_HEADS // 2):
                lanes = slice(p * LANES, (p + 1) * LANES)
                l_pair = jnp.where(top, l_sc[2 * p], l_sc[2 * p + 1])
                o_t = acc_t[lanes, :] / l_pair
                o_ref[:, lanes] = jnp.transpose(o_t)
                ot_ref[lanes, :] = o_t.astype(BF16)
            lse_ref[...] = m_sc[...] + jnp.log(l_sc[...])

        if comm:
            @pl.when(n == len(steps) - 1)
            def _():
                _comm_phase(comm, "finish", *comm_refs)

    stat = pltpu.VMEM((N_HEADS, 1, tq), F32)
    res = pl.pallas_call(
        body,
        name="attn_fwd",
        grid_spec=pltpu.PrefetchScalarGridSpec(
            num_scalar_prefetch=2,
            grid=(len(steps),),
            in_specs=[
                pl.BlockSpec((tq, FOX_W), lambda n, it, jt: (it[n], 0)),
                pl.BlockSpec((tk, FOX_W), lambda n, it, jt: (jt[n], 1)),
                pl.BlockSpec((FOX_W, tk), lambda n, it, jt: (0, jt[n])),
                pl.BlockSpec((tk, LANES), lambda n, it, jt: (jt[n], 0)),
            ] + [_ANY] * n_ci,
            out_specs=[
                pl.BlockSpec((tq, FOX_W), lambda n, it, jt: (it[n], 0)),
                pl.BlockSpec((FOX_W, tq), lambda n, it, jt: (0, it[n])),
                pl.BlockSpec((N_HEADS, 1, tq), lambda n, it, jt: (0, 0, it[n])),
            ] + [_ANY] * n_co,
            scratch_shapes=[pltpu.VMEM((FOX_W, tq), F32), stat, stat, pltpu.VMEM((N_HEADS, tk, tq), F32),
                            pltpu.VMEM((N_HEADS, tk, tq), BF16)] + comm_sems,
        ),
        out_shape=[_sds((s_len, FOX_W), F32), _sds((FOX_W, s_len), BF16), _sds((N_HEADS, 1, s_len), F32)] + comm_outs,
        compiler_params=_params(("arbitrary",)),
    )(i_tab, j_tab, qkv, qkv, v_t, c_col, *comm_arrays)
    return res[0], res[1], res[2], res[3:]


def _sgu_mix(vn, w_stack, lt64):
    outs = []
    for p in range(SGU_G // 2):
        r = _dot(w_stack[p], vn[:, p * LANES:(p + 1) * LANES])
        outs.append(jnp.where(lt64, r[:SGU_LEN], r[SGU_LEN:]))
    return jnp.concatenate(outs, axis=1)


def _sgu_norm(sv, ln_g, ln_b):
    svg = _gelu(sv)
    xc = svg - jnp.mean(svg, axis=-1, keepdims=True)
    rstd = lax.rsqrt(jnp.mean(xc * xc, axis=-1, keepdims=True) + EPS)
    xhat = xc * rstd
    return xhat, rstd, xhat * ln_g + ln_b


def _sgu_fwd(prest, ln_g, ln_b, w_stack, b_pair, tm):
    s_len = prest.shape[0]

    def body(u_ref, sv_ref, g_ref, b_ref, w_ref, bp_ref, sg_ref, sgt_ref):
        lt64 = _lane_lt64((SGU_LEN, LANES))
        _, _, vn = _sgu_norm(sv_ref[...].astype(F32), g_ref[...], b_ref[...])
        vn = vn.astype(BF16)
        w_stack_v = [w_ref[p] for p in range(SGU_G // 2)]
        for w in range(tm // SGU_LEN):
            win = slice(w * SGU_LEN, (w + 1) * SGU_LEN)
            mixed = _sgu_mix(vn[win], w_stack_v, lt64) + bp_ref[...]
            sg = (_gelu(u_ref[win, :].astype(F32)) * mixed).astype(BF16)
            sg_ref[win, :] = sg
            sgt_ref[:, win] = jnp.transpose(sg)

    return pl.pallas_call(
        body,
        name="sgu_fwd",
        grid=(s_len // tm,),
        in_specs=[
            pl.BlockSpec((tm, SGU_W), lambda i: (i, U_OFF // SGU_W)),
            pl.BlockSpec((tm, SGU_W), lambda i: (i, SV_OFF // SGU_W)),
            pl.BlockSpec((1, SGU_W), lambda i: (0, 0)),
            pl.BlockSpec((1, SGU_W), lambda i: (0, 0)),
            pl.BlockSpec((SGU_G // 2, 2 * SGU_LEN, SGU_LEN), lambda i: (0, 0, 0)),
            pl.BlockSpec((SGU_LEN, SGU_W), lambda i: (0, 0)),
        ],
        out_specs=[pl.BlockSpec((tm, SGU_W), lambda i: (i, 0)), pl.BlockSpec((SGU_W, tm), lambda i: (0, i))],
        out_shape=[_sds((s_len, SGU_W), BF16), _sds((SGU_W, s_len), BF16)],
        compiler_params=_params(("parallel",)),
    )(prest, prest, ln_g, ln_b, w_stack, b_pair)


def _sgu_bwd(prest, dyb, w_b_t, ln_g, ln_b, w_stack, wt_stack, b_pair, tm):
    s_len = prest.shape[0]
    n_pair = SGU_G // 2

    def body(u_ref, sv_ref, dyb_ref, wb_ref, g_ref, b_ref, w_ref, wt_ref, bp_ref,
             du_ref, dsv_ref, dw_ref, db_ref, dg_ref, dbeta_ref, dusvt_ref, dvn_sc):
        @pl.when(pl.program_id(0) == 0)
        def _():
            dw_ref[...] = jnp.zeros_like(dw_ref)
            db_ref[...] = jnp.zeros_like(db_ref)
            dg_ref[...] = jnp.zeros_like(dg_ref)
            dbeta_ref[...] = jnp.zeros_like(dbeta_ref)

        lt64 = _lane_lt64((SGU_LEN, LANES))
        sv = sv_ref[...].astype(F32)
        xhat, rstd, vn32 = _sgu_norm(sv, g_ref[...], b_ref[...])
        vn = vn32.astype(BF16)
        w_stack_v = [w_ref[p] for p in range(n_pair)]
        dsg = _dot(dyb_ref[...], wb_ref[...])
        db = jnp.zeros((SGU_LEN, SGU_W), F32)
        for w in range(tm // SGU_LEN):
            win = slice(w * SGU_LEN, (w + 1) * SGU_LEN)
            u = u_ref[win, :].astype(F32)
            dsg_w = dsg[win]
            mixed = _sgu_mix(vn[win], w_stack_v, lt64) + bp_ref[...]
            du = (dsg_w * mixed * _gelu_grad(u)).astype(BF16)
            du_ref[win, :] = du
            dusvt_ref[:SGU_W, win] = jnp.transpose(du)
            dmixed = dsg_w * _gelu(u)
            db = db + dmixed
            dm16 = dmixed.astype(BF16)
            for p in range(n_pair):
                lanes = slice(p * LANES, (p + 1) * LANES)
                dmp = dm16[:, lanes]
                r = _dot(wt_ref[p], dmp)
                dvn_sc[win, lanes] = jnp.where(lt64, r[:SGU_LEN], r[SGU_LEN:])
                zero = jnp.zeros_like(dmp)
                dm_ab = jnp.concatenate([jnp.where(lt64, dmp, zero), jnp.where(lt64, zero, dmp)], axis=0)
                dw_ref[p] += _dot_nt(dm_ab, vn[win, lanes])
        db_ref[...] += db
        dvn = dvn_sc[...]
        dg_ref[...] += jnp.sum(dvn * xhat, axis=0, keepdims=True)
        dbeta_ref[...] += jnp.sum(dvn, axis=0, keepdims=True)
        dxh = dvn * g_ref[...]
        dsvg = rstd * (dxh - jnp.mean(dxh, axis=-1, keepdims=True) - xhat * jnp.mean(dxh * xhat, axis=-1, keepdims=True))
        dsv = (dsvg * _gelu_grad(sv)).astype(BF16)
        dsv_ref[...] = dsv
        dusvt_ref[SGU_W:, :] = jnp.transpose(dsv)

    const2 = lambda i: (0, 0)
    const3 = lambda i: (0, 0, 0)
    return pl.pallas_call(
        body,
        name="sgu_bwd",
        grid=(s_len // tm,),
        in_specs=[
            pl.BlockSpec((tm, SGU_W), lambda i: (i, U_OFF // SGU_W)),
            pl.BlockSpec((tm, SGU_W), lambda i: (i, SV_OFF // SGU_W)),
            pl.BlockSpec((tm, dyb.shape[1]), lambda i: (i, 0)),
            pl.BlockSpec(w_b_t.shape, const2),
            pl.BlockSpec((1, SGU_W), const2),
            pl.BlockSpec((1, SGU_W), const2),
            pl.BlockSpec((n_pair, 2 * SGU_LEN, SGU_LEN), const3),
            pl.BlockSpec((n_pair, 2 * SGU_LEN, SGU_LEN), const3),
            pl.BlockSpec((SGU_LEN, SGU_W), const2),
        ],
        out_specs=[
            pl.BlockSpec((tm, SGU_W), lambda i: (i, 0)),
            pl.BlockSpec((tm, SGU_W), lambda i: (i, 0)),
            pl.BlockSpec((n_pair, 2 * SGU_LEN, SGU_LEN), const3),
            pl.BlockSpec((SGU_LEN, SGU_W), const2),
            pl.BlockSpec((1, SGU_W), const2),
            pl.BlockSpec((1, SGU_W), const2),
            pl.BlockSpec((2 * SGU_W, tm), lambda i: (0, i)),
        ],
        out_shape=[
            _sds((s_len, SGU_W), BF16), _sds((s_len, SGU_W), BF16), _sds((n_pair, 2 * SGU_LEN, SGU_LEN), F32),
            _sds((SGU_LEN, SGU_W), F32), _sds((1, SGU_W), F32), _sds((1, SGU_W), F32),
            _sds((2 * SGU_W, s_len), BF16),
        ],
        scratch_shapes=[pltpu.VMEM((tm, SGU_W), F32)],
        compiler_params=_params(("arbitrary",)),
    )(prest, prest, dyb, w_b_t, ln_g, ln_b, w_stack, wt_stack, b_pair)


def _attn_bwd(qkv, k_t, do, c_col, lse_row, delta_row, tq, tk, comm=()):
    s_len = qkv.shape[0]
    nq, nk = s_len // tq, s_len // tk
    ratio = tq // tk
    scale = HEAD_DIM ** -0.5
    steps = [(j, i) for j in range(nk) for i in range(j // ratio, nq)]
    j_tab = jnp.asarray([j for j, _ in steps], jnp.int32)
    i_tab = jnp.asarray([i for _, i in steps], jnp.int32)

    comm_arrays, comm_outs, comm_sems = _comm_operands(comm)
    n_ci, n_co = len(comm_arrays), len(comm_outs)

    def body(j_ref, i_ref, q_ref, k_ref, v_ref, kt_ref, do_ref, ck_ref, lse_ref, dl_ref, *rest):
        comm_refs = rest[:n_ci], rest[n_ci + 6:n_ci + 6 + n_co], rest[n_ci + 6 + n_co + 8:]
        dq_ref, dk_ref, dv_ref, dcr_ref, dcc_ref, dkvt_ref = rest[n_ci:n_ci + 6]
        dq_t, dk_acc, dv_acc, dcc_acc, st_sc, dpt_sc, p_sc, ds_sc = rest[n_ci + 6 + n_co:n_ci + 6 + n_co + 8]
        n = pl.program_id(0)
        j, i = j_ref[n], i_ref[n]

        @pl.when(n == 0)
        def _():
            _comm_phase(comm, "start", *comm_refs)
            dq_t[...] = jnp.zeros_like(dq_t)
            dcr_ref[...] = jnp.zeros_like(dcr_ref)

        @pl.when(i == j // ratio)
        def _():
            dk_acc[...] = jnp.zeros_like(dk_acc)
            dv_acc[...] = jnp.zeros_like(dv_acc)
            dcc_acc[...] = jnp.zeros_like(dcc_acc)

        lt64 = _lane_lt64((tk, LANES))
        _score_tiles(q_ref, k_ref, ck_ref, st_sc, tk)
        for p in range(N_HEADS // 2):
            lanes = slice(p * LANES, (p + 1) * LANES)
            dpt2 = _dot_nt(_stack_heads(v_ref[:, lanes], lt64), do_ref[:, lanes].astype(BF16))
            dpt_sc[2 * p] = dpt2[:tk]
            dpt_sc[2 * p + 1] = dpt2[tk:]
        _mask_diagonal(st_sc, i, j, tq, tk)

        pt = jnp.exp(st_sc[...] - lse_ref[...])
        dst = pt * (dpt_sc[...] - dl_ref[...])
        p_sc[...] = pt.astype(BF16)
        ds_sc[...] = dst.astype(BF16)
        dcr_ref[i] += jnp.sum(dst, axis=1, keepdims=True)
        col_sums = jnp.sum(dst, axis=2, keepdims=True)
        lane = lax.broadcasted_iota(jnp.int32, (tk, LANES), 1)
        dcc = jnp.zeros((tk, LANES), F32)
        for h in range(N_HEADS):
            dcc = jnp.where(lane == h, -col_sums[h], dcc)
        dcc_acc[...] += dcc

        for p in range(N_HEADS // 2):
            lanes = slice(p * LANES, (p + 1) * LANES)
            q_pair = q_ref[:, lanes] * jnp.asarray(scale, BF16)
            dv2 = _dot(p_sc[2 * p:2 * p + 2].reshape(2 * tk, tq), do_ref[:, lanes].astype(BF16))
            dv_acc[:, lanes] += jnp.where(lt64, dv2[:tk], dv2[tk:])
            dk2 = _dot(ds_sc[2 * p:2 * p + 2].reshape(2 * tk, tq), q_pair)
            dk_acc[:, lanes] += jnp.where(lt64, dk2[:tk], dk2[tk:])
            dq2 = _dot(kt_ref[lanes, :], jnp.concatenate([ds_sc[2 * p], ds_sc[2 * p + 1]], axis=1))
            top = lax.broadcasted_iota(jnp.int32, (LANES, tq), 0) < HEAD_DIM
            dq_t[i, lanes, :] += jnp.where(top, dq2[:, :tq], dq2[:, tq:])

        @pl.when(j == (i + 1) * ratio - 1)
        def _():
            rows = pl.ds(pl.multiple_of(i * tq, tq), tq)
            for p in range(N_HEADS // 2):
                lanes = slice(p * LANES, (p + 1) * LANES)
                dq_ref[rows, lanes] = (jnp.transpose(dq_t[i, lanes, :]) * scale).astype(BF16)

        @pl.when(i == nq - 1)
        def _():
            dk16, dv16 = dk_acc[...].astype(BF16), dv_acc[...].astype(BF16)
            dk_ref[...] = dk16
            dv_ref[...] = dv16
            dkvt_ref[:FOX_W, :] = jnp.transpose(dk16)
            dkvt_ref[FOX_W:, :] = jnp.transpose(dv16)
            dcc_ref[...] = dcc_acc[...]

        if comm:
            @pl.when(n == len(steps) // 2)
            def _():
                _comm_phase(comm, "mid", *comm_refs)

            @pl.when(n == len(steps) - 1)
            def _():
                _comm_phase(comm, "finish", *comm_refs)

    q_map = lambda n, jt, it: (it[n], 0)
    q_stat = lambda n, jt, it: (0, 0, it[n])
    k_map = lambda n, jt, it: (jt[n], 0)
    tile = (N_HEADS, tk, tq)
    res = pl.pallas_call(
        body,
        name="attn_bwd",
        grid_spec=pltpu.PrefetchScalarGridSpec(
            num_scalar_prefetch=2,
            grid=(len(steps),),
            in_specs=[
                pl.BlockSpec((tq, FOX_W), q_map),
                pl.BlockSpec((tk, FOX_W), lambda n, jt, it: (jt[n], 1)),
                pl.BlockSpec((tk, FOX_W), lambda n, jt, it: (jt[n], 2)),
                pl.BlockSpec((FOX_W, tk), lambda n, jt, it: (0, jt[n])),
                pl.BlockSpec((tq, FOX_W), q_map),
                pl.BlockSpec((tk, LANES), k_map),
                pl.BlockSpec((N_HEADS, 1, tq), q_stat),
                pl.BlockSpec((N_HEADS, 1, tq), q_stat),
            ] + [_ANY] * n_ci,
            out_specs=[
                pl.BlockSpec((s_len, FOX_W), lambda n, jt, it: (0, 0)),
                pl.BlockSpec((tk, FOX_W), k_map),
                pl.BlockSpec((tk, FOX_W), k_map),
                pl.BlockSpec((nq, N_HEADS, 1, tq), lambda n, jt, it: (0, 0, 0, 0)),
                pl.BlockSpec((tk, LANES), k_map),
                pl.BlockSpec((2 * FOX_W, tk), lambda n, jt, it: (0, jt[n])),
            ] + [_ANY] * n_co,
            scratch_shapes=[pltpu.VMEM((nq, FOX_W, tq), F32), pltpu.VMEM((tk, FOX_W), F32), pltpu.VMEM((tk, FOX_W), F32),
                            pltpu.VMEM((tk, LANES), F32), pltpu.VMEM(tile, F32), pltpu.VMEM(tile, F32),
                            pltpu.VMEM(tile, BF16), pltpu.VMEM(tile, BF16)] + comm_sems,
        ),
        out_shape=[_sds((s_len, FOX_W), BF16), _sds((s_len, FOX_W), BF16), _sds((s_len, FOX_W), BF16),
                   _sds((nq, N_HEADS, 1, tq), F32), _sds((s_len, LANES), F32),
                   _sds((2 * FOX_W, s_len), BF16)] + comm_outs,
        compiler_params=_params(("arbitrary",)),
    )(j_tab, i_tab, qkv, qkv, qkv, k_t, do, c_col, lse_row, delta_row, *comm_arrays)
    return res[:6], res[6:]


def _forget_bwd(dc_rows, dc_cols, prest, b_f_pad, tc):
    s_len = dc_rows.shape[0]
    nb = s_len // tc

    def body(dcr_ref, dc_ref, f_ref, b_ref, df_ref, db_ref, dft_ref, carry):
        @pl.when(pl.program_id(0) == 0)
        def _():
            carry[...] = jnp.zeros_like(carry)
            db_ref[...] = jnp.zeros_like(db_ref)

        row = lax.broadcasted_iota(jnp.int32, (tc, tc), 0)
        col = lax.broadcasted_iota(jnp.int32, (tc, tc), 1)
        tri = (row <= col).astype(F32)
        dlogf = _dot_f32(tri, dcr_ref[...] + dc_ref[...], "a") + carry[...]
        carry[...] = dlogf[0:1, :]
        z = f_ref[...] + b_ref[...]
        lane = lax.broadcasted_iota(jnp.int32, (tc, LANES), 1)
        dz = jnp.where(lane < N_HEADS, dlogf * _sigmoid(-z), 0.0)
        df_ref[...] = dz.astype(BF16)
        dft_ref[...] = jnp.transpose(dz).astype(BF16)
        db_ref[...] += jnp.sum(dz, axis=0, keepdims=True)

    rev = lambda i: (nb - 1 - i, 0)
    return pl.pallas_call(
        body,
        name="forget_bwd",
        grid=(nb,),
        in_specs=[
            pl.BlockSpec((tc, LANES), rev),
            pl.BlockSpec((tc, LANES), rev),
            pl.BlockSpec((tc, LANES), rev),
            pl.BlockSpec((1, LANES), lambda i: (0, 0)),
        ],
        out_specs=[pl.BlockSpec((tc, LANES), rev), pl.BlockSpec((1, LANES), lambda i: (0, 0)),
                   pl.BlockSpec((LANES, tc), lambda i: (0, nb - 1 - i))],
        out_shape=[_sds((s_len, LANES), BF16), _sds((1, LANES), F32), _sds((LANES, s_len), BF16)],
        scratch_shapes=[pltpu.VMEM((1, LANES), F32)],
        compiler_params=_params(("arbitrary",)),
    )(dc_rows, dc_cols, prest, b_f_pad)


def _pair_sum(name, g4, recv, idx, tr):
    _, _, r, c = g4.shape

    def body(idx_ref, g_ref, r_ref, p16_ref, own_ref):
        k = pl.program_id(1)
        s = g_ref[...] + r_ref[...]
        p16_ref[...] = s.astype(BF16)

        @pl.when(k == idx_ref[1])
        def _():
            own_ref[...] = s

    return pl.pallas_call(
        body,
        name=name,
        grid_spec=pltpu.PrefetchScalarGridSpec(
            num_scalar_prefetch=1,
            grid=(r // tr, 4),
            in_specs=[
                pl.BlockSpec((None, None, tr, c), lambda i, k, idx: (k, idx[0], i, 0)),
                pl.BlockSpec((None, tr, c), lambda i, k, idx: (k, i, 0)),
            ],
            out_specs=[
                pl.BlockSpec((None, tr, c), lambda i, k, idx: (k, i, 0)),
                pl.BlockSpec((tr, c), lambda i, k, idx: (i, 0)),
            ],
        ),
        out_shape=[_sds((4, r, c), BF16), _sds((r, c), F32)],
        compiler_params=_params(("parallel", "arbitrary")),
    )(idx, g4, recv)


def _adamw_math(w, g, m, v):
    m2 = ADAM_B1 * m + (1.0 - ADAM_B1) * g
    v2 = ADAM_B2 * v + (1.0 - ADAM_B2) * (g * g)
    m_hat = m2 / (1.0 - ADAM_B1 ** ADAM_STEP)
    v_hat = v2 / (1.0 - ADAM_B2 ** ADAM_STEP)
    delta = -ADAM_LR * (m_hat / (jnp.sqrt(v_hat) + ADAM_EPS) + ADAM_WD * w)
    return delta, m2, v2


def _adamw_shard(name, own, recv, w, m, v, tr):
    r, c = own.shape

    def body(own_ref, recv_ref, w_ref, m_ref, v_ref, g_ref, d_ref, m2_ref, v2_ref):
        g = own_ref[...]
        for k in range(3):
            g = g + recv_ref[k].astype(F32)
        delta, m2, v2 = _adamw_math(w_ref[...], g, m_ref[...], v_ref[...])
        g_ref[...] = g
        d_ref[...] = delta
        m2_ref[...] = m2
        v2_ref[...] = v2

    spec = pl.BlockSpec((tr, c), lambda i: (i, 0))
    return pl.pallas_call(
        body,
        name=name,
        grid=(r // tr,),
        in_specs=[spec, pl.BlockSpec((3, tr, c), lambda i: (0, i, 0)), spec, spec, spec],
        out_specs=[spec] * 4,
        out_shape=[_sds((r, c), F32)] * 4,
        compiler_params=_params(("parallel",)),
    )(own, recv, w, m, v)


def _adamw_small(name, gathered, first_row, w, m, v):
    r = w.shape[0]
    assert first_row % r == 0

    def body(ga_ref, w_ref, m_ref, v_ref, g_ref, d_ref, m2_ref, v2_ref):
        g = ga_ref[0]
        for k in range(1, N_DEV):
            g = g + ga_ref[k]
        delta, m2, v2 = _adamw_math(w_ref[...], g, m_ref[...], v_ref[...])
        g_ref[...] = g
        d_ref[...] = delta
        m2_ref[...] = m2
        v2_ref[...] = v2

    spec = pl.BlockSpec((r, LANES), lambda i: (0, 0))
    return pl.pallas_call(
        body,
        name=name,
        grid=(1,),
        in_specs=[pl.BlockSpec((N_DEV, r, LANES), lambda i: (0, first_row // r, 0)), spec, spec, spec],
        out_specs=[spec] * 4,
        out_shape=[_sds((r, LANES), F32)] * 4,
        compiler_params=_params(("arbitrary",)),
    )(gathered, w, m, v)


_TINY_EARLY = (("b_sgu", (1, SGU_G, SGU_LEN)), ("norm2_g", (1, D_MODEL)), ("normf_g", (D_MODEL,)),
               ("ln_v_g", (1, SGU_W)), ("ln_v_b", (1, SGU_W)))
_TINY_LATE = (("b_f", (1, N_HEADS)), ("norm1_g", (1, D_MODEL)), ("loss", ()))


def _pack_rows(values):
    rows = []
    for val in values:
        flat = val.reshape(-1).astype(F32)
        pad = (-flat.shape[0]) % LANES
        rows.append(jnp.pad(flat, (0, pad)).reshape(-1, LANES))
    packed = jnp.concatenate(rows, axis=0)
    return jnp.pad(packed, ((0, (-packed.shape[0]) % 8), (0, 0)))


def _unpack_rows(packed, group):
    out, row = {}, 0
    for name, shape in group:
        size = math.prod(shape)
        n_rows = -(-size // LANES)
        out[name] = packed[row:row + n_rows].reshape(-1)[:size].reshape(shape)
        row += n_rows
    return out


def kernel(x, norm1_g, w_in, b_f, ln_v_g, ln_v_b, w_sgu, b_sgu, w_a, w_b, w_o, norm2_g, w_up, w_down, normf_g, loss_target, m_norm1_g, m_w_in, m_b_f, m_ln_v_g, m_ln_v_b, m_w_sgu, m_b_sgu, m_w_a, m_w_b, m_w_o, m_norm2_g, m_w_up, m_w_down, m_normf_g, v_norm1_g, v_w_in, v_b_f, v_ln_v_g, v_ln_v_b, v_w_sgu, v_b_sgu, v_w_a, v_w_b, v_w_o, v_norm2_g, v_w_up, v_w_down, v_normf_g):
    xs = x[0]
    target = loss_target[0]
    s_len, d = xs.shape
    tm = min(512, s_len)
    tl = min(1024, s_len)
    tr = min(512, s_len)
    ta = min(512, s_len)
    tc = min(1024, s_len)

    w_in_t = jnp.transpose(w_in[0])
    lin = (IN_SHARD * d // LANES, LANES)
    big = (w_in_t.reshape(lin), w_a[0], w_b[0], w_o[0], w_up[0], w_down[0])
    h, h_t, r1, (w_in_g,) = _rms_fwd("rms1", xs, norm1_g, tm, comm=[_gather_plan(w_in_t.astype(BF16))])
    w_in_f = w_in_g.reshape(IN_COLS, d)
    later_shards = [jnp.transpose(w_a[0]), jnp.transpose(w_b[0]), w_o[0], jnp.transpose(w_up[0]), w_down[0]]
    later_plans = [_gather_plan(w.astype(BF16)) for w in later_shards]

    def unflatten(gathered):
        return [g.reshape(N_DEV * g.shape[1], g.shape[2]) for g in gathered]

    w_qkv = w_in_f[:QKV_W]
    f_lo = QKV_W
    u_lo = f_lo + N_HEADS
    w_rest = jnp.concatenate([w_in_f[u_lo:], jnp.pad(w_in_f[f_lo:u_lo], ((0, LANES - N_HEADS), (0, 0)))], axis=0)

    chunk_id = jnp.arange(SGU_LEN) // CHUNK
    sgu_mask = chunk_id[None, :] <= chunk_id[:, None]
    w_masked = jnp.where(sgu_mask[None], w_sgu[0], 0.0)
    w_stack = w_masked.reshape(SGU_G // 2, 2 * SGU_LEN, SGU_LEN).astype(BF16)
    wt_stack = jnp.transpose(w_masked, (0, 2, 1)).reshape(SGU_G // 2, 2 * SGU_LEN, SGU_LEN).astype(BF16)
    b_pair = jnp.transpose(jnp.repeat(b_sgu[0], SGU_W // SGU_G, axis=0))
    b_f_pad = jnp.pad(b_f, ((0, 0), (0, LANES - N_HEADS)))
    head_sel = (jnp.arange(FOX_W)[:, None] // HEAD_DIM == jnp.arange(LANES)[None, :]).astype(F32)

    def qkv_epi(accs, ex, out):
        tile = accs[0].astype(BF16)
        out[0][...] = tile
        out[1][...] = jnp.transpose(tile[:, FOX_W:2 * FOX_W])
        out[2][...] = jnp.transpose(tile[:, 2 * FOX_W:])

    t_spec = pl.BlockSpec((FOX_W, tm), lambda i, j: (0, i))
    qkv, k_t, v_t = _mm("proj_qkv", [(h, w_qkv, True, None)], [],
                        [(_sds((s_len, QKV_W), BF16), _tile(tm, QKV_W)), (_sds((FOX_W, s_len), BF16), t_spec),
                         (_sds((FOX_W, s_len), BF16), t_spec)],
                        qkv_epi, m=s_len, tm=tm, n=QKV_W, tn=QKV_W)
    def rest_epi(accs, ex, out):
        out[0][...] = accs[0].astype(BF16)
        out[1][...] = accs[0][:, F_OFF:F_OFF + LANES]

    prest, f_logit = _mm("proj_rest", [(h, w_rest, True, None)], [],
                         [(_sds((s_len, REST_W), BF16), _tile(tm, REST_W)), (_sds((s_len, LANES), F32), _row(tm, LANES))],
                         rest_epi, m=s_len, tm=tm, n=REST_W, tn=REST_W)

    c_col = _forget_cumsum(f_logit, b_f_pad, tc)
    o, o_t, lse_row, later_g = _attn_fwd(qkv, v_t, c_col, ta, ta, comm=later_plans)
    w_a_t, w_b_t, w_o_f, w_up_t, w_down_f = unflatten(later_g)
    sg, sg_t = _sgu_fwd(prest, ln_v_g, ln_v_b, w_stack, b_pair, tl)

    def merge_epi(accs, ex, out):
        ya, yb = accs
        sa, sb = _sigmoid(ex[0][...].astype(F32)), _sigmoid(ex[1][...].astype(F32))
        merged = (sa * ya + sb * yb).astype(BF16)
        out[0][...] = merged
        out[1][...] = ya.astype(BF16)
        out[2][...] = yb.astype(BF16)
        out[3][...] = jnp.transpose(merged)

    merged, ya, yb, merged_t = _mm(
        "merge", [(o, w_a_t, True, None), (sg, w_b_t, True, None)],
        [(prest, _tile(tm, d, GA_OFF // d)), (prest, _tile(tm, d, GB_OFF // d))],
        [(_sds((s_len, d), BF16), _tile(tm, d))] * 3 + [(_sds((d, s_len), BF16), _tile_t(tm, d))],
        merge_epi, m=s_len, tm=tm, n=d, tn=d)

    def resid_epi(accs, ex, out):
        x1v = ex[0][...] + accs[0]
        out[0][...] = x1v
        r = lax.rsqrt(jnp.mean(x1v * x1v, axis=-1, keepdims=True) + EPS)
        h2v = (x1v * r * ex[1][...]).astype(BF16)
        out[1][...] = h2v
        out[2][...] = jnp.transpose(h2v)
        out[3][...] = r

    x1, h2, h2_t, r2 = _mm(
        "out_proj", [(merged, w_o_f, False, None)], [(xs, _tile(tm, d)), (norm2_g, _whole((1, d)))],
        [(_sds((s_len, d), F32), _tile(tm, d)), (_sds((s_len, d), BF16), _tile(tm, d)),
         (_sds((d, s_len), BF16), _tile_t(tm, d)), (_sds((s_len, 1), F32), _row(tm, 1))],
        resid_epi, m=s_len, tm=tm, n=d, tn=d)

    def up_epi(accs, ex, out):
        act = jnp.square(jnp.maximum(accs[0], 0.0)).astype(BF16)
        out[0][...] = act
        out[1][...] = jnp.transpose(act)

    act, act_t = _mm(
        "mlp_up", [(h2, w_up_t, True, None)], [],
        [(_sds((s_len, D_FF), BF16), _tile(tl, 1024)), (_sds((D_FF, s_len), BF16), _tile_t(tl, 1024))],
        up_epi, m=s_len, tm=tl, n=D_FF, tn=1024)

    def first_step():
        return jnp.logical_and(pl.program_id(0) == 0, pl.program_id(1) == 0)

    def accumulate(ref, val):
        @pl.when(first_step())
        def _():
            ref[...] = val

        @pl.when(jnp.logical_not(first_step()))
        def _():
            ref[...] += val

    def final_epi(accs, ex, out):
        x1_ref, t_ref, g_ref = ex
        x2 = x1_ref[...] + accs[0]
        rf = lax.rsqrt(jnp.mean(x2 * x2, axis=-1, keepdims=True) + EPS)
        xh = x2 * rf
        gf = g_ref[...]
        err = xh * gf - t_ref[...]
        dy = err * (1.0 / d)
        dx2 = _rms_bwd(xh, rf, gf, dy)
        out[0][...] = dx2
        accumulate(out[1], jnp.sum(dy * xh, axis=0, keepdims=True))
        part = 0.5 * jnp.sum(jnp.sum(err * err, axis=-1, keepdims=True) * (1.0 / d), axis=0, keepdims=True)
        accumulate(out[2], jnp.broadcast_to(part, (1, LANES)))
        out[3][...] = dx2.astype(BF16)

    gf2 = normf_g.reshape(1, d)
    dx2, g_normf, loss_part, dx2_16 = _mm(
        "mlp_down_loss", [(act, w_down_f, False, None)],
        [(x1, _row(tr, d)), (target, _row(tr, d)), (gf2, _whole((1, d)))],
        [(_sds((s_len, d), F32), _row(tr, d)), (_sds((1, d), F32), _whole((1, d))), (_sds((1, LANES), F32), _whole((1, LANES))),
         (_sds((s_len, d), BF16), _row(tr, d))],
        final_epi, m=s_len, tm=tr, n=d, tn=d, arbitrary=True)

    def dact_epi(accs, ex, out):
        out[0][...] = (accs[0] * (2.0 * jnp.sqrt(ex[0][...].astype(F32)))).astype(BF16)

    (da,) = _mm("mlp_down_bwd", [(dx2_16, w_down_f, True, None)], [(act, _tile(tl, 1024))],
                [(_sds((s_len, D_FF), BF16), _tile(tl, 1024))], dact_epi, m=s_len, tm=tl, n=D_FF, tn=1024)
    g_down = _grad_w("grad_w_down", act_t, dx2_16, tk=1024, tn=d, ts=tl)
    g_up = _grad_w("grad_w_up", h2_t, da, tk=d, tn=1024, ts=tl, block_cols=D_FF // N_DEV)

    def dh2_epi(accs, ex, out):
        x1_ref, r_ref, g_ref, dx2_ref = ex
        r = r_ref[...]
        xh = x1_ref[...] * r
        dh2 = accs[0]
        out[0][...] = dx2_ref[...] + _rms_bwd(xh, r, g_ref[...], dh2)
        accumulate(out[1], jnp.sum(dh2 * xh, axis=0, keepdims=True))

    my_c = lax.axis_index("c")
    my_chip = 2 * lax.axis_index("x") + lax.axis_index("y")
    idx = jnp.stack([my_c, my_chip]).astype(jnp.int32)
    parts16, owns = {}, {}

    def split_cores(g8):
        return g8.reshape((4, 2) + g8.shape[1:])

    def row_tile(r):
        return 512 if r % 512 == 0 else r

    def pair_sums(names, grads4, from_sibling):
        for name, g4, recv in zip(names, grads4, from_sibling):
            parts16[name], owns[name] = _pair_sum("grad_pair_sum_" + name, g4, recv, idx, row_tile(g4.shape[2]))

    grads4_mlp = [split_cores(g_up), split_cores(g_down.reshape(N_DEV, D_FF // N_DEV, d))]
    (dx1, g_norm2), from_sibling = _mm(
        "mlp_up_bwd", [(da, w_up_t, False, None)],
        [(x1, _row(tr, d)), (r2, _row(tr, 1)), (norm2_g, _whole((1, d))), (dx2, _row(tr, d))],
        [(_sds((s_len, d), F32), _row(tr, d)), (_sds((1, d), F32), _whole((1, d)))],
        dh2_epi, m=s_len, tm=tr, n=d, tn=d, arbitrary=True, comm=[_pair_exchange_plan(grads4_mlp)])
    pair_sums(("w_up", "w_down"), grads4_mlp, from_sibling)

    def dmerge_epi(accs, ex, out):
        dm = accs[0]
        sa, sb = _sigmoid(ex[0][...].astype(F32)), _sigmoid(ex[1][...].astype(F32))
        out[0][...] = (dm * sa).astype(BF16)
        out[1][...] = (dm * sb).astype(BF16)
        dga = (dm * ex[2][...] * sa * (1.0 - sa)).astype(BF16)
        dgb = (dm * ex[3][...] * sb * (1.0 - sb)).astype(BF16)
        out[2][...] = dga
        out[3][...] = dgb
        out[4][:d, :] = jnp.transpose(dga)
        out[4][d:, :] = jnp.transpose(dgb)

    dya, dyb, dga, dgb, dg_t = _mm(
        "out_proj_bwd", [(dx1, w_o_f, True, None)],
        [(prest, _tile(tm, d, GA_OFF // d)), (prest, _tile(tm, d, GB_OFF // d)), (ya, _tile(tm, d)), (yb, _tile(tm, d))],
        [(_sds((s_len, d), BF16), _tile(tm, d))] * 4
        + [(_sds((2 * d, s_len), BF16), pl.BlockSpec((2 * d, tm), lambda i, j: (0, i)))],
        dmerge_epi, m=s_len, tm=tm, n=d, tn=d)
    g_o = _grad_w("grad_w_o", merged_t, dx1, tk=d, tn=d, ts=tl).reshape(N_DEV, d // N_DEV, d)
    def col_blocks(g):
        return jnp.transpose(g.reshape(g.shape[0], N_DEV, g.shape[1] // N_DEV), (1, 0, 2))

    g_a = col_blocks(_grad_w("grad_w_a", o_t, dya, tk=FOX_W, tn=d, ts=tl))
    g_b = col_blocks(_grad_w("grad_w_b", sg_t, dyb, tk=SGU_W, tn=d, ts=tl))

    def do_epi(accs, ex, out):
        do = accs[0]
        out[0][...] = do
        out[1][...] = _dot_f32(do * ex[0][...], ex[1][...], "b")

    grads4_mix = [split_cores(g) for g in (g_a, g_b, g_o)]
    (do, delta), from_sibling = _mm(
        "attn_out_bwd", [(dya, w_a_t, False, None)], [(o, _row(tm, FOX_W)), (head_sel, _whole((FOX_W, LANES)))],
        [(_sds((s_len, FOX_W), F32), _row(tm, FOX_W)), (_sds((s_len, LANES), F32), _row(tm, LANES))],
        do_epi, m=s_len, tm=tm, n=FOX_W, tn=FOX_W, comm=[_pair_exchange_plan(grads4_mix)])
    pair_sums(("w_a", "w_b", "w_o"), grads4_mix, from_sibling)

    du, dsv, dw_pairs, db_pos, g_ln_g, g_ln_b, dusv_t = _sgu_bwd(
        prest, dyb, w_b_t, ln_v_g, ln_v_b, w_stack, wt_stack, b_pair, tl)
    g_w_sgu = jnp.where(sgu_mask[None], dw_pairs.reshape(SGU_G, SGU_LEN, SGU_LEN), 0.0)
    g_b_sgu = jnp.transpose(jnp.sum(db_pos.reshape(SGU_LEN, SGU_G, SGU_W // SGU_G), axis=-1))

    delta_row = jnp.transpose(delta[:, :N_HEADS]).reshape(N_HEADS, 1, s_len)
    early = ("w_a", "w_b", "w_o", "w_up", "w_down")
    small_early = _pack_rows((g_w_sgu, g_b_sgu, g_norm2, g_normf, g_ln_g, g_ln_b))
    (dq, dk, dv, dc_rows_blk, dc_cols, dkv_t), (small_early_all, *from_chips_early) = _attn_bwd(
        qkv, k_t, do, c_col, lse_row, delta_row, ta, ta,
        comm=[_gather_plan(small_early), _chip_exchange_plan([parts16[n] for n in early])])
    dc_rows = jnp.transpose(dc_rows_blk.reshape(s_len // ta, N_HEADS, ta), (0, 2, 1)).reshape(s_len, N_HEADS)
    dc_rows = jnp.pad(dc_rows, ((0, 0), (0, LANES - N_HEADS)))
    dfl, g_bf, dfl_t = _forget_bwd(dc_rows, dc_cols, f_logit, b_f_pad, tc)

    dp_t = (jnp.transpose(dq), dkv_t, dfl_t, dusv_t, dg_t)
    g_in_rows = [_grad_w("grad_w_in_%d" % k, seg_t, h, tk=min(seg_t.shape[0], 1024), tn=d, ts=tl)
                 for k, seg_t in enumerate(dp_t)]
    g_in_rows[2] = g_in_rows[2][:N_HEADS]
    g_in = jnp.concatenate(g_in_rows, axis=0).reshape((N_DEV,) + lin)

    def dx_epi(accs, ex, out):
        x_ref, r_ref, g_ref, dx1_ref = ex
        dh = accs[0]
        for extra in accs[1:]:
            dh = dh + extra
        r = r_ref[...]
        xh = x_ref[...] * r
        out[0][...] = dx1_ref[...] + _rms_bwd(xh, r, g_ref[...], dh)
        accumulate(out[1], jnp.sum(dh * xh, axis=0, keepdims=True))

    rest_cols = ((du, U_OFF, 512), (dsv, SV_OFF, 512), (dga, GA_OFF, 1024), (dgb, GB_OFF, 1024), (dfl, F_OFF, LANES))
    dx_pairs = [(seg, w_qkv, False, (512 * k, 512 * (k + 1))) for k, seg in enumerate((dq, dk, dv))]
    dx_pairs += [(seg, w_rest, False, (lo, lo + width)) for seg, lo, width in rest_cols]
    grads4_in = [split_cores(g_in)]
    pair_sums(("w_in",), grads4_in, _run_comm("grad_pair_exchange_w_in", [_pair_exchange_plan(grads4_in)]))
    (grad_x, g_norm1), (from_chips_in,) = _mm(
        "proj_bwd", dx_pairs,
        [(xs, _row(tr, d)), (r1, _row(tr, 1)), (norm1_g, _whole((1, d))), (dx1, _row(tr, d))],
        [(_sds((s_len, d), F32), _row(tr, d)), (_sds((1, d), F32), _whole((1, d)))],
        dx_epi, m=s_len, tm=tr, n=d, tn=d, arbitrary=True, comm=[_chip_exchange_plan([parts16["w_in"]])])
    small_late = _pack_rows((g_bf[:, :N_HEADS], g_norm1, loss_part[0, 0]))
    (small_late_all,) = _run_comm("gather_last_grads", [_gather_plan(small_late)])
    from_chips = dict(zip(early, from_chips_early), w_in=from_chips_in)

    names = ("w_in", "w_a", "w_b", "w_o", "w_up", "w_down")
    moments_m = (m_w_in, m_w_a, m_w_b, m_w_o, m_w_up, m_w_down)
    moments_v = (v_w_in, v_w_a, v_w_b, v_w_o, v_w_up, v_w_down)
    big_out = {}
    for name, w, m, v in zip(names, big, moments_m, moments_v):
        own = owns[name]
        transposed = name == "w_in"
        m0, v0 = (jnp.transpose(m[0]).reshape(lin), jnp.transpose(v[0]).reshape(lin)) if transposed else (m[0], v[0])
        res = _adamw_shard("adamw_" + name, own, from_chips[name], w, m0, v0, row_tile(own.shape[0]))
        big_out[name] = [(jnp.transpose(t.reshape(IN_SHARD, d)) if transposed else t)[None] for t in res]

    zero = jnp.zeros((), F32)
    sgu_rows = (SGU_G * SGU_LEN, LANES)
    res_sgu = _adamw_small("adamw_w_sgu", small_early_all, 0, w_sgu.reshape(sgu_rows), m_w_sgu.reshape(sgu_rows),
                           v_w_sgu.reshape(sgu_rows))
    small_out = {"w_sgu": [t.reshape(w_sgu.shape) for t in res_sgu]}
    res_early = _adamw_small(
        "adamw_tiny_early", small_early_all, sgu_rows[0], _pack_rows((b_sgu, norm2_g, normf_g, ln_v_g, ln_v_b)),
        _pack_rows((m_b_sgu, m_norm2_g, m_normf_g, m_ln_v_g, m_ln_v_b)),
        _pack_rows((v_b_sgu, v_norm2_g, v_normf_g, v_ln_v_g, v_ln_v_b)))
    res_late = _adamw_small(
        "adamw_tiny_late", small_late_all, 0, _pack_rows((b_f, norm1_g, zero)), _pack_rows((m_b_f, m_norm1_g, zero)),
        _pack_rows((v_b_f, v_norm1_g, zero)))
    for res, group in ((res_early, _TINY_EARLY), (res_late, _TINY_LATE)):
        unpacked = [_unpack_rows(t, group) for t in res]
        small_out.update({name: [u[name] for u in unpacked] for name, _ in group})
    loss = small_out["loss"][0]

    order = ("norm1_g", "w_in", "b_f", "ln_v_g", "ln_v_b", "w_sgu", "b_sgu", "w_a", "w_b", "w_o", "norm2_g", "w_up",
             "w_down", "normf_g")
    table = {**big_out, **small_out}
    outs = [loss, grad_x[None]]
    for kind in range(4):
        outs += [table[n][kind] for n in order]
    return tuple(outs)
```

```python
import math

import jax
import jax.numpy as jnp
from jax import lax
from jax.experimental import pallas as pl
from jax.experimental.pallas import tpu as pltpu

F32 = jnp.float32
BF16 = jnp.bfloat16

N_DEV = 8
D_MODEL = 1024
N_HEADS = 8
HEAD_DIM = 64
FOX_W = N_HEADS * HEAD_DIM
SGU_G = 8
SGU_W = 512
SGU_LEN = 128
CHUNK = 64
D_FF = 4 * D_MODEL
IN_COLS = 3 * FOX_W + N_HEADS + 2 * SGU_W + 2 * D_MODEL
IN_SHARD = IN_COLS // N_DEV
LANES = 128
QKV_W = 3 * FOX_W
U_OFF, SV_OFF, GA_OFF, GB_OFF, F_OFF = 0, 512, 1024, 2048, 3072
REST_W = F_OFF + LANES
EPS = 1e-6
NEG = -1e30

ADAM_LR = 0.001
ADAM_B1 = 0.9
ADAM_B2 = 0.999
ADAM_EPS = 1e-08
ADAM_WD = 0.01
ADAM_STEP = 10

VMEM_LIMIT = 56 * 1024 * 1024
MESH = pl.DeviceIdType.MESH


def _params(sem=None):
    return pltpu.CompilerParams(dimension_semantics=sem, vmem_limit_bytes=VMEM_LIMIT)


def _dot(a, b):
    return jnp.dot(a, b, preferred_element_type=F32)


def _dot_nt(a, b):
    return lax.dot_general(a, b, (((1,), (1,)), ((), ())), preferred_element_type=F32)


def _split3(x):
    hi = x.astype(BF16)
    rest = x - hi.astype(F32)
    mid = rest.astype(BF16)
    return hi, mid, (rest - mid.astype(F32)).astype(BF16)


def _dot_f32(a, b, exact):
    if exact == "a":
        a16 = a.astype(BF16)
        return sum(_dot(a16, part) for part in _split3(b))
    b16 = b.astype(BF16)
    return sum(_dot(part, b16) for part in _split3(a))


def _sigmoid(x):
    return 1.0 / (1.0 + jnp.exp(-x))


def _log_sigmoid(z):
    return jnp.minimum(z, 0.0) - jnp.log(1.0 + jnp.exp(-jnp.abs(z)))


_GELU_K = math.sqrt(2.0 / math.pi)
_GELU_C = 0.044715


def _gelu(x):
    t = jnp.tanh(_GELU_K * (x + _GELU_C * (x * x * x)))
    return 0.5 * x * (1.0 + t)


def _gelu_grad(x):
    x2 = x * x
    t = jnp.tanh(_GELU_K * (x + _GELU_C * (x2 * x)))
    return 0.5 * (1.0 + t) + 0.5 * x * (1.0 - t * t) * (_GELU_K * (1.0 + 3.0 * _GELU_C * x2))


def _rms_bwd(xh, r, g, dy):
    gy = dy * g
    return r * (gy - xh * jnp.mean(xh * gy, axis=-1, keepdims=True))


def _lane_lt64(shape):
    return lax.broadcasted_iota(jnp.int32, shape, len(shape) - 1) < HEAD_DIM


class _Comm:
    def __init__(self, arrays, out_shapes, sems, start, finish, mid=None):
        self.arrays, self.out_shapes, self.sems = list(arrays), list(out_shapes), list(sems)
        self.start, self.mid, self.finish = start, mid, finish


def _comm_phase(plans, phase, in_refs, out_refs, sem_refs):
    ia = io = ks = 0
    for plan in plans:
        na, no, ns = len(plan.arrays), len(plan.out_shapes), len(plan.sems)
        fn = getattr(plan, phase)
        if fn is not None:
            fn(in_refs[ia:ia + na], out_refs[io:io + no], sem_refs[ks:ks + ns])
        ia, io, ks = ia + na, io + no, ks + ns


def _comm_operands(plans):
    arrays = [a for plan in plans for a in plan.arrays]
    out_shapes = [o for plan in plans for o in plan.out_shapes]
    sems = [s for plan in plans for s in plan.sems]
    return arrays, out_shapes, sems


_ANY = pl.BlockSpec(memory_space=pl.ANY)


def _run_comm(name, plans):
    arrays, out_shapes, sems = _comm_operands(plans)
    n_in, n_out = len(arrays), len(out_shapes)

    def body(*refs):
        parts = refs[:n_in], refs[n_in:n_in + n_out], refs[n_in + n_out:]
        for phase in ("start", "mid", "finish"):
            _comm_phase(plans, phase, *parts)

    return pl.pallas_call(
        body, name=name, out_shape=out_shapes, in_specs=[_ANY] * n_in, out_specs=[_ANY] * n_out, scratch_shapes=sems,
    )(*arrays)


def _gather_plan(shard):
    def setup(ins, outs, sems):
        (x_ref,), (out_ref,), (send_sems, recv_sems, local_sem) = ins, outs, sems
        x, y, c = lax.axis_index("x"), lax.axis_index("y"), lax.axis_index("c")
        me, sibling = (x, y, c), (x, y, 1 - c)
        chips = [(1 - x, y), (x, 1 - y), (1 - x, 1 - y)]

        def rows(px, py, pc):
            return out_ref.at[4 * px + 2 * py + pc]

        def copy(k, block, to, src=None):
            return pltpu.make_async_remote_copy(
                src_ref=rows(*block) if src is None else src,
                dst_ref=rows(*block),
                send_sem=send_sems.at[k],
                recv_sem=recv_sems.at[k],
                device_id=to,
                device_id_type=MESH,
            )

        mine = pltpu.make_async_copy(x_ref, rows(*me), local_sem)
        first = [copy(0, me, sibling, src=x_ref)]
        first += [copy(1 + j, me, (*chip, c), src=x_ref) for j, chip in enumerate(chips)]
        passed = [copy(4 + j, (*chip, c), sibling) for j, chip in enumerate(chips)]
        landed = [copy(1 + j, (*chip, c), me) for j, chip in enumerate(chips)]
        from_sibling = [copy(0, sibling, me)] + [copy(4 + j, (*chip, 1 - c), me) for j, chip in enumerate(chips)]
        return mine, first, passed, landed, from_sibling

    def start(ins, outs, sems):
        mine, first, _, _, _ = setup(ins, outs, sems)
        mine.start()
        for cp in first:
            cp.start()

    def mid(ins, outs, sems):
        _, _, passed, landed, _ = setup(ins, outs, sems)
        for arrived, onward in zip(landed, passed):
            arrived.wait_recv()
            onward.start()

    def finish(ins, outs, sems):
        mine, first, passed, _, from_sibling = setup(ins, outs, sems)
        for cp in from_sibling:
            cp.wait_recv()
        for cp in first + passed:
            cp.wait_send()
        mine.wait()

    return _Comm([shard], [jax.ShapeDtypeStruct((N_DEV,) + shard.shape, shard.dtype)],
                 [pltpu.SemaphoreType.DMA((7,)), pltpu.SemaphoreType.DMA((7,)), pltpu.SemaphoreType.DMA],
                 start, finish, mid)


def _start_all(copies):
    for cp in copies:
        cp.start()


def _wait_all(copies):
    for cp in copies:
        cp.wait_recv()
    for cp in copies:
        cp.wait_send()


def _pair_exchange_plan(grads):
    n = len(grads)

    def copies(ins, outs, sems):
        send_sems, recv_sems = sems
        x, y, c = lax.axis_index("x"), lax.axis_index("y"), lax.axis_index("c")
        return [
            pltpu.make_async_remote_copy(
                src_ref=ins[k].at[:, 1 - c],
                dst_ref=outs[k],
                send_sem=send_sems.at[k],
                recv_sem=recv_sems.at[k],
                device_id=(x, y, 1 - c),
                device_id_type=MESH,
            )
            for k in range(n)
        ]

    return _Comm(grads, [jax.ShapeDtypeStruct((4,) + g.shape[2:], g.dtype) for g in grads],
                 [pltpu.SemaphoreType.DMA((n,)), pltpu.SemaphoreType.DMA((n,))],
                 lambda *refs: _start_all(copies(*refs)), lambda *refs: _wait_all(copies(*refs)))


def _chip_exchange_plan(parts):
    n = len(parts)

    def copies(ins, outs, sems):
        send_sems, recv_sems = sems
        x, y, c = lax.axis_index("x"), lax.axis_index("y"), lax.axis_index("c")
        chips = [(1 - x, y), (x, 1 - y), (1 - x, 1 - y)]
        return [
            pltpu.make_async_remote_copy(
                src_ref=ins[k].at[2 * px + py],
                dst_ref=outs[k].at[j],
                send_sem=send_sems.at[3 * k + j],
                recv_sem=recv_sems.at[3 * k + j],
                device_id=(px, py, c),
                device_id_type=MESH,
            )
            for k in range(n) for j, (px, py) in enumerate(chips)
        ]

    return _Comm(parts, [jax.ShapeDtypeStruct((3,) + p.shape[1:], p.dtype) for p in parts],
                 [pltpu.SemaphoreType.DMA((3 * n,)), pltpu.SemaphoreType.DMA((3 * n,))],
                 lambda *refs: _start_all(copies(*refs)), lambda *refs: _wait_all(copies(*refs)))


def _mm(name, pairs, extras, outs, epi, *, m, tm, n, tn, arbitrary=False, comm=()):
    nj = n // tn
    a_arrays, a_specs, b_arrays, b_specs, b_index = [], [], [], [], []
    for a, b, nt, cols in pairs:
        a_arrays.append(a)
        a_specs.append(pl.BlockSpec((tm, a.shape[1]), lambda i, j: (i, 0)))
        known = [k for k, other in enumerate(b_arrays) if other is b]
        if known:
            b_index.append(known[0])
            continue
        b_index.append(len(b_arrays))
        b_arrays.append(b)
        if cols is not None:
            assert nj == 1
            b_specs.append(pl.BlockSpec(b.shape, lambda i, j: (0, 0)))
        elif nt:
            b_specs.append(pl.BlockSpec((tn, b.shape[1]), lambda i, j: (j, 0)))
        else:
            b_specs.append(pl.BlockSpec((b.shape[0], tn), lambda i, j: (0, j)))
    comm_arrays, comm_outs, comm_sems = _comm_operands(comm)
    arrays = a_arrays + b_arrays + [arr for arr, _ in extras] + comm_arrays
    in_specs = a_specs + b_specs + [spec for _, spec in extras] + [_ANY] * len(comm_arrays)
    n_a, n_b, n_extras, n_ci, n_out, n_co = len(a_arrays), len(b_arrays), len(extras), len(comm_arrays), len(outs), len(comm_outs)
    ni = m // tm

    def body(*refs):
        a_refs = refs[:n_a]
        b_refs = refs[n_a:n_a + n_b]
        ex = refs[n_a + n_b:n_a + n_b + n_extras]
        n_in = n_a + n_b + n_extras + n_ci
        comm_refs = refs[n_in - n_ci:n_in], refs[n_in + n_out:n_in + n_out + n_co], refs[n_in + n_out + n_co:]
        out = refs[n_in:n_in + n_out]
        if comm:
            @pl.when(jnp.logical_and(pl.program_id(0) == 0, pl.program_id(1) == 0))
            def _():
                _comm_phase(comm, "start", *comm_refs)

        accs = []
        for p, (_, _, nt, cols) in enumerate(pairs):
            av = a_refs[p][...]
            if av.dtype != BF16:
                av = av.astype(BF16)
            b_ref = b_refs[b_index[p]]
            if cols is None:
                bv = b_ref[...]
            else:
                bv = b_ref[:, cols[0]:cols[1]] if nt else b_ref[cols[0]:cols[1], :]
            accs.append(_dot_nt(av, bv) if nt else _dot(av, bv))
        epi(accs, ex, out)
        if comm:
            mid_row = ni // 2 if ni >= 3 else ni - 1
            mid_col = 0 if ni >= 3 else nj - 1

            @pl.when(jnp.logical_and(pl.program_id(0) == mid_row, pl.program_id(1) == mid_col))
            def _():
                _comm_phase(comm, "mid", *comm_refs)

            @pl.when(jnp.logical_and(pl.program_id(0) == ni - 1, pl.program_id(1) == nj - 1))
            def _():
                _comm_phase(comm, "finish", *comm_refs)

    sem = ("arbitrary", "arbitrary") if arbitrary or comm else ("parallel", "parallel")
    res = pl.pallas_call(
        body,
        name=name,
        grid=(ni, nj),
        in_specs=in_specs,
        out_specs=[spec for _, spec in outs] + [_ANY] * n_co,
        out_shape=[shape for shape, _ in outs] + comm_outs,
        scratch_shapes=comm_sems,
        compiler_params=_params(sem),
    )(*arrays)
    return (res[:n_out], res[n_out:]) if comm else res


def _tile(tm, tn, off=0):
    return pl.BlockSpec((tm, tn), lambda i, j: (i, j + off))


def _row(tm, w, blk=0):
    return pl.BlockSpec((tm, w), lambda i, j: (i, blk))


def _whole(shape):
    zeros = (0,) * len(shape)
    return pl.BlockSpec(shape, lambda i, j: zeros)


def _sds(shape, dtype):
    return jax.ShapeDtypeStruct(shape, dtype)


def _tile_t(tm, tn):
    return pl.BlockSpec((tn, tm), lambda i, j: (j, i))


def _grad_w(name, a_t, g, *, tk, tn, ts, block_cols=None):
    ka, s_len = a_t.shape
    n = g.shape[1]
    width = tn if block_cols is None else block_cols

    def body(a_ref, g_ref, o_ref):
        first = pl.program_id(2) == 0
        gv = g_ref[...].astype(BF16)
        for b in range(tn // width):
            part = _dot(a_ref[...], gv[:, b * width:(b + 1) * width])
            dst = o_ref if block_cols is None else o_ref.at[b]

            @pl.when(first)
            def _():
                dst[...] = part

            @pl.when(jnp.logical_not(first))
            def _():
                dst[...] += part

    if block_cols is None:
        out_shape = _sds((ka, n), F32)
        out_spec = pl.BlockSpec((tk, tn), lambda i, j, s: (i, j))
    else:
        out_shape = _sds((n // width, ka, width), F32)
        out_spec = pl.BlockSpec((tn // width, tk, width), lambda i, j, s: (j, i, 0))
    return pl.pallas_call(
        body,
        name=name,
        grid=(ka // tk, n // tn, s_len // ts),
        in_specs=[pl.BlockSpec((tk, ts), lambda i, j, s: (i, s)), pl.BlockSpec((ts, tn), lambda i, j, s: (s, j))],
        out_specs=out_spec,
        out_shape=out_shape,
        compiler_params=_params(("parallel", "parallel", "arbitrary")),
    )(a_t, g)


def _rms_fwd(name, x, g, tm, comm=()):
    s_len, d = x.shape
    steps = s_len // tm
    comm_arrays, comm_outs, comm_sems = _comm_operands(comm)
    n_ci, n_co = len(comm_arrays), len(comm_outs)

    def body(x_ref, g_ref, *rest):
        comm_refs = rest[:n_ci], rest[n_ci + 3:n_ci + 3 + n_co], rest[n_ci + 3 + n_co:]
        h_ref, ht_ref, r_ref = rest[n_ci:n_ci + 3]
        for phase, at in (("start", 0), ("mid", steps // 2)):
            if comm:
                @pl.when(pl.program_id(0) == at)
                def _():
                    _comm_phase(comm, phase, *comm_refs)

        xv = x_ref[...]
        r = lax.rsqrt(jnp.mean(xv * xv, axis=-1, keepdims=True) + EPS)
        h = (xv * r * g_ref[...]).astype(BF16)
        h_ref[...] = h
        ht_ref[...] = jnp.transpose(h)
        r_ref[...] = r
        if comm:
            @pl.when(pl.program_id(0) == steps - 1)
            def _():
                _comm_phase(comm, "finish", *comm_refs)

    res = pl.pallas_call(
        body,
        name=name,
        grid=(steps,),
        in_specs=[pl.BlockSpec((tm, d), lambda i: (i, 0)), pl.BlockSpec((1, d), lambda i: (0, 0))] + [_ANY] * n_ci,
        out_specs=[pl.BlockSpec((tm, d), lambda i: (i, 0)), pl.BlockSpec((d, tm), lambda i: (0, i)),
                   pl.BlockSpec((tm, 1), lambda i: (i, 0))] + [_ANY] * n_co,
        out_shape=[_sds((s_len, d), BF16), _sds((d, s_len), BF16), _sds((s_len, 1), F32)] + comm_outs,
        scratch_shapes=comm_sems,
        compiler_params=_params(("arbitrary",) if comm else ("parallel",)),
    )(x, g, *comm_arrays)
    return res[0], res[1], res[2], res[3:]


def _forget_cumsum(prest, b_f_pad, tc):
    s_len = prest.shape[0]

    def body(f_ref, b_ref, c_ref, carry):
        @pl.when(pl.program_id(0) == 0)
        def _():
            carry[...] = jnp.zeros_like(carry)

        logf = _log_sigmoid(f_ref[...] + b_ref[...])
        row = lax.broadcasted_iota(jnp.int32, (tc, tc), 0)
        col = lax.broadcasted_iota(jnp.int32, (tc, tc), 1)
        tri = (row >= col).astype(F32)
        c = _dot_f32(tri, logf, "a") + carry[...]
        c_ref[...] = c
        carry[...] = c[tc - 1:tc, :]

    return pl.pallas_call(
        body,
        name="forget_cumsum",
        grid=(s_len // tc,),
        in_specs=[pl.BlockSpec((tc, LANES), lambda i: (i, 0)), pl.BlockSpec((1, LANES), lambda i: (0, 0))],
        out_specs=pl.BlockSpec((tc, LANES), lambda i: (i, 0)),
        out_shape=_sds((s_len, LANES), F32),
        scratch_shapes=[pltpu.VMEM((1, LANES), F32)],
        compiler_params=_params(("arbitrary",)),
    )(prest, b_f_pad)


def _stack_heads(pair, lt64):
    zero = jnp.zeros_like(pair)
    return jnp.concatenate([jnp.where(lt64, pair, zero), jnp.where(lt64, zero, pair)], axis=0)


def _score_tiles(q_ref, k_ref, ck_ref, st_sc, tk):
    lt64 = _lane_lt64((tk, LANES))
    for p in range(N_HEADS // 2):
        lanes = slice(p * LANES, (p + 1) * LANES)
        q_pair = q_ref[:, lanes] * jnp.asarray(HEAD_DIM ** -0.5, BF16)
        st2 = _dot_nt(_stack_heads(k_ref[:, lanes], lt64), q_pair)
        for half in range(2):
            h = 2 * p + half
            st_sc[h] = st2[half * tk:(half + 1) * tk] - ck_ref[:, h:h + 1]


ROW_CHUNK = 64


def _row_chunks(tk):
    rc = min(ROW_CHUNK, tk)
    return [slice(r, r + rc) for r in range(0, tk, rc)]


def _by_sublane(x):
    return x.reshape(x.shape[0] // 8, 8, x.shape[1])


def _softmax_update(st_sc, p_sc, m_sc, l_sc, tk, tq):
    alphas = []
    for h in range(N_HEADS):
        top8 = jnp.full((8, tq), NEG, F32)
        for rows in _row_chunks(tk):
            top8 = jnp.maximum(top8, jnp.max(_by_sublane(st_sc[h, rows, :]), axis=0))
        m_old = m_sc[h]
        m_new = jnp.maximum(m_old, jnp.max(top8, axis=0, keepdims=True))
        sum8 = jnp.zeros((8, tq), F32)
        for rows in _row_chunks(tk):
            pt = jnp.exp(st_sc[h, rows, :] - m_new)
            p_sc[h, rows, :] = pt.astype(BF16)
            sum8 = sum8 + jnp.sum(_by_sublane(pt), axis=0)
        alpha = jnp.exp(m_old - m_new)
        l_sc[h] = alpha * l_sc[h] + jnp.sum(sum8, axis=0, keepdims=True)
        m_sc[h] = m_new
        alphas.append(alpha)
    return alphas


def _mask_diagonal(st_sc, i, j, tq, tk):
    @pl.when((j + 1) * tk - 1 > i * tq)
    def _():
        key = j * tk + lax.broadcasted_iota(jnp.int32, (tk, tq), 0)
        query = i * tq + lax.broadcasted_iota(jnp.int32, (tk, tq), 1)
        st_sc[...] = jnp.where((query >= key)[None], st_sc[...], NEG)


def _attn_fwd(qkv, v_t, c_col, tq, tk, comm=()):
    s_len = qkv.shape[0]
    ratio = tq // tk
    steps = [(i, j) for i in range(s_len // tq) for j in range((i + 1) * ratio)]
    i_tab = jnp.asarray([i for i, _ in steps], jnp.int32)
    j_tab = jnp.asarray([j for _, j in steps], jnp.int32)

    comm_arrays, comm_outs, comm_sems = _comm_operands(comm)
    n_ci, n_co = len(comm_arrays), len(comm_outs)

    def body(i_ref, j_ref, q_ref, k_ref, vt_ref, ck_ref, *rest):
        comm_refs = rest[:n_ci], rest[n_ci + 3:n_ci + 3 + n_co], rest[n_ci + 3 + n_co + 5:]
        o_ref, ot_ref, lse_ref = rest[n_ci:n_ci + 3]
        acc_t, m_sc, l_sc, st_sc, p_sc = rest[n_ci + 3 + n_co:n_ci + 3 + n_co + 5]
        n = pl.program_id(0)
        i, j = i_ref[n], j_ref[n]
        for phase, at in (("start", 0), ("mid", (2 * len(steps)) // 3)):
            if comm:
                @pl.when(n == at)
                def _():
                    _comm_phase(comm, phase, *comm_refs)

        @pl.when(j == 0)
        def _():
            acc_t[...] = jnp.zeros_like(acc_t)
            m_sc[...] = jnp.full_like(m_sc, NEG)
            l_sc[...] = jnp.zeros_like(l_sc)

        _score_tiles(q_ref, k_ref, ck_ref, st_sc, tk)
        _mask_diagonal(st_sc, i, j, tq, tk)
        alpha = _softmax_update(st_sc, p_sc, m_sc, l_sc, tk, tq)
        top = lax.broadcasted_iota(jnp.int32, (LANES, tq), 0) < HEAD_DIM
        for p in range(N_HEADS // 2):
            lanes = slice(p * LANES, (p + 1) * LANES)
            vt_pair = vt_ref[lanes, :]
            pv = jnp.where(top, _dot(vt_pair, p_sc[2 * p]), _dot(vt_pair, p_sc[2 * p + 1]))
            acc_t[lanes, :] = acc_t[lanes, :] * jnp.where(top, alpha[2 * p], alpha[2 * p + 1]) + pv

        @pl.when(j == (i + 1) * ratio - 1)
        def _():
            for p in range(N_HEADS // 2):
                lanes = slice(p * LANES, (p + 1) * LANES)
                l_pair = jnp.where(top, l_sc[2 * p], l_sc[2 * p + 1])
                o_t = acc_t[lanes, :] / l_pair
                o_ref[:, lanes] = jnp.transpose(o_t)
                ot_ref[lanes, :] = o_t.astype(BF16)
            lse_ref[...] = m_sc[...] + jnp.log(l_sc[...])

        if comm:
            @pl.when(n == len(steps) - 1)
            def _():
                _comm_phase(comm, "finish", *comm_refs)

    stat = pltpu.VMEM((N_HEADS, 1, tq), F32)
    res = pl.pallas_call(
        body,
        name="attn_fwd",
        grid_spec=pltpu.PrefetchScalarGridSpec(
            num_scalar_prefetch=2,
            grid=(len(steps),),
            in_specs=[
                pl.BlockSpec((tq, FOX_W), lambda n, it, jt: (it[n], 0)),
                pl.BlockSpec((tk, FOX_W), lambda n, it, jt: (jt[n], 1)),
                pl.BlockSpec((FOX_W, tk), lambda n, it, jt: (0, jt[n])),
                pl.BlockSpec((tk, LANES), lambda n, it, jt: (jt[n], 0)),
            ] + [_ANY] * n_ci,
            out_specs=[
                pl.BlockSpec((tq, FOX_W), lambda n, it, jt: (it[n], 0)),
                pl.BlockSpec((FOX_W, tq), lambda n, it, jt: (0, it[n])),
                pl.BlockSpec((N_HEADS, 1, tq), lambda n, it, jt: (0, 0, it[n])),
            ] + [_ANY] * n_co,
            scratch_shapes=[pltpu.VMEM((FOX_W, tq), F32), stat, stat, pltpu.VMEM((N_HEADS, tk, tq), F32),
                            pltpu.VMEM((N_HEADS, tk, tq), BF16)] + comm_sems,
        ),
        out_shape=[_sds((s_len, FOX_W), F32), _sds((FOX_W, s_len), BF16), _sds((N_HEADS, 1, s_len), F32)] + comm_outs,
        compiler_params=_params(("arbitrary",)),
    )(i_tab, j_tab, qkv, qkv, v_t, c_col, *comm_arrays)
    return res[0], res[1], res[2], res[3:]


def _sgu_mix(vn, w_stack, lt64):
    outs = []
    for p in range(SGU_G // 2):
        r = _dot(w_stack[p], vn[:, p * LANES:(p + 1) * LANES])
        outs.append(jnp.where(lt64, r[:SGU_LEN], r[SGU_LEN:]))
    return jnp.concatenate(outs, axis=1)


def _sgu_norm(sv, ln_g, ln_b):
    svg = _gelu(sv)
    xc = svg - jnp.mean(svg, axis=-1, keepdims=True)
    rstd = lax.rsqrt(jnp.mean(xc * xc, axis=-1, keepdims=True) + EPS)
    xhat = xc * rstd
    return xhat, rstd, xhat * ln_g + ln_b


def _sgu_fwd(prest, ln_g, ln_b, w_stack, b_pair, tm):
    s_len = prest.shape[0]

    def body(u_ref, sv_ref, g_ref, b_ref, w_ref, bp_ref, sg_ref, sgt_ref):
        lt64 = _lane_lt64((SGU_LEN, LANES))
        _, _, vn = _sgu_norm(sv_ref[...].astype(F32), g_ref[...], b_ref[...])
        vn = vn.astype(BF16)
        w_stack_v = [w_ref[p] for p in range(SGU_G // 2)]
        for w in range(tm // SGU_LEN):
            win = slice(w * SGU_LEN, (w + 1) * SGU_LEN)
            mixed = _sgu_mix(vn[win], w_stack_v, lt64) + bp_ref[...]
            sg = (_gelu(u_ref[win, :].astype(F32)) * mixed).astype(BF16)
            sg_ref[win, :] = sg
            sgt_ref[:, win] = jnp.transpose(sg)

    return pl.pallas_call(
        body,
        name="sgu_fwd",
        grid=(s_len // tm,),
        in_specs=[
            pl.BlockSpec((tm, SGU_W), lambda i: (i, U_OFF // SGU_W)),
            pl.BlockSpec((tm, SGU_W), lambda i: (i, SV_OFF // SGU_W)),
            pl.BlockSpec((1, SGU_W), lambda i: (0, 0)),
            pl.BlockSpec((1, SGU_W), lambda i: (0, 0)),
            pl.BlockSpec((SGU_G // 2, 2 * SGU_LEN, SGU_LEN), lambda i: (0, 0, 0)),
            pl.BlockSpec((SGU_LEN, SGU_W), lambda i: (0, 0)),
        ],
        out_specs=[pl.BlockSpec((tm, SGU_W), lambda i: (i, 0)), pl.BlockSpec((SGU_W, tm), lambda i: (0, i))],
        out_shape=[_sds((s_len, SGU_W), BF16), _sds((SGU_W, s_len), BF16)],
        compiler_params=_params(("parallel",)),
    )(prest, prest, ln_g, ln_b, w_stack, b_pair)


def _sgu_bwd(prest, dyb, w_b_t, ln_g, ln_b, w_stack, wt_stack, b_pair, tm):
    s_len = prest.shape[0]
    n_pair = SGU_G // 2

    def body(u_ref, sv_ref, dyb_ref, wb_ref, g_ref, b_ref, w_ref, wt_ref, bp_ref,
             du_ref, dsv_ref, dw_ref, db_ref, dg_ref, dbeta_ref, dusvt_ref, dvn_sc):
        @pl.when(pl.program_id(0) == 0)
        def _():
            dw_ref[...] = jnp.zeros_like(dw_ref)
            db_ref[...] = jnp.zeros_like(db_ref)
            dg_ref[...] = jnp.zeros_like(dg_ref)
            dbeta_ref[...] = jnp.zeros_like(dbeta_ref)

        lt64 = _lane_lt64((SGU_LEN, LANES))
        sv = sv_ref[...].astype(F32)
        xhat, rstd, vn32 = _sgu_norm(sv, g_ref[...], b_ref[...])
        vn = vn32.astype(BF16)
        w_stack_v = [w_ref[p] for p in range(n_pair)]
        dsg = _dot(dyb_ref[...], wb_ref[...])
        db = jnp.zeros((SGU_LEN, SGU_W), F32)
        for w in range(tm // SGU_LEN):
            win = slice(w * SGU_LEN, (w + 1) * SGU_LEN)
            u = u_ref[win, :].astype(F32)
            dsg_w = dsg[win]
            mixed = _sgu_mix(vn[win], w_stack_v, lt64) + bp_ref[...]
            du = (dsg_w * mixed * _gelu_grad(u)).astype(BF16)
            du_ref[win, :] = du
            dusvt_ref[:SGU_W, win] = jnp.transpose(du)
            dmixed = dsg_w * _gelu(u)
            db = db + dmixed
            dm16 = dmixed.astype(BF16)
            for p in range(n_pair):
                lanes = slice(p * LANES, (p + 1) * LANES)
                dmp = dm16[:, lanes]
                r = _dot(wt_ref[p], dmp)
                dvn_sc[win, lanes] = jnp.where(lt64, r[:SGU_LEN], r[SGU_LEN:])
                zero = jnp.zeros_like(dmp)
                dm_ab = jnp.concatenate([jnp.where(lt64, dmp, zero), jnp.where(lt64, zero, dmp)], axis=0)
                dw_ref[p] += _dot_nt(dm_ab, vn[win, lanes])
        db_ref[...] += db
        dvn = dvn_sc[...]
        dg_ref[...] += jnp.sum(dvn * xhat, axis=0, keepdims=True)
        dbeta_ref[...] += jnp.sum(dvn, axis=0, keepdims=True)
        dxh = dvn * g_ref[...]
        dsvg = rstd * (dxh - jnp.mean(dxh, axis=-1, keepdims=True) - xhat * jnp.mean(dxh * xhat, axis=-1, keepdims=True))
        dsv = (dsvg * _gelu_grad(sv)).astype(BF16)
        dsv_ref[...] = dsv
        dusvt_ref[SGU_W:, :] = jnp.transpose(dsv)

    const2 = lambda i: (0, 0)
    const3 = lambda i: (0, 0, 0)
    return pl.pallas_call(
        body,
        name="sgu_bwd",
        grid=(s_len // tm,),
        in_specs=[
            pl.BlockSpec((tm, SGU_W), lambda i: (i, U_OFF // SGU_W)),
            pl.BlockSpec((tm, SGU_W), lambda i: (i, SV_OFF // SGU_W)),
            pl.BlockSpec((tm, dyb.shape[1]), lambda i: (i, 0)),
            pl.BlockSpec(w_b_t.shape, const2),
            pl.BlockSpec((1, SGU_W), const2),
            pl.BlockSpec((1, SGU_W), const2),
            pl.BlockSpec((n_pair, 2 * SGU_LEN, SGU_LEN), const3),
            pl.BlockSpec((n_pair, 2 * SGU_LEN, SGU_LEN), const3),
            pl.BlockSpec((SGU_LEN, SGU_W), const2),
        ],
        out_specs=[
            pl.BlockSpec((tm, SGU_W), lambda i: (i, 0)),
            pl.BlockSpec((tm, SGU_W), lambda i: (i, 0)),
            pl.BlockSpec((n_pair, 2 * SGU_LEN, SGU_LEN), const3),
            pl.BlockSpec((SGU_LEN, SGU_W), const2),
            pl.BlockSpec((1, SGU_W), const2),
            pl.BlockSpec((1, SGU_W), const2),
            pl.BlockSpec((2 * SGU_W, tm), lambda i: (0, i)),
        ],
        out_shape=[
            _sds((s_len, SGU_W), BF16), _sds((s_len, SGU_W), BF16), _sds((n_pair, 2 * SGU_LEN, SGU_LEN), F32),
            _sds((SGU_LEN, SGU_W), F32), _sds((1, SGU_W), F32), _sds((1, SGU_W), F32),
            _sds((2 * SGU_W, s_len), BF16),
        ],
        scratch_shapes=[pltpu.VMEM((tm, SGU_W), F32)],
        compiler_params=_params(("arbitrary",)),
    )(prest, prest, dyb, w_b_t, ln_g, ln_b, w_stack, wt_stack, b_pair)


def _attn_bwd(qkv, k_t, do, c_col, lse_row, delta_row, tq, tk, comm=()):
    s_len = qkv.shape[0]
    nq, nk = s_len // tq, s_len // tk
    ratio = tq // tk
    scale = HEAD_DIM ** -0.5
    steps = [(j, i) for j in range(nk) for i in range(j // ratio, nq)]
    j_tab = jnp.asarray([j for j, _ in steps], jnp.int32)
    i_tab = jnp.asarray([i for _, i in steps], jnp.int32)

    comm_arrays, comm_outs, comm_sems = _comm_operands(comm)
    n_ci, n_co = len(comm_arrays), len(comm_outs)

    def body(j_ref, i_ref, q_ref, k_ref, v_ref, kt_ref, do_ref, ck_ref, lse_ref, dl_ref, *rest):
        comm_refs = rest[:n_ci], rest[n_ci + 6:n_ci + 6 + n_co], rest[n_ci + 6 + n_co + 8:]
        dq_ref, dk_ref, dv_ref, dcr_ref, dcc_ref, dkvt_ref = rest[n_ci:n_ci + 6]
        dq_t, dk_acc, dv_acc, dcc_acc, st_sc, dpt_sc, p_sc, ds_sc = rest[n_ci + 6 + n_co:n_ci + 6 + n_co + 8]
        n = pl.program_id(0)
        j, i = j_ref[n], i_ref[n]

        @pl.when(n == 0)
        def _():
            _comm_phase(comm, "start", *comm_refs)
            dq_t[...] = jnp.zeros_like(dq_t)
            dcr_ref[...] = jnp.zeros_like(dcr_ref)

        @pl.when(i == j // ratio)
        def _():
            dk_acc[...] = jnp.zeros_like(dk_acc)
            dv_acc[...] = jnp.zeros_like(dv_acc)
            dcc_acc[...] = jnp.zeros_like(dcc_acc)

        lt64 = _lane_lt64((tk, LANES))
        _score_tiles(q_ref, k_ref, ck_ref, st_sc, tk)
        for p in range(N_HEADS // 2):
            lanes = slice(p * LANES, (p + 1) * LANES)
            dpt2 = _dot_nt(_stack_heads(v_ref[:, lanes], lt64), do_ref[:, lanes].astype(BF16))
            dpt_sc[2 * p] = dpt2[:tk]
            dpt_sc[2 * p + 1] = dpt2[tk:]
        _mask_diagonal(st_sc, i, j, tq, tk)

        pt = jnp.exp(st_sc[...] - lse_ref[...])
        dst = pt * (dpt_sc[...] - dl_ref[...])
        p_sc[...] = pt.astype(BF16)
        ds_sc[...] = dst.astype(BF16)
        dcr_ref[i] += jnp.sum(dst, axis=1, keepdims=True)
        col_sums = jnp.sum(dst, axis=2, keepdims=True)
        lane = lax.broadcasted_iota(jnp.int32, (tk, LANES), 1)
        dcc = jnp.zeros((tk, LANES), F32)
        for h in range(N_HEADS):
            dcc = jnp.where(lane == h, -col_sums[h], dcc)
        dcc_acc[...] += dcc

        for p in range(N_HEADS // 2):
            lanes = slice(p * LANES, (p + 1) * LANES)
            q_pair = q_ref[:, lanes] * jnp.asarray(scale, BF16)
            dv2 = _dot(p_sc[2 * p:2 * p + 2].reshape(2 * tk, tq), do_ref[:, lanes].astype(BF16))
            dv_acc[:, lanes] += jnp.where(lt64, dv2[:tk], dv2[tk:])
            dk2 = _dot(ds_sc[2 * p:2 * p + 2].reshape(2 * tk, tq), q_pair)
            dk_acc[:, lanes] += jnp.where(lt64, dk2[:tk], dk2[tk:])
            dq2 = _dot(kt_ref[lanes, :], jnp.concatenate([ds_sc[2 * p], ds_sc[2 * p + 1]], axis=1))
            top = lax.broadcasted_iota(jnp.int32, (LANES, tq), 0) < HEAD_DIM
            dq_t[i, lanes, :] += jnp.where(top, dq2[:, :tq], dq2[:, tq:])

        @pl.when(j == (i + 1) * ratio - 1)
        def _():
            rows = pl.ds(pl.multiple_of(i * tq, tq), tq)
            for p in range(N_HEADS // 2):
                lanes = slice(p * LANES, (p + 1) * LANES)
                dq_ref[rows, lanes] = (jnp.transpose(dq_t[i, lanes, :]) * scale).astype(BF16)

        @pl.when(i == nq - 1)
        def _():
            dk16, dv16 = dk_acc[...].astype(BF16), dv_acc[...].astype(BF16)
            dk_ref[...] = dk16
            dv_ref[...] = dv16
            dkvt_ref[:FOX_W, :] = jnp.transpose(dk16)
            dkvt_ref[FOX_W:, :] = jnp.transpose(dv16)
            dcc_ref[...] = dcc_acc[...]

        if comm:
            @pl.when(n == len(steps) // 2)
            def _():
                _comm_phase(comm, "mid", *comm_refs)

            @pl.when(n == len(steps) - 1)
            def _():
                _comm_phase(comm, "finish", *comm_refs)

    q_map = lambda n, jt, it: (it[n], 0)
    q_stat = lambda n, jt, it: (0, 0, it[n])
    k_map = lambda n, jt, it: (jt[n], 0)
    tile = (N_HEADS, tk, tq)
    res = pl.pallas_call(
        body,
        name="attn_bwd",
        grid_spec=pltpu.PrefetchScalarGridSpec(
            num_scalar_prefetch=2,
            grid=(len(steps),),
            in_specs=[
                pl.BlockSpec((tq, FOX_W), q_map),
                pl.BlockSpec((tk, FOX_W), lambda n, jt, it: (jt[n], 1)),
                pl.BlockSpec((tk, FOX_W), lambda n, jt, it: (jt[n], 2)),
                pl.BlockSpec((FOX_W, tk), lambda n, jt, it: (0, jt[n])),
                pl.BlockSpec((tq, FOX_W), q_map),
                pl.BlockSpec((tk, LANES), k_map),
                pl.BlockSpec((N_HEADS, 1, tq), q_stat),
                pl.BlockSpec((N_HEADS, 1, tq), q_stat),
            ] + [_ANY] * n_ci,
            out_specs=[
                pl.BlockSpec((s_len, FOX_W), lambda n, jt, it: (0, 0)),
                pl.BlockSpec((tk, FOX_W), k_map),
                pl.BlockSpec((tk, FOX_W), k_map),
                pl.BlockSpec((nq, N_HEADS, 1, tq), lambda n, jt, it: (0, 0, 0, 0)),
                pl.BlockSpec((tk, LANES), k_map),
                pl.BlockSpec((2 * FOX_W, tk), lambda n, jt, it: (0, jt[n])),
            ] + [_ANY] * n_co,
            scratch_shapes=[pltpu.VMEM((nq, FOX_W, tq), F32), pltpu.VMEM((tk, FOX_W), F32), pltpu.VMEM((tk, FOX_W), F32),
                            pltpu.VMEM((tk, LANES), F32), pltpu.VMEM(tile, F32), pltpu.VMEM(tile, F32),
                            pltpu.VMEM(tile, BF16), pltpu.VMEM(tile, BF16)] + comm_sems,
        ),
        out_shape=[_sds((s_len, FOX_W), BF16), _sds((s_len, FOX_W), BF16), _sds((s_len, FOX_W), BF16),
                   _sds((nq, N_HEADS, 1, tq), F32), _sds((s_len, LANES), F32),
                   _sds((2 * FOX_W, s_len), BF16)] + comm_outs,
        compiler_params=_params(("arbitrary",)),
    )(j_tab, i_tab, qkv, qkv, qkv, k_t, do, c_col, lse_row, delta_row, *comm_arrays)
    return res[:6], res[6:]


def _forget_bwd(dc_rows, dc_cols, prest, b_f_pad, tc):
    s_len = dc_rows.shape[0]
    nb = s_len // tc

    def body(dcr_ref, dc_ref, f_ref, b_ref, df_ref, db_ref, dft_ref, carry):
        @pl.when(pl.program_id(0) == 0)
        def _():
            carry[...] = jnp.zeros_like(carry)
            db_ref[...] = jnp.zeros_like(db_ref)

        row = lax.broadcasted_iota(jnp.int32, (tc, tc), 0)
        col = lax.broadcasted_iota(jnp.int32, (tc, tc), 1)
        tri = (row <= col).astype(F32)
        dlogf = _dot_f32(tri, dcr_ref[...] + dc_ref[...], "a") + carry[...]
        carry[...] = dlogf[0:1, :]
        z = f_ref[...] + b_ref[...]
        lane = lax.broadcasted_iota(jnp.int32, (tc, LANES), 1)
        dz = jnp.where(lane < N_HEADS, dlogf * _sigmoid(-z), 0.0)
        df_ref[...] = dz.astype(BF16)
        dft_ref[...] = jnp.transpose(dz).astype(BF16)
        db_ref[...] += jnp.sum(dz, axis=0, keepdims=True)

    rev = lambda i: (nb - 1 - i, 0)
    return pl.pallas_call(
        body,
        name="forget_bwd",
        grid=(nb,),
        in_specs=[
            pl.BlockSpec((tc, LANES), rev),
            pl.BlockSpec((tc, LANES), rev),
            pl.BlockSpec((tc, LANES), rev),
            pl.BlockSpec((1, LANES), lambda i: (0, 0)),
        ],
        out_specs=[pl.BlockSpec((tc, LANES), rev), pl.BlockSpec((1, LANES), lambda i: (0, 0)),
                   pl.BlockSpec((LANES, tc), lambda i: (0, nb - 1 - i))],
        out_shape=[_sds((s_len, LANES), BF16), _sds((1, LANES), F32), _sds((LANES, s_len), BF16)],
        scratch_shapes=[pltpu.VMEM((1, LANES), F32)],
        compiler_params=_params(("arbitrary",)),
    )(dc_rows, dc_cols, prest, b_f_pad)


def _pair_sum(name, g4, recv, idx, tr):
    _, _, r, c = g4.shape

    def body(idx_ref, g_ref, r_ref, p16_ref, own_ref):
        k = pl.program_id(1)
        s = g_ref[...] + r_ref[...]
        p16_ref[...] = s.astype(BF16)

        @pl.when(k == idx_ref[1])
        def _():
            own_ref[...] = s

    return pl.pallas_call(
        body,
        name=name,
        grid_spec=pltpu.PrefetchScalarGridSpec(
            num_scalar_prefetch=1,
            grid=(r // tr, 4),
            in_specs=[
                pl.BlockSpec((None, None, tr, c), lambda i, k, idx: (k, idx[0], i, 0)),
                pl.BlockSpec((None, tr, c), lambda i, k, idx: (k, i, 0)),
            ],
            out_specs=[
                pl.BlockSpec((None, tr, c), lambda i, k, idx: (k, i, 0)),
                pl.BlockSpec((tr, c), lambda i, k, idx: (i, 0)),
            ],
        ),
        out_shape=[_sds((4, r, c), BF16), _sds((r, c), F32)],
        compiler_params=_params(("parallel", "arbitrary")),
    )(idx, g4, recv)


def _adamw_math(w, g, m, v):
    m2 = ADAM_B1 * m + (1.0 - ADAM_B1) * g
    v2 = ADAM_B2 * v + (1.0 - ADAM_B2) * (g * g)
    m_hat = m2 / (1.0 - ADAM_B1 ** ADAM_STEP)
    v_hat = v2 / (1.0 - ADAM_B2 ** ADAM_STEP)
    delta = -ADAM_LR * (m_hat / (jnp.sqrt(v_hat) + ADAM_EPS) + ADAM_WD * w)
    return delta, m2, v2


def _adamw_shard(name, own, recv, w, m, v, tr):
    r, c = own.shape

    def body(own_ref, recv_ref, w_ref, m_ref, v_ref, g_ref, d_ref, m2_ref, v2_ref):
        g = own_ref[...]
        for k in range(3):
            g = g + recv_ref[k].astype(F32)
        delta, m2, v2 = _adamw_math(w_ref[...], g, m_ref[...], v_ref[...])
        g_ref[...] = g
        d_ref[...] = delta
        m2_ref[...] = m2
        v2_ref[...] = v2

    spec = pl.BlockSpec((tr, c), lambda i: (i, 0))
    return pl.pallas_call(
        body,
        name=name,
        grid=(r // tr,),
        in_specs=[spec, pl.BlockSpec((3, tr, c), lambda i: (0, i, 0)), spec, spec, spec],
        out_specs=[spec] * 4,
        out_shape=[_sds((r, c), F32)] * 4,
        compiler_params=_params(("parallel",)),
    )(own, recv, w, m, v)


def _adamw_small(name, gathered, first_row, w, m, v):
    r = w.shape[0]
    assert first_row % r == 0

    def body(ga_ref, w_ref, m_ref, v_ref, g_ref, d_ref, m2_ref, v2_ref):
        g = ga_ref[0]
        for k in range(1, N_DEV):
            g = g + ga_ref[k]
        delta, m2, v2 = _adamw_math(w_ref[...], g, m_ref[...], v_ref[...])
        g_ref[...] = g
        d_ref[...] = delta
        m2_ref[...] = m2
        v2_ref[...] = v2

    spec = pl.BlockSpec((r, LANES), lambda i: (0, 0))
    return pl.pallas_call(
        body,
        name=name,
        grid=(1,),
        in_specs=[pl.BlockSpec((N_DEV, r, LANES), lambda i: (0, first_row // r, 0)), spec, spec, spec],
        out_specs=[spec] * 4,
        out_shape=[_sds((r, LANES), F32)] * 4,
        compiler_params=_params(("arbitrary",)),
    )(gathered, w, m, v)


_TINY_EARLY = (("b_sgu", (1, SGU_G, SGU_LEN)), ("norm2_g", (1, D_MODEL)), ("normf_g", (D_MODEL,)),
               ("ln_v_g", (1, SGU_W)), ("ln_v_b", (1, SGU_W)))
_TINY_LATE = (("b_f", (1, N_HEADS)), ("norm1_g", (1, D_MODEL)), ("loss", ()))


def _pack_rows(values):
    rows = []
    for val in values:
        flat = val.reshape(-1).astype(F32)
        pad = (-flat.shape[0]) % LANES
        rows.append(jnp.pad(flat, (0, pad)).reshape(-1, LANES))
    packed = jnp.concatenate(rows, axis=0)
    return jnp.pad(packed, ((0, (-packed.shape[0]) % 8), (0, 0)))


def _unpack_rows(packed, group):
    out, row = {}, 0
    for name, shape in group:
        size = math.prod(shape)
        n_rows = -(-size // LANES)
        out[name] = packed[row:row + n_rows].reshape(-1)[:size].reshape(shape)
        row += n_rows
    return out


def kernel(x, norm1_g, w_in, b_f, ln_v_g, ln_v_b, w_sgu, b_sgu, w_a, w_b, w_o, norm2_g, w_up, w_down, normf_g, loss_target, m_norm1_g, m_w_in, m_b_f, m_ln_v_g, m_ln_v_b, m_w_sgu, m_b_sgu, m_w_a, m_w_b, m_w_o, m_norm2_g, m_w_up, m_w_down, m_normf_g, v_norm1_g, v_w_in, v_b_f, v_ln_v_g, v_ln_v_b, v_w_sgu, v_b_sgu, v_w_a, v_w_b, v_w_o, v_norm2_g, v_w_up, v_w_down, v_normf_g):
    xs = x[0]
    target = loss_target[0]
    s_len, d = xs.shape
    tm = min(512, s_len)
    tl = min(1024, s_len)
    tr = min(512, s_len)
    ta = min(512, s_len)
    tc = min(512, s_len)

    w_in_t = jnp.transpose(w_in[0])
    lin = (IN_SHARD * d // LANES, LANES)
    big = (w_in_t.reshape(lin), w_a[0], w_b[0], w_o[0], w_up[0], w_down[0])
    h, h_t, r1, (w_in_g,) = _rms_fwd("rms1", xs, norm1_g, tm, comm=[_gather_plan(w_in_t.astype(BF16))])
    w_in_f = w_in_g.reshape(IN_COLS, d)
    later_shards = [jnp.transpose(w_a[0]), jnp.transpose(w_b[0]), w_o[0], jnp.transpose(w_up[0]), w_down[0]]
    later_plans = [_gather_plan(w.astype(BF16)) for w in later_shards]

    def unflatten(gathered):
        return [g.reshape(N_DEV * g.shape[1], g.shape[2]) for g in gathered]

    w_qkv = w_in_f[:QKV_W]
    f_lo = QKV_W
    u_lo = f_lo + N_HEADS
    w_rest = jnp.concatenate([w_in_f[u_lo:], jnp.pad(w_in_f[f_lo:u_lo], ((0, LANES - N_HEADS), (0, 0)))], axis=0)

    chunk_id = jnp.arange(SGU_LEN) // CHUNK
    sgu_mask = chunk_id[None, :] <= chunk_id[:, None]
    w_masked = jnp.where(sgu_mask[None], w_sgu[0], 0.0)
    w_stack = w_masked.reshape(SGU_G // 2, 2 * SGU_LEN, SGU_LEN).astype(BF16)
    wt_stack = jnp.transpose(w_masked, (0, 2, 1)).reshape(SGU_G // 2, 2 * SGU_LEN, SGU_LEN).astype(BF16)
    b_pair = jnp.transpose(jnp.repeat(b_sgu[0], SGU_W // SGU_G, axis=0))
    b_f_pad = jnp.pad(b_f, ((0, 0), (0, LANES - N_HEADS)))
    head_sel = (jnp.arange(FOX_W)[:, None] // HEAD_DIM == jnp.arange(LANES)[None, :]).astype(F32)

    def qkv_epi(accs, ex, out):
        tile = accs[0].astype(BF16)
        out[0][...] = tile
        out[1][...] = jnp.transpose(tile[:, FOX_W:2 * FOX_W])
        out[2][...] = jnp.transpose(tile[:, 2 * FOX_W:])

    t_spec = pl.BlockSpec((FOX_W, tm), lambda i, j: (0, i))
    qkv, k_t, v_t = _mm("proj_qkv", [(h, w_qkv, True, None)], [],
                        [(_sds((s_len, QKV_W), BF16), _tile(tm, QKV_W)), (_sds((FOX_W, s_len), BF16), t_spec),
                         (_sds((FOX_W, s_len), BF16), t_spec)],
                        qkv_epi, m=s_len, tm=tm, n=QKV_W, tn=QKV_W)
    def rest_epi(accs, ex, out):
        out[0][...] = accs[0].astype(BF16)
        out[1][...] = accs[0][:, F_OFF:F_OFF + LANES]

    prest, f_logit = _mm("proj_rest", [(h, w_rest, True, None)], [],
                         [(_sds((s_len, REST_W), BF16), _tile(tm, REST_W)), (_sds((s_len, LANES), F32), _row(tm, LANES))],
                         rest_epi, m=s_len, tm=tm, n=REST_W, tn=REST_W)

    c_col = _forget_cumsum(f_logit, b_f_pad, tc)
    o, o_t, lse_row, later_g = _attn_fwd(qkv, v_t, c_col, ta, ta, comm=later_plans)
    w_a_t, w_b_t, w_o_f, w_up_t, w_down_f = unflatten(later_g)
    sg, sg_t = _sgu_fwd(prest, ln_v_g, ln_v_b, w_stack, b_pair, tm)

    def merge_epi(accs, ex, out):
        ya, yb = accs
        sa, sb = _sigmoid(ex[0][...].astype(F32)), _sigmoid(ex[1][...].astype(F32))
        merged = (sa * ya + sb * yb).astype(BF16)
        out[0][...] = merged
        out[1][...] = ya.astype(BF16)
        out[2][...] = yb.astype(BF16)
        out[3][...] = jnp.transpose(merged)

    merged, ya, yb, merged_t = _mm(
        "merge", [(o, w_a_t, True, None), (sg, w_b_t, True, None)],
        [(prest, _tile(tm, d, GA_OFF // d)), (prest, _tile(tm, d, GB_OFF // d))],
        [(_sds((s_len, d), BF16), _tile(tm, d))] * 3 + [(_sds((d, s_len), BF16), _tile_t(tm, d))],
        merge_epi, m=s_len, tm=tm, n=d, tn=d)

    def resid_epi(accs, ex, out):
        x1v = ex[0][...] + accs[0]
        out[0][...] = x1v
        r = lax.rsqrt(jnp.mean(x1v * x1v, axis=-1, keepdims=True) + EPS)
        h2v = (x1v * r * ex[1][...]).astype(BF16)
        out[1][...] = h2v
        out[2][...] = jnp.transpose(h2v)
        out[3][...] = r

    x1, h2, h2_t, r2 = _mm(
        "out_proj", [(merged, w_o_f, False, None)], [(xs, _tile(tm, d)), (norm2_g, _whole((1, d)))],
        [(_sds((s_len, d), F32), _tile(tm, d)), (_sds((s_len, d), BF16), _tile(tm, d)),
         (_sds((d, s_len), BF16), _tile_t(tm, d)), (_sds((s_len, 1), F32), _row(tm, 1))],
        resid_epi, m=s_len, tm=tm, n=d, tn=d)

    def up_epi(accs, ex, out):
        act = jnp.square(jnp.maximum(accs[0], 0.0)).astype(BF16)
        out[0][...] = act
        out[1][...] = jnp.transpose(act)

    act, act_t = _mm(
        "mlp_up", [(h2, w_up_t, True, None)], [],
        [(_sds((s_len, D_FF), BF16), _tile(tl, 1024)), (_sds((D_FF, s_len), BF16), _tile_t(tl, 1024))],
        up_epi, m=s_len, tm=tl, n=D_FF, tn=1024)

    def first_step():
        return jnp.logical_and(pl.program_id(0) == 0, pl.program_id(1) == 0)

    def accumulate(ref, val):
        @pl.when(first_step())
        def _():
            ref[...] = val

        @pl.when(jnp.logical_not(first_step()))
        def _():
            ref[...] += val

    def final_epi(accs, ex, out):
        x1_ref, t_ref, g_ref = ex
        x2 = x1_ref[...] + accs[0]
        rf = lax.rsqrt(jnp.mean(x2 * x2, axis=-1, keepdims=True) + EPS)
        xh = x2 * rf
        gf = g_ref[...]
        err = xh * gf - t_ref[...]
        dy = err * (1.0 / d)
        dx2 = _rms_bwd(xh, rf, gf, dy)
        out[0][...] = dx2
        accumulate(out[1], jnp.sum(dy * xh, axis=0, keepdims=True))
        part = 0.5 * jnp.sum(jnp.sum(err * err, axis=-1, keepdims=True) * (1.0 / d), axis=0, keepdims=True)
        accumulate(out[2], jnp.broadcast_to(part, (1, LANES)))
        out[3][...] = dx2.astype(BF16)

    gf2 = normf_g.reshape(1, d)
    dx2, g_normf, loss_part, dx2_16 = _mm(
        "mlp_down_loss", [(act, w_down_f, False, None)],
        [(x1, _row(tr, d)), (target, _row(tr, d)), (gf2, _whole((1, d)))],
        [(_sds((s_len, d), F32), _row(tr, d)), (_sds((1, d), F32), _whole((1, d))), (_sds((1, LANES), F32), _whole((1, LANES))),
         (_sds((s_len, d), BF16), _row(tr, d))],
        final_epi, m=s_len, tm=tr, n=d, tn=d, arbitrary=True)

    def dact_epi(accs, ex, out):
        out[0][...] = (accs[0] * (2.0 * jnp.sqrt(ex[0][...].astype(F32)))).astype(BF16)

    (da,) = _mm("mlp_down_bwd", [(dx2_16, w_down_f, True, None)], [(act, _tile(tl, 1024))],
                [(_sds((s_len, D_FF), BF16), _tile(tl, 1024))], dact_epi, m=s_len, tm=tl, n=D_FF, tn=1024)
    g_down = _grad_w("grad_w_down", act_t, dx2_16, tk=1024, tn=d, ts=min(2048, s_len))
    g_up = _grad_w("grad_w_up", h2_t, da, tk=d, tn=1024, ts=min(2048, s_len), block_cols=D_FF // N_DEV)

    def dh2_epi(accs, ex, out):
        x1_ref, r_ref, g_ref, dx2_ref = ex
        r = r_ref[...]
        xh = x1_ref[...] * r
        dh2 = accs[0]
        out[0][...] = dx2_ref[...] + _rms_bwd(xh, r, g_ref[...], dh2)
        accumulate(out[1], jnp.sum(dh2 * xh, axis=0, keepdims=True))

    my_c = lax.axis_index("c")
    my_chip = 2 * lax.axis_index("x") + lax.axis_index("y")
    idx = jnp.stack([my_c, my_chip]).astype(jnp.int32)
    parts16, owns = {}, {}

    def split_cores(g8):
        return g8.reshape((4, 2) + g8.shape[1:])

    def row_tile(r):
        return 512 if r % 512 == 0 else r

    def pair_sums(names, grads4, from_sibling):
        for name, g4, recv in zip(names, grads4, from_sibling):
            parts16[name], owns[name] = _pair_sum("grad_pair_sum_" + name, g4, recv, idx, row_tile(g4.shape[2]))

    grads4_mlp = [split_cores(g_up), split_cores(g_down.reshape(N_DEV, D_FF // N_DEV, d))]
    (dx1, g_norm2), from_sibling = _mm(
        "mlp_up_bwd", [(da, w_up_t, False, None)],
        [(x1, _row(tr, d)), (r2, _row(tr, 1)), (norm2_g, _whole((1, d))), (dx2, _row(tr, d))],
        [(_sds((s_len, d), F32), _row(tr, d)), (_sds((1, d), F32), _whole((1, d)))],
        dh2_epi, m=s_len, tm=tr, n=d, tn=d, arbitrary=True, comm=[_pair_exchange_plan(grads4_mlp)])
    pair_sums(("w_up", "w_down"), grads4_mlp, from_sibling)

    def dmerge_epi(accs, ex, out):
        dm = accs[0]
        sa, sb = _sigmoid(ex[0][...].astype(F32)), _sigmoid(ex[1][...].astype(F32))
        out[0][...] = (dm * sa).astype(BF16)
        out[1][...] = (dm * sb).astype(BF16)
        dga = (dm * ex[2][...] * sa * (1.0 - sa)).astype(BF16)
        dgb = (dm * ex[3][...] * sb * (1.0 - sb)).astype(BF16)
        out[2][...] = dga
        out[3][...] = dgb
        out[4][:d, :] = jnp.transpose(dga)
        out[4][d:, :] = jnp.transpose(dgb)

    dya, dyb, dga, dgb, dg_t = _mm(
        "out_proj_bwd", [(dx1, w_o_f, True, None)],
        [(prest, _tile(tm, d, GA_OFF // d)), (prest, _tile(tm, d, GB_OFF // d)), (ya, _tile(tm, d)), (yb, _tile(tm, d))],
        [(_sds((s_len, d), BF16), _tile(tm, d))] * 4
        + [(_sds((2 * d, s_len), BF16), pl.BlockSpec((2 * d, tm), lambda i, j: (0, i)))],
        dmerge_epi, m=s_len, tm=tm, n=d, tn=d)
    g_o = _grad_w("grad_w_o", merged_t, dx1, tk=d, tn=d, ts=min(2048, s_len)).reshape(N_DEV, d // N_DEV, d)
    def col_blocks(g):
        return jnp.transpose(g.reshape(g.shape[0], N_DEV, g.shape[1] // N_DEV), (1, 0, 2))

    g_a = col_blocks(_grad_w("grad_w_a", o_t, dya, tk=FOX_W, tn=d, ts=min(2048, s_len)))
    g_b = col_blocks(_grad_w("grad_w_b", sg_t, dyb, tk=SGU_W, tn=d, ts=min(2048, s_len)))

    def do_epi(accs, ex, out):
        do = accs[0]
        out[0][...] = do
        out[1][...] = _dot_f32(do * ex[0][...], ex[1][...], "b")

    grads4_mix = [split_cores(g) for g in (g_a, g_b, g_o)]
    (do, delta), from_sibling = _mm(
        "attn_out_bwd", [(dya, w_a_t, False, None)], [(o, _row(tm, FOX_W)), (head_sel, _whole((FOX_W, LANES)))],
        [(_sds((s_len, FOX_W), F32), _row(tm, FOX_W)), (_sds((s_len, LANES), F32), _row(tm, LANES))],
        do_epi, m=s_len, tm=tm, n=FOX_W, tn=FOX_W, comm=[_pair_exchange_plan(grads4_mix)])
    pair_sums(("w_a", "w_b", "w_o"), grads4_mix, from_sibling)

    du, dsv, dw_pairs, db_pos, g_ln_g, g_ln_b, dusv_t = _sgu_bwd(
        prest, dyb, w_b_t, ln_v_g, ln_v_b, w_stack, wt_stack, b_pair, tm)
    g_w_sgu = jnp.where(sgu_mask[None], dw_pairs.reshape(SGU_G, SGU_LEN, SGU_LEN), 0.0)
    g_b_sgu = jnp.transpose(jnp.sum(db_pos.reshape(SGU_LEN, SGU_G, SGU_W // SGU_G), axis=-1))

    delta_row = jnp.transpose(delta[:, :N_HEADS]).reshape(N_HEADS, 1, s_len)
    early = ("w_a", "w_b", "w_o", "w_up", "w_down")
    small_early = _pack_rows((g_w_sgu, g_b_sgu, g_norm2, g_normf, g_ln_g, g_ln_b))
    (dq, dk, dv, dc_rows_blk, dc_cols, dkv_t), (small_early_all, *from_chips_early) = _attn_bwd(
        qkv, k_t, do, c_col, lse_row, delta_row, ta, ta,
        comm=[_gather_plan(small_early), _chip_exchange_plan([parts16[n] for n in early])])
    dc_rows = jnp.transpose(dc_rows_blk.reshape(s_len // ta, N_HEADS, ta), (0, 2, 1)).reshape(s_len, N_HEADS)
    dc_rows = jnp.pad(dc_rows, ((0, 0), (0, LANES - N_HEADS)))
    dfl, g_bf, dfl_t = _forget_bwd(dc_rows, dc_cols, f_logit, b_f_pad, tc)

    dp_t = (jnp.transpose(dq), dkv_t, dfl_t, dusv_t, dg_t)
    g_in_rows = [_grad_w("grad_w_in_%d" % k, seg_t, h, tk=min(seg_t.shape[0], 1024), tn=d, ts=min(2048, s_len))
                 for k, seg_t in enumerate(dp_t)]
    g_in_rows[2] = g_in_rows[2][:N_HEADS]
    g_in = jnp.concatenate(g_in_rows, axis=0).reshape((N_DEV,) + lin)

    def dx_epi(accs, ex, out):
        x_ref, r_ref, g_ref, dx1_ref = ex
        dh = accs[0]
        for extra in accs[1:]:
            dh = dh + extra
        r = r_ref[...]
        xh = x_ref[...] * r
        out[0][...] = dx1_ref[...] + _rms_bwd(xh, r, g_ref[...], dh)
        accumulate(out[1], jnp.sum(dh * xh, axis=0, keepdims=True))

    rest_cols = ((du, U_OFF, 512), (dsv, SV_OFF, 512), (dga, GA_OFF, 1024), (dgb, GB_OFF, 1024), (dfl, F_OFF, LANES))
    dx_pairs = [(seg, w_qkv, False, (512 * k, 512 * (k + 1))) for k, seg in enumerate((dq, dk, dv))]
    dx_pairs += [(seg, w_rest, False, (lo, lo + width)) for seg, lo, width in rest_cols]
    grads4_in = [split_cores(g_in)]
    pair_sums(("w_in",), grads4_in, _run_comm("grad_pair_exchange_w_in", [_pair_exchange_plan(grads4_in)]))
    (grad_x, g_norm1), (from_chips_in,) = _mm(
        "proj_bwd", dx_pairs,
        [(xs, _row(tr, d)), (r1, _row(tr, 1)), (norm1_g, _whole((1, d))), (dx1, _row(tr, d))],
        [(_sds((s_len, d), F32), _row(tr, d)), (_sds((1, d), F32), _whole((1, d)))],
        dx_epi, m=s_len, tm=tr, n=d, tn=d, arbitrary=True, comm=[_chip_exchange_plan([parts16["w_in"]])])
    small_late = _pack_rows((g_bf[:, :N_HEADS], g_norm1, loss_part[0, 0]))
    (small_late_all,) = _run_comm("gather_last_grads", [_gather_plan(small_late)])
    from_chips = dict(zip(early, from_chips_early), w_in=from_chips_in)

    names = ("w_in", "w_a", "w_b", "w_o", "w_up", "w_down")
    moments_m = (m_w_in, m_w_a, m_w_b, m_w_o, m_w_up, m_w_down)
    moments_v = (v_w_in, v_w_a, v_w_b, v_w_o, v_w_up, v_w_down)
    big_out = {}
    for name, w, m, v in zip(names, big, moments_m, moments_v):
        own = owns[name]
        transposed = name == "w_in"
        m0, v0 = (jnp.transpose(m[0]).reshape(lin), jnp.transpose(v[0]).reshape(lin)) if transposed else (m[0], v[0])
        res = _adamw_shard("adamw_" + name, own, from_chips[name], w, m0, v0, row_tile(own.shape[0]))
        big_out[name] = [(jnp.transpose(t.reshape(IN_SHARD, d)) if transposed else t)[None] for t in res]

    zero = jnp.zeros((), F32)
    sgu_rows = (SGU_G * SGU_LEN, LANES)
    res_sgu = _adamw_small("adamw_w_sgu", small_early_all, 0, w_sgu.reshape(sgu_rows), m_w_sgu.reshape(sgu_rows),
                           v_w_sgu.reshape(sgu_rows))
    small_out = {"w_sgu": [t.reshape(w_sgu.shape) for t in res_sgu]}
    res_early = _adamw_small(
        "adamw_tiny_early", small_early_all, sgu_rows[0], _pack_rows((b_sgu, norm2_g, normf_g, ln_v_g, ln_v_b)),
        _pack_rows((m_b_sgu, m_norm2_g, m_normf_g, m_ln_v_g, m_ln_v_b)),
        _pack_rows((v_b_sgu, v_norm2_g, v_normf_g, v_ln_v_g, v_ln_v_b)))
    res_late = _adamw_small(
        "adamw_tiny_late", small_late_all, 0, _pack_rows((b_f, norm1_g, zero)), _pack_rows((m_b_f, m_norm1_g, zero)),
        _pack_rows((v_b_f, v_norm1_g, zero)))
    for res, group in ((res_early, _TINY_EARLY), (res_late, _TINY_LATE)):
        unpacked = [_unpack_rows(t, group) for t in res]
        small_out.update({name: [u[name] for u in unpacked] for name, _ in group})
    loss = small_out["loss"][0]

    order = ("norm1_g", "w_in", "b_f", "ln_v_g", "ln_v_b", "w_sgu", "b_sgu", "w_a", "w_b", "w_o", "norm2_g", "w_up",
             "w_down", "normf_g")
    table = {**big_out, **small_out}
    outs = [loss, grad_x[None]]
    for kind in range(4):
        outs += [table[n][kind] for n in order]
    return tuple(outs)
```

```python
import math

import jax
import jax.numpy as jnp
from jax import lax
from jax.experimental import pallas as pl
from jax.experimental.pallas import tpu as pltpu

F32 = jnp.float32
BF16 = jnp.bfloat16

N_DEV = 8
D_MODEL = 1024
N_HEADS = 8
HEAD_DIM = 64
FOX_W = N_HEADS * HEAD_DIM
SGU_G = 8
SGU_W = 512
SGU_LEN = 128
CHUNK = 64
D_FF = 4 * D_MODEL
IN_COLS = 3 * FOX_W + N_HEADS + 2 * SGU_W + 2 * D_MODEL
IN_SHARD = IN_COLS // N_DEV
LANES = 128
QKV_W = 3 * FOX_W
U_OFF, SV_OFF, GA_OFF, GB_OFF, F_OFF = 0, 512, 1024, 2048, 3072
REST_W = F_OFF + LANES
EPS = 1e-6
NEG = -1e30

ADAM_LR = 0.001
ADAM_B1 = 0.9
ADAM_B2 = 0.999
ADAM_EPS = 1e-08
ADAM_WD = 0.01
ADAM_STEP = 10

VMEM_LIMIT = 56 * 1024 * 1024
MESH = pl.DeviceIdType.MESH


def _params(sem=None):
    return pltpu.CompilerParams(dimension_semantics=sem, vmem_limit_bytes=VMEM_LIMIT)


def _dot(a, b):
    return jnp.dot(a, b, preferred_element_type=F32)


def _dot_nt(a, b):
    return lax.dot_general(a, b, (((1,), (1,)), ((), ())), preferred_element_type=F32)


def _split3(x):
    hi = x.astype(BF16)
    rest = x - hi.astype(F32)
    mid = rest.astype(BF16)
    return hi, mid, (rest - mid.astype(F32)).astype(BF16)


def _dot_f32(a, b, exact):
    if exact == "a":
        a16 = a.astype(BF16)
        return sum(_dot(a16, part) for part in _split3(b))
    b16 = b.astype(BF16)
    return sum(_dot(part, b16) for part in _split3(a))


def _sigmoid(x):
    return 1.0 / (1.0 + jnp.exp(-x))


def _log_sigmoid(z):
    return jnp.minimum(z, 0.0) - jnp.log(1.0 + jnp.exp(-jnp.abs(z)))


_GELU_K = math.sqrt(2.0 / math.pi)
_GELU_C = 0.044715


def _gelu(x):
    t = jnp.tanh(_GELU_K * (x + _GELU_C * (x * x * x)))
    return 0.5 * x * (1.0 + t)


def _gelu_grad(x):
    x2 = x * x
    t = jnp.tanh(_GELU_K * (x + _GELU_C * (x2 * x)))
    return 0.5 * (1.0 + t) + 0.5 * x * (1.0 - t * t) * (_GELU_K * (1.0 + 3.0 * _GELU_C * x2))


def _rms_bwd(xh, r, g, dy):
    gy = dy * g
    return r * (gy - xh * jnp.mean(xh * gy, axis=-1, keepdims=True))


def _lane_lt64(shape):
    return lax.broadcasted_iota(jnp.int32, shape, len(shape) - 1) < HEAD_DIM


class _Comm:
    def __init__(self, arrays, out_shapes, sems, start, finish, mid=None):
        self.arrays, self.out_shapes, self.sems = list(arrays), list(out_shapes), list(sems)
        self.start, self.mid, self.finish = start, mid, finish


def _comm_phase(plans, phase, in_refs, out_refs, sem_refs):
    ia = io = ks = 0
    for plan in plans:
        na, no, ns = len(plan.arrays), len(plan.out_shapes), len(plan.sems)
        fn = getattr(plan, phase)
        if fn is not None:
            fn(in_refs[ia:ia + na], out_refs[io:io + no], sem_refs[ks:ks + ns])
        ia, io, ks = ia + na, io + no, ks + ns


def _comm_operands(plans):
    arrays = [a for plan in plans for a in plan.arrays]
    out_shapes = [o for plan in plans for o in plan.out_shapes]
    sems = [s for plan in plans for s in plan.sems]
    return arrays, out_shapes, sems


_ANY = pl.BlockSpec(memory_space=pl.ANY)


def _run_comm(name, plans):
    arrays, out_shapes, sems = _comm_operands(plans)
    n_in, n_out = len(arrays), len(out_shapes)

    def body(*refs):
        parts = refs[:n_in], refs[n_in:n_in + n_out], refs[n_in + n_out:]
        for phase in ("start", "mid", "finish"):
            _comm_phase(plans, phase, *parts)

    return pl.pallas_call(
        body, name=name, out_shape=out_shapes, in_specs=[_ANY] * n_in, out_specs=[_ANY] * n_out, scratch_shapes=sems,
    )(*arrays)


def _gather_plan(shard):
    def setup(ins, outs, sems):
        (x_ref,), (out_ref,), (send_sems, recv_sems, local_sem) = ins, outs, sems
        x, y, c = lax.axis_index("x"), lax.axis_index("y"), lax.axis_index("c")
        me, sibling = (x, y, c), (x, y, 1 - c)
        chips = [(1 - x, y), (x, 1 - y), (1 - x, 1 - y)]

        def rows(px, py, pc):
            return out_ref.at[4 * px + 2 * py + pc]

        def copy(k, block, to, src=None):
            return pltpu.make_async_remote_copy(
                src_ref=rows(*block) if src is None else src,
                dst_ref=rows(*block),
                send_sem=send_sems.at[k],
                recv_sem=recv_sems.at[k],
                device_id=to,
                device_id_type=MESH,
            )

        mine = pltpu.make_async_copy(x_ref, rows(*me), local_sem)
        first = [copy(0, me, sibling, src=x_ref)]
        first += [copy(1 + j, me, (*chip, c), src=x_ref) for j, chip in enumerate(chips)]
        passed = [copy(4 + j, (*chip, c), sibling) for j, chip in enumerate(chips)]
        landed = [copy(1 + j, (*chip, c), me) for j, chip in enumerate(chips)]
        from_sibling = [copy(0, sibling, me)] + [copy(4 + j, (*chip, 1 - c), me) for j, chip in enumerate(chips)]
        return mine, first, passed, landed, from_sibling

    def start(ins, outs, sems):
        mine, first, _, _, _ = setup(ins, outs, sems)
        mine.start()
        for cp in first:
            cp.start()

    def mid(ins, outs, sems):
        _, _, passed, landed, _ = setup(ins, outs, sems)
        for arrived, onward in zip(landed, passed):
            arrived.wait_recv()
            onward.start()

    def finish(ins, outs, sems):
        mine, first, passed, _, from_sibling = setup(ins, outs, sems)
        for cp in from_sibling:
            cp.wait_recv()
        for cp in first + passed:
            cp.wait_send()
        mine.wait()

    return _Comm([shard], [jax.ShapeDtypeStruct((N_DEV,) + shard.shape, shard.dtype)],
                 [pltpu.SemaphoreType.DMA((7,)), pltpu.SemaphoreType.DMA((7,)), pltpu.SemaphoreType.DMA],
                 start, finish, mid)


def _start_all(copies):
    for cp in copies:
        cp.start()


def _wait_all(copies):
    for cp in copies:
        cp.wait_recv()
    for cp in copies:
        cp.wait_send()


def _pair_exchange_plan(grads):
    n = len(grads)

    def copies(ins, outs, sems):
        send_sems, recv_sems = sems
        x, y, c = lax.axis_index("x"), lax.axis_index("y"), lax.axis_index("c")
        return [
            pltpu.make_async_remote_copy(
                src_ref=ins[k].at[:, 1 - c],
                dst_ref=outs[k],
                send_sem=send_sems.at[k],
                recv_sem=recv_sems.at[k],
                device_id=(x, y, 1 - c),
                device_id_type=MESH,
            )
            for k in range(n)
        ]

    return _Comm(grads, [jax.ShapeDtypeStruct((4,) + g.shape[2:], g.dtype) for g in grads],
                 [pltpu.SemaphoreType.DMA((n,)), pltpu.SemaphoreType.DMA((n,))],
                 lambda *refs: _start_all(copies(*refs)), lambda *refs: _wait_all(copies(*refs)))


def _chip_exchange_plan(parts):
    n = len(parts)

    def copies(ins, outs, sems):
        send_sems, recv_sems = sems
        x, y, c = lax.axis_index("x"), lax.axis_index("y"), lax.axis_index("c")
        chips = [(1 - x, y), (x, 1 - y), (1 - x, 1 - y)]
        return [
            pltpu.make_async_remote_copy(
                src_ref=ins[k].at[2 * px + py],
                dst_ref=outs[k].at[j],
                send_sem=send_sems.at[3 * k + j],
                recv_sem=recv_sems.at[3 * k + j],
                device_id=(px, py, c),
                device_id_type=MESH,
            )
            for k in range(n) for j, (px, py) in enumerate(chips)
        ]

    return _Comm(parts, [jax.ShapeDtypeStruct((3,) + p.shape[1:], p.dtype) for p in parts],
                 [pltpu.SemaphoreType.DMA((3 * n,)), pltpu.SemaphoreType.DMA((3 * n,))],
                 lambda *refs: _start_all(copies(*refs)), lambda *refs: _wait_all(copies(*refs)))


def _mm(name, pairs, extras, outs, epi, *, m, tm, n, tn, arbitrary=False, comm=()):
    nj = n // tn
    a_arrays, a_specs, b_arrays, b_specs, b_index = [], [], [], [], []
    for a, b, nt, cols in pairs:
        a_arrays.append(a)
        a_specs.append(pl.BlockSpec((tm, a.shape[1]), lambda i, j: (i, 0)))
        known = [k for k, other in enumerate(b_arrays) if other is b]
        if known:
            b_index.append(known[0])
            continue
        b_index.append(len(b_arrays))
        b_arrays.append(b)
        if cols is not None:
            assert nj == 1
            b_specs.append(pl.BlockSpec(b.shape, lambda i, j: (0, 0)))
        elif nt:
            b_specs.append(pl.BlockSpec((tn, b.shape[1]), lambda i, j: (j, 0)))
        else:
            b_specs.append(pl.BlockSpec((b.shape[0], tn), lambda i, j: (0, j)))
    comm_arrays, comm_outs, comm_sems = _comm_operands(comm)
    arrays = a_arrays + b_arrays + [arr for arr, _ in extras] + comm_arrays
    in_specs = a_specs + b_specs + [spec for _, spec in extras] + [_ANY] * len(comm_arrays)
    n_a, n_b, n_extras, n_ci, n_out, n_co = len(a_arrays), len(b_arrays), len(extras), len(comm_arrays), len(outs), len(comm_outs)
    ni = m // tm

    def body(*refs):
        a_refs = refs[:n_a]
        b_refs = refs[n_a:n_a + n_b]
        ex = refs[n_a + n_b:n_a + n_b + n_extras]
        n_in = n_a + n_b + n_extras + n_ci
        comm_refs = refs[n_in - n_ci:n_in], refs[n_in + n_out:n_in + n_out + n_co], refs[n_in + n_out + n_co:]
        out = refs[n_in:n_in + n_out]
        if comm:
            @pl.when(jnp.logical_and(pl.program_id(0) == 0, pl.program_id(1) == 0))
            def _():
                _comm_phase(comm, "start", *comm_refs)

        accs = []
        for p, (_, _, nt, cols) in enumerate(pairs):
            av = a_refs[p][...]
            if av.dtype != BF16:
                av = av.astype(BF16)
            b_ref = b_refs[b_index[p]]
            if cols is None:
                bv = b_ref[...]
            else:
                bv = b_ref[:, cols[0]:cols[1]] if nt else b_ref[cols[0]:cols[1], :]
            accs.append(_dot_nt(av, bv) if nt else _dot(av, bv))
        epi(accs, ex, out)
        if comm:
            mid_row = ni // 2 if ni >= 3 else ni - 1
            mid_col = 0 if ni >= 3 else nj - 1

            @pl.when(jnp.logical_and(pl.program_id(0) == mid_row, pl.program_id(1) == mid_col))
            def _():
                _comm_phase(comm, "mid", *comm_refs)

            @pl.when(jnp.logical_and(pl.program_id(0) == ni - 1, pl.program_id(1) == nj - 1))
            def _():
                _comm_phase(comm, "finish", *comm_refs)

    sem = ("arbitrary", "arbitrary") if arbitrary or comm else ("parallel", "parallel")
    res = pl.pallas_call(
        body,
        name=name,
        grid=(ni, nj),
        in_specs=in_specs,
        out_specs=[spec for _, spec in outs] + [_ANY] * n_co,
        out_shape=[shape for shape, _ in outs] + comm_outs,
        scratch_shapes=comm_sems,
        compiler_params=_params(sem),
    )(*arrays)
    return (res[:n_out], res[n_out:]) if comm else res


def _tile(tm, tn, off=0):
    return pl.BlockSpec((tm, tn), lambda i, j: (i, j + off))


def _row(tm, w, blk=0):
    return pl.BlockSpec((tm, w), lambda i, j: (i, blk))


def _whole(shape):
    zeros = (0,) * len(shape)
    return pl.BlockSpec(shape, lambda i, j: zeros)


def _sds(shape, dtype):
    return jax.ShapeDtypeStruct(shape, dtype)


def _tile_t(tm, tn):
    return pl.BlockSpec((tn, tm), lambda i, j: (j, i))


def _grad_w(name, a_t, g, *, tk, tn, ts, block_cols=None):
    ka, s_len = a_t.shape
    n = g.shape[1]
    width = tn if block_cols is None else block_cols

    def body(a_ref, g_ref, o_ref):
        first = pl.program_id(2) == 0
        gv = g_ref[...].astype(BF16)
        for b in range(tn // width):
            part = _dot(a_ref[...], gv[:, b * width:(b + 1) * width])
            dst = o_ref if block_cols is None else o_ref.at[b]

            @pl.when(first)
            def _():
                dst[...] = part

            @pl.when(jnp.logical_not(first))
            def _():
                dst[...] += part

    if block_cols is None:
        out_shape = _sds((ka, n), F32)
        out_spec = pl.BlockSpec((tk, tn), lambda i, j, s: (i, j))
    else:
        out_shape = _sds((n // width, ka, width), F32)
        out_spec = pl.BlockSpec((tn // width, tk, width), lambda i, j, s: (j, i, 0))
    return pl.pallas_call(
        body,
        name=name,
        grid=(ka // tk, n // tn, s_len // ts),
        in_specs=[pl.BlockSpec((tk, ts), lambda i, j, s: (i, s)), pl.BlockSpec((ts, tn), lambda i, j, s: (s, j))],
        out_specs=out_spec,
        out_shape=out_shape,
        compiler_params=_params(("parallel", "parallel", "arbitrary")),
    )(a_t, g)


def _rms_fwd(name, x, g, tm, comm=()):
    s_len, d = x.shape
    steps = s_len // tm
    comm_arrays, comm_outs, comm_sems = _comm_operands(comm)
    n_ci, n_co = len(comm_arrays), len(comm_outs)

    def body(x_ref, g_ref, *rest):
        comm_refs = rest[:n_ci], rest[n_ci + 3:n_ci + 3 + n_co], rest[n_ci + 3 + n_co:]
        h_ref, ht_ref, r_ref = rest[n_ci:n_ci + 3]
        for phase, at in (("start", 0), ("mid", steps // 2)):
            if comm:
                @pl.when(pl.program_id(0) == at)
                def _():
                    _comm_phase(comm, phase, *comm_refs)

        xv = x_ref[...]
        r = lax.rsqrt(jnp.mean(xv * xv, axis=-1, keepdims=True) + EPS)
        h = (xv * r * g_ref[...]).astype(BF16)
        h_ref[...] = h
        ht_ref[...] = jnp.transpose(h)
        r_ref[...] = r
        if comm:
            @pl.when(pl.program_id(0) == steps - 1)
            def _():
                _comm_phase(comm, "finish", *comm_refs)

    res = pl.pallas_call(
        body,
        name=name,
        grid=(steps,),
        in_specs=[pl.BlockSpec((tm, d), lambda i: (i, 0)), pl.BlockSpec((1, d), lambda i: (0, 0))] + [_ANY] * n_ci,
        out_specs=[pl.BlockSpec((tm, d), lambda i: (i, 0)), pl.BlockSpec((d, tm), lambda i: (0, i)),
                   pl.BlockSpec((tm, 1), lambda i: (i, 0))] + [_ANY] * n_co,
        out_shape=[_sds((s_len, d), BF16), _sds((d, s_len), BF16), _sds((s_len, 1), F32)] + comm_outs,
        scratch_shapes=comm_sems,
        compiler_params=_params(("arbitrary",) if comm else ("parallel",)),
    )(x, g, *comm_arrays)
    return res[0], res[1], res[2], res[3:]


def _forget_cumsum(prest, b_f_pad, tc):
    s_len = prest.shape[0]

    def body(f_ref, b_ref, c_ref, carry):
        @pl.when(pl.program_id(0) == 0)
        def _():
            carry[...] = jnp.zeros_like(carry)

        logf = _log_sigmoid(f_ref[...] + b_ref[...])
        row = lax.broadcasted_iota(jnp.int32, (tc, tc), 0)
        col = lax.broadcasted_iota(jnp.int32, (tc, tc), 1)
        tri = (row >= col).astype(F32)
        c = _dot_f32(tri, logf, "a") + carry[...]
        c_ref[...] = c
        carry[...] = c[tc - 1:tc, :]

    return pl.pallas_call(
        body,
        name="forget_cumsum",
        grid=(s_len // tc,),
        in_specs=[pl.BlockSpec((tc, LANES), lambda i: (i, 0)), pl.BlockSpec((1, LANES), lambda i: (0, 0))],
        out_specs=pl.BlockSpec((tc, LANES), lambda i: (i, 0)),
        out_shape=_sds((s_len, LANES), F32),
        scratch_shapes=[pltpu.VMEM((1, LANES), F32)],
        compiler_params=_params(("arbitrary",)),
    )(prest, b_f_pad)


def _stack_heads(pair, lt64):
    zero = jnp.zeros_like(pair)
    return jnp.concatenate([jnp.where(lt64, pair, zero), jnp.where(lt64, zero, pair)], axis=0)


def _score_tiles(q_ref, k_ref, ck_ref, st_sc, tk):
    lt64 = _lane_lt64((tk, LANES))
    for p in range(N_HEADS // 2):
        lanes = slice(p * LANES, (p + 1) * LANES)
        q_pair = q_ref[:, lanes] * jnp.asarray(HEAD_DIM ** -0.5, BF16)
        st2 = _dot_nt(_stack_heads(k_ref[:, lanes], lt64), q_pair)
        for half in range(2):
            h = 2 * p + half
            st_sc[h] = st2[half * tk:(half + 1) * tk] - ck_ref[:, h:h + 1]


ROW_CHUNK = 64


def _row_chunks(tk):
    rc = min(ROW_CHUNK, tk)
    return [slice(r, r + rc) for r in range(0, tk, rc)]


def _by_sublane(x):
    return x.reshape(x.shape[0] // 8, 8, x.shape[1])


def _softmax_update(st_sc, p_sc, m_sc, l_sc, tk, tq):
    alphas = []
    for h in range(N_HEADS):
        top8 = jnp.full((8, tq), NEG, F32)
        for rows in _row_chunks(tk):
            top8 = jnp.maximum(top8, jnp.max(_by_sublane(st_sc[h, rows, :]), axis=0))
        m_old = m_sc[h]
        m_new = jnp.maximum(m_old, jnp.max(top8, axis=0, keepdims=True))
        sum8 = jnp.zeros((8, tq), F32)
        for rows in _row_chunks(tk):
            pt = jnp.exp(st_sc[h, rows, :] - m_new)
            p_sc[h, rows, :] = pt.astype(BF16)
            sum8 = sum8 + jnp.sum(_by_sublane(pt), axis=0)
        alpha = jnp.exp(m_old - m_new)
        l_sc[h] = alpha * l_sc[h] + jnp.sum(sum8, axis=0, keepdims=True)
        m_sc[h] = m_new
        alphas.append(alpha)
    return alphas


def _mask_diagonal(st_sc, i, j, tq, tk):
    @pl.when((j + 1) * tk - 1 > i * tq)
    def _():
        key = j * tk + lax.broadcasted_iota(jnp.int32, (tk, tq), 0)
        query = i * tq + lax.broadcasted_iota(jnp.int32, (tk, tq), 1)
        st_sc[...] = jnp.where((query >= key)[None], st_sc[...], NEG)


def _attn_fwd(qkv, v_t, c_col, tq, tk, comm=()):
    s_len = qkv.shape[0]
    ratio = tq // tk
    steps = [(i, j) for i in range(s_len // tq) for j in range((i + 1) * ratio)]
    i_tab = jnp.asarray([i for i, _ in steps], jnp.int32)
    j_tab = jnp.asarray([j for _, j in steps], jnp.int32)

    comm_arrays, comm_outs, comm_sems = _comm_operands(comm)
    n_ci, n_co = len(comm_arrays), len(comm_outs)

    def body(i_ref, j_ref, q_ref, k_ref, vt_ref, ck_ref, *rest):
        comm_refs = rest[:n_ci], rest[n_ci + 3:n_ci + 3 + n_co], rest[n_ci + 3 + n_co + 5:]
        o_ref, ot_ref, lse_ref = rest[n_ci:n_ci + 3]
        acc_t, m_sc, l_sc, st_sc, p_sc = rest[n_ci + 3 + n_co:n_ci + 3 + n_co + 5]
        n = pl.program_id(0)
        i, j = i_ref[n], j_ref[n]
        for phase, at in (("start", 0), ("mid", (2 * len(steps)) // 3)):
            if comm:
                @pl.when(n == at)
                def _():
                    _comm_phase(comm, phase, *comm_refs)

        @pl.when(j == 0)
        def _():
            acc_t[...] = jnp.zeros_like(acc_t)
            m_sc[...] = jnp.full_like(m_sc, NEG)
            l_sc[...] = jnp.zeros_like(l_sc)

        _score_tiles(q_ref, k_ref, ck_ref, st_sc, tk)
        _mask_diagonal(st_sc, i, j, tq, tk)
        alpha = _softmax_update(st_sc, p_sc, m_sc, l_sc, tk, tq)
        top = lax.broadcasted_iota(jnp.int32, (LANES, tq), 0) < HEAD_DIM
        for p in range(N_HEADS // 2):
            lanes = slice(p * LANES, (p + 1) * LANES)
            vt_pair = vt_ref[lanes, :]
            pv = jnp.where(top, _dot(vt_pair, p_sc[2 * p]), _dot(vt_pair, p_sc[2 * p + 1]))
            acc_t[lanes, :] = acc_t[lanes, :] * jnp.where(top, alpha[2 * p], alpha[2 * p + 1]) + pv

        @pl.when(j == (i + 1) * ratio - 1)
        def _():
            for p in range(N_HEADS // 2):
                lanes = slice(p * LANES, (p + 1) * LANES)
                l_pair = jnp.where(top, l_sc[2 * p], l_sc[2 * p + 1])
                o_t = acc_t[lanes, :] / l_pair
                o_ref[:, lanes] = jnp.transpose(o_t)
                ot_ref[lanes, :] = o_t.astype(BF16)
            lse_ref[...] = m_sc[...] + jnp.log(l_sc[...])

        if comm:
            @pl.when(n == len(steps) - 1)
            def _():
                _comm_phase(comm, "finish", *comm_refs)

    stat = pltpu.VMEM((N_HEADS, 1, tq), F32)
    res = pl.pallas_call(
        body,
        name="attn_fwd",
        grid_spec=pltpu.PrefetchScalarGridSpec(
            num_scalar_prefetch=2,
            grid=(len(steps),),
            in_specs=[
                pl.BlockSpec((tq, FOX_W), lambda n, it, jt: (it[n], 0)),
                pl.BlockSpec((tk, FOX_W), lambda n, it, jt: (jt[n], 1)),
                pl.BlockSpec((FOX_W, tk), lambda n, it, jt: (0, jt[n])),
                pl.BlockSpec((tk, LANES), lambda n, it, jt: (jt[n], 0)),
            ] + [_ANY] * n_ci,
            out_specs=[
                pl.BlockSpec((tq, FOX_W), lambda n, it, jt: (it[n], 0)),
                pl.BlockSpec((FOX_W, tq), lambda n, it, jt: (0, it[n])),
                pl.BlockSpec((N_HEADS, 1, tq), lambda n, it, jt: (0, 0, it[n])),
            ] + [_ANY] * n_co,
            scratch_shapes=[pltpu.VMEM((FOX_W, tq), F32), stat, stat, pltpu.VMEM((N_HEADS, tk, tq), F32),
                            pltpu.VMEM((N_HEADS, tk, tq), BF16)] + comm_sems,
        ),
        out_shape=[_sds((s_len, FOX_W), F32), _sds((FOX_W, s_len), BF16), _sds((N_HEADS, 1, s_len), F32)] + comm_outs,
        compiler_params=_params(("arbitrary",)),
    )(i_tab, j_tab, qkv, qkv, v_t, c_col, *comm_arrays)
    return res[0], res[1], res[2], res[3:]


def _sgu_mix(vn, w_stack, lt64):
    outs = []
    for p in range(SGU_G // 2):
        r = _dot(w_stack[p], vn[:, p * LANES:(p + 1) * LANES])
        outs.append(jnp.where(lt64, r[:SGU_LEN], r[SGU_LEN:]))
    return jnp.concatenate(outs, axis=1)


def _sgu_norm(sv, ln_g, ln_b):
    svg = _gelu(sv)
    xc = svg - jnp.mean(svg, axis=-1, keepdims=True)
    rstd = lax.rsqrt(jnp.mean(xc * xc, axis=-1, keepdims=True) + EPS)
    xhat = xc * rstd
    return xhat, rstd, xhat * ln_g + ln_b


def _sgu_fwd(prest, ln_g, ln_b, w_stack, b_pair, tm):
    s_len = prest.shape[0]

    def body(u_ref, sv_ref, g_ref, b_ref, w_ref, bp_ref, sg_ref, sgt_ref):
        lt64 = _lane_lt64((SGU_LEN, LANES))
        _, _, vn = _sgu_norm(sv_ref[...].astype(F32), g_ref[...], b_ref[...])
        vn = vn.astype(BF16)
        w_stack_v = [w_ref[p] for p in range(SGU_G // 2)]
        for w in range(tm // SGU_LEN):
            win = slice(w * SGU_LEN, (w + 1) * SGU_LEN)
            mixed = _sgu_mix(vn[win], w_stack_v, lt64) + bp_ref[...]
            sg = (_gelu(u_ref[win, :].astype(F32)) * mixed).astype(BF16)
            sg_ref[win, :] = sg
            sgt_ref[:, win] = jnp.transpose(sg)

    return pl.pallas_call(
        body,
        name="sgu_fwd",
        grid=(s_len // tm,),
        in_specs=[
            pl.BlockSpec((tm, SGU_W), lambda i: (i, U_OFF // SGU_W)),
            pl.BlockSpec((tm, SGU_W), lambda i: (i, SV_OFF // SGU_W)),
            pl.BlockSpec((1, SGU_W), lambda i: (0, 0)),
            pl.BlockSpec((1, SGU_W), lambda i: (0, 0)),
            pl.BlockSpec((SGU_G // 2, 2 * SGU_LEN, SGU_LEN), lambda i: (0, 0, 0)),
            pl.BlockSpec((SGU_LEN, SGU_W), lambda i: (0, 0)),
        ],
        out_specs=[pl.BlockSpec((tm, SGU_W), lambda i: (i, 0)), pl.BlockSpec((SGU_W, tm), lambda i: (0, i))],
        out_shape=[_sds((s_len, SGU_W), BF16), _sds((SGU_W, s_len), BF16)],
        compiler_params=_params(("parallel",)),
    )(prest, prest, ln_g, ln_b, w_stack, b_pair)


def _sgu_bwd(prest, dyb, w_b_t, ln_g, ln_b, w_stack, wt_stack, b_pair, tm):
    s_len = prest.shape[0]
    n_pair = SGU_G // 2

    def body(u_ref, sv_ref, dyb_ref, wb_ref, g_ref, b_ref, w_ref, wt_ref, bp_ref,
             du_ref, dsv_ref, dw_ref, db_ref, dg_ref, dbeta_ref, dusvt_ref, dvn_sc):
        @pl.when(pl.program_id(0) == 0)
        def _():
            dw_ref[...] = jnp.zeros_like(dw_ref)
            db_ref[...] = jnp.zeros_like(db_ref)
            dg_ref[...] = jnp.zeros_like(dg_ref)
            dbeta_ref[...] = jnp.zeros_like(dbeta_ref)

        lt64 = _lane_lt64((SGU_LEN, LANES))
        sv = sv_ref[...].astype(F32)
        xhat, rstd, vn32 = _sgu_norm(sv, g_ref[...], b_ref[...])
        vn = vn32.astype(BF16)
        w_stack_v = [w_ref[p] for p in range(n_pair)]
        dsg = _dot(dyb_ref[...], wb_ref[...])
        db = jnp.zeros((SGU_LEN, SGU_W), F32)
        for w in range(tm // SGU_LEN):
            win = slice(w * SGU_LEN, (w + 1) * SGU_LEN)
            u = u_ref[win, :].astype(F32)
            dsg_w = dsg[win]
            mixed = _sgu_mix(vn[win], w_stack_v, lt64) + bp_ref[...]
            du = (dsg_w * mixed * _gelu_grad(u)).astype(BF16)
            du_ref[win, :] = du
            dusvt_ref[:SGU_W, win] = jnp.transpose(du)
            dmixed = dsg_w * _gelu(u)
            db = db + dmixed
            dm16 = dmixed.astype(BF16)
            for p in range(n_pair):
                lanes = slice(p * LANES, (p + 1) * LANES)
                dmp = dm16[:, lanes]
                r = _dot(wt_ref[p], dmp)
                dvn_sc[win, lanes] = jnp.where(lt64, r[:SGU_LEN], r[SGU_LEN:])
                zero = jnp.zeros_like(dmp)
                dm_ab = jnp.concatenate([jnp.where(lt64, dmp, zero), jnp.where(lt64, zero, dmp)], axis=0)
                dw_ref[p] += _dot_nt(dm_ab, vn[win, lanes])
        db_ref[...] += db
        dvn = dvn_sc[...]
        dg_ref[...] += jnp.sum(dvn * xhat, axis=0, keepdims=True)
        dbeta_ref[...] += jnp.sum(dvn, axis=0, keepdims=True)
        dxh = dvn * g_ref[...]
        dsvg = rstd * (dxh - jnp.mean(dxh, axis=-1, keepdims=True) - xhat * jnp.mean(dxh * xhat, axis=-1, keepdims=True))
        dsv = (dsvg * _gelu_grad(sv)).astype(BF16)
        dsv_ref[...] = dsv
        dusvt_ref[SGU_W:, :] = jnp.transpose(dsv)

    const2 = lambda i: (0, 0)
    const3 = lambda i: (0, 0, 0)
    return pl.pallas_call(
        body,
        name="sgu_bwd",
        grid=(s_len // tm,),
        in_specs=[
            pl.BlockSpec((tm, SGU_W), lambda i: (i, U_OFF // SGU_W)),
            pl.BlockSpec((tm, SGU_W), lambda i: (i, SV_OFF // SGU_W)),
            pl.BlockSpec((tm, dyb.shape[1]), lambda i: (i, 0)),
            pl.BlockSpec(w_b_t.shape, const2),
            pl.BlockSpec((1, SGU_W), const2),
            pl.BlockSpec((1, SGU_W), const2),
            pl.BlockSpec((n_pair, 2 * SGU_LEN, SGU_LEN), const3),
            pl.BlockSpec((n_pair, 2 * SGU_LEN, SGU_LEN), const3),
            pl.BlockSpec((SGU_LEN, SGU_W), const2),
        ],
        out_specs=[
            pl.BlockSpec((tm, SGU_W), lambda i: (i, 0)),
            pl.BlockSpec((tm, SGU_W), lambda i: (i, 0)),
            pl.BlockSpec((n_pair, 2 * SGU_LEN, SGU_LEN), const3),
            pl.BlockSpec((SGU_LEN, SGU_W), const2),
            pl.BlockSpec((1, SGU_W), const2),
            pl.BlockSpec((1, SGU_W), const2),
            pl.BlockSpec((2 * SGU_W, tm), lambda i: (0, i)),
        ],
        out_shape=[
            _sds((s_len, SGU_W), BF16), _sds((s_len, SGU_W), BF16), _sds((n_pair, 2 * SGU_LEN, SGU_LEN), F32),
            _sds((SGU_LEN, SGU_W), F32), _sds((1, SGU_W), F32), _sds((1, SGU_W), F32),
            _sds((2 * SGU_W, s_len), BF16),
        ],
        scratch_shapes=[pltpu.VMEM((tm, SGU_W), F32)],
        compiler_params=_params(("arbitrary",)),
    )(prest, prest, dyb, w_b_t, ln_g, ln_b, w_stack, wt_stack, b_pair)


def _attn_bwd(qkv, k_t, do, c_col, lse_row, delta_row, tq, tk, comm=()):
    s_len = qkv.shape[0]
    nq, nk = s_len // tq, s_len // tk
    ratio = tq // tk
    scale = HEAD_DIM ** -0.5
    steps = [(j, i) for j in range(nk) for i in range(j // ratio, nq)]
    j_tab = jnp.asarray([j for j, _ in steps], jnp.int32)
    i_tab = jnp.asarray([i for _, i in steps], jnp.int32)

    comm_arrays, comm_outs, comm_sems = _comm_operands(comm)
    n_ci, n_co = len(comm_arrays), len(comm_outs)

    def body(j_ref, i_ref, q_ref, k_ref, v_ref, kt_ref, do_ref, ck_ref, lse_ref, dl_ref, *rest):
        comm_refs = rest[:n_ci], rest[n_ci + 6:n_ci + 6 + n_co], rest[n_ci + 6 + n_co + 8:]
        dq_ref, dk_ref, dv_ref, dcr_ref, dcc_ref, dkvt_ref = rest[n_ci:n_ci + 6]
        dq_t, dk_acc, dv_acc, dcc_acc, st_sc, dpt_sc, p_sc, ds_sc = rest[n_ci + 6 + n_co:n_ci + 6 + n_co + 8]
        n = pl.program_id(0)
        j, i = j_ref[n], i_ref[n]

        @pl.when(n == 0)
        def _():
            _comm_phase(comm, "start", *comm_refs)
            dq_t[...] = jnp.zeros_like(dq_t)
            dcr_ref[...] = jnp.zeros_like(dcr_ref)

        @pl.when(i == j // ratio)
        def _():
            dk_acc[...] = jnp.zeros_like(dk_acc)
            dv_acc[...] = jnp.zeros_like(dv_acc)
            dcc_acc[...] = jnp.zeros_like(dcc_acc)

        lt64 = _lane_lt64((tk, LANES))
        _score_tiles(q_ref, k_ref, ck_ref, st_sc, tk)
        for p in range(N_HEADS // 2):
            lanes = slice(p * LANES, (p + 1) * LANES)
            dpt2 = _dot_nt(_stack_heads(v_ref[:, lanes], lt64), do_ref[:, lanes].astype(BF16))
            dpt_sc[2 * p] = dpt2[:tk]
            dpt_sc[2 * p + 1] = dpt2[tk:]
        _mask_diagonal(st_sc, i, j, tq, tk)

        pt = jnp.exp(st_sc[...] - lse_ref[...])
        dst = pt * (dpt_sc[...] - dl_ref[...])
        p_sc[...] = pt.astype(BF16)
        ds_sc[...] = dst.astype(BF16)
        dcr_ref[i] += jnp.sum(dst, axis=1, keepdims=True)
        col_sums = jnp.sum(dst, axis=2, keepdims=True)
        lane = lax.broadcasted_iota(jnp.int32, (tk, LANES), 1)
        dcc = jnp.zeros((tk, LANES), F32)
        for h in range(N_HEADS):
            dcc = jnp.where(lane == h, -col_sums[h], dcc)
        dcc_acc[...] += dcc

        for p in range(N_HEADS // 2):
            lanes = slice(p * LANES, (p + 1) * LANES)
            q_pair = q_ref[:, lanes] * jnp.asarray(scale, BF16)
            dv2 = _dot(p_sc[2 * p:2 * p + 2].reshape(2 * tk, tq), do_ref[:, lanes].astype(BF16))
            dv_acc[:, lanes] += jnp.where(lt64, dv2[:tk], dv2[tk:])
            dk2 = _dot(ds_sc[2 * p:2 * p + 2].reshape(2 * tk, tq), q_pair)
            dk_acc[:, lanes] += jnp.where(lt64, dk2[:tk], dk2[tk:])
            dq2 = _dot(kt_ref[lanes, :], jnp.concatenate([ds_sc[2 * p], ds_sc[2 * p + 1]], axis=1))
            top = lax.broadcasted_iota(jnp.int32, (LANES, tq), 0) < HEAD_DIM
            dq_t[i, lanes, :] += jnp.where(top, dq2[:, :tq], dq2[:, tq:])

        @pl.when(j == (i + 1) * ratio - 1)
        def _():
            rows = pl.ds(pl.multiple_of(i * tq, tq), tq)
            for p in range(N_HEADS // 2):
                lanes = slice(p * LANES, (p + 1) * LANES)
                dq_ref[rows, lanes] = (jnp.transpose(dq_t[i, lanes, :]) * scale).astype(BF16)

        @pl.when(i == nq - 1)
        def _():
            dk16, dv16 = dk_acc[...].astype(BF16), dv_acc[...].astype(BF16)
            dk_ref[...] = dk16
            dv_ref[...] = dv16
            dkvt_ref[:FOX_W, :] = jnp.transpose(dk16)
            dkvt_ref[FOX_W:, :] = jnp.transpose(dv16)
            dcc_ref[...] = dcc_acc[...]

        if comm:
            @pl.when(n == len(steps) // 2)
            def _():
                _comm_phase(comm, "mid", *comm_refs)

            @pl.when(n == len(steps) - 1)
            def _():
                _comm_phase(comm, "finish", *comm_refs)

    q_map = lambda n, jt, it: (it[n], 0)
    q_stat = lambda n, jt, it: (0, 0, it[n])
    k_map = lambda n, jt, it: (jt[n], 0)
    tile = (N_HEADS, tk, tq)
    res = pl.pallas_call(
        body,
        name="attn_bwd",
        grid_spec=pltpu.PrefetchScalarGridSpec(
            num_scalar_prefetch=2,
            grid=(len(steps),),
            in_specs=[
                pl.BlockSpec((tq, FOX_W), q_map),
                pl.BlockSpec((tk, FOX_W), lambda n, jt, it: (jt[n], 1)),
                pl.BlockSpec((tk, FOX_W), lambda n, jt, it: (jt[n], 2)),
                pl.BlockSpec((FOX_W, tk), lambda n, jt, it: (0, jt[n])),
                pl.BlockSpec((tq, FOX_W), q_map),
                pl.BlockSpec((tk, LANES), k_map),
                pl.BlockSpec((N_HEADS, 1, tq), q_stat),
                pl.BlockSpec((N_HEADS, 1, tq), q_stat),
            ] + [_ANY] * n_ci,
            out_specs=[
                pl.BlockSpec((s_len, FOX_W), lambda n, jt, it: (0, 0)),
                pl.BlockSpec((tk, FOX_W), k_map),
                pl.BlockSpec((tk, FOX_W), k_map),
                pl.BlockSpec((nq, N_HEADS, 1, tq), lambda n, jt, it: (0, 0, 0, 0)),
                pl.BlockSpec((tk, LANES), k_map),
                pl.BlockSpec((2 * FOX_W, tk), lambda n, jt, it: (0, jt[n])),
            ] + [_ANY] * n_co,
            scratch_shapes=[pltpu.VMEM((nq, FOX_W, tq), F32), pltpu.VMEM((tk, FOX_W), F32), pltpu.VMEM((tk, FOX_W), F32),
                            pltpu.VMEM((tk, LANES), F32), pltpu.VMEM(tile, F32), pltpu.VMEM(tile, F32),
                            pltpu.VMEM(tile, BF16), pltpu.VMEM(tile, BF16)] + comm_sems,
        ),
        out_shape=[_sds((s_len, FOX_W), BF16), _sds((s_len, FOX_W), BF16), _sds((s_len, FOX_W), BF16),
                   _sds((nq, N_HEADS, 1, tq), F32), _sds((s_len, LANES), F32),
                   _sds((2 * FOX_W, s_len), BF16)] + comm_outs,
        compiler_params=_params(("arbitrary",)),
    )(j_tab, i_tab, qkv, qkv, qkv, k_t, do, c_col, lse_row, delta_row, *comm_arrays)
    return res[:6], res[6:]


def _forget_bwd(dc_rows, dc_cols, prest, b_f_pad, tc):
    s_len = dc_rows.shape[0]
    nb = s_len // tc

    def body(dcr_ref, dc_ref, f_ref, b_ref, df_ref, db_ref, dft_ref, carry):
        @pl.when(pl.program_id(0) == 0)
        def _():
            carry[...] = jnp.zeros_like(carry)
            db_ref[...] = jnp.zeros_like(db_ref)

        row = lax.broadcasted_iota(jnp.int32, (tc, tc), 0)
        col = lax.broadcasted_iota(jnp.int32, (tc, tc), 1)
        tri = (row <= col).astype(F32)
        dlogf = _dot_f32(tri, dcr_ref[...] + dc_ref[...], "a") + carry[...]
        carry[...] = dlogf[0:1, :]
        z = f_ref[...] + b_ref[...]
        lane = lax.broadcasted_iota(jnp.int32, (tc, LANES), 1)
        dz = jnp.where(lane < N_HEADS, dlogf * _sigmoid(-z), 0.0)
        df_ref[...] = dz.astype(BF16)
        dft_ref[...] = jnp.transpose(dz).astype(BF16)
        db_ref[...] += jnp.sum(dz, axis=0, keepdims=True)

    rev = lambda i: (nb - 1 - i, 0)
    return pl.pallas_call(
        body,
        name="forget_bwd",
        grid=(nb,),
        in_specs=[
            pl.BlockSpec((tc, LANES), rev),
            pl.BlockSpec((tc, LANES), rev),
            pl.BlockSpec((tc, LANES), rev),
            pl.BlockSpec((1, LANES), lambda i: (0, 0)),
        ],
        out_specs=[pl.BlockSpec((tc, LANES), rev), pl.BlockSpec((1, LANES), lambda i: (0, 0)),
                   pl.BlockSpec((LANES, tc), lambda i: (0, nb - 1 - i))],
        out_shape=[_sds((s_len, LANES), BF16), _sds((1, LANES), F32), _sds((LANES, s_len), BF16)],
        scratch_shapes=[pltpu.VMEM((1, LANES), F32)],
        compiler_params=_params(("arbitrary",)),
    )(dc_rows, dc_cols, prest, b_f_pad)


def _pair_sum(name, g4, recv, idx, tr):
    _, _, r, c = g4.shape

    def body(idx_ref, g_ref, r_ref, p16_ref, own_ref):
        k = pl.program_id(1)
        s = g_ref[...] + r_ref[...]
        p16_ref[...] = s.astype(BF16)

        @pl.when(k == idx_ref[1])
        def _():
            own_ref[...] = s

    return pl.pallas_call(
        body,
        name=name,
        grid_spec=pltpu.PrefetchScalarGridSpec(
            num_scalar_prefetch=1,
            grid=(r // tr, 4),
            in_specs=[
                pl.BlockSpec((None, None, tr, c), lambda i, k, idx: (k, idx[0], i, 0)),
                pl.BlockSpec((None, tr, c), lambda i, k, idx: (k, i, 0)),
            ],
            out_specs=[
                pl.BlockSpec((None, tr, c), lambda i, k, idx: (k, i, 0)),
                pl.BlockSpec((tr, c), lambda i, k, idx: (i, 0)),
            ],
        ),
        out_shape=[_sds((4, r, c), BF16), _sds((r, c), F32)],
        compiler_params=_params(("parallel", "arbitrary")),
    )(idx, g4, recv)


def _adamw_math(w, g, m, v):
    m2 = ADAM_B1 * m + (1.0 - ADAM_B1) * g
    v2 = ADAM_B2 * v + (1.0 - ADAM_B2) * (g * g)
    m_hat = m2 / (1.0 - ADAM_B1 ** ADAM_STEP)
    v_hat = v2 / (1.0 - ADAM_B2 ** ADAM_STEP)
    delta = -ADAM_LR * (m_hat / (jnp.sqrt(v_hat) + ADAM_EPS) + ADAM_WD * w)
    return delta, m2, v2


def _adamw_shard(name, own, recv, w, m, v, tr):
    r, c = own.shape

    def body(own_ref, recv_ref, w_ref, m_ref, v_ref, g_ref, d_ref, m2_ref, v2_ref):
        g = own_ref[...]
        for k in range(3):
            g = g + recv_ref[k].astype(F32)
        delta, m2, v2 = _adamw_math(w_ref[...], g, m_ref[...], v_ref[...])
        g_ref[...] = g
        d_ref[...] = delta
        m2_ref[...] = m2
        v2_ref[...] = v2

    spec = pl.BlockSpec((tr, c), lambda i: (i, 0))
    return pl.pallas_call(
        body,
        name=name,
        grid=(r // tr,),
        in_specs=[spec, pl.BlockSpec((3, tr, c), lambda i: (0, i, 0)), spec, spec, spec],
        out_specs=[spec] * 4,
        out_shape=[_sds((r, c), F32)] * 4,
        compiler_params=_params(("parallel",)),
    )(own, recv, w, m, v)


def _adamw_small(name, gathered, first_row, w, m, v):
    r = w.shape[0]
    assert first_row % r == 0

    def body(ga_ref, w_ref, m_ref, v_ref, g_ref, d_ref, m2_ref, v2_ref):
        g = ga_ref[0]
        for k in range(1, N_DEV):
            g = g + ga_ref[k]
        delta, m2, v2 = _adamw_math(w_ref[...], g, m_ref[...], v_ref[...])
        g_ref[...] = g
        d_ref[...] = delta
        m2_ref[...] = m2
        v2_ref[...] = v2

    spec = pl.BlockSpec((r, LANES), lambda i: (0, 0))
    return pl.pallas_call(
        body,
        name=name,
        grid=(1,),
        in_specs=[pl.BlockSpec((N_DEV, r, LANES), lambda i: (0, first_row // r, 0)), spec, spec, spec],
        out_specs=[spec] * 4,
        out_shape=[_sds((r, LANES), F32)] * 4,
        compiler_params=_params(("arbitrary",)),
    )(gathered, w, m, v)


_TINY_EARLY = (("b_sgu", (1, SGU_G, SGU_LEN)), ("norm2_g", (1, D_MODEL)), ("normf_g", (D_MODEL,)),
               ("ln_v_g", (1, SGU_W)), ("ln_v_b", (1, SGU_W)))
_TINY_LATE = (("b_f", (1, N_HEADS)), ("norm1_g", (1, D_MODEL)), ("loss", ()))


def _pack_rows(values):
    rows = []
    for val in values:
        flat = val.reshape(-1).astype(F32)
        pad = (-flat.shape[0]) % LANES
        rows.append(jnp.pad(flat, (0, pad)).reshape(-1, LANES))
    packed = jnp.concatenate(rows, axis=0)
    return jnp.pad(packed, ((0, (-packed.shape[0]) % 8), (0, 0)))


def _unpack_rows(packed, group):
    out, row = {}, 0
    for name, shape in group:
        size = math.prod(shape)
        n_rows = -(-size // LANES)
        out[name] = packed[row:row + n_rows].reshape(-1)[:size].reshape(shape)
        row += n_rows
    return out


def kernel(x, norm1_g, w_in, b_f, ln_v_g, ln_v_b, w_sgu, b_sgu, w_a, w_b, w_o, norm2_g, w_up, w_down, normf_g, loss_target, m_norm1_g, m_w_in, m_b_f, m_ln_v_g, m_ln_v_b, m_w_sgu, m_b_sgu, m_w_a, m_w_b, m_w_o, m_norm2_g, m_w_up, m_w_down, m_normf_g, v_norm1_g, v_w_in, v_b_f, v_ln_v_g, v_ln_v_b, v_w_sgu, v_b_sgu, v_w_a, v_w_b, v_w_o, v_norm2_g, v_w_up, v_w_down, v_normf_g):
    xs = x[0]
    target = loss_target[0]
    s_len, d = xs.shape
    tm = min(512, s_len)
    tl = min(1024, s_len)
    tr = min(512, s_len)
    ta = min(512, s_len)
    tc = min(512, s_len)

    w_in_t = jnp.transpose(w_in[0])
    lin = (IN_SHARD * d // LANES, LANES)
    big = (w_in_t.reshape(lin), w_a[0], w_b[0], w_o[0], w_up[0], w_down[0])
    h, h_t, r1, (w_in_g,) = _rms_fwd("rms1", xs, norm1_g, tm, comm=[_gather_plan(w_in_t.astype(BF16))])
    w_in_f = w_in_g.reshape(IN_COLS, d)
    later_shards = [jnp.transpose(w_a[0]), jnp.transpose(w_b[0]), w_o[0], jnp.transpose(w_up[0]), w_down[0]]
    later_plans = [_gather_plan(w.astype(BF16)) for w in later_shards]

    def unflatten(gathered):
        return [g.reshape(N_DEV * g.shape[1], g.shape[2]) for g in gathered]

    w_qkv = w_in_f[:QKV_W]
    f_lo = QKV_W
    u_lo = f_lo + N_HEADS
    w_rest = jnp.concatenate([w_in_f[u_lo:], jnp.pad(w_in_f[f_lo:u_lo], ((0, LANES - N_HEADS), (0, 0)))], axis=0)

    chunk_id = jnp.arange(SGU_LEN) // CHUNK
    sgu_mask = chunk_id[None, :] <= chunk_id[:, None]
    w_masked = jnp.where(sgu_mask[None], w_sgu[0], 0.0)
    w_stack = w_masked.reshape(SGU_G // 2, 2 * SGU_LEN, SGU_LEN).astype(BF16)
    wt_stack = jnp.transpose(w_masked, (0, 2, 1)).reshape(SGU_G // 2, 2 * SGU_LEN, SGU_LEN).astype(BF16)
    b_pair = jnp.transpose(jnp.repeat(b_sgu[0], SGU_W // SGU_G, axis=0))
    b_f_pad = jnp.pad(b_f, ((0, 0), (0, LANES - N_HEADS)))
    head_sel = (jnp.arange(FOX_W)[:, None] // HEAD_DIM == jnp.arange(LANES)[None, :]).astype(F32)

    def qkv_epi(accs, ex, out):
        tile = accs[0].astype(BF16)
        out[0][...] = tile
        out[1][...] = jnp.transpose(tile[:, FOX_W:2 * FOX_W])
        out[2][...] = jnp.transpose(tile[:, 2 * FOX_W:])

    t_spec = pl.BlockSpec((FOX_W, tm), lambda i, j: (0, i))
    qkv, k_t, v_t = _mm("proj_qkv", [(h, w_qkv, True, None)], [],
                        [(_sds((s_len, QKV_W), BF16), _tile(tm, QKV_W)), (_sds((FOX_W, s_len), BF16), t_spec),
                         (_sds((FOX_W, s_len), BF16), t_spec)],
                        qkv_epi, m=s_len, tm=tm, n=QKV_W, tn=QKV_W)
    def rest_epi(accs, ex, out):
        out[0][...] = accs[0].astype(BF16)
        out[1][...] = accs[0][:, F_OFF:F_OFF + LANES]

    prest, f_logit = _mm("proj_rest", [(h, w_rest, True, None)], [],
                         [(_sds((s_len, REST_W), BF16), _tile(tm, REST_W)), (_sds((s_len, LANES), F32), _row(tm, LANES))],
                         rest_epi, m=s_len, tm=tm, n=REST_W, tn=REST_W)

    c_col = _forget_cumsum(f_logit, b_f_pad, tc)
    o, o_t, lse_row, later_g = _attn_fwd(qkv, v_t, c_col, ta, ta, comm=later_plans)
    w_a_t, w_b_t, w_o_f, w_up_t, w_down_f = unflatten(later_g)
    sg, sg_t = _sgu_fwd(prest, ln_v_g, ln_v_b, w_stack, b_pair, tm)

    def merge_epi(accs, ex, out):
        ya, yb = accs
        sa, sb = _sigmoid(ex[0][...].astype(F32)), _sigmoid(ex[1][...].astype(F32))
        merged = (sa * ya + sb * yb).astype(BF16)
        out[0][...] = merged
        out[1][...] = ya.astype(BF16)
        out[2][...] = yb.astype(BF16)
        out[3][...] = jnp.transpose(merged)

    merged, ya, yb, merged_t = _mm(
        "merge", [(o, w_a_t, True, None), (sg, w_b_t, True, None)],
        [(prest, _tile(tm, d, GA_OFF // d)), (prest, _tile(tm, d, GB_OFF // d))],
        [(_sds((s_len, d), BF16), _tile(tm, d))] * 3 + [(_sds((d, s_len), BF16), _tile_t(tm, d))],
        merge_epi, m=s_len, tm=tm, n=d, tn=d)

    def resid_epi(accs, ex, out):
        x1v = ex[0][...] + accs[0]
        out[0][...] = x1v
        r = lax.rsqrt(jnp.mean(x1v * x1v, axis=-1, keepdims=True) + EPS)
        h2v = (x1v * r * ex[1][...]).astype(BF16)
        out[1][...] = h2v
        out[2][...] = jnp.transpose(h2v)
        out[3][...] = r

    x1, h2, h2_t, r2 = _mm(
        "out_proj", [(merged, w_o_f, False, None)], [(xs, _tile(tm, d)), (norm2_g, _whole((1, d)))],
        [(_sds((s_len, d), F32), _tile(tm, d)), (_sds((s_len, d), BF16), _tile(tm, d)),
         (_sds((d, s_len), BF16), _tile_t(tm, d)), (_sds((s_len, 1), F32), _row(tm, 1))],
        resid_epi, m=s_len, tm=tm, n=d, tn=d)

    def up_epi(accs, ex, out):
        act = jnp.square(jnp.maximum(accs[0], 0.0)).astype(BF16)
        out[0][...] = act
        out[1][...] = jnp.transpose(act)

    act, act_t = _mm(
        "mlp_up", [(h2, w_up_t, True, None)], [],
        [(_sds((s_len, D_FF), BF16), _tile(tl, 1024)), (_sds((D_FF, s_len), BF16), _tile_t(tl, 1024))],
        up_epi, m=s_len, tm=tl, n=D_FF, tn=1024)

    def first_step():
        return jnp.logical_and(pl.program_id(0) == 0, pl.program_id(1) == 0)

    def accumulate(ref, val):
        @pl.when(first_step())
        def _():
            ref[...] = val

        @pl.when(jnp.logical_not(first_step()))
        def _():
            ref[...] += val

    def final_epi(accs, ex, out):
        x1_ref, t_ref, g_ref = ex
        x2 = x1_ref[...] + accs[0]
        rf = lax.rsqrt(jnp.mean(x2 * x2, axis=-1, keepdims=True) + EPS)
        xh = x2 * rf
        gf = g_ref[...]
        err = xh * gf - t_ref[...]
        dy = err * (1.0 / d)
        dx2 = _rms_bwd(xh, rf, gf, dy)
        out[0][...] = dx2
        accumulate(out[1], jnp.sum(dy * xh, axis=0, keepdims=True))
        part = 0.5 * jnp.sum(jnp.sum(err * err, axis=-1, keepdims=True) * (1.0 / d), axis=0, keepdims=True)
        accumulate(out[2], jnp.broadcast_to(part, (1, LANES)))
        out[3][...] = dx2.astype(BF16)

    gf2 = normf_g.reshape(1, d)
    dx2, g_normf, loss_part, dx2_16 = _mm(
        "mlp_down_loss", [(act, w_down_f, False, None)],
        [(x1, _row(tr, d)), (target, _row(tr, d)), (gf2, _whole((1, d)))],
        [(_sds((s_len, d), F32), _row(tr, d)), (_sds((1, d), F32), _whole((1, d))), (_sds((1, LANES), F32), _whole((1, LANES))),
         (_sds((s_len, d), BF16), _row(tr, d))],
        final_epi, m=s_len, tm=tr, n=d, tn=d, arbitrary=True)

    def dact_epi(accs, ex, out):
        out[0][...] = (accs[0] * (2.0 * jnp.sqrt(ex[0][...].astype(F32)))).astype(BF16)

    (da,) = _mm("mlp_down_bwd", [(dx2_16, w_down_f, True, None)], [(act, _tile(tl, 1024))],
                [(_sds((s_len, D_FF), BF16), _tile(tl, 1024))], dact_epi, m=s_len, tm=tl, n=D_FF, tn=1024)
    g_down = _grad_w("grad_w_down", act_t, dx2_16, tk=1024, tn=d, ts=min(4096, s_len))
    g_up = _grad_w("grad_w_up", h2_t, da, tk=d, tn=1024, ts=min(4096, s_len), block_cols=D_FF // N_DEV)

    def dh2_epi(accs, ex, out):
        x1_ref, r_ref, g_ref, dx2_ref = ex
        r = r_ref[...]
        xh = x1_ref[...] * r
        dh2 = accs[0]
        out[0][...] = dx2_ref[...] + _rms_bwd(xh, r, g_ref[...], dh2)
        accumulate(out[1], jnp.sum(dh2 * xh, axis=0, keepdims=True))

    my_c = lax.axis_index("c")
    my_chip = 2 * lax.axis_index("x") + lax.axis_index("y")
    idx = jnp.stack([my_c, my_chip]).astype(jnp.int32)
    parts16, owns = {}, {}

    def split_cores(g8):
        return g8.reshape((4, 2) + g8.shape[1:])

    def row_tile(r):
        return 512 if r % 512 == 0 else r

    def pair_sums(names, grads4, from_sibling):
        for name, g4, recv in zip(names, grads4, from_sibling):
            parts16[name], owns[name] = _pair_sum("grad_pair_sum_" + name, g4, recv, idx, row_tile(g4.shape[2]))

    grads4_mlp = [split_cores(g_up), split_cores(g_down.reshape(N_DEV, D_FF // N_DEV, d))]
    (dx1, g_norm2), from_sibling = _mm(
        "mlp_up_bwd", [(da, w_up_t, False, None)],
        [(x1, _row(tr, d)), (r2, _row(tr, 1)), (norm2_g, _whole((1, d))), (dx2, _row(tr, d))],
        [(_sds((s_len, d), F32), _row(tr, d)), (_sds((1, d), F32), _whole((1, d)))],
        dh2_epi, m=s_len, tm=tr, n=d, tn=d, arbitrary=True, comm=[_pair_exchange_plan(grads4_mlp)])
    pair_sums(("w_up", "w_down"), grads4_mlp, from_sibling)

    def dmerge_epi(accs, ex, out):
        dm = accs[0]
        sa, sb = _sigmoid(ex[0][...].astype(F32)), _sigmoid(ex[1][...].astype(F32))
        out[0][...] = (dm * sa).astype(BF16)
        out[1][...] = (dm * sb).astype(BF16)
        dga = (dm * ex[2][...] * sa * (1.0 - sa)).astype(BF16)
        dgb = (dm * ex[3][...] * sb * (1.0 - sb)).astype(BF16)
        out[2][...] = dga
        out[3][...] = dgb
        out[4][:d, :] = jnp.transpose(dga)
        out[4][d:, :] = jnp.transpose(dgb)

    dya, dyb, dga, dgb, dg_t = _mm(
        "out_proj_bwd", [(dx1, w_o_f, True, None)],
        [(prest, _tile(tm, d, GA_OFF // d)), (prest, _tile(tm, d, GB_OFF // d)), (ya, _tile(tm, d)), (yb, _tile(tm, d))],
        [(_sds((s_len, d), BF16), _tile(tm, d))] * 4
        + [(_sds((2 * d, s_len), BF16), pl.BlockSpec((2 * d, tm), lambda i, j: (0, i)))],
        dmerge_epi, m=s_len, tm=tm, n=d, tn=d)
    g_o = _grad_w("grad_w_o", merged_t, dx1, tk=d, tn=d, ts=min(2048, s_len)).reshape(N_DEV, d // N_DEV, d)
    def col_blocks(g):
        return jnp.transpose(g.reshape(g.shape[0], N_DEV, g.shape[1] // N_DEV), (1, 0, 2))

    g_a = col_blocks(_grad_w("grad_w_a", o_t, dya, tk=FOX_W, tn=d, ts=min(4096, s_len)))
    g_b = col_blocks(_grad_w("grad_w_b", sg_t, dyb, tk=SGU_W, tn=d, ts=min(4096, s_len)))

    def do_epi(accs, ex, out):
        do = accs[0]
        out[0][...] = do
        out[1][...] = _dot_f32(do * ex[0][...], ex[1][...], "b")

    grads4_mix = [split_cores(g) for g in (g_a, g_b, g_o)]
    (do, delta), from_sibling = _mm(
        "attn_out_bwd", [(dya, w_a_t, False, None)], [(o, _row(tm, FOX_W)), (head_sel, _whole((FOX_W, LANES)))],
        [(_sds((s_len, FOX_W), F32), _row(tm, FOX_W)), (_sds((s_len, LANES), F32), _row(tm, LANES))],
        do_epi, m=s_len, tm=tm, n=FOX_W, tn=FOX_W, comm=[_pair_exchange_plan(grads4_mix)])
    pair_sums(("w_a", "w_b", "w_o"), grads4_mix, from_sibling)

    du, dsv, dw_pairs, db_pos, g_ln_g, g_ln_b, dusv_t = _sgu_bwd(
        prest, dyb, w_b_t, ln_v_g, ln_v_b, w_stack, wt_stack, b_pair, tm)
    g_w_sgu = jnp.where(sgu_mask[None], dw_pairs.reshape(SGU_G, SGU_LEN, SGU_LEN), 0.0)
    g_b_sgu = jnp.transpose(jnp.sum(db_pos.reshape(SGU_LEN, SGU_G, SGU_W // SGU_G), axis=-1))

    delta_row = jnp.transpose(delta[:, :N_HEADS]).reshape(N_HEADS, 1, s_len)
    early = ("w_a", "w_b", "w_o", "w_up", "w_down")
    small_early = _pack_rows((g_w_sgu, g_b_sgu, g_norm2, g_normf, g_ln_g, g_ln_b))
    (dq, dk, dv, dc_rows_blk, dc_cols, dkv_t), (small_early_all, *from_chips_early) = _attn_bwd(
        qkv, k_t, do, c_col, lse_row, delta_row, ta, ta,
        comm=[_gather_plan(small_early), _chip_exchange_plan([parts16[n] for n in early])])
    dc_rows = jnp.transpose(dc_rows_blk.reshape(s_len // ta, N_HEADS, ta), (0, 2, 1)).reshape(s_len, N_HEADS)
    dc_rows = jnp.pad(dc_rows, ((0, 0), (0, LANES - N_HEADS)))
    dfl, g_bf, dfl_t = _forget_bwd(dc_rows, dc_cols, f_logit, b_f_pad, tc)

    dp_t = (jnp.transpose(dq), dkv_t, dfl_t, dusv_t, dg_t)
    g_in_rows = [_grad_w("grad_w_in_%d" % k, seg_t, h, tk=min(seg_t.shape[0], 1024), tn=d, ts=min(4096, s_len))
                 for k, seg_t in enumerate(dp_t)]
    g_in_rows[2] = g_in_rows[2][:N_HEADS]
    g_in = jnp.concatenate(g_in_rows, axis=0).reshape((N_DEV,) + lin)

    def dx_epi(accs, ex, out):
        x_ref, r_ref, g_ref, dx1_ref = ex
        dh = accs[0]
        for extra in accs[1:]:
            dh = dh + extra
        r = r_ref[...]
        xh = x_ref[...] * r
        out[0][...] = dx1_ref[...] + _rms_bwd(xh, r, g_ref[...], dh)
        accumulate(out[1], jnp.sum(dh * xh, axis=0, keepdims=True))

    rest_cols = ((du, U_OFF, 512), (dsv, SV_OFF, 512), (dga, GA_OFF, 1024), (dgb, GB_OFF, 1024), (dfl, F_OFF, LANES))
    dx_pairs = [(seg, w_qkv, False, (512 * k, 512 * (k + 1))) for k, seg in enumerate((dq, dk, dv))]
    dx_pairs += [(seg, w_rest, False, (lo, lo + width)) for seg, lo, width in rest_cols]
    grads4_in = [split_cores(g_in)]
    pair_sums(("w_in",), grads4_in, _run_comm("grad_pair_exchange_w_in", [_pair_exchange_plan(grads4_in)]))
    (grad_x, g_norm1), (from_chips_in,) = _mm(
        "proj_bwd", dx_pairs,
        [(xs, _row(tr, d)), (r1, _row(tr, 1)), (norm1_g, _whole((1, d))), (dx1, _row(tr, d))],
        [(_sds((s_len, d), F32), _row(tr, d)), (_sds((1, d), F32), _whole((1, d)))],
        dx_epi, m=s_len, tm=tr, n=d, tn=d, arbitrary=True, comm=[_chip_exchange_plan([parts16["w_in"]])])
    small_late = _pack_rows((g_bf[:, :N_HEADS], g_norm1, loss_part[0, 0]))
    (small_late_all,) = _run_comm("gather_last_grads", [_gather_plan(small_late)])
    from_chips = dict(zip(early, from_chips_early), w_in=from_chips_in)

    names = ("w_in", "w_a", "w_b", "w_o", "w_up", "w_down")
    moments_m = (m_w_in, m_w_a, m_w_b, m_w_o, m_w_up, m_w_down)
    moments_v = (v_w_in, v_w_a, v_w_b, v_w_o, v_w_up, v_w_down)
    big_out = {}
    for name, w, m, v in zip(names, big, moments_m, moments_v):
        own = owns[name]
        transposed = name == "w_in"
        m0, v0 = (jnp.transpose(m[0]).reshape(lin), jnp.transpose(v[0]).reshape(lin)) if transposed else (m[0], v[0])
        res = _adamw_shard("adamw_" + name, own, from_chips[name], w, m0, v0, row_tile(own.shape[0]))
        big_out[name] = [(jnp.transpose(t.reshape(IN_SHARD, d)) if transposed else t)[None] for t in res]

    zero = jnp.zeros((), F32)
    sgu_rows = (SGU_G * SGU_LEN, LANES)
    res_sgu = _adamw_small("adamw_w_sgu", small_early_all, 0, w_sgu.reshape(sgu_rows), m_w_sgu.reshape(sgu_rows),
                           v_w_sgu.reshape(sgu_rows))
    small_out = {"w_sgu": [t.reshape(w_sgu.shape) for t in res_sgu]}
    res_early = _adamw_small(
        "adamw_tiny_early", small_early_all, sgu_rows[0], _pack_rows((b_sgu, norm2_g, normf_g, ln_v_g, ln_v_b)),
        _pack_rows((m_b_sgu, m_norm2_g, m_normf_g, m_ln_v_g, m_ln_v_b)),
        _pack_rows((v_b_sgu, v_norm2_g, v_normf_g, v_ln_v_g, v_ln_v_b)))
    res_late = _adamw_small(
        "adamw_tiny_late", small_late_all, 0, _pack_rows((b_f, norm1_g, zero)), _pack_rows((m_b_f, m_norm1_g, zero)),
        _pack_rows((v_b_f, v_norm1_g, zero)))
    for res, group in ((res_early, _TINY_EARLY), (res_late, _TINY_LATE)):
        unpacked = [_unpack_rows(t, group) for t in res]
        small_out.update({name: [u[name] for u in unpacked] for name, _ in group})
    loss = small_out["loss"][0]

    order = ("norm1_g", "w_in", "b_f", "ln_v_g", "ln_v_b", "w_sgu", "b_sgu", "w_a", "w_b", "w_o", "norm2_g", "w_up",
             "w_down", "normf_g")
    table = {**big_out, **small_out}
    outs = [loss, grad_x[None]]
    for kind in range(4):
        outs += [table[n][kind] for n in order]
    return tuple(outs)
```
